```python
import math
import jax, jax.numpy as jnp
from jax import lax
import numpy as np

D_MODEL = 1024
BATCH = 16
SEQ = 4096
DEPTH = 4

PLE_DIM = 256
HEAD_DIM = 64
N_Q_HEADS = 8
N_KV_HEADS = 2
GQA_GROUP = N_Q_HEADS // N_KV_HEADS
WINDOW = 128
BLOCK = 128
ROPE_THETA = 500000.0
ROPE_DIM = HEAD_DIM // 4
Q_WIDTH = N_Q_HEADS * HEAD_DIM
KV_WIDTH = N_KV_HEADS * HEAD_DIM
SSM_WIDTH = D_MODEL // 4
SSM_GROUP = 16
SSM_GROUPS = SSM_WIDTH // SSM_GROUP
SSM_STATE = 64
CONV_WIDTH = D_MODEL // 4
CONV_K = 31
FFN_HIDDEN = ((-(-8 * D_MODEL // 3) + 255) // 256) * 256
N_BRANCH = 3
IN_WIDTHS = [Q_WIDTH, KV_WIDTH, KV_WIDTH, SSM_WIDTH, 2 * CONV_WIDTH, N_BRANCH * D_MODEL]
IN_WIDTH = sum(IN_WIDTHS)
SPLIT_POINTS = [int(v) for v in np.cumsum(IN_WIDTHS)[:-1]]
EPS = 1e-6
NEG_INF = -1e30

kernel_name = "hybrid_gated_swa_s5_conformer_block"


def rms_norm(t, g):
    t32 = t.astype(jnp.float32)
    out = t32 * lax.rsqrt(jnp.mean(t32 * t32, axis=-1, keepdims=True) + EPS) * g.astype(jnp.float32)
    return out.astype(t.dtype)


def layer_norm(t, g, b):
    t32 = t.astype(jnp.float32)
    mu = jnp.mean(t32, axis=-1, keepdims=True)
    var = jnp.mean(jnp.square(t32 - mu), axis=-1, keepdims=True)
    out = (t32 - mu) * lax.rsqrt(var + EPS) * g.astype(jnp.float32) + b.astype(jnp.float32)
    return out.astype(t.dtype)


def partial_rope(t, cos, sin):
    half = ROPE_DIM // 2
    t1 = t[..., :half]
    t2 = t[..., half:ROPE_DIM]
    return jnp.concatenate([t1 * cos - t2 * sin, t2 * cos + t1 * sin, t[..., ROPE_DIM:]], axis=-1)


def sliding_window_attention(q, k, v, sinks):
    b, s = q.shape[0], q.shape[1]
    nb = s // BLOCK
    qb = q.reshape(b, nb, BLOCK, N_KV_HEADS, GQA_GROUP, HEAD_DIM)

    def band(t):
        tb = t.reshape(b, nb, BLOCK, N_KV_HEADS, HEAD_DIM)
        prev = jnp.pad(tb, ((0, 0), (1, 0), (0, 0), (0, 0), (0, 0)))[:, :-1]
        return jnp.concatenate([prev, tb], axis=2)

    kb, vb = band(k), band(v)
    scores = jnp.einsum('bnqkgd,bnjkd->bnkgqj', qb, kb) * (HEAD_DIM ** -0.5)
    qi = jnp.arange(BLOCK)[:, None]
    kj = jnp.arange(2 * BLOCK)[None, :]
    dist = qi + BLOCK - kj
    in_window = (dist >= 0) & (dist < WINDOW)
    has_prev = (jnp.arange(nb)[:, None, None] > 0) | (kj[None] >= BLOCK)
    mask = in_window[None] & has_prev
    scores = jnp.where(mask[None, :, None, None], scores, NEG_INF)
    sink = sinks.astype(jnp.float32).reshape(N_KV_HEADS, GQA_GROUP)[None, None, :, :, None, None]
    m = jnp.maximum(jnp.max(scores, axis=-1, keepdims=True), sink)
    pr = jnp.exp(scores - m)
    denom = jnp.sum(pr, axis=-1, keepdims=True) + jnp.exp(sink - m)
    out = jnp.einsum('bnkgqj,bnjkd->bnqkgd', pr / denom, vb)
    return out.reshape(b, s, Q_WIDTH)


def s5_ssm(u, lam_re, lam_im, log_dt, b_re, b_im, c_re, c_im, d_skip):
    b, s, _ = u.shape
    ug = u.reshape(b, s, SSM_GROUPS, SSM_GROUP)
    lr = jnp.minimum(lam_re.astype(jnp.float32), -1e-4)
    li = lam_im.astype(jnp.float32)
    dt = jnp.exp(log_dt.astype(jnp.float32))[:, None]
    mag = jnp.exp(lr * dt)
    a_re = mag * jnp.cos(li * dt)
    a_im = mag * jnp.sin(li * dt)
    den = lr * lr + li * li
    x_re, x_im = a_re - 1.0, a_im
    f_re = (x_re * lr + x_im * li) / den
    f_im = (x_im * lr - x_re * li) / den
    br = b_re.astype(jnp.float32)
    bi = b_im.astype(jnp.float32)
    bb_re = f_re[..., None] * br - f_im[..., None] * bi
    bb_im = f_re[..., None] * bi + f_im[..., None] * br
    bu_re = jnp.einsum('bsgh,gnh->bsgn', ug, bb_re)
    bu_im = jnp.einsum('bsgh,gnh->bsgn', ug, bb_im)
    a_re_t = jnp.broadcast_to(a_re, (1, s) + a_re.shape)
    a_im_t = jnp.broadcast_to(a_im, (1, s) + a_im.shape)

    def combine(e1, e2):
        a1r, a1i, b1r, b1i = e1
        a2r, a2i, b2r, b2i = e2
        return (a2r * a1r - a2i * a1i, a2r * a1i + a2i * a1r,
                a2r * b1r - a2i * b1i + b2r, a2r * b1i + a2i * b1r + b2i)

    _, _, st_re, st_im = lax.associative_scan(combine, (a_re_t, a_im_t, bu_re, bu_im), axis=1)
    y = (jnp.einsum('bsgn,ghn->bsgh', st_re, c_re.astype(jnp.float32))
         - jnp.einsum('bsgn,ghn->bsgh', st_im, c_im.astype(jnp.float32)))
    return y.reshape(b, s, SSM_WIDTH) + d_skip.astype(jnp.float32) * u


def conformer_conv(c_in, dw_w, dw_b, ln_g, ln_b, w_pw_out):
    a, g = jnp.split(c_in, 2, axis=-1)
    u = a * jax.nn.sigmoid(g)
    u = lax.conv_general_dilated(u, dw_w[:, None, :].astype(u.dtype), window_strides=(1,),
                                 padding=[(CONV_K - 1, 0)],
                                 dimension_numbers=('NWC', 'WIO', 'NWC'),
                                 feature_group_count=CONV_WIDTH) + dw_b
    u = jax.nn.silu(layer_norm(u, ln_g, ln_b))
    return u @ w_pw_out


def _fwd_setup_inputs(seed: int = 0) -> dict:
    key = jax.random.key(seed)
    ks = iter(jax.random.split(key, 40))
    f32 = jnp.float32

    def nrm(shape, scale):
        return jax.random.normal(next(ks), shape, f32) * scale

    L, D = DEPTH, D_MODEL
    x = nrm((BATCH, SEQ, D), 1.0)
    p = nrm((DEPTH, BATCH, SEQ, PLE_DIM), 1.0)
    positions = (jnp.arange(SEQ, dtype=jnp.int32)[None, :]
                 + jax.random.randint(next(ks), (BATCH, 1), 0, 1024, dtype=jnp.int32))
    n_idx = jnp.arange(SSM_STATE, dtype=f32)
    return {
        'x': x,
        'p': p,
        'positions': positions,
        'mix_norm_g': 1.0 + nrm((L, D), 0.02),
        'w_in': nrm((L, D, IN_WIDTH), D ** -0.5),
        'b_gate': nrm((L, N_BRANCH * D), 0.02),
        'attn_sinks': nrm((L, N_Q_HEADS), 0.5),
        'w_attn_out': nrm((L, Q_WIDTH, D), Q_WIDTH ** -0.5),
        'ssm_lambda_re': -0.5 + nrm((L, SSM_GROUPS, SSM_STATE), 0.01),
        'ssm_lambda_im': math.pi * n_idx + nrm((L, SSM_GROUPS, SSM_STATE), 0.01),
        'ssm_log_dt': jax.random.uniform(next(ks), (L, SSM_GROUPS), f32, math.log(1e-3), math.log(1e-1)),
        'ssm_b_re': nrm((L, SSM_GROUPS, SSM_STATE, SSM_GROUP), (2 * SSM_GROUP) ** -0.5),
        'ssm_b_im': nrm((L, SSM_GROUPS, SSM_STATE, SSM_GROUP), (2 * SSM_GROUP) ** -0.5),
        'ssm_c_re': nrm((L, SSM_GROUPS, SSM_GROUP, SSM_STATE), (2 * SSM_STATE) ** -0.5),
        'ssm_c_im': nrm((L, SSM_GROUPS, SSM_GROUP, SSM_STATE), (2 * SSM_STATE) ** -0.5),
        'ssm_d': nrm((L, SSM_WIDTH), 1.0),
        'w_ssm_glu': nrm((L, SSM_WIDTH, 2 * D), SSM_WIDTH ** -0.5),
        'b_ssm_glu': nrm((L, 2 * D), 0.02),
        'conv_dw_w': nrm((L, CONV_K, CONV_WIDTH), CONV_K ** -0.5),
        'conv_dw_b': nrm((L, CONV_WIDTH), 0.02),
        'conv_norm_g': 1.0 + nrm((L, CONV_WIDTH), 0.02),
        'conv_norm_b': nrm((L, CONV_WIDTH), 0.02),
        'w_conv_out': nrm((L, CONV_WIDTH, D), CONV_WIDTH ** -0.5),
        'w_mix_out': nrm((L, D, D), D ** -0.5),
        'ffn_norm_g': 1.0 + nrm((L, D), 0.02),
        'w_ffn_in': nrm((L, D, 2 * FFN_HIDDEN), D ** -0.5),
        'w_ffn_out': nrm((L, FFN_HIDDEN, D), FFN_HIDDEN ** -0.5),
        'w_ple_in': nrm((L, PLE_DIM, D), PLE_DIM ** -0.5),
        'ple_norm_g': 1.0 + nrm((L, D), 0.02),
        'w_ple_gate': nrm((L, D, D), D ** -0.5),
        'final_norm_g': 1.0 + nrm((D,), 0.02),
    }


def _fwd_reference(x, p, positions, mix_norm_g, w_in, b_gate, attn_sinks, w_attn_out,
              ssm_lambda_re, ssm_lambda_im, ssm_log_dt, ssm_b_re, ssm_b_im, ssm_c_re, ssm_c_im,
              ssm_d, w_ssm_glu, b_ssm_glu, conv_dw_w, conv_dw_b, conv_norm_g, conv_norm_b,
              w_conv_out, w_mix_out, ffn_norm_g, w_ffn_in, w_ffn_out, w_ple_in, ple_norm_g,
              w_ple_gate, final_norm_g):
    f32 = jnp.float32
    b, s, d = x.shape
    inv_freq = ROPE_THETA ** (-jnp.arange(0, ROPE_DIM, 2, dtype=f32) / ROPE_DIM)
    ang = positions.astype(f32)[..., None] * inv_freq
    cos = jnp.cos(ang)[:, :, None, :]
    sin = jnp.sin(ang)[:, :, None, :]

    for i in range(DEPTH):
        h = rms_norm(x, mix_norm_g[i])
        z = h @ w_in[i]
        q, k, v, s_in, c_in, g_in = jnp.split(z, SPLIT_POINTS, axis=-1)

        qh = partial_rope(q.astype(f32).reshape(b, s, N_Q_HEADS, HEAD_DIM), cos, sin)
        kh = partial_rope(k.astype(f32).reshape(b, s, N_KV_HEADS, HEAD_DIM), cos, sin)
        vh = v.astype(f32).reshape(b, s, N_KV_HEADS, HEAD_DIM)
        y_attn = sliding_window_attention(qh, kh, vh, attn_sinks[i]).astype(x.dtype) @ w_attn_out[i]

        y_s = s5_ssm(s_in.astype(f32), ssm_lambda_re[i], ssm_lambda_im[i], ssm_log_dt[i],
                     ssm_b_re[i], ssm_b_im[i], ssm_c_re[i], ssm_c_im[i], ssm_d[i])
        glu_a, glu_b = jnp.split(jax.nn.gelu(y_s).astype(x.dtype) @ w_ssm_glu[i] + b_ssm_glu[i], 2, axis=-1)
        y_ssm = glu_a * jax.nn.sigmoid(glu_b)

        y_conv = conformer_conv(c_in, conv_dw_w[i], conv_dw_b[i], conv_norm_g[i], conv_norm_b[i], w_conv_out[i])

        gates = jax.nn.sigmoid(g_in + b_gate[i]).reshape(b, s, N_BRANCH, d)
        merged = gates[:, :, 0] * y_attn + gates[:, :, 1] * y_ssm + gates[:, :, 2] * y_conv
        x = x + merged @ w_mix_out[i]

        hf = rms_norm(x, ffn_norm_g[i])
        f_gate, f_up = jnp.split(hf @ w_ffn_in[i], 2, axis=-1)
        x = x + (jax.nn.silu(f_gate) * f_up) @ w_ffn_out[i]

        e = p[i] @ w_ple_in[i]
        g_ple = jax.nn.sigmoid(rms_norm(x, ple_norm_g[i]) @ w_ple_gate[i])
        x = x + g_ple * e

    return rms_norm(x, final_norm_g)


import jax as _jax
import jax.numpy as _jnp

TWIN_FORMAT = 'train_step'
FWD_PARAMS = ['x', 'p', 'positions', 'mix_norm_g', 'w_in', 'b_gate', 'attn_sinks', 'w_attn_out', 'ssm_lambda_re', 'ssm_lambda_im', 'ssm_log_dt', 'ssm_b_re', 'ssm_b_im', 'ssm_c_re', 'ssm_c_im', 'ssm_d', 'w_ssm_glu', 'b_ssm_glu', 'conv_dw_w', 'conv_dw_b', 'conv_norm_g', 'conv_norm_b', 'w_conv_out', 'w_mix_out', 'ffn_norm_g', 'w_ffn_in', 'w_ffn_out', 'w_ple_in', 'ple_norm_g', 'w_ple_gate', 'final_norm_g']
TWIN_WEIGHTS = ['mix_norm_g', 'w_in', 'b_gate', 'attn_sinks', 'w_attn_out', 'ssm_lambda_re', 'ssm_lambda_im', 'ssm_log_dt', 'ssm_b_re', 'ssm_b_im', 'ssm_c_re', 'ssm_c_im', 'ssm_d', 'w_ssm_glu', 'b_ssm_glu', 'conv_dw_w', 'conv_dw_b', 'conv_norm_g', 'conv_norm_b', 'w_conv_out', 'w_mix_out', 'ffn_norm_g', 'w_ffn_in', 'w_ffn_out', 'w_ple_in', 'ple_norm_g', 'w_ple_gate', 'final_norm_g']
TWIN_DIFF_INPUT = 'x'
TWIN_INPUTS = ['x', 'p', 'positions', 'mix_norm_g', 'w_in', 'b_gate', 'attn_sinks', 'w_attn_out', 'ssm_lambda_re', 'ssm_lambda_im', 'ssm_log_dt', 'ssm_b_re', 'ssm_b_im', 'ssm_c_re', 'ssm_c_im', 'ssm_d', 'w_ssm_glu', 'b_ssm_glu', 'conv_dw_w', 'conv_dw_b', 'conv_norm_g', 'conv_norm_b', 'w_conv_out', 'w_mix_out', 'ffn_norm_g', 'w_ffn_in', 'w_ffn_out', 'w_ple_in', 'ple_norm_g', 'w_ple_gate', 'final_norm_g', 'loss_target', 'm_mix_norm_g', 'm_w_in', 'm_b_gate', 'm_attn_sinks', 'm_w_attn_out', 'm_ssm_lambda_re', 'm_ssm_lambda_im', 'm_ssm_log_dt', 'm_ssm_b_re', 'm_ssm_b_im', 'm_ssm_c_re', 'm_ssm_c_im', 'm_ssm_d', 'm_w_ssm_glu', 'm_b_ssm_glu', 'm_conv_dw_w', 'm_conv_dw_b', 'm_conv_norm_g', 'm_conv_norm_b', 'm_w_conv_out', 'm_w_mix_out', 'm_ffn_norm_g', 'm_w_ffn_in', 'm_w_ffn_out', 'm_w_ple_in', 'm_ple_norm_g', 'm_w_ple_gate', 'm_final_norm_g', 'v_mix_norm_g', 'v_w_in', 'v_b_gate', 'v_attn_sinks', 'v_w_attn_out', 'v_ssm_lambda_re', 'v_ssm_lambda_im', 'v_ssm_log_dt', 'v_ssm_b_re', 'v_ssm_b_im', 'v_ssm_c_re', 'v_ssm_c_im', 'v_ssm_d', 'v_w_ssm_glu', 'v_b_ssm_glu', 'v_conv_dw_w', 'v_conv_dw_b', 'v_conv_norm_g', 'v_conv_norm_b', 'v_w_conv_out', 'v_w_mix_out', 'v_ffn_norm_g', 'v_w_ffn_in', 'v_w_ffn_out', 'v_w_ple_in', 'v_ple_norm_g', 'v_w_ple_gate', 'v_final_norm_g']
TWIN_OUTPUTS = ['loss', 'grad_x', 'grad_mix_norm_g', 'grad_w_in', 'grad_b_gate', 'grad_attn_sinks', 'grad_w_attn_out', 'grad_ssm_lambda_re', 'grad_ssm_lambda_im', 'grad_ssm_log_dt', 'grad_ssm_b_re', 'grad_ssm_b_im', 'grad_ssm_c_re', 'grad_ssm_c_im', 'grad_ssm_d', 'grad_w_ssm_glu', 'grad_b_ssm_glu', 'grad_conv_dw_w', 'grad_conv_dw_b', 'grad_conv_norm_g', 'grad_conv_norm_b', 'grad_w_conv_out', 'grad_w_mix_out', 'grad_ffn_norm_g', 'grad_w_ffn_in', 'grad_w_ffn_out', 'grad_w_ple_in', 'grad_ple_norm_g', 'grad_w_ple_gate', 'grad_final_norm_g', 'delta_mix_norm_g', 'delta_w_in', 'delta_b_gate', 'delta_attn_sinks', 'delta_w_attn_out', 'delta_ssm_lambda_re', 'delta_ssm_lambda_im', 'delta_ssm_log_dt', 'delta_ssm_b_re', 'delta_ssm_b_im', 'delta_ssm_c_re', 'delta_ssm_c_im', 'delta_ssm_d', 'delta_w_ssm_glu', 'delta_b_ssm_glu', 'delta_conv_dw_w', 'delta_conv_dw_b', 'delta_conv_norm_g', 'delta_conv_norm_b', 'delta_w_conv_out', 'delta_w_mix_out', 'delta_ffn_norm_g', 'delta_w_ffn_in', 'delta_w_ffn_out', 'delta_w_ple_in', 'delta_ple_norm_g', 'delta_w_ple_gate', 'delta_final_norm_g', 'new_m_mix_norm_g', 'new_m_w_in', 'new_m_b_gate', 'new_m_attn_sinks', 'new_m_w_attn_out', 'new_m_ssm_lambda_re', 'new_m_ssm_lambda_im', 'new_m_ssm_log_dt', 'new_m_ssm_b_re', 'new_m_ssm_b_im', 'new_m_ssm_c_re', 'new_m_ssm_c_im', 'new_m_ssm_d', 'new_m_w_ssm_glu', 'new_m_b_ssm_glu', 'new_m_conv_dw_w', 'new_m_conv_dw_b', 'new_m_conv_norm_g', 'new_m_conv_norm_b', 'new_m_w_conv_out', 'new_m_w_mix_out', 'new_m_ffn_norm_g', 'new_m_w_ffn_in', 'new_m_w_ffn_out', 'new_m_w_ple_in', 'new_m_ple_norm_g', 'new_m_w_ple_gate', 'new_m_final_norm_g', 'new_v_mix_norm_g', 'new_v_w_in', 'new_v_b_gate', 'new_v_attn_sinks', 'new_v_w_attn_out', 'new_v_ssm_lambda_re', 'new_v_ssm_lambda_im', 'new_v_ssm_log_dt', 'new_v_ssm_b_re', 'new_v_ssm_b_im', 'new_v_ssm_c_re', 'new_v_ssm_c_im', 'new_v_ssm_d', 'new_v_w_ssm_glu', 'new_v_b_ssm_glu', 'new_v_conv_dw_w', 'new_v_conv_dw_b', 'new_v_conv_norm_g', 'new_v_conv_norm_b', 'new_v_w_conv_out', 'new_v_w_mix_out', 'new_v_ffn_norm_g', 'new_v_w_ffn_in', 'new_v_w_ffn_out', 'new_v_w_ple_in', 'new_v_ple_norm_g', 'new_v_w_ple_gate', 'new_v_final_norm_g']
TWIN_LEAF_KINDS = {'loss': 'loss', 'grad_x': 'grad_x', 'grad_mix_norm_g': 'grad_w', 'grad_w_in': 'grad_w', 'grad_b_gate': 'grad_w', 'grad_attn_sinks': 'grad_w', 'grad_w_attn_out': 'grad_w', 'grad_ssm_lambda_re': 'grad_w', 'grad_ssm_lambda_im': 'grad_w', 'grad_ssm_log_dt': 'grad_w', 'grad_ssm_b_re': 'grad_w', 'grad_ssm_b_im': 'grad_w', 'grad_ssm_c_re': 'grad_w', 'grad_ssm_c_im': 'grad_w', 'grad_ssm_d': 'grad_w', 'grad_w_ssm_glu': 'grad_w', 'grad_b_ssm_glu': 'grad_w', 'grad_conv_dw_w': 'grad_w', 'grad_conv_dw_b': 'grad_w', 'grad_conv_norm_g': 'grad_w', 'grad_conv_norm_b': 'grad_w', 'grad_w_conv_out': 'grad_w', 'grad_w_mix_out': 'grad_w', 'grad_ffn_norm_g': 'grad_w', 'grad_w_ffn_in': 'grad_w', 'grad_w_ffn_out': 'grad_w', 'grad_w_ple_in': 'grad_w', 'grad_ple_norm_g': 'grad_w', 'grad_w_ple_gate': 'grad_w', 'grad_final_norm_g': 'grad_w', 'delta_mix_norm_g': 'delta_w', 'delta_w_in': 'delta_w', 'delta_b_gate': 'delta_w', 'delta_attn_sinks': 'delta_w', 'delta_w_attn_out': 'delta_w', 'delta_ssm_lambda_re': 'delta_w', 'delta_ssm_lambda_im': 'delta_w', 'delta_ssm_log_dt': 'delta_w', 'delta_ssm_b_re': 'delta_w', 'delta_ssm_b_im': 'delta_w', 'delta_ssm_c_re': 'delta_w', 'delta_ssm_c_im': 'delta_w', 'delta_ssm_d': 'delta_w', 'delta_w_ssm_glu': 'delta_w', 'delta_b_ssm_glu': 'delta_w', 'delta_conv_dw_w': 'delta_w', 'delta_conv_dw_b': 'delta_w', 'delta_conv_norm_g': 'delta_w', 'delta_conv_norm_b': 'delta_w', 'delta_w_conv_out': 'delta_w', 'delta_w_mix_out': 'delta_w', 'delta_ffn_norm_g': 'delta_w', 'delta_w_ffn_in': 'delta_w', 'delta_w_ffn_out': 'delta_w', 'delta_w_ple_in': 'delta_w', 'delta_ple_norm_g': 'delta_w', 'delta_w_ple_gate': 'delta_w', 'delta_final_norm_g': 'delta_w', 'new_m_mix_norm_g': 'new_m', 'new_m_w_in': 'new_m', 'new_m_b_gate': 'new_m', 'new_m_attn_sinks': 'new_m', 'new_m_w_attn_out': 'new_m', 'new_m_ssm_lambda_re': 'new_m', 'new_m_ssm_lambda_im': 'new_m', 'new_m_ssm_log_dt': 'new_m', 'new_m_ssm_b_re': 'new_m', 'new_m_ssm_b_im': 'new_m', 'new_m_ssm_c_re': 'new_m', 'new_m_ssm_c_im': 'new_m', 'new_m_ssm_d': 'new_m', 'new_m_w_ssm_glu': 'new_m', 'new_m_b_ssm_glu': 'new_m', 'new_m_conv_dw_w': 'new_m', 'new_m_conv_dw_b': 'new_m', 'new_m_conv_norm_g': 'new_m', 'new_m_conv_norm_b': 'new_m', 'new_m_w_conv_out': 'new_m', 'new_m_w_mix_out': 'new_m', 'new_m_ffn_norm_g': 'new_m', 'new_m_w_ffn_in': 'new_m', 'new_m_w_ffn_out': 'new_m', 'new_m_w_ple_in': 'new_m', 'new_m_ple_norm_g': 'new_m', 'new_m_w_ple_gate': 'new_m', 'new_m_final_norm_g': 'new_m', 'new_v_mix_norm_g': 'new_v', 'new_v_w_in': 'new_v', 'new_v_b_gate': 'new_v', 'new_v_attn_sinks': 'new_v', 'new_v_w_attn_out': 'new_v', 'new_v_ssm_lambda_re': 'new_v', 'new_v_ssm_lambda_im': 'new_v', 'new_v_ssm_log_dt': 'new_v', 'new_v_ssm_b_re': 'new_v', 'new_v_ssm_b_im': 'new_v', 'new_v_ssm_c_re': 'new_v', 'new_v_ssm_c_im': 'new_v', 'new_v_ssm_d': 'new_v', 'new_v_w_ssm_glu': 'new_v', 'new_v_b_ssm_glu': 'new_v', 'new_v_conv_dw_w': 'new_v', 'new_v_conv_dw_b': 'new_v', 'new_v_conv_norm_g': 'new_v', 'new_v_conv_norm_b': 'new_v', 'new_v_w_conv_out': 'new_v', 'new_v_w_mix_out': 'new_v', 'new_v_ffn_norm_g': 'new_v', 'new_v_w_ffn_in': 'new_v', 'new_v_w_ffn_out': 'new_v', 'new_v_w_ple_in': 'new_v', 'new_v_ple_norm_g': 'new_v', 'new_v_w_ple_gate': 'new_v', 'new_v_final_norm_g': 'new_v'}


def _forward(args):
    return _fwd_reference(*[args[k] for k in FWD_PARAMS])


def _output_shape():
    out = _jax.eval_shape(lambda: _forward(_fwd_setup_inputs(0)))
    return out.shape, out.dtype

N_MICROBATCH = 1
ADAM_LR = 0.001
ADAM_B1 = 0.9
ADAM_B2 = 0.999
ADAM_EPS = 1e-08
ADAM_WD = 0.01
ADAM_STEP = 10
PER_EXAMPLE_BATCH_AXIS = {'x': 0, 'p': 1, 'positions': 0, 'loss_target': 0}
SHARED_INPUTS = []
_WEIGHT_DTYPES = {'mix_norm_g': _jnp.float32, 'w_in': _jnp.float32, 'b_gate': _jnp.float32, 'attn_sinks': _jnp.float32, 'w_attn_out': _jnp.float32, 'ssm_lambda_re': _jnp.float32, 'ssm_lambda_im': _jnp.float32, 'ssm_log_dt': _jnp.float32, 'ssm_b_re': _jnp.float32, 'ssm_b_im': _jnp.float32, 'ssm_c_re': _jnp.float32, 'ssm_c_im': _jnp.float32, 'ssm_d': _jnp.float32, 'w_ssm_glu': _jnp.float32, 'b_ssm_glu': _jnp.float32, 'conv_dw_w': _jnp.float32, 'conv_dw_b': _jnp.float32, 'conv_norm_g': _jnp.float32, 'conv_norm_b': _jnp.float32, 'w_conv_out': _jnp.float32, 'w_mix_out': _jnp.float32, 'ffn_norm_g': _jnp.float32, 'w_ffn_in': _jnp.float32, 'w_ffn_out': _jnp.float32, 'w_ple_in': _jnp.float32, 'ple_norm_g': _jnp.float32, 'w_ple_gate': _jnp.float32, 'final_norm_g': _jnp.float32}
MOMENT_SCALE = {'mix_norm_g': 9.491069e-02, 'w_in': 4.477470e-02, 'b_gate': 1.786269e-02, 'attn_sinks': 2.280321e-02, 'w_attn_out': 2.440131e-02, 'ssm_lambda_re': 5.297718e-03, 'ssm_lambda_im': 5.968549e-03, 'ssm_log_dt': 2.665299e+00, 'ssm_b_re': 2.709006e-03, 'ssm_b_im': 2.590986e-03, 'ssm_c_re': 5.353552e-03, 'ssm_c_im': 5.433514e-03, 'ssm_d': 8.261680e-02, 'w_ssm_glu': 2.866039e-02, 'b_ssm_glu': 4.295012e-02, 'conv_dw_w': 1.348645e-01, 'conv_dw_b': 3.094826e-01, 'conv_norm_g': 1.715667e-01, 'conv_norm_b': 1.449920e-01, 'w_conv_out': 6.466797e-02, 'w_mix_out': 7.856502e-02, 'ffn_norm_g': 1.445819e-01, 'w_ffn_in': 5.971357e-02, 'w_ffn_out': 9.747751e-02, 'w_ple_in': 8.730355e-02, 'ple_norm_g': 3.439145e-02, 'w_ple_gate': 3.422838e-02, 'final_norm_g': 6.396457e+01}


def _to_microbatches(a, axis):
    t = _jnp.moveaxis(a, axis, 0)
    t = t.reshape((N_MICROBATCH, t.shape[0] // N_MICROBATCH) + t.shape[1:])
    return _jnp.moveaxis(t, 1, axis + 1)


def setup_inputs(seed: int = 0) -> dict:
    inp = _fwd_setup_inputs(seed)
    key = _jax.random.fold_in(_jax.random.key(seed), 7919)
    shape, _ = _output_shape()
    out = dict(inp)
    out["loss_target"] = _jax.random.normal(_jax.random.fold_in(key, 0), shape, _jnp.float32)
    for i, name in enumerate(TWIN_WEIGHTS):
        w = inp[name].astype(_jnp.float32)
        if MOMENT_SCALE is None:
            s = _jnp.sqrt(_jnp.mean(_jnp.square(w)) + 1e-30)
        else:
            s = MOMENT_SCALE[name]
        km, kv = _jax.random.split(_jax.random.fold_in(key, i + 1))
        out[name] = w
        out["m_" + name] = s * _jax.random.normal(km, w.shape, _jnp.float32)
        out["v_" + name] = (s * s) * _jax.random.uniform(kv, w.shape, _jnp.float32, 0.5, 1.5)
    if N_MICROBATCH > 1:
        for name, axis in PER_EXAMPLE_BATCH_AXIS.items():
            out[name] = _to_microbatches(out[name], axis)
    return {'x': out['x'], 'p': out['p'], 'positions': out['positions'], 'mix_norm_g': out['mix_norm_g'], 'w_in': out['w_in'], 'b_gate': out['b_gate'], 'attn_sinks': out['attn_sinks'], 'w_attn_out': out['w_attn_out'], 'ssm_lambda_re': out['ssm_lambda_re'], 'ssm_lambda_im': out['ssm_lambda_im'], 'ssm_log_dt': out['ssm_log_dt'], 'ssm_b_re': out['ssm_b_re'], 'ssm_b_im': out['ssm_b_im'], 'ssm_c_re': out['ssm_c_re'], 'ssm_c_im': out['ssm_c_im'], 'ssm_d': out['ssm_d'], 'w_ssm_glu': out['w_ssm_glu'], 'b_ssm_glu': out['b_ssm_glu'], 'conv_dw_w': out['conv_dw_w'], 'conv_dw_b': out['conv_dw_b'], 'conv_norm_g': out['conv_norm_g'], 'conv_norm_b': out['conv_norm_b'], 'w_conv_out': out['w_conv_out'], 'w_mix_out': out['w_mix_out'], 'ffn_norm_g': out['ffn_norm_g'], 'w_ffn_in': out['w_ffn_in'], 'w_ffn_out': out['w_ffn_out'], 'w_ple_in': out['w_ple_in'], 'ple_norm_g': out['ple_norm_g'], 'w_ple_gate': out['w_ple_gate'], 'final_norm_g': out['final_norm_g'], 'loss_target': out['loss_target'], 'm_mix_norm_g': out['m_mix_norm_g'], 'm_w_in': out['m_w_in'], 'm_b_gate': out['m_b_gate'], 'm_attn_sinks': out['m_attn_sinks'], 'm_w_attn_out': out['m_w_attn_out'], 'm_ssm_lambda_re': out['m_ssm_lambda_re'], 'm_ssm_lambda_im': out['m_ssm_lambda_im'], 'm_ssm_log_dt': out['m_ssm_log_dt'], 'm_ssm_b_re': out['m_ssm_b_re'], 'm_ssm_b_im': out['m_ssm_b_im'], 'm_ssm_c_re': out['m_ssm_c_re'], 'm_ssm_c_im': out['m_ssm_c_im'], 'm_ssm_d': out['m_ssm_d'], 'm_w_ssm_glu': out['m_w_ssm_glu'], 'm_b_ssm_glu': out['m_b_ssm_glu'], 'm_conv_dw_w': out['m_conv_dw_w'], 'm_conv_dw_b': out['m_conv_dw_b'], 'm_conv_norm_g': out['m_conv_norm_g'], 'm_conv_norm_b': out['m_conv_norm_b'], 'm_w_conv_out': out['m_w_conv_out'], 'm_w_mix_out': out['m_w_mix_out'], 'm_ffn_norm_g': out['m_ffn_norm_g'], 'm_w_ffn_in': out['m_w_ffn_in'], 'm_w_ffn_out': out['m_w_ffn_out'], 'm_w_ple_in': out['m_w_ple_in'], 'm_ple_norm_g': out['m_ple_norm_g'], 'm_w_ple_gate': out['m_w_ple_gate'], 'm_final_norm_g': out['m_final_norm_g'], 'v_mix_norm_g': out['v_mix_norm_g'], 'v_w_in': out['v_w_in'], 'v_b_gate': out['v_b_gate'], 'v_attn_sinks': out['v_attn_sinks'], 'v_w_attn_out': out['v_w_attn_out'], 'v_ssm_lambda_re': out['v_ssm_lambda_re'], 'v_ssm_lambda_im': out['v_ssm_lambda_im'], 'v_ssm_log_dt': out['v_ssm_log_dt'], 'v_ssm_b_re': out['v_ssm_b_re'], 'v_ssm_b_im': out['v_ssm_b_im'], 'v_ssm_c_re': out['v_ssm_c_re'], 'v_ssm_c_im': out['v_ssm_c_im'], 'v_ssm_d': out['v_ssm_d'], 'v_w_ssm_glu': out['v_w_ssm_glu'], 'v_b_ssm_glu': out['v_b_ssm_glu'], 'v_conv_dw_w': out['v_conv_dw_w'], 'v_conv_dw_b': out['v_conv_dw_b'], 'v_conv_norm_g': out['v_conv_norm_g'], 'v_conv_norm_b': out['v_conv_norm_b'], 'v_w_conv_out': out['v_w_conv_out'], 'v_w_mix_out': out['v_w_mix_out'], 'v_ffn_norm_g': out['v_ffn_norm_g'], 'v_w_ffn_in': out['v_w_ffn_in'], 'v_w_ffn_out': out['v_w_ffn_out'], 'v_w_ple_in': out['v_w_ple_in'], 'v_ple_norm_g': out['v_ple_norm_g'], 'v_w_ple_gate': out['v_w_ple_gate'], 'v_final_norm_g': out['v_final_norm_g']}


def _loss(weights, diff, rest, loss_target):
    with _jax.named_scope("forward"):
        args = {**rest, TWIN_DIFF_INPUT: diff, **{k: w.astype(_WEIGHT_DTYPES[k]) for k, w in weights.items()}}
        y = _forward(args)
    with _jax.named_scope("loss_head"):
        err = _jnp.square(y.astype(_jnp.float32) - loss_target)
        return 0.5 * _jnp.sum(_jnp.mean(err, axis=-1)) if err.ndim else 0.5 * err


def _adamw(w, g, m, v):
    m = ADAM_B1 * m + (1.0 - ADAM_B1) * g
    v = ADAM_B2 * v + (1.0 - ADAM_B2) * _jnp.square(g)
    m_hat = m / (1.0 - ADAM_B1 ** ADAM_STEP)
    v_hat = v / (1.0 - ADAM_B2 ** ADAM_STEP)
    delta = -ADAM_LR * (m_hat / (_jnp.sqrt(v_hat) + ADAM_EPS) + ADAM_WD * w)
    return delta, m, v


def reference(x, p, positions, mix_norm_g, w_in, b_gate, attn_sinks, w_attn_out, ssm_lambda_re, ssm_lambda_im, ssm_log_dt, ssm_b_re, ssm_b_im, ssm_c_re, ssm_c_im, ssm_d, w_ssm_glu, b_ssm_glu, conv_dw_w, conv_dw_b, conv_norm_g, conv_norm_b, w_conv_out, w_mix_out, ffn_norm_g, w_ffn_in, w_ffn_out, w_ple_in, ple_norm_g, w_ple_gate, final_norm_g, loss_target, m_mix_norm_g, m_w_in, m_b_gate, m_attn_sinks, m_w_attn_out, m_ssm_lambda_re, m_ssm_lambda_im, m_ssm_log_dt, m_ssm_b_re, m_ssm_b_im, m_ssm_c_re, m_ssm_c_im, m_ssm_d, m_w_ssm_glu, m_b_ssm_glu, m_conv_dw_w, m_conv_dw_b, m_conv_norm_g, m_conv_norm_b, m_w_conv_out, m_w_mix_out, m_ffn_norm_g, m_w_ffn_in, m_w_ffn_out, m_w_ple_in, m_ple_norm_g, m_w_ple_gate, m_final_norm_g, v_mix_norm_g, v_w_in, v_b_gate, v_attn_sinks, v_w_attn_out, v_ssm_lambda_re, v_ssm_lambda_im, v_ssm_log_dt, v_ssm_b_re, v_ssm_b_im, v_ssm_c_re, v_ssm_c_im, v_ssm_d, v_w_ssm_glu, v_b_ssm_glu, v_conv_dw_w, v_conv_dw_b, v_conv_norm_g, v_conv_norm_b, v_w_conv_out, v_w_mix_out, v_ffn_norm_g, v_w_ffn_in, v_w_ffn_out, v_w_ple_in, v_ple_norm_g, v_w_ple_gate, v_final_norm_g):
    given = dict(x=x, p=p, positions=positions, mix_norm_g=mix_norm_g, w_in=w_in, b_gate=b_gate, attn_sinks=attn_sinks, w_attn_out=w_attn_out, ssm_lambda_re=ssm_lambda_re, ssm_lambda_im=ssm_lambda_im, ssm_log_dt=ssm_log_dt, ssm_b_re=ssm_b_re, ssm_b_im=ssm_b_im, ssm_c_re=ssm_c_re, ssm_c_im=ssm_c_im, ssm_d=ssm_d, w_ssm_glu=w_ssm_glu, b_ssm_glu=b_ssm_glu, conv_dw_w=conv_dw_w, conv_dw_b=conv_dw_b, conv_norm_g=conv_norm_g, conv_norm_b=conv_norm_b, w_conv_out=w_conv_out, w_mix_out=w_mix_out, ffn_norm_g=ffn_norm_g, w_ffn_in=w_ffn_in, w_ffn_out=w_ffn_out, w_ple_in=w_ple_in, ple_norm_g=ple_norm_g, w_ple_gate=w_ple_gate, final_norm_g=final_norm_g, loss_target=loss_target, m_mix_norm_g=m_mix_norm_g, m_w_in=m_w_in, m_b_gate=m_b_gate, m_attn_sinks=m_attn_sinks, m_w_attn_out=m_w_attn_out, m_ssm_lambda_re=m_ssm_lambda_re, m_ssm_lambda_im=m_ssm_lambda_im, m_ssm_log_dt=m_ssm_log_dt, m_ssm_b_re=m_ssm_b_re, m_ssm_b_im=m_ssm_b_im, m_ssm_c_re=m_ssm_c_re, m_ssm_c_im=m_ssm_c_im, m_ssm_d=m_ssm_d, m_w_ssm_glu=m_w_ssm_glu, m_b_ssm_glu=m_b_ssm_glu, m_conv_dw_w=m_conv_dw_w, m_conv_dw_b=m_conv_dw_b, m_conv_norm_g=m_conv_norm_g, m_conv_norm_b=m_conv_norm_b, m_w_conv_out=m_w_conv_out, m_w_mix_out=m_w_mix_out, m_ffn_norm_g=m_ffn_norm_g, m_w_ffn_in=m_w_ffn_in, m_w_ffn_out=m_w_ffn_out, m_w_ple_in=m_w_ple_in, m_ple_norm_g=m_ple_norm_g, m_w_ple_gate=m_w_ple_gate, m_final_norm_g=m_final_norm_g, v_mix_norm_g=v_mix_norm_g, v_w_in=v_w_in, v_b_gate=v_b_gate, v_attn_sinks=v_attn_sinks, v_w_attn_out=v_w_attn_out, v_ssm_lambda_re=v_ssm_lambda_re, v_ssm_lambda_im=v_ssm_lambda_im, v_ssm_log_dt=v_ssm_log_dt, v_ssm_b_re=v_ssm_b_re, v_ssm_b_im=v_ssm_b_im, v_ssm_c_re=v_ssm_c_re, v_ssm_c_im=v_ssm_c_im, v_ssm_d=v_ssm_d, v_w_ssm_glu=v_w_ssm_glu, v_b_ssm_glu=v_b_ssm_glu, v_conv_dw_w=v_conv_dw_w, v_conv_dw_b=v_conv_dw_b, v_conv_norm_g=v_conv_norm_g, v_conv_norm_b=v_conv_norm_b, v_w_conv_out=v_w_conv_out, v_w_mix_out=v_w_mix_out, v_ffn_norm_g=v_ffn_norm_g, v_w_ffn_in=v_w_ffn_in, v_w_ffn_out=v_w_ffn_out, v_w_ple_in=v_w_ple_in, v_ple_norm_g=v_ple_norm_g, v_w_ple_gate=v_w_ple_gate, v_final_norm_g=v_final_norm_g)
    weights = {n: given[n] for n in TWIN_WEIGHTS}
    shared = {n: given[n] for n in SHARED_INPUTS}
    per_example = {n: given[n] for n in ['x', 'p', 'positions']}
    grad_fn = _jax.value_and_grad(_loss, argnums=(0, 1))

    def one_microbatch(ex, loss_target):
        ex = dict(ex)
        diff = ex.pop(TWIN_DIFF_INPUT)
        return grad_fn(weights, diff, {**shared, **ex}, loss_target)

    if N_MICROBATCH == 1:
        loss, (grad_w, grad_x) = one_microbatch(per_example, given["loss_target"])
    else:
        def body(carry, xs):
            loss_sum, grad_sum = carry
            l_k, (gw_k, gx_k) = one_microbatch(xs[0], xs[1])
            with _jax.named_scope("update"):
                return (loss_sum + l_k, _jax.tree.map(_jnp.add, grad_sum, gw_k)), gx_k

        init = (_jnp.zeros((), _jnp.float32), _jax.tree.map(_jnp.zeros_like, weights))
        (loss, grad_w), grad_x = _jax.lax.scan(body, init, (per_example, given["loss_target"]))
    with _jax.named_scope("update"):
        delta_w, new_m, new_v = {}, {}, {}
        for n in TWIN_WEIGHTS:
            delta_w[n], new_m[n], new_v[n] = _adamw(weights[n], grad_w[n], given["m_" + n], given["v_" + n])
    return (loss, grad_x, *[grad_w[n] for n in TWIN_WEIGHTS], *[delta_w[n] for n in TWIN_WEIGHTS],
            *[new_m[n] for n in TWIN_WEIGHTS], *[new_v[n] for n in TWIN_WEIGHTS])
```

```python
import functools
import math

import jax
import jax.numpy as jnp
from jax import lax
from jax.experimental import pallas as pl
from jax.experimental.pallas import tpu as pltpu

F32 = jnp.float32
BF16 = jnp.bfloat16

D_MODEL = 1024
HEAD_DIM = 64
N_Q_HEADS = 8
N_KV_HEADS = 2
GQA_GROUP = N_Q_HEADS // N_KV_HEADS
ATT_BLOCK = 128
ROPE_THETA = 500000.0
ROPE_DIM = HEAD_DIM // 4
ROPE_HALF = ROPE_DIM // 2
Q_WIDTH = N_Q_HEADS * HEAD_DIM
KV_WIDTH = N_KV_HEADS * HEAD_DIM
SSM_WIDTH = 256
SSM_GROUP = 16
SSM_GROUPS = 16
SSM_STATE = 64
SSM_LANES = SSM_GROUPS * SSM_STATE
CONV_WIDTH = 256
CONV_K = 31
CONV_HALO = 32
FFN_HIDDEN = 2816
EPS = 1e-6
NEG_INF = -1e30
SCALE = HEAD_DIM ** -0.5

ADAM_LR = 0.001
ADAM_B1 = 0.9
ADAM_B2 = 0.999
ADAM_EPS = 1e-08
ADAM_WD = 0.01
ADAM_STEP = 10

N_CHIPS = 4
N_DEV = 8
SUBLANES = 8
VMEM_LIMIT = 56 * 1024 * 1024

MESH = pl.DeviceIdType.MESH


def _params(sem=None):
    return pltpu.CompilerParams(dimension_semantics=sem, vmem_limit_bytes=VMEM_LIMIT)


def R(arr, width=None, cb=0, rb=0):
    return ("r", arr, arr.shape[1] if width is None else width, (cb, rb))


def V(arr, width=None, cb=0):
    return ("v", arr, arr.shape[1] if width is None else width, cb)


def _cbf(cb):
    return cb if callable(cb) else (lambda j, c=cb: c + j)


def _rowwise(name, fn, ins, outs, accs=(), *, tm, ncol=1, rows=None):
    t = rows if rows is not None else [a for k, a, _, _ in ins if k == "r"][0].shape[0]
    tm = min(tm, t)
    assert t % tm == 0, (name, t, tm)
    n_i, n_o, n_a = len(ins), len(outs), len(accs)

    def body(*refs):
        vals = fn(*[r[...] for r in refs[:n_i]])
        if not isinstance(vals, (tuple, list)):
            vals = (vals,)
        for ref, val in zip(refs[n_i:n_i + n_o], vals[:n_o]):
            ref[...] = val.astype(ref.dtype)
        if n_a:
            acc_refs = refs[n_i + n_o:]

            @pl.when(pl.program_id(1) == 0)
            def _():
                for ref in acc_refs:
                    ref[...] = jnp.zeros_like(ref)

            for ref, val in zip(acc_refs, vals[n_o:]):
                ref[...] += val

    in_specs = []
    for kind, arr, width, cb in ins:
        if kind == "r":
            f = _cbf(cb[0])
            in_specs.append(pl.BlockSpec((tm, width), functools.partial(lambda j, i, f, rb: (i + rb, f(j)), f=f, rb=cb[1])))
        else:
            f = _cbf(cb)
            in_specs.append(pl.BlockSpec((arr.shape[0], width), functools.partial(lambda j, i, f: (0, f(j)), f=f)))
    out_specs, out_shape = [], []
    for total, width, cb, dt in outs:
        f = _cbf(cb)
        out_specs.append(pl.BlockSpec((tm, width), functools.partial(lambda j, i, f: (i, f(j)), f=f)))
        out_shape.append(jax.ShapeDtypeStruct((t, total), dt))
    for total, width, cb in accs:
        f = _cbf(cb)
        out_specs.append(pl.BlockSpec((1, width), functools.partial(lambda j, i, f: (0, f(j)), f=f)))
        out_shape.append(jax.ShapeDtypeStruct((1, total), F32))
    sem = ("arbitrary", "arbitrary") if n_a else ("parallel", "parallel")
    res = pl.pallas_call(body, out_shape=out_shape, grid=(ncol, t // tm), in_specs=in_specs, out_specs=out_specs,
                         name=name, compiler_params=_params(sem))(*[a for _, a, _, _ in ins])
    return res[0] if len(res) == 1 else res


def O(width, dtype, total=None, cb=0):
    return (width if total is None else total, width, cb, dtype)


def A(width, total=None, cb=0):
    return (width if total is None else total, width, cb)


_DIMS = {"nn": (((1,), (0,)), ((), ())), "nt": (((1,), (1,)), ((), ())), "tn": (((0,), (0,)), ((), ()))}


def _mm(name, a, b, mode, out_dtype, *, m, n, k, tm, tn, tk, a_off=0, b_off=0, res=None, bias=None, b_sh=None, o_sh=None):
    tm, tn, tk = min(tm, m), min(tn, n), min(tk, k)
    assert m % tm == 0 and n % tn == 0 and k % tk == 0, (name, m, n, k, tm, tn, tk)
    nk = k // tk
    has_res, has_bias = res is not None, bias is not None

    def body(*refs):
        a_ref, b_ref = refs[0], refs[1]
        pos = 2
        res_ref = bias_ref = None
        if has_res:
            res_ref = refs[pos]
            pos += 1
        if has_bias:
            bias_ref = refs[pos]
            pos += 1
        o_ref, acc_ref = refs[pos], refs[pos + 1]
        kk = pl.program_id(2)

        @pl.when(kk == 0)
        def _():
            acc_ref[...] = jnp.zeros_like(acc_ref)

        acc_ref[...] += lax.dot_general(a_ref[...].astype(BF16), b_ref[...].astype(BF16), _DIMS[mode],
                                        preferred_element_type=F32)

        @pl.when(kk == nk - 1)
        def _():
            r = acc_ref[...]
            if has_bias:
                r = r + bias_ref[...]
            if has_res:
                r = r + res_ref[...].astype(F32)
            o_ref[...] = r.astype(o_ref.dtype)

    if mode == "nn":
        a_spec = pl.BlockSpec((tm, tk), lambda i, j, kk: (i, kk + a_off))
        b_spec = pl.BlockSpec((tk, tn), lambda i, j, kk: (kk, j + b_off))
        if b_sh is not None:
            assert b_sh % tn == 0, (name, b_sh, tn)
            per = b_sh // tn
            b_spec = pl.BlockSpec((None, tk, tn), lambda i, j, kk: (j // per, kk, j % per))
    elif mode == "nt":
        a_spec = pl.BlockSpec((tm, tk), lambda i, j, kk: (i, kk + a_off))
        b_spec = pl.BlockSpec((tn, tk), lambda i, j, kk: (j, kk + b_off))
        if b_sh is not None:
            assert b_sh % tk == 0, (name, b_sh, tk)
            per = b_sh // tk
            b_spec = pl.BlockSpec((None, tn, tk), lambda i, j, kk: (kk // per, j, kk % per))
    else:
        a_spec = pl.BlockSpec((tk, tm), lambda i, j, kk: (kk, i + a_off))
        b_spec = pl.BlockSpec((tk, tn), lambda i, j, kk: (kk, j + b_off))
    in_specs, args = [a_spec, b_spec], [a, b]
    if has_res:
        in_specs.append(pl.BlockSpec((tm, tn), lambda i, j, kk: (i, j)))
        args.append(res)
    if has_bias:
        in_specs.append(pl.BlockSpec((1, tn), lambda i, j, kk: (0, j)))
        args.append(bias)
    out_spec, out_shape = pl.BlockSpec((tm, tn), lambda i, j, kk: (i, j)), (m, n)
    if o_sh is not None:
        assert o_sh % tn == 0, (name, o_sh, tn)
        per_o = o_sh // tn
        out_spec = pl.BlockSpec((None, tm, tn), lambda i, j, kk: (j // per_o, i, j % per_o))
        out_shape = (n // o_sh, m, o_sh)
    return pl.pallas_call(
        body, out_shape=jax.ShapeDtypeStruct(out_shape, out_dtype), grid=(m // tm, n // tn, nk),
        in_specs=in_specs, out_specs=out_spec,
        scratch_shapes=[pltpu.VMEM((tm, tn), F32)], name=name,
        compiler_params=_params(("parallel", "parallel", "arbitrary")))(*args)


def _sig(v):
    return jax.nn.sigmoid(v)


def _rms_fwd(x, g):
    r = lax.rsqrt(jnp.mean(x * x, axis=-1, keepdims=True) + EPS)
    return x * r * g


def _rms_bwd(dh, x, dres, g):
    dh = dh.astype(F32)
    r = lax.rsqrt(jnp.mean(x * x, axis=-1, keepdims=True) + EPS)
    xh = x * r
    dxh = dh * g
    dx = r * (dxh - xh * jnp.mean(dxh * xh, axis=-1, keepdims=True))
    return dres + dx, jnp.sum(dh * xh, axis=0, keepdims=True)


def _rope_apply(t, c, sa, sb):
    w = t.shape[1]
    return t * c + pltpu.roll(t, w - ROPE_HALF, 1) * sa + pltpu.roll(t, ROPE_HALF, 1) * sb


def _rope_transpose(g, c, sa, sb):
    w = g.shape[1]
    return g * c + pltpu.roll(g * sa, ROPE_HALF, 1) + pltpu.roll(g * sb, w - ROPE_HALF, 1)


def _tile_lanes(tab, reps):
    return jnp.concatenate([tab] * reps, axis=1) if reps > 1 else tab


def _rope_fwd(q, k, c, sa, sb):
    rq = Q_WIDTH // c.shape[1]
    qr = _rope_apply(q.astype(F32), _tile_lanes(c, rq), _tile_lanes(sa, rq), _tile_lanes(sb, rq))
    kr = _rope_apply(k.astype(F32), c, sa, sb)
    return qr, kr


def _rope_bwd_q(g, c, sa, sb):
    rq = Q_WIDTH // c.shape[1]
    return _rope_transpose(g.astype(F32), _tile_lanes(c, rq), _tile_lanes(sa, rq), _tile_lanes(sb, rq))


def _gelu(v):
    return jax.nn.gelu(v, approximate=True)


def _gelu_grad(v):
    c0 = math.sqrt(2.0 / math.pi)
    inner = c0 * (v + 0.044715 * v * v * v)
    th = jnp.tanh(inner)
    return 0.5 * (1.0 + th) + 0.5 * v * (1.0 - th * th) * c0 * (1.0 + 3 * 0.044715 * v * v)


def _merge_fwd(g0, g1, g2, b0, b1, b2, ya, ga, gb, yc):
    s0 = _sig(g0.astype(F32) + b0)
    s1 = _sig(g1.astype(F32) + b1)
    s2 = _sig(g2.astype(F32) + b2)
    ys = ga.astype(F32) * _sig(gb.astype(F32))
    return s0 * ya.astype(F32) + s1 * ys + s2 * yc.astype(F32)


def _merge_bwd(dm, g0, g1, g2, b0, b1, b2, ya, ga, gb, yc):
    dm = dm.astype(F32)
    s0 = _sig(g0.astype(F32) + b0)
    s1 = _sig(g1.astype(F32) + b1)
    s2 = _sig(g2.astype(F32) + b2)
    ga = ga.astype(F32)
    sb = _sig(gb.astype(F32))
    ys = ga * sb
    dya = dm * s0
    dys = dm * s1
    dyc = dm * s2
    dga = dys * sb
    dgb = dys * ga * sb * (1.0 - sb)
    d0 = dm * ya.astype(F32) * s0 * (1.0 - s0)
    d1 = dm * ys * s1 * (1.0 - s1)
    d2 = dm * yc.astype(F32) * s2 * (1.0 - s2)
    cs = lambda v: jnp.sum(v, axis=0, keepdims=True)
    return dya, dga, dgb, dyc, d0, d1, d2, cs(d0), cs(d1), cs(d2), cs(dga), cs(dgb)


def _ffn_act(fg, fu):
    fg = fg.astype(F32)
    return fg * _sig(fg) * fu.astype(F32)


def _ffn_act_bwd(da, fg, fu):
    da, fg, fu = da.astype(F32), fg.astype(F32), fu.astype(F32)
    s = _sig(fg)
    return da * fu * (s * (1.0 + fg * (1.0 - s))), da * fg * s


def _ple_fwd(x, gp, e):
    return x + _sig(gp.astype(F32)) * e.astype(F32)


def _ple_bwd(dx, gp, e):
    s = _sig(gp.astype(F32))
    e = e.astype(F32)
    return dx * s, dx * e * s * (1.0 - s)


def _loss_fn(x, tgt, g):
    d = x.shape[1]
    r = lax.rsqrt(jnp.mean(x * x, axis=-1, keepdims=True) + EPS)
    xh = x * r
    err = xh * g - tgt
    dy = err * (1.0 / d)
    dxh = dy * g
    dx = r * (dxh - xh * jnp.mean(dxh * xh, axis=-1, keepdims=True))
    return dx, jnp.sum(err * err, axis=0, keepdims=True) * (0.5 / d), jnp.sum(dy * xh, axis=0, keepdims=True)


def _adamw_fn(w, g, m, v):
    m = ADAM_B1 * m + (1.0 - ADAM_B1) * g
    v = ADAM_B2 * v + (1.0 - ADAM_B2) * (g * g)
    m_hat = m / (1.0 - ADAM_B1 ** ADAM_STEP)
    v_hat = v / (1.0 - ADAM_B2 ** ADAM_STEP)
    delta = -ADAM_LR * (m_hat / (jnp.sqrt(v_hat) + ADAM_EPS) + ADAM_WD * w)
    return delta, m, v


def _band_mask(n):
    qi = lax.broadcasted_iota(jnp.int32, (ATT_BLOCK, 2 * ATT_BLOCK), 0)
    kj = lax.broadcasted_iota(jnp.int32, (ATT_BLOCK, 2 * ATT_BLOCK), 1)
    dist = qi + ATT_BLOCK - kj
    return (dist >= 0) & (dist < ATT_BLOCK) & ((n > 0) | (kj >= ATT_BLOCK))


def _att_specs(nb):
    cur = lambda b, n: (0, b * nb + n, 0)
    prev = lambda b, n: (0, b * nb + jnp.maximum(n - 1, 0), 0)
    qs = pl.BlockSpec((N_Q_HEADS, ATT_BLOCK, HEAD_DIM), cur)
    kc = pl.BlockSpec((N_KV_HEADS, ATT_BLOCK, HEAD_DIM), cur)
    kp = pl.BlockSpec((N_KV_HEADS, ATT_BLOCK, HEAD_DIM), prev)
    stat = pl.BlockSpec((N_Q_HEADS, ATT_BLOCK, 1), cur)
    sink = pl.BlockSpec((N_Q_HEADS, 1, 1), lambda b, n: (0, 0, 0))
    return qs, kc, kp, stat, sink


def _attn_fwd(qh, kh, vh, sinks, nbatch, seq):
    t = qh.shape[1]
    nb = seq // ATT_BLOCK
    qs, kc, kp, stat, sink = _att_specs(nb)

    def body(q_ref, kp_ref, kc_ref, vp_ref, vc_ref, sink_ref, o_ref, lse_ref):
        mask = _band_mask(pl.program_id(1))
        for h in range(N_Q_HEADS):
            kv = h // GQA_GROUP
            kk = jnp.concatenate([kp_ref[kv], kc_ref[kv]], axis=0)
            vv = jnp.concatenate([vp_ref[kv], vc_ref[kv]], axis=0)
            s = lax.dot_general(q_ref[h], kk, _DIMS["nt"], preferred_element_type=F32) * SCALE
            s = jnp.where(mask, s, NEG_INF)
            sk = sink_ref[h]
            mx = jnp.maximum(jnp.max(s, axis=-1, keepdims=True), sk)
            p = jnp.exp(s - mx)
            den = jnp.sum(p, axis=-1, keepdims=True) + jnp.exp(sk - mx)
            o = lax.dot_general((p / den).astype(BF16), vv, _DIMS["nn"], preferred_element_type=F32)
            o_ref[h] = o.astype(o_ref.dtype)
            lse_ref[h] = mx + jnp.log(den)

    return pl.pallas_call(
        body, grid=(nbatch, nb), in_specs=[qs, kp, kc, kp, kc, sink], out_specs=[qs, stat],
        out_shape=[jax.ShapeDtypeStruct((N_Q_HEADS, t, HEAD_DIM), BF16), jax.ShapeDtypeStruct((N_Q_HEADS, t, 1), F32)],
        name="attn_fwd", compiler_params=_params(("parallel", "parallel")))(qh, kh, kh, vh, vh, sinks)


def _attn_bwd(qh, kh, vh, oh, doh, lse, sinks, nbatch, seq):
    t = qh.shape[1]
    nb = seq // ATT_BLOCK
    qs, kc, kp, stat, sink = _att_specs(nb)

    def body(q_ref, kp_ref, kc_ref, vp_ref, vc_ref, o_ref, do_ref, lse_ref, sink_ref,
             dq_ref, dkc_ref, dvc_ref, dkp_ref, dvp_ref, dsink_ref):
        first = (pl.program_id(0) == 0) & (pl.program_id(1) == 0)

        @pl.when(first)
        def _():
            dsink_ref[...] = jnp.zeros_like(dsink_ref)

        mask = _band_mask(pl.program_id(1))
        for kv in range(N_KV_HEADS):
            kk = jnp.concatenate([kp_ref[kv], kc_ref[kv]], axis=0)
            vv = jnp.concatenate([vp_ref[kv], vc_ref[kv]], axis=0)
            dk = jnp.zeros((2 * ATT_BLOCK, HEAD_DIM), F32)
            dv = jnp.zeros((2 * ATT_BLOCK, HEAD_DIM), F32)
            for h in range(kv * GQA_GROUP, (kv + 1) * GQA_GROUP):
                q = q_ref[h]
                do = do_ref[h]
                lse_h = lse_ref[h]
                s = lax.dot_general(q, kk, _DIMS["nt"], preferred_element_type=F32) * SCALE
                p = jnp.where(mask, jnp.exp(s - lse_h), 0.0)
                dd = jnp.sum(do.astype(F32) * o_ref[h].astype(F32), axis=-1, keepdims=True)
                dp = lax.dot_general(do, vv, _DIMS["nt"], preferred_element_type=F32)
                ds = (p * (dp - dd) * SCALE).astype(BF16)
                dq_ref[h] = lax.dot_general(ds, kk, _DIMS["nn"], preferred_element_type=F32).astype(dq_ref.dtype)
                dk = dk + lax.dot_general(ds, q, _DIMS["tn"], preferred_element_type=F32)
                dv = dv + lax.dot_general(p.astype(BF16), do, _DIMS["tn"], preferred_element_type=F32)
                dsk = -jnp.sum(jnp.exp(sink_ref[h] - lse_h) * dd, axis=0, keepdims=True)
                dsink_ref[h] += dsk
            dkp_ref[kv] = dk[:ATT_BLOCK]
            dkc_ref[kv] = dk[ATT_BLOCK:]
            dvp_ref[kv] = dv[:ATT_BLOCK]
            dvc_ref[kv] = dv[ATT_BLOCK:]

    kvs = jax.ShapeDtypeStruct((N_KV_HEADS, t, HEAD_DIM), F32)
    return pl.pallas_call(
        body, grid=(nbatch, nb), in_specs=[qs, kp, kc, kp, kc, qs, qs, stat, sink],
        out_specs=[qs, kc, kc, kc, kc, sink],
        out_shape=[jax.ShapeDtypeStruct((N_Q_HEADS, t, HEAD_DIM), F32), kvs, kvs, kvs, kvs,
                   jax.ShapeDtypeStruct((N_Q_HEADS, 1, 1), F32)],
        name="attn_bwd", compiler_params=_params(("arbitrary", "arbitrary")))(qh, kh, kh, vh, vh, oh, doh, lse, sinks)


def _kv_combine(dkc, dkp, dvc, dvp, c, sa, sb, seq):
    t = dkc.shape[0]
    nb = seq // ATT_BLOCK
    nblk = t // ATT_BLOCK

    def body(kc_ref, kp_ref, vc_ref, vp_ref, c_ref, sa_ref, sb_ref, dk_ref, dv_ref):
        has_next = (pl.program_id(0) % nb) != nb - 1
        dk = kc_ref[...] + jnp.where(has_next, kp_ref[...], 0.0)
        dv = vc_ref[...] + jnp.where(has_next, vp_ref[...], 0.0)
        dk_ref[...] = _rope_transpose(dk, c_ref[...], sa_ref[...], sb_ref[...]).astype(dk_ref.dtype)
        dv_ref[...] = dv.astype(dv_ref.dtype)

    cur = pl.BlockSpec((ATT_BLOCK, KV_WIDTH), lambda i: (i, 0))
    nxt = pl.BlockSpec((ATT_BLOCK, KV_WIDTH), lambda i: (jnp.minimum(i + 1, nblk - 1), 0))
    o = jax.ShapeDtypeStruct((t, KV_WIDTH), BF16)
    return pl.pallas_call(body, grid=(nblk,), in_specs=[cur, nxt, cur, nxt, cur, cur, cur], out_specs=[cur, cur],
                          out_shape=[o, o], name="kv_combine", compiler_params=_params(("parallel",)))(
        dkc, dkp, dvc, dvp, c, sa, sb)


def _scan_block(ref, tab_ref, carry, ngroups, reverse):
    shifts = (7, 6, 4) if reverse else (1, 2, 4)
    n = SSM_LANES

    def step(i, car):
        g = (ngroups - 1 - i) if reverse else i
        r0 = pl.multiple_of(g * SUBLANES, SUBLANES)
        xr = ref[pl.ds(r0, SUBLANES), :n]
        xi = ref[pl.ds(r0, SUBLANES), n:]
        for s, sh in enumerate(shifts):
            pr, pi = tab_ref[2 * s], tab_ref[2 * s + 1]
            yr, yi = pltpu.roll(xr, sh, 0), pltpu.roll(xi, sh, 0)
            xr, xi = xr + pr * yr - pi * yi, xi + pr * yi + pi * yr
        cr, ci = car
        qr, qi = tab_ref[6], tab_ref[7]
        xr, xi = xr + qr * cr - qi * ci, xi + qr * ci + qi * cr
        ref[pl.ds(r0, SUBLANES), :n] = xr
        ref[pl.ds(r0, SUBLANES), n:] = xi
        last = r0 if reverse else r0 + SUBLANES - 1
        return ref[pl.ds(last, 1), :n], ref[pl.ds(last, 1), n:]

    return lax.fori_loop(0, ngroups, step, carry)


def _ssm_chunk(seq):
    return min(256, seq)


def _ssm_fwd(z, wb, wc, tab, dskip, nbatch, seq):
    t = z.shape[0]
    tc = _ssm_chunk(seq)
    nc = seq // tc
    n2 = 2 * SSM_LANES

    def body(u_ref, wb_ref, wc_ref, tab_ref, d_ref, st_ref, y_ref, gel_ref, car_ref):
        @pl.when(pl.program_id(1) == 0)
        def _():
            car_ref[...] = jnp.zeros_like(car_ref)

        u = u_ref[...]
        st_ref[...] = lax.dot_general(u, wb_ref[...], _DIMS["nn"], preferred_element_type=F32)
        cr, ci = _scan_block(st_ref, tab_ref, (car_ref[:, :SSM_LANES], car_ref[:, SSM_LANES:]), tc // SUBLANES, False)
        car_ref[:, :SSM_LANES] = cr
        car_ref[:, SSM_LANES:] = ci
        y = lax.dot_general(st_ref[...].astype(BF16), wc_ref[...], _DIMS["nn"], preferred_element_type=F32)
        y = y + d_ref[...] * u.astype(F32)
        y_ref[...] = y
        gel_ref[...] = _gelu(y).astype(gel_ref.dtype)

    row = lambda b, c: (b * nc + c, 0)
    full = lambda b, c: (0, 0)
    return pl.pallas_call(
        body, grid=(nbatch, nc),
        in_specs=[pl.BlockSpec((tc, SSM_WIDTH), lambda b, c: (b * nc + c, 3)), pl.BlockSpec((SSM_WIDTH, n2), full),
                  pl.BlockSpec((n2, SSM_WIDTH), full), pl.BlockSpec((8, SUBLANES, SSM_LANES), lambda b, c: (0, 0, 0)),
                  pl.BlockSpec((1, SSM_WIDTH), full)],
        out_specs=[pl.BlockSpec((tc, n2), row), pl.BlockSpec((tc, SSM_WIDTH), row), pl.BlockSpec((tc, SSM_WIDTH), row)],
        out_shape=[jax.ShapeDtypeStruct((t, n2), F32), jax.ShapeDtypeStruct((t, SSM_WIDTH), F32),
                   jax.ShapeDtypeStruct((t, SSM_WIDTH), BF16)],
        scratch_shapes=[pltpu.VMEM((1, n2), F32)], name="ssm_fwd",
        compiler_params=_params(("arbitrary", "arbitrary")))(z, wb, wc, tab, dskip)


def _ssm_bwd(dgi, ys, st, z, wbt, wct, tab_rev, dskip, nbatch, seq):
    t = z.shape[0]
    tc = _ssm_chunk(seq)
    nc = seq // tc
    n = SSM_LANES
    n2 = 2 * n
    ng = tc // SUBLANES

    def body(dgi_ref, ys_ref, st_ref, stp_ref, u_ref, wbt_ref, wct_ref, tab_ref, d_ref,
             du_ref, dwb_ref, dwc_ref, dd_ref, da_ref, p_ref, sb_ref, car_ref):
        b, c = pl.program_id(0), pl.program_id(1)
        ct = nc - 1 - c

        @pl.when((b == 0) & (c == 0))
        def _():
            dwb_ref[...] = jnp.zeros_like(dwb_ref)
            dwc_ref[...] = jnp.zeros_like(dwc_ref)
            dd_ref[...] = jnp.zeros_like(dd_ref)
            da_ref[...] = jnp.zeros_like(da_ref)

        @pl.when(c == 0)
        def _():
            car_ref[...] = jnp.zeros_like(car_ref)

        u = u_ref[...]
        dys = dgi_ref[...].astype(F32) * _gelu_grad(ys_ref[...])
        dys_b = dys.astype(BF16)
        st = st_ref[...]
        dd_ref[...] += jnp.sum(dys * u.astype(F32), axis=0, keepdims=True)
        dwc_ref[...] += lax.dot_general(st.astype(BF16), dys_b, _DIMS["tn"], preferred_element_type=F32)
        p_ref[...] = lax.dot_general(dys_b, wct_ref[...], _DIMS["nn"], preferred_element_type=F32)
        cr, ci = _scan_block(p_ref, tab_ref, (car_ref[:, :n], car_ref[:, n:]), ng, True)
        car_ref[:, :n] = cr
        car_ref[:, n:] = ci
        p = p_ref[...]
        pb = p.astype(BF16)
        dwb_ref[...] += lax.dot_general(u, pb, _DIMS["tn"], preferred_element_type=F32)
        du = lax.dot_general(pb, wbt_ref[...], _DIMS["nn"], preferred_element_type=F32) + d_ref[...] * dys
        du_ref[...] = du.astype(du_ref.dtype)
        sb_ref[pl.ds(0, SUBLANES), :] = jnp.where(ct > 0, stp_ref[...], 0.0)
        sb_ref[pl.ds(SUBLANES, tc), :] = st
        row0 = lax.broadcasted_iota(jnp.int32, (SUBLANES, n), 0) == 0

        def acc_step(g, acc):
            ar, ai = acc
            r0 = pl.multiple_of(g * SUBLANES, SUBLANES)
            edge_r = sb_ref[pl.ds(r0 + SUBLANES - 1, 1), :n]
            edge_i = sb_ref[pl.ds(r0 + SUBLANES - 1, 1), n:]
            sr = jnp.where(row0, edge_r, pltpu.roll(sb_ref[pl.ds(r0 + SUBLANES, SUBLANES), :n], 1, 0))
            si = jnp.where(row0, edge_i, pltpu.roll(sb_ref[pl.ds(r0 + SUBLANES, SUBLANES), n:], 1, 0))
            pr = p_ref[pl.ds(r0, SUBLANES), :n]
            pi = p_ref[pl.ds(r0, SUBLANES), n:]
            return ar + pr * sr + pi * si, ai + pi * sr - pr * si

        zero = jnp.zeros((SUBLANES, n), F32)
        ar, ai = lax.fori_loop(0, ng, acc_step, (zero, zero))
        da_ref[:, :n] += ar
        da_ref[:, n:] += ai

    row = lambda b, c: (b * nc + (nc - 1 - c), 0)
    prev8 = lambda b, c: (jnp.maximum((b * nc + (nc - 1 - c)) * (tc // SUBLANES) - 1, 0), 0)
    full = lambda b, c: (0, 0)
    return pl.pallas_call(
        body, grid=(nbatch, nc),
        in_specs=[pl.BlockSpec((tc, SSM_WIDTH), row), pl.BlockSpec((tc, SSM_WIDTH), row), pl.BlockSpec((tc, n2), row),
                  pl.BlockSpec((SUBLANES, n2), prev8),
                  pl.BlockSpec((tc, SSM_WIDTH), lambda b, c: (b * nc + (nc - 1 - c), 3)),
                  pl.BlockSpec((n2, SSM_WIDTH), full), pl.BlockSpec((SSM_WIDTH, n2), full),
                  pl.BlockSpec((8, SUBLANES, n), lambda b, c: (0, 0, 0)), pl.BlockSpec((1, SSM_WIDTH), full)],
        out_specs=[pl.BlockSpec((tc, SSM_WIDTH), row), pl.BlockSpec((SSM_WIDTH, n2), full),
                   pl.BlockSpec((n2, SSM_WIDTH), full), pl.BlockSpec((1, SSM_WIDTH), full),
                   pl.BlockSpec((SUBLANES, n2), full)],
        out_shape=[jax.ShapeDtypeStruct((t, SSM_WIDTH), BF16), jax.ShapeDtypeStruct((SSM_WIDTH, n2), F32),
                   jax.ShapeDtypeStruct((n2, SSM_WIDTH), F32), jax.ShapeDtypeStruct((1, SSM_WIDTH), F32),
                   jax.ShapeDtypeStruct((SUBLANES, n2), F32)],
        scratch_shapes=[pltpu.VMEM((tc, n2), F32), pltpu.VMEM((tc + SUBLANES, n2), F32), pltpu.VMEM((1, n2), F32)],
        name="ssm_bwd", compiler_params=_params(("arbitrary", "arbitrary")))(
        dgi, ys, st, st, z, wbt, wct, tab_rev, dskip)


def _ssm_prep(lam_re, lam_im, log_dt, b_re, b_im, c_re, c_im):
    lr = jnp.minimum(lam_re, -1e-4)
    li = lam_im
    dt = jnp.exp(log_dt)[:, None]
    mag = jnp.exp(lr * dt)
    a_re = mag * jnp.cos(li * dt)
    a_im = mag * jnp.sin(li * dt)
    den = lr * lr + li * li
    x_re, x_im = a_re - 1.0, a_im
    f_re = (x_re * lr + x_im * li) / den
    f_im = (x_im * lr - x_re * li) / den
    bb_re = f_re[..., None] * b_re - f_im[..., None] * b_im
    bb_im = f_re[..., None] * b_im + f_im[..., None] * b_re
    eye = jnp.eye(SSM_GROUPS, dtype=F32)
    emb_b = lambda v: jnp.einsum("gnh,gk->ghkn", v, eye).reshape(SSM_WIDTH, SSM_LANES)
    emb_c = lambda v: jnp.einsum("ghn,gk->gnkh", v, eye).reshape(SSM_LANES, SSM_WIDTH)
    wb = jnp.concatenate([emb_b(bb_re), emb_b(bb_im)], axis=1)
    wc = jnp.concatenate([emb_c(c_re), -emb_c(c_im)], axis=0)
    return a_re.reshape(-1), a_im.reshape(-1), wb, wc


def _ssm_tables(a_re, a_im, reverse):
    if reverse:
        a_im = -a_im
    pw = [(a_re, a_im)]
    for _ in range(SUBLANES - 1):
        pr, pi = pw[-1]
        pw.append((pr * a_re - pi * a_im, pr * a_im + pi * a_re))
    rows = jnp.arange(SUBLANES)[:, None]
    tabs = []
    for k in (1, 2, 4):
        ok = (rows + k <= SUBLANES - 1) if reverse else (rows >= k)
        tabs += [jnp.where(ok, pw[k - 1][0][None], 0.0), jnp.where(ok, pw[k - 1][1][None], 0.0)]
    order = list(range(SUBLANES - 1, -1, -1)) if reverse else list(range(SUBLANES))
    tabs += [jnp.stack([pw[i][0] for i in order]), jnp.stack([pw[i][1] for i in order])]
    return jnp.stack(tabs)


def _conv_chunk(seq):
    return min(512, seq)


def _conv_fwd(z, w, bias, lg, lb, nbatch, seq):
    t = z.shape[0]
    tc = _conv_chunk(seq)
    nc = seq // tc

    def body(a_ref, g_ref, w_ref, b_ref, lg_ref, lb_ref, cv_ref, sc_ref, ubuf):
        c = pl.program_id(1)

        @pl.when(c == 0)
        def _():
            ubuf[pl.ds(0, CONV_HALO), :] = jnp.zeros((CONV_HALO, CONV_WIDTH), F32)

        @pl.when(c > 0)
        def _():
            ubuf[pl.ds(0, CONV_HALO), :] = ubuf[pl.ds(tc, CONV_HALO), :]

        ubuf[pl.ds(CONV_HALO, tc), :] = a_ref[...].astype(F32) * _sig(g_ref[...].astype(F32))
        acc = jnp.zeros((tc, CONV_WIDTH), F32) + b_ref[...]
        for k in range(CONV_K):
            acc = acc + w_ref[pl.ds(k, 1), :] * ubuf[pl.ds(CONV_HALO - (CONV_K - 1) + k, tc), :]
        cv_ref[...] = acc
        mu = jnp.mean(acc, axis=-1, keepdims=True)
        xc = acc - mu
        y = xc * lax.rsqrt(jnp.mean(xc * xc, axis=-1, keepdims=True) + EPS) * lg_ref[...] + lb_ref[...]
        sc_ref[...] = (y * _sig(y)).astype(sc_ref.dtype)

    row = lambda b, c: (b * nc + c, 0)
    full = lambda b, c: (0, 0)
    vec = pl.BlockSpec((1, CONV_WIDTH), full)
    return pl.pallas_call(
        body, grid=(nbatch, nc),
        in_specs=[pl.BlockSpec((tc, CONV_WIDTH), lambda b, c: (b * nc + c, 4)),
                  pl.BlockSpec((tc, CONV_WIDTH), lambda b, c: (b * nc + c, 5)),
                  pl.BlockSpec((CONV_HALO, CONV_WIDTH), full), vec, vec, vec],
        out_specs=[pl.BlockSpec((tc, CONV_WIDTH), row), pl.BlockSpec((tc, CONV_WIDTH), row)],
        out_shape=[jax.ShapeDtypeStruct((t, CONV_WIDTH), F32), jax.ShapeDtypeStruct((t, CONV_WIDTH), BF16)],
        scratch_shapes=[pltpu.VMEM((CONV_HALO + tc, CONV_WIDTH), F32)], name="conv_fwd",
        compiler_params=_params(("arbitrary", "arbitrary")))(z, z, w, bias, lg, lb)


def _conv_bwd(dsc, cv, z, w, lg, lb, nbatch, seq):
    t = z.shape[0]
    tc = _conv_chunk(seq)
    nc = seq // tc
    hb = tc // CONV_HALO

    def body(dsc_ref, cv_ref, a_ref, g_ref, ap_ref, gp_ref, w_ref, lg_ref, lb_ref,
             da_ref, dg_ref, dw_ref, db_ref, dlg_ref, dlb_ref, ubuf, dbuf):
        b, c = pl.program_id(0), pl.program_id(1)
        ct = nc - 1 - c

        @pl.when((b == 0) & (c == 0))
        def _():
            dw_ref[...] = jnp.zeros_like(dw_ref)
            db_ref[...] = jnp.zeros_like(db_ref)
            dlg_ref[...] = jnp.zeros_like(dlg_ref)
            dlb_ref[...] = jnp.zeros_like(dlb_ref)

        cvv = cv_ref[...]
        mu = jnp.mean(cvv, axis=-1, keepdims=True)
        xc = cvv - mu
        rstd = lax.rsqrt(jnp.mean(xc * xc, axis=-1, keepdims=True) + EPS)
        xh = xc * rstd
        y = xh * lg_ref[...] + lb_ref[...]
        sy = _sig(y)
        dy = dsc_ref[...].astype(F32) * (sy * (1.0 + y * (1.0 - sy)))
        dlg_ref[...] += jnp.sum(dy * xh, axis=0, keepdims=True)
        dlb_ref[...] += jnp.sum(dy, axis=0, keepdims=True)
        dxh = dy * lg_ref[...]
        dcv = rstd * (dxh - jnp.mean(dxh, axis=-1, keepdims=True) - xh * jnp.mean(dxh * xh, axis=-1, keepdims=True))
        db_ref[...] += jnp.sum(dcv, axis=0, keepdims=True)

        @pl.when(c == 0)
        def _():
            dbuf[pl.ds(tc, CONV_HALO), :] = jnp.zeros((CONV_HALO, CONV_WIDTH), F32)

        @pl.when(c > 0)
        def _():
            dbuf[pl.ds(tc, CONV_HALO), :] = dbuf[pl.ds(0, CONV_HALO), :]

        dbuf[pl.ds(0, tc), :] = dcv
        a = a_ref[...].astype(F32)
        sg = _sig(g_ref[...].astype(F32))
        ubuf[pl.ds(0, CONV_HALO), :] = jnp.where(ct > 0, ap_ref[...].astype(F32) * _sig(gp_ref[...].astype(F32)), 0.0)
        ubuf[pl.ds(CONV_HALO, tc), :] = a * sg
        du = jnp.zeros((tc, CONV_WIDTH), F32)
        for k in range(CONV_K):
            du = du + w_ref[pl.ds(k, 1), :] * dbuf[pl.ds(CONV_K - 1 - k, tc), :]
            dw_ref[pl.ds(k, 1), :] += jnp.sum(dcv * ubuf[pl.ds(CONV_HALO - (CONV_K - 1) + k, tc), :],
                                             axis=0, keepdims=True)
        da_ref[...] = (du * sg).astype(da_ref.dtype)
        dg_ref[...] = (du * a * sg * (1.0 - sg)).astype(dg_ref.dtype)

    row = lambda b, c: (b * nc + (nc - 1 - c), 0)
    full = lambda b, c: (0, 0)
    vec = pl.BlockSpec((1, CONV_WIDTH), full)
    blk = pl.BlockSpec((tc, CONV_WIDTH), row)

    def zcol(col):
        return pl.BlockSpec((tc, CONV_WIDTH), lambda b, c: (b * nc + (nc - 1 - c), col))

    def zprev(col):
        return pl.BlockSpec((CONV_HALO, CONV_WIDTH),
                            lambda b, c: (jnp.maximum((b * nc + (nc - 1 - c)) * hb - 1, 0), col))

    o = jax.ShapeDtypeStruct((t, CONV_WIDTH), BF16)
    v = jax.ShapeDtypeStruct((1, CONV_WIDTH), F32)
    return pl.pallas_call(
        body, grid=(nbatch, nc),
        in_specs=[blk, blk, zcol(4), zcol(5), zprev(4), zprev(5), pl.BlockSpec((CONV_HALO, CONV_WIDTH), full), vec, vec],
        out_specs=[blk, blk, pl.BlockSpec((CONV_HALO, CONV_WIDTH), full), vec, vec, vec],
        out_shape=[o, o, jax.ShapeDtypeStruct((CONV_HALO, CONV_WIDTH), F32), v, v, v],
        scratch_shapes=[pltpu.VMEM((CONV_HALO + tc, CONV_WIDTH), F32), pltpu.VMEM((tc + CONV_HALO, CONV_WIDTH), F32)],
        name="conv_bwd", compiler_params=_params(("arbitrary", "arbitrary")))(dsc, cv, z, z, z, z, w, lg, lb)


BIG = ("w_in", "w_attn_out", "w_ssm_glu", "w_conv_out", "w_mix_out", "w_ffn_in", "w_ffn_out", "w_ple_in", "w_ple_gate")
BIG_AXIS = {"w_in": 2, "w_attn_out": 2, "w_ssm_glu": 2, "w_conv_out": 2, "w_mix_out": 1, "w_ffn_in": 2,
            "w_ffn_out": 1, "w_ple_in": 2, "w_ple_gate": 1}
SMALL = ("mix_norm_g", "b_gate", "attn_sinks", "ssm_lambda_re", "ssm_lambda_im", "ssm_log_dt", "ssm_b_re", "ssm_b_im",
         "ssm_c_re", "ssm_c_im", "ssm_d", "b_ssm_glu", "conv_dw_w", "conv_dw_b", "conv_norm_g", "conv_norm_b",
         "ffn_norm_g", "ple_norm_g", "final_norm_g")
WEIGHTS = ("mix_norm_g", "w_in", "b_gate", "attn_sinks", "w_attn_out", "ssm_lambda_re", "ssm_lambda_im", "ssm_log_dt",
           "ssm_b_re", "ssm_b_im", "ssm_c_re", "ssm_c_im", "ssm_d", "w_ssm_glu", "b_ssm_glu", "conv_dw_w", "conv_dw_b",
           "conv_norm_g", "conv_norm_b", "w_conv_out", "w_mix_out", "ffn_norm_g", "w_ffn_in", "w_ffn_out", "w_ple_in",
           "ple_norm_g", "w_ple_gate", "final_norm_g")
SSM_NAMES = ("ssm_lambda_re", "ssm_lambda_im", "ssm_log_dt", "ssm_b_re", "ssm_b_im", "ssm_c_re", "ssm_c_im")


def _heads(v, nh):
    return v.reshape(v.shape[0], nh, HEAD_DIM).transpose(1, 0, 2)


def _tokens(v):
    return v.transpose(1, 0, 2).reshape(v.shape[1], v.shape[0] * HEAD_DIM)


def _row(v):
    return v.reshape(1, -1)


def _layer_fwd(x, p_l, w, s, rope, nbatch, seq):
    t = x.shape[0]
    tm = 512
    d = D_MODEL
    sv = {}
    sv["x"] = x
    h = _rowwise("rms_mix", _rms_fwd, [R(x), V(_row(s["mix_norm_g"]))], [O(d, BF16)], tm=tm)
    cs = {nm: w[nm].shape[2] for nm in BIG if BIG_AXIS[nm] == 2}
    z = _mm("mm_in", h, w["w_in"], "nn", BF16, m=t, n=N_CHIPS * cs["w_in"], k=d, tm=tm, tn=cs["w_in"], tk=1024,
            b_sh=cs["w_in"])
    sv["h"], sv["z"] = h, z
    c, sa, sb = rope
    qr, kr = _rowwise("rope_fwd", _rope_fwd, [R(z, Q_WIDTH, 0), R(z, KV_WIDTH, 4), R(c), R(sa), R(sb)],
                      [O(Q_WIDTH, BF16), O(KV_WIDTH, BF16)], tm=tm)
    qh, kh = _heads(qr, N_Q_HEADS), _heads(kr, N_KV_HEADS)
    vh = _heads(z[:, Q_WIDTH + KV_WIDTH:Q_WIDTH + 2 * KV_WIDTH], N_KV_HEADS)
    sinks = s["attn_sinks"].reshape(N_Q_HEADS, 1, 1)
    oh, lse = _attn_fwd(qh, kh, vh, sinks, nbatch, seq)
    o = _tokens(oh)
    ya = _mm("mm_attn_out", o, w["w_attn_out"], "nn", BF16, m=t, n=d, k=Q_WIDTH, tm=tm, tn=cs["w_attn_out"], tk=512,
             b_sh=cs["w_attn_out"])
    sv.update(qh=qh, kh=kh, vh=vh, oh=oh, lse=lse, o=o, ya=ya, sinks=sinks)
    ssm_args = [s[nm] for nm in SSM_NAMES]
    a_re, a_im, wb, wc = _ssm_prep(*ssm_args)
    dskip = _row(s["ssm_d"])
    st, ys, gel = _ssm_fwd(z, wb.astype(BF16), wc.astype(BF16), _ssm_tables(a_re, a_im, False), dskip, nbatch, seq)
    glu = _mm("mm_glu", gel, w["w_ssm_glu"], "nn", BF16, m=t, n=2 * d, k=SSM_WIDTH, tm=tm, tn=cs["w_ssm_glu"], tk=256,
              bias=_row(s["b_ssm_glu"]), b_sh=cs["w_ssm_glu"])
    sv.update(st=st, ys=ys, gel=gel, glu=glu, a=(a_re, a_im), wb=wb, wc=wc, dskip=dskip)
    cw = jnp.pad(s["conv_dw_w"], ((0, CONV_HALO - CONV_K), (0, 0)))
    cv, sc = _conv_fwd(z, cw, _row(s["conv_dw_b"]), _row(s["conv_norm_g"]), _row(s["conv_norm_b"]), nbatch, seq)
    yc = _mm("mm_conv_out", sc, w["w_conv_out"], "nn", BF16, m=t, n=d, k=CONV_WIDTH, tm=tm, tn=cs["w_conv_out"], tk=256,
             b_sh=cs["w_conv_out"])
    sv.update(cw=cw, cv=cv, sc=sc, yc=yc)
    bg = _row(s["b_gate"])
    merge_ins = [R(z, 512, 3), R(z, 512, 5), R(z, 512, 7), V(bg, 512, 0), V(bg, 512, 2), V(bg, 512, 4),
                 R(ya, 512, 0), R(glu, 512, 0), R(glu, 512, 2), R(yc, 512, 0)]
    merged = _rowwise("merge_fwd", _merge_fwd, merge_ins, [O(512, BF16, total=d)], tm=tm, ncol=2)
    x1 = _mm("mm_mix", merged, w["w_mix_out"], "nn", F32, m=t, n=d, k=d, tm=tm, tn=512, tk=1024, res=x)
    sv.update(merged=merged, x1=x1)
    hf = _rowwise("rms_ffn", _rms_fwd, [R(x1), V(_row(s["ffn_norm_g"]))], [O(d, BF16)], tm=tm)
    f = _mm("mm_ffn_in", hf, w["w_ffn_in"], "nn", BF16, m=t, n=2 * FFN_HIDDEN, k=d, tm=tm, tn=cs["w_ffn_in"], tk=1024,
            b_sh=cs["w_ffn_in"])
    fw = FFN_HIDDEN // 2
    act = _rowwise("ffn_act", _ffn_act, [R(f, fw, 0), R(f, fw, 2)], [O(fw, BF16, total=FFN_HIDDEN)], tm=tm, ncol=2)
    x2 = _mm("mm_ffn_out", act, w["w_ffn_out"], "nn", F32, m=t, n=d, k=FFN_HIDDEN, tm=tm, tn=512, tk=fw, res=x1)
    sv.update(hf=hf, f=f, act=act, x2=x2)
    e = _mm("mm_ple_in", p_l, w["w_ple_in"], "nn", BF16, m=t, n=d, k=p_l.shape[1], tm=tm, tn=cs["w_ple_in"], tk=256,
            b_sh=cs["w_ple_in"])
    hp = _rowwise("rms_ple", _rms_fwd, [R(x2), V(_row(s["ple_norm_g"]))], [O(d, BF16)], tm=tm)
    gp = _mm("mm_ple_gate", hp, w["w_ple_gate"], "nn", BF16, m=t, n=d, k=d, tm=tm, tn=512, tk=1024)
    x3 = _rowwise("ple_fwd", _ple_fwd, [R(x2), R(gp), R(e)], [O(d, F32)], tm=tm)
    sv.update(e=e, hp=hp, gp=gp, p=p_l)
    return x3, sv


def _layer_bwd(dx3, sv, w, s, rope, nbatch, seq):
    t = dx3.shape[0]
    tm = 512
    d = D_MODEL
    gb, gs = {}, {}
    cs = {nm: w[nm].shape[2] for nm in BIG if BIG_AXIS[nm] == 2}

    def wg(name, a, b, m, n, tm=512, shard=None):
        return _mm(name, a, b, "tn", BF16, m=m, n=n, k=t, tm=tm, tn=512 if shard is None else cs[shard], tk=512,
                   o_sh=None if shard is None else cs[shard])

    de, dgp = _rowwise("ple_bwd", _ple_bwd, [R(dx3), R(sv["gp"]), R(sv["e"])], [O(d, BF16), O(d, BF16)], tm=tm)
    gb["w_ple_in"] = wg("wg_ple_in", sv["p"], de, sv["p"].shape[1], d, shard="w_ple_in")
    gb["w_ple_gate"] = wg("wg_ple_gate", sv["hp"], dgp, d, d)
    dhp = _mm("mmb_ple_gate", dgp, w["w_ple_gate"], "nt", BF16, m=t, n=d, k=d, tm=tm, tn=512, tk=1024)
    dx2, gs["ple_norm_g"] = _rowwise("rms_ple_bwd", _rms_bwd, [R(dhp), R(sv["x2"]), R(dx3), V(_row(s["ple_norm_g"]))],
                                     [O(d, F32)], [A(d)], tm=tm)
    fw = FFN_HIDDEN // 2
    dact = _mm("mmb_ffn_out", dx2, w["w_ffn_out"], "nt", BF16, m=t, n=FFN_HIDDEN, k=d, tm=tm, tn=fw, tk=1024)
    gb["w_ffn_out"] = wg("wg_ffn_out", sv["act"], dx2, FFN_HIDDEN, d, tm=fw)
    dfg, dfu = _rowwise("ffn_act_bwd", _ffn_act_bwd, [R(dact, fw, 0), R(sv["f"], fw, 0), R(sv["f"], fw, 2)],
                        [O(fw, BF16, total=FFN_HIDDEN), O(fw, BF16, total=FFN_HIDDEN)], tm=tm, ncol=2)
    df = jnp.concatenate([dfg, dfu], axis=1)
    gb["w_ffn_in"] = wg("wg_ffn_in", sv["hf"], df, d, 2 * FFN_HIDDEN, tm=1024, shard="w_ffn_in")
    dhf = _mm("mmb_ffn_in", df, w["w_ffn_in"], "nt", BF16, m=t, n=d, k=2 * FFN_HIDDEN, tm=tm, tn=512,
              tk=cs["w_ffn_in"], b_sh=cs["w_ffn_in"])
    dx1, gs["ffn_norm_g"] = _rowwise("rms_ffn_bwd", _rms_bwd, [R(dhf), R(sv["x1"]), R(dx2), V(_row(s["ffn_norm_g"]))],
                                     [O(d, F32)], [A(d)], tm=tm)
    dm = _mm("mmb_mix", dx1, w["w_mix_out"], "nt", BF16, m=t, n=d, k=d, tm=tm, tn=512, tk=1024)
    gb["w_mix_out"] = wg("wg_mix", sv["merged"], dx1, d, d)
    z, glu, bg = sv["z"], sv["glu"], _row(s["b_gate"])
    ins = [R(dm, 512, 0), R(z, 512, 3), R(z, 512, 5), R(z, 512, 7), V(bg, 512, 0), V(bg, 512, 2), V(bg, 512, 4),
           R(sv["ya"], 512, 0), R(glu, 512, 0), R(glu, 512, 2), R(sv["yc"], 512, 0)]
    ob = lambda: O(512, BF16, total=d)
    ab = lambda: A(512, total=d)
    dya, dga, dgb, dyc, d0, d1, d2, db0, db1, db2, dba, dbb = _rowwise(
        "merge_bwd", _merge_bwd, ins, [ob() for _ in range(7)], [ab() for _ in range(5)], tm=tm, ncol=2)
    gs["b_gate"] = jnp.concatenate([db0, db1, db2], axis=1)
    gs["b_ssm_glu"] = jnp.concatenate([dba, dbb], axis=1)
    dglu = jnp.concatenate([dga, dgb], axis=1)
    gb["w_attn_out"] = wg("wg_attn_out", sv["o"], dya, Q_WIDTH, d, shard="w_attn_out")
    do = _mm("mmb_attn_out", dya, w["w_attn_out"], "nt", BF16, m=t, n=Q_WIDTH, k=d, tm=tm, tn=512,
             tk=cs["w_attn_out"], b_sh=cs["w_attn_out"])
    dqh, dkc, dvc, dkp, dvp, dsink = _attn_bwd(sv["qh"], sv["kh"], sv["vh"], sv["oh"], _heads(do, N_Q_HEADS),
                                               sv["lse"], sv["sinks"], nbatch, seq)
    gs["attn_sinks"] = dsink.reshape(-1)
    c, sa, sb = rope
    dq = _rowwise("rope_bwd_q", _rope_bwd_q, [R(_tokens(dqh)), R(c), R(sa), R(sb)], [O(Q_WIDTH, BF16)], tm=tm)
    dk, dv = _kv_combine(_tokens(dkc), _tokens(dkp), _tokens(dvc), _tokens(dvp), c, sa, sb, seq)
    gb["w_ssm_glu"] = wg("wg_ssm_glu", sv["gel"], dglu, SSM_WIDTH, 2 * d, shard="w_ssm_glu")
    dgi = _mm("mmb_glu", dglu, w["w_ssm_glu"], "nt", BF16, m=t, n=SSM_WIDTH, k=2 * d, tm=tm, tn=256,
              tk=cs["w_ssm_glu"], b_sh=cs["w_ssm_glu"])
    a_re, a_im = sv["a"]
    du, dwb, dwc, dd, da = _ssm_bwd(dgi, sv["ys"], sv["st"], z, sv["wb"].T.astype(BF16), sv["wc"].T.astype(BF16),
                                    _ssm_tables(a_re, a_im, True), sv["dskip"], nbatch, seq)
    gs["ssm_d"] = dd.reshape(-1)
    da = jnp.sum(da, axis=0)
    _, prep_vjp = jax.vjp(_ssm_prep, *[s[nm] for nm in SSM_NAMES])
    for nm, g in zip(SSM_NAMES, prep_vjp((da[:SSM_LANES], da[SSM_LANES:], dwb, dwc))):
        gs[nm] = g
    gb["w_conv_out"] = wg("wg_conv_out", sv["sc"], dyc, CONV_WIDTH, d, shard="w_conv_out")
    dsc = _mm("mmb_conv_out", dyc, w["w_conv_out"], "nt", BF16, m=t, n=CONV_WIDTH, k=d, tm=tm, tn=256,
              tk=cs["w_conv_out"], b_sh=cs["w_conv_out"])
    dca, dcg, dcw, dcb, dlg, dlb = _conv_bwd(dsc, sv["cv"], z, sv["cw"], _row(s["conv_norm_g"]),
                                             _row(s["conv_norm_b"]), nbatch, seq)
    gs["conv_dw_w"] = dcw[:CONV_K]
    gs["conv_dw_b"], gs["conv_norm_g"], gs["conv_norm_b"] = dcb.reshape(-1), dlg.reshape(-1), dlb.reshape(-1)
    dz = jnp.concatenate([dq, dk, dv, du, dca, dcg, d0, d1, d2], axis=1)
    gb["w_in"] = wg("wg_in", sv["h"], dz, d, dz.shape[1], tm=1024, shard="w_in")
    dh = _mm("mmb_in", dz, w["w_in"], "nt", BF16, m=t, n=d, k=dz.shape[1], tm=tm, tn=512, tk=cs["w_in"],
             b_sh=cs["w_in"])
    dx, gs["mix_norm_g"] = _rowwise("rms_mix_bwd", _rms_bwd, [R(dh), R(sv["x"]), R(dx1), V(_row(s["mix_norm_g"]))],
                                    [O(d, F32)], [A(d)], tm=tm)
    gs["mix_norm_g"], gs["ffn_norm_g"], gs["ple_norm_g"] = (gs[nm].reshape(-1) for nm in
                                                            ("mix_norm_g", "ffn_norm_g", "ple_norm_g"))
    gs["b_gate"], gs["b_ssm_glu"] = gs["b_gate"].reshape(-1), gs["b_ssm_glu"].reshape(-1)
    return dx, gb, gs


def _rope_tables(positions):
    inv_freq = ROPE_THETA ** (-jnp.arange(0, ROPE_DIM, 2, dtype=F32) / ROPE_DIM)
    ang = positions.reshape(-1).astype(F32)[:, None] * inv_freq
    cos, sin = jnp.cos(ang), jnp.sin(ang)
    t = ang.shape[0]
    rest = HEAD_DIM - ROPE_DIM
    c = jnp.concatenate([cos, cos, jnp.ones((t, rest), F32)], axis=1)
    sa = jnp.concatenate([-sin, jnp.zeros((t, HEAD_DIM - ROPE_HALF), F32)], axis=1)
    sb = jnp.concatenate([jnp.zeros((t, ROPE_HALF), F32), sin, jnp.zeros((t, rest), F32)], axis=1)
    two = lambda v: jnp.concatenate([v, v], axis=1)
    return two(c), two(sa), two(sb)


def _local_step(x, p, positions, loss_target, wfull, small):
    nbatch, seq, d = x.shape
    depth = p.shape[0]
    t = nbatch * seq
    rope = _rope_tables(positions)
    xs = x.reshape(t, d)
    saved, ws, ss = [], [], []
    for l in range(depth):
        w_l = {nm: wfull[nm][l] if BIG_AXIS[nm] == 2 else wfull[nm][l].reshape(-1, wfull[nm].shape[-1]) for nm in BIG}
        s_l = {nm: small[nm][l] for nm in SMALL if nm != "final_norm_g"}
        xs, sv = _layer_fwd(xs, p[l].reshape(t, -1), w_l, s_l, rope, nbatch, seq)
        saved.append(sv)
        ws.append(w_l)
        ss.append(s_l)
    dx, loss_cols, dgf = _rowwise("loss_head", _loss_fn, [R(xs), R(loss_target.reshape(t, d)),
                                                          V(_row(small["final_norm_g"]))],
                                  [O(d, F32)], [A(d), A(d)], tm=512)
    gbs, gss = [], []
    for l in reversed(range(depth)):
        dx, gb, gs = _layer_bwd(dx, saved[l], ws[l], ss[l], rope, nbatch, seq)
        gbs.insert(0, gb)
        gss.insert(0, gs)
    gbig = {nm: jnp.stack([g[nm] for g in gbs]).reshape(wfull[nm].shape) for nm in BIG}
    gsmall = {nm: jnp.stack([g[nm] for g in gss]) for nm in SMALL if nm != "final_norm_g"}
    gsmall["final_norm_g"] = dgf.reshape(-1)
    return loss_cols, dx.reshape(nbatch, seq, d), gbig, gsmall


HBM = pl.BlockSpec(memory_space=pltpu.HBM)


def _place():
    x, y, c = lax.axis_index("x"), lax.axis_index("y"), lax.axis_index("c")
    chips = [(1 - x, y), (x, 1 - y), (1 - x, 1 - y)]
    return x, y, c, chips


def _remote(src, dst, send_sem, recv_sem, to):
    return pltpu.make_async_remote_copy(src_ref=src, dst_ref=dst, send_sem=send_sem, recv_sem=recv_sem,
                                        device_id=to, device_id_type=MESH)


def _gather_shards(shards):
    n = len(shards)
    hl = shards[0].shape[0] // 2

    def body(*refs):
        ins, outs = refs[:n], refs[n:2 * n]
        send1, recv1, send2, recv2, lsem = refs[2 * n:]
        x, y, c, chips = _place()
        me = 2 * x + y
        mine, other = pl.ds(c * hl, hl), pl.ds((1 - c) * hl, hl)
        sib = (x, y, 1 - c)
        local = [pltpu.make_async_copy(ins[i], outs[i].at[:, me], lsem.at[i]) for i in range(n)]
        for cp in local:
            cp.start()
        first = [[_remote(ins[i].at[mine], outs[i].at[mine, me], send1.at[i, k], recv1.at[i, k], (cx, cy, c))
                  for k, (cx, cy) in enumerate(chips)] for i in range(n)]
        for row in first:
            for cp in row:
                cp.start()
        passed = []
        for i in range(n):
            for k, (cx, cy) in enumerate(chips):
                slot = outs[i].at[mine, 2 * cx + cy]
                _remote(slot, slot, send1.at[i, k], recv1.at[i, k], (cx, cy, c)).wait_recv()
                cp = _remote(slot, slot, send2.at[i, k], recv2.at[i, k], sib)
                cp.start()
                passed.append(cp)
        for i in range(n):
            for k, (cx, cy) in enumerate(chips):
                slot = outs[i].at[other, 2 * cx + cy]
                _remote(slot, slot, send2.at[i, k], recv2.at[i, k], sib).wait_recv()
        for row in first:
            for cp in row:
                cp.wait_send()
        for cp in passed:
            cp.wait_send()
        for cp in local:
            cp.wait()

    out_shape = [jax.ShapeDtypeStruct((s.shape[0], N_CHIPS) + s.shape[1:], s.dtype) for s in shards]
    sems = [pltpu.SemaphoreType.DMA((n, 3)) for _ in range(4)] + [pltpu.SemaphoreType.DMA((n,))]
    return pl.pallas_call(body, out_shape=out_shape, in_specs=[HBM] * n, out_specs=[HBM] * n, scratch_shapes=sems,
                          name="gather_weights")(*shards)


def _pair_exchange(grads):
    n = len(grads)
    hl = grads[0].shape[0] // 2

    def body(*refs):
        ins, outs = refs[:n], refs[n:2 * n]
        send, recv = refs[2 * n:]
        x, y, c, _ = _place()
        other = pl.ds((1 - c) * hl, hl)
        cps = [_remote(ins[i].at[other], outs[i], send.at[i], recv.at[i], (x, y, 1 - c)) for i in range(n)]
        for cp in cps:
            cp.start()
        for cp in cps:
            cp.wait()

    out_shape = [jax.ShapeDtypeStruct((hl,) + g.shape[1:], g.dtype) for g in grads]
    sems = [pltpu.SemaphoreType.DMA((n,)) for _ in range(2)]
    return pl.pallas_call(body, out_shape=out_shape, in_specs=[HBM] * n, out_specs=[HBM] * n, scratch_shapes=sems,
                          name="reduce_pair_exchange")(*grads)


def _pair_add(g, r):
    hl, _, rr, cc = r.shape
    rows = hl * N_CHIPS * rr
    nblk = rows // rr

    def body(c_ref, g_ref, r_ref, o_ref):
        o_ref[...] = (g_ref[...].astype(F32) + r_ref[...].astype(F32)).astype(o_ref.dtype)

    grid_spec = pltpu.PrefetchScalarGridSpec(
        num_scalar_prefetch=1, grid=(nblk,),
        in_specs=[pl.BlockSpec((rr, cc), lambda i, c_ref: (c_ref[0] * nblk + i, 0)),
                  pl.BlockSpec((rr, cc), lambda i, c_ref: (i, 0))],
        out_specs=pl.BlockSpec((rr, cc), lambda i, c_ref: (i, 0)))
    c = lax.axis_index("c").astype(jnp.int32).reshape(1)
    out = pl.pallas_call(body, out_shape=jax.ShapeDtypeStruct((rows, cc), r.dtype), grid_spec=grid_spec,
                         name="reduce_pair_add", compiler_params=_params(("parallel",)))(
        c, g.reshape(-1, cc), r.reshape(rows, cc))
    return out.reshape(r.shape)


def _chip_exchange(psums):
    n = len(psums)

    def body(*refs):
        ins, got, own = refs[:n], refs[n:2 * n], refs[2 * n:3 * n]
        send, recv, lsem = refs[3 * n:]
        x, y, c, chips = _place()
        me = 2 * x + y
        local = [pltpu.make_async_copy(ins[i].at[:, me], own[i], lsem.at[i]) for i in range(n)]
        for cp in local:
            cp.start()
        cps = [_remote(ins[i].at[:, 2 * cx + cy], got[i].at[k], send.at[i, k], recv.at[i, k], (cx, cy, c))
               for i in range(n) for k, (cx, cy) in enumerate(chips)]
        for cp in cps:
            cp.start()
        for cp in cps:
            cp.wait()
        for cp in local:
            cp.wait()

    got_shape = [jax.ShapeDtypeStruct((3, p.shape[0]) + p.shape[2:], p.dtype) for p in psums]
    own_shape = [jax.ShapeDtypeStruct((p.shape[0],) + p.shape[2:], p.dtype) for p in psums]
    sems = [pltpu.SemaphoreType.DMA((n, 3)), pltpu.SemaphoreType.DMA((n, 3)), pltpu.SemaphoreType.DMA((n,))]
    res = pl.pallas_call(body, out_shape=got_shape + own_shape, in_specs=[HBM] * n, out_specs=[HBM] * (2 * n),
                         scratch_shapes=sems, name="reduce_chip_exchange")(*psums)
    return res[:n], res[n:]


def _sum4(own, got):
    hl, rr, cc = own.shape
    rows = hl * rr
    g2 = got.reshape(3 * rows, cc)
    fn = lambda a, b, c, d: ((a.astype(F32) + b.astype(F32)) + c.astype(F32)) + d.astype(F32)
    tm = math.gcd(rows, 256)
    nb = rows // tm
    out = _rowwise("reduce_sum4", fn, [R(own.reshape(rows, cc)), R(g2, rb=0), R(g2, rb=nb), R(g2, rb=2 * nb)],
                   [O(cc, F32)], tm=tm, rows=rows)
    return out.reshape(own.shape)


def _pair_gather(halves):
    n = len(halves)
    hl = halves[0].shape[0]

    def body(*refs):
        ins, outs = refs[:n], refs[n:2 * n]
        send, recv, lsem = refs[2 * n:]
        x, y, c, _ = _place()
        mine = pl.ds(c * hl, hl)
        local = [pltpu.make_async_copy(ins[i], outs[i].at[mine], lsem.at[i]) for i in range(n)]
        for cp in local:
            cp.start()
        cps = [_remote(ins[i], outs[i].at[mine], send.at[i], recv.at[i], (x, y, 1 - c)) for i in range(n)]
        for cp in cps:
            cp.start()
        for cp in cps:
            cp.wait()
        for cp in local:
            cp.wait()

    out_shape = [jax.ShapeDtypeStruct((2 * hl,) + h.shape[1:], h.dtype) for h in halves]
    sems = [pltpu.SemaphoreType.DMA((n,)) for _ in range(3)]
    return pl.pallas_call(body, out_shape=out_shape, in_specs=[HBM] * n, out_specs=[HBM] * n, scratch_shapes=sems,
                          name="reduce_pair_gather")(*halves)


def _allreduce_small(vec):
    rows = vec.shape[0]

    def body(v_ref, o_ref, all_ref, send, recv):
        x, y, c, _ = _place()
        me = 4 * x + 2 * y + c
        all_ref[me] = v_ref[...]
        cps = []
        for dlt in range(1, N_DEV):
            fx, fy, fc = (dlt >> 2) & 1, (dlt >> 1) & 1, dlt & 1
            to = (1 - x if fx else x, 1 - y if fy else y, 1 - c if fc else c)
            cps.append(_remote(v_ref, all_ref.at[me], send.at[dlt - 1], recv.at[dlt - 1], to))
        for cp in cps:
            cp.start()
        for cp in cps:
            cp.wait()
        tot = all_ref[0]
        for dev in range(1, N_DEV):
            tot = tot + all_ref[dev]
        o_ref[...] = tot

    vm = pl.BlockSpec(memory_space=pltpu.VMEM)
    return pl.pallas_call(
        body, out_shape=jax.ShapeDtypeStruct(vec.shape, F32), in_specs=[vm], out_specs=vm,
        scratch_shapes=[pltpu.VMEM((N_DEV, rows, 128), F32), pltpu.SemaphoreType.DMA((N_DEV - 1,)),
                        pltpu.SemaphoreType.DMA((N_DEV - 1,))],
        name="allreduce_small", compiler_params=pltpu.CompilerParams(vmem_limit_bytes=VMEM_LIMIT))(vec)


def _adamw(name, w, g, m, v):
    rows, cc = w.shape
    tm = math.gcd(rows, 256)
    return _rowwise(name, _adamw_fn, [R(w), R(g), R(m), R(v)], [O(cc, F32), O(cc, F32), O(cc, F32)], tm=tm)


def _pack(parts):
    flat = jnp.concatenate([v.reshape(-1).astype(F32) for v in parts])
    pad = (-flat.shape[0]) % (SUBLANES * 128)
    return jnp.pad(flat, (0, pad)).reshape(-1, 128)


def _unpack(packed, shapes):
    flat, out, pos = packed.reshape(-1), [], 0
    for shp in shapes:
        size = math.prod(shp)
        out.append(flat[pos:pos + size].reshape(shp))
        pos += size
    return out


def kernel(x, p, positions, mix_norm_g, w_in, b_gate, attn_sinks, w_attn_out, ssm_lambda_re, ssm_lambda_im, ssm_log_dt, ssm_b_re, ssm_b_im, ssm_c_re, ssm_c_im, ssm_d, w_ssm_glu, b_ssm_glu, conv_dw_w, conv_dw_b, conv_norm_g, conv_norm_b, w_conv_out, w_mix_out, ffn_norm_g, w_ffn_in, w_ffn_out, w_ple_in, ple_norm_g, w_ple_gate, final_norm_g, loss_target, m_mix_norm_g, m_w_in, m_b_gate, m_attn_sinks, m_w_attn_out, m_ssm_lambda_re, m_ssm_lambda_im, m_ssm_log_dt, m_ssm_b_re, m_ssm_b_im, m_ssm_c_re, m_ssm_c_im, m_ssm_d, m_w_ssm_glu, m_b_ssm_glu, m_conv_dw_w, m_conv_dw_b, m_conv_norm_g, m_conv_norm_b, m_w_conv_out, m_w_mix_out, m_ffn_norm_g, m_w_ffn_in, m_w_ffn_out, m_w_ple_in, m_ple_norm_g, m_w_ple_gate, m_final_norm_g, v_mix_norm_g, v_w_in, v_b_gate, v_attn_sinks, v_w_attn_out, v_ssm_lambda_re, v_ssm_lambda_im, v_ssm_log_dt, v_ssm_b_re, v_ssm_b_im, v_ssm_c_re, v_ssm_c_im, v_ssm_d, v_w_ssm_glu, v_b_ssm_glu, v_conv_dw_w, v_conv_dw_b, v_conv_norm_g, v_conv_norm_b, v_w_conv_out, v_w_mix_out, v_ffn_norm_g, v_w_ffn_in, v_w_ffn_out, v_w_ple_in, v_ple_norm_g, v_w_ple_gate, v_final_norm_g):
    given = dict(locals())
    wts = {nm: given[nm] for nm in WEIGHTS}
    mom = {nm: given["m_" + nm] for nm in WEIGHTS}
    var = {nm: given["v_" + nm] for nm in WEIGHTS}
    depth = p.shape[0]
    chip = 2 * lax.axis_index("x") + lax.axis_index("y")

    cw_cols = conv_dw_w.shape[2]
    taps = jnp.pad(conv_dw_w.reshape(depth, -1), ((0, 0), (0, (-CONV_K * cw_cols) % (SUBLANES * 128))))
    gathered = _gather_shards([wts[nm].astype(BF16) for nm in BIG] + [taps.reshape(depth, -1, 128)])
    wfull = dict(zip(BIG, gathered[:-1]))
    taps_all = gathered[-1].reshape(depth, N_CHIPS, -1)[:, :, :CONV_K * cw_cols]
    small = {nm: wts[nm] for nm in SMALL}
    small["conv_dw_w"] = taps_all.reshape(depth, N_CHIPS, CONV_K, cw_cols).transpose(0, 2, 1, 3).reshape(
        depth, CONV_K, N_CHIPS * cw_cols)

    loss_cols, grad_x, gbig, gsmall = _local_step(x, p, positions, loss_target, wfull, small)

    parts = [loss_cols] + [gsmall[nm] for nm in SMALL]
    total = _allreduce_small(_pack(parts))
    summed = _unpack(total, [v.shape for v in parts])
    loss = jnp.sum(summed[0])
    gsum = dict(zip(SMALL, summed[1:]))
    gsum["conv_dw_w"] = lax.dynamic_slice_in_dim(gsum["conv_dw_w"], chip * cw_cols, cw_cols, axis=2)
    shapes = [wts[nm].shape for nm in SMALL]
    deltas, new_m, new_v = _adamw("adamw_small", _pack([wts[nm] for nm in SMALL]), _pack([gsum[nm] for nm in SMALL]),
                                  _pack([mom[nm] for nm in SMALL]), _pack([var[nm] for nm in SMALL]))
    grads = dict(gsum)
    delta = dict(zip(SMALL, _unpack(deltas, shapes)))
    newm = dict(zip(SMALL, _unpack(new_m, shapes)))
    newv = dict(zip(SMALL, _unpack(new_v, shapes)))

    gl = [gbig[nm] for nm in BIG]
    sib = _pair_exchange(gl)
    psums = [_pair_add(g, r) for g, r in zip(gl, sib)]
    got, own = _chip_exchange(psums)
    halves = [_sum4(o, g) for o, g in zip(own, got)]
    for nm, g in zip(BIG, _pair_gather(halves)):
        shp = wts[nm].shape
        two = lambda v: v.reshape(-1, shp[-1])
        g = g.reshape(shp)
        d_w, n_m, n_v = _adamw("adamw_" + nm, two(wts[nm]), two(g), two(mom[nm]), two(var[nm]))
        grads[nm], delta[nm], newm[nm], newv[nm] = g, d_w.reshape(shp), n_m.reshape(shp), n_v.reshape(shp)

    return (loss, grad_x, *[grads[nm] for nm in WEIGHTS], *[delta[nm] for nm in WEIGHTS],
            *[newm[nm] for nm in WEIGHTS], *[newv[nm] for nm in WEIGHTS])
```

```python
import functools
import math

import jax
import jax.numpy as jnp
from jax import lax
from jax.experimental import pallas as pl
from jax.experimental.pallas import tpu as pltpu

F32 = jnp.float32
BF16 = jnp.bfloat16

D_MODEL = 1024
HEAD_DIM = 64
N_Q_HEADS = 8
N_KV_HEADS = 2
GQA_GROUP = N_Q_HEADS // N_KV_HEADS
ATT_BLOCK = 128
ROPE_THETA = 500000.0
ROPE_DIM = HEAD_DIM // 4
ROPE_HALF = ROPE_DIM // 2
Q_WIDTH = N_Q_HEADS * HEAD_DIM
KV_WIDTH = N_KV_HEADS * HEAD_DIM
SSM_WIDTH = 256
SSM_GROUP = 16
SSM_GROUPS = 16
SSM_STATE = 64
SSM_LANES = SSM_GROUPS * SSM_STATE
CONV_WIDTH = 256
CONV_K = 31
CONV_HALO = 32
FFN_HIDDEN = 2816
EPS = 1e-6
NEG_INF = -1e30
SCALE = HEAD_DIM ** -0.5

ADAM_LR = 0.001
ADAM_B1 = 0.9
ADAM_B2 = 0.999
ADAM_EPS = 1e-08
ADAM_WD = 0.01
ADAM_STEP = 10

N_CHIPS = 4
N_DEV = 8
SUBLANES = 8
VMEM_LIMIT = 56 * 1024 * 1024

MESH = pl.DeviceIdType.MESH


def _params(sem=None):
    return pltpu.CompilerParams(dimension_semantics=sem, vmem_limit_bytes=VMEM_LIMIT)


def R(arr, width=None, cb=0, rb=0):
    return ("r", arr, arr.shape[1] if width is None else width, (cb, rb))


def V(arr, width=None, cb=0):
    return ("v", arr, arr.shape[1] if width is None else width, cb)


def _cbf(cb):
    return cb if callable(cb) else (lambda j, c=cb: c + j)


def _rowwise(name, fn, ins, outs, accs=(), *, tm, ncol=1, rows=None):
    t = rows if rows is not None else [a for k, a, _, _ in ins if k == "r"][0].shape[0]
    tm = min(tm, t)
    assert t % tm == 0, (name, t, tm)
    n_i, n_o, n_a = len(ins), len(outs), len(accs)

    def body(*refs):
        vals = fn(*[r[...] for r in refs[:n_i]])
        if not isinstance(vals, (tuple, list)):
            vals = (vals,)
        for ref, val in zip(refs[n_i:n_i + n_o], vals[:n_o]):
            ref[...] = val.astype(ref.dtype)
        if n_a:
            acc_refs = refs[n_i + n_o:]

            @pl.when(pl.program_id(1) == 0)
            def _():
                for ref in acc_refs:
                    ref[...] = jnp.zeros_like(ref)

            for ref, val in zip(acc_refs, vals[n_o:]):
                ref[...] += val

    in_specs = []
    for kind, arr, width, cb in ins:
        if kind == "r":
            f = _cbf(cb[0])
            in_specs.append(pl.BlockSpec((tm, width), functools.partial(lambda j, i, f, rb: (i + rb, f(j)), f=f, rb=cb[1])))
        else:
            f = _cbf(cb)
            in_specs.append(pl.BlockSpec((arr.shape[0], width), functools.partial(lambda j, i, f: (0, f(j)), f=f)))
    out_specs, out_shape = [], []
    for total, width, cb, dt in outs:
        f = _cbf(cb)
        out_specs.append(pl.BlockSpec((tm, width), functools.partial(lambda j, i, f: (i, f(j)), f=f)))
        out_shape.append(jax.ShapeDtypeStruct((t, total), dt))
    for total, width, cb in accs:
        f = _cbf(cb)
        out_specs.append(pl.BlockSpec((1, width), functools.partial(lambda j, i, f: (0, f(j)), f=f)))
        out_shape.append(jax.ShapeDtypeStruct((1, total), F32))
    sem = ("arbitrary", "arbitrary") if n_a else ("parallel", "parallel")
    res = pl.pallas_call(body, out_shape=out_shape, grid=(ncol, t // tm), in_specs=in_specs, out_specs=out_specs,
                         name=name, compiler_params=_params(sem))(*[a for _, a, _, _ in ins])
    return res[0] if len(res) == 1 else res


def O(width, dtype, total=None, cb=0):
    return (width if total is None else total, width, cb, dtype)


def A(width, total=None, cb=0):
    return (width if total is None else total, width, cb)


_DIMS = {"nn": (((1,), (0,)), ((), ())), "nt": (((1,), (1,)), ((), ())), "tn": (((0,), (0,)), ((), ()))}


def _mm(name, a, b, mode, out_dtype, *, m, n, k, tm, tn, tk, a_off=0, b_off=0, res=None, bias=None, b_sh=None, o_sh=None):
    tm, tn, tk = min(tm, m), min(tn, n), min(tk, k)
    assert m % tm == 0 and n % tn == 0 and k % tk == 0, (name, m, n, k, tm, tn, tk)
    nk = k // tk
    has_res, has_bias = res is not None, bias is not None
    a_bytes, b_bytes = m * k * a.dtype.itemsize, n * k * b.dtype.itemsize
    swap = nk == 1 and b_bytes + (n // tn) * a_bytes < a_bytes + (m // tm) * b_bytes

    def body(*refs):
        a_ref, b_ref = refs[0], refs[1]
        pos = 2
        res_ref = bias_ref = None
        if has_res:
            res_ref = refs[pos]
            pos += 1
        if has_bias:
            bias_ref = refs[pos]
            pos += 1
        o_ref = refs[pos]

        def finish(r):
            if has_bias:
                r = r + bias_ref[...]
            if has_res:
                r = r + res_ref[...].astype(F32)
            o_ref[...] = r.astype(o_ref.dtype)

        part = lax.dot_general(a_ref[...].astype(BF16), b_ref[...].astype(BF16), _DIMS[mode],
                               preferred_element_type=F32)
        if nk == 1:
            finish(part)
            return
        acc_ref = refs[pos + 1]
        kk = pl.program_id(2)

        @pl.when(kk == 0)
        def _():
            acc_ref[...] = part

        @pl.when(kk > 0)
        def _():
            acc_ref[...] += part

        @pl.when(kk == nk - 1)
        def _():
            finish(acc_ref[...])

    def at(f):
        return (lambda g0, g1, kk: f(g1, g0, kk)) if swap else f

    if mode == "nn":
        a_spec = pl.BlockSpec((tm, tk), at(lambda i, j, kk: (i, kk + a_off)))
        b_spec = pl.BlockSpec((tk, tn), at(lambda i, j, kk: (kk, j + b_off)))
        if b_sh is not None:
            assert b_sh % tn == 0, (name, b_sh, tn)
            per = b_sh // tn
            b_spec = pl.BlockSpec((None, tk, tn), at(lambda i, j, kk: (j // per, kk, j % per)))
    elif mode == "nt":
        a_spec = pl.BlockSpec((tm, tk), at(lambda i, j, kk: (i, kk + a_off)))
        b_spec = pl.BlockSpec((tn, tk), at(lambda i, j, kk: (j, kk + b_off)))
        if b_sh is not None:
            assert b_sh % tk == 0, (name, b_sh, tk)
            per = b_sh // tk
            b_spec = pl.BlockSpec((None, tn, tk), at(lambda i, j, kk: (kk // per, j, kk % per)))
    else:
        a_spec = pl.BlockSpec((tk, tm), at(lambda i, j, kk: (kk, i + a_off)))
        b_spec = pl.BlockSpec((tk, tn), at(lambda i, j, kk: (kk, j + b_off)))
    in_specs, args = [a_spec, b_spec], [a, b]
    if has_res:
        in_specs.append(pl.BlockSpec((tm, tn), at(lambda i, j, kk: (i, j))))
        args.append(res)
    if has_bias:
        in_specs.append(pl.BlockSpec((1, tn), at(lambda i, j, kk: (0, j))))
        args.append(bias)
    out_spec, out_shape = pl.BlockSpec((tm, tn), at(lambda i, j, kk: (i, j))), (m, n)
    if o_sh is not None:
        assert o_sh % tn == 0, (name, o_sh, tn)
        per_o = o_sh // tn
        out_spec = pl.BlockSpec((None, tm, tn), at(lambda i, j, kk: (j // per_o, i, j % per_o)))
        out_shape = (n // o_sh, m, o_sh)
    grid = (n // tn, m // tm, nk) if swap else (m // tm, n // tn, nk)
    return pl.pallas_call(
        body, out_shape=jax.ShapeDtypeStruct(out_shape, out_dtype), grid=grid,
        in_specs=in_specs, out_specs=out_spec,
        scratch_shapes=[pltpu.VMEM((tm, tn), F32)] if nk > 1 else [], name=name,
        compiler_params=_params(("parallel", "parallel", "arbitrary")))(*args)


def _sig(v):
    return jax.nn.sigmoid(v)


def _rms_fwd(x, g):
    r = lax.rsqrt(jnp.mean(x * x, axis=-1, keepdims=True) + EPS)
    return x * r * g


def _rms_bwd(dh, x, dres, g):
    dh = dh.astype(F32)
    r = lax.rsqrt(jnp.mean(x * x, axis=-1, keepdims=True) + EPS)
    xh = x * r
    dxh = dh * g
    dx = r * (dxh - xh * jnp.mean(dxh * xh, axis=-1, keepdims=True))
    return dres + dx, jnp.sum(dh * xh, axis=0, keepdims=True)


def _rope_apply(t, c, sa, sb):
    w = t.shape[1]
    return t * c + pltpu.roll(t, w - ROPE_HALF, 1) * sa + pltpu.roll(t, ROPE_HALF, 1) * sb


def _rope_transpose(g, c, sa, sb):
    w = g.shape[1]
    return g * c + pltpu.roll(g * sa, ROPE_HALF, 1) + pltpu.roll(g * sb, w - ROPE_HALF, 1)


def _tile_lanes(tab, reps):
    return jnp.concatenate([tab] * reps, axis=1) if reps > 1 else tab


def _rope_fwd(q, k, c, sa, sb):
    rq = Q_WIDTH // c.shape[1]
    qr = _rope_apply(q.astype(F32), _tile_lanes(c, rq), _tile_lanes(sa, rq), _tile_lanes(sb, rq))
    kr = _rope_apply(k.astype(F32), c, sa, sb)
    return qr, kr


def _rope_bwd_q(g, c, sa, sb):
    rq = Q_WIDTH // c.shape[1]
    return _rope_transpose(g.astype(F32), _tile_lanes(c, rq), _tile_lanes(sa, rq), _tile_lanes(sb, rq))


def _gelu(v):
    return jax.nn.gelu(v, approximate=True)


def _gelu_grad(v):
    c0 = math.sqrt(2.0 / math.pi)
    inner = c0 * (v + 0.044715 * v * v * v)
    th = jnp.tanh(inner)
    return 0.5 * (1.0 + th) + 0.5 * v * (1.0 - th * th) * c0 * (1.0 + 3 * 0.044715 * v * v)


def _merge_fwd(g0, g1, g2, b0, b1, b2, ya, ga, gb, yc):
    s0 = _sig(g0.astype(F32) + b0)
    s1 = _sig(g1.astype(F32) + b1)
    s2 = _sig(g2.astype(F32) + b2)
    ys = ga.astype(F32) * _sig(gb.astype(F32))
    return s0 * ya.astype(F32) + s1 * ys + s2 * yc.astype(F32)


def _merge_bwd(dm, g0, g1, g2, b0, b1, b2, ya, ga, gb, yc):
    dm = dm.astype(F32)
    s0 = _sig(g0.astype(F32) + b0)
    s1 = _sig(g1.astype(F32) + b1)
    s2 = _sig(g2.astype(F32) + b2)
    ga = ga.astype(F32)
    sb = _sig(gb.astype(F32))
    ys = ga * sb
    dya = dm * s0
    dys = dm * s1
    dyc = dm * s2
    dga = dys * sb
    dgb = dys * ga * sb * (1.0 - sb)
    d0 = dm * ya.astype(F32) * s0 * (1.0 - s0)
    d1 = dm * ys * s1 * (1.0 - s1)
    d2 = dm * yc.astype(F32) * s2 * (1.0 - s2)
    cs = lambda v: jnp.sum(v, axis=0, keepdims=True)
    return dya, dga, dgb, dyc, d0, d1, d2, cs(d0), cs(d1), cs(d2), cs(dga), cs(dgb)


def _ffn_act(fg, fu):
    fg = fg.astype(F32)
    return fg * _sig(fg) * fu.astype(F32)


def _ffn_act_bwd(da, fg, fu):
    da, fg, fu = da.astype(F32), fg.astype(F32), fu.astype(F32)
    s = _sig(fg)
    return da * fu * (s * (1.0 + fg * (1.0 - s))), da * fg * s


def _ple_fwd(x, gp, e):
    return x + _sig(gp.astype(F32)) * e.astype(F32)


def _ple_bwd(dx, gp, e):
    s = _sig(gp.astype(F32))
    e = e.astype(F32)
    return dx * s, dx * e * s * (1.0 - s)


def _loss_fn(x, tgt, g):
    d = x.shape[1]
    r = lax.rsqrt(jnp.mean(x * x, axis=-1, keepdims=True) + EPS)
    xh = x * r
    err = xh * g - tgt
    dy = err * (1.0 / d)
    dxh = dy * g
    dx = r * (dxh - xh * jnp.mean(dxh * xh, axis=-1, keepdims=True))
    return dx, jnp.sum(err * err, axis=0, keepdims=True) * (0.5 / d), jnp.sum(dy * xh, axis=0, keepdims=True)


def _adamw_fn(w, g, m, v):
    m = ADAM_B1 * m + (1.0 - ADAM_B1) * g
    v = ADAM_B2 * v + (1.0 - ADAM_B2) * (g * g)
    m_hat = m / (1.0 - ADAM_B1 ** ADAM_STEP)
    v_hat = v / (1.0 - ADAM_B2 ** ADAM_STEP)
    delta = -ADAM_LR * (m_hat / (jnp.sqrt(v_hat) + ADAM_EPS) + ADAM_WD * w)
    return delta, m, v


def _band_mask(n):
    qi = lax.broadcasted_iota(jnp.int32, (ATT_BLOCK, 2 * ATT_BLOCK), 0)
    kj = lax.broadcasted_iota(jnp.int32, (ATT_BLOCK, 2 * ATT_BLOCK), 1)
    dist = qi + ATT_BLOCK - kj
    return (dist >= 0) & (dist < ATT_BLOCK) & ((n > 0) | (kj >= ATT_BLOCK))


def _att_specs(nb):
    cur = lambda b, n: (0, b * nb + n, 0)
    prev = lambda b, n: (0, b * nb + jnp.maximum(n - 1, 0), 0)
    qs = pl.BlockSpec((N_Q_HEADS, ATT_BLOCK, HEAD_DIM), cur)
    kc = pl.BlockSpec((N_KV_HEADS, ATT_BLOCK, HEAD_DIM), cur)
    kp = pl.BlockSpec((N_KV_HEADS, ATT_BLOCK, HEAD_DIM), prev)
    stat = pl.BlockSpec((N_Q_HEADS, ATT_BLOCK, 1), cur)
    sink = pl.BlockSpec((N_Q_HEADS, 1, 1), lambda b, n: (0, 0, 0))
    return qs, kc, kp, stat, sink


def _attn_fwd(qh, kh, vh, sinks, nbatch, seq):
    t = qh.shape[1]
    nb = seq // ATT_BLOCK
    qs, kc, kp, stat, sink = _att_specs(nb)

    def body(q_ref, kp_ref, kc_ref, vp_ref, vc_ref, sink_ref, o_ref, lse_ref):
        mask = _band_mask(pl.program_id(1))
        for h in range(N_Q_HEADS):
            kv = h // GQA_GROUP
            kk = jnp.concatenate([kp_ref[kv], kc_ref[kv]], axis=0)
            vv = jnp.concatenate([vp_ref[kv], vc_ref[kv]], axis=0)
            s = lax.dot_general(q_ref[h] * SCALE, kk, _DIMS["nt"], preferred_element_type=F32)
            s = jnp.where(mask, s, NEG_INF)
            sk = sink_ref[h]
            mx = jnp.maximum(jnp.max(s, axis=-1, keepdims=True), sk)
            p = jnp.exp(s - mx)
            den = jnp.sum(p, axis=-1, keepdims=True) + jnp.exp(sk - mx)
            o = lax.dot_general(p.astype(BF16), vv, _DIMS["nn"], preferred_element_type=F32)
            o_ref[h] = (o * (1.0 / den)).astype(o_ref.dtype)
            lse_ref[h] = mx + jnp.log(den)

    return pl.pallas_call(
        body, grid=(nbatch, nb), in_specs=[qs, kp, kc, kp, kc, sink], out_specs=[qs, stat],
        out_shape=[jax.ShapeDtypeStruct((N_Q_HEADS, t, HEAD_DIM), BF16), jax.ShapeDtypeStruct((N_Q_HEADS, t, 1), F32)],
        name="attn_fwd", compiler_params=_params(("parallel", "parallel")))(qh, kh, kh, vh, vh, sinks)


def _attn_bwd(qh, kh, vh, oh, doh, lse, sinks, nbatch, seq):
    t = qh.shape[1]
    nb = seq // ATT_BLOCK
    qs, kc, kp, stat, sink = _att_specs(nb)

    def body(q_ref, kp_ref, kc_ref, vp_ref, vc_ref, o_ref, do_ref, lse_ref, sink_ref,
             dq_ref, dkc_ref, dvc_ref, dkp_ref, dvp_ref, dsink_ref):
        first = (pl.program_id(0) == 0) & (pl.program_id(1) == 0)

        @pl.when(first)
        def _():
            dsink_ref[...] = jnp.zeros_like(dsink_ref)

        mask = _band_mask(pl.program_id(1))
        for kv in range(N_KV_HEADS):
            kk = jnp.concatenate([kp_ref[kv], kc_ref[kv]], axis=0)
            vv = jnp.concatenate([vp_ref[kv], vc_ref[kv]], axis=0)
            dk = jnp.zeros((2 * ATT_BLOCK, HEAD_DIM), F32)
            dv = jnp.zeros((2 * ATT_BLOCK, HEAD_DIM), F32)
            for h in range(kv * GQA_GROUP, (kv + 1) * GQA_GROUP):
                q = q_ref[h]
                do = do_ref[h]
                lse_h = lse_ref[h]
                s = lax.dot_general(q * SCALE, kk, _DIMS["nt"], preferred_element_type=F32)
                p = jnp.where(mask, jnp.exp(s - lse_h), 0.0)
                dd = jnp.sum(do.astype(F32) * o_ref[h].astype(F32), axis=-1, keepdims=True)
                dp = lax.dot_general(do, vv, _DIMS["nt"], preferred_element_type=F32)
                ds = (p * (dp - dd) * SCALE).astype(BF16)
                dq_ref[h] = lax.dot_general(ds, kk, _DIMS["nn"], preferred_element_type=F32).astype(dq_ref.dtype)
                dk = dk + lax.dot_general(ds, q, _DIMS["tn"], preferred_element_type=F32)
                dv = dv + lax.dot_general(p.astype(BF16), do, _DIMS["tn"], preferred_element_type=F32)
                dsk = -jnp.sum(jnp.exp(sink_ref[h] - lse_h) * dd, axis=0, keepdims=True)
                dsink_ref[h] += dsk
            dkp_ref[kv] = dk[:ATT_BLOCK]
            dkc_ref[kv] = dk[ATT_BLOCK:]
            dvp_ref[kv] = dv[:ATT_BLOCK]
            dvc_ref[kv] = dv[ATT_BLOCK:]

    kvs = jax.ShapeDtypeStruct((N_KV_HEADS, t, HEAD_DIM), F32)
    return pl.pallas_call(
        body, grid=(nbatch, nb), in_specs=[qs, kp, kc, kp, kc, qs, qs, stat, sink],
        out_specs=[qs, kc, kc, kc, kc, sink],
        out_shape=[jax.ShapeDtypeStruct((N_Q_HEADS, t, HEAD_DIM), F32), kvs, kvs, kvs, kvs,
                   jax.ShapeDtypeStruct((N_Q_HEADS, 1, 1), F32)],
        name="attn_bwd", compiler_params=_params(("arbitrary", "arbitrary")))(qh, kh, kh, vh, vh, oh, doh, lse, sinks)


def _kv_combine(dkc, dkp, dvc, dvp, c, sa, sb, seq):
    t = dkc.shape[0]
    nb = seq // ATT_BLOCK
    nblk = t // ATT_BLOCK

    def body(kc_ref, kp_ref, vc_ref, vp_ref, c_ref, sa_ref, sb_ref, dk_ref, dv_ref):
        has_next = (pl.program_id(0) % nb) != nb - 1
        dk = kc_ref[...] + jnp.where(has_next, kp_ref[...], 0.0)
        dv = vc_ref[...] + jnp.where(has_next, vp_ref[...], 0.0)
        dk_ref[...] = _rope_transpose(dk, c_ref[...], sa_ref[...], sb_ref[...]).astype(dk_ref.dtype)
        dv_ref[...] = dv.astype(dv_ref.dtype)

    cur = pl.BlockSpec((ATT_BLOCK, KV_WIDTH), lambda i: (i, 0))
    nxt = pl.BlockSpec((ATT_BLOCK, KV_WIDTH), lambda i: (jnp.minimum(i + 1, nblk - 1), 0))
    o = jax.ShapeDtypeStruct((t, KV_WIDTH), BF16)
    return pl.pallas_call(body, grid=(nblk,), in_specs=[cur, nxt, cur, nxt, cur, cur, cur], out_specs=[cur, cur],
                          out_shape=[o, o], name="kv_combine", compiler_params=_params(("parallel",)))(
        dkc, dkp, dvc, dvp, c, sa, sb)


def _scan_block(ref, tab_ref, carry, ngroups, reverse):
    shifts = (7, 6, 4) if reverse else (1, 2, 4)
    n = SSM_LANES

    def step(i, car):
        g = (ngroups - 1 - i) if reverse else i
        r0 = pl.multiple_of(g * SUBLANES, SUBLANES)
        xr = ref[pl.ds(r0, SUBLANES), :n]
        xi = ref[pl.ds(r0, SUBLANES), n:]
        for s, sh in enumerate(shifts):
            pr, pi = tab_ref[2 * s], tab_ref[2 * s + 1]
            yr, yi = pltpu.roll(xr, sh, 0), pltpu.roll(xi, sh, 0)
            xr, xi = xr + pr * yr - pi * yi, xi + pr * yi + pi * yr
        cr, ci = car
        qr, qi = tab_ref[6], tab_ref[7]
        xr, xi = xr + qr * cr - qi * ci, xi + qr * ci + qi * cr
        ref[pl.ds(r0, SUBLANES), :n] = xr
        ref[pl.ds(r0, SUBLANES), n:] = xi
        last = r0 if reverse else r0 + SUBLANES - 1
        return ref[pl.ds(last, 1), :n], ref[pl.ds(last, 1), n:]

    return lax.fori_loop(0, ngroups, step, carry)


def _ssm_chunk(seq):
    return min(256, seq)


def _ssm_fwd(z, wb, wc, tab, dskip, nbatch, seq):
    t = z.shape[0]
    tc = _ssm_chunk(seq)
    nc = seq // tc
    n2 = 2 * SSM_LANES

    def body(u_ref, wb_ref, wc_ref, tab_ref, d_ref, st_ref, y_ref, gel_ref, car_ref):
        @pl.when(pl.program_id(1) == 0)
        def _():
            car_ref[...] = jnp.zeros_like(car_ref)

        u = u_ref[...]
        st_ref[...] = lax.dot_general(u, wb_ref[...], _DIMS["nn"], preferred_element_type=F32)
        cr, ci = _scan_block(st_ref, tab_ref, (car_ref[:, :SSM_LANES], car_ref[:, SSM_LANES:]), tc // SUBLANES, False)
        car_ref[:, :SSM_LANES] = cr
        car_ref[:, SSM_LANES:] = ci
        y = lax.dot_general(st_ref[...].astype(BF16), wc_ref[...], _DIMS["nn"], preferred_element_type=F32)
        y = y + d_ref[...] * u.astype(F32)
        y_ref[...] = y
        gel_ref[...] = _gelu(y).astype(gel_ref.dtype)

    row = lambda b, c: (b * nc + c, 0)
    full = lambda b, c: (0, 0)
    return pl.pallas_call(
        body, grid=(nbatch, nc),
        in_specs=[pl.BlockSpec((tc, SSM_WIDTH), lambda b, c: (b * nc + c, 3)), pl.BlockSpec((SSM_WIDTH, n2), full),
                  pl.BlockSpec((n2, SSM_WIDTH), full), pl.BlockSpec((8, SUBLANES, SSM_LANES), lambda b, c: (0, 0, 0)),
                  pl.BlockSpec((1, SSM_WIDTH), full)],
        out_specs=[pl.BlockSpec((tc, n2), row), pl.BlockSpec((tc, SSM_WIDTH), row), pl.BlockSpec((tc, SSM_WIDTH), row)],
        out_shape=[jax.ShapeDtypeStruct((t, n2), F32), jax.ShapeDtypeStruct((t, SSM_WIDTH), F32),
                   jax.ShapeDtypeStruct((t, SSM_WIDTH), BF16)],
        scratch_shapes=[pltpu.VMEM((1, n2), F32)], name="ssm_fwd",
        compiler_params=_params(("arbitrary", "arbitrary")))(z, wb, wc, tab, dskip)


def _ssm_bwd(dgi, ys, st, z, wbt, wct, tab_rev, dskip, nbatch, seq):
    t = z.shape[0]
    tc = _ssm_chunk(seq)
    nc = seq // tc
    n = SSM_LANES
    n2 = 2 * n
    ng = tc // SUBLANES

    def body(dgi_ref, ys_ref, st_ref, stp_ref, u_ref, wbt_ref, wct_ref, tab_ref, d_ref,
             du_ref, dwb_ref, dwc_ref, dd_ref, da_ref, p_ref, sb_ref, car_ref):
        b, c = pl.program_id(0), pl.program_id(1)
        ct = nc - 1 - c

        @pl.when((b == 0) & (c == 0))
        def _():
            dwb_ref[...] = jnp.zeros_like(dwb_ref)
            dwc_ref[...] = jnp.zeros_like(dwc_ref)
            dd_ref[...] = jnp.zeros_like(dd_ref)
            da_ref[...] = jnp.zeros_like(da_ref)

        @pl.when(c == 0)
        def _():
            car_ref[...] = jnp.zeros_like(car_ref)

        u = u_ref[...]
        dys = dgi_ref[...].astype(F32) * _gelu_grad(ys_ref[...])
        dys_b = dys.astype(BF16)
        st = st_ref[...]
        dd_ref[...] += jnp.sum(dys * u.astype(F32), axis=0, keepdims=True)
        dwc_ref[...] += lax.dot_general(st.astype(BF16), dys_b, _DIMS["tn"], preferred_element_type=F32)
        p_ref[...] = lax.dot_general(dys_b, wct_ref[...], _DIMS["nn"], preferred_element_type=F32)
        cr, ci = _scan_block(p_ref, tab_ref, (car_ref[:, :n], car_ref[:, n:]), ng, True)
        car_ref[:, :n] = cr
        car_ref[:, n:] = ci
        p = p_ref[...]
        pb = p.astype(BF16)
        dwb_ref[...] += lax.dot_general(u, pb, _DIMS["tn"], preferred_element_type=F32)
        du = lax.dot_general(pb, wbt_ref[...], _DIMS["nn"], preferred_element_type=F32) + d_ref[...] * dys
        du_ref[...] = du.astype(du_ref.dtype)
        sb_ref[pl.ds(0, SUBLANES), :] = jnp.where(ct > 0, stp_ref[...], 0.0)
        sb_ref[pl.ds(SUBLANES, tc), :] = st
        row0 = lax.broadcasted_iota(jnp.int32, (SUBLANES, n), 0) == 0

        def acc_step(g, acc):
            ar, ai = acc
            r0 = pl.multiple_of(g * SUBLANES, SUBLANES)
            edge_r = sb_ref[pl.ds(r0 + SUBLANES - 1, 1), :n]
            edge_i = sb_ref[pl.ds(r0 + SUBLANES - 1, 1), n:]
            sr = jnp.where(row0, edge_r, pltpu.roll(sb_ref[pl.ds(r0 + SUBLANES, SUBLANES), :n], 1, 0))
            si = jnp.where(row0, edge_i, pltpu.roll(sb_ref[pl.ds(r0 + SUBLANES, SUBLANES), n:], 1, 0))
            pr = p_ref[pl.ds(r0, SUBLANES), :n]
            pi = p_ref[pl.ds(r0, SUBLANES), n:]
            return ar + pr * sr + pi * si, ai + pi * sr - pr * si

        zero = jnp.zeros((SUBLANES, n), F32)
        ar, ai = lax.fori_loop(0, ng, acc_step, (zero, zero))
        da_ref[:, :n] += ar
        da_ref[:, n:] += ai

    row = lambda b, c: (b * nc + (nc - 1 - c), 0)
    prev8 = lambda b, c: (jnp.maximum((b * nc + (nc - 1 - c)) * (tc // SUBLANES) - 1, 0), 0)
    full = lambda b, c: (0, 0)
    return pl.pallas_call(
        body, grid=(nbatch, nc),
        in_specs=[pl.BlockSpec((tc, SSM_WIDTH), row), pl.BlockSpec((tc, SSM_WIDTH), row), pl.BlockSpec((tc, n2), row),
                  pl.BlockSpec((SUBLANES, n2), prev8),
                  pl.BlockSpec((tc, SSM_WIDTH), lambda b, c: (b * nc + (nc - 1 - c), 3)),
                  pl.BlockSpec((n2, SSM_WIDTH), full), pl.BlockSpec((SSM_WIDTH, n2), full),
                  pl.BlockSpec((8, SUBLANES, n), lambda b, c: (0, 0, 0)), pl.BlockSpec((1, SSM_WIDTH), full)],
        out_specs=[pl.BlockSpec((tc, SSM_WIDTH), row), pl.BlockSpec((SSM_WIDTH, n2), full),
                   pl.BlockSpec((n2, SSM_WIDTH), full), pl.BlockSpec((1, SSM_WIDTH), full),
                   pl.BlockSpec((SUBLANES, n2), full)],
        out_shape=[jax.ShapeDtypeStruct((t, SSM_WIDTH), BF16), jax.ShapeDtypeStruct((SSM_WIDTH, n2), F32),
                   jax.ShapeDtypeStruct((n2, SSM_WIDTH), F32), jax.ShapeDtypeStruct((1, SSM_WIDTH), F32),
                   jax.ShapeDtypeStruct((SUBLANES, n2), F32)],
        scratch_shapes=[pltpu.VMEM((tc, n2), F32), pltpu.VMEM((tc + SUBLANES, n2), F32), pltpu.VMEM((1, n2), F32)],
        name="ssm_bwd", compiler_params=_params(("arbitrary", "arbitrary")))(
        dgi, ys, st, st, z, wbt, wct, tab_rev, dskip)


def _ssm_prep(lam_re, lam_im, log_dt, b_re, b_im, c_re, c_im):
    lr = jnp.minimum(lam_re, -1e-4)
    li = lam_im
    dt = jnp.exp(log_dt)[:, None]
    mag = jnp.exp(lr * dt)
    a_re = mag * jnp.cos(li * dt)
    a_im = mag * jnp.sin(li * dt)
    den = lr * lr + li * li
    x_re, x_im = a_re - 1.0, a_im
    f_re = (x_re * lr + x_im * li) / den
    f_im = (x_im * lr - x_re * li) / den
    bb_re = f_re[..., None] * b_re - f_im[..., None] * b_im
    bb_im = f_re[..., None] * b_im + f_im[..., None] * b_re
    eye = jnp.eye(SSM_GROUPS, dtype=F32)
    emb_b = lambda v: jnp.einsum("gnh,gk->ghkn", v, eye).reshape(SSM_WIDTH, SSM_LANES)
    emb_c = lambda v: jnp.einsum("ghn,gk->gnkh", v, eye).reshape(SSM_LANES, SSM_WIDTH)
    wb = jnp.concatenate([emb_b(bb_re), emb_b(bb_im)], axis=1)
    wc = jnp.concatenate([emb_c(c_re), -emb_c(c_im)], axis=0)
    return a_re.reshape(-1), a_im.reshape(-1), wb, wc


def _ssm_tables(a_re, a_im, reverse):
    if reverse:
        a_im = -a_im
    pw = [(a_re, a_im)]
    for _ in range(SUBLANES - 1):
        pr, pi = pw[-1]
        pw.append((pr * a_re - pi * a_im, pr * a_im + pi * a_re))
    rows = jnp.arange(SUBLANES)[:, None]
    tabs = []
    for k in (1, 2, 4):
        ok = (rows + k <= SUBLANES - 1) if reverse else (rows >= k)
        tabs += [jnp.where(ok, pw[k - 1][0][None], 0.0), jnp.where(ok, pw[k - 1][1][None], 0.0)]
    order = list(range(SUBLANES - 1, -1, -1)) if reverse else list(range(SUBLANES))
    tabs += [jnp.stack([pw[i][0] for i in order]), jnp.stack([pw[i][1] for i in order])]
    return jnp.stack(tabs)


def _conv_chunk(seq):
    return min(512, seq)


def _conv_fwd(z, w, bias, lg, lb, nbatch, seq):
    t = z.shape[0]
    tc = _conv_chunk(seq)
    nc = seq // tc

    def body(a_ref, g_ref, w_ref, b_ref, lg_ref, lb_ref, cv_ref, sc_ref, ubuf):
        c = pl.program_id(1)

        @pl.when(c == 0)
        def _():
            ubuf[pl.ds(0, CONV_HALO), :] = jnp.zeros((CONV_HALO, CONV_WIDTH), F32)

        @pl.when(c > 0)
        def _():
            ubuf[pl.ds(0, CONV_HALO), :] = ubuf[pl.ds(tc, CONV_HALO), :]

        ubuf[pl.ds(CONV_HALO, tc), :] = a_ref[...].astype(F32) * _sig(g_ref[...].astype(F32))
        acc = jnp.zeros((tc, CONV_WIDTH), F32) + b_ref[...]
        for k in range(CONV_K):
            acc = acc + w_ref[pl.ds(k, 1), :] * ubuf[pl.ds(CONV_HALO - (CONV_K - 1) + k, tc), :]
        cv_ref[...] = acc
        mu = jnp.mean(acc, axis=-1, keepdims=True)
        xc = acc - mu
        y = xc * lax.rsqrt(jnp.mean(xc * xc, axis=-1, keepdims=True) + EPS) * lg_ref[...] + lb_ref[...]
        sc_ref[...] = (y * _sig(y)).astype(sc_ref.dtype)

    row = lambda b, c: (b * nc + c, 0)
    full = lambda b, c: (0, 0)
    vec = pl.BlockSpec((1, CONV_WIDTH), full)
    return pl.pallas_call(
        body, grid=(nbatch, nc),
        in_specs=[pl.BlockSpec((tc, CONV_WIDTH), lambda b, c: (b * nc + c, 4)),
                  pl.BlockSpec((tc, CONV_WIDTH), lambda b, c: (b * nc + c, 5)),
                  pl.BlockSpec((CONV_HALO, CONV_WIDTH), full), vec, vec, vec],
        out_specs=[pl.BlockSpec((tc, CONV_WIDTH), row), pl.BlockSpec((tc, CONV_WIDTH), row)],
        out_shape=[jax.ShapeDtypeStruct((t, CONV_WIDTH), F32), jax.ShapeDtypeStruct((t, CONV_WIDTH), BF16)],
        scratch_shapes=[pltpu.VMEM((CONV_HALO + tc, CONV_WIDTH), F32)], name="conv_fwd",
        compiler_params=_params(("arbitrary", "arbitrary")))(z, z, w, bias, lg, lb)


def _conv_bwd(dsc, cv, z, w, lg, lb, nbatch, seq):
    t = z.shape[0]
    tc = _conv_chunk(seq)
    nc = seq // tc
    hb = tc // CONV_HALO

    def body(dsc_ref, cv_ref, a_ref, g_ref, ap_ref, gp_ref, w_ref, lg_ref, lb_ref,
             da_ref, dg_ref, dw_ref, db_ref, dlg_ref, dlb_ref, ubuf, dbuf):
        b, c = pl.program_id(0), pl.program_id(1)
        ct = nc - 1 - c

        @pl.when((b == 0) & (c == 0))
        def _():
            dw_ref[...] = jnp.zeros_like(dw_ref)
            db_ref[...] = jnp.zeros_like(db_ref)
            dlg_ref[...] = jnp.zeros_like(dlg_ref)
            dlb_ref[...] = jnp.zeros_like(dlb_ref)

        cvv = cv_ref[...]
        mu = jnp.mean(cvv, axis=-1, keepdims=True)
        xc = cvv - mu
        rstd = lax.rsqrt(jnp.mean(xc * xc, axis=-1, keepdims=True) + EPS)
        xh = xc * rstd
        y = xh * lg_ref[...] + lb_ref[...]
        sy = _sig(y)
        dy = dsc_ref[...].astype(F32) * (sy * (1.0 + y * (1.0 - sy)))
        dlg_ref[...] += jnp.sum(dy * xh, axis=0, keepdims=True)
        dlb_ref[...] += jnp.sum(dy, axis=0, keepdims=True)
        dxh = dy * lg_ref[...]
        dcv = rstd * (dxh - jnp.mean(dxh, axis=-1, keepdims=True) - xh * jnp.mean(dxh * xh, axis=-1, keepdims=True))
        db_ref[...] += jnp.sum(dcv, axis=0, keepdims=True)

        @pl.when(c == 0)
        def _():
            dbuf[pl.ds(tc, CONV_HALO), :] = jnp.zeros((CONV_HALO, CONV_WIDTH), F32)

        @pl.when(c > 0)
        def _():
            dbuf[pl.ds(tc, CONV_HALO), :] = dbuf[pl.ds(0, CONV_HALO), :]

        dbuf[pl.ds(0, tc), :] = dcv
        a = a_ref[...].astype(F32)
        sg = _sig(g_ref[...].astype(F32))
        ubuf[pl.ds(0, CONV_HALO), :] = jnp.where(ct > 0, ap_ref[...].astype(F32) * _sig(gp_ref[...].astype(F32)), 0.0)
        ubuf[pl.ds(CONV_HALO, tc), :] = a * sg
        du = jnp.zeros((tc, CONV_WIDTH), F32)
        for k in range(CONV_K):
            du = du + w_ref[pl.ds(k, 1), :] * dbuf[pl.ds(CONV_K - 1 - k, tc), :]
            dw_ref[pl.ds(k, 1), :] += jnp.sum(dcv * ubuf[pl.ds(CONV_HALO - (CONV_K - 1) + k, tc), :],
                                             axis=0, keepdims=True)
        da_ref[...] = (du * sg).astype(da_ref.dtype)
        dg_ref[...] = (du * a * sg * (1.0 - sg)).astype(dg_ref.dtype)

    row = lambda b, c: (b * nc + (nc - 1 - c), 0)
    full = lambda b, c: (0, 0)
    vec = pl.BlockSpec((1, CONV_WIDTH), full)
    blk = pl.BlockSpec((tc, CONV_WIDTH), row)

    def zcol(col):
        return pl.BlockSpec((tc, CONV_WIDTH), lambda b, c: (b * nc + (nc - 1 - c), col))

    def zprev(col):
        return pl.BlockSpec((CONV_HALO, CONV_WIDTH),
                            lambda b, c: (jnp.maximum((b * nc + (nc - 1 - c)) * hb - 1, 0), col))

    o = jax.ShapeDtypeStruct((t, CONV_WIDTH), BF16)
    v = jax.ShapeDtypeStruct((1, CONV_WIDTH), F32)
    return pl.pallas_call(
        body, grid=(nbatch, nc),
        in_specs=[blk, blk, zcol(4), zcol(5), zprev(4), zprev(5), pl.BlockSpec((CONV_HALO, CONV_WIDTH), full), vec, vec],
        out_specs=[blk, blk, pl.BlockSpec((CONV_HALO, CONV_WIDTH), full), vec, vec, vec],
        out_shape=[o, o, jax.ShapeDtypeStruct((CONV_HALO, CONV_WIDTH), F32), v, v, v],
        scratch_shapes=[pltpu.VMEM((CONV_HALO + tc, CONV_WIDTH), F32), pltpu.VMEM((tc + CONV_HALO, CONV_WIDTH), F32)],
        name="conv_bwd", compiler_params=_params(("arbitrary", "arbitrary")))(dsc, cv, z, z, z, z, w, lg, lb)


BIG = ("w_in", "w_attn_out", "w_ssm_glu", "w_conv_out", "w_mix_out", "w_ffn_in", "w_ffn_out", "w_ple_in", "w_ple_gate")
BIG_AXIS = {"w_in": 2, "w_attn_out": 2, "w_ssm_glu": 2, "w_conv_out": 2, "w_mix_out": 1, "w_ffn_in": 2,
            "w_ffn_out": 1, "w_ple_in": 2, "w_ple_gate": 1}
SHARD_MAJOR = ("w_in", "w_ffn_in")
SMALL = ("mix_norm_g", "b_gate", "attn_sinks", "ssm_lambda_re", "ssm_lambda_im", "ssm_log_dt", "ssm_b_re", "ssm_b_im",
         "ssm_c_re", "ssm_c_im", "ssm_d", "b_ssm_glu", "conv_dw_w", "conv_dw_b", "conv_norm_g", "conv_norm_b",
         "ffn_norm_g", "ple_norm_g", "final_norm_g")
WEIGHTS = ("mix_norm_g", "w_in", "b_gate", "attn_sinks", "w_attn_out", "ssm_lambda_re", "ssm_lambda_im", "ssm_log_dt",
           "ssm_b_re", "ssm_b_im", "ssm_c_re", "ssm_c_im", "ssm_d", "w_ssm_glu", "b_ssm_glu", "conv_dw_w", "conv_dw_b",
           "conv_norm_g", "conv_norm_b", "w_conv_out", "w_mix_out", "ffn_norm_g", "w_ffn_in", "w_ffn_out", "w_ple_in",
           "ple_norm_g", "w_ple_gate", "final_norm_g")
SSM_NAMES = ("ssm_lambda_re", "ssm_lambda_im", "ssm_log_dt", "ssm_b_re", "ssm_b_im", "ssm_c_re", "ssm_c_im")


def _heads(v, nh):
    return v.reshape(v.shape[0], nh, HEAD_DIM).transpose(1, 0, 2)


def _tokens(v):
    return v.transpose(1, 0, 2).reshape(v.shape[1], v.shape[0] * HEAD_DIM)


def _row(v):
    return v.reshape(1, -1)


def _layer_fwd(x, p_l, w, s, rope, nbatch, seq):
    t = x.shape[0]
    tm = 512
    d = D_MODEL
    sv = {}
    sv["x"] = x
    h = _rowwise("rms_mix", _rms_fwd, [R(x), V(_row(s["mix_norm_g"]))], [O(d, BF16)], tm=tm)
    cs = {nm: w[nm].shape[2] for nm in SHARD_MAJOR}
    tb = 1024
    z = _mm("mm_in", h, w["w_in"], "nn", BF16, m=t, n=N_CHIPS * cs["w_in"], k=d, tm=tb, tn=cs["w_in"], tk=d,
            b_sh=cs["w_in"])
    sv["h"], sv["z"] = h, z
    c, sa, sb = rope
    qr, kr = _rowwise("rope_fwd", _rope_fwd, [R(z, Q_WIDTH, 0), R(z, KV_WIDTH, 4), R(c), R(sa), R(sb)],
                      [O(Q_WIDTH, BF16), O(KV_WIDTH, BF16)], tm=tm)
    qh, kh = _heads(qr, N_Q_HEADS), _heads(kr, N_KV_HEADS)
    vh = _heads(z[:, Q_WIDTH + KV_WIDTH:Q_WIDTH + 2 * KV_WIDTH], N_KV_HEADS)
    sinks = s["attn_sinks"].reshape(N_Q_HEADS, 1, 1)
    oh, lse = _attn_fwd(qh, kh, vh, sinks, nbatch, seq)
    o = _tokens(oh)
    ya = _mm("mm_attn_out", o, w["w_attn_out"], "nn", BF16, m=t, n=d, k=Q_WIDTH, tm=tb, tn=d, tk=Q_WIDTH)
    sv.update(qh=qh, kh=kh, vh=vh, oh=oh, lse=lse, o=o, ya=ya, sinks=sinks)
    ssm_args = [s[nm] for nm in SSM_NAMES]
    a_re, a_im, wb, wc = _ssm_prep(*ssm_args)
    dskip = _row(s["ssm_d"])
    st, ys, gel = _ssm_fwd(z, wb.astype(BF16), wc.astype(BF16), _ssm_tables(a_re, a_im, False), dskip, nbatch, seq)
    glu = _mm("mm_glu", gel, w["w_ssm_glu"], "nn", BF16, m=t, n=2 * d, k=SSM_WIDTH, tm=tb, tn=2 * d, tk=SSM_WIDTH,
              bias=_row(s["b_ssm_glu"]))
    sv.update(st=st, ys=ys, gel=gel, glu=glu, a=(a_re, a_im), wb=wb, wc=wc, dskip=dskip)
    cw = jnp.pad(s["conv_dw_w"], ((0, CONV_HALO - CONV_K), (0, 0)))
    cv, sc = _conv_fwd(z, cw, _row(s["conv_dw_b"]), _row(s["conv_norm_g"]), _row(s["conv_norm_b"]), nbatch, seq)
    yc = _mm("mm_conv_out", sc, w["w_conv_out"], "nn", BF16, m=t, n=d, k=CONV_WIDTH, tm=tb, tn=d, tk=CONV_WIDTH)
    sv.update(cw=cw, cv=cv, sc=sc, yc=yc)
    bg = _row(s["b_gate"])
    merge_ins = [R(z, 512, 3), R(z, 512, 5), R(z, 512, 7), V(bg, 512, 0), V(bg, 512, 2), V(bg, 512, 4),
                 R(ya, 512, 0), R(glu, 512, 0), R(glu, 512, 2), R(yc, 512, 0)]
    merged = _rowwise("merge_fwd", _merge_fwd, merge_ins, [O(512, BF16, total=d)], tm=tm, ncol=2)
    x1 = _mm("mm_mix", merged, w["w_mix_out"], "nn", F32, m=t, n=d, k=d, tm=tb, tn=d, tk=d, res=x)
    sv.update(merged=merged, x1=x1)
    hf = _rowwise("rms_ffn", _rms_fwd, [R(x1), V(_row(s["ffn_norm_g"]))], [O(d, BF16)], tm=tm)
    f = _mm("mm_ffn_in", hf, w["w_ffn_in"], "nn", BF16, m=t, n=2 * FFN_HIDDEN, k=d, tm=tb, tn=cs["w_ffn_in"], tk=d,
            b_sh=cs["w_ffn_in"])
    fw = FFN_HIDDEN // 2
    act = _rowwise("ffn_act", _ffn_act, [R(f, fw, 0), R(f, fw, 2)], [O(fw, BF16, total=FFN_HIDDEN)], tm=tm, ncol=2)
    x2 = _mm("mm_ffn_out", act, w["w_ffn_out"], "nn", F32, m=t, n=d, k=FFN_HIDDEN, tm=512, tn=d, tk=FFN_HIDDEN, res=x1)
    sv.update(hf=hf, f=f, act=act, x2=x2)
    e = _mm("mm_ple_in", p_l, w["w_ple_in"], "nn", BF16, m=t, n=d, k=p_l.shape[1], tm=tb, tn=d, tk=p_l.shape[1])
    hp = _rowwise("rms_ple", _rms_fwd, [R(x2), V(_row(s["ple_norm_g"]))], [O(d, BF16)], tm=tm)
    gp = _mm("mm_ple_gate", hp, w["w_ple_gate"], "nn", BF16, m=t, n=d, k=d, tm=tb, tn=d, tk=d)
    x3 = _rowwise("ple_fwd", _ple_fwd, [R(x2), R(gp), R(e)], [O(d, F32)], tm=tm)
    sv.update(e=e, hp=hp, gp=gp, p=p_l)
    return x3, sv


def _layer_bwd(dx3, sv, w, s, rope, nbatch, seq):
    t = dx3.shape[0]
    tm = 512
    d = D_MODEL
    gb, gs = {}, {}
    cs = {nm: w[nm].shape[2] for nm in SHARD_MAJOR}
    tb = 1024

    def wg(name, a, b, m, n, tm=1024, tk=1024, shard=None):
        return _mm(name, a, b, "tn", BF16, m=m, n=n, k=t, tm=tm, tn=n if shard is None else cs[shard], tk=tk,
                   o_sh=None if shard is None else cs[shard])

    de, dgp = _rowwise("ple_bwd", _ple_bwd, [R(dx3), R(sv["gp"]), R(sv["e"])], [O(d, BF16), O(d, BF16)], tm=tm)
    gb["w_ple_in"] = wg("wg_ple_in", sv["p"], de, sv["p"].shape[1], d, tk=2048)
    gb["w_ple_gate"] = wg("wg_ple_gate", sv["hp"], dgp, d, d, tk=2048)
    dhp = _mm("mmb_ple_gate", dgp, w["w_ple_gate"], "nt", BF16, m=t, n=d, k=d, tm=tb, tn=d, tk=d)
    dx2, gs["ple_norm_g"] = _rowwise("rms_ple_bwd", _rms_bwd, [R(dhp), R(sv["x2"]), R(dx3), V(_row(s["ple_norm_g"]))],
                                     [O(d, F32)], [A(d)], tm=tm)
    fw = FFN_HIDDEN // 2
    dact = _mm("mmb_ffn_out", dx2, w["w_ffn_out"], "nt", BF16, m=t, n=FFN_HIDDEN, k=d, tm=tb, tn=fw, tk=d)
    gb["w_ffn_out"] = wg("wg_ffn_out", sv["act"], dx2, FFN_HIDDEN, d, tm=fw)
    dfg, dfu = _rowwise("ffn_act_bwd", _ffn_act_bwd, [R(dact, fw, 0), R(sv["f"], fw, 0), R(sv["f"], fw, 2)],
                        [O(fw, BF16, total=FFN_HIDDEN), O(fw, BF16, total=FFN_HIDDEN)], tm=tm, ncol=2)
    df = jnp.concatenate([dfg, dfu], axis=1)
    gb["w_ffn_in"] = wg("wg_ffn_in", sv["hf"], df, d, 2 * FFN_HIDDEN, shard="w_ffn_in")
    dhf = _mm("mmb_ffn_in", df, w["w_ffn_in"], "nt", BF16, m=t, n=d, k=2 * FFN_HIDDEN, tm=tb, tn=d,
              tk=cs["w_ffn_in"], b_sh=cs["w_ffn_in"])
    dx1, gs["ffn_norm_g"] = _rowwise("rms_ffn_bwd", _rms_bwd, [R(dhf), R(sv["x1"]), R(dx2), V(_row(s["ffn_norm_g"]))],
                                     [O(d, F32)], [A(d)], tm=tm)
    dm = _mm("mmb_mix", dx1, w["w_mix_out"], "nt", BF16, m=t, n=d, k=d, tm=tb, tn=d, tk=d)
    gb["w_mix_out"] = wg("wg_mix", sv["merged"], dx1, d, d)
    z, glu, bg = sv["z"], sv["glu"], _row(s["b_gate"])
    ins = [R(dm, 512, 0), R(z, 512, 3), R(z, 512, 5), R(z, 512, 7), V(bg, 512, 0), V(bg, 512, 2), V(bg, 512, 4),
           R(sv["ya"], 512, 0), R(glu, 512, 0), R(glu, 512, 2), R(sv["yc"], 512, 0)]
    ob = lambda: O(512, BF16, total=d)
    ab = lambda: A(512, total=d)
    dya, dga, dgb, dyc, d0, d1, d2, db0, db1, db2, dba, dbb = _rowwise(
        "merge_bwd", _merge_bwd, ins, [ob() for _ in range(7)], [ab() for _ in range(5)], tm=tm, ncol=2)
    gs["b_gate"] = jnp.concatenate([db0, db1, db2], axis=1)
    gs["b_ssm_glu"] = jnp.concatenate([dba, dbb], axis=1)
    dglu = jnp.concatenate([dga, dgb], axis=1)
    gb["w_attn_out"] = wg("wg_attn_out", sv["o"], dya, Q_WIDTH, d, tk=2048)
    do = _mm("mmb_attn_out", dya, w["w_attn_out"], "nt", BF16, m=t, n=Q_WIDTH, k=d, tm=tb, tn=Q_WIDTH, tk=d)
    dqh, dkc, dvc, dkp, dvp, dsink = _attn_bwd(sv["qh"], sv["kh"], sv["vh"], sv["oh"], _heads(do, N_Q_HEADS),
                                               sv["lse"], sv["sinks"], nbatch, seq)
    gs["attn_sinks"] = dsink.reshape(-1)
    c, sa, sb = rope
    dq = _rowwise("rope_bwd_q", _rope_bwd_q, [R(_tokens(dqh)), R(c), R(sa), R(sb)], [O(Q_WIDTH, BF16)], tm=tm)
    dk, dv = _kv_combine(_tokens(dkc), _tokens(dkp), _tokens(dvc), _tokens(dvp), c, sa, sb, seq)
    gb["w_ssm_glu"] = wg("wg_ssm_glu", sv["gel"], dglu, SSM_WIDTH, 2 * d, tk=2048)
    dgi = _mm("mmb_glu", dglu, w["w_ssm_glu"], "nt", BF16, m=t, n=SSM_WIDTH, k=2 * d, tm=tb, tn=SSM_WIDTH, tk=2 * d)
    a_re, a_im = sv["a"]
    du, dwb, dwc, dd, da = _ssm_bwd(dgi, sv["ys"], sv["st"], z, sv["wb"].T.astype(BF16), sv["wc"].T.astype(BF16),
                                    _ssm_tables(a_re, a_im, True), sv["dskip"], nbatch, seq)
    gs["ssm_d"] = dd.reshape(-1)
    da = jnp.sum(da, axis=0)
    _, prep_vjp = jax.vjp(_ssm_prep, *[s[nm] for nm in SSM_NAMES])
    for nm, g in zip(SSM_NAMES, prep_vjp((da[:SSM_LANES], da[SSM_LANES:], dwb, dwc))):
        gs[nm] = g
    gb["w_conv_out"] = wg("wg_conv_out", sv["sc"], dyc, CONV_WIDTH, d, tk=2048)
    dsc = _mm("mmb_conv_out", dyc, w["w_conv_out"], "nt", BF16, m=t, n=CONV_WIDTH, k=d, tm=tb, tn=CONV_WIDTH, tk=d)
    dca, dcg, dcw, dcb, dlg, dlb = _conv_bwd(dsc, sv["cv"], z, sv["cw"], _row(s["conv_norm_g"]),
                                             _row(s["conv_norm_b"]), nbatch, seq)
    gs["conv_dw_w"] = dcw[:CONV_K]
    gs["conv_dw_b"], gs["conv_norm_g"], gs["conv_norm_b"] = dcb.reshape(-1), dlg.reshape(-1), dlb.reshape(-1)
    dz = jnp.concatenate([dq, dk, dv, du, dca, dcg, d0, d1, d2], axis=1)
    gb["w_in"] = wg("wg_in", sv["h"], dz, d, dz.shape[1], tk=2048, shard="w_in")
    dh = _mm("mmb_in", dz, w["w_in"], "nt", BF16, m=t, n=d, k=dz.shape[1], tm=tb, tn=d, tk=cs["w_in"],
             b_sh=cs["w_in"])
    dx, gs["mix_norm_g"] = _rowwise("rms_mix_bwd", _rms_bwd, [R(dh), R(sv["x"]), R(dx1), V(_row(s["mix_norm_g"]))],
                                    [O(d, F32)], [A(d)], tm=tm)
    gs["mix_norm_g"], gs["ffn_norm_g"], gs["ple_norm_g"] = (gs[nm].reshape(-1) for nm in
                                                            ("mix_norm_g", "ffn_norm_g", "ple_norm_g"))
    gs["b_gate"], gs["b_ssm_glu"] = gs["b_gate"].reshape(-1), gs["b_ssm_glu"].reshape(-1)
    return dx, gb, gs


def _rope_tables(positions):
    inv_freq = ROPE_THETA ** (-jnp.arange(0, ROPE_DIM, 2, dtype=F32) / ROPE_DIM)
    ang = positions.reshape(-1).astype(F32)[:, None] * inv_freq
    cos, sin = jnp.cos(ang), jnp.sin(ang)
    t = ang.shape[0]
    rest = HEAD_DIM - ROPE_DIM
    c = jnp.concatenate([cos, cos, jnp.ones((t, rest), F32)], axis=1)
    sa = jnp.concatenate([-sin, jnp.zeros((t, HEAD_DIM - ROPE_HALF), F32)], axis=1)
    sb = jnp.concatenate([jnp.zeros((t, ROPE_HALF), F32), sin, jnp.zeros((t, rest), F32)], axis=1)
    two = lambda v: jnp.concatenate([v, v], axis=1)
    return two(c), two(sa), two(sb)


def _natural(nm, w4):
    if nm in SHARD_MAJOR:
        return w4
    if BIG_AXIS[nm] == 1:
        return w4.reshape(-1, w4.shape[2])
    return w4.transpose(1, 0, 2).reshape(w4.shape[1], -1)


def _shard_major(nm, g):
    if nm in SHARD_MAJOR:
        return g
    if BIG_AXIS[nm] == 1:
        return g.reshape(N_CHIPS, -1, g.shape[1])
    return g.reshape(g.shape[0], N_CHIPS, -1).transpose(1, 0, 2)


def _local_step(x, p, positions, loss_target, wfull, small):
    nbatch, seq, d = x.shape
    depth = p.shape[0]
    t = nbatch * seq
    rope = _rope_tables(positions)
    xs = x.reshape(t, d)
    saved, ws, ss = [], [], []
    for l in range(depth):
        w_l = {nm: _natural(nm, wfull[nm][l]) for nm in BIG}
        s_l = {nm: small[nm][l] for nm in SMALL if nm != "final_norm_g"}
        xs, sv = _layer_fwd(xs, p[l].reshape(t, -1), w_l, s_l, rope, nbatch, seq)
        saved.append(sv)
        ws.append(w_l)
        ss.append(s_l)
    dx, loss_cols, dgf = _rowwise("loss_head", _loss_fn, [R(xs), R(loss_target.reshape(t, d)),
                                                          V(_row(small["final_norm_g"]))],
                                  [O(d, F32)], [A(d), A(d)], tm=512)
    gbs, gss = [], []
    for l in reversed(range(depth)):
        dx, gb, gs = _layer_bwd(dx, saved[l], ws[l], ss[l], rope, nbatch, seq)
        gbs.insert(0, gb)
        gss.insert(0, gs)
    gbig = {nm: jnp.stack([_shard_major(nm, g[nm]) for g in gbs]) for nm in BIG}
    gsmall = {nm: jnp.stack([g[nm] for g in gss]) for nm in SMALL if nm != "final_norm_g"}
    gsmall["final_norm_g"] = dgf.reshape(-1)
    return loss_cols, dx.reshape(nbatch, seq, d), gbig, gsmall


HBM = pl.BlockSpec(memory_space=pltpu.HBM)


def _place():
    x, y, c = lax.axis_index("x"), lax.axis_index("y"), lax.axis_index("c")
    chips = [(1 - x, y), (x, 1 - y), (1 - x, 1 - y)]
    return x, y, c, chips


def _remote(src, dst, send_sem, recv_sem, to):
    return pltpu.make_async_remote_copy(src_ref=src, dst_ref=dst, send_sem=send_sem, recv_sem=recv_sem,
                                        device_id=to, device_id_type=MESH)


def _gather_shards(shards):
    n = len(shards)
    hl = shards[0].shape[0] // 2

    def body(*refs):
        ins, outs = refs[:n], refs[n:2 * n]
        send1, recv1, send2, recv2, send0, recv0 = refs[2 * n:]
        x, y, c, chips = _place()
        me = 2 * x + y
        mine, other = pl.ds(c * hl, hl), pl.ds((1 - c) * hl, hl)
        sib = (x, y, 1 - c)
        local = [_remote(ins[i], outs[i].at[:, me], send0.at[i], recv0.at[i], sib) for i in range(n)]
        for cp in local:
            cp.start()
        first = [[_remote(ins[i].at[mine], outs[i].at[mine, me], send1.at[i, k], recv1.at[i, k], (cx, cy, c))
                  for k, (cx, cy) in enumerate(chips)] for i in range(n)]
        for row in first:
            for cp in row:
                cp.start()
        passed = []
        for i in range(n):
            for k, (cx, cy) in enumerate(chips):
                slot = outs[i].at[mine, 2 * cx + cy]
                _remote(slot, slot, send1.at[i, k], recv1.at[i, k], (cx, cy, c)).wait_recv()
                cp = _remote(slot, slot, send2.at[i, k], recv2.at[i, k], sib)
                cp.start()
                passed.append(cp)
        for i in range(n):
            for k, (cx, cy) in enumerate(chips):
                slot = outs[i].at[other, 2 * cx + cy]
                _remote(slot, slot, send2.at[i, k], recv2.at[i, k], sib).wait_recv()
        for row in first:
            for cp in row:
                cp.wait_send()
        for cp in passed:
            cp.wait_send()
        for cp in local:
            cp.wait()

    out_shape = [jax.ShapeDtypeStruct((s.shape[0], N_CHIPS) + s.shape[1:], s.dtype) for s in shards]
    sems = [pltpu.SemaphoreType.DMA((n, 3)) for _ in range(4)] + [pltpu.SemaphoreType.DMA((n,)) for _ in range(2)]
    return pl.pallas_call(body, out_shape=out_shape, in_specs=[HBM] * n, out_specs=[HBM] * n, scratch_shapes=sems,
                          name="gather_weights")(*shards)


def _pair_exchange(grads):
    n = len(grads)
    hl = grads[0].shape[0] // 2

    def body(*refs):
        ins, outs = refs[:n], refs[n:2 * n]
        send, recv = refs[2 * n:]
        x, y, c, _ = _place()
        other = pl.ds((1 - c) * hl, hl)
        cps = [_remote(ins[i].at[other], outs[i], send.at[i], recv.at[i], (x, y, 1 - c)) for i in range(n)]
        for cp in cps:
            cp.start()
        for cp in cps:
            cp.wait()

    out_shape = [jax.ShapeDtypeStruct((hl,) + g.shape[1:], g.dtype) for g in grads]
    sems = [pltpu.SemaphoreType.DMA((n,)) for _ in range(2)]
    return pl.pallas_call(body, out_shape=out_shape, in_specs=[HBM] * n, out_specs=[HBM] * n, scratch_shapes=sems,
                          name="reduce_pair_exchange")(*grads)


def _pair_add(g, r):
    hl, _, rr, cc = r.shape
    rows = hl * N_CHIPS * rr
    nblk = rows // rr

    def body(c_ref, g_ref, r_ref, o_ref):
        o_ref[...] = (g_ref[...].astype(F32) + r_ref[...].astype(F32)).astype(o_ref.dtype)

    grid_spec = pltpu.PrefetchScalarGridSpec(
        num_scalar_prefetch=1, grid=(nblk,),
        in_specs=[pl.BlockSpec((rr, cc), lambda i, c_ref: (c_ref[0] * nblk + i, 0)),
                  pl.BlockSpec((rr, cc), lambda i, c_ref: (i, 0))],
        out_specs=pl.BlockSpec((rr, cc), lambda i, c_ref: (i, 0)))
    c = lax.axis_index("c").astype(jnp.int32).reshape(1)
    out = pl.pallas_call(body, out_shape=jax.ShapeDtypeStruct((rows, cc), r.dtype), grid_spec=grid_spec,
                         name="reduce_pair_add", compiler_params=_params(("parallel",)))(
        c, g.reshape(-1, cc), r.reshape(rows, cc))
    return out.reshape(r.shape)


def _chip_exchange(psums):
    n = len(psums)

    def body(*refs):
        ins, got = refs[:n], refs[n:2 * n]
        send, recv = refs[2 * n:]
        x, y, c, chips = _place()
        cps = [_remote(ins[i].at[:, 2 * cx + cy], got[i].at[k], send.at[i, k], recv.at[i, k], (cx, cy, c))
               for i in range(n) for k, (cx, cy) in enumerate(chips)]
        for cp in cps:
            cp.start()
        for cp in cps:
            cp.wait()

    got_shape = [jax.ShapeDtypeStruct((3, p.shape[0]) + p.shape[2:], p.dtype) for p in psums]
    sems = [pltpu.SemaphoreType.DMA((n, 3)), pltpu.SemaphoreType.DMA((n, 3))]
    return pl.pallas_call(body, out_shape=got_shape, in_specs=[HBM] * n, out_specs=[HBM] * n, scratch_shapes=sems,
                          name="reduce_chip_exchange")(*psums)


def _sum4(psum, got):
    hl, _, rr, cc = psum.shape
    tr = rr if rr * cc <= 512 * 1024 else rr // 2

    def body(place_ref, own_ref, g0_ref, g1_ref, g2_ref, o_ref):
        tot = (own_ref[...].astype(F32) + g0_ref[...].astype(F32)) + g1_ref[...].astype(F32)
        o_ref[...] = tot + g2_ref[...].astype(F32)

    def got_spec(k):
        return pl.BlockSpec((None, None, tr, cc), lambda h, i, place: (k, h, i, 0))

    grid_spec = pltpu.PrefetchScalarGridSpec(
        num_scalar_prefetch=1, grid=(hl, rr // tr),
        in_specs=[pl.BlockSpec((None, None, tr, cc), lambda h, i, place: (h, place[0], i, 0)),
                  got_spec(0), got_spec(1), got_spec(2)],
        out_specs=pl.BlockSpec((None, tr, cc), lambda h, i, place: (place[1] * hl + h, i, 0)))
    place = jnp.stack([2 * lax.axis_index("x") + lax.axis_index("y"), lax.axis_index("c")]).astype(jnp.int32)
    return pl.pallas_call(body, out_shape=jax.ShapeDtypeStruct((2 * hl, rr, cc), F32), grid_spec=grid_spec,
                          name="reduce_sum4", compiler_params=_params(("parallel", "parallel")))(
        place, psum, got, got, got)


def _pair_gather(sums):
    n = len(sums)
    hl = sums[0].shape[0] // 2

    def body(*refs):
        bufs = refs[n:2 * n]
        send, recv = refs[2 * n:]
        x, y, c, _ = _place()
        mine = pl.ds(c * hl, hl)
        cps = [_remote(bufs[i].at[mine], bufs[i].at[mine], send.at[i], recv.at[i], (x, y, 1 - c)) for i in range(n)]
        for cp in cps:
            cp.start()
        for cp in cps:
            cp.wait()

    out_shape = [jax.ShapeDtypeStruct(v.shape, v.dtype) for v in sums]
    sems = [pltpu.SemaphoreType.DMA((n,)) for _ in range(2)]
    return pl.pallas_call(body, out_shape=out_shape, in_specs=[HBM] * n, out_specs=[HBM] * n, scratch_shapes=sems,
                          input_output_aliases={i: i for i in range(n)}, name="reduce_pair_gather")(*sums)


def _allreduce_small(vec):
    rows = vec.shape[0]

    def body(v_ref, o_ref, all_ref, send, recv):
        x, y, c, _ = _place()
        me = 4 * x + 2 * y + c
        all_ref[me] = v_ref[...]
        cps = []
        for dlt in range(1, N_DEV):
            fx, fy, fc = (dlt >> 2) & 1, (dlt >> 1) & 1, dlt & 1
            to = (1 - x if fx else x, 1 - y if fy else y, 1 - c if fc else c)
            cps.append(_remote(v_ref, all_ref.at[me], send.at[dlt - 1], recv.at[dlt - 1], to))
        for cp in cps:
            cp.start()
        for cp in cps:
            cp.wait()
        tot = all_ref[0]
        for dev in range(1, N_DEV):
            tot = tot + all_ref[dev]
        o_ref[...] = tot

    vm = pl.BlockSpec(memory_space=pltpu.VMEM)
    return pl.pallas_call(
        body, out_shape=jax.ShapeDtypeStruct(vec.shape, F32), in_specs=[vm], out_specs=vm,
        scratch_shapes=[pltpu.VMEM((N_DEV, rows, 128), F32), pltpu.SemaphoreType.DMA((N_DEV - 1,)),
                        pltpu.SemaphoreType.DMA((N_DEV - 1,))],
        name="allreduce_small", compiler_params=pltpu.CompilerParams(vmem_limit_bytes=VMEM_LIMIT))(vec)


def _adamw(name, w, g, m, v):
    rows, cc = w.shape
    tm = math.gcd(rows, 256)
    return _rowwise(name, _adamw_fn, [R(w), R(g), R(m), R(v)], [O(cc, F32), O(cc, F32), O(cc, F32)], tm=tm)


def _pack(parts):
    flat = jnp.concatenate([v.reshape(-1).astype(F32) for v in parts])
    pad = (-flat.shape[0]) % (SUBLANES * 128)
    return jnp.pad(flat, (0, pad)).reshape(-1, 128)


def _unpack(packed, shapes):
    flat, out, pos = packed.reshape(-1), [], 0
    for shp in shapes:
        size = math.prod(shp)
        out.append(flat[pos:pos + size].reshape(shp))
        pos += size
    return out


def kernel(x, p, positions, mix_norm_g, w_in, b_gate, attn_sinks, w_attn_out, ssm_lambda_re, ssm_lambda_im, ssm_log_dt, ssm_b_re, ssm_b_im, ssm_c_re, ssm_c_im, ssm_d, w_ssm_glu, b_ssm_glu, conv_dw_w, conv_dw_b, conv_norm_g, conv_norm_b, w_conv_out, w_mix_out, ffn_norm_g, w_ffn_in, w_ffn_out, w_ple_in, ple_norm_g, w_ple_gate, final_norm_g, loss_target, m_mix_norm_g, m_w_in, m_b_gate, m_attn_sinks, m_w_attn_out, m_ssm_lambda_re, m_ssm_lambda_im, m_ssm_log_dt, m_ssm_b_re, m_ssm_b_im, m_ssm_c_re, m_ssm_c_im, m_ssm_d, m_w_ssm_glu, m_b_ssm_glu, m_conv_dw_w, m_conv_dw_b, m_conv_norm_g, m_conv_norm_b, m_w_conv_out, m_w_mix_out, m_ffn_norm_g, m_w_ffn_in, m_w_ffn_out, m_w_ple_in, m_ple_norm_g, m_w_ple_gate, m_final_norm_g, v_mix_norm_g, v_w_in, v_b_gate, v_attn_sinks, v_w_attn_out, v_ssm_lambda_re, v_ssm_lambda_im, v_ssm_log_dt, v_ssm_b_re, v_ssm_b_im, v_ssm_c_re, v_ssm_c_im, v_ssm_d, v_w_ssm_glu, v_b_ssm_glu, v_conv_dw_w, v_conv_dw_b, v_conv_norm_g, v_conv_norm_b, v_w_conv_out, v_w_mix_out, v_ffn_norm_g, v_w_ffn_in, v_w_ffn_out, v_w_ple_in, v_ple_norm_g, v_w_ple_gate, v_final_norm_g):
    given = dict(locals())
    wts = {nm: given[nm] for nm in WEIGHTS}
    mom = {nm: given["m_" + nm] for nm in WEIGHTS}
    var = {nm: given["v_" + nm] for nm in WEIGHTS}
    depth = p.shape[0]
    chip = 2 * lax.axis_index("x") + lax.axis_index("y")

    cw_cols = conv_dw_w.shape[2]
    taps = jnp.pad(conv_dw_w.reshape(depth, -1), ((0, 0), (0, (-CONV_K * cw_cols) % (SUBLANES * 128))))
    gathered = _gather_shards([wts[nm].astype(BF16) for nm in BIG] + [taps.reshape(depth, -1, 128)])
    wfull = dict(zip(BIG, gathered[:-1]))
    taps_all = gathered[-1].reshape(depth, N_CHIPS, -1)[:, :, :CONV_K * cw_cols]
    small = {nm: wts[nm] for nm in SMALL}
    small["conv_dw_w"] = taps_all.reshape(depth, N_CHIPS, CONV_K, cw_cols).transpose(0, 2, 1, 3).reshape(
        depth, CONV_K, N_CHIPS * cw_cols)

    loss_cols, grad_x, gbig, gsmall = _local_step(x, p, positions, loss_target, wfull, small)

    parts = [loss_cols] + [gsmall[nm] for nm in SMALL]
    total = _allreduce_small(_pack(parts))
    summed = _unpack(total, [v.shape for v in parts])
    loss = jnp.sum(summed[0])
    gsum = dict(zip(SMALL, summed[1:]))
    gsum["conv_dw_w"] = lax.dynamic_slice_in_dim(gsum["conv_dw_w"], chip * cw_cols, cw_cols, axis=2)
    shapes = [wts[nm].shape for nm in SMALL]
    deltas, new_m, new_v = _adamw("adamw_small", _pack([wts[nm] for nm in SMALL]), _pack([gsum[nm] for nm in SMALL]),
                                  _pack([mom[nm] for nm in SMALL]), _pack([var[nm] for nm in SMALL]))
    grads = dict(gsum)
    delta = dict(zip(SMALL, _unpack(deltas, shapes)))
    newm = dict(zip(SMALL, _unpack(new_m, shapes)))
    newv = dict(zip(SMALL, _unpack(new_v, shapes)))

    gl = [gbig[nm] for nm in BIG]
    sib = _pair_exchange(gl)
    psums = [_pair_add(g, r) for g, r in zip(gl, sib)]
    got = _chip_exchange(psums)
    sums = _pair_gather([_sum4(ps, g) for ps, g in zip(psums, got)])
    for nm, g in zip(BIG, sums):
        shp = wts[nm].shape
        two = lambda v: v.reshape(-1, shp[-1])
        g = g.reshape(shp)
        d_w, n_m, n_v = _adamw("adamw_" + nm, two(wts[nm]), two(g), two(mom[nm]), two(var[nm]))
        grads[nm], delta[nm], newm[nm], newv[nm] = g, d_w.reshape(shp), n_m.reshape(shp), n_v.reshape(shp)

    return (loss, grad_x, *[grads[nm] for nm in WEIGHTS], *[delta[nm] for nm in WEIGHTS],
            *[newm[nm] for nm in WEIGHTS], *[newv[nm] for nm in WEIGHTS])
```

```python
import functools
import math

import jax
import jax.numpy as jnp
from jax import lax
from jax.experimental import pallas as pl
from jax.experimental.pallas import tpu as pltpu

F32 = jnp.float32
BF16 = jnp.bfloat16

D_MODEL = 1024
HEAD_DIM = 64
N_Q_HEADS = 8
N_KV_HEADS = 2
GQA_GROUP = N_Q_HEADS // N_KV_HEADS
ATT_BLOCK = 128
ROPE_THETA = 500000.0
ROPE_DIM = HEAD_DIM // 4
ROPE_HALF = ROPE_DIM // 2
Q_WIDTH = N_Q_HEADS * HEAD_DIM
KV_WIDTH = N_KV_HEADS * HEAD_DIM
SSM_WIDTH = 256
SSM_GROUP = 16
SSM_GROUPS = 16
SSM_STATE = 64
SSM_LANES = SSM_GROUPS * SSM_STATE
CONV_WIDTH = 256
CONV_K = 31
CONV_HALO = 32
FFN_HIDDEN = 2816
EPS = 1e-6
NEG_INF = -1e30
SCALE = HEAD_DIM ** -0.5

ADAM_LR = 0.001
ADAM_B1 = 0.9
ADAM_B2 = 0.999
ADAM_EPS = 1e-08
ADAM_WD = 0.01
ADAM_STEP = 10

N_CHIPS = 4
N_DEV = 8
SUBLANES = 8
VMEM_LIMIT = 56 * 1024 * 1024

MESH = pl.DeviceIdType.MESH


def _params(sem=None):
    return pltpu.CompilerParams(dimension_semantics=sem, vmem_limit_bytes=VMEM_LIMIT)


def R(arr, width=None, cb=0, rb=0):
    return ("r", arr, arr.shape[1] if width is None else width, (cb, rb))


def V(arr, width=None, cb=0):
    return ("v", arr, arr.shape[1] if width is None else width, cb)


def _cbf(cb):
    return cb if callable(cb) else (lambda j, c=cb: c + j)


def _rowwise(name, fn, ins, outs, accs=(), *, tm, ncol=1, rows=None):
    t = rows if rows is not None else [a for k, a, _, _ in ins if k == "r"][0].shape[0]
    tm = min(tm, t)
    assert t % tm == 0, (name, t, tm)
    n_i, n_o, n_a = len(ins), len(outs), len(accs)

    def body(*refs):
        vals = fn(*[r[...] for r in refs[:n_i]])
        if not isinstance(vals, (tuple, list)):
            vals = (vals,)
        for ref, val in zip(refs[n_i:n_i + n_o], vals[:n_o]):
            ref[...] = val.astype(ref.dtype)
        if n_a:
            acc_refs = refs[n_i + n_o:]

            @pl.when(pl.program_id(1) == 0)
            def _():
                for ref in acc_refs:
                    ref[...] = jnp.zeros_like(ref)

            for ref, val in zip(acc_refs, vals[n_o:]):
                ref[...] += val

    in_specs = []
    for kind, arr, width, cb in ins:
        if kind == "r":
            f = _cbf(cb[0])
            in_specs.append(pl.BlockSpec((tm, width), functools.partial(lambda j, i, f, rb: (i + rb, f(j)), f=f, rb=cb[1])))
        else:
            f = _cbf(cb)
            in_specs.append(pl.BlockSpec((arr.shape[0], width), functools.partial(lambda j, i, f: (0, f(j)), f=f)))
    out_specs, out_shape = [], []
    for total, width, cb, dt in outs:
        f = _cbf(cb)
        out_specs.append(pl.BlockSpec((tm, width), functools.partial(lambda j, i, f: (i, f(j)), f=f)))
        out_shape.append(jax.ShapeDtypeStruct((t, total), dt))
    for total, width, cb in accs:
        f = _cbf(cb)
        out_specs.append(pl.BlockSpec((1, width), functools.partial(lambda j, i, f: (0, f(j)), f=f)))
        out_shape.append(jax.ShapeDtypeStruct((1, total), F32))
    sem = ("arbitrary", "arbitrary") if n_a else ("parallel", "parallel")
    res = pl.pallas_call(body, out_shape=out_shape, grid=(ncol, t // tm), in_specs=in_specs, out_specs=out_specs,
                         name=name, compiler_params=_params(sem))(*[a for _, a, _, _ in ins])
    return res[0] if len(res) == 1 else res


def O(width, dtype, total=None, cb=0):
    return (width if total is None else total, width, cb, dtype)


def A(width, total=None, cb=0):
    return (width if total is None else total, width, cb)


_DIMS = {"nn": (((1,), (0,)), ((), ())), "nt": (((1,), (1,)), ((), ())), "tn": (((0,), (0,)), ((), ()))}


def _mm(name, a, b, mode, out_dtype, *, m, n, k, tm, tn, tk, a_off=0, b_off=0, res=None, bias=None, b_sh=None, o_sh=None):
    tm, tn, tk = min(tm, m), min(tn, n), min(tk, k)
    assert m % tm == 0 and n % tn == 0 and k % tk == 0, (name, m, n, k, tm, tn, tk)
    nk = k // tk
    has_res, has_bias = res is not None, bias is not None
    a_fn, a_ops = a if isinstance(a, tuple) else (None, [(a, None)])
    b_fn, b_ops = b if isinstance(b, tuple) else (None, [(b, None)])
    na, nb_ = len(a_ops), len(b_ops)
    a_bytes = sum(m * k * arr.dtype.itemsize for arr, _ in a_ops)
    b_bytes = sum(n * k * arr.dtype.itemsize for arr, _ in b_ops)
    swap = nk == 1 and b_bytes + (n // tn) * a_bytes < a_bytes + (m // tm) * b_bytes

    def body(*refs):
        g0, g1, kk = pl.program_id(0), pl.program_id(1), pl.program_id(2)
        gi, gj = (g1, g0) if swap else (g0, g1)
        a_tiles = [r[...] for r in refs[:na]]
        b_tiles = [r[...] for r in refs[na:na + nb_]]
        a_val = a_tiles[0] if a_fn is None else a_fn(gi, gj, kk, *a_tiles)
        b_val = b_tiles[0] if b_fn is None else b_fn(gi, gj, kk, *b_tiles)
        pos = na + nb_
        res_ref = bias_ref = None
        if has_res:
            res_ref = refs[pos]
            pos += 1
        if has_bias:
            bias_ref = refs[pos]
            pos += 1
        o_ref = refs[pos]

        def finish(r):
            if has_bias:
                r = r + bias_ref[...]
            if has_res:
                r = r + res_ref[...].astype(F32)
            o_ref[...] = r.astype(o_ref.dtype)

        part = lax.dot_general(a_val.astype(BF16), b_val.astype(BF16), _DIMS[mode], preferred_element_type=F32)
        if nk == 1:
            finish(part)
            return
        acc_ref = refs[pos + 1]

        @pl.when(kk == 0)
        def _():
            acc_ref[...] = part

        @pl.when(kk > 0)
        def _():
            acc_ref[...] += part

        @pl.when(kk == nk - 1)
        def _():
            finish(acc_ref[...])

    def at(f):
        return (lambda g0, g1, kk: f(g1, g0, kk)) if swap else f

    if mode == "nn":
        a_spec = pl.BlockSpec((tm, tk), at(lambda i, j, kk: (i, kk + a_off)))
        b_spec = pl.BlockSpec((tk, tn), at(lambda i, j, kk: (kk, j + b_off)))
        if b_sh is not None:
            assert b_sh % tn == 0, (name, b_sh, tn)
            per = b_sh // tn
            b_spec = pl.BlockSpec((None, tk, tn), at(lambda i, j, kk: (j // per, kk, j % per)))
    elif mode == "nt":
        a_spec = pl.BlockSpec((tm, tk), at(lambda i, j, kk: (i, kk + a_off)))
        b_spec = pl.BlockSpec((tn, tk), at(lambda i, j, kk: (j, kk + b_off)))
        if b_sh is not None:
            assert b_sh % tk == 0, (name, b_sh, tk)
            per = b_sh // tk
            b_spec = pl.BlockSpec((None, tn, tk), at(lambda i, j, kk: (kk // per, j, kk % per)))
    else:
        a_spec = pl.BlockSpec((tk, tm), at(lambda i, j, kk: (kk, i + a_off)))
        b_spec = pl.BlockSpec((tk, tn), at(lambda i, j, kk: (kk, j + b_off)))
    a_specs = [a_spec] if a_fn is None else [pl.BlockSpec(a_spec.block_shape, at(f)) for _, f in a_ops]
    b_specs = [b_spec] if b_fn is None else [pl.BlockSpec(b_spec.block_shape, at(f)) for _, f in b_ops]
    in_specs, args = a_specs + b_specs, [arr for arr, _ in a_ops] + [arr for arr, _ in b_ops]
    if has_res:
        in_specs.append(pl.BlockSpec((tm, tn), at(lambda i, j, kk: (i, j))))
        args.append(res)
    if has_bias:
        in_specs.append(pl.BlockSpec((1, tn), at(lambda i, j, kk: (0, j))))
        args.append(bias)
    out_spec, out_shape = pl.BlockSpec((tm, tn), at(lambda i, j, kk: (i, j))), (m, n)
    if o_sh is not None:
        assert o_sh % tn == 0, (name, o_sh, tn)
        per_o = o_sh // tn
        out_spec = pl.BlockSpec((None, tm, tn), at(lambda i, j, kk: (j // per_o, i, j % per_o)))
        out_shape = (n // o_sh, m, o_sh)
    grid = (n // tn, m // tm, nk) if swap else (m // tm, n // tn, nk)
    return pl.pallas_call(
        body, out_shape=jax.ShapeDtypeStruct(out_shape, out_dtype), grid=grid,
        in_specs=in_specs, out_specs=out_spec,
        scratch_shapes=[pltpu.VMEM((tm, tn), F32)] if nk > 1 else [], name=name,
        compiler_params=_params(("parallel", "parallel", "arbitrary")))(*args)


def _sig(v):
    return jax.nn.sigmoid(v)


def _rms_fwd(x, g):
    r = lax.rsqrt(jnp.mean(x * x, axis=-1, keepdims=True) + EPS)
    return x * r * g


def _rms_bwd(dh, x, dres, g):
    dh = dh.astype(F32)
    r = lax.rsqrt(jnp.mean(x * x, axis=-1, keepdims=True) + EPS)
    xh = x * r
    dxh = dh * g
    dx = r * (dxh - xh * jnp.mean(dxh * xh, axis=-1, keepdims=True))
    return dres + dx, jnp.sum(dh * xh, axis=0, keepdims=True)


def _rope_apply(t, c, sa, sb):
    w = t.shape[1]
    return t * c + pltpu.roll(t, w - ROPE_HALF, 1) * sa + pltpu.roll(t, ROPE_HALF, 1) * sb


def _rope_transpose(g, c, sa, sb):
    w = g.shape[1]
    return g * c + pltpu.roll(g * sa, ROPE_HALF, 1) + pltpu.roll(g * sb, w - ROPE_HALF, 1)


def _tile_lanes(tab, reps):
    return jnp.concatenate([tab] * reps, axis=1) if reps > 1 else tab


def _rope_fwd(q, k, c, sa, sb):
    rq = Q_WIDTH // c.shape[1]
    qr = _rope_apply(q.astype(F32), _tile_lanes(c, rq), _tile_lanes(sa, rq), _tile_lanes(sb, rq))
    kr = _rope_apply(k.astype(F32), c, sa, sb)
    return qr, kr


def _rope_bwd_q(g, c, sa, sb):
    rq = Q_WIDTH // c.shape[1]
    return _rope_transpose(g.astype(F32), _tile_lanes(c, rq), _tile_lanes(sa, rq), _tile_lanes(sb, rq))


def _gelu(v):
    return jax.nn.gelu(v, approximate=True)


def _gelu_grad(v):
    c0 = math.sqrt(2.0 / math.pi)
    inner = c0 * (v + 0.044715 * v * v * v)
    th = jnp.tanh(inner)
    return 0.5 * (1.0 + th) + 0.5 * v * (1.0 - th * th) * c0 * (1.0 + 3 * 0.044715 * v * v)


def _merge_fwd(g0, g1, g2, b0, b1, b2, ya, ga, gb, yc):
    s0 = _sig(g0.astype(F32) + b0)
    s1 = _sig(g1.astype(F32) + b1)
    s2 = _sig(g2.astype(F32) + b2)
    ys = ga.astype(F32) * _sig(gb.astype(F32))
    return s0 * ya.astype(F32) + s1 * ys + s2 * yc.astype(F32)


def _merge_bwd(dm, g0, g1, g2, b0, b1, b2, ya, ga, gb, yc):
    dm = dm.astype(F32)
    s0 = _sig(g0.astype(F32) + b0)
    s1 = _sig(g1.astype(F32) + b1)
    s2 = _sig(g2.astype(F32) + b2)
    ga = ga.astype(F32)
    sb = _sig(gb.astype(F32))
    ys = ga * sb
    dya = dm * s0
    dys = dm * s1
    dyc = dm * s2
    dga = dys * sb
    dgb = dys * ga * sb * (1.0 - sb)
    d0 = dm * ya.astype(F32) * s0 * (1.0 - s0)
    d1 = dm * ys * s1 * (1.0 - s1)
    d2 = dm * yc.astype(F32) * s2 * (1.0 - s2)
    cs = lambda v: jnp.sum(v, axis=0, keepdims=True)
    return dya, dga, dgb, dyc, d0, d1, d2, cs(d0), cs(d1), cs(d2), cs(dga), cs(dgb)


def _ffn_act(fg, fu):
    fg = fg.astype(F32)
    return fg * _sig(fg) * fu.astype(F32)


def _ffn_act_bwd(da, fg, fu):
    da, fg, fu = da.astype(F32), fg.astype(F32), fu.astype(F32)
    s = _sig(fg)
    return da * fu * (s * (1.0 + fg * (1.0 - s))), da * fg * s


def _ple_fwd(x, gp, e):
    return x + _sig(gp.astype(F32)) * e.astype(F32)


def _ple_bwd(dx, gp, e):
    s = _sig(gp.astype(F32))
    e = e.astype(F32)
    return dx * s, dx * e * s * (1.0 - s)


def _loss_fn(x, tgt, g):
    d = x.shape[1]
    r = lax.rsqrt(jnp.mean(x * x, axis=-1, keepdims=True) + EPS)
    xh = x * r
    err = xh * g - tgt
    dy = err * (1.0 / d)
    dxh = dy * g
    dx = r * (dxh - xh * jnp.mean(dxh * xh, axis=-1, keepdims=True))
    return dx, jnp.sum(err * err, axis=0, keepdims=True) * (0.5 / d), jnp.sum(dy * xh, axis=0, keepdims=True)


def _adamw_fn(w, g, m, v):
    m = ADAM_B1 * m + (1.0 - ADAM_B1) * g
    v = ADAM_B2 * v + (1.0 - ADAM_B2) * (g * g)
    m_hat = m / (1.0 - ADAM_B1 ** ADAM_STEP)
    v_hat = v / (1.0 - ADAM_B2 ** ADAM_STEP)
    delta = -ADAM_LR * (m_hat / (jnp.sqrt(v_hat) + ADAM_EPS) + ADAM_WD * w)
    return delta, m, v


def _band_mask(n):
    qi = lax.broadcasted_iota(jnp.int32, (ATT_BLOCK, 2 * ATT_BLOCK), 0)
    kj = lax.broadcasted_iota(jnp.int32, (ATT_BLOCK, 2 * ATT_BLOCK), 1)
    dist = qi + ATT_BLOCK - kj
    return (dist >= 0) & (dist < ATT_BLOCK) & ((n > 0) | (kj >= ATT_BLOCK))


def _att_specs(nb):
    cur = lambda b, n: (0, b * nb + n, 0)
    prev = lambda b, n: (0, b * nb + jnp.maximum(n - 1, 0), 0)
    qs = pl.BlockSpec((N_Q_HEADS, ATT_BLOCK, HEAD_DIM), cur)
    kc = pl.BlockSpec((N_KV_HEADS, ATT_BLOCK, HEAD_DIM), cur)
    kp = pl.BlockSpec((N_KV_HEADS, ATT_BLOCK, HEAD_DIM), prev)
    stat = pl.BlockSpec((N_Q_HEADS, ATT_BLOCK, 1), cur)
    sink = pl.BlockSpec((N_Q_HEADS, 1, 1), lambda b, n: (0, 0, 0))
    return qs, kc, kp, stat, sink


def _attn_fwd(qh, kh, vh, sinks, nbatch, seq):
    t = qh.shape[1]
    nb = seq // ATT_BLOCK
    qs, kc, kp, stat, sink = _att_specs(nb)

    def body(q_ref, kp_ref, kc_ref, vp_ref, vc_ref, sink_ref, o_ref, lse_ref):
        mask = _band_mask(pl.program_id(1))
        rows = GQA_GROUP * ATT_BLOCK
        for kv in range(N_KV_HEADS):
            hs = slice(kv * GQA_GROUP, (kv + 1) * GQA_GROUP)
            kk = jnp.concatenate([kp_ref[kv], kc_ref[kv]], axis=0)
            vv = jnp.concatenate([vp_ref[kv], vc_ref[kv]], axis=0)
            q4 = (q_ref[hs] * SCALE).reshape(rows, HEAD_DIM)
            s = lax.dot_general(q4, kk, _DIMS["nt"], preferred_element_type=F32)
            s = jnp.where(mask, s.reshape(GQA_GROUP, ATT_BLOCK, 2 * ATT_BLOCK), NEG_INF)
            sk = sink_ref[hs]
            mx = jnp.maximum(jnp.max(s, axis=-1, keepdims=True), sk)
            p = jnp.exp(s - mx)
            den = jnp.sum(p, axis=-1, keepdims=True) + jnp.exp(sk - mx)
            o = lax.dot_general(p.reshape(rows, 2 * ATT_BLOCK).astype(BF16), vv, _DIMS["nn"],
                                preferred_element_type=F32).reshape(GQA_GROUP, ATT_BLOCK, HEAD_DIM)
            o_ref[hs] = (o * (1.0 / den)).astype(o_ref.dtype)
            lse_ref[hs] = mx + jnp.log(den)

    return pl.pallas_call(
        body, grid=(nbatch, nb), in_specs=[qs, kp, kc, kp, kc, sink], out_specs=[qs, stat],
        out_shape=[jax.ShapeDtypeStruct((N_Q_HEADS, t, HEAD_DIM), BF16), jax.ShapeDtypeStruct((N_Q_HEADS, t, 1), F32)],
        name="attn_fwd", compiler_params=_params(("parallel", "parallel")))(qh, kh, kh, vh, vh, sinks)


def _attn_bwd(qh, kh, vh, oh, doh, lse, sinks, nbatch, seq):
    t = qh.shape[1]
    nb = seq // ATT_BLOCK
    qs, kc, kp, stat, sink = _att_specs(nb)

    def body(q_ref, kp_ref, kc_ref, vp_ref, vc_ref, o_ref, do_ref, lse_ref, sink_ref,
             dq_ref, dkc_ref, dvc_ref, dkp_ref, dvp_ref, dsink_ref):
        first = (pl.program_id(0) == 0) & (pl.program_id(1) == 0)

        @pl.when(first)
        def _():
            dsink_ref[...] = jnp.zeros_like(dsink_ref)

        mask = _band_mask(pl.program_id(1))
        rows = GQA_GROUP * ATT_BLOCK
        band = (GQA_GROUP, ATT_BLOCK, 2 * ATT_BLOCK)
        for kv in range(N_KV_HEADS):
            hs = slice(kv * GQA_GROUP, (kv + 1) * GQA_GROUP)
            kk = jnp.concatenate([kp_ref[kv], kc_ref[kv]], axis=0)
            vv = jnp.concatenate([vp_ref[kv], vc_ref[kv]], axis=0)
            q4 = q_ref[hs].reshape(rows, HEAD_DIM)
            do4 = do_ref[hs].reshape(rows, HEAD_DIM)
            lse4 = lse_ref[hs]
            s = lax.dot_general(q4 * SCALE, kk, _DIMS["nt"], preferred_element_type=F32).reshape(band)
            p = jnp.where(mask, jnp.exp(s - lse4), 0.0)
            dd = jnp.sum(do_ref[hs].astype(F32) * o_ref[hs].astype(F32), axis=-1, keepdims=True)
            dp = lax.dot_general(do4, vv, _DIMS["nt"], preferred_element_type=F32).reshape(band)
            ds = (p * (dp - dd) * SCALE).astype(BF16).reshape(rows, 2 * ATT_BLOCK)
            dq = lax.dot_general(ds, kk, _DIMS["nn"], preferred_element_type=F32)
            dq_ref[hs] = dq.reshape(GQA_GROUP, ATT_BLOCK, HEAD_DIM).astype(dq_ref.dtype)
            dk = lax.dot_general(ds, q4, _DIMS["tn"], preferred_element_type=F32)
            dv = lax.dot_general(p.astype(BF16).reshape(rows, 2 * ATT_BLOCK), do4, _DIMS["tn"],
                                 preferred_element_type=F32)
            dsink_ref[hs] += -jnp.sum(jnp.exp(sink_ref[hs] - lse4) * dd, axis=1, keepdims=True)
            dkp_ref[kv] = dk[:ATT_BLOCK]
            dkc_ref[kv] = dk[ATT_BLOCK:]
            dvp_ref[kv] = dv[:ATT_BLOCK]
            dvc_ref[kv] = dv[ATT_BLOCK:]

    kvs = jax.ShapeDtypeStruct((N_KV_HEADS, t, HEAD_DIM), F32)
    return pl.pallas_call(
        body, grid=(nbatch, nb), in_specs=[qs, kp, kc, kp, kc, qs, qs, stat, sink],
        out_specs=[qs, kc, kc, kc, kc, sink],
        out_shape=[jax.ShapeDtypeStruct((N_Q_HEADS, t, HEAD_DIM), F32), kvs, kvs, kvs, kvs,
                   jax.ShapeDtypeStruct((N_Q_HEADS, 1, 1), F32)],
        name="attn_bwd", compiler_params=_params(("arbitrary", "arbitrary")))(qh, kh, kh, vh, vh, oh, doh, lse, sinks)


def _kv_combine(dkc, dkp, dvc, dvp, c, sa, sb, seq):
    t = dkc.shape[0]
    nb = seq // ATT_BLOCK
    nblk = t // ATT_BLOCK

    def body(kc_ref, kp_ref, vc_ref, vp_ref, c_ref, sa_ref, sb_ref, dk_ref, dv_ref):
        has_next = (pl.program_id(0) % nb) != nb - 1
        dk = kc_ref[...] + jnp.where(has_next, kp_ref[...], 0.0)
        dv = vc_ref[...] + jnp.where(has_next, vp_ref[...], 0.0)
        dk_ref[...] = _rope_transpose(dk, c_ref[...], sa_ref[...], sb_ref[...]).astype(dk_ref.dtype)
        dv_ref[...] = dv.astype(dv_ref.dtype)

    cur = pl.BlockSpec((ATT_BLOCK, KV_WIDTH), lambda i: (i, 0))
    nxt = pl.BlockSpec((ATT_BLOCK, KV_WIDTH), lambda i: (jnp.minimum(i + 1, nblk - 1), 0))
    o = jax.ShapeDtypeStruct((t, KV_WIDTH), BF16)
    return pl.pallas_call(body, grid=(nblk,), in_specs=[cur, nxt, cur, nxt, cur, cur, cur], out_specs=[cur, cur],
                          out_shape=[o, o], name="kv_combine", compiler_params=_params(("parallel",)))(
        dkc, dkp, dvc, dvp, c, sa, sb)


def _scan_block(ref, tab_ref, carry, ngroups, reverse):
    shifts = (7, 6, 4) if reverse else (1, 2, 4)
    n = SSM_LANES

    def step(i, car):
        g = (ngroups - 1 - i) if reverse else i
        r0 = pl.multiple_of(g * SUBLANES, SUBLANES)
        xr = ref[pl.ds(r0, SUBLANES), :n]
        xi = ref[pl.ds(r0, SUBLANES), n:]
        for s, sh in enumerate(shifts):
            pr, pi = tab_ref[2 * s], tab_ref[2 * s + 1]
            yr, yi = pltpu.roll(xr, sh, 0), pltpu.roll(xi, sh, 0)
            xr, xi = xr + pr * yr - pi * yi, xi + pr * yi + pi * yr
        cr, ci = car
        qr, qi = tab_ref[6], tab_ref[7]
        xr, xi = xr + qr * cr - qi * ci, xi + qr * ci + qi * cr
        ref[pl.ds(r0, SUBLANES), :n] = xr
        ref[pl.ds(r0, SUBLANES), n:] = xi
        last = r0 if reverse else r0 + SUBLANES - 1
        return ref[pl.ds(last, 1), :n], ref[pl.ds(last, 1), n:]

    return lax.fori_loop(0, ngroups, step, carry, unroll=2)


def _ssm_chunk(seq):
    return min(512, seq)


def _ssm_fwd(z, wb, wc, tab, dskip, nbatch, seq):
    t = z.shape[0]
    tc = _ssm_chunk(seq)
    nc = seq // tc
    n2 = 2 * SSM_LANES

    def body(u_ref, wb_ref, wc_ref, tab_ref, d_ref, st_ref, y_ref, gel_ref, car_ref):
        @pl.when(pl.program_id(1) == 0)
        def _():
            car_ref[...] = jnp.zeros_like(car_ref)

        u = u_ref[...]
        st_ref[...] = lax.dot_general(u, wb_ref[...], _DIMS["nn"], preferred_element_type=F32)
        cr, ci = _scan_block(st_ref, tab_ref, (car_ref[:, :SSM_LANES], car_ref[:, SSM_LANES:]), tc // SUBLANES, False)
        car_ref[:, :SSM_LANES] = cr
        car_ref[:, SSM_LANES:] = ci
        y = lax.dot_general(st_ref[...].astype(BF16), wc_ref[...], _DIMS["nn"], preferred_element_type=F32)
        y = y + d_ref[...] * u.astype(F32)
        y_ref[...] = y
        gel_ref[...] = _gelu(y).astype(gel_ref.dtype)

    row = lambda b, c: (b * nc + c, 0)
    full = lambda b, c: (0, 0)
    return pl.pallas_call(
        body, grid=(nbatch, nc),
        in_specs=[pl.BlockSpec((tc, SSM_WIDTH), lambda b, c: (b * nc + c, 3)), pl.BlockSpec((SSM_WIDTH, n2), full),
                  pl.BlockSpec((n2, SSM_WIDTH), full), pl.BlockSpec((8, SUBLANES, SSM_LANES), lambda b, c: (0, 0, 0)),
                  pl.BlockSpec((1, SSM_WIDTH), full)],
        out_specs=[pl.BlockSpec((tc, n2), row), pl.BlockSpec((tc, SSM_WIDTH), row), pl.BlockSpec((tc, SSM_WIDTH), row)],
        out_shape=[jax.ShapeDtypeStruct((t, n2), F32), jax.ShapeDtypeStruct((t, SSM_WIDTH), F32),
                   jax.ShapeDtypeStruct((t, SSM_WIDTH), BF16)],
        scratch_shapes=[pltpu.VMEM((1, n2), F32)], name="ssm_fwd",
        compiler_params=_params(("arbitrary", "arbitrary")))(z, wb, wc, tab, dskip)


def _ssm_bwd(dgi, ys, st, z, wbt, wct, tab_rev, dskip, nbatch, seq):
    t = z.shape[0]
    tc = _ssm_chunk(seq)
    nc = seq // tc
    n = SSM_LANES
    n2 = 2 * n
    ng = tc // SUBLANES

    def body(dgi_ref, ys_ref, st_ref, stp_ref, u_ref, wbt_ref, wct_ref, tab_ref, d_ref,
             du_ref, dwb_ref, dwc_ref, dd_ref, da_ref, p_ref, sb_ref, car_ref):
        b, c = pl.program_id(0), pl.program_id(1)
        ct = nc - 1 - c

        @pl.when((b == 0) & (c == 0))
        def _():
            dwb_ref[...] = jnp.zeros_like(dwb_ref)
            dwc_ref[...] = jnp.zeros_like(dwc_ref)
            dd_ref[...] = jnp.zeros_like(dd_ref)
            da_ref[...] = jnp.zeros_like(da_ref)

        @pl.when(c == 0)
        def _():
            car_ref[...] = jnp.zeros_like(car_ref)

        u = u_ref[...]
        dys = dgi_ref[...].astype(F32) * _gelu_grad(ys_ref[...])
        dys_b = dys.astype(BF16)
        st = st_ref[...]
        dd_ref[...] += jnp.sum(dys * u.astype(F32), axis=0, keepdims=True)
        dwc_ref[...] += lax.dot_general(st.astype(BF16), dys_b, _DIMS["tn"], preferred_element_type=F32)
        p_ref[...] = lax.dot_general(dys_b, wct_ref[...], _DIMS["nn"], preferred_element_type=F32)
        cr, ci = _scan_block(p_ref, tab_ref, (car_ref[:, :n], car_ref[:, n:]), ng, True)
        car_ref[:, :n] = cr
        car_ref[:, n:] = ci
        p = p_ref[...]
        pb = p.astype(BF16)
        dwb_ref[...] += lax.dot_general(u, pb, _DIMS["tn"], preferred_element_type=F32)
        du = lax.dot_general(pb, wbt_ref[...], _DIMS["nn"], preferred_element_type=F32) + d_ref[...] * dys
        du_ref[...] = du.astype(du_ref.dtype)
        sb_ref[pl.ds(0, SUBLANES), :] = jnp.where(ct > 0, stp_ref[...], 0.0)
        sb_ref[pl.ds(SUBLANES, tc), :] = st
        row0 = lax.broadcasted_iota(jnp.int32, (SUBLANES, n), 0) == 0

        def acc_step(g, acc):
            ar, ai = acc
            r0 = pl.multiple_of(g * SUBLANES, SUBLANES)
            edge_r = sb_ref[pl.ds(r0 + SUBLANES - 1, 1), :n]
            edge_i = sb_ref[pl.ds(r0 + SUBLANES - 1, 1), n:]
            sr = jnp.where(row0, edge_r, pltpu.roll(sb_ref[pl.ds(r0 + SUBLANES, SUBLANES), :n], 1, 0))
            si = jnp.where(row0, edge_i, pltpu.roll(sb_ref[pl.ds(r0 + SUBLANES, SUBLANES), n:], 1, 0))
            pr = p_ref[pl.ds(r0, SUBLANES), :n]
            pi = p_ref[pl.ds(r0, SUBLANES), n:]
            return ar + pr * sr + pi * si, ai + pi * sr - pr * si

        zero = jnp.zeros((SUBLANES, n), F32)
        ar, ai = lax.fori_loop(0, ng, acc_step, (zero, zero), unroll=2)
        da_ref[:, :n] += ar
        da_ref[:, n:] += ai

    row = lambda b, c: (b * nc + (nc - 1 - c), 0)
    prev8 = lambda b, c: (jnp.maximum((b * nc + (nc - 1 - c)) * (tc // SUBLANES) - 1, 0), 0)
    full = lambda b, c: (0, 0)
    return pl.pallas_call(
        body, grid=(nbatch, nc),
        in_specs=[pl.BlockSpec((tc, SSM_WIDTH), row), pl.BlockSpec((tc, SSM_WIDTH), row), pl.BlockSpec((tc, n2), row),
                  pl.BlockSpec((SUBLANES, n2), prev8),
                  pl.BlockSpec((tc, SSM_WIDTH), lambda b, c: (b * nc + (nc - 1 - c), 3)),
                  pl.BlockSpec((n2, SSM_WIDTH), full), pl.BlockSpec((SSM_WIDTH, n2), full),
                  pl.BlockSpec((8, SUBLANES, n), lambda b, c: (0, 0, 0)), pl.BlockSpec((1, SSM_WIDTH), full)],
        out_specs=[pl.BlockSpec((tc, SSM_WIDTH), row), pl.BlockSpec((SSM_WIDTH, n2), full),
                   pl.BlockSpec((n2, SSM_WIDTH), full), pl.BlockSpec((1, SSM_WIDTH), full),
                   pl.BlockSpec((SUBLANES, n2), full)],
        out_shape=[jax.ShapeDtypeStruct((t, SSM_WIDTH), BF16), jax.ShapeDtypeStruct((SSM_WIDTH, n2), F32),
                   jax.ShapeDtypeStruct((n2, SSM_WIDTH), F32), jax.ShapeDtypeStruct((1, SSM_WIDTH), F32),
                   jax.ShapeDtypeStruct((SUBLANES, n2), F32)],
        scratch_shapes=[pltpu.VMEM((tc, n2), F32), pltpu.VMEM((tc + SUBLANES, n2), F32), pltpu.VMEM((1, n2), F32)],
        name="ssm_bwd", compiler_params=_params(("arbitrary", "arbitrary")))(
        dgi, ys, st, st, z, wbt, wct, tab_rev, dskip)


def _ssm_prep(lam_re, lam_im, log_dt, b_re, b_im, c_re, c_im):
    lr = jnp.minimum(lam_re, -1e-4)
    li = lam_im
    dt = jnp.exp(log_dt)[:, None]
    mag = jnp.exp(lr * dt)
    a_re = mag * jnp.cos(li * dt)
    a_im = mag * jnp.sin(li * dt)
    den = lr * lr + li * li
    x_re, x_im = a_re - 1.0, a_im
    f_re = (x_re * lr + x_im * li) / den
    f_im = (x_im * lr - x_re * li) / den
    bb_re = f_re[..., None] * b_re - f_im[..., None] * b_im
    bb_im = f_re[..., None] * b_im + f_im[..., None] * b_re
    eye = jnp.eye(SSM_GROUPS, dtype=F32)
    emb_b = lambda v: jnp.einsum("gnh,gk->ghkn", v, eye).reshape(SSM_WIDTH, SSM_LANES)
    emb_c = lambda v: jnp.einsum("ghn,gk->gnkh", v, eye).reshape(SSM_LANES, SSM_WIDTH)
    wb = jnp.concatenate([emb_b(bb_re), emb_b(bb_im)], axis=1)
    wc = jnp.concatenate([emb_c(c_re), -emb_c(c_im)], axis=0)
    return a_re.reshape(-1), a_im.reshape(-1), wb, wc


def _ssm_tables(a_re, a_im, reverse):
    if reverse:
        a_im = -a_im
    pw = [(a_re, a_im)]
    for _ in range(SUBLANES - 1):
        pr, pi = pw[-1]
        pw.append((pr * a_re - pi * a_im, pr * a_im + pi * a_re))
    rows = jnp.arange(SUBLANES)[:, None]
    tabs = []
    for k in (1, 2, 4):
        ok = (rows + k <= SUBLANES - 1) if reverse else (rows >= k)
        tabs += [jnp.where(ok, pw[k - 1][0][None], 0.0), jnp.where(ok, pw[k - 1][1][None], 0.0)]
    order = list(range(SUBLANES - 1, -1, -1)) if reverse else list(range(SUBLANES))
    tabs += [jnp.stack([pw[i][0] for i in order]), jnp.stack([pw[i][1] for i in order])]
    return jnp.stack(tabs)


def _conv_chunk(seq):
    return min(512, seq)


def _conv_fwd(z, w, bias, lg, lb, nbatch, seq):
    t = z.shape[0]
    tc = _conv_chunk(seq)
    nc = seq // tc

    def body(a_ref, g_ref, w_ref, b_ref, lg_ref, lb_ref, cv_ref, sc_ref, ubuf):
        c = pl.program_id(1)

        @pl.when(c == 0)
        def _():
            ubuf[pl.ds(0, CONV_HALO), :] = jnp.zeros((CONV_HALO, CONV_WIDTH), F32)

        @pl.when(c > 0)
        def _():
            ubuf[pl.ds(0, CONV_HALO), :] = ubuf[pl.ds(tc, CONV_HALO), :]

        ubuf[pl.ds(CONV_HALO, tc), :] = a_ref[...].astype(F32) * _sig(g_ref[...].astype(F32))
        acc = jnp.zeros((tc, CONV_WIDTH), F32) + b_ref[...]
        for k in range(CONV_K):
            acc = acc + w_ref[pl.ds(k, 1), :] * ubuf[pl.ds(CONV_HALO - (CONV_K - 1) + k, tc), :]
        cv_ref[...] = acc
        mu = jnp.mean(acc, axis=-1, keepdims=True)
        xc = acc - mu
        y = xc * lax.rsqrt(jnp.mean(xc * xc, axis=-1, keepdims=True) + EPS) * lg_ref[...] + lb_ref[...]
        sc_ref[...] = (y * _sig(y)).astype(sc_ref.dtype)

    row = lambda b, c: (b * nc + c, 0)
    full = lambda b, c: (0, 0)
    vec = pl.BlockSpec((1, CONV_WIDTH), full)
    return pl.pallas_call(
        body, grid=(nbatch, nc),
        in_specs=[pl.BlockSpec((tc, CONV_WIDTH), lambda b, c: (b * nc + c, 4)),
                  pl.BlockSpec((tc, CONV_WIDTH), lambda b, c: (b * nc + c, 5)),
                  pl.BlockSpec((CONV_HALO, CONV_WIDTH), full), vec, vec, vec],
        out_specs=[pl.BlockSpec((tc, CONV_WIDTH), row), pl.BlockSpec((tc, CONV_WIDTH), row)],
        out_shape=[jax.ShapeDtypeStruct((t, CONV_WIDTH), F32), jax.ShapeDtypeStruct((t, CONV_WIDTH), BF16)],
        scratch_shapes=[pltpu.VMEM((CONV_HALO + tc, CONV_WIDTH), F32)], name="conv_fwd",
        compiler_params=_params(("arbitrary", "arbitrary")))(z, z, w, bias, lg, lb)


def _conv_bwd(dsc, cv, z, w, lg, lb, nbatch, seq):
    t = z.shape[0]
    tc = _conv_chunk(seq)
    nc = seq // tc
    hb = tc // CONV_HALO

    def body(dsc_ref, cv_ref, a_ref, g_ref, ap_ref, gp_ref, w_ref, lg_ref, lb_ref,
             da_ref, dg_ref, dw_ref, db_ref, dlg_ref, dlb_ref, ubuf, dbuf):
        b, c = pl.program_id(0), pl.program_id(1)
        ct = nc - 1 - c

        @pl.when((b == 0) & (c == 0))
        def _():
            dw_ref[...] = jnp.zeros_like(dw_ref)
            db_ref[...] = jnp.zeros_like(db_ref)
            dlg_ref[...] = jnp.zeros_like(dlg_ref)
            dlb_ref[...] = jnp.zeros_like(dlb_ref)

        cvv = cv_ref[...]
        mu = jnp.mean(cvv, axis=-1, keepdims=True)
        xc = cvv - mu
        rstd = lax.rsqrt(jnp.mean(xc * xc, axis=-1, keepdims=True) + EPS)
        xh = xc * rstd
        y = xh * lg_ref[...] + lb_ref[...]
        sy = _sig(y)
        dy = dsc_ref[...].astype(F32) * (sy * (1.0 + y * (1.0 - sy)))
        dlg_ref[...] += jnp.sum(dy * xh, axis=0, keepdims=True)
        dlb_ref[...] += jnp.sum(dy, axis=0, keepdims=True)
        dxh = dy * lg_ref[...]
        dcv = rstd * (dxh - jnp.mean(dxh, axis=-1, keepdims=True) - xh * jnp.mean(dxh * xh, axis=-1, keepdims=True))
        db_ref[...] += jnp.sum(dcv, axis=0, keepdims=True)

        @pl.when(c == 0)
        def _():
            dbuf[pl.ds(tc, CONV_HALO), :] = jnp.zeros((CONV_HALO, CONV_WIDTH), F32)

        @pl.when(c > 0)
        def _():
            dbuf[pl.ds(tc, CONV_HALO), :] = dbuf[pl.ds(0, CONV_HALO), :]

        dbuf[pl.ds(0, tc), :] = dcv
        a = a_ref[...].astype(F32)
        sg = _sig(g_ref[...].astype(F32))
        ubuf[pl.ds(0, CONV_HALO), :] = jnp.where(ct > 0, ap_ref[...].astype(F32) * _sig(gp_ref[...].astype(F32)), 0.0)
        ubuf[pl.ds(CONV_HALO, tc), :] = a * sg
        du = jnp.zeros((tc, CONV_WIDTH), F32)
        for k in range(CONV_K):
            du = du + w_ref[pl.ds(k, 1), :] * dbuf[pl.ds(CONV_K - 1 - k, tc), :]
            dw_ref[pl.ds(k, 1), :] += jnp.sum(dcv * ubuf[pl.ds(CONV_HALO - (CONV_K - 1) + k, tc), :],
                                             axis=0, keepdims=True)
        da_ref[...] = (du * sg).astype(da_ref.dtype)
        dg_ref[...] = (du * a * sg * (1.0 - sg)).astype(dg_ref.dtype)

    row = lambda b, c: (b * nc + (nc - 1 - c), 0)
    full = lambda b, c: (0, 0)
    vec = pl.BlockSpec((1, CONV_WIDTH), full)
    blk = pl.BlockSpec((tc, CONV_WIDTH), row)

    def zcol(col):
        return pl.BlockSpec((tc, CONV_WIDTH), lambda b, c: (b * nc + (nc - 1 - c), col))

    def zprev(col):
        return pl.BlockSpec((CONV_HALO, CONV_WIDTH),
                            lambda b, c: (jnp.maximum((b * nc + (nc - 1 - c)) * hb - 1, 0), col))

    o = jax.ShapeDtypeStruct((t, CONV_WIDTH), BF16)
    v = jax.ShapeDtypeStruct((1, CONV_WIDTH), F32)
    return pl.pallas_call(
        body, grid=(nbatch, nc),
        in_specs=[blk, blk, zcol(4), zcol(5), zprev(4), zprev(5), pl.BlockSpec((CONV_HALO, CONV_WIDTH), full), vec, vec],
        out_specs=[blk, blk, pl.BlockSpec((CONV_HALO, CONV_WIDTH), full), vec, vec, vec],
        out_shape=[o, o, jax.ShapeDtypeStruct((CONV_HALO, CONV_WIDTH), F32), v, v, v],
        scratch_shapes=[pltpu.VMEM((CONV_HALO + tc, CONV_WIDTH), F32), pltpu.VMEM((tc + CONV_HALO, CONV_WIDTH), F32)],
        name="conv_bwd", compiler_params=_params(("arbitrary", "arbitrary")))(dsc, cv, z, z, z, z, w, lg, lb)


BIG = ("w_in", "w_attn_out", "w_ssm_glu", "w_conv_out", "w_mix_out", "w_ffn_in", "w_ffn_out", "w_ple_in", "w_ple_gate")
BIG_AXIS = {"w_in": 2, "w_attn_out": 2, "w_ssm_glu": 2, "w_conv_out": 2, "w_mix_out": 1, "w_ffn_in": 2,
            "w_ffn_out": 1, "w_ple_in": 2, "w_ple_gate": 1}
SHARD_MAJOR = ("w_in", "w_ffn_in")
SMALL = ("mix_norm_g", "b_gate", "attn_sinks", "ssm_lambda_re", "ssm_lambda_im", "ssm_log_dt", "ssm_b_re", "ssm_b_im",
         "ssm_c_re", "ssm_c_im", "ssm_d", "b_ssm_glu", "conv_dw_w", "conv_dw_b", "conv_norm_g", "conv_norm_b",
         "ffn_norm_g", "ple_norm_g", "final_norm_g")
WEIGHTS = ("mix_norm_g", "w_in", "b_gate", "attn_sinks", "w_attn_out", "ssm_lambda_re", "ssm_lambda_im", "ssm_log_dt",
           "ssm_b_re", "ssm_b_im", "ssm_c_re", "ssm_c_im", "ssm_d", "w_ssm_glu", "b_ssm_glu", "conv_dw_w", "conv_dw_b",
           "conv_norm_g", "conv_norm_b", "w_conv_out", "w_mix_out", "ffn_norm_g", "w_ffn_in", "w_ffn_out", "w_ple_in",
           "ple_norm_g", "w_ple_gate", "final_norm_g")
SSM_NAMES = ("ssm_lambda_re", "ssm_lambda_im", "ssm_log_dt", "ssm_b_re", "ssm_b_im", "ssm_c_re", "ssm_c_im")


def _heads(v, nh):
    return v.reshape(v.shape[0], nh, HEAD_DIM).transpose(1, 0, 2)


def _tokens(v):
    return v.transpose(1, 0, 2).reshape(v.shape[1], v.shape[0] * HEAD_DIM)


def _row(v):
    return v.reshape(1, -1)


def _layer_fwd(x, p_l, w, s, rope, nbatch, seq):
    t = x.shape[0]
    tm = 512
    d = D_MODEL
    sv = {}
    sv["x"] = x
    h = _rowwise("rms_mix", _rms_fwd, [R(x), V(_row(s["mix_norm_g"]))], [O(d, BF16)], tm=tm)
    cs = {nm: w[nm].shape[2] for nm in SHARD_MAJOR}
    tb = 1024
    z = _mm("mm_in", h, w["w_in"], "nn", BF16, m=t, n=N_CHIPS * cs["w_in"], k=d, tm=tb, tn=cs["w_in"], tk=d,
            b_sh=cs["w_in"])
    sv["h"], sv["z"] = h, z
    c, sa, sb = rope
    qr, kr = _rowwise("rope_fwd", _rope_fwd, [R(z, Q_WIDTH, 0), R(z, KV_WIDTH, 4), R(c), R(sa), R(sb)],
                      [O(Q_WIDTH, BF16), O(KV_WIDTH, BF16)], tm=tm)
    qh, kh = _heads(qr, N_Q_HEADS), _heads(kr, N_KV_HEADS)
    vh = _heads(z[:, Q_WIDTH + KV_WIDTH:Q_WIDTH + 2 * KV_WIDTH], N_KV_HEADS)
    sinks = s["attn_sinks"].reshape(N_Q_HEADS, 1, 1)
    oh, lse = _attn_fwd(qh, kh, vh, sinks, nbatch, seq)
    o = _tokens(oh)
    ya = _mm("mm_attn_out", o, w["w_attn_out"], "nn", BF16, m=t, n=d, k=Q_WIDTH, tm=tb, tn=d, tk=Q_WIDTH)
    sv.update(qh=qh, kh=kh, vh=vh, oh=oh, lse=lse, o=o, ya=ya, sinks=sinks)
    ssm_args = [s[nm] for nm in SSM_NAMES]
    a_re, a_im, wb, wc = _ssm_prep(*ssm_args)
    dskip = _row(s["ssm_d"])
    st, ys, gel = _ssm_fwd(z, wb.astype(BF16), wc.astype(BF16), _ssm_tables(a_re, a_im, False), dskip, nbatch, seq)
    glu = _mm("mm_glu", gel, w["w_ssm_glu"], "nn", BF16, m=t, n=2 * d, k=SSM_WIDTH, tm=tb, tn=2 * d, tk=SSM_WIDTH,
              bias=_row(s["b_ssm_glu"]))
    sv.update(st=st, ys=ys, gel=gel, glu=glu, a=(a_re, a_im), wb=wb, wc=wc, dskip=dskip)
    cw = jnp.pad(s["conv_dw_w"], ((0, CONV_HALO - CONV_K), (0, 0)))
    cv, sc = _conv_fwd(z, cw, _row(s["conv_dw_b"]), _row(s["conv_norm_g"]), _row(s["conv_norm_b"]), nbatch, seq)
    yc = _mm("mm_conv_out", sc, w["w_conv_out"], "nn", BF16, m=t, n=d, k=CONV_WIDTH, tm=tb, tn=d, tk=CONV_WIDTH)
    sv.update(cw=cw, cv=cv, sc=sc, yc=yc)
    bg = _row(s["b_gate"])
    merge_ins = [R(z, 512, 3), R(z, 512, 5), R(z, 512, 7), V(bg, 512, 0), V(bg, 512, 2), V(bg, 512, 4),
                 R(ya, 512, 0), R(glu, 512, 0), R(glu, 512, 2), R(yc, 512, 0)]
    merged = _rowwise("merge_fwd", _merge_fwd, merge_ins, [O(512, BF16, total=d)], tm=tm, ncol=2)
    x1 = _mm("mm_mix", merged, w["w_mix_out"], "nn", F32, m=t, n=d, k=d, tm=tb, tn=d, tk=d, res=x)
    sv.update(merged=merged, x1=x1)
    hf = _rowwise("rms_ffn", _rms_fwd, [R(x1), V(_row(s["ffn_norm_g"]))], [O(d, BF16)], tm=tm)
    f = _mm("mm_ffn_in", hf, w["w_ffn_in"], "nn", BF16, m=t, n=2 * FFN_HIDDEN, k=d, tm=tb, tn=cs["w_ffn_in"], tk=d,
            b_sh=cs["w_ffn_in"])
    act = (lambda i, j, kk, fg, fu: _ffn_act(fg, fu), [(f, lambda i, j, kk: (i, 0)), (f, lambda i, j, kk: (i, 1))])
    x2 = _mm("mm_ffn_out", act, w["w_ffn_out"], "nn", F32, m=t, n=d, k=FFN_HIDDEN, tm=256, tn=d, tk=FFN_HIDDEN, res=x1)
    sv.update(hf=hf, f=f, x2=x2)
    e = _mm("mm_ple_in", p_l, w["w_ple_in"], "nn", BF16, m=t, n=d, k=p_l.shape[1], tm=tb, tn=d, tk=p_l.shape[1])
    hp = _rowwise("rms_ple", _rms_fwd, [R(x2), V(_row(s["ple_norm_g"]))], [O(d, BF16)], tm=tm)
    gp = _mm("mm_ple_gate", hp, w["w_ple_gate"], "nn", BF16, m=t, n=d, k=d, tm=tb, tn=d, tk=d)
    x3 = _rowwise("ple_fwd", _ple_fwd, [R(x2), R(gp), R(e)], [O(d, F32)], tm=tm)
    sv.update(e=e, hp=hp, gp=gp, p=p_l)
    return x3, sv


def _layer_bwd(dx3, sv, w, s, rope, nbatch, seq):
    t = dx3.shape[0]
    tm = 512
    d = D_MODEL
    gb, gs = {}, {}
    cs = {nm: w[nm].shape[2] for nm in SHARD_MAJOR}
    tb = 1024

    def wg(name, a, b, m, n, tm=1024, tk=1024, shard=None):
        return _mm(name, a, b, "tn", BF16, m=m, n=n, k=t, tm=tm, tn=n if shard is None else cs[shard], tk=tk,
                   o_sh=None if shard is None else cs[shard])

    de, dgp = _rowwise("ple_bwd", _ple_bwd, [R(dx3), R(sv["gp"]), R(sv["e"])], [O(d, BF16), O(d, BF16)], tm=tm)
    gb["w_ple_in"] = wg("wg_ple_in", sv["p"], de, sv["p"].shape[1], d, tk=2048)
    gb["w_ple_gate"] = wg("wg_ple_gate", sv["hp"], dgp, d, d, tk=2048)
    dhp = _mm("mmb_ple_gate", dgp, w["w_ple_gate"], "nt", BF16, m=t, n=d, k=d, tm=tb, tn=d, tk=d)
    dx2, gs["ple_norm_g"] = _rowwise("rms_ple_bwd", _rms_bwd, [R(dhp), R(sv["x2"]), R(dx3), V(_row(s["ple_norm_g"]))],
                                     [O(d, F32)], [A(d)], tm=tm)
    fw = FFN_HIDDEN // 2
    dact = _mm("mmb_ffn_out", dx2, w["w_ffn_out"], "nt", BF16, m=t, n=FFN_HIDDEN, k=d, tm=tb, tn=fw, tk=d)
    f = sv["f"]
    act_t = (lambda i, j, kk, fg, fu: _ffn_act(fg, fu), [(f, lambda i, j, kk: (kk, i)), (f, lambda i, j, kk: (kk, 2 + i))])
    gb["w_ffn_out"] = _mm("wg_ffn_out", act_t, dx2, "tn", BF16, m=FFN_HIDDEN, n=d, k=t, tm=fw, tn=d, tk=512)

    def df_tile(is_gate, da, fg, fu):
        dfg, dfu = _ffn_act_bwd(da, fg, fu)
        return jnp.where(is_gate, dfg, dfu)

    assert cs["w_ffn_in"] == fw
    df_cols = (lambda i, j, kk, *v: df_tile(j < 2, *v),
               [(dact, lambda i, j, kk: (kk, j % 2)), (f, lambda i, j, kk: (kk, j % 2)), (f, lambda i, j, kk: (kk, 2 + j % 2))])
    gb["w_ffn_in"] = _mm("wg_ffn_in", sv["hf"], df_cols, "tn", BF16, m=d, n=2 * FFN_HIDDEN, k=t, tm=d, tn=fw, tk=512,
                         o_sh=fw)
    df_rows = (lambda i, j, kk, *v: df_tile(kk < 2, *v),
               [(dact, lambda i, j, kk: (i, kk % 2)), (f, lambda i, j, kk: (i, kk % 2)), (f, lambda i, j, kk: (i, 2 + kk % 2))])
    dhf = _mm("mmb_ffn_in", df_rows, w["w_ffn_in"], "nt", BF16, m=t, n=d, k=2 * FFN_HIDDEN, tm=512, tn=d, tk=fw, b_sh=fw)
    dx1, gs["ffn_norm_g"] = _rowwise("rms_ffn_bwd", _rms_bwd, [R(dhf), R(sv["x1"]), R(dx2), V(_row(s["ffn_norm_g"]))],
                                     [O(d, F32)], [A(d)], tm=tm)
    dm = _mm("mmb_mix", dx1, w["w_mix_out"], "nt", BF16, m=t, n=d, k=d, tm=tb, tn=d, tk=d)
    gb["w_mix_out"] = wg("wg_mix", sv["merged"], dx1, d, d)
    z, glu, bg = sv["z"], sv["glu"], _row(s["b_gate"])
    ins = [R(dm, 512, 0), R(z, 512, 3), R(z, 512, 5), R(z, 512, 7), V(bg, 512, 0), V(bg, 512, 2), V(bg, 512, 4),
           R(sv["ya"], 512, 0), R(glu, 512, 0), R(glu, 512, 2), R(sv["yc"], 512, 0)]
    ob = lambda: O(512, BF16, total=d)
    ab = lambda: A(512, total=d)
    dya, dga, dgb, dyc, d0, d1, d2, db0, db1, db2, dba, dbb = _rowwise(
        "merge_bwd", _merge_bwd, ins, [ob() for _ in range(7)], [ab() for _ in range(5)], tm=tm, ncol=2)
    gs["b_gate"] = jnp.concatenate([db0, db1, db2], axis=1)
    gs["b_ssm_glu"] = jnp.concatenate([dba, dbb], axis=1)
    dglu = jnp.concatenate([dga, dgb], axis=1)
    gb["w_attn_out"] = wg("wg_attn_out", sv["o"], dya, Q_WIDTH, d, tk=2048)
    do = _mm("mmb_attn_out", dya, w["w_attn_out"], "nt", BF16, m=t, n=Q_WIDTH, k=d, tm=tb, tn=Q_WIDTH, tk=d)
    dqh, dkc, dvc, dkp, dvp, dsink = _attn_bwd(sv["qh"], sv["kh"], sv["vh"], sv["oh"], _heads(do, N_Q_HEADS),
                                               sv["lse"], sv["sinks"], nbatch, seq)
    gs["attn_sinks"] = dsink.reshape(-1)
    c, sa, sb = rope
    dq = _rowwise("rope_bwd_q", _rope_bwd_q, [R(_tokens(dqh)), R(c), R(sa), R(sb)], [O(Q_WIDTH, BF16)], tm=tm)
    dk, dv = _kv_combine(_tokens(dkc), _tokens(dkp), _tokens(dvc), _tokens(dvp), c, sa, sb, seq)
    gb["w_ssm_glu"] = wg("wg_ssm_glu", sv["gel"], dglu, SSM_WIDTH, 2 * d, tk=2048)
    dgi = _mm("mmb_glu", dglu, w["w_ssm_glu"], "nt", BF16, m=t, n=SSM_WIDTH, k=2 * d, tm=tb, tn=SSM_WIDTH, tk=2 * d)
    a_re, a_im = sv["a"]
    du, dwb, dwc, dd, da = _ssm_bwd(dgi, sv["ys"], sv["st"], z, sv["wb"].T.astype(BF16), sv["wc"].T.astype(BF16),
                                    _ssm_tables(a_re, a_im, True), sv["dskip"], nbatch, seq)
    gs["ssm_d"] = dd.reshape(-1)
    da = jnp.sum(da, axis=0)
    _, prep_vjp = jax.vjp(_ssm_prep, *[s[nm] for nm in SSM_NAMES])
    for nm, g in zip(SSM_NAMES, prep_vjp((da[:SSM_LANES], da[SSM_LANES:], dwb, dwc))):
        gs[nm] = g
    gb["w_conv_out"] = wg("wg_conv_out", sv["sc"], dyc, CONV_WIDTH, d, tk=2048)
    dsc = _mm("mmb_conv_out", dyc, w["w_conv_out"], "nt", BF16, m=t, n=CONV_WIDTH, k=d, tm=tb, tn=CONV_WIDTH, tk=d)
    dca, dcg, dcw, dcb, dlg, dlb = _conv_bwd(dsc, sv["cv"], z, sv["cw"], _row(s["conv_norm_g"]),
                                             _row(s["conv_norm_b"]), nbatch, seq)
    gs["conv_dw_w"] = dcw[:CONV_K]
    gs["conv_dw_b"], gs["conv_norm_g"], gs["conv_norm_b"] = dcb.reshape(-1), dlg.reshape(-1), dlb.reshape(-1)
    dz = jnp.concatenate([dq, dk, dv, du, dca, dcg, d0, d1, d2], axis=1)
    gb["w_in"] = wg("wg_in", sv["h"], dz, d, dz.shape[1], tk=2048, shard="w_in")
    dh = _mm("mmb_in", dz, w["w_in"], "nt", BF16, m=t, n=d, k=dz.shape[1], tm=tb, tn=d, tk=cs["w_in"],
             b_sh=cs["w_in"])
    dx, gs["mix_norm_g"] = _rowwise("rms_mix_bwd", _rms_bwd, [R(dh), R(sv["x"]), R(dx1), V(_row(s["mix_norm_g"]))],
                                    [O(d, F32)], [A(d)], tm=tm)
    gs["mix_norm_g"], gs["ffn_norm_g"], gs["ple_norm_g"] = (gs[nm].reshape(-1) for nm in
                                                            ("mix_norm_g", "ffn_norm_g", "ple_norm_g"))
    gs["b_gate"], gs["b_ssm_glu"] = gs["b_gate"].reshape(-1), gs["b_ssm_glu"].reshape(-1)
    return dx, gb, gs


def _rope_tables(positions):
    inv_freq = ROPE_THETA ** (-jnp.arange(0, ROPE_DIM, 2, dtype=F32) / ROPE_DIM)
    ang = positions.reshape(-1).astype(F32)[:, None] * inv_freq
    cos, sin = jnp.cos(ang), jnp.sin(ang)
    t = ang.shape[0]
    rest = HEAD_DIM - ROPE_DIM
    c = jnp.concatenate([cos, cos, jnp.ones((t, rest), F32)], axis=1)
    sa = jnp.concatenate([-sin, jnp.zeros((t, HEAD_DIM - ROPE_HALF), F32)], axis=1)
    sb = jnp.concatenate([jnp.zeros((t, ROPE_HALF), F32), sin, jnp.zeros((t, rest), F32)], axis=1)
    two = lambda v: jnp.concatenate([v, v], axis=1)
    return two(c), two(sa), two(sb)


def _natural(nm, w4):
    if nm in SHARD_MAJOR:
        return w4
    if BIG_AXIS[nm] == 1:
        return w4.reshape(-1, w4.shape[2])
    return w4.transpose(1, 0, 2).reshape(w4.shape[1], -1)


def _shard_major(nm, g):
    if nm in SHARD_MAJOR:
        return g
    if BIG_AXIS[nm] == 1:
        return g.reshape(N_CHIPS, -1, g.shape[1])
    return g.reshape(g.shape[0], N_CHIPS, -1).transpose(1, 0, 2)


def _local_step(x, p, positions, loss_target, wfull, small):
    nbatch, seq, d = x.shape
    depth = p.shape[0]
    t = nbatch * seq
    rope = _rope_tables(positions)
    xs = x.reshape(t, d)
    saved, ws, ss = [], [], []
    for l in range(depth):
        w_l = {nm: _natural(nm, wfull[nm][l]) for nm in BIG}
        s_l = {nm: small[nm][l] for nm in SMALL if nm != "final_norm_g"}
        xs, sv = _layer_fwd(xs, p[l].reshape(t, -1), w_l, s_l, rope, nbatch, seq)
        saved.append(sv)
        ws.append(w_l)
        ss.append(s_l)
    dx, loss_cols, dgf = _rowwise("loss_head", _loss_fn, [R(xs), R(loss_target.reshape(t, d)),
                                                          V(_row(small["final_norm_g"]))],
                                  [O(d, F32)], [A(d), A(d)], tm=512)
    gbs, gss = [], []
    for l in reversed(range(depth)):
        dx, gb, gs = _layer_bwd(dx, saved[l], ws[l], ss[l], rope, nbatch, seq)
        gbs.insert(0, gb)
        gss.insert(0, gs)
    gbig = {nm: jnp.stack([_shard_major(nm, g[nm]) for g in gbs]) for nm in BIG}
    gsmall = {nm: jnp.stack([g[nm] for g in gss]) for nm in SMALL if nm != "final_norm_g"}
    gsmall["final_norm_g"] = dgf.reshape(-1)
    return loss_cols, dx.reshape(nbatch, seq, d), gbig, gsmall


HBM = pl.BlockSpec(memory_space=pltpu.HBM)


def _place():
    x, y, c = lax.axis_index("x"), lax.axis_index("y"), lax.axis_index("c")
    chips = [(1 - x, y), (x, 1 - y), (1 - x, 1 - y)]
    return x, y, c, chips


def _remote(src, dst, send_sem, recv_sem, to):
    return pltpu.make_async_remote_copy(src_ref=src, dst_ref=dst, send_sem=send_sem, recv_sem=recv_sem,
                                        device_id=to, device_id_type=MESH)


def _gather_shards(shards):
    n = len(shards)
    hl = shards[0].shape[0] // 2

    def body(*refs):
        ins, outs = refs[:n], refs[n:2 * n]
        send1, recv1, send2, recv2, send0, recv0 = refs[2 * n:]
        x, y, c, chips = _place()
        me = 2 * x + y
        mine, other = pl.ds(c * hl, hl), pl.ds((1 - c) * hl, hl)
        sib = (x, y, 1 - c)
        local = [_remote(ins[i], outs[i].at[:, me], send0.at[i], recv0.at[i], sib) for i in range(n)]
        for cp in local:
            cp.start()
        first = [[_remote(ins[i].at[mine], outs[i].at[mine, me], send1.at[i, k], recv1.at[i, k], (cx, cy, c))
                  for k, (cx, cy) in enumerate(chips)] for i in range(n)]
        for row in first:
            for cp in row:
                cp.start()
        passed = []
        for i in range(n):
            for k, (cx, cy) in enumerate(chips):
                slot = outs[i].at[mine, 2 * cx + cy]
                _remote(slot, slot, send1.at[i, k], recv1.at[i, k], (cx, cy, c)).wait_recv()
                cp = _remote(slot, slot, send2.at[i, k], recv2.at[i, k], sib)
                cp.start()
                passed.append(cp)
        for i in range(n):
            for k, (cx, cy) in enumerate(chips):
                slot = outs[i].at[other, 2 * cx + cy]
                _remote(slot, slot, send2.at[i, k], recv2.at[i, k], sib).wait_recv()
        for row in first:
            for cp in row:
                cp.wait_send()
        for cp in passed:
            cp.wait_send()
        for cp in local:
            cp.wait()

    out_shape = [jax.ShapeDtypeStruct((s.shape[0], N_CHIPS) + s.shape[1:], s.dtype) for s in shards]
    sems = [pltpu.SemaphoreType.DMA((n, 3)) for _ in range(4)] + [pltpu.SemaphoreType.DMA((n,)) for _ in range(2)]
    return pl.pallas_call(body, out_shape=out_shape, in_specs=[HBM] * n, out_specs=[HBM] * n, scratch_shapes=sems,
                          name="gather_weights")(*shards)


def _pair_exchange(grads):
    n = len(grads)
    hl = grads[0].shape[0] // 2

    def body(*refs):
        ins, outs = refs[:n], refs[n:2 * n]
        send, recv = refs[2 * n:]
        x, y, c, _ = _place()
        other = pl.ds((1 - c) * hl, hl)
        cps = [_remote(ins[i].at[other], outs[i], send.at[i], recv.at[i], (x, y, 1 - c)) for i in range(n)]
        for cp in cps:
            cp.start()
        for cp in cps:
            cp.wait()

    out_shape = [jax.ShapeDtypeStruct((hl,) + g.shape[1:], g.dtype) for g in grads]
    sems = [pltpu.SemaphoreType.DMA((n,)) for _ in range(2)]
    return pl.pallas_call(body, out_shape=out_shape, in_specs=[HBM] * n, out_specs=[HBM] * n, scratch_shapes=sems,
                          name="reduce_pair_exchange")(*grads)


def _pair_add(g, r):
    hl, _, rr, cc = r.shape
    rows = hl * N_CHIPS * rr
    nblk = rows // rr

    def body(c_ref, g_ref, r_ref, o_ref):
        o_ref[...] = (g_ref[...].astype(F32) + r_ref[...].astype(F32)).astype(o_ref.dtype)

    grid_spec = pltpu.PrefetchScalarGridSpec(
        num_scalar_prefetch=1, grid=(nblk,),
        in_specs=[pl.BlockSpec((rr, cc), lambda i, c_ref: (c_ref[0] * nblk + i, 0)),
                  pl.BlockSpec((rr, cc), lambda i, c_ref: (i, 0))],
        out_specs=pl.BlockSpec((rr, cc), lambda i, c_ref: (i, 0)))
    c = lax.axis_index("c").astype(jnp.int32).reshape(1)
    out = pl.pallas_call(body, out_shape=jax.ShapeDtypeStruct((rows, cc), r.dtype), grid_spec=grid_spec,
                         name="reduce_pair_add", compiler_params=_params(("parallel",)))(
        c, g.reshape(-1, cc), r.reshape(rows, cc))
    return out.reshape(r.shape)


def _chip_exchange(psums):
    n = len(psums)

    def body(*refs):
        ins, got = refs[:n], refs[n:2 * n]
        send, recv = refs[2 * n:]
        x, y, c, chips = _place()
        cps = [_remote(ins[i].at[:, 2 * cx + cy], got[i].at[k], send.at[i, k], recv.at[i, k], (cx, cy, c))
               for i in range(n) for k, (cx, cy) in enumerate(chips)]
        for cp in cps:
            cp.start()
        for cp in cps:
            cp.wait()

    got_shape = [jax.ShapeDtypeStruct((3, p.shape[0]) + p.shape[2:], p.dtype) for p in psums]
    sems = [pltpu.SemaphoreType.DMA((n, 3)), pltpu.SemaphoreType.DMA((n, 3))]
    return pl.pallas_call(body, out_shape=got_shape, in_specs=[HBM] * n, out_specs=[HBM] * n, scratch_shapes=sems,
                          name="reduce_chip_exchange")(*psums)


def _sum4(psum, got):
    hl, _, rr, cc = psum.shape
    tr = rr if rr * cc <= 512 * 1024 else rr // 2

    def body(place_ref, own_ref, g0_ref, g1_ref, g2_ref, o_ref):
        tot = (own_ref[...].astype(F32) + g0_ref[...].astype(F32)) + g1_ref[...].astype(F32)
        o_ref[...] = tot + g2_ref[...].astype(F32)

    def got_spec(k):
        return pl.BlockSpec((None, None, tr, cc), lambda h, i, place: (k, h, i, 0))

    grid_spec = pltpu.PrefetchScalarGridSpec(
        num_scalar_prefetch=1, grid=(hl, rr // tr),
        in_specs=[pl.BlockSpec((None, None, tr, cc), lambda h, i, place: (h, place[0], i, 0)),
                  got_spec(0), got_spec(1), got_spec(2)],
        out_specs=pl.BlockSpec((None, tr, cc), lambda h, i, place: (place[1] * hl + h, i, 0)))
    place = jnp.stack([2 * lax.axis_index("x") + lax.axis_index("y"), lax.axis_index("c")]).astype(jnp.int32)
    return pl.pallas_call(body, out_shape=jax.ShapeDtypeStruct((2 * hl, rr, cc), F32), grid_spec=grid_spec,
                          name="reduce_sum4", compiler_params=_params(("parallel", "parallel")))(
        place, psum, got, got, got)


def _pair_gather(sums):
    n = len(sums)
    hl = sums[0].shape[0] // 2

    def body(*refs):
        bufs = refs[n:2 * n]
        send, recv = refs[2 * n:]
        x, y, c, _ = _place()
        mine = pl.ds(c * hl, hl)
        cps = [_remote(bufs[i].at[mine], bufs[i].at[mine], send.at[i], recv.at[i], (x, y, 1 - c)) for i in range(n)]
        for cp in cps:
            cp.start()
        for cp in cps:
            cp.wait()

    out_shape = [jax.ShapeDtypeStruct(v.shape, v.dtype) for v in sums]
    sems = [pltpu.SemaphoreType.DMA((n,)) for _ in range(2)]
    return pl.pallas_call(body, out_shape=out_shape, in_specs=[HBM] * n, out_specs=[HBM] * n, scratch_shapes=sems,
                          input_output_aliases={i: i for i in range(n)}, name="reduce_pair_gather")(*sums)


def _allreduce_small(vec):
    rows = vec.shape[0]

    def body(v_ref, o_ref, all_ref, send, recv):
        x, y, c, _ = _place()
        me = 4 * x + 2 * y + c
        all_ref[me] = v_ref[...]
        cps = []
        for dlt in range(1, N_DEV):
            fx, fy, fc = (dlt >> 2) & 1, (dlt >> 1) & 1, dlt & 1
            to = (1 - x if fx else x, 1 - y if fy else y, 1 - c if fc else c)
            cps.append(_remote(v_ref, all_ref.at[me], send.at[dlt - 1], recv.at[dlt - 1], to))
        for cp in cps:
            cp.start()
        for cp in cps:
            cp.wait()
        tot = all_ref[0]
        for dev in range(1, N_DEV):
            tot = tot + all_ref[dev]
        o_ref[...] = tot

    vm = pl.BlockSpec(memory_space=pltpu.VMEM)
    return pl.pallas_call(
        body, out_shape=jax.ShapeDtypeStruct(vec.shape, F32), in_specs=[vm], out_specs=vm,
        scratch_shapes=[pltpu.VMEM((N_DEV, rows, 128), F32), pltpu.SemaphoreType.DMA((N_DEV - 1,)),
                        pltpu.SemaphoreType.DMA((N_DEV - 1,))],
        name="allreduce_small", compiler_params=pltpu.CompilerParams(vmem_limit_bytes=VMEM_LIMIT))(vec)


def _adamw(name, w, g, m, v):
    rows, cc = w.shape
    tm = math.gcd(rows, 256)
    return _rowwise(name, _adamw_fn, [R(w), R(g), R(m), R(v)], [O(cc, F32), O(cc, F32), O(cc, F32)], tm=tm)


def _pack(parts):
    flat = jnp.concatenate([v.reshape(-1).astype(F32) for v in parts])
    pad = (-flat.shape[0]) % (SUBLANES * 128)
    return jnp.pad(flat, (0, pad)).reshape(-1, 128)


def _unpack(packed, shapes):
    flat, out, pos = packed.reshape(-1), [], 0
    for shp in shapes:
        size = math.prod(shp)
        out.append(flat[pos:pos + size].reshape(shp))
        pos += size
    return out


def kernel(x, p, positions, mix_norm_g, w_in, b_gate, attn_sinks, w_attn_out, ssm_lambda_re, ssm_lambda_im, ssm_log_dt, ssm_b_re, ssm_b_im, ssm_c_re, ssm_c_im, ssm_d, w_ssm_glu, b_ssm_glu, conv_dw_w, conv_dw_b, conv_norm_g, conv_norm_b, w_conv_out, w_mix_out, ffn_norm_g, w_ffn_in, w_ffn_out, w_ple_in, ple_norm_g, w_ple_gate, final_norm_g, loss_target, m_mix_norm_g, m_w_in, m_b_gate, m_attn_sinks, m_w_attn_out, m_ssm_lambda_re, m_ssm_lambda_im, m_ssm_log_dt, m_ssm_b_re, m_ssm_b_im, m_ssm_c_re, m_ssm_c_im, m_ssm_d, m_w_ssm_glu, m_b_ssm_glu, m_conv_dw_w, m_conv_dw_b, m_conv_norm_g, m_conv_norm_b, m_w_conv_out, m_w_mix_out, m_ffn_norm_g, m_w_ffn_in, m_w_ffn_out, m_w_ple_in, m_ple_norm_g, m_w_ple_gate, m_final_norm_g, v_mix_norm_g, v_w_in, v_b_gate, v_attn_sinks, v_w_attn_out, v_ssm_lambda_re, v_ssm_lambda_im, v_ssm_log_dt, v_ssm_b_re, v_ssm_b_im, v_ssm_c_re, v_ssm_c_im, v_ssm_d, v_w_ssm_glu, v_b_ssm_glu, v_conv_dw_w, v_conv_dw_b, v_conv_norm_g, v_conv_norm_b, v_w_conv_out, v_w_mix_out, v_ffn_norm_g, v_w_ffn_in, v_w_ffn_out, v_w_ple_in, v_ple_norm_g, v_w_ple_gate, v_final_norm_g):
    given = dict(locals())
    wts = {nm: given[nm] for nm in WEIGHTS}
    mom = {nm: given["m_" + nm] for nm in WEIGHTS}
    var = {nm: given["v_" + nm] for nm in WEIGHTS}
    depth = p.shape[0]
    chip = 2 * lax.axis_index("x") + lax.axis_index("y")

    cw_cols = conv_dw_w.shape[2]
    taps = jnp.pad(conv_dw_w.reshape(depth, -1), ((0, 0), (0, (-CONV_K * cw_cols) % (SUBLANES * 128))))
    gathered = _gather_shards([wts[nm].astype(BF16) for nm in BIG] + [taps.reshape(depth, -1, 128)])
    wfull = dict(zip(BIG, gathered[:-1]))
    taps_all = gathered[-1].reshape(depth, N_CHIPS, -1)[:, :, :CONV_K * cw_cols]
    small = {nm: wts[nm] for nm in SMALL}
    small["conv_dw_w"] = taps_all.reshape(depth, N_CHIPS, CONV_K, cw_cols).transpose(0, 2, 1, 3).reshape(
        depth, CONV_K, N_CHIPS * cw_cols)

    loss_cols, grad_x, gbig, gsmall = _local_step(x, p, positions, loss_target, wfull, small)

    parts = [loss_cols] + [gsmall[nm] for nm in SMALL]
    total = _allreduce_small(_pack(parts))
    summed = _unpack(total, [v.shape for v in parts])
    loss = jnp.sum(summed[0])
    gsum = dict(zip(SMALL, summed[1:]))
    gsum["conv_dw_w"] = lax.dynamic_slice_in_dim(gsum["conv_dw_w"], chip * cw_cols, cw_cols, axis=2)
    shapes = [wts[nm].shape for nm in SMALL]
    deltas, new_m, new_v = _adamw("adamw_small", _pack([wts[nm] for nm in SMALL]), _pack([gsum[nm] for nm in SMALL]),
                                  _pack([mom[nm] for nm in SMALL]), _pack([var[nm] for nm in SMALL]))
    grads = dict(gsum)
    delta = dict(zip(SMALL, _unpack(deltas, shapes)))
    newm = dict(zip(SMALL, _unpack(new_m, shapes)))
    newv = dict(zip(SMALL, _unpack(new_v, shapes)))

    gl = [gbig[nm] for nm in BIG]
    sib = _pair_exchange(gl)
    psums = [_pair_add(g, r) for g, r in zip(gl, sib)]
    got = _chip_exchange(psums)
    sums = _pair_gather([_sum4(ps, g) for ps, g in zip(psums, got)])
    for nm, g in zip(BIG, sums):
        shp = wts[nm].shape
        two = lambda v: v.reshape(-1, shp[-1])
        g = g.reshape(shp)
        d_w, n_m, n_v = _adamw("adamw_" + nm, two(wts[nm]), two(g), two(mom[nm]), two(var[nm]))
        grads[nm], delta[nm], newm[nm], newv[nm] = g, d_w.reshape(shp), n_m.reshape(shp), n_v.reshape(shp)

    return (loss, grad_x, *[grads[nm] for nm in WEIGHTS], *[delta[nm] for nm in WEIGHTS],
            *[newm[nm] for nm in WEIGHTS], *[newv[nm] for nm in WEIGHTS])
```

```python
import functools
import math

import jax
import jax.numpy as jnp
from jax import lax
from jax.experimental import pallas as pl
from jax.experimental.pallas import tpu as pltpu

F32 = jnp.float32
BF16 = jnp.bfloat16

D_MODEL = 1024
HEAD_DIM = 64
N_Q_HEADS = 8
N_KV_HEADS = 2
GQA_GROUP = N_Q_HEADS // N_KV_HEADS
ATT_BLOCK = 128
ROPE_THETA = 500000.0
ROPE_DIM = HEAD_DIM // 4
ROPE_HALF = ROPE_DIM // 2
Q_WIDTH = N_Q_HEADS * HEAD_DIM
KV_WIDTH = N_KV_HEADS * HEAD_DIM
SSM_WIDTH = 256
SSM_GROUP = 16
SSM_GROUPS = 16
SSM_STATE = 64
SSM_LANES = SSM_GROUPS * SSM_STATE
CONV_WIDTH = 256
CONV_K = 31
CONV_HALO = 32
FFN_HIDDEN = 2816
EPS = 1e-6
NEG_INF = -1e30
SCALE = HEAD_DIM ** -0.5

ADAM_LR = 0.001
ADAM_B1 = 0.9
ADAM_B2 = 0.999
ADAM_EPS = 1e-08
ADAM_WD = 0.01
ADAM_STEP = 10

N_CHIPS = 4
N_DEV = 8
SUBLANES = 8
VMEM_LIMIT = 56 * 1024 * 1024

MESH = pl.DeviceIdType.MESH


def _params(sem=None):
    return pltpu.CompilerParams(dimension_semantics=sem, vmem_limit_bytes=VMEM_LIMIT)


def R(arr, width=None, cb=0, rb=0):
    return ("r", arr, arr.shape[1] if width is None else width, (cb, rb))


def V(arr, width=None, cb=0):
    return ("v", arr, arr.shape[1] if width is None else width, cb)


def _cbf(cb):
    return cb if callable(cb) else (lambda j, c=cb: c + j)


def _rowwise(name, fn, ins, outs, accs=(), *, tm, ncol=1, rows=None):
    t = rows if rows is not None else [a for k, a, _, _ in ins if k == "r"][0].shape[0]
    tm = min(tm, t)
    assert t % tm == 0, (name, t, tm)
    n_i, n_o, n_a = len(ins), len(outs), len(accs)

    def body(*refs):
        vals = fn(*[r[...] for r in refs[:n_i]])
        if not isinstance(vals, (tuple, list)):
            vals = (vals,)
        for ref, val in zip(refs[n_i:n_i + n_o], vals[:n_o]):
            ref[...] = val.astype(ref.dtype)
        if n_a:
            acc_refs = refs[n_i + n_o:]

            @pl.when(pl.program_id(1) == 0)
            def _():
                for ref in acc_refs:
                    ref[...] = jnp.zeros_like(ref)

            for ref, val in zip(acc_refs, vals[n_o:]):
                ref[...] += val

    in_specs = []
    for kind, arr, width, cb in ins:
        if kind == "r":
            f = _cbf(cb[0])
            in_specs.append(pl.BlockSpec((tm, width), functools.partial(lambda j, i, f, rb: (i + rb, f(j)), f=f, rb=cb[1])))
        else:
            f = _cbf(cb)
            in_specs.append(pl.BlockSpec((arr.shape[0], width), functools.partial(lambda j, i, f: (0, f(j)), f=f)))
    out_specs, out_shape = [], []
    for total, width, cb, dt in outs:
        f = _cbf(cb)
        out_specs.append(pl.BlockSpec((tm, width), functools.partial(lambda j, i, f: (i, f(j)), f=f)))
        out_shape.append(jax.ShapeDtypeStruct((t, total), dt))
    for total, width, cb in accs:
        f = _cbf(cb)
        out_specs.append(pl.BlockSpec((1, width), functools.partial(lambda j, i, f: (0, f(j)), f=f)))
        out_shape.append(jax.ShapeDtypeStruct((1, total), F32))
    sem = ("arbitrary", "arbitrary") if n_a else ("parallel", "parallel")
    res = pl.pallas_call(body, out_shape=out_shape, grid=(ncol, t // tm), in_specs=in_specs, out_specs=out_specs,
                         name=name, compiler_params=_params(sem))(*[a for _, a, _, _ in ins])
    return res[0] if len(res) == 1 else res


def O(width, dtype, total=None, cb=0):
    return (width if total is None else total, width, cb, dtype)


def A(width, total=None, cb=0):
    return (width if total is None else total, width, cb)


_DIMS = {"nn": (((1,), (0,)), ((), ())), "nt": (((1,), (1,)), ((), ())), "tn": (((0,), (0,)), ((), ()))}


def _mm(name, a, b, mode, out_dtype, *, m, n, k, tm, tn, tk, a_off=0, b_off=0, res=None, bias=None, b_sh=None, o_sh=None,
        comm=None):
    tm, tn, tk = min(tm, m), min(tn, n), min(tk, k)
    assert m % tm == 0 and n % tn == 0 and k % tk == 0, (name, m, n, k, tm, tn, tk)
    nk = k // tk
    has_res, has_bias = res is not None, bias is not None
    a_fn, a_ops = a if isinstance(a, tuple) else (None, [(a, None)])
    b_fn, b_ops = b if isinstance(b, tuple) else (None, [(b, None)])
    na, nb_ = len(a_ops), len(b_ops)
    a_bytes = sum(m * k * arr.dtype.itemsize for arr, _ in a_ops)
    b_bytes = sum(n * k * arr.dtype.itemsize for arr, _ in b_ops)
    swap = nk == 1 and b_bytes + (n // tn) * a_bytes < a_bytes + (m // tm) * b_bytes
    grid = (n // tn, m // tm, nk) if swap else (m // tm, n // tn, nk)
    ncomm = 0 if comm is None else len(comm["ins"])

    def body(*refs):
        g0, g1, kk = pl.program_id(0), pl.program_id(1), pl.program_id(2)
        gi, gj = (g1, g0) if swap else (g0, g1)
        a_tiles = [r[...] for r in refs[:na]]
        b_tiles = [r[...] for r in refs[na:na + nb_]]
        a_val = a_tiles[0] if a_fn is None else a_fn(gi, gj, kk, *a_tiles)
        b_val = b_tiles[0] if b_fn is None else b_fn(gi, gj, kk, *b_tiles)
        pos = na + nb_
        res_ref = bias_ref = None
        if has_res:
            res_ref = refs[pos]
            pos += 1
        if has_bias:
            bias_ref = refs[pos]
            pos += 1
        comm_ins = refs[pos:pos + ncomm]
        o_ref = refs[pos + ncomm]
        comm_outs = refs[pos + ncomm + 1:pos + 2 * ncomm + 1]
        scratch = refs[pos + 2 * ncomm + 1:]
        if comm is not None:
            sems = scratch[1:] if nk > 1 else scratch

            @pl.when((g0 == 0) & (g1 == 0) & (kk == 0))
            def _():
                comm["start"](comm_ins, comm_outs, sems)

        def finish(r):
            if has_bias:
                r = r + bias_ref[...]
            if has_res:
                r = r + res_ref[...].astype(F32)
            o_ref[...] = r.astype(o_ref.dtype)

        part = lax.dot_general(a_val.astype(BF16), b_val.astype(BF16), _DIMS[mode], preferred_element_type=F32)
        if nk == 1:
            finish(part)
        else:
            acc_ref = scratch[0]

            @pl.when(kk == 0)
            def _():
                acc_ref[...] = part

            @pl.when(kk > 0)
            def _():
                acc_ref[...] += part

            @pl.when(kk == nk - 1)
            def _():
                finish(acc_ref[...])

        if comm is not None:
            @pl.when((g0 == grid[0] - 1) & (g1 == grid[1] - 1) & (kk == nk - 1))
            def _():
                comm["finish"](comm_ins, comm_outs, sems)

    def at(f):
        return (lambda g0, g1, kk: f(g1, g0, kk)) if swap else f

    if mode == "nn":
        a_spec = pl.BlockSpec((tm, tk), at(lambda i, j, kk: (i, kk + a_off)))
        b_spec = pl.BlockSpec((tk, tn), at(lambda i, j, kk: (kk, j + b_off)))
        if b_sh is not None:
            assert b_sh % tn == 0, (name, b_sh, tn)
            per = b_sh // tn
            b_spec = pl.BlockSpec((None, tk, tn), at(lambda i, j, kk: (j // per, kk, j % per)))
    elif mode == "nt":
        a_spec = pl.BlockSpec((tm, tk), at(lambda i, j, kk: (i, kk + a_off)))
        b_spec = pl.BlockSpec((tn, tk), at(lambda i, j, kk: (j, kk + b_off)))
        if b_sh is not None:
            assert b_sh % tk == 0, (name, b_sh, tk)
            per = b_sh // tk
            b_spec = pl.BlockSpec((None, tn, tk), at(lambda i, j, kk: (kk // per, j, kk % per)))
    else:
        a_spec = pl.BlockSpec((tk, tm), at(lambda i, j, kk: (kk, i + a_off)))
        b_spec = pl.BlockSpec((tk, tn), at(lambda i, j, kk: (kk, j + b_off)))
    a_specs = [a_spec] if a_fn is None else [pl.BlockSpec(a_spec.block_shape, at(f)) for _, f in a_ops]
    b_specs = [b_spec] if b_fn is None else [pl.BlockSpec(b_spec.block_shape, at(f)) for _, f in b_ops]
    in_specs, args = a_specs + b_specs, [arr for arr, _ in a_ops] + [arr for arr, _ in b_ops]
    if has_res:
        in_specs.append(pl.BlockSpec((tm, tn), at(lambda i, j, kk: (i, j))))
        args.append(res)
    if has_bias:
        in_specs.append(pl.BlockSpec((1, tn), at(lambda i, j, kk: (0, j))))
        args.append(bias)
    out_spec, out_shape = pl.BlockSpec((tm, tn), at(lambda i, j, kk: (i, j))), (m, n)
    if o_sh is not None:
        assert o_sh % tn == 0, (name, o_sh, tn)
        per_o = o_sh // tn
        out_spec = pl.BlockSpec((None, tm, tn), at(lambda i, j, kk: (j // per_o, i, j % per_o)))
        out_shape = (n // o_sh, m, o_sh)
    scratch = [pltpu.VMEM((tm, tn), F32)] if nk > 1 else []
    if comm is None:
        return pl.pallas_call(
            body, out_shape=jax.ShapeDtypeStruct(out_shape, out_dtype), grid=grid, in_specs=in_specs,
            out_specs=out_spec, scratch_shapes=scratch, name=name,
            compiler_params=_params(("parallel", "parallel", "arbitrary")))(*args)
    outs = pl.pallas_call(
        body, out_shape=[jax.ShapeDtypeStruct(out_shape, out_dtype)] + comm["out_shapes"], grid=grid,
        in_specs=in_specs + [HBM] * ncomm, out_specs=[out_spec] + [HBM] * ncomm,
        scratch_shapes=scratch + comm["sems"], name=name,
        compiler_params=_params(("arbitrary", "arbitrary", "arbitrary")))(*args, *comm["ins"])
    return outs[0], outs[1:]


def _sig(v):
    return jax.nn.sigmoid(v)


def _rms_fwd(x, g):
    r = lax.rsqrt(jnp.mean(x * x, axis=-1, keepdims=True) + EPS)
    return x * r * g


def _rms_bwd(dh, x, dres, g):
    dh = dh.astype(F32)
    r = lax.rsqrt(jnp.mean(x * x, axis=-1, keepdims=True) + EPS)
    xh = x * r
    dxh = dh * g
    dx = r * (dxh - xh * jnp.mean(dxh * xh, axis=-1, keepdims=True))
    return dres + dx, jnp.sum(dh * xh, axis=0, keepdims=True)


def _rope_apply(t, c, sa, sb):
    w = t.shape[1]
    return t * c + pltpu.roll(t, w - ROPE_HALF, 1) * sa + pltpu.roll(t, ROPE_HALF, 1) * sb


def _rope_transpose(g, c, sa, sb):
    w = g.shape[1]
    return g * c + pltpu.roll(g * sa, ROPE_HALF, 1) + pltpu.roll(g * sb, w - ROPE_HALF, 1)


def _tile_lanes(tab, reps):
    return jnp.concatenate([tab] * reps, axis=1) if reps > 1 else tab


def _rope_fwd(q, k, c, sa, sb):
    rq = Q_WIDTH // c.shape[1]
    qr = _rope_apply(q.astype(F32), _tile_lanes(c, rq), _tile_lanes(sa, rq), _tile_lanes(sb, rq))
    kr = _rope_apply(k.astype(F32), c, sa, sb)
    return qr, kr


def _rope_bwd_q(g, c, sa, sb):
    rq = Q_WIDTH // c.shape[1]
    return _rope_transpose(g.astype(F32), _tile_lanes(c, rq), _tile_lanes(sa, rq), _tile_lanes(sb, rq))


def _gelu(v):
    return jax.nn.gelu(v, approximate=True)


def _gelu_grad(v):
    c0 = math.sqrt(2.0 / math.pi)
    inner = c0 * (v + 0.044715 * v * v * v)
    th = jnp.tanh(inner)
    return 0.5 * (1.0 + th) + 0.5 * v * (1.0 - th * th) * c0 * (1.0 + 3 * 0.044715 * v * v)


def _merge_fwd(g0, g1, g2, b0, b1, b2, ya, ga, gb, yc):
    s0 = _sig(g0.astype(F32) + b0)
    s1 = _sig(g1.astype(F32) + b1)
    s2 = _sig(g2.astype(F32) + b2)
    ys = ga.astype(F32) * _sig(gb.astype(F32))
    return s0 * ya.astype(F32) + s1 * ys + s2 * yc.astype(F32)


def _merge_bwd(dm, g0, g1, g2, b0, b1, b2, ya, ga, gb, yc):
    dm = dm.astype(F32)
    s0 = _sig(g0.astype(F32) + b0)
    s1 = _sig(g1.astype(F32) + b1)
    s2 = _sig(g2.astype(F32) + b2)
    ga = ga.astype(F32)
    sb = _sig(gb.astype(F32))
    ys = ga * sb
    dya = dm * s0
    dys = dm * s1
    dyc = dm * s2
    dga = dys * sb
    dgb = dys * ga * sb * (1.0 - sb)
    d0 = dm * ya.astype(F32) * s0 * (1.0 - s0)
    d1 = dm * ys * s1 * (1.0 - s1)
    d2 = dm * yc.astype(F32) * s2 * (1.0 - s2)
    cs = lambda v: jnp.sum(v, axis=0, keepdims=True)
    return dya, dga, dgb, dyc, d0, d1, d2, cs(d0), cs(d1), cs(d2), cs(dga), cs(dgb)


def _ffn_act(fg, fu):
    fg = fg.astype(F32)
    return fg * _sig(fg) * fu.astype(F32)


def _ffn_act_bwd(da, fg, fu):
    da, fg, fu = da.astype(F32), fg.astype(F32), fu.astype(F32)
    s = _sig(fg)
    return da * fu * (s * (1.0 + fg * (1.0 - s))), da * fg * s


def _ple_fwd(x, gp, e):
    return x + _sig(gp.astype(F32)) * e.astype(F32)


def _ple_bwd(dx, gp, e):
    s = _sig(gp.astype(F32))
    e = e.astype(F32)
    return dx * s, dx * e * s * (1.0 - s)


def _loss_fn(x, tgt, g):
    d = x.shape[1]
    r = lax.rsqrt(jnp.mean(x * x, axis=-1, keepdims=True) + EPS)
    xh = x * r
    err = xh * g - tgt
    dy = err * (1.0 / d)
    dxh = dy * g
    dx = r * (dxh - xh * jnp.mean(dxh * xh, axis=-1, keepdims=True))
    return dx, jnp.sum(err * err, axis=0, keepdims=True) * (0.5 / d), jnp.sum(dy * xh, axis=0, keepdims=True)


def _adamw_fn(w, g, m, v):
    m = ADAM_B1 * m + (1.0 - ADAM_B1) * g
    v = ADAM_B2 * v + (1.0 - ADAM_B2) * (g * g)
    m_hat = m / (1.0 - ADAM_B1 ** ADAM_STEP)
    v_hat = v / (1.0 - ADAM_B2 ** ADAM_STEP)
    delta = -ADAM_LR * (m_hat / (jnp.sqrt(v_hat) + ADAM_EPS) + ADAM_WD * w)
    return delta, m, v


def _band_mask(n):
    qi = lax.broadcasted_iota(jnp.int32, (ATT_BLOCK, 2 * ATT_BLOCK), 0)
    kj = lax.broadcasted_iota(jnp.int32, (ATT_BLOCK, 2 * ATT_BLOCK), 1)
    dist = qi + ATT_BLOCK - kj
    return (dist >= 0) & (dist < ATT_BLOCK) & ((n > 0) | (kj >= ATT_BLOCK))


def _att_specs(nb):
    cur = lambda b, n: (0, b * nb + n, 0)
    prev = lambda b, n: (0, b * nb + jnp.maximum(n - 1, 0), 0)
    qs = pl.BlockSpec((N_Q_HEADS, ATT_BLOCK, HEAD_DIM), cur)
    kc = pl.BlockSpec((N_KV_HEADS, ATT_BLOCK, HEAD_DIM), cur)
    kp = pl.BlockSpec((N_KV_HEADS, ATT_BLOCK, HEAD_DIM), prev)
    stat = pl.BlockSpec((N_Q_HEADS, ATT_BLOCK, 1), cur)
    sink = pl.BlockSpec((N_Q_HEADS, 1, 1), lambda b, n: (0, 0, 0))
    return qs, kc, kp, stat, sink


def _attn_fwd(qh, kh, vh, sinks, nbatch, seq):
    t = qh.shape[1]
    nb = seq // ATT_BLOCK
    qs, kc, kp, stat, sink = _att_specs(nb)

    def body(q_ref, kp_ref, kc_ref, vp_ref, vc_ref, sink_ref, o_ref, lse_ref):
        mask = _band_mask(pl.program_id(1))
        rows = GQA_GROUP * ATT_BLOCK
        for kv in range(N_KV_HEADS):
            hs = slice(kv * GQA_GROUP, (kv + 1) * GQA_GROUP)
            kk = jnp.concatenate([kp_ref[kv], kc_ref[kv]], axis=0)
            vv = jnp.concatenate([vp_ref[kv], vc_ref[kv]], axis=0)
            q4 = (q_ref[hs] * SCALE).reshape(rows, HEAD_DIM)
            s = lax.dot_general(q4, kk, _DIMS["nt"], preferred_element_type=F32)
            s = jnp.where(mask, s.reshape(GQA_GROUP, ATT_BLOCK, 2 * ATT_BLOCK), NEG_INF)
            sk = sink_ref[hs]
            mx = jnp.maximum(jnp.max(s, axis=-1, keepdims=True), sk)
            p = jnp.exp(s - mx)
            den = jnp.sum(p, axis=-1, keepdims=True) + jnp.exp(sk - mx)
            o = lax.dot_general(p.reshape(rows, 2 * ATT_BLOCK).astype(BF16), vv, _DIMS["nn"],
                                preferred_element_type=F32).reshape(GQA_GROUP, ATT_BLOCK, HEAD_DIM)
            o_ref[hs] = (o * (1.0 / den)).astype(o_ref.dtype)
            lse_ref[hs] = mx + jnp.log(den)

    return pl.pallas_call(
        body, grid=(nbatch, nb), in_specs=[qs, kp, kc, kp, kc, sink], out_specs=[qs, stat],
        out_shape=[jax.ShapeDtypeStruct((N_Q_HEADS, t, HEAD_DIM), BF16), jax.ShapeDtypeStruct((N_Q_HEADS, t, 1), F32)],
        name="attn_fwd", compiler_params=_params(("parallel", "parallel")))(qh, kh, kh, vh, vh, sinks)


def _attn_bwd(qh, kh, vh, oh, doh, lse, sinks, nbatch, seq):
    t = qh.shape[1]
    nb = seq // ATT_BLOCK
    qs, kc, kp, stat, sink = _att_specs(nb)

    def body(q_ref, kp_ref, kc_ref, vp_ref, vc_ref, o_ref, do_ref, lse_ref, sink_ref,
             dq_ref, dkc_ref, dvc_ref, dkp_ref, dvp_ref, dsink_ref):
        first = (pl.program_id(0) == 0) & (pl.program_id(1) == 0)

        @pl.when(first)
        def _():
            dsink_ref[...] = jnp.zeros_like(dsink_ref)

        mask = _band_mask(pl.program_id(1))
        rows = GQA_GROUP * ATT_BLOCK
        band = (GQA_GROUP, ATT_BLOCK, 2 * ATT_BLOCK)
        for kv in range(N_KV_HEADS):
            hs = slice(kv * GQA_GROUP, (kv + 1) * GQA_GROUP)
            kk = jnp.concatenate([kp_ref[kv], kc_ref[kv]], axis=0)
            vv = jnp.concatenate([vp_ref[kv], vc_ref[kv]], axis=0)
            q4 = q_ref[hs].reshape(rows, HEAD_DIM)
            do4 = do_ref[hs].reshape(rows, HEAD_DIM)
            lse4 = lse_ref[hs]
            s = lax.dot_general(q4 * SCALE, kk, _DIMS["nt"], preferred_element_type=F32).reshape(band)
            p = jnp.where(mask, jnp.exp(s - lse4), 0.0)
            dd = jnp.sum(do_ref[hs].astype(F32) * o_ref[hs].astype(F32), axis=-1, keepdims=True)
            dp = lax.dot_general(do4, vv, _DIMS["nt"], preferred_element_type=F32).reshape(band)
            ds = (p * (dp - dd) * SCALE).astype(BF16).reshape(rows, 2 * ATT_BLOCK)
            dq = lax.dot_general(ds, kk, _DIMS["nn"], preferred_element_type=F32)
            dq_ref[hs] = dq.reshape(GQA_GROUP, ATT_BLOCK, HEAD_DIM).astype(dq_ref.dtype)
            dk = lax.dot_general(ds, q4, _DIMS["tn"], preferred_element_type=F32)
            dv = lax.dot_general(p.astype(BF16).reshape(rows, 2 * ATT_BLOCK), do4, _DIMS["tn"],
                                 preferred_element_type=F32)
            dsink_ref[hs] += -jnp.sum(jnp.exp(sink_ref[hs] - lse4) * dd, axis=1, keepdims=True)
            dkp_ref[kv] = dk[:ATT_BLOCK]
            dkc_ref[kv] = dk[ATT_BLOCK:]
            dvp_ref[kv] = dv[:ATT_BLOCK]
            dvc_ref[kv] = dv[ATT_BLOCK:]

    kvs = jax.ShapeDtypeStruct((N_KV_HEADS, t, HEAD_DIM), F32)
    return pl.pallas_call(
        body, grid=(nbatch, nb), in_specs=[qs, kp, kc, kp, kc, qs, qs, stat, sink],
        out_specs=[qs, kc, kc, kc, kc, sink],
        out_shape=[jax.ShapeDtypeStruct((N_Q_HEADS, t, HEAD_DIM), F32), kvs, kvs, kvs, kvs,
                   jax.ShapeDtypeStruct((N_Q_HEADS, 1, 1), F32)],
        name="attn_bwd", compiler_params=_params(("arbitrary", "arbitrary")))(qh, kh, kh, vh, vh, oh, doh, lse, sinks)


def _kv_combine(dkc, dkp, dvc, dvp, c, sa, sb, seq):
    t = dkc.shape[0]
    nb = seq // ATT_BLOCK
    nblk = t // ATT_BLOCK

    def body(kc_ref, kp_ref, vc_ref, vp_ref, c_ref, sa_ref, sb_ref, dk_ref, dv_ref):
        has_next = (pl.program_id(0) % nb) != nb - 1
        dk = kc_ref[...] + jnp.where(has_next, kp_ref[...], 0.0)
        dv = vc_ref[...] + jnp.where(has_next, vp_ref[...], 0.0)
        dk_ref[...] = _rope_transpose(dk, c_ref[...], sa_ref[...], sb_ref[...]).astype(dk_ref.dtype)
        dv_ref[...] = dv.astype(dv_ref.dtype)

    cur = pl.BlockSpec((ATT_BLOCK, KV_WIDTH), lambda i: (i, 0))
    nxt = pl.BlockSpec((ATT_BLOCK, KV_WIDTH), lambda i: (jnp.minimum(i + 1, nblk - 1), 0))
    o = jax.ShapeDtypeStruct((t, KV_WIDTH), BF16)
    return pl.pallas_call(body, grid=(nblk,), in_specs=[cur, nxt, cur, nxt, cur, cur, cur], out_specs=[cur, cur],
                          out_shape=[o, o], name="kv_combine", compiler_params=_params(("parallel",)))(
        dkc, dkp, dvc, dvp, c, sa, sb)


def _scan_block(ref, tab_ref, carry, ngroups, reverse):
    shifts = (7, 6, 4) if reverse else (1, 2, 4)
    n = SSM_LANES

    def step(i, car):
        g = (ngroups - 1 - i) if reverse else i
        r0 = pl.multiple_of(g * SUBLANES, SUBLANES)
        xr = ref[pl.ds(r0, SUBLANES), :n]
        xi = ref[pl.ds(r0, SUBLANES), n:]
        for s, sh in enumerate(shifts):
            pr, pi = tab_ref[2 * s], tab_ref[2 * s + 1]
            yr, yi = pltpu.roll(xr, sh, 0), pltpu.roll(xi, sh, 0)
            xr, xi = xr + pr * yr - pi * yi, xi + pr * yi + pi * yr
        cr, ci = car
        qr, qi = tab_ref[6], tab_ref[7]
        xr, xi = xr + qr * cr - qi * ci, xi + qr * ci + qi * cr
        ref[pl.ds(r0, SUBLANES), :n] = xr
        ref[pl.ds(r0, SUBLANES), n:] = xi
        last = r0 if reverse else r0 + SUBLANES - 1
        return ref[pl.ds(last, 1), :n], ref[pl.ds(last, 1), n:]

    return lax.fori_loop(0, ngroups, step, carry, unroll=2)


def _ssm_chunk(seq):
    return min(512, seq)


def _ssm_fwd(z, wb, wc, tab, dskip, nbatch, seq):
    t = z.shape[0]
    tc = _ssm_chunk(seq)
    nc = seq // tc
    n2 = 2 * SSM_LANES

    def body(u_ref, wb_ref, wc_ref, tab_ref, d_ref, st_ref, y_ref, gel_ref, car_ref):
        @pl.when(pl.program_id(1) == 0)
        def _():
            car_ref[...] = jnp.zeros_like(car_ref)

        u = u_ref[...]
        st_ref[...] = lax.dot_general(u, wb_ref[...], _DIMS["nn"], preferred_element_type=F32)
        cr, ci = _scan_block(st_ref, tab_ref, (car_ref[:, :SSM_LANES], car_ref[:, SSM_LANES:]), tc // SUBLANES, False)
        car_ref[:, :SSM_LANES] = cr
        car_ref[:, SSM_LANES:] = ci
        y = lax.dot_general(st_ref[...].astype(BF16), wc_ref[...], _DIMS["nn"], preferred_element_type=F32)
        y = y + d_ref[...] * u.astype(F32)
        y_ref[...] = y
        gel_ref[...] = _gelu(y).astype(gel_ref.dtype)

    row = lambda b, c: (b * nc + c, 0)
    full = lambda b, c: (0, 0)
    return pl.pallas_call(
        body, grid=(nbatch, nc),
        in_specs=[pl.BlockSpec((tc, SSM_WIDTH), lambda b, c: (b * nc + c, 3)), pl.BlockSpec((SSM_WIDTH, n2), full),
                  pl.BlockSpec((n2, SSM_WIDTH), full), pl.BlockSpec((8, SUBLANES, SSM_LANES), lambda b, c: (0, 0, 0)),
                  pl.BlockSpec((1, SSM_WIDTH), full)],
        out_specs=[pl.BlockSpec((tc, n2), row), pl.BlockSpec((tc, SSM_WIDTH), row), pl.BlockSpec((tc, SSM_WIDTH), row)],
        out_shape=[jax.ShapeDtypeStruct((t, n2), F32), jax.ShapeDtypeStruct((t, SSM_WIDTH), F32),
                   jax.ShapeDtypeStruct((t, SSM_WIDTH), BF16)],
        scratch_shapes=[pltpu.VMEM((1, n2), F32)], name="ssm_fwd",
        compiler_params=_params(("arbitrary", "arbitrary")))(z, wb, wc, tab, dskip)


def _ssm_bwd(dgi, ys, st, z, wbt, wct, tab_rev, dskip, nbatch, seq):
    t = z.shape[0]
    tc = _ssm_chunk(seq)
    nc = seq // tc
    n = SSM_LANES
    n2 = 2 * n
    ng = tc // SUBLANES

    def body(dgi_ref, ys_ref, st_ref, stp_ref, u_ref, wbt_ref, wct_ref, tab_ref, d_ref,
             du_ref, dwb_ref, dwc_ref, dd_ref, da_ref, p_ref, sb_ref, car_ref):
        b, c = pl.program_id(0), pl.program_id(1)
        ct = nc - 1 - c

        @pl.when((b == 0) & (c == 0))
        def _():
            dwb_ref[...] = jnp.zeros_like(dwb_ref)
            dwc_ref[...] = jnp.zeros_like(dwc_ref)
            dd_ref[...] = jnp.zeros_like(dd_ref)
            da_ref[...] = jnp.zeros_like(da_ref)

        @pl.when(c == 0)
        def _():
            car_ref[...] = jnp.zeros_like(car_ref)

        u = u_ref[...]
        dys = dgi_ref[...].astype(F32) * _gelu_grad(ys_ref[...])
        dys_b = dys.astype(BF16)
        st = st_ref[...]
        dd_ref[...] += jnp.sum(dys * u.astype(F32), axis=0, keepdims=True)
        dwc_ref[...] += lax.dot_general(st.astype(BF16), dys_b, _DIMS["tn"], preferred_element_type=F32)
        p_ref[...] = lax.dot_general(dys_b, wct_ref[...], _DIMS["nn"], preferred_element_type=F32)
        cr, ci = _scan_block(p_ref, tab_ref, (car_ref[:, :n], car_ref[:, n:]), ng, True)
        car_ref[:, :n] = cr
        car_ref[:, n:] = ci
        p = p_ref[...]
        pb = p.astype(BF16)
        dwb_ref[...] += lax.dot_general(u, pb, _DIMS["tn"], preferred_element_type=F32)
        du = lax.dot_general(pb, wbt_ref[...], _DIMS["nn"], preferred_element_type=F32) + d_ref[...] * dys
        du_ref[...] = du.astype(du_ref.dtype)
        sb_ref[pl.ds(0, SUBLANES), :] = jnp.where(ct > 0, stp_ref[...], 0.0)
        sb_ref[pl.ds(SUBLANES, tc), :] = st
        row0 = lax.broadcasted_iota(jnp.int32, (SUBLANES, n), 0) == 0

        def acc_step(g, acc):
            ar, ai = acc
            r0 = pl.multiple_of(g * SUBLANES, SUBLANES)
            edge_r = sb_ref[pl.ds(r0 + SUBLANES - 1, 1), :n]
            edge_i = sb_ref[pl.ds(r0 + SUBLANES - 1, 1), n:]
            sr = jnp.where(row0, edge_r, pltpu.roll(sb_ref[pl.ds(r0 + SUBLANES, SUBLANES), :n], 1, 0))
            si = jnp.where(row0, edge_i, pltpu.roll(sb_ref[pl.ds(r0 + SUBLANES, SUBLANES), n:], 1, 0))
            pr = p_ref[pl.ds(r0, SUBLANES), :n]
            pi = p_ref[pl.ds(r0, SUBLANES), n:]
            return ar + pr * sr + pi * si, ai + pi * sr - pr * si

        zero = jnp.zeros((SUBLANES, n), F32)
        ar, ai = lax.fori_loop(0, ng, acc_step, (zero, zero), unroll=2)
        da_ref[:, :n] += ar
        da_ref[:, n:] += ai

    row = lambda b, c: (b * nc + (nc - 1 - c), 0)
    prev8 = lambda b, c: (jnp.maximum((b * nc + (nc - 1 - c)) * (tc // SUBLANES) - 1, 0), 0)
    full = lambda b, c: (0, 0)
    return pl.pallas_call(
        body, grid=(nbatch, nc),
        in_specs=[pl.BlockSpec((tc, SSM_WIDTH), row), pl.BlockSpec((tc, SSM_WIDTH), row), pl.BlockSpec((tc, n2), row),
                  pl.BlockSpec((SUBLANES, n2), prev8),
                  pl.BlockSpec((tc, SSM_WIDTH), lambda b, c: (b * nc + (nc - 1 - c), 3)),
                  pl.BlockSpec((n2, SSM_WIDTH), full), pl.BlockSpec((SSM_WIDTH, n2), full),
                  pl.BlockSpec((8, SUBLANES, n), lambda b, c: (0, 0, 0)), pl.BlockSpec((1, SSM_WIDTH), full)],
        out_specs=[pl.BlockSpec((tc, SSM_WIDTH), row), pl.BlockSpec((SSM_WIDTH, n2), full),
                   pl.BlockSpec((n2, SSM_WIDTH), full), pl.BlockSpec((1, SSM_WIDTH), full),
                   pl.BlockSpec((SUBLANES, n2), full)],
        out_shape=[jax.ShapeDtypeStruct((t, SSM_WIDTH), BF16), jax.ShapeDtypeStruct((SSM_WIDTH, n2), F32),
                   jax.ShapeDtypeStruct((n2, SSM_WIDTH), F32), jax.ShapeDtypeStruct((1, SSM_WIDTH), F32),
                   jax.ShapeDtypeStruct((SUBLANES, n2), F32)],
        scratch_shapes=[pltpu.VMEM((tc, n2), F32), pltpu.VMEM((tc + SUBLANES, n2), F32), pltpu.VMEM((1, n2), F32)],
        name="ssm_bwd", compiler_params=_params(("arbitrary", "arbitrary")))(
        dgi, ys, st, st, z, wbt, wct, tab_rev, dskip)


def _ssm_prep(lam_re, lam_im, log_dt, b_re, b_im, c_re, c_im):
    lr = jnp.minimum(lam_re, -1e-4)
    li = lam_im
    dt = jnp.exp(log_dt)[:, None]
    mag = jnp.exp(lr * dt)
    a_re = mag * jnp.cos(li * dt)
    a_im = mag * jnp.sin(li * dt)
    den = lr * lr + li * li
    x_re, x_im = a_re - 1.0, a_im
    f_re = (x_re * lr + x_im * li) / den
    f_im = (x_im * lr - x_re * li) / den
    bb_re = f_re[..., None] * b_re - f_im[..., None] * b_im
    bb_im = f_re[..., None] * b_im + f_im[..., None] * b_re
    eye = jnp.eye(SSM_GROUPS, dtype=F32)
    emb_b = lambda v: jnp.einsum("gnh,gk->ghkn", v, eye).reshape(SSM_WIDTH, SSM_LANES)
    emb_c = lambda v: jnp.einsum("ghn,gk->gnkh", v, eye).reshape(SSM_LANES, SSM_WIDTH)
    wb = jnp.concatenate([emb_b(bb_re), emb_b(bb_im)], axis=1)
    wc = jnp.concatenate([emb_c(c_re), -emb_c(c_im)], axis=0)
    return a_re.reshape(-1), a_im.reshape(-1), wb, wc


def _ssm_tables(a_re, a_im, reverse):
    if reverse:
        a_im = -a_im
    pw = [(a_re, a_im)]
    for _ in range(SUBLANES - 1):
        pr, pi = pw[-1]
        pw.append((pr * a_re - pi * a_im, pr * a_im + pi * a_re))
    rows = jnp.arange(SUBLANES)[:, None]
    tabs = []
    for k in (1, 2, 4):
        ok = (rows + k <= SUBLANES - 1) if reverse else (rows >= k)
        tabs += [jnp.where(ok, pw[k - 1][0][None], 0.0), jnp.where(ok, pw[k - 1][1][None], 0.0)]
    order = list(range(SUBLANES - 1, -1, -1)) if reverse else list(range(SUBLANES))
    tabs += [jnp.stack([pw[i][0] for i in order]), jnp.stack([pw[i][1] for i in order])]
    return jnp.stack(tabs)


def _conv_chunk(seq):
    return min(512, seq)


def _conv_fwd(z, w, bias, lg, lb, nbatch, seq):
    t = z.shape[0]
    tc = _conv_chunk(seq)
    nc = seq // tc

    def body(a_ref, g_ref, w_ref, b_ref, lg_ref, lb_ref, cv_ref, sc_ref, ubuf):
        c = pl.program_id(1)

        @pl.when(c == 0)
        def _():
            ubuf[pl.ds(0, CONV_HALO), :] = jnp.zeros((CONV_HALO, CONV_WIDTH), F32)

        @pl.when(c > 0)
        def _():
            ubuf[pl.ds(0, CONV_HALO), :] = ubuf[pl.ds(tc, CONV_HALO), :]

        ubuf[pl.ds(CONV_HALO, tc), :] = a_ref[...].astype(F32) * _sig(g_ref[...].astype(F32))
        acc = jnp.zeros((tc, CONV_WIDTH), F32) + b_ref[...]
        for k in range(CONV_K):
            acc = acc + w_ref[pl.ds(k, 1), :] * ubuf[pl.ds(CONV_HALO - (CONV_K - 1) + k, tc), :]
        cv_ref[...] = acc
        mu = jnp.mean(acc, axis=-1, keepdims=True)
        xc = acc - mu
        y = xc * lax.rsqrt(jnp.mean(xc * xc, axis=-1, keepdims=True) + EPS) * lg_ref[...] + lb_ref[...]
        sc_ref[...] = (y * _sig(y)).astype(sc_ref.dtype)

    row = lambda b, c: (b * nc + c, 0)
    full = lambda b, c: (0, 0)
    vec = pl.BlockSpec((1, CONV_WIDTH), full)
    return pl.pallas_call(
        body, grid=(nbatch, nc),
        in_specs=[pl.BlockSpec((tc, CONV_WIDTH), lambda b, c: (b * nc + c, 4)),
                  pl.BlockSpec((tc, CONV_WIDTH), lambda b, c: (b * nc + c, 5)),
                  pl.BlockSpec((CONV_HALO, CONV_WIDTH), full), vec, vec, vec],
        out_specs=[pl.BlockSpec((tc, CONV_WIDTH), row), pl.BlockSpec((tc, CONV_WIDTH), row)],
        out_shape=[jax.ShapeDtypeStruct((t, CONV_WIDTH), F32), jax.ShapeDtypeStruct((t, CONV_WIDTH), BF16)],
        scratch_shapes=[pltpu.VMEM((CONV_HALO + tc, CONV_WIDTH), F32)], name="conv_fwd",
        compiler_params=_params(("arbitrary", "arbitrary")))(z, z, w, bias, lg, lb)


def _conv_bwd(dsc, cv, z, w, lg, lb, nbatch, seq):
    t = z.shape[0]
    tc = _conv_chunk(seq)
    nc = seq // tc
    hb = tc // CONV_HALO

    def body(dsc_ref, cv_ref, a_ref, g_ref, ap_ref, gp_ref, w_ref, lg_ref, lb_ref,
             da_ref, dg_ref, dw_ref, db_ref, dlg_ref, dlb_ref, ubuf, dbuf):
        b, c = pl.program_id(0), pl.program_id(1)
        ct = nc - 1 - c

        @pl.when((b == 0) & (c == 0))
        def _():
            dw_ref[...] = jnp.zeros_like(dw_ref)
            db_ref[...] = jnp.zeros_like(db_ref)
            dlg_ref[...] = jnp.zeros_like(dlg_ref)
            dlb_ref[...] = jnp.zeros_like(dlb_ref)

        cvv = cv_ref[...]
        mu = jnp.mean(cvv, axis=-1, keepdims=True)
        xc = cvv - mu
        rstd = lax.rsqrt(jnp.mean(xc * xc, axis=-1, keepdims=True) + EPS)
        xh = xc * rstd
        y = xh * lg_ref[...] + lb_ref[...]
        sy = _sig(y)
        dy = dsc_ref[...].astype(F32) * (sy * (1.0 + y * (1.0 - sy)))
        dlg_ref[...] += jnp.sum(dy * xh, axis=0, keepdims=True)
        dlb_ref[...] += jnp.sum(dy, axis=0, keepdims=True)
        dxh = dy * lg_ref[...]
        dcv = rstd * (dxh - jnp.mean(dxh, axis=-1, keepdims=True) - xh * jnp.mean(dxh * xh, axis=-1, keepdims=True))
        db_ref[...] += jnp.sum(dcv, axis=0, keepdims=True)

        @pl.when(c == 0)
        def _():
            dbuf[pl.ds(tc, CONV_HALO), :] = jnp.zeros((CONV_HALO, CONV_WIDTH), F32)

        @pl.when(c > 0)
        def _():
            dbuf[pl.ds(tc, CONV_HALO), :] = dbuf[pl.ds(0, CONV_HALO), :]

        dbuf[pl.ds(0, tc), :] = dcv
        a = a_ref[...].astype(F32)
        sg = _sig(g_ref[...].astype(F32))
        ubuf[pl.ds(0, CONV_HALO), :] = jnp.where(ct > 0, ap_ref[...].astype(F32) * _sig(gp_ref[...].astype(F32)), 0.0)
        ubuf[pl.ds(CONV_HALO, tc), :] = a * sg
        du = jnp.zeros((tc, CONV_WIDTH), F32)
        for k in range(CONV_K):
            du = du + w_ref[pl.ds(k, 1), :] * dbuf[pl.ds(CONV_K - 1 - k, tc), :]
            dw_ref[pl.ds(k, 1), :] += jnp.sum(dcv * ubuf[pl.ds(CONV_HALO - (CONV_K - 1) + k, tc), :],
                                             axis=0, keepdims=True)
        da_ref[...] = (du * sg).astype(da_ref.dtype)
        dg_ref[...] = (du * a * sg * (1.0 - sg)).astype(dg_ref.dtype)

    row = lambda b, c: (b * nc + (nc - 1 - c), 0)
    full = lambda b, c: (0, 0)
    vec = pl.BlockSpec((1, CONV_WIDTH), full)
    blk = pl.BlockSpec((tc, CONV_WIDTH), row)

    def zcol(col):
        return pl.BlockSpec((tc, CONV_WIDTH), lambda b, c: (b * nc + (nc - 1 - c), col))

    def zprev(col):
        return pl.BlockSpec((CONV_HALO, CONV_WIDTH),
                            lambda b, c: (jnp.maximum((b * nc + (nc - 1 - c)) * hb - 1, 0), col))

    o = jax.ShapeDtypeStruct((t, CONV_WIDTH), BF16)
    v = jax.ShapeDtypeStruct((1, CONV_WIDTH), F32)
    return pl.pallas_call(
        body, grid=(nbatch, nc),
        in_specs=[blk, blk, zcol(4), zcol(5), zprev(4), zprev(5), pl.BlockSpec((CONV_HALO, CONV_WIDTH), full), vec, vec],
        out_specs=[blk, blk, pl.BlockSpec((CONV_HALO, CONV_WIDTH), full), vec, vec, vec],
        out_shape=[o, o, jax.ShapeDtypeStruct((CONV_HALO, CONV_WIDTH), F32), v, v, v],
        scratch_shapes=[pltpu.VMEM((CONV_HALO + tc, CONV_WIDTH), F32), pltpu.VMEM((tc + CONV_HALO, CONV_WIDTH), F32)],
        name="conv_bwd", compiler_params=_params(("arbitrary", "arbitrary")))(dsc, cv, z, z, z, z, w, lg, lb)


BIG = ("w_in", "w_attn_out", "w_ssm_glu", "w_conv_out", "w_mix_out", "w_ffn_in", "w_ffn_out", "w_ple_in", "w_ple_gate")
BIG_AXIS = {"w_in": 2, "w_attn_out": 2, "w_ssm_glu": 2, "w_conv_out": 2, "w_mix_out": 1, "w_ffn_in": 2,
            "w_ffn_out": 1, "w_ple_in": 2, "w_ple_gate": 1}
SHARD_MAJOR = ("w_in", "w_ffn_in")
SMALL = ("mix_norm_g", "b_gate", "attn_sinks", "ssm_lambda_re", "ssm_lambda_im", "ssm_log_dt", "ssm_b_re", "ssm_b_im",
         "ssm_c_re", "ssm_c_im", "ssm_d", "b_ssm_glu", "conv_dw_w", "conv_dw_b", "conv_norm_g", "conv_norm_b",
         "ffn_norm_g", "ple_norm_g", "final_norm_g")
WEIGHTS = ("mix_norm_g", "w_in", "b_gate", "attn_sinks", "w_attn_out", "ssm_lambda_re", "ssm_lambda_im", "ssm_log_dt",
           "ssm_b_re", "ssm_b_im", "ssm_c_re", "ssm_c_im", "ssm_d", "w_ssm_glu", "b_ssm_glu", "conv_dw_w", "conv_dw_b",
           "conv_norm_g", "conv_norm_b", "w_conv_out", "w_mix_out", "ffn_norm_g", "w_ffn_in", "w_ffn_out", "w_ple_in",
           "ple_norm_g", "w_ple_gate", "final_norm_g")
SSM_NAMES = ("ssm_lambda_re", "ssm_lambda_im", "ssm_log_dt", "ssm_b_re", "ssm_b_im", "ssm_c_re", "ssm_c_im")


def _heads(v, nh):
    return v.reshape(v.shape[0], nh, HEAD_DIM).transpose(1, 0, 2)


def _tokens(v):
    return v.transpose(1, 0, 2).reshape(v.shape[1], v.shape[0] * HEAD_DIM)


def _row(v):
    return v.reshape(1, -1)


def _layer_fwd(x, p_l, w, s, rope, nbatch, seq, next_shards=None):
    t = x.shape[0]
    tm = 512
    d = D_MODEL
    sv = {}
    sv["x"] = x
    h = _rowwise("rms_mix", _rms_fwd, [R(x), V(_row(s["mix_norm_g"]))], [O(d, BF16)], tm=tm)
    cs = {nm: w[nm].shape[2] for nm in SHARD_MAJOR}
    tb = 1024
    got = {}
    plan = None if next_shards is None else _gather_plan(next_shards, GATHER_A)
    z = _mm("mm_in", h, w["w_in"], "nn", BF16, m=t, n=N_CHIPS * cs["w_in"], k=d, tm=tb, tn=cs["w_in"], tk=d,
            b_sh=cs["w_in"], comm=plan)
    if plan is not None:
        z, outs = z
        got.update(zip(plan["names"], outs))
    sv["h"], sv["z"] = h, z
    c, sa, sb = rope
    qr, kr = _rowwise("rope_fwd", _rope_fwd, [R(z, Q_WIDTH, 0), R(z, KV_WIDTH, 4), R(c), R(sa), R(sb)],
                      [O(Q_WIDTH, BF16), O(KV_WIDTH, BF16)], tm=tm)
    qh, kh = _heads(qr, N_Q_HEADS), _heads(kr, N_KV_HEADS)
    vh = _heads(z[:, Q_WIDTH + KV_WIDTH:Q_WIDTH + 2 * KV_WIDTH], N_KV_HEADS)
    sinks = s["attn_sinks"].reshape(N_Q_HEADS, 1, 1)
    oh, lse = _attn_fwd(qh, kh, vh, sinks, nbatch, seq)
    o = _tokens(oh)
    ya = _mm("mm_attn_out", o, w["w_attn_out"], "nn", BF16, m=t, n=d, k=Q_WIDTH, tm=tb, tn=d, tk=Q_WIDTH)
    sv.update(qh=qh, kh=kh, vh=vh, oh=oh, lse=lse, o=o, ya=ya, sinks=sinks)
    ssm_args = [s[nm] for nm in SSM_NAMES]
    a_re, a_im, wb, wc = _ssm_prep(*ssm_args)
    dskip = _row(s["ssm_d"])
    st, ys, gel = _ssm_fwd(z, wb.astype(BF16), wc.astype(BF16), _ssm_tables(a_re, a_im, False), dskip, nbatch, seq)
    glu = _mm("mm_glu", gel, w["w_ssm_glu"], "nn", BF16, m=t, n=2 * d, k=SSM_WIDTH, tm=tb, tn=2 * d, tk=SSM_WIDTH,
              bias=_row(s["b_ssm_glu"]))
    sv.update(st=st, ys=ys, gel=gel, glu=glu, a=(a_re, a_im), wb=wb, wc=wc, dskip=dskip)
    cw = jnp.pad(s["conv_dw_w"], ((0, CONV_HALO - CONV_K), (0, 0)))
    cv, sc = _conv_fwd(z, cw, _row(s["conv_dw_b"]), _row(s["conv_norm_g"]), _row(s["conv_norm_b"]), nbatch, seq)
    yc = _mm("mm_conv_out", sc, w["w_conv_out"], "nn", BF16, m=t, n=d, k=CONV_WIDTH, tm=tb, tn=d, tk=CONV_WIDTH)
    sv.update(cw=cw, cv=cv, sc=sc, yc=yc)
    bg = _row(s["b_gate"])
    merge_ins = [R(z, 512, 3), R(z, 512, 5), R(z, 512, 7), V(bg, 512, 0), V(bg, 512, 2), V(bg, 512, 4),
                 R(ya, 512, 0), R(glu, 512, 0), R(glu, 512, 2), R(yc, 512, 0)]
    merged = _rowwise("merge_fwd", _merge_fwd, merge_ins, [O(512, BF16, total=d)], tm=tm, ncol=2)
    x1 = _mm("mm_mix", merged, w["w_mix_out"], "nn", F32, m=t, n=d, k=d, tm=tb, tn=d, tk=d, res=x)
    sv.update(merged=merged, x1=x1)
    hf = _rowwise("rms_ffn", _rms_fwd, [R(x1), V(_row(s["ffn_norm_g"]))], [O(d, BF16)], tm=tm)
    plan = None if next_shards is None else _gather_plan(next_shards, GATHER_B)
    f = _mm("mm_ffn_in", hf, w["w_ffn_in"], "nn", BF16, m=t, n=2 * FFN_HIDDEN, k=d, tm=tb, tn=cs["w_ffn_in"], tk=d,
            b_sh=cs["w_ffn_in"], comm=plan)
    if plan is not None:
        f, outs = f
        got.update(zip(plan["names"], outs))
    act = (lambda i, j, kk, fg, fu: _ffn_act(fg, fu), [(f, lambda i, j, kk: (i, 0)), (f, lambda i, j, kk: (i, 1))])
    x2 = _mm("mm_ffn_out", act, w["w_ffn_out"], "nn", F32, m=t, n=d, k=FFN_HIDDEN, tm=256, tn=d, tk=FFN_HIDDEN, res=x1)
    sv.update(hf=hf, f=f, x2=x2)
    e = _mm("mm_ple_in", p_l, w["w_ple_in"], "nn", BF16, m=t, n=d, k=p_l.shape[1], tm=tb, tn=d, tk=p_l.shape[1])
    hp = _rowwise("rms_ple", _rms_fwd, [R(x2), V(_row(s["ple_norm_g"]))], [O(d, BF16)], tm=tm)
    gp = _mm("mm_ple_gate", hp, w["w_ple_gate"], "nn", BF16, m=t, n=d, k=d, tm=tb, tn=d, tk=d)
    x3 = _rowwise("ple_fwd", _ple_fwd, [R(x2), R(gp), R(e)], [O(d, F32)], tm=tm)
    sv.update(e=e, hp=hp, gp=gp, p=p_l)
    return x3, sv, got


def _layer_bwd(dx3, sv, w, s, rope, nbatch, seq):
    t = dx3.shape[0]
    tm = 512
    d = D_MODEL
    gb, gs = {}, {}
    cs = {nm: w[nm].shape[2] for nm in SHARD_MAJOR}
    tb = 1024

    def wg(name, a, b, m, n, tm=1024, tk=1024, shard=None):
        return _mm(name, a, b, "tn", BF16, m=m, n=n, k=t, tm=tm, tn=n if shard is None else cs[shard], tk=tk,
                   o_sh=None if shard is None else cs[shard])

    de, dgp = _rowwise("ple_bwd", _ple_bwd, [R(dx3), R(sv["gp"]), R(sv["e"])], [O(d, BF16), O(d, BF16)], tm=tm)
    gb["w_ple_in"] = wg("wg_ple_in", sv["p"], de, sv["p"].shape[1], d, tk=2048)
    gb["w_ple_gate"] = wg("wg_ple_gate", sv["hp"], dgp, d, d, tk=2048)
    dhp = _mm("mmb_ple_gate", dgp, w["w_ple_gate"], "nt", BF16, m=t, n=d, k=d, tm=tb, tn=d, tk=d)
    dx2, gs["ple_norm_g"] = _rowwise("rms_ple_bwd", _rms_bwd, [R(dhp), R(sv["x2"]), R(dx3), V(_row(s["ple_norm_g"]))],
                                     [O(d, F32)], [A(d)], tm=tm)
    fw = FFN_HIDDEN // 2
    dact = _mm("mmb_ffn_out", dx2, w["w_ffn_out"], "nt", BF16, m=t, n=FFN_HIDDEN, k=d, tm=tb, tn=fw, tk=d)
    f = sv["f"]
    act_t = (lambda i, j, kk, fg, fu: _ffn_act(fg, fu), [(f, lambda i, j, kk: (kk, i)), (f, lambda i, j, kk: (kk, 2 + i))])
    gb["w_ffn_out"] = _mm("wg_ffn_out", act_t, dx2, "tn", BF16, m=FFN_HIDDEN, n=d, k=t, tm=fw, tn=d, tk=512)

    def df_tile(is_gate, da, fg, fu):
        dfg, dfu = _ffn_act_bwd(da, fg, fu)
        return jnp.where(is_gate, dfg, dfu)

    assert cs["w_ffn_in"] == fw
    df_cols = (lambda i, j, kk, *v: df_tile(j < 2, *v),
               [(dact, lambda i, j, kk: (kk, j % 2)), (f, lambda i, j, kk: (kk, j % 2)), (f, lambda i, j, kk: (kk, 2 + j % 2))])
    gb["w_ffn_in"] = _mm("wg_ffn_in", sv["hf"], df_cols, "tn", BF16, m=d, n=2 * FFN_HIDDEN, k=t, tm=d, tn=fw, tk=512,
                         o_sh=fw)
    df_rows = (lambda i, j, kk, *v: df_tile(kk < 2, *v),
               [(dact, lambda i, j, kk: (i, kk % 2)), (f, lambda i, j, kk: (i, kk % 2)), (f, lambda i, j, kk: (i, 2 + kk % 2))])
    dhf = _mm("mmb_ffn_in", df_rows, w["w_ffn_in"], "nt", BF16, m=t, n=d, k=2 * FFN_HIDDEN, tm=512, tn=d, tk=fw, b_sh=fw)
    dx1, gs["ffn_norm_g"] = _rowwise("rms_ffn_bwd", _rms_bwd, [R(dhf), R(sv["x1"]), R(dx2), V(_row(s["ffn_norm_g"]))],
                                     [O(d, F32)], [A(d)], tm=tm)
    dm = _mm("mmb_mix", dx1, w["w_mix_out"], "nt", BF16, m=t, n=d, k=d, tm=tb, tn=d, tk=d)
    gb["w_mix_out"] = wg("wg_mix", sv["merged"], dx1, d, d)
    z, glu, bg = sv["z"], sv["glu"], _row(s["b_gate"])
    ins = [R(dm, 512, 0), R(z, 512, 3), R(z, 512, 5), R(z, 512, 7), V(bg, 512, 0), V(bg, 512, 2), V(bg, 512, 4),
           R(sv["ya"], 512, 0), R(glu, 512, 0), R(glu, 512, 2), R(sv["yc"], 512, 0)]
    ob = lambda: O(512, BF16, total=d)
    ab = lambda: A(512, total=d)
    dya, dga, dgb, dyc, d0, d1, d2, db0, db1, db2, dba, dbb = _rowwise(
        "merge_bwd", _merge_bwd, ins, [ob() for _ in range(7)], [ab() for _ in range(5)], tm=tm, ncol=2)
    gs["b_gate"] = jnp.concatenate([db0, db1, db2], axis=1)
    gs["b_ssm_glu"] = jnp.concatenate([dba, dbb], axis=1)
    dglu = jnp.concatenate([dga, dgb], axis=1)
    gb["w_attn_out"] = wg("wg_attn_out", sv["o"], dya, Q_WIDTH, d, tk=2048)
    do = _mm("mmb_attn_out", dya, w["w_attn_out"], "nt", BF16, m=t, n=Q_WIDTH, k=d, tm=tb, tn=Q_WIDTH, tk=d)
    dqh, dkc, dvc, dkp, dvp, dsink = _attn_bwd(sv["qh"], sv["kh"], sv["vh"], sv["oh"], _heads(do, N_Q_HEADS),
                                               sv["lse"], sv["sinks"], nbatch, seq)
    gs["attn_sinks"] = dsink.reshape(-1)
    c, sa, sb = rope
    dq = _rowwise("rope_bwd_q", _rope_bwd_q, [R(_tokens(dqh)), R(c), R(sa), R(sb)], [O(Q_WIDTH, BF16)], tm=tm)
    dk, dv = _kv_combine(_tokens(dkc), _tokens(dkp), _tokens(dvc), _tokens(dvp), c, sa, sb, seq)
    gb["w_ssm_glu"] = wg("wg_ssm_glu", sv["gel"], dglu, SSM_WIDTH, 2 * d, tk=2048)
    dgi = _mm("mmb_glu", dglu, w["w_ssm_glu"], "nt", BF16, m=t, n=SSM_WIDTH, k=2 * d, tm=tb, tn=SSM_WIDTH, tk=2 * d)
    a_re, a_im = sv["a"]
    du, dwb, dwc, dd, da = _ssm_bwd(dgi, sv["ys"], sv["st"], z, sv["wb"].T.astype(BF16), sv["wc"].T.astype(BF16),
                                    _ssm_tables(a_re, a_im, True), sv["dskip"], nbatch, seq)
    gs["ssm_d"] = dd.reshape(-1)
    da = jnp.sum(da, axis=0)
    _, prep_vjp = jax.vjp(_ssm_prep, *[s[nm] for nm in SSM_NAMES])
    for nm, g in zip(SSM_NAMES, prep_vjp((da[:SSM_LANES], da[SSM_LANES:], dwb, dwc))):
        gs[nm] = g
    gb["w_conv_out"] = wg("wg_conv_out", sv["sc"], dyc, CONV_WIDTH, d, tk=2048)
    dsc = _mm("mmb_conv_out", dyc, w["w_conv_out"], "nt", BF16, m=t, n=CONV_WIDTH, k=d, tm=tb, tn=CONV_WIDTH, tk=d)
    dca, dcg, dcw, dcb, dlg, dlb = _conv_bwd(dsc, sv["cv"], z, sv["cw"], _row(s["conv_norm_g"]),
                                             _row(s["conv_norm_b"]), nbatch, seq)
    gs["conv_dw_w"] = dcw[:CONV_K]
    gs["conv_dw_b"], gs["conv_norm_g"], gs["conv_norm_b"] = dcb.reshape(-1), dlg.reshape(-1), dlb.reshape(-1)
    dz = jnp.concatenate([dq, dk, dv, du, dca, dcg, d0, d1, d2], axis=1)
    gb["w_in"] = wg("wg_in", sv["h"], dz, d, dz.shape[1], tk=2048, shard="w_in")
    dh = _mm("mmb_in", dz, w["w_in"], "nt", BF16, m=t, n=d, k=dz.shape[1], tm=tb, tn=d, tk=cs["w_in"],
             b_sh=cs["w_in"])
    dx, gs["mix_norm_g"] = _rowwise("rms_mix_bwd", _rms_bwd, [R(dh), R(sv["x"]), R(dx1), V(_row(s["mix_norm_g"]))],
                                    [O(d, F32)], [A(d)], tm=tm)
    gs["mix_norm_g"], gs["ffn_norm_g"], gs["ple_norm_g"] = (gs[nm].reshape(-1) for nm in
                                                            ("mix_norm_g", "ffn_norm_g", "ple_norm_g"))
    gs["b_gate"], gs["b_ssm_glu"] = gs["b_gate"].reshape(-1), gs["b_ssm_glu"].reshape(-1)
    return dx, gb, gs


def _rope_tables(positions):
    inv_freq = ROPE_THETA ** (-jnp.arange(0, ROPE_DIM, 2, dtype=F32) / ROPE_DIM)
    ang = positions.reshape(-1).astype(F32)[:, None] * inv_freq
    cos, sin = jnp.cos(ang), jnp.sin(ang)
    t = ang.shape[0]
    rest = HEAD_DIM - ROPE_DIM
    c = jnp.concatenate([cos, cos, jnp.ones((t, rest), F32)], axis=1)
    sa = jnp.concatenate([-sin, jnp.zeros((t, HEAD_DIM - ROPE_HALF), F32)], axis=1)
    sb = jnp.concatenate([jnp.zeros((t, ROPE_HALF), F32), sin, jnp.zeros((t, rest), F32)], axis=1)
    two = lambda v: jnp.concatenate([v, v], axis=1)
    return two(c), two(sa), two(sb)


def _natural(nm, w4):
    if nm in SHARD_MAJOR:
        return w4
    if BIG_AXIS[nm] == 1:
        return w4.reshape(-1, w4.shape[2])
    return w4.transpose(1, 0, 2).reshape(w4.shape[1], -1)


def _shard_major(nm, g):
    if nm in SHARD_MAJOR:
        return g
    if BIG_AXIS[nm] == 1:
        return g.reshape(N_CHIPS, -1, g.shape[1])
    return g.reshape(g.shape[0], N_CHIPS, -1).transpose(1, 0, 2)


def _untap(taps4, cols):
    flat = taps4.reshape(N_CHIPS, -1)[:, :CONV_K * cols]
    return flat.reshape(N_CHIPS, CONV_K, cols).transpose(1, 0, 2).reshape(CONV_K, N_CHIPS * cols)


def _local_step(x, p, positions, loss_target, small, wfull=None, shards=None):
    nbatch, seq, d = x.shape
    depth = p.shape[0]
    t = nbatch * seq
    rope = _rope_tables(positions)
    xs = x.reshape(t, d)
    saved, ws, ss = [], [], []
    got = None if shards is None else _gather_now(shards[0])
    for l in range(depth):
        w4 = {nm: wfull[nm][l] for nm in BIG} if shards is None else got
        w_l = {nm: _natural(nm, w4[nm]) for nm in BIG}
        s_l = {nm: small[nm][l] for nm in small if nm != "final_norm_g"}
        if shards is not None:
            s_l["conv_dw_w"] = _untap(got[TAPS], CONV_WIDTH // N_CHIPS)
        nxt = shards[l + 1] if shards is not None and l + 1 < depth else None
        xs, sv, got = _layer_fwd(xs, p[l].reshape(t, -1), w_l, s_l, rope, nbatch, seq, nxt)
        saved.append(sv)
        ws.append(w_l)
        ss.append(s_l)
    dx, loss_cols, dgf = _rowwise("loss_head", _loss_fn, [R(xs), R(loss_target.reshape(t, d)),
                                                          V(_row(small["final_norm_g"]))],
                                  [O(d, F32)], [A(d), A(d)], tm=512)
    gbs, gss = [], []
    for l in reversed(range(depth)):
        dx, gb, gs = _layer_bwd(dx, saved[l], ws[l], ss[l], rope, nbatch, seq)
        gbs.insert(0, gb)
        gss.insert(0, gs)
    gbig = {nm: jnp.stack([_shard_major(nm, g[nm]) for g in gbs]) for nm in BIG}
    gsmall = {nm: jnp.stack([g[nm] for g in gss]) for nm in SMALL if nm != "final_norm_g"}
    gsmall["final_norm_g"] = dgf.reshape(-1)
    return loss_cols, dx.reshape(nbatch, seq, d), gbig, gsmall


HBM = pl.BlockSpec(memory_space=pltpu.HBM)


def _place():
    x, y, c = lax.axis_index("x"), lax.axis_index("y"), lax.axis_index("c")
    chips = [(1 - x, y), (x, 1 - y), (1 - x, 1 - y)]
    return x, y, c, chips


def _remote(src, dst, send_sem, recv_sem, to):
    return pltpu.make_async_remote_copy(src_ref=src, dst_ref=dst, send_sem=send_sem, recv_sem=recv_sem,
                                        device_id=to, device_id_type=MESH)


TAPS = "taps"
GATHER_ALL = (("w_ffn_in", "w_ffn_out"),
              ("w_in", "w_ple_gate", "w_mix_out", "w_attn_out", "w_ssm_glu", "w_conv_out", "w_ple_in", TAPS))
GATHER_A = (("w_ffn_in",), ("w_in", "w_ple_gate"))
GATHER_B = (("w_ffn_out",), ("w_mix_out", "w_attn_out", "w_ssm_glu", "w_conv_out", "w_ple_in", TAPS))


def _gather_plan(shards, sets):
    names = sets[0] + sets[1]
    n = len(names)
    idx = {nm: i for i, nm in enumerate(names)}

    def start(ins, outs, sems):
        send1, recv1, _, _, send0, recv0 = sems
        x, y, c, chips = _place()
        me = 2 * x + y
        for i in range(n):
            _remote(ins[i], outs[i].at[me], send0.at[i], recv0.at[i], (x, y, 1 - c)).start()
        for role in (0, 1):
            @pl.when(c == role)
            def _():
                for nm in sets[role]:
                    for k, (cx, cy) in enumerate(chips):
                        _remote(ins[idx[nm]], outs[idx[nm]].at[me], send1.at[idx[nm], k], recv1.at[idx[nm], k],
                                (cx, cy, c)).start()

    def finish(ins, outs, sems):
        send1, recv1, send2, recv2, send0, recv0 = sems
        x, y, c, chips = _place()
        me = 2 * x + y
        sib = (x, y, 1 - c)
        for role in (0, 1):
            @pl.when(c == role)
            def _():
                passed = []
                for nm in sets[role]:
                    i = idx[nm]
                    for k, (cx, cy) in enumerate(chips):
                        slot = outs[i].at[2 * cx + cy]
                        _remote(slot, slot, send1.at[i, k], recv1.at[i, k], (cx, cy, c)).wait_recv()
                        cp = _remote(slot, slot, send2.at[i, k], recv2.at[i, k], sib)
                        cp.start()
                        passed.append(cp)
                for nm in sets[1 - role]:
                    i = idx[nm]
                    for k, (cx, cy) in enumerate(chips):
                        slot = outs[i].at[2 * cx + cy]
                        _remote(slot, slot, send2.at[i, k], recv2.at[i, k], sib).wait_recv()
                for nm in sets[role]:
                    i = idx[nm]
                    for k, (cx, cy) in enumerate(chips):
                        _remote(ins[i], outs[i].at[me], send1.at[i, k], recv1.at[i, k], (cx, cy, c)).wait_send()
                for cp in passed:
                    cp.wait_send()
        for i in range(n):
            _remote(ins[i], outs[i].at[me], send0.at[i], recv0.at[i], sib).wait()

    ins = [shards[nm] for nm in names]
    return dict(names=names, ins=ins, start=start, finish=finish,
                out_shapes=[jax.ShapeDtypeStruct((N_CHIPS,) + v.shape, v.dtype) for v in ins],
                sems=[pltpu.SemaphoreType.DMA((n, 3)) for _ in range(4)] + [pltpu.SemaphoreType.DMA((n,))
                                                                            for _ in range(2)])


def _gather_now(shards):
    plan = _gather_plan(shards, GATHER_ALL)
    n = len(plan["ins"])

    def body(*refs):
        ins, outs, sems = refs[:n], refs[n:2 * n], refs[2 * n:]
        plan["start"](ins, outs, sems)
        plan["finish"](ins, outs, sems)

    outs = pl.pallas_call(body, out_shape=plan["out_shapes"], in_specs=[HBM] * n, out_specs=[HBM] * n,
                          scratch_shapes=plan["sems"], name="gather_weights")(*plan["ins"])
    return dict(zip(plan["names"], outs))


def _pair_exchange(grads):
    n = len(grads)
    hl = grads[0].shape[0] // 2

    def body(*refs):
        ins, outs = refs[:n], refs[n:2 * n]
        send, recv = refs[2 * n:]
        x, y, c, _ = _place()
        other = pl.ds((1 - c) * hl, hl)
        cps = [_remote(ins[i].at[other], outs[i], send.at[i], recv.at[i], (x, y, 1 - c)) for i in range(n)]
        for cp in cps:
            cp.start()
        for cp in cps:
            cp.wait()

    out_shape = [jax.ShapeDtypeStruct((hl,) + g.shape[1:], g.dtype) for g in grads]
    sems = [pltpu.SemaphoreType.DMA((n,)) for _ in range(2)]
    return pl.pallas_call(body, out_shape=out_shape, in_specs=[HBM] * n, out_specs=[HBM] * n, scratch_shapes=sems,
                          name="reduce_pair_exchange")(*grads)


def _pair_add(g, r):
    hl, _, rr, cc = r.shape
    rows = hl * N_CHIPS * rr
    nblk = rows // rr

    def body(c_ref, g_ref, r_ref, o_ref):
        o_ref[...] = (g_ref[...].astype(F32) + r_ref[...].astype(F32)).astype(o_ref.dtype)

    grid_spec = pltpu.PrefetchScalarGridSpec(
        num_scalar_prefetch=1, grid=(nblk,),
        in_specs=[pl.BlockSpec((rr, cc), lambda i, c_ref: (c_ref[0] * nblk + i, 0)),
                  pl.BlockSpec((rr, cc), lambda i, c_ref: (i, 0))],
        out_specs=pl.BlockSpec((rr, cc), lambda i, c_ref: (i, 0)))
    c = lax.axis_index("c").astype(jnp.int32).reshape(1)
    out = pl.pallas_call(body, out_shape=jax.ShapeDtypeStruct((rows, cc), r.dtype), grid_spec=grid_spec,
                         name="reduce_pair_add", compiler_params=_params(("parallel",)))(
        c, g.reshape(-1, cc), r.reshape(rows, cc))
    return out.reshape(r.shape)


def _chip_exchange(psums):
    n = len(psums)

    def body(*refs):
        ins, got = refs[:n], refs[n:2 * n]
        send, recv = refs[2 * n:]
        x, y, c, chips = _place()
        cps = [_remote(ins[i].at[:, 2 * cx + cy], got[i].at[k], send.at[i, k], recv.at[i, k], (cx, cy, c))
               for i in range(n) for k, (cx, cy) in enumerate(chips)]
        for cp in cps:
            cp.start()
        for cp in cps:
            cp.wait()

    got_shape = [jax.ShapeDtypeStruct((3, p.shape[0]) + p.shape[2:], p.dtype) for p in psums]
    sems = [pltpu.SemaphoreType.DMA((n, 3)), pltpu.SemaphoreType.DMA((n, 3))]
    return pl.pallas_call(body, out_shape=got_shape, in_specs=[HBM] * n, out_specs=[HBM] * n, scratch_shapes=sems,
                          name="reduce_chip_exchange")(*psums)


def _sum4(psum, got):
    hl, _, rr, cc = psum.shape
    tr = rr if rr * cc <= 512 * 1024 else rr // 2

    def body(place_ref, own_ref, g0_ref, g1_ref, g2_ref, o_ref):
        tot = (own_ref[...].astype(F32) + g0_ref[...].astype(F32)) + g1_ref[...].astype(F32)
        o_ref[...] = tot + g2_ref[...].astype(F32)

    def got_spec(k):
        return pl.BlockSpec((None, None, tr, cc), lambda h, i, place: (k, h, i, 0))

    grid_spec = pltpu.PrefetchScalarGridSpec(
        num_scalar_prefetch=1, grid=(hl, rr // tr),
        in_specs=[pl.BlockSpec((None, None, tr, cc), lambda h, i, place: (h, place[0], i, 0)),
                  got_spec(0), got_spec(1), got_spec(2)],
        out_specs=pl.BlockSpec((None, tr, cc), lambda h, i, place: (place[1] * hl + h, i, 0)))
    place = jnp.stack([2 * lax.axis_index("x") + lax.axis_index("y"), lax.axis_index("c")]).astype(jnp.int32)
    return pl.pallas_call(body, out_shape=jax.ShapeDtypeStruct((2 * hl, rr, cc), F32), grid_spec=grid_spec,
                          name="reduce_sum4", compiler_params=_params(("parallel", "parallel")))(
        place, psum, got, got, got)


def _pair_gather(sums):
    n = len(sums)
    hl = sums[0].shape[0] // 2

    def body(*refs):
        bufs = refs[n:2 * n]
        send, recv = refs[2 * n:]
        x, y, c, _ = _place()
        mine = pl.ds(c * hl, hl)
        cps = [_remote(bufs[i].at[mine], bufs[i].at[mine], send.at[i], recv.at[i], (x, y, 1 - c)) for i in range(n)]
        for cp in cps:
            cp.start()
        for cp in cps:
            cp.wait()

    out_shape = [jax.ShapeDtypeStruct(v.shape, v.dtype) for v in sums]
    sems = [pltpu.SemaphoreType.DMA((n,)) for _ in range(2)]
    return pl.pallas_call(body, out_shape=out_shape, in_specs=[HBM] * n, out_specs=[HBM] * n, scratch_shapes=sems,
                          input_output_aliases={i: i for i in range(n)}, name="reduce_pair_gather")(*sums)


def _allreduce_small(vec):
    rows = vec.shape[0]

    def body(v_ref, o_ref, all_ref, send, recv):
        x, y, c, _ = _place()
        me = 4 * x + 2 * y + c
        all_ref[me] = v_ref[...]
        cps = []
        for dlt in range(1, N_DEV):
            fx, fy, fc = (dlt >> 2) & 1, (dlt >> 1) & 1, dlt & 1
            to = (1 - x if fx else x, 1 - y if fy else y, 1 - c if fc else c)
            cps.append(_remote(v_ref, all_ref.at[me], send.at[dlt - 1], recv.at[dlt - 1], to))
        for cp in cps:
            cp.start()
        for cp in cps:
            cp.wait()
        tot = all_ref[0]
        for dev in range(1, N_DEV):
            tot = tot + all_ref[dev]
        o_ref[...] = tot

    vm = pl.BlockSpec(memory_space=pltpu.VMEM)
    return pl.pallas_call(
        body, out_shape=jax.ShapeDtypeStruct(vec.shape, F32), in_specs=[vm], out_specs=vm,
        scratch_shapes=[pltpu.VMEM((N_DEV, rows, 128), F32), pltpu.SemaphoreType.DMA((N_DEV - 1,)),
                        pltpu.SemaphoreType.DMA((N_DEV - 1,))],
        name="allreduce_small", compiler_params=pltpu.CompilerParams(vmem_limit_bytes=VMEM_LIMIT))(vec)


def _adamw(name, w, g, m, v):
    rows, cc = w.shape
    tm = math.gcd(rows, 256)
    return _rowwise(name, _adamw_fn, [R(w), R(g), R(m), R(v)], [O(cc, F32), O(cc, F32), O(cc, F32)], tm=tm)


def _pack(parts):
    flat = jnp.concatenate([v.reshape(-1).astype(F32) for v in parts])
    pad = (-flat.shape[0]) % (SUBLANES * 128)
    return jnp.pad(flat, (0, pad)).reshape(-1, 128)


def _unpack(packed, shapes):
    flat, out, pos = packed.reshape(-1), [], 0
    for shp in shapes:
        size = math.prod(shp)
        out.append(flat[pos:pos + size].reshape(shp))
        pos += size
    return out


def kernel(x, p, positions, mix_norm_g, w_in, b_gate, attn_sinks, w_attn_out, ssm_lambda_re, ssm_lambda_im, ssm_log_dt, ssm_b_re, ssm_b_im, ssm_c_re, ssm_c_im, ssm_d, w_ssm_glu, b_ssm_glu, conv_dw_w, conv_dw_b, conv_norm_g, conv_norm_b, w_conv_out, w_mix_out, ffn_norm_g, w_ffn_in, w_ffn_out, w_ple_in, ple_norm_g, w_ple_gate, final_norm_g, loss_target, m_mix_norm_g, m_w_in, m_b_gate, m_attn_sinks, m_w_attn_out, m_ssm_lambda_re, m_ssm_lambda_im, m_ssm_log_dt, m_ssm_b_re, m_ssm_b_im, m_ssm_c_re, m_ssm_c_im, m_ssm_d, m_w_ssm_glu, m_b_ssm_glu, m_conv_dw_w, m_conv_dw_b, m_conv_norm_g, m_conv_norm_b, m_w_conv_out, m_w_mix_out, m_ffn_norm_g, m_w_ffn_in, m_w_ffn_out, m_w_ple_in, m_ple_norm_g, m_w_ple_gate, m_final_norm_g, v_mix_norm_g, v_w_in, v_b_gate, v_attn_sinks, v_w_attn_out, v_ssm_lambda_re, v_ssm_lambda_im, v_ssm_log_dt, v_ssm_b_re, v_ssm_b_im, v_ssm_c_re, v_ssm_c_im, v_ssm_d, v_w_ssm_glu, v_b_ssm_glu, v_conv_dw_w, v_conv_dw_b, v_conv_norm_g, v_conv_norm_b, v_w_conv_out, v_w_mix_out, v_ffn_norm_g, v_w_ffn_in, v_w_ffn_out, v_w_ple_in, v_ple_norm_g, v_w_ple_gate, v_final_norm_g):
    given = dict(locals())
    wts = {nm: given[nm] for nm in WEIGHTS}
    mom = {nm: given["m_" + nm] for nm in WEIGHTS}
    var = {nm: given["v_" + nm] for nm in WEIGHTS}
    depth = p.shape[0]
    chip = 2 * lax.axis_index("x") + lax.axis_index("y")

    cw_cols = conv_dw_w.shape[2]
    taps = jnp.pad(conv_dw_w.reshape(depth, -1), ((0, 0), (0, (-CONV_K * cw_cols) % (SUBLANES * 128))))
    shards = [{**{nm: wts[nm][l].astype(BF16) for nm in BIG}, TAPS: taps[l].reshape(-1, 128)} for l in range(depth)]
    small = {nm: wts[nm] for nm in SMALL if nm != "conv_dw_w"}

    loss_cols, grad_x, gbig, gsmall = _local_step(x, p, positions, loss_target, small, shards=shards)

    parts = [loss_cols] + [gsmall[nm] for nm in SMALL]
    total = _allreduce_small(_pack(parts))
    summed = _unpack(total, [v.shape for v in parts])
    loss = jnp.sum(summed[0])
    gsum = dict(zip(SMALL, summed[1:]))
    gsum["conv_dw_w"] = lax.dynamic_slice_in_dim(gsum["conv_dw_w"], chip * cw_cols, cw_cols, axis=2)
    shapes = [wts[nm].shape for nm in SMALL]
    deltas, new_m, new_v = _adamw("adamw_small", _pack([wts[nm] for nm in SMALL]), _pack([gsum[nm] for nm in SMALL]),
                                  _pack([mom[nm] for nm in SMALL]), _pack([var[nm] for nm in SMALL]))
    grads = dict(gsum)
    delta = dict(zip(SMALL, _unpack(deltas, shapes)))
    newm = dict(zip(SMALL, _unpack(new_m, shapes)))
    newv = dict(zip(SMALL, _unpack(new_v, shapes)))

    gl = [gbig[nm] for nm in BIG]
    sib = _pair_exchange(gl)
    psums = [_pair_add(g, r) for g, r in zip(gl, sib)]
    got = _chip_exchange(psums)
    sums = _pair_gather([_sum4(ps, g) for ps, g in zip(psums, got)])
    for nm, g in zip(BIG, sums):
        shp = wts[nm].shape
        two = lambda v: v.reshape(-1, shp[-1])
        g = g.reshape(shp)
        d_w, n_m, n_v = _adamw("adamw_" + nm, two(wts[nm]), two(g), two(mom[nm]), two(var[nm]))
        grads[nm], delta[nm], newm[nm], newv[nm] = g, d_w.reshape(shp), n_m.reshape(shp), n_v.reshape(shp)

    return (loss, grad_x, *[grads[nm] for nm in WEIGHTS], *[delta[nm] for nm in WEIGHTS],
            *[newm[nm] for nm in WEIGHTS], *[newv[nm] for nm in WEIGHTS])
```

```python
import functools
import math

import jax
import jax.numpy as jnp
from jax import lax
from jax.experimental import pallas as pl
from jax.experimental.pallas import tpu as pltpu

F32 = jnp.float32
BF16 = jnp.bfloat16

D_MODEL = 1024
HEAD_DIM = 64
N_Q_HEADS = 8
N_KV_HEADS = 2
GQA_GROUP = N_Q_HEADS // N_KV_HEADS
ATT_BLOCK = 128
ROPE_THETA = 500000.0
ROPE_DIM = HEAD_DIM // 4
ROPE_HALF = ROPE_DIM // 2
Q_WIDTH = N_Q_HEADS * HEAD_DIM
KV_WIDTH = N_KV_HEADS * HEAD_DIM
SSM_WIDTH = 256
SSM_GROUP = 16
SSM_GROUPS = 16
SSM_STATE = 64
SSM_LANES = SSM_GROUPS * SSM_STATE
CONV_WIDTH = 256
CONV_K = 31
CONV_HALO = 32
FFN_HIDDEN = 2816
EPS = 1e-6
NEG_INF = -1e30
SCALE = HEAD_DIM ** -0.5

ADAM_LR = 0.001
ADAM_B1 = 0.9
ADAM_B2 = 0.999
ADAM_EPS = 1e-08
ADAM_WD = 0.01
ADAM_STEP = 10

N_CHIPS = 4
N_DEV = 8
SUBLANES = 8
VMEM_LIMIT = 56 * 1024 * 1024

MESH = pl.DeviceIdType.MESH


def _params(sem=None):
    return pltpu.CompilerParams(dimension_semantics=sem, vmem_limit_bytes=VMEM_LIMIT)


def R(arr, width=None, cb=0, rb=0):
    return ("r", arr, arr.shape[1] if width is None else width, (cb, rb))


def V(arr, width=None, cb=0):
    return ("v", arr, arr.shape[1] if width is None else width, cb)


def _cbf(cb):
    return cb if callable(cb) else (lambda j, c=cb: c + j)


def _rowwise(name, fn, ins, outs, accs=(), *, tm, ncol=1):
    t = [a for k, a, _, _ in ins if k == "r"][0].shape[0]
    tm = min(tm, t)
    assert t % tm == 0, (name, t, tm)
    n_i, n_o, n_a = len(ins), len(outs), len(accs)

    def body(*refs):
        vals = fn(*[r[...] for r in refs[:n_i]])
        if not isinstance(vals, (tuple, list)):
            vals = (vals,)
        for ref, val in zip(refs[n_i:n_i + n_o], vals[:n_o]):
            ref[...] = val.astype(ref.dtype)
        if n_a:
            acc_refs = refs[n_i + n_o:]

            @pl.when(pl.program_id(1) == 0)
            def _():
                for ref in acc_refs:
                    ref[...] = jnp.zeros_like(ref)

            for ref, val in zip(acc_refs, vals[n_o:]):
                ref[...] += val

    in_specs = []
    for kind, arr, width, cb in ins:
        if kind == "r":
            f = _cbf(cb[0])
            in_specs.append(pl.BlockSpec((tm, width), functools.partial(lambda j, i, f, rb: (i + rb, f(j)), f=f, rb=cb[1])))
        else:
            f = _cbf(cb)
            in_specs.append(pl.BlockSpec((arr.shape[0], width), functools.partial(lambda j, i, f: (0, f(j)), f=f)))
    out_specs, out_shape = [], []
    for total, width, cb, dt in outs:
        f = _cbf(cb)
        out_specs.append(pl.BlockSpec((tm, width), functools.partial(lambda j, i, f: (i, f(j)), f=f)))
        out_shape.append(jax.ShapeDtypeStruct((t, total), dt))
    for total, width, cb in accs:
        f = _cbf(cb)
        out_specs.append(pl.BlockSpec((1, width), functools.partial(lambda j, i, f: (0, f(j)), f=f)))
        out_shape.append(jax.ShapeDtypeStruct((1, total), F32))
    sem = ("arbitrary", "arbitrary") if n_a else ("parallel", "parallel")
    res = pl.pallas_call(body, out_shape=out_shape, grid=(ncol, t // tm), in_specs=in_specs, out_specs=out_specs,
                         name=name, compiler_params=_params(sem))(*[a for _, a, _, _ in ins])
    return res[0] if len(res) == 1 else res


def O(width, dtype, total=None, cb=0):
    return (width if total is None else total, width, cb, dtype)


def A(width, total=None, cb=0):
    return (width if total is None else total, width, cb)


_DIMS = {"nn": (((1,), (0,)), ((), ())), "nt": (((1,), (1,)), ((), ())), "tn": (((0,), (0,)), ((), ()))}


def _mm(name, a, b, mode, out_dtype, *, m, n, k, tm, tn, tk, a_off=0, b_off=0, res=None, bias=None, b_sh=None, o_sh=None,
        comm=None):
    tm, tn, tk = min(tm, m), min(tn, n), min(tk, k)
    assert m % tm == 0 and n % tn == 0 and k % tk == 0, (name, m, n, k, tm, tn, tk)
    nk = k // tk
    has_res, has_bias = res is not None, bias is not None
    a_fn, a_ops = a if isinstance(a, tuple) else (None, [(a, None)])
    b_fn, b_ops = b if isinstance(b, tuple) else (None, [(b, None)])
    na, nb_ = len(a_ops), len(b_ops)
    a_bytes = sum(m * k * arr.dtype.itemsize for arr, _ in a_ops)
    b_bytes = sum(n * k * arr.dtype.itemsize for arr, _ in b_ops)
    swap = nk == 1 and b_bytes + (n // tn) * a_bytes < a_bytes + (m // tm) * b_bytes
    grid = (n // tn, m // tm, nk) if swap else (m // tm, n // tn, nk)
    ncomm = 0 if comm is None else len(comm["ins"])

    def body(*refs):
        g0, g1, kk = pl.program_id(0), pl.program_id(1), pl.program_id(2)
        gi, gj = (g1, g0) if swap else (g0, g1)
        a_tiles = [r[...] for r in refs[:na]]
        b_tiles = [r[...] for r in refs[na:na + nb_]]
        a_val = a_tiles[0] if a_fn is None else a_fn(gi, gj, kk, *a_tiles)
        b_val = b_tiles[0] if b_fn is None else b_fn(gi, gj, kk, *b_tiles)
        pos = na + nb_
        res_ref = bias_ref = None
        if has_res:
            res_ref = refs[pos]
            pos += 1
        if has_bias:
            bias_ref = refs[pos]
            pos += 1
        comm_ins = refs[pos:pos + ncomm]
        o_ref = refs[pos + ncomm]
        comm_outs = refs[pos + ncomm + 1:pos + 2 * ncomm + 1]
        scratch = refs[pos + 2 * ncomm + 1:]
        if comm is not None:
            sems = scratch[1:] if nk > 1 else scratch

            @pl.when((g0 == 0) & (g1 == 0) & (kk == 0))
            def _():
                comm["start"](comm_ins, comm_outs, sems)

        def finish(r):
            if has_bias:
                r = r + bias_ref[...]
            if has_res:
                r = r + res_ref[...].astype(F32)
            o_ref[...] = r.astype(o_ref.dtype)

        part = lax.dot_general(a_val.astype(BF16), b_val.astype(BF16), _DIMS[mode], preferred_element_type=F32)
        if nk == 1:
            finish(part)
        else:
            acc_ref = scratch[0]

            @pl.when(kk == 0)
            def _():
                acc_ref[...] = part

            @pl.when(kk > 0)
            def _():
                acc_ref[...] += part

            @pl.when(kk == nk - 1)
            def _():
                finish(acc_ref[...])

        if comm is not None:
            @pl.when((g0 == grid[0] - 1) & (g1 == grid[1] - 1) & (kk == nk - 1))
            def _():
                comm["finish"](comm_ins, comm_outs, sems)

    def at(f):
        return (lambda g0, g1, kk: f(g1, g0, kk)) if swap else f

    if mode == "nn":
        a_spec = pl.BlockSpec((tm, tk), at(lambda i, j, kk: (i, kk + a_off)))
        b_spec = pl.BlockSpec((tk, tn), at(lambda i, j, kk: (kk, j + b_off)))
        if b_sh is not None:
            assert b_sh % tn == 0, (name, b_sh, tn)
            per = b_sh // tn
            b_spec = pl.BlockSpec((None, tk, tn), at(lambda i, j, kk: (j // per, kk, j % per)))
    elif mode == "nt":
        a_spec = pl.BlockSpec((tm, tk), at(lambda i, j, kk: (i, kk + a_off)))
        b_spec = pl.BlockSpec((tn, tk), at(lambda i, j, kk: (j, kk + b_off)))
        if b_sh is not None:
            assert b_sh % tk == 0, (name, b_sh, tk)
            per = b_sh // tk
            b_spec = pl.BlockSpec((None, tn, tk), at(lambda i, j, kk: (kk // per, j, kk % per)))
    else:
        a_spec = pl.BlockSpec((tk, tm), at(lambda i, j, kk: (kk, i + a_off)))
        b_spec = pl.BlockSpec((tk, tn), at(lambda i, j, kk: (kk, j + b_off)))
    a_specs = [a_spec] if a_fn is None else [pl.BlockSpec(a_spec.block_shape, at(f)) for _, f in a_ops]
    b_specs = [b_spec] if b_fn is None else [pl.BlockSpec(b_spec.block_shape, at(f)) for _, f in b_ops]
    in_specs, args = a_specs + b_specs, [arr for arr, _ in a_ops] + [arr for arr, _ in b_ops]
    if has_res:
        in_specs.append(pl.BlockSpec((tm, tn), at(lambda i, j, kk: (i, j))))
        args.append(res)
    if has_bias:
        in_specs.append(pl.BlockSpec((1, tn), at(lambda i, j, kk: (0, j))))
        args.append(bias)
    out_spec, out_shape = pl.BlockSpec((tm, tn), at(lambda i, j, kk: (i, j))), (m, n)
    if o_sh is not None:
        assert o_sh % tn == 0, (name, o_sh, tn)
        per_o = o_sh // tn
        out_spec = pl.BlockSpec((None, tm, tn), at(lambda i, j, kk: (j // per_o, i, j % per_o)))
        out_shape = (n // o_sh, m, o_sh)
    scratch = [pltpu.VMEM((tm, tn), F32)] if nk > 1 else []
    if comm is None:
        return pl.pallas_call(
            body, out_shape=jax.ShapeDtypeStruct(out_shape, out_dtype), grid=grid, in_specs=in_specs,
            out_specs=out_spec, scratch_shapes=scratch, name=name,
            compiler_params=_params(("parallel", "parallel", "arbitrary")))(*args)
    outs = pl.pallas_call(
        body, out_shape=[jax.ShapeDtypeStruct(out_shape, out_dtype)] + comm["out_shapes"], grid=grid,
        in_specs=in_specs + [HBM] * ncomm, out_specs=[out_spec] + [HBM] * ncomm,
        scratch_shapes=scratch + comm["sems"], name=name,
        compiler_params=_params(("arbitrary", "arbitrary", "arbitrary")))(*args, *comm["ins"])
    return outs[0], outs[1:]


def _sig(v):
    return jax.nn.sigmoid(v)


def _rms_fwd(x, g):
    r = lax.rsqrt(jnp.mean(x * x, axis=-1, keepdims=True) + EPS)
    return x * r * g


def _rms_bwd(dh, x, dres, g):
    dh = dh.astype(F32)
    r = lax.rsqrt(jnp.mean(x * x, axis=-1, keepdims=True) + EPS)
    xh = x * r
    dxh = dh * g
    dx = r * (dxh - xh * jnp.mean(dxh * xh, axis=-1, keepdims=True))
    return dres + dx, jnp.sum(dh * xh, axis=0, keepdims=True)


def _rope_apply(t, c, sa, sb):
    w = t.shape[1]
    return t * c + pltpu.roll(t, w - ROPE_HALF, 1) * sa + pltpu.roll(t, ROPE_HALF, 1) * sb


def _rope_transpose(g, c, sa, sb):
    w = g.shape[1]
    return g * c + pltpu.roll(g * sa, ROPE_HALF, 1) + pltpu.roll(g * sb, w - ROPE_HALF, 1)


def _tile_lanes(tab, reps):
    return jnp.concatenate([tab] * reps, axis=1) if reps > 1 else tab


def _rope_fwd(q, k, c, sa, sb):
    rq = Q_WIDTH // c.shape[1]
    qr = _rope_apply(q.astype(F32), _tile_lanes(c, rq), _tile_lanes(sa, rq), _tile_lanes(sb, rq))
    kr = _rope_apply(k.astype(F32), c, sa, sb)
    return qr, kr


def _rope_bwd_q(g, c, sa, sb):
    rq = Q_WIDTH // c.shape[1]
    return _rope_transpose(g.astype(F32), _tile_lanes(c, rq), _tile_lanes(sa, rq), _tile_lanes(sb, rq))


def _gelu(v):
    return jax.nn.gelu(v, approximate=True)


def _gelu_grad(v):
    c0 = math.sqrt(2.0 / math.pi)
    inner = c0 * (v + 0.044715 * v * v * v)
    th = jnp.tanh(inner)
    return 0.5 * (1.0 + th) + 0.5 * v * (1.0 - th * th) * c0 * (1.0 + 3 * 0.044715 * v * v)


def _merge_fwd(g0, g1, g2, b0, b1, b2, ya, ga, gb, yc):
    s0 = _sig(g0.astype(F32) + b0)
    s1 = _sig(g1.astype(F32) + b1)
    s2 = _sig(g2.astype(F32) + b2)
    ys = ga.astype(F32) * _sig(gb.astype(F32))
    return s0 * ya.astype(F32) + s1 * ys + s2 * yc.astype(F32)


def _merge_bwd(dm, g0, g1, g2, b0, b1, b2, ya, ga, gb, yc):
    dm = dm.astype(F32)
    s0 = _sig(g0.astype(F32) + b0)
    s1 = _sig(g1.astype(F32) + b1)
    s2 = _sig(g2.astype(F32) + b2)
    ga = ga.astype(F32)
    sb = _sig(gb.astype(F32))
    ys = ga * sb
    dya = dm * s0
    dys = dm * s1
    dyc = dm * s2
    dga = dys * sb
    dgb = dys * ga * sb * (1.0 - sb)
    d0 = dm * ya.astype(F32) * s0 * (1.0 - s0)
    d1 = dm * ys * s1 * (1.0 - s1)
    d2 = dm * yc.astype(F32) * s2 * (1.0 - s2)
    cs = lambda v: jnp.sum(v, axis=0, keepdims=True)
    return dya, dga, dgb, dyc, d0, d1, d2, cs(d0), cs(d1), cs(d2), cs(dga), cs(dgb)


def _ffn_act(fg, fu):
    fg = fg.astype(F32)
    return fg * _sig(fg) * fu.astype(F32)


def _ffn_act_bwd(da, fg, fu):
    da, fg, fu = da.astype(F32), fg.astype(F32), fu.astype(F32)
    s = _sig(fg)
    return da * fu * (s * (1.0 + fg * (1.0 - s))), da * fg * s


def _ple_fwd(x, gp, e):
    return x + _sig(gp.astype(F32)) * e.astype(F32)


def _ple_bwd(dx, gp, e):
    s = _sig(gp.astype(F32))
    e = e.astype(F32)
    return dx * s, dx * e * s * (1.0 - s)


def _loss_fn(x, tgt, g):
    d = x.shape[1]
    r = lax.rsqrt(jnp.mean(x * x, axis=-1, keepdims=True) + EPS)
    xh = x * r
    err = xh * g - tgt
    dy = err * (1.0 / d)
    dxh = dy * g
    dx = r * (dxh - xh * jnp.mean(dxh * xh, axis=-1, keepdims=True))
    return dx, jnp.sum(err * err, axis=0, keepdims=True) * (0.5 / d), jnp.sum(dy * xh, axis=0, keepdims=True)


def _adamw_fn(w, g, m, v):
    m = ADAM_B1 * m + (1.0 - ADAM_B1) * g
    v = ADAM_B2 * v + (1.0 - ADAM_B2) * (g * g)
    m_hat = m / (1.0 - ADAM_B1 ** ADAM_STEP)
    v_hat = v / (1.0 - ADAM_B2 ** ADAM_STEP)
    delta = -ADAM_LR * (m_hat / (jnp.sqrt(v_hat) + ADAM_EPS) + ADAM_WD * w)
    return delta, m, v


def _band_mask(n):
    qi = lax.broadcasted_iota(jnp.int32, (ATT_BLOCK, 2 * ATT_BLOCK), 0)
    kj = lax.broadcasted_iota(jnp.int32, (ATT_BLOCK, 2 * ATT_BLOCK), 1)
    dist = qi + ATT_BLOCK - kj
    return (dist >= 0) & (dist < ATT_BLOCK) & ((n > 0) | (kj >= ATT_BLOCK))


def _att_specs(nb):
    cur = lambda b, n: (0, b * nb + n, 0)
    prev = lambda b, n: (0, b * nb + jnp.maximum(n - 1, 0), 0)
    qs = pl.BlockSpec((N_Q_HEADS, ATT_BLOCK, HEAD_DIM), cur)
    kc = pl.BlockSpec((N_KV_HEADS, ATT_BLOCK, HEAD_DIM), cur)
    kp = pl.BlockSpec((N_KV_HEADS, ATT_BLOCK, HEAD_DIM), prev)
    stat = pl.BlockSpec((N_Q_HEADS, ATT_BLOCK, 1), cur)
    sink = pl.BlockSpec((N_Q_HEADS, 1, 1), lambda b, n: (0, 0, 0))
    return qs, kc, kp, stat, sink


def _attn_fwd(qh, kh, vh, sinks, nbatch, seq):
    t = qh.shape[1]
    nb = seq // ATT_BLOCK
    qs, kc, kp, stat, sink = _att_specs(nb)

    def body(q_ref, kp_ref, kc_ref, vp_ref, vc_ref, sink_ref, o_ref, lse_ref):
        mask = _band_mask(pl.program_id(1))
        rows = GQA_GROUP * ATT_BLOCK
        for kv in range(N_KV_HEADS):
            hs = slice(kv * GQA_GROUP, (kv + 1) * GQA_GROUP)
            kk = jnp.concatenate([kp_ref[kv], kc_ref[kv]], axis=0)
            vv = jnp.concatenate([vp_ref[kv], vc_ref[kv]], axis=0)
            q4 = (q_ref[hs] * SCALE).reshape(rows, HEAD_DIM)
            s = lax.dot_general(q4, kk, _DIMS["nt"], preferred_element_type=F32)
            s = jnp.where(mask, s.reshape(GQA_GROUP, ATT_BLOCK, 2 * ATT_BLOCK), NEG_INF)
            sk = sink_ref[hs]
            mx = jnp.maximum(jnp.max(s, axis=-1, keepdims=True), sk)
            p = jnp.exp(s - mx)
            den = jnp.sum(p, axis=-1, keepdims=True) + jnp.exp(sk - mx)
            o = lax.dot_general(p.reshape(rows, 2 * ATT_BLOCK).astype(BF16), vv, _DIMS["nn"],
                                preferred_element_type=F32).reshape(GQA_GROUP, ATT_BLOCK, HEAD_DIM)
            o_ref[hs] = (o * (1.0 / den)).astype(o_ref.dtype)
            lse_ref[hs] = mx + jnp.log(den)

    return pl.pallas_call(
        body, grid=(nbatch, nb), in_specs=[qs, kp, kc, kp, kc, sink], out_specs=[qs, stat],
        out_shape=[jax.ShapeDtypeStruct((N_Q_HEADS, t, HEAD_DIM), BF16), jax.ShapeDtypeStruct((N_Q_HEADS, t, 1), F32)],
        name="attn_fwd", compiler_params=_params(("parallel", "parallel")))(qh, kh, kh, vh, vh, sinks)


def _attn_bwd(qh, kh, vh, oh, doh, lse, sinks, nbatch, seq):
    t = qh.shape[1]
    nb = seq // ATT_BLOCK
    qs, kc, kp, stat, sink = _att_specs(nb)

    def body(q_ref, kp_ref, kc_ref, vp_ref, vc_ref, o_ref, do_ref, lse_ref, sink_ref,
             dq_ref, dkc_ref, dvc_ref, dkp_ref, dvp_ref, dsink_ref):
        first = (pl.program_id(0) == 0) & (pl.program_id(1) == 0)

        @pl.when(first)
        def _():
            dsink_ref[...] = jnp.zeros_like(dsink_ref)

        mask = _band_mask(pl.program_id(1))
        rows = GQA_GROUP * ATT_BLOCK
        band = (GQA_GROUP, ATT_BLOCK, 2 * ATT_BLOCK)
        for kv in range(N_KV_HEADS):
            hs = slice(kv * GQA_GROUP, (kv + 1) * GQA_GROUP)
            kk = jnp.concatenate([kp_ref[kv], kc_ref[kv]], axis=0)
            vv = jnp.concatenate([vp_ref[kv], vc_ref[kv]], axis=0)
            q4 = q_ref[hs].reshape(rows, HEAD_DIM)
            do4 = do_ref[hs].reshape(rows, HEAD_DIM)
            lse4 = lse_ref[hs]
            s = lax.dot_general(q4 * SCALE, kk, _DIMS["nt"], preferred_element_type=F32).reshape(band)
            p = jnp.where(mask, jnp.exp(s - lse4), 0.0)
            dd = jnp.sum(do_ref[hs].astype(F32) * o_ref[hs].astype(F32), axis=-1, keepdims=True)
            dp = lax.dot_general(do4, vv, _DIMS["nt"], preferred_element_type=F32).reshape(band)
            ds = (p * (dp - dd) * SCALE).astype(BF16).reshape(rows, 2 * ATT_BLOCK)
            dq = lax.dot_general(ds, kk, _DIMS["nn"], preferred_element_type=F32)
            dq_ref[hs] = dq.reshape(GQA_GROUP, ATT_BLOCK, HEAD_DIM).astype(dq_ref.dtype)
            dk = lax.dot_general(ds, q4, _DIMS["tn"], preferred_element_type=F32)
            dv = lax.dot_general(p.astype(BF16).reshape(rows, 2 * ATT_BLOCK), do4, _DIMS["tn"],
                                 preferred_element_type=F32)
            dsink_ref[hs] += -jnp.sum(jnp.exp(sink_ref[hs] - lse4) * dd, axis=1, keepdims=True)
            dkp_ref[kv] = dk[:ATT_BLOCK]
            dkc_ref[kv] = dk[ATT_BLOCK:]
            dvp_ref[kv] = dv[:ATT_BLOCK]
            dvc_ref[kv] = dv[ATT_BLOCK:]

    kvs = jax.ShapeDtypeStruct((N_KV_HEADS, t, HEAD_DIM), F32)
    return pl.pallas_call(
        body, grid=(nbatch, nb), in_specs=[qs, kp, kc, kp, kc, qs, qs, stat, sink],
        out_specs=[qs, kc, kc, kc, kc, sink],
        out_shape=[jax.ShapeDtypeStruct((N_Q_HEADS, t, HEAD_DIM), F32), kvs, kvs, kvs, kvs,
                   jax.ShapeDtypeStruct((N_Q_HEADS, 1, 1), F32)],
        name="attn_bwd", compiler_params=_params(("arbitrary", "arbitrary")))(qh, kh, kh, vh, vh, oh, doh, lse, sinks)


def _kv_combine(dkc, dkp, dvc, dvp, c, sa, sb, seq):
    t = dkc.shape[0]
    nb = seq // ATT_BLOCK
    nblk = t // ATT_BLOCK

    def body(kc_ref, kp_ref, vc_ref, vp_ref, c_ref, sa_ref, sb_ref, dk_ref, dv_ref):
        has_next = (pl.program_id(0) % nb) != nb - 1
        dk = kc_ref[...] + jnp.where(has_next, kp_ref[...], 0.0)
        dv = vc_ref[...] + jnp.where(has_next, vp_ref[...], 0.0)
        dk_ref[...] = _rope_transpose(dk, c_ref[...], sa_ref[...], sb_ref[...]).astype(dk_ref.dtype)
        dv_ref[...] = dv.astype(dv_ref.dtype)

    cur = pl.BlockSpec((ATT_BLOCK, KV_WIDTH), lambda i: (i, 0))
    nxt = pl.BlockSpec((ATT_BLOCK, KV_WIDTH), lambda i: (jnp.minimum(i + 1, nblk - 1), 0))
    o = jax.ShapeDtypeStruct((t, KV_WIDTH), BF16)
    return pl.pallas_call(body, grid=(nblk,), in_specs=[cur, nxt, cur, nxt, cur, cur, cur], out_specs=[cur, cur],
                          out_shape=[o, o], name="kv_combine", compiler_params=_params(("parallel",)))(
        dkc, dkp, dvc, dvp, c, sa, sb)


def _scan_block(ref, tab_ref, carry, ngroups, reverse):
    shifts = (7, 6, 4) if reverse else (1, 2, 4)
    n = SSM_LANES

    def step(i, car):
        g = (ngroups - 1 - i) if reverse else i
        r0 = pl.multiple_of(g * SUBLANES, SUBLANES)
        xr = ref[pl.ds(r0, SUBLANES), :n]
        xi = ref[pl.ds(r0, SUBLANES), n:]
        for s, sh in enumerate(shifts):
            pr, pi = tab_ref[2 * s], tab_ref[2 * s + 1]
            yr, yi = pltpu.roll(xr, sh, 0), pltpu.roll(xi, sh, 0)
            xr, xi = xr + pr * yr - pi * yi, xi + pr * yi + pi * yr
        cr, ci = car
        qr, qi = tab_ref[6], tab_ref[7]
        xr, xi = xr + qr * cr - qi * ci, xi + qr * ci + qi * cr
        ref[pl.ds(r0, SUBLANES), :n] = xr
        ref[pl.ds(r0, SUBLANES), n:] = xi
        last = r0 if reverse else r0 + SUBLANES - 1
        return ref[pl.ds(last, 1), :n], ref[pl.ds(last, 1), n:]

    return lax.fori_loop(0, ngroups, step, carry, unroll=2)


def _ssm_chunk(seq):
    return min(512, seq)


def _ssm_fwd(z, wb, wc, tab, dskip, nbatch, seq):
    t = z.shape[0]
    tc = _ssm_chunk(seq)
    nc = seq // tc
    n2 = 2 * SSM_LANES

    def body(u_ref, wb_ref, wc_ref, tab_ref, d_ref, st_ref, y_ref, gel_ref, car_ref):
        @pl.when(pl.program_id(1) == 0)
        def _():
            car_ref[...] = jnp.zeros_like(car_ref)

        u = u_ref[...]
        st_ref[...] = lax.dot_general(u, wb_ref[...], _DIMS["nn"], preferred_element_type=F32)
        cr, ci = _scan_block(st_ref, tab_ref, (car_ref[:, :SSM_LANES], car_ref[:, SSM_LANES:]), tc // SUBLANES, False)
        car_ref[:, :SSM_LANES] = cr
        car_ref[:, SSM_LANES:] = ci
        y = lax.dot_general(st_ref[...].astype(BF16), wc_ref[...], _DIMS["nn"], preferred_element_type=F32)
        y = y + d_ref[...] * u.astype(F32)
        y_ref[...] = y
        gel_ref[...] = _gelu(y).astype(gel_ref.dtype)

    row = lambda b, c: (b * nc + c, 0)
    full = lambda b, c: (0, 0)
    return pl.pallas_call(
        body, grid=(nbatch, nc),
        in_specs=[pl.BlockSpec((tc, SSM_WIDTH), lambda b, c: (b * nc + c, 3)), pl.BlockSpec((SSM_WIDTH, n2), full),
                  pl.BlockSpec((n2, SSM_WIDTH), full), pl.BlockSpec((8, SUBLANES, SSM_LANES), lambda b, c: (0, 0, 0)),
                  pl.BlockSpec((1, SSM_WIDTH), full)],
        out_specs=[pl.BlockSpec((tc, n2), row), pl.BlockSpec((tc, SSM_WIDTH), row), pl.BlockSpec((tc, SSM_WIDTH), row)],
        out_shape=[jax.ShapeDtypeStruct((t, n2), F32), jax.ShapeDtypeStruct((t, SSM_WIDTH), F32),
                   jax.ShapeDtypeStruct((t, SSM_WIDTH), BF16)],
        scratch_shapes=[pltpu.VMEM((1, n2), F32)], name="ssm_fwd",
        compiler_params=_params(("arbitrary", "arbitrary")))(z, wb, wc, tab, dskip)


def _ssm_bwd(dgi, ys, st, z, wbt, wct, tab_rev, dskip, nbatch, seq):
    t = z.shape[0]
    tc = _ssm_chunk(seq)
    nc = seq // tc
    n = SSM_LANES
    n2 = 2 * n
    ng = tc // SUBLANES

    def body(dgi_ref, ys_ref, st_ref, stp_ref, u_ref, wbt_ref, wct_ref, tab_ref, d_ref,
             du_ref, dwb_ref, dwc_ref, dd_ref, da_ref, p_ref, sb_ref, car_ref):
        b, c = pl.program_id(0), pl.program_id(1)
        ct = nc - 1 - c

        @pl.when((b == 0) & (c == 0))
        def _():
            dwb_ref[...] = jnp.zeros_like(dwb_ref)
            dwc_ref[...] = jnp.zeros_like(dwc_ref)
            dd_ref[...] = jnp.zeros_like(dd_ref)
            da_ref[...] = jnp.zeros_like(da_ref)

        @pl.when(c == 0)
        def _():
            car_ref[...] = jnp.zeros_like(car_ref)

        u = u_ref[...]
        dys = dgi_ref[...].astype(F32) * _gelu_grad(ys_ref[...])
        dys_b = dys.astype(BF16)
        st = st_ref[...]
        dd_ref[...] += jnp.sum(dys * u.astype(F32), axis=0, keepdims=True)
        dwc_ref[...] += lax.dot_general(st.astype(BF16), dys_b, _DIMS["tn"], preferred_element_type=F32)
        p_ref[...] = lax.dot_general(dys_b, wct_ref[...], _DIMS["nn"], preferred_element_type=F32)
        cr, ci = _scan_block(p_ref, tab_ref, (car_ref[:, :n], car_ref[:, n:]), ng, True)
        car_ref[:, :n] = cr
        car_ref[:, n:] = ci
        p = p_ref[...]
        pb = p.astype(BF16)
        dwb_ref[...] += lax.dot_general(u, pb, _DIMS["tn"], preferred_element_type=F32)
        du = lax.dot_general(pb, wbt_ref[...], _DIMS["nn"], preferred_element_type=F32) + d_ref[...] * dys
        du_ref[...] = du.astype(du_ref.dtype)
        sb_ref[pl.ds(0, SUBLANES), :] = jnp.where(ct > 0, stp_ref[...], 0.0)
        sb_ref[pl.ds(SUBLANES, tc), :] = st
        row0 = lax.broadcasted_iota(jnp.int32, (SUBLANES, n), 0) == 0

        def acc_step(g, acc):
            ar, ai = acc
            r0 = pl.multiple_of(g * SUBLANES, SUBLANES)
            edge_r = sb_ref[pl.ds(r0 + SUBLANES - 1, 1), :n]
            edge_i = sb_ref[pl.ds(r0 + SUBLANES - 1, 1), n:]
            sr = jnp.where(row0, edge_r, pltpu.roll(sb_ref[pl.ds(r0 + SUBLANES, SUBLANES), :n], 1, 0))
            si = jnp.where(row0, edge_i, pltpu.roll(sb_ref[pl.ds(r0 + SUBLANES, SUBLANES), n:], 1, 0))
            pr = p_ref[pl.ds(r0, SUBLANES), :n]
            pi = p_ref[pl.ds(r0, SUBLANES), n:]
            return ar + pr * sr + pi * si, ai + pi * sr - pr * si

        zero = jnp.zeros((SUBLANES, n), F32)
        ar, ai = lax.fori_loop(0, ng, acc_step, (zero, zero), unroll=2)
        da_ref[:, :n] += ar
        da_ref[:, n:] += ai

    row = lambda b, c: (b * nc + (nc - 1 - c), 0)
    prev8 = lambda b, c: (jnp.maximum((b * nc + (nc - 1 - c)) * (tc // SUBLANES) - 1, 0), 0)
    full = lambda b, c: (0, 0)
    return pl.pallas_call(
        body, grid=(nbatch, nc),
        in_specs=[pl.BlockSpec((tc, SSM_WIDTH), row), pl.BlockSpec((tc, SSM_WIDTH), row), pl.BlockSpec((tc, n2), row),
                  pl.BlockSpec((SUBLANES, n2), prev8),
                  pl.BlockSpec((tc, SSM_WIDTH), lambda b, c: (b * nc + (nc - 1 - c), 3)),
                  pl.BlockSpec((n2, SSM_WIDTH), full), pl.BlockSpec((SSM_WIDTH, n2), full),
                  pl.BlockSpec((8, SUBLANES, n), lambda b, c: (0, 0, 0)), pl.BlockSpec((1, SSM_WIDTH), full)],
        out_specs=[pl.BlockSpec((tc, SSM_WIDTH), row), pl.BlockSpec((SSM_WIDTH, n2), full),
                   pl.BlockSpec((n2, SSM_WIDTH), full), pl.BlockSpec((1, SSM_WIDTH), full),
                   pl.BlockSpec((SUBLANES, n2), full)],
        out_shape=[jax.ShapeDtypeStruct((t, SSM_WIDTH), BF16), jax.ShapeDtypeStruct((SSM_WIDTH, n2), F32),
                   jax.ShapeDtypeStruct((n2, SSM_WIDTH), F32), jax.ShapeDtypeStruct((1, SSM_WIDTH), F32),
                   jax.ShapeDtypeStruct((SUBLANES, n2), F32)],
        scratch_shapes=[pltpu.VMEM((tc, n2), F32), pltpu.VMEM((tc + SUBLANES, n2), F32), pltpu.VMEM((1, n2), F32)],
        name="ssm_bwd", compiler_params=_params(("arbitrary", "arbitrary")))(
        dgi, ys, st, st, z, wbt, wct, tab_rev, dskip)


def _ssm_prep(lam_re, lam_im, log_dt, b_re, b_im, c_re, c_im):
    lr = jnp.minimum(lam_re, -1e-4)
    li = lam_im
    dt = jnp.exp(log_dt)[:, None]
    mag = jnp.exp(lr * dt)
    a_re = mag * jnp.cos(li * dt)
    a_im = mag * jnp.sin(li * dt)
    den = lr * lr + li * li
    x_re, x_im = a_re - 1.0, a_im
    f_re = (x_re * lr + x_im * li) / den
    f_im = (x_im * lr - x_re * li) / den
    bb_re = f_re[..., None] * b_re - f_im[..., None] * b_im
    bb_im = f_re[..., None] * b_im + f_im[..., None] * b_re
    eye = jnp.eye(SSM_GROUPS, dtype=F32)
    emb_b = lambda v: jnp.einsum("gnh,gk->ghkn", v, eye).reshape(SSM_WIDTH, SSM_LANES)
    emb_c = lambda v: jnp.einsum("ghn,gk->gnkh", v, eye).reshape(SSM_LANES, SSM_WIDTH)
    wb = jnp.concatenate([emb_b(bb_re), emb_b(bb_im)], axis=1)
    wc = jnp.concatenate([emb_c(c_re), -emb_c(c_im)], axis=0)
    return a_re.reshape(-1), a_im.reshape(-1), wb, wc


def _ssm_tables(a_re, a_im, reverse):
    if reverse:
        a_im = -a_im
    pw = [(a_re, a_im)]
    for _ in range(SUBLANES - 1):
        pr, pi = pw[-1]
        pw.append((pr * a_re - pi * a_im, pr * a_im + pi * a_re))
    rows = jnp.arange(SUBLANES)[:, None]
    tabs = []
    for k in (1, 2, 4):
        ok = (rows + k <= SUBLANES - 1) if reverse else (rows >= k)
        tabs += [jnp.where(ok, pw[k - 1][0][None], 0.0), jnp.where(ok, pw[k - 1][1][None], 0.0)]
    order = list(range(SUBLANES - 1, -1, -1)) if reverse else list(range(SUBLANES))
    tabs += [jnp.stack([pw[i][0] for i in order]), jnp.stack([pw[i][1] for i in order])]
    return jnp.stack(tabs)


def _conv_chunk(seq):
    return min(512, seq)


def _conv_fwd(z, w, bias, lg, lb, nbatch, seq):
    t = z.shape[0]
    tc = _conv_chunk(seq)
    nc = seq // tc

    def body(a_ref, g_ref, w_ref, b_ref, lg_ref, lb_ref, cv_ref, sc_ref, ubuf):
        c = pl.program_id(1)

        @pl.when(c == 0)
        def _():
            ubuf[pl.ds(0, CONV_HALO), :] = jnp.zeros((CONV_HALO, CONV_WIDTH), F32)

        @pl.when(c > 0)
        def _():
            ubuf[pl.ds(0, CONV_HALO), :] = ubuf[pl.ds(tc, CONV_HALO), :]

        ubuf[pl.ds(CONV_HALO, tc), :] = a_ref[...].astype(F32) * _sig(g_ref[...].astype(F32))
        acc = jnp.zeros((tc, CONV_WIDTH), F32) + b_ref[...]
        for k in range(CONV_K):
            acc = acc + w_ref[pl.ds(k, 1), :] * ubuf[pl.ds(CONV_HALO - (CONV_K - 1) + k, tc), :]
        cv_ref[...] = acc
        mu = jnp.mean(acc, axis=-1, keepdims=True)
        xc = acc - mu
        y = xc * lax.rsqrt(jnp.mean(xc * xc, axis=-1, keepdims=True) + EPS) * lg_ref[...] + lb_ref[...]
        sc_ref[...] = (y * _sig(y)).astype(sc_ref.dtype)

    row = lambda b, c: (b * nc + c, 0)
    full = lambda b, c: (0, 0)
    vec = pl.BlockSpec((1, CONV_WIDTH), full)
    return pl.pallas_call(
        body, grid=(nbatch, nc),
        in_specs=[pl.BlockSpec((tc, CONV_WIDTH), lambda b, c: (b * nc + c, 4)),
                  pl.BlockSpec((tc, CONV_WIDTH), lambda b, c: (b * nc + c, 5)),
                  pl.BlockSpec((CONV_HALO, CONV_WIDTH), full), vec, vec, vec],
        out_specs=[pl.BlockSpec((tc, CONV_WIDTH), row), pl.BlockSpec((tc, CONV_WIDTH), row)],
        out_shape=[jax.ShapeDtypeStruct((t, CONV_WIDTH), F32), jax.ShapeDtypeStruct((t, CONV_WIDTH), BF16)],
        scratch_shapes=[pltpu.VMEM((CONV_HALO + tc, CONV_WIDTH), F32)], name="conv_fwd",
        compiler_params=_params(("arbitrary", "arbitrary")))(z, z, w, bias, lg, lb)


def _conv_bwd(dsc, cv, z, w, lg, lb, nbatch, seq):
    t = z.shape[0]
    tc = _conv_chunk(seq)
    nc = seq // tc
    hb = tc // CONV_HALO

    def body(dsc_ref, cv_ref, a_ref, g_ref, ap_ref, gp_ref, w_ref, lg_ref, lb_ref,
             da_ref, dg_ref, dw_ref, db_ref, dlg_ref, dlb_ref, ubuf, dbuf):
        b, c = pl.program_id(0), pl.program_id(1)
        ct = nc - 1 - c

        @pl.when((b == 0) & (c == 0))
        def _():
            dw_ref[...] = jnp.zeros_like(dw_ref)
            db_ref[...] = jnp.zeros_like(db_ref)
            dlg_ref[...] = jnp.zeros_like(dlg_ref)
            dlb_ref[...] = jnp.zeros_like(dlb_ref)

        cvv = cv_ref[...]
        mu = jnp.mean(cvv, axis=-1, keepdims=True)
        xc = cvv - mu
        rstd = lax.rsqrt(jnp.mean(xc * xc, axis=-1, keepdims=True) + EPS)
        xh = xc * rstd
        y = xh * lg_ref[...] + lb_ref[...]
        sy = _sig(y)
        dy = dsc_ref[...].astype(F32) * (sy * (1.0 + y * (1.0 - sy)))
        dlg_ref[...] += jnp.sum(dy * xh, axis=0, keepdims=True)
        dlb_ref[...] += jnp.sum(dy, axis=0, keepdims=True)
        dxh = dy * lg_ref[...]
        dcv = rstd * (dxh - jnp.mean(dxh, axis=-1, keepdims=True) - xh * jnp.mean(dxh * xh, axis=-1, keepdims=True))
        db_ref[...] += jnp.sum(dcv, axis=0, keepdims=True)

        @pl.when(c == 0)
        def _():
            dbuf[pl.ds(tc, CONV_HALO), :] = jnp.zeros((CONV_HALO, CONV_WIDTH), F32)

        @pl.when(c > 0)
        def _():
            dbuf[pl.ds(tc, CONV_HALO), :] = dbuf[pl.ds(0, CONV_HALO), :]

        dbuf[pl.ds(0, tc), :] = dcv
        a = a_ref[...].astype(F32)
        sg = _sig(g_ref[...].astype(F32))
        ubuf[pl.ds(0, CONV_HALO), :] = jnp.where(ct > 0, ap_ref[...].astype(F32) * _sig(gp_ref[...].astype(F32)), 0.0)
        ubuf[pl.ds(CONV_HALO, tc), :] = a * sg
        du = jnp.zeros((tc, CONV_WIDTH), F32)
        for k in range(CONV_K):
            du = du + w_ref[pl.ds(k, 1), :] * dbuf[pl.ds(CONV_K - 1 - k, tc), :]
            dw_ref[pl.ds(k, 1), :] += jnp.sum(dcv * ubuf[pl.ds(CONV_HALO - (CONV_K - 1) + k, tc), :],
                                             axis=0, keepdims=True)
        da_ref[...] = (du * sg).astype(da_ref.dtype)
        dg_ref[...] = (du * a * sg * (1.0 - sg)).astype(dg_ref.dtype)

    row = lambda b, c: (b * nc + (nc - 1 - c), 0)
    full = lambda b, c: (0, 0)
    vec = pl.BlockSpec((1, CONV_WIDTH), full)
    blk = pl.BlockSpec((tc, CONV_WIDTH), row)

    def zcol(col):
        return pl.BlockSpec((tc, CONV_WIDTH), lambda b, c: (b * nc + (nc - 1 - c), col))

    def zprev(col):
        return pl.BlockSpec((CONV_HALO, CONV_WIDTH),
                            lambda b, c: (jnp.maximum((b * nc + (nc - 1 - c)) * hb - 1, 0), col))

    o = jax.ShapeDtypeStruct((t, CONV_WIDTH), BF16)
    v = jax.ShapeDtypeStruct((1, CONV_WIDTH), F32)
    return pl.pallas_call(
        body, grid=(nbatch, nc),
        in_specs=[blk, blk, zcol(4), zcol(5), zprev(4), zprev(5), pl.BlockSpec((CONV_HALO, CONV_WIDTH), full), vec, vec],
        out_specs=[blk, blk, pl.BlockSpec((CONV_HALO, CONV_WIDTH), full), vec, vec, vec],
        out_shape=[o, o, jax.ShapeDtypeStruct((CONV_HALO, CONV_WIDTH), F32), v, v, v],
        scratch_shapes=[pltpu.VMEM((CONV_HALO + tc, CONV_WIDTH), F32), pltpu.VMEM((tc + CONV_HALO, CONV_WIDTH), F32)],
        name="conv_bwd", compiler_params=_params(("arbitrary", "arbitrary")))(dsc, cv, z, z, z, z, w, lg, lb)


BIG = ("w_in", "w_attn_out", "w_ssm_glu", "w_conv_out", "w_mix_out", "w_ffn_in", "w_ffn_out", "w_ple_in", "w_ple_gate")
BIG_AXIS = {"w_in": 2, "w_attn_out": 2, "w_ssm_glu": 2, "w_conv_out": 2, "w_mix_out": 1, "w_ffn_in": 2,
            "w_ffn_out": 1, "w_ple_in": 2, "w_ple_gate": 1}
SHARD_MAJOR = ("w_in", "w_ffn_in")
SMALL = ("mix_norm_g", "b_gate", "attn_sinks", "ssm_lambda_re", "ssm_lambda_im", "ssm_log_dt", "ssm_b_re", "ssm_b_im",
         "ssm_c_re", "ssm_c_im", "ssm_d", "b_ssm_glu", "conv_dw_w", "conv_dw_b", "conv_norm_g", "conv_norm_b",
         "ffn_norm_g", "ple_norm_g", "final_norm_g")
WEIGHTS = ("mix_norm_g", "w_in", "b_gate", "attn_sinks", "w_attn_out", "ssm_lambda_re", "ssm_lambda_im", "ssm_log_dt",
           "ssm_b_re", "ssm_b_im", "ssm_c_re", "ssm_c_im", "ssm_d", "w_ssm_glu", "b_ssm_glu", "conv_dw_w", "conv_dw_b",
           "conv_norm_g", "conv_norm_b", "w_conv_out", "w_mix_out", "ffn_norm_g", "w_ffn_in", "w_ffn_out", "w_ple_in",
           "ple_norm_g", "w_ple_gate", "final_norm_g")
SSM_NAMES = ("ssm_lambda_re", "ssm_lambda_im", "ssm_log_dt", "ssm_b_re", "ssm_b_im", "ssm_c_re", "ssm_c_im")


def _heads(v, nh):
    return v.reshape(v.shape[0], nh, HEAD_DIM).transpose(1, 0, 2)


def _tokens(v):
    return v.transpose(1, 0, 2).reshape(v.shape[1], v.shape[0] * HEAD_DIM)


def _row(v):
    return v.reshape(1, -1)


def _layer_fwd(x, p_l, w, s, rope, nbatch, seq, next_shards=None):
    t = x.shape[0]
    tm = 512
    d = D_MODEL
    sv = {}
    sv["x"] = x
    h = _rowwise("rms_mix", _rms_fwd, [R(x), V(_row(s["mix_norm_g"]))], [O(d, BF16)], tm=tm)
    cs = {nm: w[nm].shape[2] for nm in SHARD_MAJOR}
    tb = 1024
    got = {}
    plan = None if next_shards is None else _gather_plan(next_shards, GATHER_A)
    z = _mm("mm_in", h, w["w_in"], "nn", BF16, m=t, n=N_CHIPS * cs["w_in"], k=d, tm=tb, tn=cs["w_in"], tk=d,
            b_sh=cs["w_in"], comm=plan)
    if plan is not None:
        z, outs = z
        got.update(zip(plan["names"], outs))
    sv["h"], sv["z"] = h, z
    c, sa, sb = rope
    qr, kr = _rowwise("rope_fwd", _rope_fwd, [R(z, Q_WIDTH, 0), R(z, KV_WIDTH, 4), R(c), R(sa), R(sb)],
                      [O(Q_WIDTH, BF16), O(KV_WIDTH, BF16)], tm=tm)
    qh, kh = _heads(qr, N_Q_HEADS), _heads(kr, N_KV_HEADS)
    vh = _heads(z[:, Q_WIDTH + KV_WIDTH:Q_WIDTH + 2 * KV_WIDTH], N_KV_HEADS)
    sinks = s["attn_sinks"].reshape(N_Q_HEADS, 1, 1)
    oh, lse = _attn_fwd(qh, kh, vh, sinks, nbatch, seq)
    o = _tokens(oh)
    ya = _mm("mm_attn_out", o, w["w_attn_out"], "nn", BF16, m=t, n=d, k=Q_WIDTH, tm=tb, tn=d, tk=Q_WIDTH)
    sv.update(qh=qh, kh=kh, vh=vh, oh=oh, lse=lse, o=o, ya=ya, sinks=sinks)
    ssm_args = [s[nm] for nm in SSM_NAMES]
    a_re, a_im, wb, wc = _ssm_prep(*ssm_args)
    dskip = _row(s["ssm_d"])
    st, ys, gel = _ssm_fwd(z, wb.astype(BF16), wc.astype(BF16), _ssm_tables(a_re, a_im, False), dskip, nbatch, seq)
    glu = _mm("mm_glu", gel, w["w_ssm_glu"], "nn", BF16, m=t, n=2 * d, k=SSM_WIDTH, tm=tb, tn=2 * d, tk=SSM_WIDTH,
              bias=_row(s["b_ssm_glu"]))
    sv.update(st=st, ys=ys, gel=gel, glu=glu, a=(a_re, a_im), wb=wb, wc=wc, dskip=dskip)
    cw = jnp.pad(s["conv_dw_w"], ((0, CONV_HALO - CONV_K), (0, 0)))
    cv, sc = _conv_fwd(z, cw, _row(s["conv_dw_b"]), _row(s["conv_norm_g"]), _row(s["conv_norm_b"]), nbatch, seq)
    yc = _mm("mm_conv_out", sc, w["w_conv_out"], "nn", BF16, m=t, n=d, k=CONV_WIDTH, tm=tb, tn=d, tk=CONV_WIDTH)
    sv.update(cw=cw, cv=cv, sc=sc, yc=yc)
    bg = _row(s["b_gate"])
    merge_ins = [R(z, 512, 3), R(z, 512, 5), R(z, 512, 7), V(bg, 512, 0), V(bg, 512, 2), V(bg, 512, 4),
                 R(ya, 512, 0), R(glu, 512, 0), R(glu, 512, 2), R(yc, 512, 0)]
    merged = _rowwise("merge_fwd", _merge_fwd, merge_ins, [O(512, BF16, total=d)], tm=tm, ncol=2)
    x1 = _mm("mm_mix", merged, w["w_mix_out"], "nn", F32, m=t, n=d, k=d, tm=tb, tn=d, tk=d, res=x)
    sv.update(merged=merged, x1=x1)
    hf = _rowwise("rms_ffn", _rms_fwd, [R(x1), V(_row(s["ffn_norm_g"]))], [O(d, BF16)], tm=tm)
    plan = None if next_shards is None else _gather_plan(next_shards, GATHER_B)
    f = _mm("mm_ffn_in", hf, w["w_ffn_in"], "nn", BF16, m=t, n=2 * FFN_HIDDEN, k=d, tm=tb, tn=cs["w_ffn_in"], tk=d,
            b_sh=cs["w_ffn_in"], comm=plan)
    if plan is not None:
        f, outs = f
        got.update(zip(plan["names"], outs))
    act = (lambda i, j, kk, fg, fu: _ffn_act(fg, fu), [(f, lambda i, j, kk: (i, 0)), (f, lambda i, j, kk: (i, 1))])
    plan = None if next_shards is None else _gather_plan(next_shards, GATHER_C)
    x2 = _mm("mm_ffn_out", act, w["w_ffn_out"], "nn", F32, m=t, n=d, k=FFN_HIDDEN, tm=256, tn=d, tk=FFN_HIDDEN, res=x1,
             comm=plan)
    if plan is not None:
        x2, outs = x2
        got.update(zip(plan["names"], outs))
    sv.update(hf=hf, f=f, x2=x2)
    e = _mm("mm_ple_in", p_l, w["w_ple_in"], "nn", BF16, m=t, n=d, k=p_l.shape[1], tm=tb, tn=d, tk=p_l.shape[1])
    hp = _rowwise("rms_ple", _rms_fwd, [R(x2), V(_row(s["ple_norm_g"]))], [O(d, BF16)], tm=tm)
    gp = _mm("mm_ple_gate", hp, w["w_ple_gate"], "nn", BF16, m=t, n=d, k=d, tm=tb, tn=d, tk=d)
    x3 = _rowwise("ple_fwd", _ple_fwd, [R(x2), R(gp), R(e)], [O(d, F32)], tm=tm)
    sv.update(e=e, hp=hp, gp=gp, p=p_l)
    return x3, sv, got


def _layer_bwd(dx3, sv, w, s, rope, nbatch, seq, above=None):
    t = dx3.shape[0]
    tm = 512
    d = D_MODEL
    gb, gs = {}, {}
    cs = {nm: w[nm].shape[2] for nm in SHARD_MAJOR}
    tb = 1024

    def wg(name, a, b, m, n, tm=1024, tk=1024, shard=None):
        return _mm(name, a, b, "tn", BF16, m=m, n=n, k=t, tm=tm, tn=n if shard is None else cs[shard], tk=tk,
                   o_sh=None if shard is None else cs[shard])

    de, dgp = _rowwise("ple_bwd", _ple_bwd, [R(dx3), R(sv["gp"]), R(sv["e"])], [O(d, BF16), O(d, BF16)], tm=tm)
    gb["w_ple_in"] = wg("wg_ple_in", sv["p"], de, sv["p"].shape[1], d, tk=2048)
    gb["w_ple_gate"] = wg("wg_ple_gate", sv["hp"], dgp, d, d, tk=2048)
    dhp = _mm("mmb_ple_gate", dgp, w["w_ple_gate"], "nt", BF16, m=t, n=d, k=d, tm=tb, tn=d, tk=d)
    dx2, gs["ple_norm_g"] = _rowwise("rms_ple_bwd", _rms_bwd, [R(dhp), R(sv["x2"]), R(dx3), V(_row(s["ple_norm_g"]))],
                                     [O(d, F32)], [A(d)], tm=tm)
    fw = FFN_HIDDEN // 2
    dact = _mm("mmb_ffn_out", dx2, w["w_ffn_out"], "nt", BF16, m=t, n=FFN_HIDDEN, k=d, tm=tb, tn=fw, tk=d)
    f = sv["f"]
    act_t = (lambda i, j, kk, fg, fu: _ffn_act(fg, fu), [(f, lambda i, j, kk: (kk, i)), (f, lambda i, j, kk: (kk, 2 + i))])
    gb["w_ffn_out"] = _mm("wg_ffn_out", act_t, dx2, "tn", BF16, m=FFN_HIDDEN, n=d, k=t, tm=fw, tn=d, tk=512)

    def df_tile(is_gate, da, fg, fu):
        dfg, dfu = _ffn_act_bwd(da, fg, fu)
        return jnp.where(is_gate, dfg, dfu)

    assert cs["w_ffn_in"] == fw
    df_cols = (lambda i, j, kk, *v: df_tile(j < 2, *v),
               [(dact, lambda i, j, kk: (kk, j % 2)), (f, lambda i, j, kk: (kk, j % 2)), (f, lambda i, j, kk: (kk, 2 + j % 2))])
    got = {}
    plan = None if above is None else _exchange_plan(above, EXCHANGE_B)
    gb["w_ffn_in"] = _mm("wg_ffn_in", sv["hf"], df_cols, "tn", BF16, m=d, n=2 * FFN_HIDDEN, k=t, tm=d, tn=fw, tk=512,
                         o_sh=fw, comm=plan)
    if plan is not None:
        gb["w_ffn_in"], outs = gb["w_ffn_in"]
        got.update(zip(plan["names"], outs))
    df_rows = (lambda i, j, kk, *v: df_tile(kk < 2, *v),
               [(dact, lambda i, j, kk: (i, kk % 2)), (f, lambda i, j, kk: (i, kk % 2)), (f, lambda i, j, kk: (i, 2 + kk % 2))])
    plan = None if above is None else _exchange_plan(above, EXCHANGE_A)
    dhf = _mm("mmb_ffn_in", df_rows, w["w_ffn_in"], "nt", BF16, m=t, n=d, k=2 * FFN_HIDDEN, tm=512, tn=d, tk=fw, b_sh=fw,
              comm=plan)
    if plan is not None:
        dhf, outs = dhf
        got.update(zip(plan["names"], outs))
    dx1, gs["ffn_norm_g"] = _rowwise("rms_ffn_bwd", _rms_bwd, [R(dhf), R(sv["x1"]), R(dx2), V(_row(s["ffn_norm_g"]))],
                                     [O(d, F32)], [A(d)], tm=tm)
    dm = _mm("mmb_mix", dx1, w["w_mix_out"], "nt", BF16, m=t, n=d, k=d, tm=tb, tn=d, tk=d)
    gb["w_mix_out"] = wg("wg_mix", sv["merged"], dx1, d, d)
    z, glu, bg = sv["z"], sv["glu"], _row(s["b_gate"])
    ins = [R(dm, 512, 0), R(z, 512, 3), R(z, 512, 5), R(z, 512, 7), V(bg, 512, 0), V(bg, 512, 2), V(bg, 512, 4),
           R(sv["ya"], 512, 0), R(glu, 512, 0), R(glu, 512, 2), R(sv["yc"], 512, 0)]
    ob = lambda: O(512, BF16, total=d)
    ab = lambda: A(512, total=d)
    dya, dga, dgb, dyc, d0, d1, d2, db0, db1, db2, dba, dbb = _rowwise(
        "merge_bwd", _merge_bwd, ins, [ob() for _ in range(7)], [ab() for _ in range(5)], tm=tm, ncol=2)
    gs["b_gate"] = jnp.concatenate([db0, db1, db2], axis=1)
    gs["b_ssm_glu"] = jnp.concatenate([dba, dbb], axis=1)
    dglu = jnp.concatenate([dga, dgb], axis=1)
    gb["w_attn_out"] = wg("wg_attn_out", sv["o"], dya, Q_WIDTH, d, tk=2048)
    do = _mm("mmb_attn_out", dya, w["w_attn_out"], "nt", BF16, m=t, n=Q_WIDTH, k=d, tm=tb, tn=Q_WIDTH, tk=d)
    dqh, dkc, dvc, dkp, dvp, dsink = _attn_bwd(sv["qh"], sv["kh"], sv["vh"], sv["oh"], _heads(do, N_Q_HEADS),
                                               sv["lse"], sv["sinks"], nbatch, seq)
    gs["attn_sinks"] = dsink.reshape(-1)
    c, sa, sb = rope
    dq = _rowwise("rope_bwd_q", _rope_bwd_q, [R(_tokens(dqh)), R(c), R(sa), R(sb)], [O(Q_WIDTH, BF16)], tm=tm)
    dk, dv = _kv_combine(_tokens(dkc), _tokens(dkp), _tokens(dvc), _tokens(dvp), c, sa, sb, seq)
    gb["w_ssm_glu"] = wg("wg_ssm_glu", sv["gel"], dglu, SSM_WIDTH, 2 * d, tk=2048)
    dgi = _mm("mmb_glu", dglu, w["w_ssm_glu"], "nt", BF16, m=t, n=SSM_WIDTH, k=2 * d, tm=tb, tn=SSM_WIDTH, tk=2 * d)
    a_re, a_im = sv["a"]
    du, dwb, dwc, dd, da = _ssm_bwd(dgi, sv["ys"], sv["st"], z, sv["wb"].T.astype(BF16), sv["wc"].T.astype(BF16),
                                    _ssm_tables(a_re, a_im, True), sv["dskip"], nbatch, seq)
    gs["ssm_d"] = dd.reshape(-1)
    da = jnp.sum(da, axis=0)
    _, prep_vjp = jax.vjp(_ssm_prep, *[s[nm] for nm in SSM_NAMES])
    for nm, g in zip(SSM_NAMES, prep_vjp((da[:SSM_LANES], da[SSM_LANES:], dwb, dwc))):
        gs[nm] = g
    gb["w_conv_out"] = wg("wg_conv_out", sv["sc"], dyc, CONV_WIDTH, d, tk=2048)
    dsc = _mm("mmb_conv_out", dyc, w["w_conv_out"], "nt", BF16, m=t, n=CONV_WIDTH, k=d, tm=tb, tn=CONV_WIDTH, tk=d)
    dca, dcg, dcw, dcb, dlg, dlb = _conv_bwd(dsc, sv["cv"], z, sv["cw"], _row(s["conv_norm_g"]),
                                             _row(s["conv_norm_b"]), nbatch, seq)
    gs["conv_dw_w"] = dcw[:CONV_K]
    gs["conv_dw_b"], gs["conv_norm_g"], gs["conv_norm_b"] = dcb.reshape(-1), dlg.reshape(-1), dlb.reshape(-1)
    dz = jnp.concatenate([dq, dk, dv, du, dca, dcg, d0, d1, d2], axis=1)
    gb["w_in"] = wg("wg_in", sv["h"], dz, d, dz.shape[1], tk=2048, shard="w_in")
    dh = _mm("mmb_in", dz, w["w_in"], "nt", BF16, m=t, n=d, k=dz.shape[1], tm=tb, tn=d, tk=cs["w_in"],
             b_sh=cs["w_in"])
    dx, gs["mix_norm_g"] = _rowwise("rms_mix_bwd", _rms_bwd, [R(dh), R(sv["x"]), R(dx1), V(_row(s["mix_norm_g"]))],
                                    [O(d, F32)], [A(d)], tm=tm)
    gs["mix_norm_g"], gs["ffn_norm_g"], gs["ple_norm_g"] = (gs[nm].reshape(-1) for nm in
                                                            ("mix_norm_g", "ffn_norm_g", "ple_norm_g"))
    gs["b_gate"], gs["b_ssm_glu"] = gs["b_gate"].reshape(-1), gs["b_ssm_glu"].reshape(-1)
    return dx, {nm: _shard_major(nm, g) for nm, g in gb.items()}, gs, got


def _rope_tables(positions):
    inv_freq = ROPE_THETA ** (-jnp.arange(0, ROPE_DIM, 2, dtype=F32) / ROPE_DIM)
    ang = positions.reshape(-1).astype(F32)[:, None] * inv_freq
    cos, sin = jnp.cos(ang), jnp.sin(ang)
    t = ang.shape[0]
    rest = HEAD_DIM - ROPE_DIM
    c = jnp.concatenate([cos, cos, jnp.ones((t, rest), F32)], axis=1)
    sa = jnp.concatenate([-sin, jnp.zeros((t, HEAD_DIM - ROPE_HALF), F32)], axis=1)
    sb = jnp.concatenate([jnp.zeros((t, ROPE_HALF), F32), sin, jnp.zeros((t, rest), F32)], axis=1)
    two = lambda v: jnp.concatenate([v, v], axis=1)
    return two(c), two(sa), two(sb)


def _natural(nm, w4):
    if nm in SHARD_MAJOR:
        return w4
    if BIG_AXIS[nm] == 1:
        return w4.reshape(-1, w4.shape[2])
    return w4.transpose(1, 0, 2).reshape(w4.shape[1], -1)


def _shard_major(nm, g):
    if nm in SHARD_MAJOR:
        return g
    if BIG_AXIS[nm] == 1:
        return g.reshape(N_CHIPS, -1, g.shape[1])
    return g.reshape(g.shape[0], N_CHIPS, -1).transpose(1, 0, 2)


def _untap(taps4, cols):
    flat = taps4.reshape(N_CHIPS, -1)[:, :CONV_K * cols]
    return flat.reshape(N_CHIPS, CONV_K, cols).transpose(1, 0, 2).reshape(CONV_K, N_CHIPS * cols)


def _local_step(x, p, positions, loss_target, small, wfull=None, shards=None):
    nbatch, seq, d = x.shape
    depth = p.shape[0]
    t = nbatch * seq
    rope = _rope_tables(positions)
    xs = x.reshape(t, d)
    saved, ws, ss = [], [], []
    got = None if shards is None else _gather_now(shards[0])
    for l in range(depth):
        w4 = {nm: wfull[nm][l] for nm in BIG} if shards is None else got
        w_l = {nm: _natural(nm, w4[nm]) for nm in BIG}
        s_l = {nm: small[nm][l] for nm in small if nm != "final_norm_g"}
        if shards is not None:
            s_l["conv_dw_w"] = _untap(got[TAPS], CONV_WIDTH // N_CHIPS)
        nxt = shards[l + 1] if shards is not None and l + 1 < depth else None
        xs, sv, got = _layer_fwd(xs, p[l].reshape(t, -1), w_l, s_l, rope, nbatch, seq, nxt)
        saved.append(sv)
        ws.append(w_l)
        ss.append(s_l)
    dx, loss_cols, dgf = _rowwise("loss_head", _loss_fn, [R(xs), R(loss_target.reshape(t, d)),
                                                          V(_row(small["final_norm_g"]))],
                                  [O(d, F32)], [A(d), A(d)], tm=512)
    gbs, gss, gots = [None] * depth, [None] * depth, [None] * depth
    for l in reversed(range(depth)):
        above = gbs[l + 1] if shards is not None and l + 1 < depth else None
        dx, gbs[l], gss[l], got = _layer_bwd(dx, saved[l], ws[l], ss[l], rope, nbatch, seq, above)
        if above is not None:
            gots[l + 1] = got
    if shards is not None:
        gots[0] = _comm_now("reduce_exchange", _exchange_plan(gbs[0], EXCHANGE_A + EXCHANGE_B))
    gsmall = {nm: jnp.stack([g[nm] for g in gss]) for nm in SMALL if nm != "final_norm_g"}
    gsmall["final_norm_g"] = dgf.reshape(-1)
    return loss_cols, dx.reshape(nbatch, seq, d), gsmall, gbs, gots


HBM = pl.BlockSpec(memory_space=pltpu.HBM)


def _place():
    x, y, c = lax.axis_index("x"), lax.axis_index("y"), lax.axis_index("c")
    chips = [(1 - x, y), (x, 1 - y), (1 - x, 1 - y)]
    return x, y, c, chips


def _remote(src, dst, send_sem, recv_sem, to):
    return pltpu.make_async_remote_copy(src_ref=src, dst_ref=dst, send_sem=send_sem, recv_sem=recv_sem,
                                        device_id=to, device_id_type=MESH)


TAPS = "taps"
GATHER_ALL = (("w_ffn_in", "w_ffn_out"),
              ("w_in", "w_ple_gate", "w_mix_out", "w_attn_out", "w_ssm_glu", "w_conv_out", "w_ple_in", TAPS))
GATHER_A = (("w_ffn_in",), ())
GATHER_B = ((), ("w_in", "w_ple_gate"))
GATHER_C = (("w_ffn_out",), ("w_mix_out", "w_attn_out", "w_ssm_glu", "w_conv_out", "w_ple_in", TAPS))
EXCHANGE_A = ("w_ffn_in", "w_ffn_out")
EXCHANGE_B = ("w_in", "w_mix_out", "w_ple_gate", "w_attn_out", "w_ssm_glu", "w_conv_out", "w_ple_in")


def _gather_plan(shards, sets):
    names = sets[0] + sets[1]
    n = len(names)
    idx = {nm: i for i, nm in enumerate(names)}

    def start(ins, outs, sems):
        send1, recv1, _, _, send0, recv0 = sems
        x, y, c, chips = _place()
        me = 2 * x + y
        for i in range(n):
            _remote(ins[i], outs[i].at[me], send0.at[i], recv0.at[i], (x, y, 1 - c)).start()
        for role in (0, 1):
            @pl.when(c == role)
            def _():
                for nm in sets[role]:
                    for k, (cx, cy) in enumerate(chips):
                        _remote(ins[idx[nm]], outs[idx[nm]].at[me], send1.at[idx[nm], k], recv1.at[idx[nm], k],
                                (cx, cy, c)).start()

    def finish(ins, outs, sems):
        send1, recv1, send2, recv2, send0, recv0 = sems
        x, y, c, chips = _place()
        me = 2 * x + y
        sib = (x, y, 1 - c)
        for role in (0, 1):
            @pl.when(c == role)
            def _():
                passed = []
                for nm in sets[role]:
                    i = idx[nm]
                    for k, (cx, cy) in enumerate(chips):
                        slot = outs[i].at[2 * cx + cy]
                        _remote(slot, slot, send1.at[i, k], recv1.at[i, k], (cx, cy, c)).wait_recv()
                        cp = _remote(slot, slot, send2.at[i, k], recv2.at[i, k], sib)
                        cp.start()
                        passed.append(cp)
                for nm in sets[1 - role]:
                    i = idx[nm]
                    for k, (cx, cy) in enumerate(chips):
                        slot = outs[i].at[2 * cx + cy]
                        _remote(slot, slot, send2.at[i, k], recv2.at[i, k], sib).wait_recv()
                for nm in sets[role]:
                    i = idx[nm]
                    for k, (cx, cy) in enumerate(chips):
                        _remote(ins[i], outs[i].at[me], send1.at[i, k], recv1.at[i, k], (cx, cy, c)).wait_send()
                for cp in passed:
                    cp.wait_send()
        for i in range(n):
            _remote(ins[i], outs[i].at[me], send0.at[i], recv0.at[i], sib).wait()

    ins = [shards[nm] for nm in names]
    return dict(names=names, ins=ins, start=start, finish=finish,
                out_shapes=[jax.ShapeDtypeStruct((N_CHIPS,) + v.shape, v.dtype) for v in ins],
                sems=[pltpu.SemaphoreType.DMA((n, 3)) for _ in range(4)] + [pltpu.SemaphoreType.DMA((n,))
                                                                            for _ in range(2)])


def _gather_now(shards):
    return _comm_now("gather_weights", _gather_plan(shards, GATHER_ALL))


def _exchange_plan(grads, names):
    n = len(names)

    def copies(ins, outs, sems):
        send, recv = sems
        x, y, c, chips = _place()
        return [_remote(ins[i].at[2 * cx + cy], outs[i].at[k], send.at[i, k], recv.at[i, k], (cx, cy, c))
                for i in range(n) for k, (cx, cy) in enumerate(chips)]

    def start(ins, outs, sems):
        for cp in copies(ins, outs, sems):
            cp.start()

    def finish(ins, outs, sems):
        for cp in copies(ins, outs, sems):
            cp.wait()

    ins = [grads[nm] for nm in names]
    return dict(names=names, ins=ins, start=start, finish=finish,
                out_shapes=[jax.ShapeDtypeStruct((3,) + v.shape[1:], v.dtype) for v in ins],
                sems=[pltpu.SemaphoreType.DMA((n, 3)), pltpu.SemaphoreType.DMA((n, 3))])


def _comm_now(name, plan):
    n = len(plan["ins"])

    def body(*refs):
        ins, outs, sems = refs[:n], refs[n:2 * n], refs[2 * n:]
        plan["start"](ins, outs, sems)
        plan["finish"](ins, outs, sems)

    outs = pl.pallas_call(body, out_shape=plan["out_shapes"], in_specs=[HBM] * n, out_specs=[HBM] * n,
                          scratch_shapes=plan["sems"], name=name)(*plan["ins"])
    return dict(zip(plan["names"], outs))


def _sum4(own, got):
    rr, cc = own.shape
    g2 = got.reshape(3 * rr, cc)
    fn = lambda a, b, c, d: ((a.astype(F32) + b.astype(F32)) + c.astype(F32)) + d.astype(F32)
    tm = rr if rr * cc <= 512 * 1024 else rr // 2
    nb = rr // tm
    return _rowwise("reduce_sum4", fn, [R(own), R(g2, rb=0), R(g2, rb=nb), R(g2, rb=2 * nb)], [O(cc, F32)], tm=tm)


def _pair_swap(sums):
    n = len(sums)

    def body(*refs):
        ins, outs = refs[:n], refs[n:2 * n]
        send, recv = refs[2 * n:]
        x, y, c, _ = _place()
        cps = [_remote(ins[i], outs[i], send.at[i], recv.at[i], (x, y, 1 - c)) for i in range(n)]
        for cp in cps:
            cp.start()
        for cp in cps:
            cp.wait()

    out_shape = [jax.ShapeDtypeStruct(v.shape, v.dtype) for v in sums]
    sems = [pltpu.SemaphoreType.DMA((n,)) for _ in range(2)]
    return pl.pallas_call(body, out_shape=out_shape, in_specs=[HBM] * n, out_specs=[HBM] * n, scratch_shapes=sems,
                          name="reduce_pair_swap")(*sums)


def _allreduce_small(vec):
    rows = vec.shape[0]

    def body(v_ref, o_ref, all_ref, send, recv):
        x, y, c, _ = _place()
        me = 4 * x + 2 * y + c
        all_ref[me] = v_ref[...]
        cps = []
        for dlt in range(1, N_DEV):
            fx, fy, fc = (dlt >> 2) & 1, (dlt >> 1) & 1, dlt & 1
            to = (1 - x if fx else x, 1 - y if fy else y, 1 - c if fc else c)
            cps.append(_remote(v_ref, all_ref.at[me], send.at[dlt - 1], recv.at[dlt - 1], to))
        for cp in cps:
            cp.start()
        for cp in cps:
            cp.wait()
        tot = all_ref[0]
        for dev in range(1, N_DEV):
            tot = tot + all_ref[dev]
        o_ref[...] = tot

    vm = pl.BlockSpec(memory_space=pltpu.VMEM)
    return pl.pallas_call(
        body, out_shape=jax.ShapeDtypeStruct(vec.shape, F32), in_specs=[vm], out_specs=vm,
        scratch_shapes=[pltpu.VMEM((N_DEV, rows, 128), F32), pltpu.SemaphoreType.DMA((N_DEV - 1,)),
                        pltpu.SemaphoreType.DMA((N_DEV - 1,))],
        name="allreduce_small", compiler_params=pltpu.CompilerParams(vmem_limit_bytes=VMEM_LIMIT))(vec)


def _adamw(name, w, g, m, v):
    rows, cc = w.shape
    tm = math.gcd(rows, 256)
    return _rowwise(name, _adamw_fn, [R(w), R(g), R(m), R(v)], [O(cc, F32), O(cc, F32), O(cc, F32)], tm=tm)


def _adamw_pair(name, w, g_a, g_b, m, v):
    rows, cc = w.shape
    fn = lambda w_, a, b, m_, v_: (a + b,) + _adamw_fn(w_, a + b, m_, v_)
    return _rowwise(name, fn, [R(w), R(g_a), R(g_b), R(m), R(v)], [O(cc, F32) for _ in range(4)],
                    tm=math.gcd(rows, 256))


def _pack(parts):
    flat = jnp.concatenate([v.reshape(-1).astype(F32) for v in parts])
    pad = (-flat.shape[0]) % (SUBLANES * 128)
    return jnp.pad(flat, (0, pad)).reshape(-1, 128)


def _unpack(packed, shapes):
    flat, out, pos = packed.reshape(-1), [], 0
    for shp in shapes:
        size = math.prod(shp)
        out.append(flat[pos:pos + size].reshape(shp))
        pos += size
    return out


def kernel(x, p, positions, mix_norm_g, w_in, b_gate, attn_sinks, w_attn_out, ssm_lambda_re, ssm_lambda_im, ssm_log_dt, ssm_b_re, ssm_b_im, ssm_c_re, ssm_c_im, ssm_d, w_ssm_glu, b_ssm_glu, conv_dw_w, conv_dw_b, conv_norm_g, conv_norm_b, w_conv_out, w_mix_out, ffn_norm_g, w_ffn_in, w_ffn_out, w_ple_in, ple_norm_g, w_ple_gate, final_norm_g, loss_target, m_mix_norm_g, m_w_in, m_b_gate, m_attn_sinks, m_w_attn_out, m_ssm_lambda_re, m_ssm_lambda_im, m_ssm_log_dt, m_ssm_b_re, m_ssm_b_im, m_ssm_c_re, m_ssm_c_im, m_ssm_d, m_w_ssm_glu, m_b_ssm_glu, m_conv_dw_w, m_conv_dw_b, m_conv_norm_g, m_conv_norm_b, m_w_conv_out, m_w_mix_out, m_ffn_norm_g, m_w_ffn_in, m_w_ffn_out, m_w_ple_in, m_ple_norm_g, m_w_ple_gate, m_final_norm_g, v_mix_norm_g, v_w_in, v_b_gate, v_attn_sinks, v_w_attn_out, v_ssm_lambda_re, v_ssm_lambda_im, v_ssm_log_dt, v_ssm_b_re, v_ssm_b_im, v_ssm_c_re, v_ssm_c_im, v_ssm_d, v_w_ssm_glu, v_b_ssm_glu, v_conv_dw_w, v_conv_dw_b, v_conv_norm_g, v_conv_norm_b, v_w_conv_out, v_w_mix_out, v_ffn_norm_g, v_w_ffn_in, v_w_ffn_out, v_w_ple_in, v_ple_norm_g, v_w_ple_gate, v_final_norm_g):
    given = dict(locals())
    wts = {nm: given[nm] for nm in WEIGHTS}
    mom = {nm: given["m_" + nm] for nm in WEIGHTS}
    var = {nm: given["v_" + nm] for nm in WEIGHTS}
    depth = p.shape[0]
    chip = 2 * lax.axis_index("x") + lax.axis_index("y")

    cw_cols = conv_dw_w.shape[2]
    taps = jnp.pad(conv_dw_w.reshape(depth, -1), ((0, 0), (0, (-CONV_K * cw_cols) % (SUBLANES * 128))))
    shards = [{**{nm: wts[nm][l].astype(BF16) for nm in BIG}, TAPS: taps[l].reshape(-1, 128)} for l in range(depth)]
    small = {nm: wts[nm] for nm in SMALL if nm != "conv_dw_w"}

    loss_cols, grad_x, gsmall, gbs, gots = _local_step(x, p, positions, loss_target, small, shards=shards)

    parts = [loss_cols] + [gsmall[nm] for nm in SMALL]
    total = _allreduce_small(_pack(parts))
    summed = _unpack(total, [v.shape for v in parts])
    loss = jnp.sum(summed[0])
    gsum = dict(zip(SMALL, summed[1:]))
    gsum["conv_dw_w"] = lax.dynamic_slice_in_dim(gsum["conv_dw_w"], chip * cw_cols, cw_cols, axis=2)
    shapes = [wts[nm].shape for nm in SMALL]
    deltas, new_m, new_v = _adamw("adamw_small", _pack([wts[nm] for nm in SMALL]), _pack([gsum[nm] for nm in SMALL]),
                                  _pack([mom[nm] for nm in SMALL]), _pack([var[nm] for nm in SMALL]))
    grads = dict(gsum)
    delta = dict(zip(SMALL, _unpack(deltas, shapes)))
    newm = dict(zip(SMALL, _unpack(new_m, shapes)))
    newv = dict(zip(SMALL, _unpack(new_v, shapes)))

    sums = [jnp.stack([_sum4(lax.dynamic_index_in_dim(gbs[l][nm], chip, 0, keepdims=False), gots[l][nm])
                       for l in range(depth)]) for nm in BIG]
    for nm, mine, theirs in zip(BIG, sums, _pair_swap(sums)):
        shp = wts[nm].shape
        two = lambda v: v.reshape(-1, shp[-1])
        outs = _adamw_pair("adamw_" + nm, two(wts[nm]), two(mine), two(theirs), two(mom[nm]), two(var[nm]))
        grads[nm], delta[nm], newm[nm], newv[nm] = (v.reshape(shp) for v in outs)

    return (loss, grad_x, *[grads[nm] for nm in WEIGHTS], *[delta[nm] for nm in WEIGHTS],
            *[newm[nm] for nm in WEIGHTS], *[newv[nm] for nm in WEIGHTS])
```

```python
import functools
import math

import jax
import jax.numpy as jnp
from jax import lax
from jax.experimental import pallas as pl
from jax.experimental.pallas import tpu as pltpu

F32 = jnp.float32
BF16 = jnp.bfloat16

D_MODEL = 1024
HEAD_DIM = 64
N_Q_HEADS = 8
N_KV_HEADS = 2
GQA_GROUP = N_Q_HEADS // N_KV_HEADS
ATT_BLOCK = 128
ROPE_THETA = 500000.0
ROPE_DIM = HEAD_DIM // 4
ROPE_HALF = ROPE_DIM // 2
Q_WIDTH = N_Q_HEADS * HEAD_DIM
KV_WIDTH = N_KV_HEADS * HEAD_DIM
SSM_WIDTH = 256
SSM_GROUP = 16
SSM_GROUPS = 16
SSM_STATE = 64
SSM_LANES = SSM_GROUPS * SSM_STATE
CONV_WIDTH = 256
CONV_K = 31
CONV_HALO = 32
FFN_HIDDEN = 2816
EPS = 1e-6
NEG_INF = -1e30
SCALE = HEAD_DIM ** -0.5

ADAM_LR = 0.001
ADAM_B1 = 0.9
ADAM_B2 = 0.999
ADAM_EPS = 1e-08
ADAM_WD = 0.01
ADAM_STEP = 10

N_CHIPS = 4
N_DEV = 8
SUBLANES = 8
VMEM_LIMIT = 56 * 1024 * 1024

MESH = pl.DeviceIdType.MESH


def _params(sem=None):
    return pltpu.CompilerParams(dimension_semantics=sem, vmem_limit_bytes=VMEM_LIMIT)


def R(arr, width=None, cb=0, rb=0):
    return ("r", arr, arr.shape[1] if width is None else width, (cb, rb))


def V(arr, width=None, cb=0):
    return ("v", arr, arr.shape[1] if width is None else width, cb)


def _cbf(cb):
    return cb if callable(cb) else (lambda j, c=cb: c + j)


def _rowwise(name, fn, ins, outs, accs=(), *, tm, ncol=1):
    t = [a for k, a, _, _ in ins if k == "r"][0].shape[0]
    tm = min(tm, t)
    assert t % tm == 0, (name, t, tm)
    n_i, n_o, n_a = len(ins), len(outs), len(accs)

    def body(*refs):
        vals = fn(*[r[...] for r in refs[:n_i]])
        if not isinstance(vals, (tuple, list)):
            vals = (vals,)
        for ref, val in zip(refs[n_i:n_i + n_o], vals[:n_o]):
            ref[...] = val.astype(ref.dtype)
        if n_a:
            acc_refs = refs[n_i + n_o:]

            @pl.when(pl.program_id(1) == 0)
            def _():
                for ref in acc_refs:
                    ref[...] = jnp.zeros_like(ref)

            for ref, val in zip(acc_refs, vals[n_o:]):
                ref[...] += val

    in_specs = []
    for kind, arr, width, cb in ins:
        if kind == "r":
            f = _cbf(cb[0])
            in_specs.append(pl.BlockSpec((tm, width), functools.partial(lambda j, i, f, rb: (i + rb, f(j)), f=f, rb=cb[1])))
        else:
            f = _cbf(cb)
            in_specs.append(pl.BlockSpec((arr.shape[0], width), functools.partial(lambda j, i, f: (0, f(j)), f=f)))
    out_specs, out_shape = [], []
    for total, width, cb, dt in outs:
        f = _cbf(cb)
        out_specs.append(pl.BlockSpec((tm, width), functools.partial(lambda j, i, f: (i, f(j)), f=f)))
        out_shape.append(jax.ShapeDtypeStruct((t, total), dt))
    for total, width, cb in accs:
        f = _cbf(cb)
        out_specs.append(pl.BlockSpec((1, width), functools.partial(lambda j, i, f: (0, f(j)), f=f)))
        out_shape.append(jax.ShapeDtypeStruct((1, total), F32))
    sem = ("arbitrary", "arbitrary") if n_a else ("parallel", "parallel")
    res = pl.pallas_call(body, out_shape=out_shape, grid=(ncol, t // tm), in_specs=in_specs, out_specs=out_specs,
                         name=name, compiler_params=_params(sem))(*[a for _, a, _, _ in ins])
    return res[0] if len(res) == 1 else res


def O(width, dtype, total=None, cb=0):
    return (width if total is None else total, width, cb, dtype)


def A(width, total=None, cb=0):
    return (width if total is None else total, width, cb)


_DIMS = {"nn": (((1,), (0,)), ((), ())), "nt": (((1,), (1,)), ((), ())), "tn": (((0,), (0,)), ((), ()))}


def _mm(name, a, b, mode, out_dtype, *, m, n, k, tm, tn, tk, a_off=0, b_off=0, res=None, bias=None, b_sh=None, o_sh=None,
        comm=None):
    tm, tn, tk = min(tm, m), min(tn, n), min(tk, k)
    assert m % tm == 0 and n % tn == 0 and k % tk == 0, (name, m, n, k, tm, tn, tk)
    nk = k // tk
    has_res, has_bias = res is not None, bias is not None
    a_fn, a_ops = a if isinstance(a, tuple) else (None, [(a, None)])
    b_fn, b_ops = b if isinstance(b, tuple) else (None, [(b, None)])
    na, nb_ = len(a_ops), len(b_ops)
    a_bytes = sum(m * k * arr.dtype.itemsize for arr, _ in a_ops)
    b_bytes = sum(n * k * arr.dtype.itemsize for arr, _ in b_ops)
    swap = nk == 1 and b_bytes + (n // tn) * a_bytes < a_bytes + (m // tm) * b_bytes
    grid = (n // tn, m // tm, nk) if swap else (m // tm, n // tn, nk)
    ncomm = 0 if comm is None else len(comm["ins"])

    def body(*refs):
        g0, g1, kk = pl.program_id(0), pl.program_id(1), pl.program_id(2)
        gi, gj = (g1, g0) if swap else (g0, g1)
        a_tiles = [r[...] for r in refs[:na]]
        b_tiles = [r[...] for r in refs[na:na + nb_]]
        a_val = a_tiles[0] if a_fn is None else a_fn(gi, gj, kk, *a_tiles)
        b_val = b_tiles[0] if b_fn is None else b_fn(gi, gj, kk, *b_tiles)
        pos = na + nb_
        res_ref = bias_ref = None
        if has_res:
            res_ref = refs[pos]
            pos += 1
        if has_bias:
            bias_ref = refs[pos]
            pos += 1
        comm_ins = refs[pos:pos + ncomm]
        o_ref = refs[pos + ncomm]
        comm_outs = refs[pos + ncomm + 1:pos + 2 * ncomm + 1]
        scratch = refs[pos + 2 * ncomm + 1:]
        if comm is not None:
            sems = scratch[1:] if nk > 1 else scratch

            @pl.when((g0 == 0) & (g1 == 0) & (kk == 0))
            def _():
                comm["start"](comm_ins, comm_outs, sems)

        def finish(r):
            if has_bias:
                r = r + bias_ref[...]
            if has_res:
                r = r + res_ref[...].astype(F32)
            o_ref[...] = r.astype(o_ref.dtype)

        part = lax.dot_general(a_val.astype(BF16), b_val.astype(BF16), _DIMS[mode], preferred_element_type=F32)
        if nk == 1:
            finish(part)
        else:
            acc_ref = scratch[0]

            @pl.when(kk == 0)
            def _():
                acc_ref[...] = part

            @pl.when(kk > 0)
            def _():
                acc_ref[...] += part

            @pl.when(kk == nk - 1)
            def _():
                finish(acc_ref[...])

        if comm is not None:
            @pl.when((g0 == grid[0] - 1) & (g1 == grid[1] - 1) & (kk == nk - 1))
            def _():
                comm["finish"](comm_ins, comm_outs, sems)

    def at(f):
        return (lambda g0, g1, kk: f(g1, g0, kk)) if swap else f

    if mode == "nn":
        a_spec = pl.BlockSpec((tm, tk), at(lambda i, j, kk: (i, kk + a_off)))
        b_spec = pl.BlockSpec((tk, tn), at(lambda i, j, kk: (kk, j + b_off)))
        if b_sh is not None:
            assert b_sh % tn == 0, (name, b_sh, tn)
            per = b_sh // tn
            b_spec = pl.BlockSpec((None, tk, tn), at(lambda i, j, kk: (j // per, kk, j % per)))
    elif mode == "nt":
        a_spec = pl.BlockSpec((tm, tk), at(lambda i, j, kk: (i, kk + a_off)))
        b_spec = pl.BlockSpec((tn, tk), at(lambda i, j, kk: (j, kk + b_off)))
        if b_sh is not None:
            assert b_sh % tk == 0, (name, b_sh, tk)
            per = b_sh // tk
            b_spec = pl.BlockSpec((None, tn, tk), at(lambda i, j, kk: (kk // per, j, kk % per)))
    else:
        a_spec = pl.BlockSpec((tk, tm), at(lambda i, j, kk: (kk, i + a_off)))
        b_spec = pl.BlockSpec((tk, tn), at(lambda i, j, kk: (kk, j + b_off)))
    a_specs = [a_spec] if a_fn is None else [pl.BlockSpec(a_spec.block_shape, at(f)) for _, f in a_ops]
    b_specs = [b_spec] if b_fn is None else [pl.BlockSpec(b_spec.block_shape, at(f)) for _, f in b_ops]
    in_specs, args = a_specs + b_specs, [arr for arr, _ in a_ops] + [arr for arr, _ in b_ops]
    if has_res:
        in_specs.append(pl.BlockSpec((tm, tn), at(lambda i, j, kk: (i, j))))
        args.append(res)
    if has_bias:
        in_specs.append(pl.BlockSpec((1, tn), at(lambda i, j, kk: (0, j))))
        args.append(bias)
    out_spec, out_shape = pl.BlockSpec((tm, tn), at(lambda i, j, kk: (i, j))), (m, n)
    if o_sh is not None:
        assert o_sh % tn == 0, (name, o_sh, tn)
        per_o = o_sh // tn
        out_spec = pl.BlockSpec((None, tm, tn), at(lambda i, j, kk: (j // per_o, i, j % per_o)))
        out_shape = (n // o_sh, m, o_sh)
    scratch = [pltpu.VMEM((tm, tn), F32)] if nk > 1 else []
    if comm is None:
        return pl.pallas_call(
            body, out_shape=jax.ShapeDtypeStruct(out_shape, out_dtype), grid=grid, in_specs=in_specs,
            out_specs=out_spec, scratch_shapes=scratch, name=name,
            compiler_params=_params(("parallel", "parallel", "arbitrary")))(*args)
    outs = pl.pallas_call(
        body, out_shape=[jax.ShapeDtypeStruct(out_shape, out_dtype)] + comm["out_shapes"], grid=grid,
        in_specs=in_specs + [HBM] * ncomm, out_specs=[out_spec] + [HBM] * ncomm,
        scratch_shapes=scratch + comm["sems"], name=name,
        compiler_params=_params(("arbitrary", "arbitrary", "arbitrary")))(*args, *comm["ins"])
    return outs[0], outs[1:]


def _sig(v):
    return jax.nn.sigmoid(v)


def _rms_fwd(x, g):
    r = lax.rsqrt(jnp.mean(x * x, axis=-1, keepdims=True) + EPS)
    return x * r * g


def _rms_bwd(dh, x, dres, g):
    dh = dh.astype(F32)
    r = lax.rsqrt(jnp.mean(x * x, axis=-1, keepdims=True) + EPS)
    xh = x * r
    dxh = dh * g
    dx = r * (dxh - xh * jnp.mean(dxh * xh, axis=-1, keepdims=True))
    return dres + dx, jnp.sum(dh * xh, axis=0, keepdims=True)


def _rope_apply(t, c, sa, sb):
    w = t.shape[1]
    return t * c + pltpu.roll(t, w - ROPE_HALF, 1) * sa + pltpu.roll(t, ROPE_HALF, 1) * sb


def _rope_transpose(g, c, sa, sb):
    w = g.shape[1]
    return g * c + pltpu.roll(g * sa, ROPE_HALF, 1) + pltpu.roll(g * sb, w - ROPE_HALF, 1)


def _tile_lanes(tab, reps):
    return jnp.concatenate([tab] * reps, axis=1) if reps > 1 else tab


def _rope_fwd(q, k, c, sa, sb):
    rq = Q_WIDTH // c.shape[1]
    qr = _rope_apply(q.astype(F32), _tile_lanes(c, rq), _tile_lanes(sa, rq), _tile_lanes(sb, rq))
    kr = _rope_apply(k.astype(F32), c, sa, sb)
    return qr, kr


def _rope_bwd_q(g, c, sa, sb):
    rq = Q_WIDTH // c.shape[1]
    return _rope_transpose(g.astype(F32), _tile_lanes(c, rq), _tile_lanes(sa, rq), _tile_lanes(sb, rq))


def _gelu(v):
    return jax.nn.gelu(v, approximate=True)


def _gelu_grad(v):
    c0 = math.sqrt(2.0 / math.pi)
    inner = c0 * (v + 0.044715 * v * v * v)
    th = jnp.tanh(inner)
    return 0.5 * (1.0 + th) + 0.5 * v * (1.0 - th * th) * c0 * (1.0 + 3 * 0.044715 * v * v)


def _merge_fwd(g0, g1, g2, b0, b1, b2, ya, ga, gb, yc):
    s0 = _sig(g0.astype(F32) + b0)
    s1 = _sig(g1.astype(F32) + b1)
    s2 = _sig(g2.astype(F32) + b2)
    ys = ga.astype(F32) * _sig(gb.astype(F32))
    return s0 * ya.astype(F32) + s1 * ys + s2 * yc.astype(F32)


def _merge_bwd(dm, g0, g1, g2, b0, b1, b2, ya, ga, gb, yc):
    dm = dm.astype(F32)
    s0 = _sig(g0.astype(F32) + b0)
    s1 = _sig(g1.astype(F32) + b1)
    s2 = _sig(g2.astype(F32) + b2)
    ga = ga.astype(F32)
    sb = _sig(gb.astype(F32))
    ys = ga * sb
    dya = dm * s0
    dys = dm * s1
    dyc = dm * s2
    dga = dys * sb
    dgb = dys * ga * sb * (1.0 - sb)
    d0 = dm * ya.astype(F32) * s0 * (1.0 - s0)
    d1 = dm * ys * s1 * (1.0 - s1)
    d2 = dm * yc.astype(F32) * s2 * (1.0 - s2)
    cs = lambda v: jnp.sum(v, axis=0, keepdims=True)
    return dya, dga, dgb, dyc, d0, d1, d2, cs(d0), cs(d1), cs(d2), cs(dga), cs(dgb)


def _ffn_act(fg, fu):
    fg = fg.astype(F32)
    return fg * _sig(fg) * fu.astype(F32)


def _ffn_act_bwd(da, fg, fu):
    da, fg, fu = da.astype(F32), fg.astype(F32), fu.astype(F32)
    s = _sig(fg)
    return da * fu * (s * (1.0 + fg * (1.0 - s))), da * fg * s


def _ple_fwd(x, gp, e):
    return x + _sig(gp.astype(F32)) * e.astype(F32)


def _ple_bwd(dx, gp, e):
    s = _sig(gp.astype(F32))
    e = e.astype(F32)
    return dx * s, dx * e * s * (1.0 - s)


def _loss_fn(x, tgt, g):
    d = x.shape[1]
    r = lax.rsqrt(jnp.mean(x * x, axis=-1, keepdims=True) + EPS)
    xh = x * r
    err = xh * g - tgt
    dy = err * (1.0 / d)
    dxh = dy * g
    dx = r * (dxh - xh * jnp.mean(dxh * xh, axis=-1, keepdims=True))
    return dx, jnp.sum(err * err, axis=0, keepdims=True) * (0.5 / d), jnp.sum(dy * xh, axis=0, keepdims=True)


def _adamw_fn(w, g, m, v):
    m = ADAM_B1 * m + (1.0 - ADAM_B1) * g
    v = ADAM_B2 * v + (1.0 - ADAM_B2) * (g * g)
    m_hat = m / (1.0 - ADAM_B1 ** ADAM_STEP)
    v_hat = v / (1.0 - ADAM_B2 ** ADAM_STEP)
    delta = -ADAM_LR * (m_hat / (jnp.sqrt(v_hat) + ADAM_EPS) + ADAM_WD * w)
    return delta, m, v


def _band_mask(n):
    qi = lax.broadcasted_iota(jnp.int32, (ATT_BLOCK, 2 * ATT_BLOCK), 0)
    kj = lax.broadcasted_iota(jnp.int32, (ATT_BLOCK, 2 * ATT_BLOCK), 1)
    dist = qi + ATT_BLOCK - kj
    return (dist >= 0) & (dist < ATT_BLOCK) & ((n > 0) | (kj >= ATT_BLOCK))


def _att_specs(nb):
    cur = lambda b, n: (0, b * nb + n, 0)
    prev = lambda b, n: (0, b * nb + jnp.maximum(n - 1, 0), 0)
    qs = pl.BlockSpec((N_Q_HEADS, ATT_BLOCK, HEAD_DIM), cur)
    kc = pl.BlockSpec((N_KV_HEADS, ATT_BLOCK, HEAD_DIM), cur)
    kp = pl.BlockSpec((N_KV_HEADS, ATT_BLOCK, HEAD_DIM), prev)
    stat = pl.BlockSpec((N_Q_HEADS, ATT_BLOCK, 1), cur)
    sink = pl.BlockSpec((N_Q_HEADS, 1, 1), lambda b, n: (0, 0, 0))
    return qs, kc, kp, stat, sink


def _attn_fwd(qh, kh, vh, sinks, nbatch, seq):
    t = qh.shape[1]
    nb = seq // ATT_BLOCK
    qs, kc, kp, stat, sink = _att_specs(nb)

    def body(q_ref, kp_ref, kc_ref, vp_ref, vc_ref, sink_ref, o_ref, lse_ref):
        mask = _band_mask(pl.program_id(1))
        rows = GQA_GROUP * ATT_BLOCK
        for kv in range(N_KV_HEADS):
            hs = slice(kv * GQA_GROUP, (kv + 1) * GQA_GROUP)
            kk = jnp.concatenate([kp_ref[kv], kc_ref[kv]], axis=0)
            vv = jnp.concatenate([vp_ref[kv], vc_ref[kv]], axis=0)
            q4 = (q_ref[hs] * SCALE).reshape(rows, HEAD_DIM)
            s = lax.dot_general(q4, kk, _DIMS["nt"], preferred_element_type=F32)
            s = jnp.where(mask, s.reshape(GQA_GROUP, ATT_BLOCK, 2 * ATT_BLOCK), NEG_INF)
            sk = sink_ref[hs]
            mx = jnp.maximum(jnp.max(s, axis=-1, keepdims=True), sk)
            p = jnp.exp(s - mx)
            den = jnp.sum(p, axis=-1, keepdims=True) + jnp.exp(sk - mx)
            o = lax.dot_general(p.reshape(rows, 2 * ATT_BLOCK).astype(BF16), vv, _DIMS["nn"],
                                preferred_element_type=F32).reshape(GQA_GROUP, ATT_BLOCK, HEAD_DIM)
            o_ref[hs] = (o * (1.0 / den)).astype(o_ref.dtype)
            lse_ref[hs] = mx + jnp.log(den)

    return pl.pallas_call(
        body, grid=(nbatch, nb), in_specs=[qs, kp, kc, kp, kc, sink], out_specs=[qs, stat],
        out_shape=[jax.ShapeDtypeStruct((N_Q_HEADS, t, HEAD_DIM), BF16), jax.ShapeDtypeStruct((N_Q_HEADS, t, 1), F32)],
        name="attn_fwd", compiler_params=_params(("parallel", "parallel")))(qh, kh, kh, vh, vh, sinks)


def _attn_bwd(qh, kh, vh, oh, doh, lse, sinks, nbatch, seq):
    t = qh.shape[1]
    nb = seq // ATT_BLOCK
    qs, kc, kp, stat, sink = _att_specs(nb)

    def body(q_ref, kp_ref, kc_ref, vp_ref, vc_ref, o_ref, do_ref, lse_ref, sink_ref,
             dq_ref, dkc_ref, dvc_ref, dkp_ref, dvp_ref, dsink_ref):
        first = (pl.program_id(0) == 0) & (pl.program_id(1) == 0)

        @pl.when(first)
        def _():
            dsink_ref[...] = jnp.zeros_like(dsink_ref)

        mask = _band_mask(pl.program_id(1))
        rows = GQA_GROUP * ATT_BLOCK
        band = (GQA_GROUP, ATT_BLOCK, 2 * ATT_BLOCK)
        for kv in range(N_KV_HEADS):
            hs = slice(kv * GQA_GROUP, (kv + 1) * GQA_GROUP)
            kk = jnp.concatenate([kp_ref[kv], kc_ref[kv]], axis=0)
            vv = jnp.concatenate([vp_ref[kv], vc_ref[kv]], axis=0)
            q4 = q_ref[hs].reshape(rows, HEAD_DIM)
            do4 = do_ref[hs].reshape(rows, HEAD_DIM)
            lse4 = lse_ref[hs]
            s = lax.dot_general(q4 * SCALE, kk, _DIMS["nt"], preferred_element_type=F32).reshape(band)
            p = jnp.where(mask, jnp.exp(s - lse4), 0.0)
            dd = jnp.sum(do_ref[hs].astype(F32) * o_ref[hs].astype(F32), axis=-1, keepdims=True)
            dp = lax.dot_general(do4, vv, _DIMS["nt"], preferred_element_type=F32).reshape(band)
            ds = (p * (dp - dd) * SCALE).astype(BF16).reshape(rows, 2 * ATT_BLOCK)
            dq = lax.dot_general(ds, kk, _DIMS["nn"], preferred_element_type=F32)
            dq_ref[hs] = dq.reshape(GQA_GROUP, ATT_BLOCK, HEAD_DIM).astype(dq_ref.dtype)
            dk = lax.dot_general(ds, q4, _DIMS["tn"], preferred_element_type=F32)
            dv = lax.dot_general(p.astype(BF16).reshape(rows, 2 * ATT_BLOCK), do4, _DIMS["tn"],
                                 preferred_element_type=F32)
            dsink_ref[hs] += -jnp.sum(jnp.exp(sink_ref[hs] - lse4) * dd, axis=1, keepdims=True)
            dkp_ref[kv] = dk[:ATT_BLOCK]
            dkc_ref[kv] = dk[ATT_BLOCK:]
            dvp_ref[kv] = dv[:ATT_BLOCK]
            dvc_ref[kv] = dv[ATT_BLOCK:]

    kvs = jax.ShapeDtypeStruct((N_KV_HEADS, t, HEAD_DIM), F32)
    return pl.pallas_call(
        body, grid=(nbatch, nb), in_specs=[qs, kp, kc, kp, kc, qs, qs, stat, sink],
        out_specs=[qs, kc, kc, kc, kc, sink],
        out_shape=[jax.ShapeDtypeStruct((N_Q_HEADS, t, HEAD_DIM), F32), kvs, kvs, kvs, kvs,
                   jax.ShapeDtypeStruct((N_Q_HEADS, 1, 1), F32)],
        name="attn_bwd", compiler_params=_params(("arbitrary", "arbitrary")))(qh, kh, kh, vh, vh, oh, doh, lse, sinks)


def _kv_combine(dkc, dkp, dvc, dvp, c, sa, sb, seq):
    t = dkc.shape[0]
    nb = seq // ATT_BLOCK
    nblk = t // ATT_BLOCK

    def body(kc_ref, kp_ref, vc_ref, vp_ref, c_ref, sa_ref, sb_ref, dk_ref, dv_ref):
        has_next = (pl.program_id(0) % nb) != nb - 1
        dk = kc_ref[...] + jnp.where(has_next, kp_ref[...], 0.0)
        dv = vc_ref[...] + jnp.where(has_next, vp_ref[...], 0.0)
        dk_ref[...] = _rope_transpose(dk, c_ref[...], sa_ref[...], sb_ref[...]).astype(dk_ref.dtype)
        dv_ref[...] = dv.astype(dv_ref.dtype)

    cur = pl.BlockSpec((ATT_BLOCK, KV_WIDTH), lambda i: (i, 0))
    nxt = pl.BlockSpec((ATT_BLOCK, KV_WIDTH), lambda i: (jnp.minimum(i + 1, nblk - 1), 0))
    o = jax.ShapeDtypeStruct((t, KV_WIDTH), BF16)
    return pl.pallas_call(body, grid=(nblk,), in_specs=[cur, nxt, cur, nxt, cur, cur, cur], out_specs=[cur, cur],
                          out_shape=[o, o], name="kv_combine", compiler_params=_params(("parallel",)))(
        dkc, dkp, dvc, dvp, c, sa, sb)


def _scan_block(ref, tab_ref, carry, ngroups, reverse):
    shifts = (7, 6, 4) if reverse else (1, 2, 4)
    n = SSM_LANES

    def step(i, car):
        g = (ngroups - 1 - i) if reverse else i
        r0 = pl.multiple_of(g * SUBLANES, SUBLANES)
        xr = ref[pl.ds(r0, SUBLANES), :n]
        xi = ref[pl.ds(r0, SUBLANES), n:]
        for s, sh in enumerate(shifts):
            pr, pi = tab_ref[2 * s], tab_ref[2 * s + 1]
            yr, yi = pltpu.roll(xr, sh, 0), pltpu.roll(xi, sh, 0)
            xr, xi = xr + pr * yr - pi * yi, xi + pr * yi + pi * yr
        cr, ci = car
        qr, qi = tab_ref[6], tab_ref[7]
        xr, xi = xr + qr * cr - qi * ci, xi + qr * ci + qi * cr
        ref[pl.ds(r0, SUBLANES), :n] = xr
        ref[pl.ds(r0, SUBLANES), n:] = xi
        last = r0 if reverse else r0 + SUBLANES - 1
        return ref[pl.ds(last, 1), :n], ref[pl.ds(last, 1), n:]

    return lax.fori_loop(0, ngroups, step, carry, unroll=2)


def _ssm_chunk(seq):
    return min(512, seq)


def _ssm_fwd(z, wb, wc, tab, dskip, nbatch, seq):
    t = z.shape[0]
    tc = _ssm_chunk(seq)
    nc = seq // tc
    n2 = 2 * SSM_LANES

    def body(u_ref, wb_ref, wc_ref, tab_ref, d_ref, st_ref, y_ref, gel_ref, car_ref):
        @pl.when(pl.program_id(1) == 0)
        def _():
            car_ref[...] = jnp.zeros_like(car_ref)

        u = u_ref[...]
        st_ref[...] = lax.dot_general(u, wb_ref[...], _DIMS["nn"], preferred_element_type=F32)
        cr, ci = _scan_block(st_ref, tab_ref, (car_ref[:, :SSM_LANES], car_ref[:, SSM_LANES:]), tc // SUBLANES, False)
        car_ref[:, :SSM_LANES] = cr
        car_ref[:, SSM_LANES:] = ci
        y = lax.dot_general(st_ref[...].astype(BF16), wc_ref[...], _DIMS["nn"], preferred_element_type=F32)
        y = y + d_ref[...] * u.astype(F32)
        y_ref[...] = y
        gel_ref[...] = _gelu(y).astype(gel_ref.dtype)

    row = lambda b, c: (b * nc + c, 0)
    full = lambda b, c: (0, 0)
    return pl.pallas_call(
        body, grid=(nbatch, nc),
        in_specs=[pl.BlockSpec((tc, SSM_WIDTH), lambda b, c: (b * nc + c, 3)), pl.BlockSpec((SSM_WIDTH, n2), full),
                  pl.BlockSpec((n2, SSM_WIDTH), full), pl.BlockSpec((8, SUBLANES, SSM_LANES), lambda b, c: (0, 0, 0)),
                  pl.BlockSpec((1, SSM_WIDTH), full)],
        out_specs=[pl.BlockSpec((tc, n2), row), pl.BlockSpec((tc, SSM_WIDTH), row), pl.BlockSpec((tc, SSM_WIDTH), row)],
        out_shape=[jax.ShapeDtypeStruct((t, n2), F32), jax.ShapeDtypeStruct((t, SSM_WIDTH), F32),
                   jax.ShapeDtypeStruct((t, SSM_WIDTH), BF16)],
        scratch_shapes=[pltpu.VMEM((1, n2), F32)], name="ssm_fwd",
        compiler_params=_params(("arbitrary", "arbitrary")))(z, wb, wc, tab, dskip)


def _ssm_bwd(dgi, ys, st, z, wbt, wct, tab_rev, dskip, nbatch, seq):
    t = z.shape[0]
    tc = _ssm_chunk(seq)
    nc = seq // tc
    n = SSM_LANES
    n2 = 2 * n
    ng = tc // SUBLANES

    def body(dgi_ref, ys_ref, st_ref, stp_ref, u_ref, wbt_ref, wct_ref, tab_ref, d_ref,
             du_ref, dwb_ref, dwc_ref, dd_ref, da_ref, p_ref, sb_ref, car_ref):
        b, c = pl.program_id(0), pl.program_id(1)
        ct = nc - 1 - c

        @pl.when((b == 0) & (c == 0))
        def _():
            dwb_ref[...] = jnp.zeros_like(dwb_ref)
            dwc_ref[...] = jnp.zeros_like(dwc_ref)
            dd_ref[...] = jnp.zeros_like(dd_ref)
            da_ref[...] = jnp.zeros_like(da_ref)

        @pl.when(c == 0)
        def _():
            car_ref[...] = jnp.zeros_like(car_ref)

        u = u_ref[...]
        dys = dgi_ref[...].astype(F32) * _gelu_grad(ys_ref[...])
        dys_b = dys.astype(BF16)
        st = st_ref[...]
        dd_ref[...] += jnp.sum(dys * u.astype(F32), axis=0, keepdims=True)
        dwc_ref[...] += lax.dot_general(st.astype(BF16), dys_b, _DIMS["tn"], preferred_element_type=F32)
        p_ref[...] = lax.dot_general(dys_b, wct_ref[...], _DIMS["nn"], preferred_element_type=F32)
        cr, ci = _scan_block(p_ref, tab_ref, (car_ref[:, :n], car_ref[:, n:]), ng, True)
        car_ref[:, :n] = cr
        car_ref[:, n:] = ci
        p = p_ref[...]
        pb = p.astype(BF16)
        dwb_ref[...] += lax.dot_general(u, pb, _DIMS["tn"], preferred_element_type=F32)
        du = lax.dot_general(pb, wbt_ref[...], _DIMS["nn"], preferred_element_type=F32) + d_ref[...] * dys
        du_ref[...] = du.astype(du_ref.dtype)
        sb_ref[pl.ds(0, SUBLANES), :] = jnp.where(ct > 0, stp_ref[...], 0.0)
        sb_ref[pl.ds(SUBLANES, tc), :] = st
        row0 = lax.broadcasted_iota(jnp.int32, (SUBLANES, n), 0) == 0

        def acc_step(g, acc):
            ar, ai = acc
            r0 = pl.multiple_of(g * SUBLANES, SUBLANES)
            edge_r = sb_ref[pl.ds(r0 + SUBLANES - 1, 1), :n]
            edge_i = sb_ref[pl.ds(r0 + SUBLANES - 1, 1), n:]
            sr = jnp.where(row0, edge_r, pltpu.roll(sb_ref[pl.ds(r0 + SUBLANES, SUBLANES), :n], 1, 0))
            si = jnp.where(row0, edge_i, pltpu.roll(sb_ref[pl.ds(r0 + SUBLANES, SUBLANES), n:], 1, 0))
            pr = p_ref[pl.ds(r0, SUBLANES), :n]
            pi = p_ref[pl.ds(r0, SUBLANES), n:]
            return ar + pr * sr + pi * si, ai + pi * sr - pr * si

        zero = jnp.zeros((SUBLANES, n), F32)
        ar, ai = lax.fori_loop(0, ng, acc_step, (zero, zero), unroll=2)
        da_ref[:, :n] += ar
        da_ref[:, n:] += ai

    row = lambda b, c: (b * nc + (nc - 1 - c), 0)
    prev8 = lambda b, c: (jnp.maximum((b * nc + (nc - 1 - c)) * (tc // SUBLANES) - 1, 0), 0)
    full = lambda b, c: (0, 0)
    return pl.pallas_call(
        body, grid=(nbatch, nc),
        in_specs=[pl.BlockSpec((tc, SSM_WIDTH), row), pl.BlockSpec((tc, SSM_WIDTH), row), pl.BlockSpec((tc, n2), row),
                  pl.BlockSpec((SUBLANES, n2), prev8),
                  pl.BlockSpec((tc, SSM_WIDTH), lambda b, c: (b * nc + (nc - 1 - c), 3)),
                  pl.BlockSpec((n2, SSM_WIDTH), full), pl.BlockSpec((SSM_WIDTH, n2), full),
                  pl.BlockSpec((8, SUBLANES, n), lambda b, c: (0, 0, 0)), pl.BlockSpec((1, SSM_WIDTH), full)],
        out_specs=[pl.BlockSpec((tc, SSM_WIDTH), row), pl.BlockSpec((SSM_WIDTH, n2), full),
                   pl.BlockSpec((n2, SSM_WIDTH), full), pl.BlockSpec((1, SSM_WIDTH), full),
                   pl.BlockSpec((SUBLANES, n2), full)],
        out_shape=[jax.ShapeDtypeStruct((t, SSM_WIDTH), BF16), jax.ShapeDtypeStruct((SSM_WIDTH, n2), F32),
                   jax.ShapeDtypeStruct((n2, SSM_WIDTH), F32), jax.ShapeDtypeStruct((1, SSM_WIDTH), F32),
                   jax.ShapeDtypeStruct((SUBLANES, n2), F32)],
        scratch_shapes=[pltpu.VMEM((tc, n2), F32), pltpu.VMEM((tc + SUBLANES, n2), F32), pltpu.VMEM((1, n2), F32)],
        name="ssm_bwd", compiler_params=_params(("arbitrary", "arbitrary")))(
        dgi, ys, st, st, z, wbt, wct, tab_rev, dskip)


def _ssm_prep(lam_re, lam_im, log_dt, b_re, b_im, c_re, c_im):
    lr = jnp.minimum(lam_re, -1e-4)
    li = lam_im
    dt = jnp.exp(log_dt)[:, None]
    mag = jnp.exp(lr * dt)
    a_re = mag * jnp.cos(li * dt)
    a_im = mag * jnp.sin(li * dt)
    den = lr * lr + li * li
    x_re, x_im = a_re - 1.0, a_im
    f_re = (x_re * lr + x_im * li) / den
    f_im = (x_im * lr - x_re * li) / den
    bb_re = f_re[..., None] * b_re - f_im[..., None] * b_im
    bb_im = f_re[..., None] * b_im + f_im[..., None] * b_re
    eye = jnp.eye(SSM_GROUPS, dtype=F32)
    emb_b = lambda v: jnp.einsum("gnh,gk->ghkn", v, eye).reshape(SSM_WIDTH, SSM_LANES)
    emb_c = lambda v: jnp.einsum("ghn,gk->gnkh", v, eye).reshape(SSM_LANES, SSM_WIDTH)
    wb = jnp.concatenate([emb_b(bb_re), emb_b(bb_im)], axis=1)
    wc = jnp.concatenate([emb_c(c_re), -emb_c(c_im)], axis=0)
    return a_re.reshape(-1), a_im.reshape(-1), wb, wc


def _ssm_tables(a_re, a_im, reverse):
    if reverse:
        a_im = -a_im
    pw = [(a_re, a_im)]
    for _ in range(SUBLANES - 1):
        pr, pi = pw[-1]
        pw.append((pr * a_re - pi * a_im, pr * a_im + pi * a_re))
    rows = jnp.arange(SUBLANES)[:, None]
    tabs = []
    for k in (1, 2, 4):
        ok = (rows + k <= SUBLANES - 1) if reverse else (rows >= k)
        tabs += [jnp.where(ok, pw[k - 1][0][None], 0.0), jnp.where(ok, pw[k - 1][1][None], 0.0)]
    order = list(range(SUBLANES - 1, -1, -1)) if reverse else list(range(SUBLANES))
    tabs += [jnp.stack([pw[i][0] for i in order]), jnp.stack([pw[i][1] for i in order])]
    return jnp.stack(tabs)


def _conv_chunk(seq):
    return min(512, seq)


def _shifted(buf, sh, tc, offsets):
    for b in range(SUBLANES):
        idx = [i for i, o in enumerate(offsets) if o % SUBLANES == b]
        if not idx:
            continue
        src = buf
        if b:
            span = tc + SUBLANES * max(offsets[i] // SUBLANES for i in idx)
            sh[pl.ds(0, span), :] = buf[pl.ds(b, span), :]
            src = sh
        for i in idx:
            yield i, src[pl.ds(offsets[i] // SUBLANES * SUBLANES, tc), :]


def _conv_fwd(z, w, bias, lg, lb, nbatch, seq):
    t = z.shape[0]
    tc = _conv_chunk(seq)
    nc = seq // tc

    def body(a_ref, g_ref, w_ref, b_ref, lg_ref, lb_ref, cv_ref, sc_ref, ubuf, sh):
        c = pl.program_id(1)

        @pl.when(c == 0)
        def _():
            ubuf[pl.ds(0, CONV_HALO), :] = jnp.zeros((CONV_HALO, CONV_WIDTH), F32)

        @pl.when(c > 0)
        def _():
            ubuf[pl.ds(0, CONV_HALO), :] = ubuf[pl.ds(tc, CONV_HALO), :]

        ubuf[pl.ds(CONV_HALO, tc), :] = a_ref[...].astype(F32) * _sig(g_ref[...].astype(F32))
        acc = jnp.zeros((tc, CONV_WIDTH), F32) + b_ref[...]
        for k, win in _shifted(ubuf, sh, tc, [CONV_HALO - (CONV_K - 1) + k for k in range(CONV_K)]):
            acc = acc + w_ref[pl.ds(k, 1), :] * win
        cv_ref[...] = acc
        mu = jnp.mean(acc, axis=-1, keepdims=True)
        xc = acc - mu
        y = xc * lax.rsqrt(jnp.mean(xc * xc, axis=-1, keepdims=True) + EPS) * lg_ref[...] + lb_ref[...]
        sc_ref[...] = (y * _sig(y)).astype(sc_ref.dtype)

    row = lambda b, c: (b * nc + c, 0)
    full = lambda b, c: (0, 0)
    vec = pl.BlockSpec((1, CONV_WIDTH), full)
    return pl.pallas_call(
        body, grid=(nbatch, nc),
        in_specs=[pl.BlockSpec((tc, CONV_WIDTH), lambda b, c: (b * nc + c, 4)),
                  pl.BlockSpec((tc, CONV_WIDTH), lambda b, c: (b * nc + c, 5)),
                  pl.BlockSpec((CONV_HALO, CONV_WIDTH), full), vec, vec, vec],
        out_specs=[pl.BlockSpec((tc, CONV_WIDTH), row), pl.BlockSpec((tc, CONV_WIDTH), row)],
        out_shape=[jax.ShapeDtypeStruct((t, CONV_WIDTH), F32), jax.ShapeDtypeStruct((t, CONV_WIDTH), BF16)],
        scratch_shapes=[pltpu.VMEM((CONV_HALO + tc, CONV_WIDTH), F32)] * 2, name="conv_fwd",
        compiler_params=_params(("arbitrary", "arbitrary")))(z, z, w, bias, lg, lb)


def _conv_bwd(dsc, cv, z, w, lg, lb, nbatch, seq):
    t = z.shape[0]
    tc = _conv_chunk(seq)
    nc = seq // tc
    hb = tc // CONV_HALO

    def body(dsc_ref, cv_ref, a_ref, g_ref, ap_ref, gp_ref, w_ref, lg_ref, lb_ref,
             da_ref, dg_ref, dw_ref, db_ref, dlg_ref, dlb_ref, ubuf, dbuf, sh):
        b, c = pl.program_id(0), pl.program_id(1)
        ct = nc - 1 - c

        @pl.when((b == 0) & (c == 0))
        def _():
            dw_ref[...] = jnp.zeros_like(dw_ref)
            db_ref[...] = jnp.zeros_like(db_ref)
            dlg_ref[...] = jnp.zeros_like(dlg_ref)
            dlb_ref[...] = jnp.zeros_like(dlb_ref)

        cvv = cv_ref[...]
        mu = jnp.mean(cvv, axis=-1, keepdims=True)
        xc = cvv - mu
        rstd = lax.rsqrt(jnp.mean(xc * xc, axis=-1, keepdims=True) + EPS)
        xh = xc * rstd
        y = xh * lg_ref[...] + lb_ref[...]
        sy = _sig(y)
        dy = dsc_ref[...].astype(F32) * (sy * (1.0 + y * (1.0 - sy)))
        dlg_ref[...] += jnp.sum(dy * xh, axis=0, keepdims=True)
        dlb_ref[...] += jnp.sum(dy, axis=0, keepdims=True)
        dxh = dy * lg_ref[...]
        dcv = rstd * (dxh - jnp.mean(dxh, axis=-1, keepdims=True) - xh * jnp.mean(dxh * xh, axis=-1, keepdims=True))
        db_ref[...] += jnp.sum(dcv, axis=0, keepdims=True)

        @pl.when(c == 0)
        def _():
            dbuf[pl.ds(tc, CONV_HALO), :] = jnp.zeros((CONV_HALO, CONV_WIDTH), F32)

        @pl.when(c > 0)
        def _():
            dbuf[pl.ds(tc, CONV_HALO), :] = dbuf[pl.ds(0, CONV_HALO), :]

        dbuf[pl.ds(0, tc), :] = dcv
        a = a_ref[...].astype(F32)
        sg = _sig(g_ref[...].astype(F32))
        ubuf[pl.ds(0, CONV_HALO), :] = jnp.where(ct > 0, ap_ref[...].astype(F32) * _sig(gp_ref[...].astype(F32)), 0.0)
        ubuf[pl.ds(CONV_HALO, tc), :] = a * sg
        du = jnp.zeros((tc, CONV_WIDTH), F32)
        for k, win in _shifted(dbuf, sh, tc, [CONV_K - 1 - k for k in range(CONV_K)]):
            du = du + w_ref[pl.ds(k, 1), :] * win
        for k, win in _shifted(ubuf, sh, tc, [CONV_HALO - (CONV_K - 1) + k for k in range(CONV_K)]):
            dw_ref[pl.ds(k, 1), :] += jnp.sum(dcv * win, axis=0, keepdims=True)
        da_ref[...] = (du * sg).astype(da_ref.dtype)
        dg_ref[...] = (du * a * sg * (1.0 - sg)).astype(dg_ref.dtype)

    row = lambda b, c: (b * nc + (nc - 1 - c), 0)
    full = lambda b, c: (0, 0)
    vec = pl.BlockSpec((1, CONV_WIDTH), full)
    blk = pl.BlockSpec((tc, CONV_WIDTH), row)

    def zcol(col):
        return pl.BlockSpec((tc, CONV_WIDTH), lambda b, c: (b * nc + (nc - 1 - c), col))

    def zprev(col):
        return pl.BlockSpec((CONV_HALO, CONV_WIDTH),
                            lambda b, c: (jnp.maximum((b * nc + (nc - 1 - c)) * hb - 1, 0), col))

    o = jax.ShapeDtypeStruct((t, CONV_WIDTH), BF16)
    v = jax.ShapeDtypeStruct((1, CONV_WIDTH), F32)
    return pl.pallas_call(
        body, grid=(nbatch, nc),
        in_specs=[blk, blk, zcol(4), zcol(5), zprev(4), zprev(5), pl.BlockSpec((CONV_HALO, CONV_WIDTH), full), vec, vec],
        out_specs=[blk, blk, pl.BlockSpec((CONV_HALO, CONV_WIDTH), full), vec, vec, vec],
        out_shape=[o, o, jax.ShapeDtypeStruct((CONV_HALO, CONV_WIDTH), F32), v, v, v],
        scratch_shapes=[pltpu.VMEM((CONV_HALO + tc, CONV_WIDTH), F32)] * 3, name="conv_bwd", compiler_params=_params(("arbitrary", "arbitrary")))(dsc, cv, z, z, z, z, w, lg, lb)


BIG = ("w_in", "w_attn_out", "w_ssm_glu", "w_conv_out", "w_mix_out", "w_ffn_in", "w_ffn_out", "w_ple_in", "w_ple_gate")
BIG_AXIS = {"w_in": 2, "w_attn_out": 2, "w_ssm_glu": 2, "w_conv_out": 2, "w_mix_out": 1, "w_ffn_in": 2,
            "w_ffn_out": 1, "w_ple_in": 2, "w_ple_gate": 1}
SHARD_MAJOR = ("w_in", "w_ffn_in")
SMALL = ("mix_norm_g", "b_gate", "attn_sinks", "ssm_lambda_re", "ssm_lambda_im", "ssm_log_dt", "ssm_b_re", "ssm_b_im",
         "ssm_c_re", "ssm_c_im", "ssm_d", "b_ssm_glu", "conv_dw_w", "conv_dw_b", "conv_norm_g", "conv_norm_b",
         "ffn_norm_g", "ple_norm_g", "final_norm_g")
WEIGHTS = ("mix_norm_g", "w_in", "b_gate", "attn_sinks", "w_attn_out", "ssm_lambda_re", "ssm_lambda_im", "ssm_log_dt",
           "ssm_b_re", "ssm_b_im", "ssm_c_re", "ssm_c_im", "ssm_d", "w_ssm_glu", "b_ssm_glu", "conv_dw_w", "conv_dw_b",
           "conv_norm_g", "conv_norm_b", "w_conv_out", "w_mix_out", "ffn_norm_g", "w_ffn_in", "w_ffn_out", "w_ple_in",
           "ple_norm_g", "w_ple_gate", "final_norm_g")
SSM_NAMES = ("ssm_lambda_re", "ssm_lambda_im", "ssm_log_dt", "ssm_b_re", "ssm_b_im", "ssm_c_re", "ssm_c_im")


def _heads(v, nh):
    return v.reshape(v.shape[0], nh, HEAD_DIM).transpose(1, 0, 2)


def _tokens(v):
    return v.transpose(1, 0, 2).reshape(v.shape[1], v.shape[0] * HEAD_DIM)


def _row(v):
    return v.reshape(1, -1)


def _layer_fwd(x, p_l, w, s, rope, nbatch, seq, next_shards=None):
    t = x.shape[0]
    tm = 512
    d = D_MODEL
    sv = {}
    sv["x"] = x
    h = _rowwise("rms_mix", _rms_fwd, [R(x), V(_row(s["mix_norm_g"]))], [O(d, BF16)], tm=tm)
    cs = {nm: w[nm].shape[2] for nm in SHARD_MAJOR}
    tb = 1024
    got = {}
    plan = None if next_shards is None else _gather_plan(next_shards, GATHER_A)
    z = _mm("mm_in", h, w["w_in"], "nn", BF16, m=t, n=N_CHIPS * cs["w_in"], k=d, tm=tb, tn=cs["w_in"], tk=d,
            b_sh=cs["w_in"], comm=plan)
    if plan is not None:
        z, outs = z
        got.update(zip(plan["names"], outs))
    sv["h"], sv["z"] = h, z
    c, sa, sb = rope
    qr, kr = _rowwise("rope_fwd", _rope_fwd, [R(z, Q_WIDTH, 0), R(z, KV_WIDTH, 4), R(c), R(sa), R(sb)],
                      [O(Q_WIDTH, BF16), O(KV_WIDTH, BF16)], tm=tm)
    qh, kh = _heads(qr, N_Q_HEADS), _heads(kr, N_KV_HEADS)
    vh = _heads(z[:, Q_WIDTH + KV_WIDTH:Q_WIDTH + 2 * KV_WIDTH], N_KV_HEADS)
    sinks = s["attn_sinks"].reshape(N_Q_HEADS, 1, 1)
    oh, lse = _attn_fwd(qh, kh, vh, sinks, nbatch, seq)
    o = _tokens(oh)
    ya = _mm("mm_attn_out", o, w["w_attn_out"], "nn", BF16, m=t, n=d, k=Q_WIDTH, tm=tb, tn=d, tk=Q_WIDTH)
    sv.update(qh=qh, kh=kh, vh=vh, oh=oh, lse=lse, o=o, ya=ya, sinks=sinks)
    ssm_args = [s[nm] for nm in SSM_NAMES]
    a_re, a_im, wb, wc = _ssm_prep(*ssm_args)
    dskip = _row(s["ssm_d"])
    st, ys, gel = _ssm_fwd(z, wb.astype(BF16), wc.astype(BF16), _ssm_tables(a_re, a_im, False), dskip, nbatch, seq)
    glu = _mm("mm_glu", gel, w["w_ssm_glu"], "nn", BF16, m=t, n=2 * d, k=SSM_WIDTH, tm=tb, tn=2 * d, tk=SSM_WIDTH,
              bias=_row(s["b_ssm_glu"]))
    sv.update(st=st, ys=ys, gel=gel, glu=glu, a=(a_re, a_im), wb=wb, wc=wc, dskip=dskip)
    cw = jnp.pad(s["conv_dw_w"], ((0, CONV_HALO - CONV_K), (0, 0)))
    cv, sc = _conv_fwd(z, cw, _row(s["conv_dw_b"]), _row(s["conv_norm_g"]), _row(s["conv_norm_b"]), nbatch, seq)
    yc = _mm("mm_conv_out", sc, w["w_conv_out"], "nn", BF16, m=t, n=d, k=CONV_WIDTH, tm=tb, tn=d, tk=CONV_WIDTH)
    sv.update(cw=cw, cv=cv, sc=sc, yc=yc)
    bg = _row(s["b_gate"])
    merge_ins = [R(z, 512, 3), R(z, 512, 5), R(z, 512, 7), V(bg, 512, 0), V(bg, 512, 2), V(bg, 512, 4),
                 R(ya, 512, 0), R(glu, 512, 0), R(glu, 512, 2), R(yc, 512, 0)]
    merged = _rowwise("merge_fwd", _merge_fwd, merge_ins, [O(512, BF16, total=d)], tm=tm, ncol=2)
    x1 = _mm("mm_mix", merged, w["w_mix_out"], "nn", F32, m=t, n=d, k=d, tm=tb, tn=d, tk=d, res=x)
    sv.update(merged=merged, x1=x1)
    hf = _rowwise("rms_ffn", _rms_fwd, [R(x1), V(_row(s["ffn_norm_g"]))], [O(d, BF16)], tm=tm)
    plan = None if next_shards is None else _gather_plan(next_shards, GATHER_B)
    f = _mm("mm_ffn_in", hf, w["w_ffn_in"], "nn", BF16, m=t, n=2 * FFN_HIDDEN, k=d, tm=tb, tn=cs["w_ffn_in"], tk=d,
            b_sh=cs["w_ffn_in"], comm=plan)
    if plan is not None:
        f, outs = f
        got.update(zip(plan["names"], outs))
    act = (lambda i, j, kk, fg, fu: _ffn_act(fg, fu), [(f, lambda i, j, kk: (i, 0)), (f, lambda i, j, kk: (i, 1))])
    x2 = _mm("mm_ffn_out", act, w["w_ffn_out"], "nn", F32, m=t, n=d, k=FFN_HIDDEN, tm=256, tn=d, tk=FFN_HIDDEN, res=x1)
    sv.update(hf=hf, f=f, x2=x2)
    e = _mm("mm_ple_in", p_l, w["w_ple_in"], "nn", BF16, m=t, n=d, k=p_l.shape[1], tm=tb, tn=d, tk=p_l.shape[1])
    hp = _rowwise("rms_ple", _rms_fwd, [R(x2), V(_row(s["ple_norm_g"]))], [O(d, BF16)], tm=tm)
    gp = _mm("mm_ple_gate", hp, w["w_ple_gate"], "nn", BF16, m=t, n=d, k=d, tm=tb, tn=d, tk=d)
    x3 = _rowwise("ple_fwd", _ple_fwd, [R(x2), R(gp), R(e)], [O(d, F32)], tm=tm)
    sv.update(e=e, hp=hp, gp=gp, p=p_l)
    return x3, sv, got


def _layer_bwd(dx3, sv, w, s, rope, nbatch, seq):
    t = dx3.shape[0]
    tm = 512
    d = D_MODEL
    gb, gs = {}, {}
    cs = {nm: w[nm].shape[2] for nm in SHARD_MAJOR}
    tb = 1024

    def wg(name, a, b, m, n, tm=1024, tk=1024, shard=None):
        return _mm(name, a, b, "tn", BF16, m=m, n=n, k=t, tm=tm, tn=n if shard is None else cs[shard], tk=tk,
                   o_sh=None if shard is None else cs[shard])

    de, dgp = _rowwise("ple_bwd", _ple_bwd, [R(dx3), R(sv["gp"]), R(sv["e"])], [O(d, BF16), O(d, BF16)], tm=tm)
    gb["w_ple_in"] = wg("wg_ple_in", sv["p"], de, sv["p"].shape[1], d, tk=2048)
    gb["w_ple_gate"] = wg("wg_ple_gate", sv["hp"], dgp, d, d, tk=2048)
    dhp = _mm("mmb_ple_gate", dgp, w["w_ple_gate"], "nt", BF16, m=t, n=d, k=d, tm=tb, tn=d, tk=d)
    dx2, gs["ple_norm_g"] = _rowwise("rms_ple_bwd", _rms_bwd, [R(dhp), R(sv["x2"]), R(dx3), V(_row(s["ple_norm_g"]))],
                                     [O(d, F32)], [A(d)], tm=tm)
    fw = FFN_HIDDEN // 2
    dact = _mm("mmb_ffn_out", dx2, w["w_ffn_out"], "nt", BF16, m=t, n=FFN_HIDDEN, k=d, tm=tb, tn=fw, tk=d)
    f = sv["f"]
    act_t = (lambda i, j, kk, fg, fu: _ffn_act(fg, fu), [(f, lambda i, j, kk: (kk, i)), (f, lambda i, j, kk: (kk, 2 + i))])
    gb["w_ffn_out"] = _mm("wg_ffn_out", act_t, dx2, "tn", BF16, m=FFN_HIDDEN, n=d, k=t, tm=fw, tn=d, tk=512)

    def df_tile(is_gate, da, fg, fu):
        dfg, dfu = _ffn_act_bwd(da, fg, fu)
        return jnp.where(is_gate, dfg, dfu)

    assert cs["w_ffn_in"] == fw
    df_cols = (lambda i, j, kk, *v: df_tile(j < 2, *v),
               [(dact, lambda i, j, kk: (kk, j % 2)), (f, lambda i, j, kk: (kk, j % 2)), (f, lambda i, j, kk: (kk, 2 + j % 2))])
    gb["w_ffn_in"] = _mm("wg_ffn_in", sv["hf"], df_cols, "tn", BF16, m=d, n=2 * FFN_HIDDEN, k=t, tm=d, tn=fw, tk=512,
                         o_sh=fw)
    df_rows = (lambda i, j, kk, *v: df_tile(kk < 2, *v),
               [(dact, lambda i, j, kk: (i, kk % 2)), (f, lambda i, j, kk: (i, kk % 2)), (f, lambda i, j, kk: (i, 2 + kk % 2))])
    dhf = _mm("mmb_ffn_in", df_rows, w["w_ffn_in"], "nt", BF16, m=t, n=d, k=2 * FFN_HIDDEN, tm=512, tn=d, tk=fw, b_sh=fw)
    dx1, gs["ffn_norm_g"] = _rowwise("rms_ffn_bwd", _rms_bwd, [R(dhf), R(sv["x1"]), R(dx2), V(_row(s["ffn_norm_g"]))],
                                     [O(d, F32)], [A(d)], tm=tm)
    dm = _mm("mmb_mix", dx1, w["w_mix_out"], "nt", BF16, m=t, n=d, k=d, tm=tb, tn=d, tk=d)
    gb["w_mix_out"] = wg("wg_mix", sv["merged"], dx1, d, d)
    z, glu, bg = sv["z"], sv["glu"], _row(s["b_gate"])
    ins = [R(dm, 512, 0), R(z, 512, 3), R(z, 512, 5), R(z, 512, 7), V(bg, 512, 0), V(bg, 512, 2), V(bg, 512, 4),
           R(sv["ya"], 512, 0), R(glu, 512, 0), R(glu, 512, 2), R(sv["yc"], 512, 0)]
    ob = lambda: O(512, BF16, total=d)
    ab = lambda: A(512, total=d)
    dya, dga, dgb, dyc, d0, d1, d2, db0, db1, db2, dba, dbb = _rowwise(
        "merge_bwd", _merge_bwd, ins, [ob() for _ in range(7)], [ab() for _ in range(5)], tm=tm, ncol=2)
    gs["b_gate"] = jnp.concatenate([db0, db1, db2], axis=1)
    gs["b_ssm_glu"] = jnp.concatenate([dba, dbb], axis=1)
    dglu = jnp.concatenate([dga, dgb], axis=1)
    gb["w_attn_out"] = wg("wg_attn_out", sv["o"], dya, Q_WIDTH, d, tk=2048)
    do = _mm("mmb_attn_out", dya, w["w_attn_out"], "nt", BF16, m=t, n=Q_WIDTH, k=d, tm=tb, tn=Q_WIDTH, tk=d)
    dqh, dkc, dvc, dkp, dvp, dsink = _attn_bwd(sv["qh"], sv["kh"], sv["vh"], sv["oh"], _heads(do, N_Q_HEADS),
                                               sv["lse"], sv["sinks"], nbatch, seq)
    gs["attn_sinks"] = dsink.reshape(-1)
    c, sa, sb = rope
    dq = _rowwise("rope_bwd_q", _rope_bwd_q, [R(_tokens(dqh)), R(c), R(sa), R(sb)], [O(Q_WIDTH, BF16)], tm=tm)
    dk, dv = _kv_combine(_tokens(dkc), _tokens(dkp), _tokens(dvc), _tokens(dvp), c, sa, sb, seq)
    gb["w_ssm_glu"] = wg("wg_ssm_glu", sv["gel"], dglu, SSM_WIDTH, 2 * d, tk=2048)
    dgi = _mm("mmb_glu", dglu, w["w_ssm_glu"], "nt", BF16, m=t, n=SSM_WIDTH, k=2 * d, tm=tb, tn=SSM_WIDTH, tk=2 * d)
    a_re, a_im = sv["a"]
    du, dwb, dwc, dd, da = _ssm_bwd(dgi, sv["ys"], sv["st"], z, sv["wb"].T.astype(BF16), sv["wc"].T.astype(BF16),
                                    _ssm_tables(a_re, a_im, True), sv["dskip"], nbatch, seq)
    gs["ssm_d"] = dd.reshape(-1)
    da = jnp.sum(da, axis=0)
    _, prep_vjp = jax.vjp(_ssm_prep, *[s[nm] for nm in SSM_NAMES])
    for nm, g in zip(SSM_NAMES, prep_vjp((da[:SSM_LANES], da[SSM_LANES:], dwb, dwc))):
        gs[nm] = g
    gb["w_conv_out"] = wg("wg_conv_out", sv["sc"], dyc, CONV_WIDTH, d, tk=2048)
    dsc = _mm("mmb_conv_out", dyc, w["w_conv_out"], "nt", BF16, m=t, n=CONV_WIDTH, k=d, tm=tb, tn=CONV_WIDTH, tk=d)
    dca, dcg, dcw, dcb, dlg, dlb = _conv_bwd(dsc, sv["cv"], z, sv["cw"], _row(s["conv_norm_g"]),
                                             _row(s["conv_norm_b"]), nbatch, seq)
    gs["conv_dw_w"] = dcw[:CONV_K]
    gs["conv_dw_b"], gs["conv_norm_g"], gs["conv_norm_b"] = dcb.reshape(-1), dlg.reshape(-1), dlb.reshape(-1)
    dz = jnp.concatenate([dq, dk, dv, du, dca, dcg, d0, d1, d2], axis=1)
    gb["w_in"] = wg("wg_in", sv["h"], dz, d, dz.shape[1], tk=2048, shard="w_in")
    dh = _mm("mmb_in", dz, w["w_in"], "nt", BF16, m=t, n=d, k=dz.shape[1], tm=tb, tn=d, tk=cs["w_in"],
             b_sh=cs["w_in"])
    dx, gs["mix_norm_g"] = _rowwise("rms_mix_bwd", _rms_bwd, [R(dh), R(sv["x"]), R(dx1), V(_row(s["mix_norm_g"]))],
                                    [O(d, F32)], [A(d)], tm=tm)
    gs["mix_norm_g"], gs["ffn_norm_g"], gs["ple_norm_g"] = (gs[nm].reshape(-1) for nm in
                                                            ("mix_norm_g", "ffn_norm_g", "ple_norm_g"))
    gs["b_gate"], gs["b_ssm_glu"] = gs["b_gate"].reshape(-1), gs["b_ssm_glu"].reshape(-1)
    return dx, {nm: _shard_major(nm, g) for nm, g in gb.items()}, gs


def _rope_tables(positions):
    inv_freq = ROPE_THETA ** (-jnp.arange(0, ROPE_DIM, 2, dtype=F32) / ROPE_DIM)
    ang = positions.reshape(-1).astype(F32)[:, None] * inv_freq
    cos, sin = jnp.cos(ang), jnp.sin(ang)
    t = ang.shape[0]
    rest = HEAD_DIM - ROPE_DIM
    c = jnp.concatenate([cos, cos, jnp.ones((t, rest), F32)], axis=1)
    sa = jnp.concatenate([-sin, jnp.zeros((t, HEAD_DIM - ROPE_HALF), F32)], axis=1)
    sb = jnp.concatenate([jnp.zeros((t, ROPE_HALF), F32), sin, jnp.zeros((t, rest), F32)], axis=1)
    two = lambda v: jnp.concatenate([v, v], axis=1)
    return two(c), two(sa), two(sb)


def _natural(nm, w4):
    if nm in SHARD_MAJOR:
        return w4
    if BIG_AXIS[nm] == 1:
        return w4.reshape(-1, w4.shape[2])
    return w4.transpose(1, 0, 2).reshape(w4.shape[1], -1)


def _shard_major(nm, g):
    if nm in SHARD_MAJOR:
        return g
    if BIG_AXIS[nm] == 1:
        return g.reshape(N_CHIPS, -1, g.shape[1])
    return g.reshape(g.shape[0], N_CHIPS, -1).transpose(1, 0, 2)


def _untap(taps4, cols):
    flat = taps4.reshape(N_CHIPS, -1)[:, :CONV_K * cols]
    return flat.reshape(N_CHIPS, CONV_K, cols).transpose(1, 0, 2).reshape(CONV_K, N_CHIPS * cols)


def _local_step(x, p, positions, loss_target, small, wfull=None, shards=None):
    nbatch, seq, d = x.shape
    depth = p.shape[0]
    t = nbatch * seq
    rope = _rope_tables(positions)
    xs = x.reshape(t, d)
    saved, ws, ss = [], [], []
    got = None if shards is None else _gather_now(shards[0])
    for l in range(depth):
        w4 = {nm: wfull[nm][l] for nm in BIG} if shards is None else got
        w_l = {nm: _natural(nm, w4[nm]) for nm in BIG}
        s_l = {nm: small[nm][l] for nm in small if nm != "final_norm_g"}
        if shards is not None:
            s_l["conv_dw_w"] = _untap(got[TAPS], CONV_WIDTH // N_CHIPS)
        nxt = shards[l + 1] if shards is not None and l + 1 < depth else None
        xs, sv, got = _layer_fwd(xs, p[l].reshape(t, -1), w_l, s_l, rope, nbatch, seq, nxt)
        saved.append(sv)
        ws.append(w_l)
        ss.append(s_l)
    dx, loss_cols, dgf = _rowwise("loss_head", _loss_fn, [R(xs), R(loss_target.reshape(t, d)),
                                                          V(_row(small["final_norm_g"]))],
                                  [O(d, F32)], [A(d), A(d)], tm=512)
    gbs, gss = [None] * depth, [None] * depth
    for l in reversed(range(depth)):
        dx, gbs[l], gss[l] = _layer_bwd(dx, saved[l], ws[l], ss[l], rope, nbatch, seq)
    gbig = {nm: jnp.stack([g[nm] for g in gbs]) for nm in BIG}
    gsmall = {nm: jnp.stack([g[nm] for g in gss]) for nm in SMALL if nm != "final_norm_g"}
    gsmall["final_norm_g"] = dgf.reshape(-1)
    return loss_cols, dx.reshape(nbatch, seq, d), gbig, gsmall


HBM = pl.BlockSpec(memory_space=pltpu.HBM)


def _place():
    x, y, c = lax.axis_index("x"), lax.axis_index("y"), lax.axis_index("c")
    chips = [(1 - x, y), (x, 1 - y), (1 - x, 1 - y)]
    return x, y, c, chips


def _remote(src, dst, send_sem, recv_sem, to):
    return pltpu.make_async_remote_copy(src_ref=src, dst_ref=dst, send_sem=send_sem, recv_sem=recv_sem,
                                        device_id=to, device_id_type=MESH)


TAPS = "taps"
GATHER_ALL = (("w_ffn_in", "w_ffn_out"),
              ("w_in", "w_ple_gate", "w_mix_out", "w_attn_out", "w_ssm_glu", "w_conv_out", "w_ple_in", TAPS))
GATHER_A = (("w_ffn_in",), ("w_in", "w_ple_gate"))
GATHER_B = (("w_ffn_out",), ("w_mix_out", "w_attn_out", "w_ssm_glu", "w_conv_out", "w_ple_in", TAPS))


def _gather_plan(shards, sets):
    names = sets[0] + sets[1]
    n = len(names)
    idx = {nm: i for i, nm in enumerate(names)}

    def start(ins, outs, sems):
        send1, recv1, _, _, send0, recv0 = sems
        x, y, c, chips = _place()
        me = 2 * x + y
        for i in range(n):
            _remote(ins[i], outs[i].at[me], send0.at[i], recv0.at[i], (x, y, 1 - c)).start()
        for role in (0, 1):
            @pl.when(c == role)
            def _():
                for nm in sets[role]:
                    i = idx[nm]
                    for k, (cx, cy) in enumerate(chips):
                        _remote(ins[i], outs[i].at[me], send1.at[i, k], recv1.at[i, k], (cx, cy, c)).start()

    def finish(ins, outs, sems):
        send1, recv1, send2, recv2, send0, recv0 = sems
        x, y, c, chips = _place()
        me = 2 * x + y
        sib = (x, y, 1 - c)
        for role in (0, 1):
            @pl.when(c == role)
            def _():
                passed = []
                for nm in sets[role]:
                    i = idx[nm]
                    for k, (cx, cy) in enumerate(chips):
                        slot = outs[i].at[2 * cx + cy]
                        _remote(slot, slot, send1.at[i, k], recv1.at[i, k], (cx, cy, c)).wait_recv()
                        cp = _remote(slot, slot, send2.at[i, k], recv2.at[i, k], sib)
                        cp.start()
                        passed.append(cp)
                for nm in sets[1 - role]:
                    i = idx[nm]
                    for k, (cx, cy) in enumerate(chips):
                        slot = outs[i].at[2 * cx + cy]
                        _remote(slot, slot, send2.at[i, k], recv2.at[i, k], sib).wait_recv()
                for nm in sets[role]:
                    i = idx[nm]
                    for k, (cx, cy) in enumerate(chips):
                        _remote(ins[i], outs[i].at[me], send1.at[i, k], recv1.at[i, k], (cx, cy, c)).wait_send()
                for cp in passed:
                    cp.wait_send()
        for i in range(n):
            _remote(ins[i], outs[i].at[me], send0.at[i], recv0.at[i], sib).wait()

    ins = [shards[nm] for nm in names]
    return dict(names=names, ins=ins, start=start, finish=finish,
                out_shapes=[jax.ShapeDtypeStruct((N_CHIPS,) + v.shape, v.dtype) for v in ins],
                sems=[pltpu.SemaphoreType.DMA((n, 3)) for _ in range(4)] + [pltpu.SemaphoreType.DMA((n,))
                                                                            for _ in range(2)])


def _gather_now(shards):
    return _comm_now("gather_weights", _gather_plan(shards, GATHER_ALL))


def _pair_exchange(grads):
    n = len(grads)
    hl = grads[0].shape[0] // 2

    def body(*refs):
        ins, outs = refs[:n], refs[n:2 * n]
        send, recv = refs[2 * n:]
        x, y, c, _ = _place()
        other = pl.ds((1 - c) * hl, hl)
        cps = [_remote(ins[i].at[other], outs[i], send.at[i], recv.at[i], (x, y, 1 - c)) for i in range(n)]
        for cp in cps:
            cp.start()
        for cp in cps:
            cp.wait()

    out_shape = [jax.ShapeDtypeStruct((hl,) + g.shape[1:], g.dtype) for g in grads]
    sems = [pltpu.SemaphoreType.DMA((n,)) for _ in range(2)]
    return pl.pallas_call(body, out_shape=out_shape, in_specs=[HBM] * n, out_specs=[HBM] * n, scratch_shapes=sems,
                          name="reduce_pair_exchange")(*grads)


def _pair_add(g, r):
    hl, _, rr, cc = r.shape
    rows = hl * N_CHIPS * rr
    nblk = rows // rr

    def body(c_ref, g_ref, r_ref, o_ref):
        o_ref[...] = (g_ref[...].astype(F32) + r_ref[...].astype(F32)).astype(o_ref.dtype)

    grid_spec = pltpu.PrefetchScalarGridSpec(
        num_scalar_prefetch=1, grid=(nblk,),
        in_specs=[pl.BlockSpec((rr, cc), lambda i, c_ref: (c_ref[0] * nblk + i, 0)),
                  pl.BlockSpec((rr, cc), lambda i, c_ref: (i, 0))],
        out_specs=pl.BlockSpec((rr, cc), lambda i, c_ref: (i, 0)))
    c = lax.axis_index("c").astype(jnp.int32).reshape(1)
    out = pl.pallas_call(body, out_shape=jax.ShapeDtypeStruct((rows, cc), r.dtype), grid_spec=grid_spec,
                         name="reduce_pair_add", compiler_params=_params(("parallel",)))(
        c, g.reshape(-1, cc), r.reshape(rows, cc))
    return out.reshape(r.shape)


def _chip_exchange(psums):
    n = len(psums)

    def body(*refs):
        ins, got = refs[:n], refs[n:2 * n]
        send, recv = refs[2 * n:]
        x, y, c, chips = _place()
        cps = [_remote(ins[i].at[:, 2 * cx + cy], got[i].at[k], send.at[i, k], recv.at[i, k], (cx, cy, c))
               for i in range(n) for k, (cx, cy) in enumerate(chips)]
        for cp in cps:
            cp.start()
        for cp in cps:
            cp.wait()

    got_shape = [jax.ShapeDtypeStruct((3, p.shape[0]) + p.shape[2:], p.dtype) for p in psums]
    sems = [pltpu.SemaphoreType.DMA((n, 3)), pltpu.SemaphoreType.DMA((n, 3))]
    return pl.pallas_call(body, out_shape=got_shape, in_specs=[HBM] * n, out_specs=[HBM] * n, scratch_shapes=sems,
                          name="reduce_chip_exchange")(*psums)


def _comm_now(name, plan):
    n = len(plan["ins"])

    def body(*refs):
        ins, outs, sems = refs[:n], refs[n:2 * n], refs[2 * n:]
        plan["start"](ins, outs, sems)
        plan["finish"](ins, outs, sems)

    outs = pl.pallas_call(body, out_shape=plan["out_shapes"], in_specs=[HBM] * n, out_specs=[HBM] * n,
                          scratch_shapes=plan["sems"], name=name)(*plan["ins"])
    return dict(zip(plan["names"], outs))


def _sum4(psum, got):
    hl, _, rr, cc = psum.shape
    tr = rr if rr * cc <= 512 * 1024 else rr // 2

    def body(place_ref, own_ref, g0_ref, g1_ref, g2_ref, o_ref):
        tot = (own_ref[...].astype(F32) + g0_ref[...].astype(F32)) + g1_ref[...].astype(F32)
        o_ref[...] = tot + g2_ref[...].astype(F32)

    def got_spec(k):
        return pl.BlockSpec((None, None, tr, cc), lambda h, i, place: (k, h, i, 0))

    grid_spec = pltpu.PrefetchScalarGridSpec(
        num_scalar_prefetch=1, grid=(hl, rr // tr),
        in_specs=[pl.BlockSpec((None, None, tr, cc), lambda h, i, place: (h, place[0], i, 0)),
                  got_spec(0), got_spec(1), got_spec(2)],
        out_specs=pl.BlockSpec((None, tr, cc), lambda h, i, place: (place[1] * hl + h, i, 0)))
    place = jnp.stack([2 * lax.axis_index("x") + lax.axis_index("y"), lax.axis_index("c")]).astype(jnp.int32)
    return pl.pallas_call(body, out_shape=jax.ShapeDtypeStruct((2 * hl, rr, cc), F32), grid_spec=grid_spec,
                          name="reduce_sum4", compiler_params=_params(("parallel", "parallel")))(
        place, psum, got, got, got)


def _pair_gather(sums):
    n = len(sums)
    hl = sums[0].shape[0] // 2

    def body(*refs):
        bufs = refs[n:2 * n]
        send, recv = refs[2 * n:]
        x, y, c, _ = _place()
        mine = pl.ds(c * hl, hl)
        cps = [_remote(bufs[i].at[mine], bufs[i].at[mine], send.at[i], recv.at[i], (x, y, 1 - c)) for i in range(n)]
        for cp in cps:
            cp.start()
        for cp in cps:
            cp.wait()

    out_shape = [jax.ShapeDtypeStruct(v.shape, v.dtype) for v in sums]
    sems = [pltpu.SemaphoreType.DMA((n,)) for _ in range(2)]
    return pl.pallas_call(body, out_shape=out_shape, in_specs=[HBM] * n, out_specs=[HBM] * n, scratch_shapes=sems,
                          input_output_aliases={i: i for i in range(n)}, name="reduce_pair_gather")(*sums)


def _allreduce_small(vec):
    rows = vec.shape[0]

    def body(v_ref, o_ref, all_ref, send, recv):
        x, y, c, _ = _place()
        me = 4 * x + 2 * y + c
        all_ref[me] = v_ref[...]
        cps = []
        for dlt in range(1, N_DEV):
            fx, fy, fc = (dlt >> 2) & 1, (dlt >> 1) & 1, dlt & 1
            to = (1 - x if fx else x, 1 - y if fy else y, 1 - c if fc else c)
            cps.append(_remote(v_ref, all_ref.at[me], send.at[dlt - 1], recv.at[dlt - 1], to))
        for cp in cps:
            cp.start()
        for cp in cps:
            cp.wait()
        tot = all_ref[0]
        for dev in range(1, N_DEV):
            tot = tot + all_ref[dev]
        o_ref[...] = tot

    vm = pl.BlockSpec(memory_space=pltpu.VMEM)
    return pl.pallas_call(
        body, out_shape=jax.ShapeDtypeStruct(vec.shape, F32), in_specs=[vm], out_specs=vm,
        scratch_shapes=[pltpu.VMEM((N_DEV, rows, 128), F32), pltpu.SemaphoreType.DMA((N_DEV - 1,)),
                        pltpu.SemaphoreType.DMA((N_DEV - 1,))],
        name="allreduce_small", compiler_params=pltpu.CompilerParams(vmem_limit_bytes=VMEM_LIMIT))(vec)


def _adamw(name, w, g, m, v):
    rows, cc = w.shape
    tm = math.gcd(rows, 256)
    return _rowwise(name, _adamw_fn, [R(w), R(g), R(m), R(v)], [O(cc, F32), O(cc, F32), O(cc, F32)], tm=tm)


def _pack(parts):
    flat = jnp.concatenate([v.reshape(-1).astype(F32) for v in parts])
    pad = (-flat.shape[0]) % (SUBLANES * 128)
    return jnp.pad(flat, (0, pad)).reshape(-1, 128)


def _unpack(packed, shapes):
    flat, out, pos = packed.reshape(-1), [], 0
    for shp in shapes:
        size = math.prod(shp)
        out.append(flat[pos:pos + size].reshape(shp))
        pos += size
    return out


def kernel(x, p, positions, mix_norm_g, w_in, b_gate, attn_sinks, w_attn_out, ssm_lambda_re, ssm_lambda_im, ssm_log_dt, ssm_b_re, ssm_b_im, ssm_c_re, ssm_c_im, ssm_d, w_ssm_glu, b_ssm_glu, conv_dw_w, conv_dw_b, conv_norm_g, conv_norm_b, w_conv_out, w_mix_out, ffn_norm_g, w_ffn_in, w_ffn_out, w_ple_in, ple_norm_g, w_ple_gate, final_norm_g, loss_target, m_mix_norm_g, m_w_in, m_b_gate, m_attn_sinks, m_w_attn_out, m_ssm_lambda_re, m_ssm_lambda_im, m_ssm_log_dt, m_ssm_b_re, m_ssm_b_im, m_ssm_c_re, m_ssm_c_im, m_ssm_d, m_w_ssm_glu, m_b_ssm_glu, m_conv_dw_w, m_conv_dw_b, m_conv_norm_g, m_conv_norm_b, m_w_conv_out, m_w_mix_out, m_ffn_norm_g, m_w_ffn_in, m_w_ffn_out, m_w_ple_in, m_ple_norm_g, m_w_ple_gate, m_final_norm_g, v_mix_norm_g, v_w_in, v_b_gate, v_attn_sinks, v_w_attn_out, v_ssm_lambda_re, v_ssm_lambda_im, v_ssm_log_dt, v_ssm_b_re, v_ssm_b_im, v_ssm_c_re, v_ssm_c_im, v_ssm_d, v_w_ssm_glu, v_b_ssm_glu, v_conv_dw_w, v_conv_dw_b, v_conv_norm_g, v_conv_norm_b, v_w_conv_out, v_w_mix_out, v_ffn_norm_g, v_w_ffn_in, v_w_ffn_out, v_w_ple_in, v_ple_norm_g, v_w_ple_gate, v_final_norm_g):
    given = dict(locals())
    wts = {nm: given[nm] for nm in WEIGHTS}
    mom = {nm: given["m_" + nm] for nm in WEIGHTS}
    var = {nm: given["v_" + nm] for nm in WEIGHTS}
    depth = p.shape[0]
    chip = 2 * lax.axis_index("x") + lax.axis_index("y")

    cw_cols = conv_dw_w.shape[2]
    taps = jnp.pad(conv_dw_w.reshape(depth, -1), ((0, 0), (0, (-CONV_K * cw_cols) % (SUBLANES * 128))))
    shards = [{**{nm: wts[nm][l].astype(BF16) for nm in BIG}, TAPS: taps[l].reshape(-1, 128)} for l in range(depth)]
    small = {nm: wts[nm] for nm in SMALL if nm != "conv_dw_w"}

    loss_cols, grad_x, gbig, gsmall = _local_step(x, p, positions, loss_target, small, shards=shards)

    parts = [loss_cols] + [gsmall[nm] for nm in SMALL]
    total = _allreduce_small(_pack(parts))
    summed = _unpack(total, [v.shape for v in parts])
    loss = jnp.sum(summed[0])
    gsum = dict(zip(SMALL, summed[1:]))
    gsum["conv_dw_w"] = lax.dynamic_slice_in_dim(gsum["conv_dw_w"], chip * cw_cols, cw_cols, axis=2)
    shapes = [wts[nm].shape for nm in SMALL]
    deltas, new_m, new_v = _adamw("adamw_small", _pack([wts[nm] for nm in SMALL]), _pack([gsum[nm] for nm in SMALL]),
                                  _pack([mom[nm] for nm in SMALL]), _pack([var[nm] for nm in SMALL]))
    grads = dict(gsum)
    delta = dict(zip(SMALL, _unpack(deltas, shapes)))
    newm = dict(zip(SMALL, _unpack(new_m, shapes)))
    newv = dict(zip(SMALL, _unpack(new_v, shapes)))

    gl = [gbig[nm] for nm in BIG]
    sib = _pair_exchange(gl)
    psums = [_pair_add(g, r) for g, r in zip(gl, sib)]
    got = _chip_exchange(psums)
    sums = _pair_gather([_sum4(ps, g) for ps, g in zip(psums, got)])
    for nm, g in zip(BIG, sums):
        shp = wts[nm].shape
        two = lambda v: v.reshape(-1, shp[-1])
        g = g.reshape(shp)
        d_w, n_m, n_v = _adamw("adamw_" + nm, two(wts[nm]), two(g), two(mom[nm]), two(var[nm]))
        grads[nm], delta[nm], newm[nm], newv[nm] = g, d_w.reshape(shp), n_m.reshape(shp), n_v.reshape(shp)

    return (loss, grad_x, *[grads[nm] for nm in WEIGHTS], *[delta[nm] for nm in WEIGHTS],
            *[newm[nm] for nm in WEIGHTS], *[newv[nm] for nm in WEIGHTS])
```

```python
import functools
import math

import jax
import jax.numpy as jnp
from jax import lax
from jax.experimental import pallas as pl
from jax.experimental.pallas import tpu as pltpu

F32 = jnp.float32
BF16 = jnp.bfloat16

D_MODEL = 1024
HEAD_DIM = 64
N_Q_HEADS = 8
N_KV_HEADS = 2
GQA_GROUP = N_Q_HEADS // N_KV_HEADS
ATT_BLOCK = 128
ROPE_THETA = 500000.0
ROPE_DIM = HEAD_DIM // 4
ROPE_HALF = ROPE_DIM // 2
Q_WIDTH = N_Q_HEADS * HEAD_DIM
KV_WIDTH = N_KV_HEADS * HEAD_DIM
SSM_WIDTH = 256
SSM_GROUP = 16
SSM_GROUPS = 16
SSM_STATE = 64
SSM_LANES = SSM_GROUPS * SSM_STATE
CONV_WIDTH = 256
CONV_K = 31
CONV_HALO = 32
FFN_HIDDEN = 2816
EPS = 1e-6
NEG_INF = -1e30
SCALE = HEAD_DIM ** -0.5

ADAM_LR = 0.001
ADAM_B1 = 0.9
ADAM_B2 = 0.999
ADAM_EPS = 1e-08
ADAM_WD = 0.01
ADAM_STEP = 10

N_CHIPS = 4
N_DEV = 8
SUBLANES = 8
VMEM_LIMIT = 56 * 1024 * 1024

MESH = pl.DeviceIdType.MESH


def _params(sem=None):
    return pltpu.CompilerParams(dimension_semantics=sem, vmem_limit_bytes=VMEM_LIMIT)


def R(arr, width=None, cb=0, rb=0):
    return ("r", arr, arr.shape[1] if width is None else width, (cb, rb))


def V(arr, width=None, cb=0):
    return ("v", arr, arr.shape[1] if width is None else width, cb)


def _cbf(cb):
    return cb if callable(cb) else (lambda j, c=cb: c + j)


def _rowwise(name, fn, ins, outs, accs=(), *, tm, ncol=1):
    t = [a for k, a, _, _ in ins if k == "r"][0].shape[0]
    tm = min(tm, t)
    assert t % tm == 0, (name, t, tm)
    n_i, n_o, n_a = len(ins), len(outs), len(accs)

    def body(*refs):
        vals = fn(*[r[...] for r in refs[:n_i]])
        if not isinstance(vals, (tuple, list)):
            vals = (vals,)
        for ref, val in zip(refs[n_i:n_i + n_o], vals[:n_o]):
            ref[...] = val.astype(ref.dtype)
        if n_a:
            acc_refs = refs[n_i + n_o:]

            @pl.when(pl.program_id(1) == 0)
            def _():
                for ref in acc_refs:
                    ref[...] = jnp.zeros_like(ref)

            for ref, val in zip(acc_refs, vals[n_o:]):
                ref[...] += val

    in_specs = []
    for kind, arr, width, cb in ins:
        if kind == "r":
            f = _cbf(cb[0])
            in_specs.append(pl.BlockSpec((tm, width), functools.partial(lambda j, i, f, rb: (i + rb, f(j)), f=f, rb=cb[1])))
        else:
            f = _cbf(cb)
            in_specs.append(pl.BlockSpec((arr.shape[0], width), functools.partial(lambda j, i, f: (0, f(j)), f=f)))
    out_specs, out_shape = [], []
    for total, width, cb, dt in outs:
        f = _cbf(cb)
        out_specs.append(pl.BlockSpec((tm, width), functools.partial(lambda j, i, f: (i, f(j)), f=f)))
        out_shape.append(jax.ShapeDtypeStruct((t, total), dt))
    for total, width, cb in accs:
        f = _cbf(cb)
        out_specs.append(pl.BlockSpec((1, width), functools.partial(lambda j, i, f: (0, f(j)), f=f)))
        out_shape.append(jax.ShapeDtypeStruct((1, total), F32))
    sem = ("arbitrary", "arbitrary") if n_a else ("parallel", "parallel")
    res = pl.pallas_call(body, out_shape=out_shape, grid=(ncol, t // tm), in_specs=in_specs, out_specs=out_specs,
                         name=name, compiler_params=_params(sem))(*[a for _, a, _, _ in ins])
    return res[0] if len(res) == 1 else res


def O(width, dtype, total=None, cb=0):
    return (width if total is None else total, width, cb, dtype)


def A(width, total=None, cb=0):
    return (width if total is None else total, width, cb)


_DIMS = {"nn": (((1,), (0,)), ((), ())), "nt": (((1,), (1,)), ((), ())), "tn": (((0,), (0,)), ((), ()))}


def _mm(name, a, b, mode, out_dtype, *, m, n, k, tm, tn, tk, a_off=0, b_off=0, res=None, bias=None, b_sh=None, o_sh=None,
        comm=None, a_keep=False):
    tm, tn, tk = min(tm, m), min(tn, n), min(tk, k)
    assert m % tm == 0 and n % tn == 0 and k % tk == 0, (name, m, n, k, tm, tn, tk)
    nk = k // tk
    has_res, has_bias = res is not None, bias is not None
    a_fn, a_ops = a if isinstance(a, tuple) else (None, [(a, None)])
    b_fn, b_ops = b if isinstance(b, tuple) else (None, [(b, None)])
    na, nb_ = len(a_ops), len(b_ops)
    a_bytes = sum(m * k * arr.dtype.itemsize for arr, _ in a_ops)
    b_bytes = sum(n * k * arr.dtype.itemsize for arr, _ in b_ops)
    swap = nk == 1 and b_bytes + (n // tn) * a_bytes < a_bytes + (m // tm) * b_bytes
    grid = (n // tn, m // tm, nk) if swap else (m // tm, n // tn, nk)
    ncomm = 0 if comm is None else len(comm["ins"])

    def body(*refs):
        g0, g1, kk = pl.program_id(0), pl.program_id(1), pl.program_id(2)
        gi, gj = (g1, g0) if swap else (g0, g1)
        a_tiles = [r[...] for r in refs[:na]]
        b_tiles = [r[...] for r in refs[na:na + nb_]]
        a_val = a_tiles[0] if a_fn is None else a_fn(gi, gj, kk, *a_tiles)
        b_val = b_tiles[0] if b_fn is None else b_fn(gi, gj, kk, *b_tiles)
        pos = na + nb_
        res_ref = bias_ref = None
        if has_res:
            res_ref = refs[pos]
            pos += 1
        if has_bias:
            bias_ref = refs[pos]
            pos += 1
        comm_ins = refs[pos:pos + ncomm]
        o_ref = refs[pos + ncomm]
        comm_outs = refs[pos + ncomm + 1:pos + 2 * ncomm + 1]
        scratch = refs[pos + 2 * ncomm + 1:]
        if a_keep:
            scratch[0][...] = a_val.astype(BF16)
            scratch = scratch[1:]
        if comm is not None:
            sems = scratch[1:] if nk > 1 else scratch

            @pl.when((g0 == 0) & (g1 == 0) & (kk == 0))
            def _():
                comm["start"](comm_ins, comm_outs, sems)

        def finish(r):
            if has_bias:
                r = r + bias_ref[...]
            if has_res:
                r = r + res_ref[...].astype(F32)
            o_ref[...] = r.astype(o_ref.dtype)

        part = lax.dot_general(a_val.astype(BF16), b_val.astype(BF16), _DIMS[mode], preferred_element_type=F32)
        if nk == 1:
            finish(part)
        else:
            acc_ref = scratch[0]

            @pl.when(kk == 0)
            def _():
                acc_ref[...] = part

            @pl.when(kk > 0)
            def _():
                acc_ref[...] += part

            @pl.when(kk == nk - 1)
            def _():
                finish(acc_ref[...])

        if comm is not None:
            @pl.when((g0 == grid[0] - 1) & (g1 == grid[1] - 1) & (kk == nk - 1))
            def _():
                comm["finish"](comm_ins, comm_outs, sems)

    def at(f):
        return (lambda g0, g1, kk: f(g1, g0, kk)) if swap else f

    if mode == "nn":
        a_spec = pl.BlockSpec((tm, tk), at(lambda i, j, kk: (i, kk + a_off)))
        b_spec = pl.BlockSpec((tk, tn), at(lambda i, j, kk: (kk, j + b_off)))
        if b_sh is not None:
            assert b_sh % tn == 0, (name, b_sh, tn)
            per = b_sh // tn
            b_spec = pl.BlockSpec((None, tk, tn), at(lambda i, j, kk: (j // per, kk, j % per)))
    elif mode == "nt":
        a_spec = pl.BlockSpec((tm, tk), at(lambda i, j, kk: (i, kk + a_off)))
        b_spec = pl.BlockSpec((tn, tk), at(lambda i, j, kk: (j, kk + b_off)))
        if b_sh is not None:
            assert b_sh % tk == 0, (name, b_sh, tk)
            per = b_sh // tk
            b_spec = pl.BlockSpec((None, tn, tk), at(lambda i, j, kk: (kk // per, j, kk % per)))
    else:
        a_spec = pl.BlockSpec((tk, tm), at(lambda i, j, kk: (kk, i + a_off)))
        b_spec = pl.BlockSpec((tk, tn), at(lambda i, j, kk: (kk, j + b_off)))
    a_specs = [a_spec] if a_fn is None else [pl.BlockSpec(a_spec.block_shape, at(f)) for _, f in a_ops]
    b_specs = [b_spec] if b_fn is None else [pl.BlockSpec(b_spec.block_shape, at(f)) for _, f in b_ops]
    in_specs, args = a_specs + b_specs, [arr for arr, _ in a_ops] + [arr for arr, _ in b_ops]
    if has_res:
        in_specs.append(pl.BlockSpec((tm, tn), at(lambda i, j, kk: (i, j))))
        args.append(res)
    if has_bias:
        in_specs.append(pl.BlockSpec((1, tn), at(lambda i, j, kk: (0, j))))
        args.append(bias)
    out_spec, out_shape = pl.BlockSpec((tm, tn), at(lambda i, j, kk: (i, j))), (m, n)
    if o_sh is not None:
        assert o_sh % tn == 0, (name, o_sh, tn)
        per_o = o_sh // tn
        out_spec = pl.BlockSpec((None, tm, tn), at(lambda i, j, kk: (j // per_o, i, j % per_o)))
        out_shape = (n // o_sh, m, o_sh)
    scratch = [pltpu.VMEM((tm, tn), F32)] if nk > 1 else []
    if a_keep:
        assert comm is None and o_sh is None and mode != "tn" and n == tn, name
        outs = pl.pallas_call(
            body, out_shape=[jax.ShapeDtypeStruct(out_shape, out_dtype), jax.ShapeDtypeStruct((m, k), BF16)], grid=grid,
            in_specs=in_specs, out_specs=[out_spec, pl.BlockSpec((tm, tk), at(lambda i, j, kk: (i, kk)))],
            scratch_shapes=scratch, name=name, compiler_params=_params(("parallel", "parallel", "arbitrary")))(*args)
        return outs[0], outs[1]
    if comm is None:
        return pl.pallas_call(
            body, out_shape=jax.ShapeDtypeStruct(out_shape, out_dtype), grid=grid, in_specs=in_specs,
            out_specs=out_spec, scratch_shapes=scratch, name=name,
            compiler_params=_params(("parallel", "parallel", "arbitrary")))(*args)
    outs = pl.pallas_call(
        body, out_shape=[jax.ShapeDtypeStruct(out_shape, out_dtype)] + comm["out_shapes"], grid=grid,
        in_specs=in_specs + [HBM] * ncomm, out_specs=[out_spec] + [HBM] * ncomm,
        scratch_shapes=scratch + comm["sems"], name=name,
        compiler_params=_params(("arbitrary", "arbitrary", "arbitrary")))(*args, *comm["ins"])
    return outs[0], outs[1:]


def _sig(v):
    return jax.nn.sigmoid(v)


def _rms_fwd(x, g):
    r = lax.rsqrt(jnp.mean(x * x, axis=-1, keepdims=True) + EPS)
    return x * r * g


def _rms_bwd(dh, x, dres, g):
    dh = dh.astype(F32)
    r = lax.rsqrt(jnp.mean(x * x, axis=-1, keepdims=True) + EPS)
    xh = x * r
    dxh = dh * g
    dx = r * (dxh - xh * jnp.mean(dxh * xh, axis=-1, keepdims=True))
    return dres + dx, jnp.sum(dh * xh, axis=0, keepdims=True)


def _rope_apply(t, c, sa, sb):
    w = t.shape[1]
    return t * c + pltpu.roll(t, w - ROPE_HALF, 1) * sa + pltpu.roll(t, ROPE_HALF, 1) * sb


def _rope_transpose(g, c, sa, sb):
    w = g.shape[1]
    return g * c + pltpu.roll(g * sa, ROPE_HALF, 1) + pltpu.roll(g * sb, w - ROPE_HALF, 1)


def _tile_lanes(tab, reps):
    return jnp.concatenate([tab] * reps, axis=1) if reps > 1 else tab


def _rope_fwd(q, k, v, c, sa, sb):
    rq = Q_WIDTH // c.shape[1]
    qr = _rope_apply(q.astype(F32), _tile_lanes(c, rq), _tile_lanes(sa, rq), _tile_lanes(sb, rq))
    kr = _rope_apply(k.astype(F32), c, sa, sb)
    return jnp.concatenate([qr, kr, v.astype(F32)], axis=1)


def _rope_bwd_q(g, c, sa, sb):
    rq = Q_WIDTH // c.shape[1]
    return _rope_transpose(g.astype(F32), _tile_lanes(c, rq), _tile_lanes(sa, rq), _tile_lanes(sb, rq))


def _gelu(v):
    return jax.nn.gelu(v, approximate=True)


def _gelu_grad(v):
    c0 = math.sqrt(2.0 / math.pi)
    inner = c0 * (v + 0.044715 * v * v * v)
    th = jnp.tanh(inner)
    return 0.5 * (1.0 + th) + 0.5 * v * (1.0 - th * th) * c0 * (1.0 + 3 * 0.044715 * v * v)


def _merge_fwd(g0, g1, g2, b0, b1, b2, ya, ga, gb, yc):
    s0 = _sig(g0.astype(F32) + b0)
    s1 = _sig(g1.astype(F32) + b1)
    s2 = _sig(g2.astype(F32) + b2)
    ys = ga.astype(F32) * _sig(gb.astype(F32))
    return s0 * ya.astype(F32) + s1 * ys + s2 * yc.astype(F32)


def _merge_bwd(dm, g0, g1, g2, b0, b1, b2, ya, ga, gb, yc):
    dm = dm.astype(F32)
    s0 = _sig(g0.astype(F32) + b0)
    s1 = _sig(g1.astype(F32) + b1)
    s2 = _sig(g2.astype(F32) + b2)
    ga = ga.astype(F32)
    sb = _sig(gb.astype(F32))
    ys = ga * sb
    dya = dm * s0
    dys = dm * s1
    dyc = dm * s2
    dga = dys * sb
    dgb = dys * ga * sb * (1.0 - sb)
    d0 = dm * ya.astype(F32) * s0 * (1.0 - s0)
    d1 = dm * ys * s1 * (1.0 - s1)
    d2 = dm * yc.astype(F32) * s2 * (1.0 - s2)
    cs = lambda v: jnp.sum(v, axis=0, keepdims=True)
    return dya, dga, dgb, dyc, d0, d1, d2, cs(d0), cs(d1), cs(d2), cs(dga), cs(dgb)


def _ffn_act(fg, fu):
    fg = fg.astype(F32)
    return fg * _sig(fg) * fu.astype(F32)


def _ffn_act_bwd(da, fg, fu):
    da, fg, fu = da.astype(F32), fg.astype(F32), fu.astype(F32)
    s = _sig(fg)
    return da * fu * (s * (1.0 + fg * (1.0 - s))), da * fg * s


def _ple_fwd(x, gp, e):
    return x + _sig(gp.astype(F32)) * e.astype(F32)


def _ple_bwd(dx, gp, e):
    s = _sig(gp.astype(F32))
    e = e.astype(F32)
    return dx * s, dx * e * s * (1.0 - s)


def _loss_fn(x, tgt, g):
    d = x.shape[1]
    r = lax.rsqrt(jnp.mean(x * x, axis=-1, keepdims=True) + EPS)
    xh = x * r
    err = xh * g - tgt
    dy = err * (1.0 / d)
    dxh = dy * g
    dx = r * (dxh - xh * jnp.mean(dxh * xh, axis=-1, keepdims=True))
    return dx, jnp.sum(err * err, axis=0, keepdims=True) * (0.5 / d), jnp.sum(dy * xh, axis=0, keepdims=True)


def _adamw_fn(w, g, m, v):
    m = ADAM_B1 * m + (1.0 - ADAM_B1) * g
    v = ADAM_B2 * v + (1.0 - ADAM_B2) * (g * g)
    m_hat = m / (1.0 - ADAM_B1 ** ADAM_STEP)
    v_hat = v / (1.0 - ADAM_B2 ** ADAM_STEP)
    delta = -ADAM_LR * (m_hat / (jnp.sqrt(v_hat) + ADAM_EPS) + ADAM_WD * w)
    return delta, m, v


def _band_mask(n):
    qi = lax.broadcasted_iota(jnp.int32, (ATT_BLOCK, 2 * ATT_BLOCK), 0)
    kj = lax.broadcasted_iota(jnp.int32, (ATT_BLOCK, 2 * ATT_BLOCK), 1)
    dist = qi + ATT_BLOCK - kj
    return (dist >= 0) & (dist < ATT_BLOCK) & ((n > 0) | (kj >= ATT_BLOCK))


K_HEADS_AT = N_Q_HEADS // N_KV_HEADS


def _att_specs(nb):
    qs = pl.BlockSpec((N_Q_HEADS, ATT_BLOCK, HEAD_DIM), lambda b, n: (0, b * nb + n, 0))

    def kv(head_block, back):
        return pl.BlockSpec((N_KV_HEADS, ATT_BLOCK, HEAD_DIM),
                            lambda b, n: (head_block, b * nb + jnp.maximum(n - back, 0), 0))

    stat = pl.BlockSpec((N_Q_HEADS, ATT_BLOCK, 1), lambda b, n: (0, b * nb + n, 0))
    sink = pl.BlockSpec((N_Q_HEADS, 1, 1), lambda b, n: (0, 0, 0))
    return qs, [kv(K_HEADS_AT, 1), kv(K_HEADS_AT, 0), kv(K_HEADS_AT + 1, 1), kv(K_HEADS_AT + 1, 0)], stat, sink


def _attn_fwd(qkv, sinks, nbatch, seq):
    t = qkv.shape[1]
    nb = seq // ATT_BLOCK
    qs, kvs, stat, sink = _att_specs(nb)

    def body(q_ref, kp_ref, kc_ref, vp_ref, vc_ref, sink_ref, o_ref, lse_ref):
        mask = _band_mask(pl.program_id(1))
        rows = GQA_GROUP * ATT_BLOCK
        for kv in range(N_KV_HEADS):
            hs = slice(kv * GQA_GROUP, (kv + 1) * GQA_GROUP)
            kk = jnp.concatenate([kp_ref[kv], kc_ref[kv]], axis=0)
            vv = jnp.concatenate([vp_ref[kv], vc_ref[kv]], axis=0)
            q4 = (q_ref[hs] * SCALE).reshape(rows, HEAD_DIM)
            s = lax.dot_general(q4, kk, _DIMS["nt"], preferred_element_type=F32)
            s = jnp.where(mask, s.reshape(GQA_GROUP, ATT_BLOCK, 2 * ATT_BLOCK), NEG_INF)
            sk = sink_ref[hs]
            mx = jnp.maximum(jnp.max(s, axis=-1, keepdims=True), sk)
            p = jnp.exp(s - mx)
            den = jnp.sum(p, axis=-1, keepdims=True) + jnp.exp(sk - mx)
            o = lax.dot_general(p.reshape(rows, 2 * ATT_BLOCK).astype(BF16), vv, _DIMS["nn"],
                                preferred_element_type=F32).reshape(GQA_GROUP, ATT_BLOCK, HEAD_DIM)
            o_ref[hs] = (o * (1.0 / den)).astype(o_ref.dtype)
            lse_ref[hs] = mx + jnp.log(den)

    return pl.pallas_call(
        body, grid=(nbatch, nb), in_specs=[qs] + kvs + [sink], out_specs=[qs, stat],
        out_shape=[jax.ShapeDtypeStruct((N_Q_HEADS, t, HEAD_DIM), BF16), jax.ShapeDtypeStruct((N_Q_HEADS, t, 1), F32)],
        name="attn_fwd", compiler_params=_params(("parallel", "parallel")))(qkv, qkv, qkv, qkv, qkv, sinks)


def _attn_bwd(qkv, oh, doh, lse, sinks, nbatch, seq):
    t = qkv.shape[1]
    nb = seq // ATT_BLOCK
    qs, kvs, stat, sink = _att_specs(nb)

    def body(q_ref, kp_ref, kc_ref, vp_ref, vc_ref, o_ref, do_ref, lse_ref, sink_ref, dqkv_ref, dsink_ref):
        dq_ref = dqkv_ref.at[pl.ds(0, N_Q_HEADS)]
        dkc_ref, dvc_ref, dkp_ref, dvp_ref = (dqkv_ref.at[pl.ds(N_Q_HEADS + N_KV_HEADS * i, N_KV_HEADS)]
                                              for i in range(4))
        first = (pl.program_id(0) == 0) & (pl.program_id(1) == 0)

        @pl.when(first)
        def _():
            dsink_ref[...] = jnp.zeros_like(dsink_ref)

        mask = _band_mask(pl.program_id(1))
        rows = GQA_GROUP * ATT_BLOCK
        band = (GQA_GROUP, ATT_BLOCK, 2 * ATT_BLOCK)
        for kv in range(N_KV_HEADS):
            hs = slice(kv * GQA_GROUP, (kv + 1) * GQA_GROUP)
            kk = jnp.concatenate([kp_ref[kv], kc_ref[kv]], axis=0)
            vv = jnp.concatenate([vp_ref[kv], vc_ref[kv]], axis=0)
            q4 = q_ref[hs].reshape(rows, HEAD_DIM)
            do4 = do_ref[hs].reshape(rows, HEAD_DIM)
            lse4 = lse_ref[hs]
            s = lax.dot_general(q4 * SCALE, kk, _DIMS["nt"], preferred_element_type=F32).reshape(band)
            p = jnp.where(mask, jnp.exp(s - lse4), 0.0)
            dd = jnp.sum(do_ref[hs].astype(F32) * o_ref[hs].astype(F32), axis=-1, keepdims=True)
            dp = lax.dot_general(do4, vv, _DIMS["nt"], preferred_element_type=F32).reshape(band)
            ds = (p * (dp - dd) * SCALE).astype(BF16).reshape(rows, 2 * ATT_BLOCK)
            dq = lax.dot_general(ds, kk, _DIMS["nn"], preferred_element_type=F32)
            dq_ref[hs] = dq.reshape(GQA_GROUP, ATT_BLOCK, HEAD_DIM).astype(dq_ref.dtype)
            dk = lax.dot_general(ds, q4, _DIMS["tn"], preferred_element_type=F32)
            dv = lax.dot_general(p.astype(BF16).reshape(rows, 2 * ATT_BLOCK), do4, _DIMS["tn"],
                                 preferred_element_type=F32)
            dsink_ref[hs] += -jnp.sum(jnp.exp(sink_ref[hs] - lse4) * dd, axis=1, keepdims=True)
            dkp_ref[kv] = dk[:ATT_BLOCK]
            dkc_ref[kv] = dk[ATT_BLOCK:]
            dvp_ref[kv] = dv[:ATT_BLOCK]
            dvc_ref[kv] = dv[ATT_BLOCK:]

    n_out = 2 * N_Q_HEADS
    return pl.pallas_call(
        body, grid=(nbatch, nb), in_specs=[qs] + kvs + [qs, qs, stat, sink],
        out_specs=[pl.BlockSpec((n_out, ATT_BLOCK, HEAD_DIM), lambda b, n: (0, b * nb + n, 0)), sink],
        out_shape=[jax.ShapeDtypeStruct((n_out, t, HEAD_DIM), F32), jax.ShapeDtypeStruct((N_Q_HEADS, 1, 1), F32)],
        name="attn_bwd", compiler_params=_params(("arbitrary", "arbitrary")))(
        qkv, qkv, qkv, qkv, qkv, oh, doh, lse, sinks)


def _kv_combine(dqkv, c, sa, sb, seq):
    t = dqkv.shape[0]
    nb = seq // ATT_BLOCK
    nblk = t // ATT_BLOCK
    col0 = Q_WIDTH // KV_WIDTH

    def body(kc_ref, kp_ref, vc_ref, vp_ref, c_ref, sa_ref, sb_ref, dk_ref, dv_ref):
        has_next = (pl.program_id(0) % nb) != nb - 1
        dk = kc_ref[...] + jnp.where(has_next, kp_ref[...], 0.0)
        dv = vc_ref[...] + jnp.where(has_next, vp_ref[...], 0.0)
        dk_ref[...] = _rope_transpose(dk, c_ref[...], sa_ref[...], sb_ref[...]).astype(dk_ref.dtype)
        dv_ref[...] = dv.astype(dv_ref.dtype)

    cur = pl.BlockSpec((ATT_BLOCK, KV_WIDTH), lambda i: (i, 0))
    own = lambda col: pl.BlockSpec((ATT_BLOCK, KV_WIDTH), lambda i: (i, col0 + col))
    nxt = lambda col: pl.BlockSpec((ATT_BLOCK, KV_WIDTH), lambda i: (jnp.minimum(i + 1, nblk - 1), col0 + col))
    o = jax.ShapeDtypeStruct((t, KV_WIDTH), BF16)
    return pl.pallas_call(body, grid=(nblk,), in_specs=[own(0), nxt(2), own(1), nxt(3), cur, cur, cur],
                          out_specs=[cur, cur], out_shape=[o, o], name="kv_combine",
                          compiler_params=_params(("parallel",)))(dqkv, dqkv, dqkv, dqkv, c, sa, sb)


def _scan_block(ref, tab_ref, carry, ngroups, reverse):
    shifts = (7, 6, 4) if reverse else (1, 2, 4)
    n = SSM_LANES

    def step(i, car):
        g = (ngroups - 1 - i) if reverse else i
        r0 = pl.multiple_of(g * SUBLANES, SUBLANES)
        xr = ref[pl.ds(r0, SUBLANES), :n]
        xi = ref[pl.ds(r0, SUBLANES), n:]
        for s, sh in enumerate(shifts):
            pr, pi = tab_ref[2 * s], tab_ref[2 * s + 1]
            yr, yi = pltpu.roll(xr, sh, 0), pltpu.roll(xi, sh, 0)
            xr, xi = xr + pr * yr - pi * yi, xi + pr * yi + pi * yr
        cr, ci = car
        qr, qi = tab_ref[6], tab_ref[7]
        xr, xi = xr + qr * cr - qi * ci, xi + qr * ci + qi * cr
        ref[pl.ds(r0, SUBLANES), :n] = xr
        ref[pl.ds(r0, SUBLANES), n:] = xi
        last = r0 if reverse else r0 + SUBLANES - 1
        return ref[pl.ds(last, 1), :n], ref[pl.ds(last, 1), n:]

    return lax.fori_loop(0, ngroups, step, carry, unroll=2)


def _ssm_chunk(seq):
    return min(512, seq)


def _ssm_fwd(z, wb, wc, tab, dskip, nbatch, seq):
    t = z.shape[0]
    tc = _ssm_chunk(seq)
    nc = seq // tc
    n2 = 2 * SSM_LANES

    def body(u_ref, wb_ref, wc_ref, tab_ref, d_ref, st_ref, y_ref, gel_ref, car_ref):
        @pl.when(pl.program_id(1) == 0)
        def _():
            car_ref[...] = jnp.zeros_like(car_ref)

        u = u_ref[...]
        st_ref[...] = lax.dot_general(u, wb_ref[...], _DIMS["nn"], preferred_element_type=F32)
        cr, ci = _scan_block(st_ref, tab_ref, (car_ref[:, :SSM_LANES], car_ref[:, SSM_LANES:]), tc // SUBLANES, False)
        car_ref[:, :SSM_LANES] = cr
        car_ref[:, SSM_LANES:] = ci
        y = lax.dot_general(st_ref[...].astype(BF16), wc_ref[...], _DIMS["nn"], preferred_element_type=F32)
        y = y + d_ref[...] * u.astype(F32)
        y_ref[...] = y
        gel_ref[...] = _gelu(y).astype(gel_ref.dtype)

    row = lambda b, c: (b * nc + c, 0)
    full = lambda b, c: (0, 0)
    return pl.pallas_call(
        body, grid=(nbatch, nc),
        in_specs=[pl.BlockSpec((tc, SSM_WIDTH), lambda b, c: (b * nc + c, 3)), pl.BlockSpec((SSM_WIDTH, n2), full),
                  pl.BlockSpec((n2, SSM_WIDTH), full), pl.BlockSpec((8, SUBLANES, SSM_LANES), lambda b, c: (0, 0, 0)),
                  pl.BlockSpec((1, SSM_WIDTH), full)],
        out_specs=[pl.BlockSpec((tc, n2), row), pl.BlockSpec((tc, SSM_WIDTH), row), pl.BlockSpec((tc, SSM_WIDTH), row)],
        out_shape=[jax.ShapeDtypeStruct((t, n2), F32), jax.ShapeDtypeStruct((t, SSM_WIDTH), F32),
                   jax.ShapeDtypeStruct((t, SSM_WIDTH), BF16)],
        scratch_shapes=[pltpu.VMEM((1, n2), F32)], name="ssm_fwd",
        compiler_params=_params(("arbitrary", "arbitrary")))(z, wb, wc, tab, dskip)


def _ssm_bwd(dgi, ys, st, z, wbt, wct, tab_rev, dskip, nbatch, seq):
    t = z.shape[0]
    tc = _ssm_chunk(seq)
    nc = seq // tc
    n = SSM_LANES
    n2 = 2 * n
    ng = tc // SUBLANES

    def body(dgi_ref, ys_ref, st_ref, stp_ref, u_ref, wbt_ref, wct_ref, tab_ref, d_ref,
             du_ref, dwb_ref, dwc_ref, dd_ref, da_ref, p_ref, sb_ref, car_ref):
        b, c = pl.program_id(0), pl.program_id(1)
        ct = nc - 1 - c

        @pl.when((b == 0) & (c == 0))
        def _():
            dwb_ref[...] = jnp.zeros_like(dwb_ref)
            dwc_ref[...] = jnp.zeros_like(dwc_ref)
            dd_ref[...] = jnp.zeros_like(dd_ref)
            da_ref[...] = jnp.zeros_like(da_ref)

        @pl.when(c == 0)
        def _():
            car_ref[...] = jnp.zeros_like(car_ref)

        u = u_ref[...]
        dys = dgi_ref[...].astype(F32) * _gelu_grad(ys_ref[...])
        dys_b = dys.astype(BF16)
        st = st_ref[...]
        dd_ref[...] += jnp.sum(dys * u.astype(F32), axis=0, keepdims=True)
        dwc_ref[...] += lax.dot_general(st.astype(BF16), dys_b, _DIMS["tn"], preferred_element_type=F32)
        p_ref[...] = lax.dot_general(dys_b, wct_ref[...], _DIMS["nn"], preferred_element_type=F32)
        cr, ci = _scan_block(p_ref, tab_ref, (car_ref[:, :n], car_ref[:, n:]), ng, True)
        car_ref[:, :n] = cr
        car_ref[:, n:] = ci
        p = p_ref[...]
        pb = p.astype(BF16)
        dwb_ref[...] += lax.dot_general(u, pb, _DIMS["tn"], preferred_element_type=F32)
        du = lax.dot_general(pb, wbt_ref[...], _DIMS["nn"], preferred_element_type=F32) + d_ref[...] * dys
        du_ref[...] = du.astype(du_ref.dtype)
        sb_ref[pl.ds(0, SUBLANES), :] = jnp.where(ct > 0, stp_ref[...], 0.0)
        sb_ref[pl.ds(SUBLANES, tc), :] = st
        row0 = lax.broadcasted_iota(jnp.int32, (SUBLANES, n), 0) == 0

        def acc_step(g, acc):
            ar, ai = acc
            r0 = pl.multiple_of(g * SUBLANES, SUBLANES)
            edge_r = sb_ref[pl.ds(r0 + SUBLANES - 1, 1), :n]
            edge_i = sb_ref[pl.ds(r0 + SUBLANES - 1, 1), n:]
            sr = jnp.where(row0, edge_r, pltpu.roll(sb_ref[pl.ds(r0 + SUBLANES, SUBLANES), :n], 1, 0))
            si = jnp.where(row0, edge_i, pltpu.roll(sb_ref[pl.ds(r0 + SUBLANES, SUBLANES), n:], 1, 0))
            pr = p_ref[pl.ds(r0, SUBLANES), :n]
            pi = p_ref[pl.ds(r0, SUBLANES), n:]
            return ar + pr * sr + pi * si, ai + pi * sr - pr * si

        zero = jnp.zeros((SUBLANES, n), F32)
        ar, ai = lax.fori_loop(0, ng, acc_step, (zero, zero), unroll=2)
        da_ref[:, :n] += ar
        da_ref[:, n:] += ai

    row = lambda b, c: (b * nc + (nc - 1 - c), 0)
    prev8 = lambda b, c: (jnp.maximum((b * nc + (nc - 1 - c)) * (tc // SUBLANES) - 1, 0), 0)
    full = lambda b, c: (0, 0)
    return pl.pallas_call(
        body, grid=(nbatch, nc),
        in_specs=[pl.BlockSpec((tc, SSM_WIDTH), row), pl.BlockSpec((tc, SSM_WIDTH), row), pl.BlockSpec((tc, n2), row),
                  pl.BlockSpec((SUBLANES, n2), prev8),
                  pl.BlockSpec((tc, SSM_WIDTH), lambda b, c: (b * nc + (nc - 1 - c), 3)),
                  pl.BlockSpec((n2, SSM_WIDTH), full), pl.BlockSpec((SSM_WIDTH, n2), full),
                  pl.BlockSpec((8, SUBLANES, n), lambda b, c: (0, 0, 0)), pl.BlockSpec((1, SSM_WIDTH), full)],
        out_specs=[pl.BlockSpec((tc, SSM_WIDTH), row), pl.BlockSpec((SSM_WIDTH, n2), full),
                   pl.BlockSpec((n2, SSM_WIDTH), full), pl.BlockSpec((1, SSM_WIDTH), full),
                   pl.BlockSpec((SUBLANES, n2), full)],
        out_shape=[jax.ShapeDtypeStruct((t, SSM_WIDTH), BF16), jax.ShapeDtypeStruct((SSM_WIDTH, n2), F32),
                   jax.ShapeDtypeStruct((n2, SSM_WIDTH), F32), jax.ShapeDtypeStruct((1, SSM_WIDTH), F32),
                   jax.ShapeDtypeStruct((SUBLANES, n2), F32)],
        scratch_shapes=[pltpu.VMEM((tc, n2), F32), pltpu.VMEM((tc + SUBLANES, n2), F32), pltpu.VMEM((1, n2), F32)],
        name="ssm_bwd", compiler_params=_params(("arbitrary", "arbitrary")))(
        dgi, ys, st, st, z, wbt, wct, tab_rev, dskip)


def _ssm_prep(lam_re, lam_im, log_dt, b_re, b_im, c_re, c_im):
    lr = jnp.minimum(lam_re, -1e-4)
    li = lam_im
    dt = jnp.exp(log_dt)[:, None]
    mag = jnp.exp(lr * dt)
    a_re = mag * jnp.cos(li * dt)
    a_im = mag * jnp.sin(li * dt)
    den = lr * lr + li * li
    x_re, x_im = a_re - 1.0, a_im
    f_re = (x_re * lr + x_im * li) / den
    f_im = (x_im * lr - x_re * li) / den
    bb_re = f_re[..., None] * b_re - f_im[..., None] * b_im
    bb_im = f_re[..., None] * b_im + f_im[..., None] * b_re
    eye = jnp.eye(SSM_GROUPS, dtype=F32)
    emb_b = lambda v: jnp.einsum("gnh,gk->ghkn", v, eye).reshape(SSM_WIDTH, SSM_LANES)
    emb_c = lambda v: jnp.einsum("ghn,gk->gnkh", v, eye).reshape(SSM_LANES, SSM_WIDTH)
    wb = jnp.concatenate([emb_b(bb_re), emb_b(bb_im)], axis=1)
    wc = jnp.concatenate([emb_c(c_re), -emb_c(c_im)], axis=0)
    return a_re.reshape(-1), a_im.reshape(-1), wb, wc


def _ssm_tables(a_re, a_im, reverse):
    if reverse:
        a_im = -a_im
    pw = [(a_re, a_im)]
    for _ in range(SUBLANES - 1):
        pr, pi = pw[-1]
        pw.append((pr * a_re - pi * a_im, pr * a_im + pi * a_re))
    rows = jnp.arange(SUBLANES)[:, None]
    tabs = []
    for k in (1, 2, 4):
        ok = (rows + k <= SUBLANES - 1) if reverse else (rows >= k)
        tabs += [jnp.where(ok, pw[k - 1][0][None], 0.0), jnp.where(ok, pw[k - 1][1][None], 0.0)]
    order = list(range(SUBLANES - 1, -1, -1)) if reverse else list(range(SUBLANES))
    tabs += [jnp.stack([pw[i][0] for i in order]), jnp.stack([pw[i][1] for i in order])]
    return jnp.stack(tabs)


def _conv_chunk(seq):
    return min(512, seq)


def _shifted(buf, sh, tc, offsets):
    for b in range(SUBLANES):
        idx = [i for i, o in enumerate(offsets) if o % SUBLANES == b]
        if not idx:
            continue
        src = buf
        if b:
            span = tc + SUBLANES * max(offsets[i] // SUBLANES for i in idx)
            sh[pl.ds(0, span), :] = buf[pl.ds(b, span), :]
            src = sh
        for i in idx:
            yield i, src[pl.ds(offsets[i] // SUBLANES * SUBLANES, tc), :]


def _conv_fwd(z, w, bias, lg, lb, nbatch, seq):
    t = z.shape[0]
    tc = _conv_chunk(seq)
    nc = seq // tc

    def body(a_ref, g_ref, w_ref, b_ref, lg_ref, lb_ref, cv_ref, sc_ref, ubuf, sh):
        c = pl.program_id(1)

        @pl.when(c == 0)
        def _():
            ubuf[pl.ds(0, CONV_HALO), :] = jnp.zeros((CONV_HALO, CONV_WIDTH), F32)

        @pl.when(c > 0)
        def _():
            ubuf[pl.ds(0, CONV_HALO), :] = ubuf[pl.ds(tc, CONV_HALO), :]

        ubuf[pl.ds(CONV_HALO, tc), :] = a_ref[...].astype(F32) * _sig(g_ref[...].astype(F32))
        acc = jnp.zeros((tc, CONV_WIDTH), F32) + b_ref[...]
        for k, win in _shifted(ubuf, sh, tc, [CONV_HALO - (CONV_K - 1) + k for k in range(CONV_K)]):
            acc = acc + w_ref[pl.ds(k, 1), :] * win
        cv_ref[...] = acc
        mu = jnp.mean(acc, axis=-1, keepdims=True)
        xc = acc - mu
        y = xc * lax.rsqrt(jnp.mean(xc * xc, axis=-1, keepdims=True) + EPS) * lg_ref[...] + lb_ref[...]
        sc_ref[...] = (y * _sig(y)).astype(sc_ref.dtype)

    row = lambda b, c: (b * nc + c, 0)
    full = lambda b, c: (0, 0)
    vec = pl.BlockSpec((1, CONV_WIDTH), full)
    return pl.pallas_call(
        body, grid=(nbatch, nc),
        in_specs=[pl.BlockSpec((tc, CONV_WIDTH), lambda b, c: (b * nc + c, 4)),
                  pl.BlockSpec((tc, CONV_WIDTH), lambda b, c: (b * nc + c, 5)),
                  pl.BlockSpec((CONV_HALO, CONV_WIDTH), full), vec, vec, vec],
        out_specs=[pl.BlockSpec((tc, CONV_WIDTH), row), pl.BlockSpec((tc, CONV_WIDTH), row)],
        out_shape=[jax.ShapeDtypeStruct((t, CONV_WIDTH), F32), jax.ShapeDtypeStruct((t, CONV_WIDTH), BF16)],
        scratch_shapes=[pltpu.VMEM((CONV_HALO + tc, CONV_WIDTH), F32)] * 2, name="conv_fwd",
        compiler_params=_params(("arbitrary", "arbitrary")))(z, z, w, bias, lg, lb)


def _conv_bwd(dsc, cv, z, w, lg, lb, nbatch, seq):
    t = z.shape[0]
    tc = _conv_chunk(seq)
    nc = seq // tc
    hb = tc // CONV_HALO

    def body(dsc_ref, cv_ref, a_ref, g_ref, ap_ref, gp_ref, w_ref, lg_ref, lb_ref,
             da_ref, dg_ref, dw_ref, db_ref, dlg_ref, dlb_ref, ubuf, dbuf, sh):
        b, c = pl.program_id(0), pl.program_id(1)
        ct = nc - 1 - c

        @pl.when((b == 0) & (c == 0))
        def _():
            dw_ref[...] = jnp.zeros_like(dw_ref)
            db_ref[...] = jnp.zeros_like(db_ref)
            dlg_ref[...] = jnp.zeros_like(dlg_ref)
            dlb_ref[...] = jnp.zeros_like(dlb_ref)

        cvv = cv_ref[...]
        mu = jnp.mean(cvv, axis=-1, keepdims=True)
        xc = cvv - mu
        rstd = lax.rsqrt(jnp.mean(xc * xc, axis=-1, keepdims=True) + EPS)
        xh = xc * rstd
        y = xh * lg_ref[...] + lb_ref[...]
        sy = _sig(y)
        dy = dsc_ref[...].astype(F32) * (sy * (1.0 + y * (1.0 - sy)))
        dlg_ref[...] += jnp.sum(dy * xh, axis=0, keepdims=True)
        dlb_ref[...] += jnp.sum(dy, axis=0, keepdims=True)
        dxh = dy * lg_ref[...]
        dcv = rstd * (dxh - jnp.mean(dxh, axis=-1, keepdims=True) - xh * jnp.mean(dxh * xh, axis=-1, keepdims=True))
        db_ref[...] += jnp.sum(dcv, axis=0, keepdims=True)

        @pl.when(c == 0)
        def _():
            dbuf[pl.ds(tc, CONV_HALO), :] = jnp.zeros((CONV_HALO, CONV_WIDTH), F32)

        @pl.when(c > 0)
        def _():
            dbuf[pl.ds(tc, CONV_HALO), :] = dbuf[pl.ds(0, CONV_HALO), :]

        dbuf[pl.ds(0, tc), :] = dcv
        a = a_ref[...].astype(F32)
        sg = _sig(g_ref[...].astype(F32))
        ubuf[pl.ds(0, CONV_HALO), :] = jnp.where(ct > 0, ap_ref[...].astype(F32) * _sig(gp_ref[...].astype(F32)), 0.0)
        ubuf[pl.ds(CONV_HALO, tc), :] = a * sg
        du = jnp.zeros((tc, CONV_WIDTH), F32)
        for k, win in _shifted(dbuf, sh, tc, [CONV_K - 1 - k for k in range(CONV_K)]):
            du = du + w_ref[pl.ds(k, 1), :] * win
        for k, win in _shifted(ubuf, sh, tc, [CONV_HALO - (CONV_K - 1) + k for k in range(CONV_K)]):
            dw_ref[pl.ds(k, 1), :] += jnp.sum(dcv * win, axis=0, keepdims=True)
        da_ref[...] = (du * sg).astype(da_ref.dtype)
        dg_ref[...] = (du * a * sg * (1.0 - sg)).astype(dg_ref.dtype)

    row = lambda b, c: (b * nc + (nc - 1 - c), 0)
    full = lambda b, c: (0, 0)
    vec = pl.BlockSpec((1, CONV_WIDTH), full)
    blk = pl.BlockSpec((tc, CONV_WIDTH), row)

    def zcol(col):
        return pl.BlockSpec((tc, CONV_WIDTH), lambda b, c: (b * nc + (nc - 1 - c), col))

    def zprev(col):
        return pl.BlockSpec((CONV_HALO, CONV_WIDTH),
                            lambda b, c: (jnp.maximum((b * nc + (nc - 1 - c)) * hb - 1, 0), col))

    o = jax.ShapeDtypeStruct((t, CONV_WIDTH), BF16)
    v = jax.ShapeDtypeStruct((1, CONV_WIDTH), F32)
    return pl.pallas_call(
        body, grid=(nbatch, nc),
        in_specs=[blk, blk, zcol(4), zcol(5), zprev(4), zprev(5), pl.BlockSpec((CONV_HALO, CONV_WIDTH), full), vec, vec],
        out_specs=[blk, blk, pl.BlockSpec((CONV_HALO, CONV_WIDTH), full), vec, vec, vec],
        out_shape=[o, o, jax.ShapeDtypeStruct((CONV_HALO, CONV_WIDTH), F32), v, v, v],
        scratch_shapes=[pltpu.VMEM((CONV_HALO + tc, CONV_WIDTH), F32)] * 3, name="conv_bwd", compiler_params=_params(("arbitrary", "arbitrary")))(dsc, cv, z, z, z, z, w, lg, lb)


BIG = ("w_in", "w_attn_out", "w_ssm_glu", "w_conv_out", "w_mix_out", "w_ffn_in", "w_ffn_out", "w_ple_in", "w_ple_gate")
BIG_AXIS = {"w_in": 2, "w_attn_out": 2, "w_ssm_glu": 2, "w_conv_out": 2, "w_mix_out": 1, "w_ffn_in": 2,
            "w_ffn_out": 1, "w_ple_in": 2, "w_ple_gate": 1}
SHARD_MAJOR = ("w_in", "w_ffn_in")
SMALL = ("mix_norm_g", "b_gate", "attn_sinks", "ssm_lambda_re", "ssm_lambda_im", "ssm_log_dt", "ssm_b_re", "ssm_b_im",
         "ssm_c_re", "ssm_c_im", "ssm_d", "b_ssm_glu", "conv_dw_w", "conv_dw_b", "conv_norm_g", "conv_norm_b",
         "ffn_norm_g", "ple_norm_g", "final_norm_g")
WEIGHTS = ("mix_norm_g", "w_in", "b_gate", "attn_sinks", "w_attn_out", "ssm_lambda_re", "ssm_lambda_im", "ssm_log_dt",
           "ssm_b_re", "ssm_b_im", "ssm_c_re", "ssm_c_im", "ssm_d", "w_ssm_glu", "b_ssm_glu", "conv_dw_w", "conv_dw_b",
           "conv_norm_g", "conv_norm_b", "w_conv_out", "w_mix_out", "ffn_norm_g", "w_ffn_in", "w_ffn_out", "w_ple_in",
           "ple_norm_g", "w_ple_gate", "final_norm_g")
SSM_NAMES = ("ssm_lambda_re", "ssm_lambda_im", "ssm_log_dt", "ssm_b_re", "ssm_b_im", "ssm_c_re", "ssm_c_im")


def _heads(v, nh):
    return v.reshape(v.shape[0], nh, HEAD_DIM).transpose(1, 0, 2)


def _tokens(v):
    return v.transpose(1, 0, 2).reshape(v.shape[1], v.shape[0] * HEAD_DIM)


def _row(v):
    return v.reshape(1, -1)


def _layer_fwd(x, p_l, w, s, rope, nbatch, seq, next_shards=None):
    t = x.shape[0]
    tm = 512
    d = D_MODEL
    sv = {}
    sv["x"] = x
    h = _rowwise("rms_mix", _rms_fwd, [R(x), V(_row(s["mix_norm_g"]))], [O(d, BF16)], tm=tm)
    cs = {nm: w[nm].shape[2] for nm in SHARD_MAJOR}
    tb = 1024
    got = {}
    plan = None if next_shards is None else _gather_plan(next_shards, GATHER_A)
    z = _mm("mm_in", h, w["w_in"], "nn", BF16, m=t, n=N_CHIPS * cs["w_in"], k=d, tm=tb, tn=cs["w_in"], tk=d,
            b_sh=cs["w_in"], comm=plan)
    if plan is not None:
        z, outs = z
        got.update(zip(plan["names"], outs))
    sv["h"], sv["z"] = h, z
    c, sa, sb = rope
    qkv_w = Q_WIDTH + 2 * KV_WIDTH
    qkv = _rowwise("rope_fwd", _rope_fwd, [R(z, Q_WIDTH, 0), R(z, KV_WIDTH, 4), R(z, KV_WIDTH, 5), R(c), R(sa), R(sb)],
                   [O(qkv_w, BF16)], tm=tm)
    qkv = _heads(qkv, qkv_w // HEAD_DIM)
    sinks = s["attn_sinks"].reshape(N_Q_HEADS, 1, 1)
    oh, lse = _attn_fwd(qkv, sinks, nbatch, seq)
    o = _tokens(oh)
    ya = _mm("mm_attn_out", o, w["w_attn_out"], "nn", BF16, m=t, n=d, k=Q_WIDTH, tm=tb, tn=d, tk=Q_WIDTH)
    sv.update(qkv=qkv, oh=oh, lse=lse, o=o, ya=ya, sinks=sinks)
    ssm_args = [s[nm] for nm in SSM_NAMES]
    a_re, a_im, wb, wc = _ssm_prep(*ssm_args)
    dskip = _row(s["ssm_d"])
    st, ys, gel = _ssm_fwd(z, wb.astype(BF16), wc.astype(BF16), _ssm_tables(a_re, a_im, False), dskip, nbatch, seq)
    glu = _mm("mm_glu", gel, w["w_ssm_glu"], "nn", BF16, m=t, n=2 * d, k=SSM_WIDTH, tm=tb, tn=2 * d, tk=SSM_WIDTH,
              bias=_row(s["b_ssm_glu"]))
    sv.update(st=st, ys=ys, gel=gel, glu=glu, a=(a_re, a_im), wb=wb, wc=wc, dskip=dskip)
    cw = jnp.pad(s["conv_dw_w"], ((0, CONV_HALO - CONV_K), (0, 0)))
    cv, sc = _conv_fwd(z, cw, _row(s["conv_dw_b"]), _row(s["conv_norm_g"]), _row(s["conv_norm_b"]), nbatch, seq)
    yc = _mm("mm_conv_out", sc, w["w_conv_out"], "nn", BF16, m=t, n=d, k=CONV_WIDTH, tm=tb, tn=d, tk=CONV_WIDTH)
    sv.update(cw=cw, cv=cv, sc=sc, yc=yc)
    bg = _row(s["b_gate"])
    merge_ins = [R(z, 512, 3), R(z, 512, 5), R(z, 512, 7), V(bg, 512, 0), V(bg, 512, 2), V(bg, 512, 4),
                 R(ya, 512, 0), R(glu, 512, 0), R(glu, 512, 2), R(yc, 512, 0)]
    merged = _rowwise("merge_fwd", _merge_fwd, merge_ins, [O(512, BF16, total=d)], tm=tm, ncol=2)
    x1 = _mm("mm_mix", merged, w["w_mix_out"], "nn", F32, m=t, n=d, k=d, tm=tb, tn=d, tk=d, res=x)
    sv.update(merged=merged, x1=x1)
    hf = _rowwise("rms_ffn", _rms_fwd, [R(x1), V(_row(s["ffn_norm_g"]))], [O(d, BF16)], tm=tm)
    plan = None if next_shards is None else _gather_plan(next_shards, GATHER_B)
    f = _mm("mm_ffn_in", hf, w["w_ffn_in"], "nn", BF16, m=t, n=2 * FFN_HIDDEN, k=d, tm=tb, tn=cs["w_ffn_in"], tk=d,
            b_sh=cs["w_ffn_in"], comm=plan)
    if plan is not None:
        f, outs = f
        got.update(zip(plan["names"], outs))
    act = (lambda i, j, kk, fg, fu: _ffn_act(fg, fu), [(f, lambda i, j, kk: (i, 0)), (f, lambda i, j, kk: (i, 1))])
    x2, act = _mm("mm_ffn_out", act, w["w_ffn_out"], "nn", F32, m=t, n=d, k=FFN_HIDDEN, tm=256, tn=d, tk=FFN_HIDDEN,
                  res=x1, a_keep=True)
    sv.update(hf=hf, f=f, act=act, x2=x2)
    e = _mm("mm_ple_in", p_l, w["w_ple_in"], "nn", BF16, m=t, n=d, k=p_l.shape[1], tm=tb, tn=d, tk=p_l.shape[1])
    hp = _rowwise("rms_ple", _rms_fwd, [R(x2), V(_row(s["ple_norm_g"]))], [O(d, BF16)], tm=tm)
    gp = _mm("mm_ple_gate", hp, w["w_ple_gate"], "nn", BF16, m=t, n=d, k=d, tm=tb, tn=d, tk=d)
    x3 = _rowwise("ple_fwd", _ple_fwd, [R(x2), R(gp), R(e)], [O(d, F32)], tm=tm)
    sv.update(e=e, hp=hp, gp=gp, p=p_l)
    return x3, sv, got


def _layer_bwd(dx3, sv, w, s, rope, nbatch, seq):
    t = dx3.shape[0]
    tm = 512
    d = D_MODEL
    gb, gs = {}, {}
    cs = {nm: w[nm].shape[2] for nm in SHARD_MAJOR}
    tb = 1024

    def wg(name, a, b, m, n, tm=1024, tk=1024, shard=None):
        return _mm(name, a, b, "tn", BF16, m=m, n=n, k=t, tm=tm, tn=n if shard is None else cs[shard], tk=tk,
                   o_sh=None if shard is None else cs[shard])

    de, dgp = _rowwise("ple_bwd", _ple_bwd, [R(dx3), R(sv["gp"]), R(sv["e"])], [O(d, BF16), O(d, BF16)], tm=tm)
    gb["w_ple_in"] = wg("wg_ple_in", sv["p"], de, sv["p"].shape[1], d, tk=2048)
    gb["w_ple_gate"] = wg("wg_ple_gate", sv["hp"], dgp, d, d, tk=2048)
    dhp = _mm("mmb_ple_gate", dgp, w["w_ple_gate"], "nt", BF16, m=t, n=d, k=d, tm=tb, tn=d, tk=d)
    dx2, gs["ple_norm_g"] = _rowwise("rms_ple_bwd", _rms_bwd, [R(dhp), R(sv["x2"]), R(dx3), V(_row(s["ple_norm_g"]))],
                                     [O(d, F32)], [A(d)], tm=tm)
    fw = FFN_HIDDEN // 2
    dact = _mm("mmb_ffn_out", dx2, w["w_ffn_out"], "nt", BF16, m=t, n=FFN_HIDDEN, k=d, tm=tb, tn=fw, tk=d)
    gb["w_ffn_out"] = wg("wg_ffn_out", sv["act"], dx2, FFN_HIDDEN, d, tm=fw)
    f = sv["f"]

    def df_tile(is_gate, da, fg, fu):
        dfg, dfu = _ffn_act_bwd(da, fg, fu)
        return jnp.where(is_gate, dfg, dfu)

    assert cs["w_ffn_in"] == fw
    df_rows = (lambda i, j, kk, *v: df_tile(kk < 2, *v),
               [(dact, lambda i, j, kk: (i, kk % 2)), (f, lambda i, j, kk: (i, kk % 2)), (f, lambda i, j, kk: (i, 2 + kk % 2))])
    dhf, df = _mm("mmb_ffn_in", df_rows, w["w_ffn_in"], "nt", BF16, m=t, n=d, k=2 * FFN_HIDDEN, tm=512, tn=d, tk=fw,
                  b_sh=fw, a_keep=True)
    gb["w_ffn_in"] = wg("wg_ffn_in", sv["hf"], df, d, 2 * FFN_HIDDEN, shard="w_ffn_in")
    dx1, gs["ffn_norm_g"] = _rowwise("rms_ffn_bwd", _rms_bwd, [R(dhf), R(sv["x1"]), R(dx2), V(_row(s["ffn_norm_g"]))],
                                     [O(d, F32)], [A(d)], tm=tm)
    dm = _mm("mmb_mix", dx1, w["w_mix_out"], "nt", BF16, m=t, n=d, k=d, tm=tb, tn=d, tk=d)
    gb["w_mix_out"] = wg("wg_mix", sv["merged"], dx1, d, d)
    z, glu, bg = sv["z"], sv["glu"], _row(s["b_gate"])
    ins = [R(dm, 512, 0), R(z, 512, 3), R(z, 512, 5), R(z, 512, 7), V(bg, 512, 0), V(bg, 512, 2), V(bg, 512, 4),
           R(sv["ya"], 512, 0), R(glu, 512, 0), R(glu, 512, 2), R(sv["yc"], 512, 0)]
    ob = lambda: O(512, BF16, total=d)
    ab = lambda: A(512, total=d)
    dya, dga, dgb, dyc, d0, d1, d2, db0, db1, db2, dba, dbb = _rowwise(
        "merge_bwd", _merge_bwd, ins, [ob() for _ in range(7)], [ab() for _ in range(5)], tm=tm, ncol=2)
    gs["b_gate"] = jnp.concatenate([db0, db1, db2], axis=1)
    gs["b_ssm_glu"] = jnp.concatenate([dba, dbb], axis=1)
    dglu = jnp.concatenate([dga, dgb], axis=1)
    gb["w_attn_out"] = wg("wg_attn_out", sv["o"], dya, Q_WIDTH, d, tk=2048)
    do = _mm("mmb_attn_out", dya, w["w_attn_out"], "nt", BF16, m=t, n=Q_WIDTH, k=d, tm=tb, tn=Q_WIDTH, tk=d)
    dqkv, dsink = _attn_bwd(sv["qkv"], sv["oh"], _heads(do, N_Q_HEADS), sv["lse"], sv["sinks"], nbatch, seq)
    dqkv = _tokens(dqkv)
    gs["attn_sinks"] = dsink.reshape(-1)
    c, sa, sb = rope
    dq = _rowwise("rope_bwd_q", _rope_bwd_q, [R(dqkv, Q_WIDTH, 0), R(c), R(sa), R(sb)], [O(Q_WIDTH, BF16)], tm=tm)
    dk, dv = _kv_combine(dqkv, c, sa, sb, seq)
    gb["w_ssm_glu"] = wg("wg_ssm_glu", sv["gel"], dglu, SSM_WIDTH, 2 * d, tk=2048)
    dgi = _mm("mmb_glu", dglu, w["w_ssm_glu"], "nt", BF16, m=t, n=SSM_WIDTH, k=2 * d, tm=tb, tn=SSM_WIDTH, tk=2 * d)
    a_re, a_im = sv["a"]
    du, dwb, dwc, dd, da = _ssm_bwd(dgi, sv["ys"], sv["st"], z, sv["wb"].T.astype(BF16), sv["wc"].T.astype(BF16),
                                    _ssm_tables(a_re, a_im, True), sv["dskip"], nbatch, seq)
    gs["ssm_d"] = dd.reshape(-1)
    da = jnp.sum(da, axis=0)
    _, prep_vjp = jax.vjp(_ssm_prep, *[s[nm] for nm in SSM_NAMES])
    for nm, g in zip(SSM_NAMES, prep_vjp((da[:SSM_LANES], da[SSM_LANES:], dwb, dwc))):
        gs[nm] = g
    gb["w_conv_out"] = wg("wg_conv_out", sv["sc"], dyc, CONV_WIDTH, d, tk=2048)
    dsc = _mm("mmb_conv_out", dyc, w["w_conv_out"], "nt", BF16, m=t, n=CONV_WIDTH, k=d, tm=tb, tn=CONV_WIDTH, tk=d)
    dca, dcg, dcw, dcb, dlg, dlb = _conv_bwd(dsc, sv["cv"], z, sv["cw"], _row(s["conv_norm_g"]),
                                             _row(s["conv_norm_b"]), nbatch, seq)
    gs["conv_dw_w"] = dcw[:CONV_K]
    gs["conv_dw_b"], gs["conv_norm_g"], gs["conv_norm_b"] = dcb.reshape(-1), dlg.reshape(-1), dlb.reshape(-1)
    dz = jnp.concatenate([dq, dk, dv, du, dca, dcg, d0, d1, d2], axis=1)
    gb["w_in"] = wg("wg_in", sv["h"], dz, d, dz.shape[1], tk=2048, shard="w_in")
    dh = _mm("mmb_in", dz, w["w_in"], "nt", BF16, m=t, n=d, k=dz.shape[1], tm=tb, tn=d, tk=cs["w_in"],
             b_sh=cs["w_in"])
    dx, gs["mix_norm_g"] = _rowwise("rms_mix_bwd", _rms_bwd, [R(dh), R(sv["x"]), R(dx1), V(_row(s["mix_norm_g"]))],
                                    [O(d, F32)], [A(d)], tm=tm)
    gs["mix_norm_g"], gs["ffn_norm_g"], gs["ple_norm_g"] = (gs[nm].reshape(-1) for nm in
                                                            ("mix_norm_g", "ffn_norm_g", "ple_norm_g"))
    gs["b_gate"], gs["b_ssm_glu"] = gs["b_gate"].reshape(-1), gs["b_ssm_glu"].reshape(-1)
    return dx, {nm: _shard_major(nm, g) for nm, g in gb.items()}, gs


def _rope_tables(positions):
    inv_freq = ROPE_THETA ** (-jnp.arange(0, ROPE_DIM, 2, dtype=F32) / ROPE_DIM)
    ang = positions.reshape(-1).astype(F32)[:, None] * inv_freq
    cos, sin = jnp.cos(ang), jnp.sin(ang)
    t = ang.shape[0]
    rest = HEAD_DIM - ROPE_DIM
    c = jnp.concatenate([cos, cos, jnp.ones((t, rest), F32)], axis=1)
    sa = jnp.concatenate([-sin, jnp.zeros((t, HEAD_DIM - ROPE_HALF), F32)], axis=1)
    sb = jnp.concatenate([jnp.zeros((t, ROPE_HALF), F32), sin, jnp.zeros((t, rest), F32)], axis=1)
    two = lambda v: jnp.concatenate([v, v], axis=1)
    return two(c), two(sa), two(sb)


def _natural(nm, w4):
    if nm in SHARD_MAJOR:
        return w4
    if BIG_AXIS[nm] == 1:
        return w4.reshape(-1, w4.shape[2])
    return w4.transpose(1, 0, 2).reshape(w4.shape[1], -1)


def _shard_major(nm, g):
    if nm in SHARD_MAJOR:
        return g
    if BIG_AXIS[nm] == 1:
        return g.reshape(N_CHIPS, -1, g.shape[1])
    return g.reshape(g.shape[0], N_CHIPS, -1).transpose(1, 0, 2)


def _untap(taps4, cols):
    flat = taps4.reshape(N_CHIPS, -1)[:, :CONV_K * cols]
    return flat.reshape(N_CHIPS, CONV_K, cols).transpose(1, 0, 2).reshape(CONV_K, N_CHIPS * cols)


def _local_step(x, p, positions, loss_target, small, wfull=None, shards=None):
    nbatch, seq, d = x.shape
    depth = p.shape[0]
    t = nbatch * seq
    rope = _rope_tables(positions)
    xs = x.reshape(t, d)
    saved, ws, ss = [], [], []
    got = None if shards is None else _gather_now(shards[0])
    for l in range(depth):
        w4 = {nm: wfull[nm][l] for nm in BIG} if shards is None else got
        w_l = {nm: _natural(nm, w4[nm]) for nm in BIG}
        s_l = {nm: small[nm][l] for nm in small if nm != "final_norm_g"}
        if shards is not None:
            s_l["conv_dw_w"] = _untap(got[TAPS], CONV_WIDTH // N_CHIPS)
        nxt = shards[l + 1] if shards is not None and l + 1 < depth else None
        xs, sv, got = _layer_fwd(xs, p[l].reshape(t, -1), w_l, s_l, rope, nbatch, seq, nxt)
        saved.append(sv)
        ws.append(w_l)
        ss.append(s_l)
    dx, loss_cols, dgf = _rowwise("loss_head", _loss_fn, [R(xs), R(loss_target.reshape(t, d)),
                                                          V(_row(small["final_norm_g"]))],
                                  [O(d, F32)], [A(d), A(d)], tm=512)
    gbs, gss = [None] * depth, [None] * depth
    for l in reversed(range(depth)):
        dx, gbs[l], gss[l] = _layer_bwd(dx, saved[l], ws[l], ss[l], rope, nbatch, seq)
    gbig = {nm: jnp.stack([g[nm] for g in gbs]) for nm in BIG}
    gsmall = {nm: jnp.stack([g[nm] for g in gss]) for nm in SMALL if nm != "final_norm_g"}
    gsmall["final_norm_g"] = dgf.reshape(-1)
    return loss_cols, dx.reshape(nbatch, seq, d), gbig, gsmall


HBM = pl.BlockSpec(memory_space=pltpu.HBM)


def _place():
    x, y, c = lax.axis_index("x"), lax.axis_index("y"), lax.axis_index("c")
    chips = [(1 - x, y), (x, 1 - y), (1 - x, 1 - y)]
    return x, y, c, chips


def _remote(src, dst, send_sem, recv_sem, to):
    return pltpu.make_async_remote_copy(src_ref=src, dst_ref=dst, send_sem=send_sem, recv_sem=recv_sem,
                                        device_id=to, device_id_type=MESH)


TAPS = "taps"
GATHER_ALL = (("w_ffn_in", "w_ffn_out"),
              ("w_in", "w_ple_gate", "w_mix_out", "w_attn_out", "w_ssm_glu", "w_conv_out", "w_ple_in", TAPS))
GATHER_A = (("w_ffn_in",), ("w_in", "w_ple_gate"))
GATHER_B = (("w_ffn_out",), ("w_mix_out", "w_attn_out", "w_ssm_glu", "w_conv_out", "w_ple_in", TAPS))


def _gather_plan(shards, sets):
    names = sets[0] + sets[1]
    n = len(names)
    idx = {nm: i for i, nm in enumerate(names)}

    def start(ins, outs, sems):
        send1, recv1, _, _, send0, recv0 = sems
        x, y, c, chips = _place()
        me = 2 * x + y
        for i in range(n):
            _remote(ins[i], outs[i].at[me], send0.at[i], recv0.at[i], (x, y, 1 - c)).start()
        for role in (0, 1):
            @pl.when(c == role)
            def _():
                for nm in sets[role]:
                    i = idx[nm]
                    for k, (cx, cy) in enumerate(chips):
                        _remote(ins[i], outs[i].at[me], send1.at[i, k], recv1.at[i, k], (cx, cy, c)).start()

    def finish(ins, outs, sems):
        send1, recv1, send2, recv2, send0, recv0 = sems
        x, y, c, chips = _place()
        me = 2 * x + y
        sib = (x, y, 1 - c)
        for role in (0, 1):
            @pl.when(c == role)
            def _():
                passed = []
                for nm in sets[role]:
                    i = idx[nm]
                    for k, (cx, cy) in enumerate(chips):
                        slot = outs[i].at[2 * cx + cy]
                        _remote(slot, slot, send1.at[i, k], recv1.at[i, k], (cx, cy, c)).wait_recv()
                        cp = _remote(slot, slot, send2.at[i, k], recv2.at[i, k], sib)
                        cp.start()
                        passed.append(cp)
                for nm in sets[1 - role]:
                    i = idx[nm]
                    for k, (cx, cy) in enumerate(chips):
                        slot = outs[i].at[2 * cx + cy]
                        _remote(slot, slot, send2.at[i, k], recv2.at[i, k], sib).wait_recv()
                for nm in sets[role]:
                    i = idx[nm]
                    for k, (cx, cy) in enumerate(chips):
                        _remote(ins[i], outs[i].at[me], send1.at[i, k], recv1.at[i, k], (cx, cy, c)).wait_send()
                for cp in passed:
                    cp.wait_send()
        for i in range(n):
            _remote(ins[i], outs[i].at[me], send0.at[i], recv0.at[i], sib).wait()

    ins = [shards[nm] for nm in names]
    return dict(names=names, ins=ins, start=start, finish=finish,
                out_shapes=[jax.ShapeDtypeStruct((N_CHIPS,) + v.shape, v.dtype) for v in ins],
                sems=[pltpu.SemaphoreType.DMA((n, 3)) for _ in range(4)] + [pltpu.SemaphoreType.DMA((n,))
                                                                            for _ in range(2)])


def _gather_now(shards):
    return _comm_now("gather_weights", _gather_plan(shards, GATHER_ALL))


def _pair_exchange(grads):
    n = len(grads)
    hl = grads[0].shape[0] // 2

    def body(*refs):
        ins, outs = refs[:n], refs[n:2 * n]
        send, recv = refs[2 * n:]
        x, y, c, _ = _place()
        other = pl.ds((1 - c) * hl, hl)
        cps = [_remote(ins[i].at[other], outs[i], send.at[i], recv.at[i], (x, y, 1 - c)) for i in range(n)]
        for cp in cps:
            cp.start()
        for cp in cps:
            cp.wait()

    out_shape = [jax.ShapeDtypeStruct((hl,) + g.shape[1:], g.dtype) for g in grads]
    sems = [pltpu.SemaphoreType.DMA((n,)) for _ in range(2)]
    return pl.pallas_call(body, out_shape=out_shape, in_specs=[HBM] * n, out_specs=[HBM] * n, scratch_shapes=sems,
                          name="reduce_pair_exchange")(*grads)


def _pair_add(g, r):
    hl, _, rr, cc = r.shape
    rows = hl * N_CHIPS * rr
    nblk = rows // rr

    def body(c_ref, g_ref, r_ref, o_ref):
        o_ref[...] = (g_ref[...].astype(F32) + r_ref[...].astype(F32)).astype(o_ref.dtype)

    grid_spec = pltpu.PrefetchScalarGridSpec(
        num_scalar_prefetch=1, grid=(nblk,),
        in_specs=[pl.BlockSpec((rr, cc), lambda i, c_ref: (c_ref[0] * nblk + i, 0)),
                  pl.BlockSpec((rr, cc), lambda i, c_ref: (i, 0))],
        out_specs=pl.BlockSpec((rr, cc), lambda i, c_ref: (i, 0)))
    c = lax.axis_index("c").astype(jnp.int32).reshape(1)
    out = pl.pallas_call(body, out_shape=jax.ShapeDtypeStruct((rows, cc), r.dtype), grid_spec=grid_spec,
                         name="reduce_pair_add", compiler_params=_params(("parallel",)))(
        c, g.reshape(-1, cc), r.reshape(rows, cc))
    return out.reshape(r.shape)


def _chip_exchange(psums):
    n = len(psums)

    def body(*refs):
        ins, got = refs[:n], refs[n:2 * n]
        send, recv = refs[2 * n:]
        x, y, c, chips = _place()
        cps = [_remote(ins[i].at[:, 2 * cx + cy], got[i].at[k], send.at[i, k], recv.at[i, k], (cx, cy, c))
               for i in range(n) for k, (cx, cy) in enumerate(chips)]
        for cp in cps:
            cp.start()
        for cp in cps:
            cp.wait()

    got_shape = [jax.ShapeDtypeStruct((3, p.shape[0]) + p.shape[2:], p.dtype) for p in psums]
    sems = [pltpu.SemaphoreType.DMA((n, 3)), pltpu.SemaphoreType.DMA((n, 3))]
    return pl.pallas_call(body, out_shape=got_shape, in_specs=[HBM] * n, out_specs=[HBM] * n, scratch_shapes=sems,
                          name="reduce_chip_exchange")(*psums)


def _comm_now(name, plan):
    n = len(plan["ins"])

    def body(*refs):
        ins, outs, sems = refs[:n], refs[n:2 * n], refs[2 * n:]
        plan["start"](ins, outs, sems)
        plan["finish"](ins, outs, sems)

    outs = pl.pallas_call(body, out_shape=plan["out_shapes"], in_specs=[HBM] * n, out_specs=[HBM] * n,
                          scratch_shapes=plan["sems"], name=name)(*plan["ins"])
    return dict(zip(plan["names"], outs))


def _sum4(psum, got):
    hl, _, rr, cc = psum.shape
    tr = rr if rr * cc <= 512 * 1024 else rr // 2

    def body(place_ref, own_ref, g0_ref, g1_ref, g2_ref, o_ref):
        tot = (own_ref[...].astype(F32) + g0_ref[...].astype(F32)) + g1_ref[...].astype(F32)
        o_ref[...] = tot + g2_ref[...].astype(F32)

    def got_spec(k):
        return pl.BlockSpec((None, None, tr, cc), lambda h, i, place: (k, h, i, 0))

    grid_spec = pltpu.PrefetchScalarGridSpec(
        num_scalar_prefetch=1, grid=(hl, rr // tr),
        in_specs=[pl.BlockSpec((None, None, tr, cc), lambda h, i, place: (h, place[0], i, 0)),
                  got_spec(0), got_spec(1), got_spec(2)],
        out_specs=pl.BlockSpec((None, tr, cc), lambda h, i, place: (place[1] * hl + h, i, 0)))
    place = jnp.stack([2 * lax.axis_index("x") + lax.axis_index("y"), lax.axis_index("c")]).astype(jnp.int32)
    return pl.pallas_call(body, out_shape=jax.ShapeDtypeStruct((2 * hl, rr, cc), F32), grid_spec=grid_spec,
                          name="reduce_sum4", compiler_params=_params(("parallel", "parallel")))(
        place, psum, got, got, got)


def _pair_gather(sums):
    n = len(sums)
    hl = sums[0].shape[0] // 2

    def body(*refs):
        bufs = refs[n:2 * n]
        send, recv = refs[2 * n:]
        x, y, c, _ = _place()
        mine = pl.ds(c * hl, hl)
        cps = [_remote(bufs[i].at[mine], bufs[i].at[mine], send.at[i], recv.at[i], (x, y, 1 - c)) for i in range(n)]
        for cp in cps:
            cp.start()
        for cp in cps:
            cp.wait()

    out_shape = [jax.ShapeDtypeStruct(v.shape, v.dtype) for v in sums]
    sems = [pltpu.SemaphoreType.DMA((n,)) for _ in range(2)]
    return pl.pallas_call(body, out_shape=out_shape, in_specs=[HBM] * n, out_specs=[HBM] * n, scratch_shapes=sems,
                          input_output_aliases={i: i for i in range(n)}, name="reduce_pair_gather")(*sums)


def _allreduce_small(vec):
    rows = vec.shape[0]

    def body(v_ref, o_ref, all_ref, send, recv):
        x, y, c, _ = _place()
        me = 4 * x + 2 * y + c
        all_ref[me] = v_ref[...]
        cps = []
        for dlt in range(1, N_DEV):
            fx, fy, fc = (dlt >> 2) & 1, (dlt >> 1) & 1, dlt & 1
            to = (1 - x if fx else x, 1 - y if fy else y, 1 - c if fc else c)
            cps.append(_remote(v_ref, all_ref.at[me], send.at[dlt - 1], recv.at[dlt - 1], to))
        for cp in cps:
            cp.start()
        for cp in cps:
            cp.wait()
        tot = all_ref[0]
        for dev in range(1, N_DEV):
            tot = tot + all_ref[dev]
        o_ref[...] = tot

    vm = pl.BlockSpec(memory_space=pltpu.VMEM)
    return pl.pallas_call(
        body, out_shape=jax.ShapeDtypeStruct(vec.shape, F32), in_specs=[vm], out_specs=vm,
        scratch_shapes=[pltpu.VMEM((N_DEV, rows, 128), F32), pltpu.SemaphoreType.DMA((N_DEV - 1,)),
                        pltpu.SemaphoreType.DMA((N_DEV - 1,))],
        name="allreduce_small", compiler_params=pltpu.CompilerParams(vmem_limit_bytes=VMEM_LIMIT))(vec)


def _adamw(name, w, g, m, v):
    rows, cc = w.shape
    tm = math.gcd(rows, 256)
    return _rowwise(name, _adamw_fn, [R(w), R(g), R(m), R(v)], [O(cc, F32), O(cc, F32), O(cc, F32)], tm=tm)


def _pack(parts):
    flat = jnp.concatenate([v.reshape(-1).astype(F32) for v in parts])
    pad = (-flat.shape[0]) % (SUBLANES * 128)
    return jnp.pad(flat, (0, pad)).reshape(-1, 128)


def _unpack(packed, shapes):
    flat, out, pos = packed.reshape(-1), [], 0
    for shp in shapes:
        size = math.prod(shp)
        out.append(flat[pos:pos + size].reshape(shp))
        pos += size
    return out


def kernel(x, p, positions, mix_norm_g, w_in, b_gate, attn_sinks, w_attn_out, ssm_lambda_re, ssm_lambda_im, ssm_log_dt, ssm_b_re, ssm_b_im, ssm_c_re, ssm_c_im, ssm_d, w_ssm_glu, b_ssm_glu, conv_dw_w, conv_dw_b, conv_norm_g, conv_norm_b, w_conv_out, w_mix_out, ffn_norm_g, w_ffn_in, w_ffn_out, w_ple_in, ple_norm_g, w_ple_gate, final_norm_g, loss_target, m_mix_norm_g, m_w_in, m_b_gate, m_attn_sinks, m_w_attn_out, m_ssm_lambda_re, m_ssm_lambda_im, m_ssm_log_dt, m_ssm_b_re, m_ssm_b_im, m_ssm_c_re, m_ssm_c_im, m_ssm_d, m_w_ssm_glu, m_b_ssm_glu, m_conv_dw_w, m_conv_dw_b, m_conv_norm_g, m_conv_norm_b, m_w_conv_out, m_w_mix_out, m_ffn_norm_g, m_w_ffn_in, m_w_ffn_out, m_w_ple_in, m_ple_norm_g, m_w_ple_gate, m_final_norm_g, v_mix_norm_g, v_w_in, v_b_gate, v_attn_sinks, v_w_attn_out, v_ssm_lambda_re, v_ssm_lambda_im, v_ssm_log_dt, v_ssm_b_re, v_ssm_b_im, v_ssm_c_re, v_ssm_c_im, v_ssm_d, v_w_ssm_glu, v_b_ssm_glu, v_conv_dw_w, v_conv_dw_b, v_conv_norm_g, v_conv_norm_b, v_w_conv_out, v_w_mix_out, v_ffn_norm_g, v_w_ffn_in, v_w_ffn_out, v_w_ple_in, v_ple_norm_g, v_w_ple_gate, v_final_norm_g):
    given = dict(locals())
    wts = {nm: given[nm] for nm in WEIGHTS}
    mom = {nm: given["m_" + nm] for nm in WEIGHTS}
    var = {nm: given["v_" + nm] for nm in WEIGHTS}
    depth = p.shape[0]
    chip = 2 * lax.axis_index("x") + lax.axis_index("y")

    cw_cols = conv_dw_w.shape[2]
    taps = jnp.pad(conv_dw_w.reshape(depth, -1), ((0, 0), (0, (-CONV_K * cw_cols) % (SUBLANES * 128))))
    shards = [{**{nm: wts[nm][l].astype(BF16) for nm in BIG}, TAPS: taps[l].reshape(-1, 128)} for l in range(depth)]
    small = {nm: wts[nm] for nm in SMALL if nm != "conv_dw_w"}

    loss_cols, grad_x, gbig, gsmall = _local_step(x, p, positions, loss_target, small, shards=shards)

    parts = [loss_cols] + [gsmall[nm] for nm in SMALL]
    total = _allreduce_small(_pack(parts))
    summed = _unpack(total, [v.shape for v in parts])
    loss = jnp.sum(summed[0])
    gsum = dict(zip(SMALL, summed[1:]))
    gsum["conv_dw_w"] = lax.dynamic_slice_in_dim(gsum["conv_dw_w"], chip * cw_cols, cw_cols, axis=2)
    shapes = [wts[nm].shape for nm in SMALL]
    deltas, new_m, new_v = _adamw("adamw_small", _pack([wts[nm] for nm in SMALL]), _pack([gsum[nm] for nm in SMALL]),
                                  _pack([mom[nm] for nm in SMALL]), _pack([var[nm] for nm in SMALL]))
    grads = dict(gsum)
    delta = dict(zip(SMALL, _unpack(deltas, shapes)))
    newm = dict(zip(SMALL, _unpack(new_m, shapes)))
    newv = dict(zip(SMALL, _unpack(new_v, shapes)))

    gl = [gbig[nm] for nm in BIG]
    sib = _pair_exchange(gl)
    psums = [_pair_add(g, r) for g, r in zip(gl, sib)]
    got = _chip_exchange(psums)
    sums = _pair_gather([_sum4(ps, g) for ps, g in zip(psums, got)])
    for nm, g in zip(BIG, sums):
        shp = wts[nm].shape
        two = lambda v: v.reshape(-1, shp[-1])
        g = g.reshape(shp)
        d_w, n_m, n_v = _adamw("adamw_" + nm, two(wts[nm]), two(g), two(mom[nm]), two(var[nm]))
        grads[nm], delta[nm], newm[nm], newv[nm] = g, d_w.reshape(shp), n_m.reshape(shp), n_v.reshape(shp)

    return (loss, grad_x, *[grads[nm] for nm in WEIGHTS], *[delta[nm] for nm in WEIGHTS],
            *[newm[nm] for nm in WEIGHTS], *[newv[nm] for nm in WEIGHTS])
```

```python
import functools
import math

import jax
import jax.numpy as jnp
from jax import lax
from jax.experimental import pallas as pl
from jax.experimental.pallas import tpu as pltpu

F32 = jnp.float32
BF16 = jnp.bfloat16

D_MODEL = 1024
HEAD_DIM = 64
N_Q_HEADS = 8
N_KV_HEADS = 2
GQA_GROUP = N_Q_HEADS // N_KV_HEADS
ATT_BLOCK = 128
ROPE_THETA = 500000.0
ROPE_DIM = HEAD_DIM // 4
ROPE_HALF = ROPE_DIM // 2
Q_WIDTH = N_Q_HEADS * HEAD_DIM
KV_WIDTH = N_KV_HEADS * HEAD_DIM
SSM_WIDTH = 256
SSM_GROUP = 16
SSM_GROUPS = 16
SSM_STATE = 64
SSM_LANES = SSM_GROUPS * SSM_STATE
CONV_WIDTH = 256
CONV_K = 31
CONV_HALO = 32
FFN_HIDDEN = 2816
EPS = 1e-6
NEG_INF = -1e30
SCALE = HEAD_DIM ** -0.5

ADAM_LR = 0.001
ADAM_B1 = 0.9
ADAM_B2 = 0.999
ADAM_EPS = 1e-08
ADAM_WD = 0.01
ADAM_STEP = 10

N_CHIPS = 4
N_DEV = 8
SUBLANES = 8
VMEM_LIMIT = 56 * 1024 * 1024

MESH = pl.DeviceIdType.MESH


def _params(sem=None):
    return pltpu.CompilerParams(dimension_semantics=sem, vmem_limit_bytes=VMEM_LIMIT)


def R(arr, width=None, cb=0, rb=0):
    return ("r", arr, arr.shape[1] if width is None else width, (cb, rb))


def V(arr, width=None, cb=0):
    return ("v", arr, arr.shape[1] if width is None else width, cb)


def _cbf(cb):
    return cb if callable(cb) else (lambda j, c=cb: c + j)


def _rowwise(name, fn, ins, outs, accs=(), *, tm, ncol=1):
    t = [a for k, a, _, _ in ins if k == "r"][0].shape[0]
    tm = min(tm, t)
    assert t % tm == 0, (name, t, tm)
    n_i, n_o, n_a = len(ins), len(outs), len(accs)

    def body(*refs):
        vals = fn(*[r[...] for r in refs[:n_i]])
        if not isinstance(vals, (tuple, list)):
            vals = (vals,)
        for ref, val in zip(refs[n_i:n_i + n_o], vals[:n_o]):
            ref[...] = val.astype(ref.dtype)
        if n_a:
            acc_refs = refs[n_i + n_o:]

            @pl.when(pl.program_id(1) == 0)
            def _():
                for ref in acc_refs:
                    ref[...] = jnp.zeros_like(ref)

            for ref, val in zip(acc_refs, vals[n_o:]):
                ref[...] += val

    in_specs = []
    for kind, arr, width, cb in ins:
        if kind == "r":
            f = _cbf(cb[0])
            in_specs.append(pl.BlockSpec((tm, width), functools.partial(lambda j, i, f, rb: (i + rb, f(j)), f=f, rb=cb[1])))
        else:
            f = _cbf(cb)
            in_specs.append(pl.BlockSpec((arr.shape[0], width), functools.partial(lambda j, i, f: (0, f(j)), f=f)))
    out_specs, out_shape = [], []
    for total, width, cb, dt in outs:
        f = _cbf(cb)
        out_specs.append(pl.BlockSpec((tm, width), functools.partial(lambda j, i, f: (i, f(j)), f=f)))
        out_shape.append(jax.ShapeDtypeStruct((t, total), dt))
    for total, width, cb in accs:
        f = _cbf(cb)
        out_specs.append(pl.BlockSpec((1, width), functools.partial(lambda j, i, f: (0, f(j)), f=f)))
        out_shape.append(jax.ShapeDtypeStruct((1, total), F32))
    sem = ("arbitrary", "arbitrary") if n_a else ("parallel", "parallel")
    res = pl.pallas_call(body, out_shape=out_shape, grid=(ncol, t // tm), in_specs=in_specs, out_specs=out_specs,
                         name=name, compiler_params=_params(sem))(*[a for _, a, _, _ in ins])
    return res[0] if len(res) == 1 else res


def O(width, dtype, total=None, cb=0):
    return (width if total is None else total, width, cb, dtype)


def A(width, total=None, cb=0):
    return (width if total is None else total, width, cb)


_DIMS = {"nn": (((1,), (0,)), ((), ())), "nt": (((1,), (1,)), ((), ())), "tn": (((0,), (0,)), ((), ()))}


def _mm(name, a, b, mode, out_dtype, *, m, n, k, tm, tn, tk, a_off=0, b_off=0, res=None, bias=None, b_sh=None, o_sh=None,
        comm=None, a_keep=False):
    tm, tn, tk = min(tm, m), min(tn, n), min(tk, k)
    assert m % tm == 0 and n % tn == 0 and k % tk == 0, (name, m, n, k, tm, tn, tk)
    nk = k // tk
    has_res, has_bias = res is not None, bias is not None
    a_fn, a_ops = a if isinstance(a, tuple) else (None, [(a, None)])
    b_fn, b_ops = b if isinstance(b, tuple) else (None, [(b, None)])
    na, nb_ = len(a_ops), len(b_ops)
    a_bytes = sum(m * k * arr.dtype.itemsize for arr, _ in a_ops)
    b_bytes = sum(n * k * arr.dtype.itemsize for arr, _ in b_ops)
    swap = nk == 1 and b_bytes + (n // tn) * a_bytes < a_bytes + (m // tm) * b_bytes
    grid = (n // tn, m // tm, nk) if swap else (m // tm, n // tn, nk)
    ncomm = 0 if comm is None else len(comm["ins"])

    def body(*refs):
        g0, g1, kk = pl.program_id(0), pl.program_id(1), pl.program_id(2)
        gi, gj = (g1, g0) if swap else (g0, g1)
        a_tiles = [r[...] for r in refs[:na]]
        b_tiles = [r[...] for r in refs[na:na + nb_]]
        a_val = a_tiles[0] if a_fn is None else a_fn(gi, gj, kk, *a_tiles)
        b_val = b_tiles[0] if b_fn is None else b_fn(gi, gj, kk, *b_tiles)
        pos = na + nb_
        res_ref = bias_ref = None
        if has_res:
            res_ref = refs[pos]
            pos += 1
        if has_bias:
            bias_ref = refs[pos]
            pos += 1
        comm_ins = refs[pos:pos + ncomm]
        o_ref = refs[pos + ncomm]
        comm_outs = refs[pos + ncomm + 1:pos + 2 * ncomm + 1]
        scratch = refs[pos + 2 * ncomm + 1:]
        if a_keep:
            scratch[0][...] = a_val.astype(BF16)
            scratch = scratch[1:]
        if comm is not None:
            sems = scratch[1:] if nk > 1 else scratch

            @pl.when((g0 == 0) & (g1 == 0) & (kk == 0))
            def _():
                comm["start"](comm_ins, comm_outs, sems)

        def finish(r):
            if has_bias:
                r = r + bias_ref[...]
            if has_res:
                r = r + res_ref[...].astype(F32)
            o_ref[...] = r.astype(o_ref.dtype)

        part = lax.dot_general(a_val.astype(BF16), b_val.astype(BF16), _DIMS[mode], preferred_element_type=F32)
        if nk == 1:
            finish(part)
        else:
            acc_ref = scratch[0]

            @pl.when(kk == 0)
            def _():
                acc_ref[...] = part

            @pl.when(kk > 0)
            def _():
                acc_ref[...] += part

            @pl.when(kk == nk - 1)
            def _():
                finish(acc_ref[...])

        if comm is not None:
            @pl.when((g0 == grid[0] - 1) & (g1 == grid[1] - 1) & (kk == nk - 1))
            def _():
                comm["finish"](comm_ins, comm_outs, sems)

    def at(f):
        return (lambda g0, g1, kk: f(g1, g0, kk)) if swap else f

    if mode == "nn":
        a_spec = pl.BlockSpec((tm, tk), at(lambda i, j, kk: (i, kk + a_off)))
        b_spec = pl.BlockSpec((tk, tn), at(lambda i, j, kk: (kk, j + b_off)))
        if b_sh is not None:
            assert b_sh % tn == 0, (name, b_sh, tn)
            per = b_sh // tn
            b_spec = pl.BlockSpec((None, tk, tn), at(lambda i, j, kk: (j // per, kk, j % per)))
    elif mode == "nt":
        a_spec = pl.BlockSpec((tm, tk), at(lambda i, j, kk: (i, kk + a_off)))
        b_spec = pl.BlockSpec((tn, tk), at(lambda i, j, kk: (j, kk + b_off)))
        if b_sh is not None:
            assert b_sh % tk == 0, (name, b_sh, tk)
            per = b_sh // tk
            b_spec = pl.BlockSpec((None, tn, tk), at(lambda i, j, kk: (kk // per, j, kk % per)))
    else:
        a_spec = pl.BlockSpec((tk, tm), at(lambda i, j, kk: (kk, i + a_off)))
        b_spec = pl.BlockSpec((tk, tn), at(lambda i, j, kk: (kk, j + b_off)))
    a_specs = [a_spec] if a_fn is None else [pl.BlockSpec(a_spec.block_shape, at(f)) for _, f in a_ops]
    b_specs = [b_spec] if b_fn is None else [pl.BlockSpec(b_spec.block_shape, at(f)) for _, f in b_ops]
    in_specs, args = a_specs + b_specs, [arr for arr, _ in a_ops] + [arr for arr, _ in b_ops]
    if has_res:
        in_specs.append(pl.BlockSpec((tm, tn), at(lambda i, j, kk: (i, j))))
        args.append(res)
    if has_bias:
        in_specs.append(pl.BlockSpec((1, tn), at(lambda i, j, kk: (0, j))))
        args.append(bias)
    out_spec, out_shape = pl.BlockSpec((tm, tn), at(lambda i, j, kk: (i, j))), (m, n)
    if o_sh is not None:
        assert o_sh % tn == 0, (name, o_sh, tn)
        per_o = o_sh // tn
        out_spec = pl.BlockSpec((None, tm, tn), at(lambda i, j, kk: (j // per_o, i, j % per_o)))
        out_shape = (n // o_sh, m, o_sh)
    scratch = [pltpu.VMEM((tm, tn), F32)] if nk > 1 else []
    if a_keep:
        assert comm is None and o_sh is None and mode != "tn" and n == tn, name
        outs = pl.pallas_call(
            body, out_shape=[jax.ShapeDtypeStruct(out_shape, out_dtype), jax.ShapeDtypeStruct((m, k), BF16)], grid=grid,
            in_specs=in_specs, out_specs=[out_spec, pl.BlockSpec((tm, tk), at(lambda i, j, kk: (i, kk)))],
            scratch_shapes=scratch, name=name, compiler_params=_params(("parallel", "parallel", "arbitrary")))(*args)
        return outs[0], outs[1]
    if comm is None:
        return pl.pallas_call(
            body, out_shape=jax.ShapeDtypeStruct(out_shape, out_dtype), grid=grid, in_specs=in_specs,
            out_specs=out_spec, scratch_shapes=scratch, name=name,
            compiler_params=_params(("parallel", "parallel", "arbitrary")))(*args)
    outs = pl.pallas_call(
        body, out_shape=[jax.ShapeDtypeStruct(out_shape, out_dtype)] + comm["out_shapes"], grid=grid,
        in_specs=in_specs + [HBM] * ncomm, out_specs=[out_spec] + [HBM] * ncomm,
        scratch_shapes=scratch + comm["sems"], name=name,
        compiler_params=_params(("arbitrary", "arbitrary", "arbitrary")))(*args, *comm["ins"])
    return outs[0], outs[1:]


def _sig(v):
    return jax.nn.sigmoid(v)


def _rms_fwd(x, g):
    r = lax.rsqrt(jnp.mean(x * x, axis=-1, keepdims=True) + EPS)
    return x * r * g


def _rms_bwd(dh, x, dres, g):
    dh = dh.astype(F32)
    r = lax.rsqrt(jnp.mean(x * x, axis=-1, keepdims=True) + EPS)
    xh = x * r
    dxh = dh * g
    dx = r * (dxh - xh * jnp.mean(dxh * xh, axis=-1, keepdims=True))
    return dres + dx, jnp.sum(dh * xh, axis=0, keepdims=True)


def _rope_apply(t, c, sa, sb):
    w = t.shape[1]
    return t * c + pltpu.roll(t, w - ROPE_HALF, 1) * sa + pltpu.roll(t, ROPE_HALF, 1) * sb


def _rope_transpose(g, c, sa, sb):
    w = g.shape[1]
    return g * c + pltpu.roll(g * sa, ROPE_HALF, 1) + pltpu.roll(g * sb, w - ROPE_HALF, 1)


def _tile_lanes(tab, reps):
    return jnp.concatenate([tab] * reps, axis=1) if reps > 1 else tab


def _rope_fwd(q, k, v, c, sa, sb):
    rq = Q_WIDTH // c.shape[1]
    qr = _rope_apply(q.astype(F32), _tile_lanes(c, rq), _tile_lanes(sa, rq), _tile_lanes(sb, rq))
    kr = _rope_apply(k.astype(F32), c, sa, sb)
    return jnp.concatenate([qr, kr, v.astype(F32)], axis=1)


def _rope_bwd_q(g, c, sa, sb):
    rq = Q_WIDTH // c.shape[1]
    return _rope_transpose(g.astype(F32), _tile_lanes(c, rq), _tile_lanes(sa, rq), _tile_lanes(sb, rq))


def _gelu(v):
    return jax.nn.gelu(v, approximate=True)


def _gelu_grad(v):
    c0 = math.sqrt(2.0 / math.pi)
    inner = c0 * (v + 0.044715 * v * v * v)
    th = jnp.tanh(inner)
    return 0.5 * (1.0 + th) + 0.5 * v * (1.0 - th * th) * c0 * (1.0 + 3 * 0.044715 * v * v)


def _merge_fwd(g0, g1, g2, b0, b1, b2, ya, ga, gb, yc):
    s0 = _sig(g0.astype(F32) + b0)
    s1 = _sig(g1.astype(F32) + b1)
    s2 = _sig(g2.astype(F32) + b2)
    ys = ga.astype(F32) * _sig(gb.astype(F32))
    return s0 * ya.astype(F32) + s1 * ys + s2 * yc.astype(F32)


def _merge_bwd(dm, g0, g1, g2, b0, b1, b2, ya, ga, gb, yc):
    dm = dm.astype(F32)
    s0 = _sig(g0.astype(F32) + b0)
    s1 = _sig(g1.astype(F32) + b1)
    s2 = _sig(g2.astype(F32) + b2)
    ga = ga.astype(F32)
    sb = _sig(gb.astype(F32))
    ys = ga * sb
    dya = dm * s0
    dys = dm * s1
    dyc = dm * s2
    dga = dys * sb
    dgb = dys * ga * sb * (1.0 - sb)
    d0 = dm * ya.astype(F32) * s0 * (1.0 - s0)
    d1 = dm * ys * s1 * (1.0 - s1)
    d2 = dm * yc.astype(F32) * s2 * (1.0 - s2)
    cs = lambda v: jnp.sum(v, axis=0, keepdims=True)
    return dya, dga, dgb, dyc, d0, d1, d2, cs(d0), cs(d1), cs(d2), cs(dga), cs(dgb)


def _ffn_act(fg, fu):
    fg = fg.astype(F32)
    return fg * _sig(fg) * fu.astype(F32)


def _ffn_act_bwd(da, fg, fu):
    da, fg, fu = da.astype(F32), fg.astype(F32), fu.astype(F32)
    s = _sig(fg)
    return da * fu * (s * (1.0 + fg * (1.0 - s))), da * fg * s


def _ple_fwd(x, gp, e):
    return x + _sig(gp.astype(F32)) * e.astype(F32)


def _ple_bwd(dx, gp, e):
    s = _sig(gp.astype(F32))
    e = e.astype(F32)
    return dx * s, dx * e * s * (1.0 - s)


def _loss_fn(x, tgt, g):
    d = x.shape[1]
    r = lax.rsqrt(jnp.mean(x * x, axis=-1, keepdims=True) + EPS)
    xh = x * r
    err = xh * g - tgt
    dy = err * (1.0 / d)
    dxh = dy * g
    dx = r * (dxh - xh * jnp.mean(dxh * xh, axis=-1, keepdims=True))
    return dx, jnp.sum(err * err, axis=0, keepdims=True) * (0.5 / d), jnp.sum(dy * xh, axis=0, keepdims=True)


def _adamw_fn(w, g, m, v):
    m = ADAM_B1 * m + (1.0 - ADAM_B1) * g
    v = ADAM_B2 * v + (1.0 - ADAM_B2) * (g * g)
    m_hat = m / (1.0 - ADAM_B1 ** ADAM_STEP)
    v_hat = v / (1.0 - ADAM_B2 ** ADAM_STEP)
    delta = -ADAM_LR * (m_hat / (jnp.sqrt(v_hat) + ADAM_EPS) + ADAM_WD * w)
    return delta, m, v


def _band_mask(n):
    qi = lax.broadcasted_iota(jnp.int32, (ATT_BLOCK, 2 * ATT_BLOCK), 0)
    kj = lax.broadcasted_iota(jnp.int32, (ATT_BLOCK, 2 * ATT_BLOCK), 1)
    dist = qi + ATT_BLOCK - kj
    return (dist >= 0) & (dist < ATT_BLOCK) & ((n > 0) | (kj >= ATT_BLOCK))


K_HEADS_AT = N_Q_HEADS // N_KV_HEADS


def _att_specs(nb):
    qs = pl.BlockSpec((N_Q_HEADS, ATT_BLOCK, HEAD_DIM), lambda b, n: (0, b * nb + n, 0))

    def kv(head_block, back):
        return pl.BlockSpec((N_KV_HEADS, ATT_BLOCK, HEAD_DIM),
                            lambda b, n: (head_block, b * nb + jnp.maximum(n - back, 0), 0))

    stat = pl.BlockSpec((N_Q_HEADS, ATT_BLOCK, 1), lambda b, n: (0, b * nb + n, 0))
    sink = pl.BlockSpec((N_Q_HEADS, 1, 1), lambda b, n: (0, 0, 0))
    return qs, [kv(K_HEADS_AT, 1), kv(K_HEADS_AT, 0), kv(K_HEADS_AT + 1, 1), kv(K_HEADS_AT + 1, 0)], stat, sink


def _attn_fwd(qkv, sinks, nbatch, seq):
    t = qkv.shape[1]
    nb = seq // ATT_BLOCK
    qs, kvs, stat, sink = _att_specs(nb)

    def body(q_ref, kp_ref, kc_ref, vp_ref, vc_ref, sink_ref, o_ref, lse_ref):
        mask = _band_mask(pl.program_id(1))
        rows = GQA_GROUP * ATT_BLOCK
        for kv in range(N_KV_HEADS):
            hs = slice(kv * GQA_GROUP, (kv + 1) * GQA_GROUP)
            kk = jnp.concatenate([kp_ref[kv], kc_ref[kv]], axis=0)
            vv = jnp.concatenate([vp_ref[kv], vc_ref[kv]], axis=0)
            q4 = (q_ref[hs] * SCALE).reshape(rows, HEAD_DIM)
            s = lax.dot_general(q4, kk, _DIMS["nt"], preferred_element_type=F32)
            s = jnp.where(mask, s.reshape(GQA_GROUP, ATT_BLOCK, 2 * ATT_BLOCK), NEG_INF)
            sk = sink_ref[hs]
            mx = jnp.maximum(jnp.max(s, axis=-1, keepdims=True), sk)
            p = jnp.exp(s - mx)
            den = jnp.sum(p, axis=-1, keepdims=True) + jnp.exp(sk - mx)
            o = lax.dot_general(p.reshape(rows, 2 * ATT_BLOCK).astype(BF16), vv, _DIMS["nn"],
                                preferred_element_type=F32).reshape(GQA_GROUP, ATT_BLOCK, HEAD_DIM)
            o_ref[hs] = (o * (1.0 / den)).astype(o_ref.dtype)
            lse_ref[hs] = mx + jnp.log(den)

    return pl.pallas_call(
        body, grid=(nbatch, nb), in_specs=[qs] + kvs + [sink], out_specs=[qs, stat],
        out_shape=[jax.ShapeDtypeStruct((N_Q_HEADS, t, HEAD_DIM), BF16), jax.ShapeDtypeStruct((N_Q_HEADS, t, 1), F32)],
        name="attn_fwd", compiler_params=_params(("parallel", "parallel")))(qkv, qkv, qkv, qkv, qkv, sinks)


def _attn_bwd(qkv, oh, doh, lse, sinks, nbatch, seq):
    t = qkv.shape[1]
    nb = seq // ATT_BLOCK
    qs, kvs, stat, sink = _att_specs(nb)

    def body(q_ref, kp_ref, kc_ref, vp_ref, vc_ref, o_ref, do_ref, lse_ref, sink_ref, dqkv_ref, dsink_ref):
        dq_ref = dqkv_ref.at[pl.ds(0, N_Q_HEADS)]
        dkc_ref, dvc_ref, dkp_ref, dvp_ref = (dqkv_ref.at[pl.ds(N_Q_HEADS + N_KV_HEADS * i, N_KV_HEADS)]
                                              for i in range(4))
        first = (pl.program_id(0) == 0) & (pl.program_id(1) == 0)

        @pl.when(first)
        def _():
            dsink_ref[...] = jnp.zeros_like(dsink_ref)

        mask = _band_mask(pl.program_id(1))
        rows = GQA_GROUP * ATT_BLOCK
        band = (GQA_GROUP, ATT_BLOCK, 2 * ATT_BLOCK)
        for kv in range(N_KV_HEADS):
            hs = slice(kv * GQA_GROUP, (kv + 1) * GQA_GROUP)
            kk = jnp.concatenate([kp_ref[kv], kc_ref[kv]], axis=0)
            vv = jnp.concatenate([vp_ref[kv], vc_ref[kv]], axis=0)
            q4 = q_ref[hs].reshape(rows, HEAD_DIM)
            do4 = do_ref[hs].reshape(rows, HEAD_DIM)
            lse4 = lse_ref[hs]
            s = lax.dot_general(q4 * SCALE, kk, _DIMS["nt"], preferred_element_type=F32).reshape(band)
            p = jnp.where(mask, jnp.exp(s - lse4), 0.0)
            dd = jnp.sum(do_ref[hs].astype(F32) * o_ref[hs].astype(F32), axis=-1, keepdims=True)
            dp = lax.dot_general(do4, vv, _DIMS["nt"], preferred_element_type=F32).reshape(band)
            ds = (p * (dp - dd) * SCALE).astype(BF16).reshape(rows, 2 * ATT_BLOCK)
            dq = lax.dot_general(ds, kk, _DIMS["nn"], preferred_element_type=F32)
            dq_ref[hs] = dq.reshape(GQA_GROUP, ATT_BLOCK, HEAD_DIM).astype(dq_ref.dtype)
            dk = lax.dot_general(ds, q4, _DIMS["tn"], preferred_element_type=F32)
            dv = lax.dot_general(p.astype(BF16).reshape(rows, 2 * ATT_BLOCK), do4, _DIMS["tn"],
                                 preferred_element_type=F32)
            dsink_ref[hs] += -jnp.sum(jnp.exp(sink_ref[hs] - lse4) * dd, axis=1, keepdims=True)
            dkp_ref[kv] = dk[:ATT_BLOCK]
            dkc_ref[kv] = dk[ATT_BLOCK:]
            dvp_ref[kv] = dv[:ATT_BLOCK]
            dvc_ref[kv] = dv[ATT_BLOCK:]

    n_out = 2 * N_Q_HEADS
    return pl.pallas_call(
        body, grid=(nbatch, nb), in_specs=[qs] + kvs + [qs, qs, stat, sink],
        out_specs=[pl.BlockSpec((n_out, ATT_BLOCK, HEAD_DIM), lambda b, n: (0, b * nb + n, 0)), sink],
        out_shape=[jax.ShapeDtypeStruct((n_out, t, HEAD_DIM), F32), jax.ShapeDtypeStruct((N_Q_HEADS, 1, 1), F32)],
        name="attn_bwd", compiler_params=_params(("arbitrary", "arbitrary")))(
        qkv, qkv, qkv, qkv, qkv, oh, doh, lse, sinks)


def _kv_combine(dqkv, c, sa, sb, seq):
    t = dqkv.shape[0]
    nb = seq // ATT_BLOCK
    nblk = t // ATT_BLOCK
    col0 = Q_WIDTH // KV_WIDTH

    def body(kc_ref, kp_ref, vc_ref, vp_ref, c_ref, sa_ref, sb_ref, dk_ref, dv_ref):
        has_next = (pl.program_id(0) % nb) != nb - 1
        dk = kc_ref[...] + jnp.where(has_next, kp_ref[...], 0.0)
        dv = vc_ref[...] + jnp.where(has_next, vp_ref[...], 0.0)
        dk_ref[...] = _rope_transpose(dk, c_ref[...], sa_ref[...], sb_ref[...]).astype(dk_ref.dtype)
        dv_ref[...] = dv.astype(dv_ref.dtype)

    cur = pl.BlockSpec((ATT_BLOCK, KV_WIDTH), lambda i: (i, 0))
    own = lambda col: pl.BlockSpec((ATT_BLOCK, KV_WIDTH), lambda i: (i, col0 + col))
    nxt = lambda col: pl.BlockSpec((ATT_BLOCK, KV_WIDTH), lambda i: (jnp.minimum(i + 1, nblk - 1), col0 + col))
    o = jax.ShapeDtypeStruct((t, KV_WIDTH), BF16)
    return pl.pallas_call(body, grid=(nblk,), in_specs=[own(0), nxt(2), own(1), nxt(3), cur, cur, cur],
                          out_specs=[cur, cur], out_shape=[o, o], name="kv_combine",
                          compiler_params=_params(("parallel",)))(dqkv, dqkv, dqkv, dqkv, c, sa, sb)


def _scan_block(ref, tab_ref, carry, ngroups, reverse):
    shifts = (7, 6, 4) if reverse else (1, 2, 4)
    n = SSM_LANES

    def step(i, car):
        g = (ngroups - 1 - i) if reverse else i
        r0 = pl.multiple_of(g * SUBLANES, SUBLANES)
        xr = ref[pl.ds(r0, SUBLANES), :n]
        xi = ref[pl.ds(r0, SUBLANES), n:]
        for s, sh in enumerate(shifts):
            pr, pi = tab_ref[2 * s], tab_ref[2 * s + 1]
            yr, yi = pltpu.roll(xr, sh, 0), pltpu.roll(xi, sh, 0)
            xr, xi = xr + pr * yr - pi * yi, xi + pr * yi + pi * yr
        cr, ci = car
        qr, qi = tab_ref[6], tab_ref[7]
        xr, xi = xr + qr * cr - qi * ci, xi + qr * ci + qi * cr
        ref[pl.ds(r0, SUBLANES), :n] = xr
        ref[pl.ds(r0, SUBLANES), n:] = xi
        last = r0 if reverse else r0 + SUBLANES - 1
        return ref[pl.ds(last, 1), :n], ref[pl.ds(last, 1), n:]

    return lax.fori_loop(0, ngroups, step, carry, unroll=2)


def _ssm_chunk(seq):
    return min(512, seq)


def _ssm_fwd(z, wb, wc, tab, dskip, nbatch, seq):
    t = z.shape[0]
    tc = _ssm_chunk(seq)
    nc = seq // tc
    n2 = 2 * SSM_LANES

    def body(u_ref, wb_ref, wc_ref, tab_ref, d_ref, st_ref, y_ref, gel_ref, car_ref):
        @pl.when(pl.program_id(1) == 0)
        def _():
            car_ref[...] = jnp.zeros_like(car_ref)

        u = u_ref[...]
        st_ref[...] = lax.dot_general(u, wb_ref[...], _DIMS["nn"], preferred_element_type=F32)
        cr, ci = _scan_block(st_ref, tab_ref, (car_ref[:, :SSM_LANES], car_ref[:, SSM_LANES:]), tc // SUBLANES, False)
        car_ref[:, :SSM_LANES] = cr
        car_ref[:, SSM_LANES:] = ci
        y = lax.dot_general(st_ref[...].astype(BF16), wc_ref[...], _DIMS["nn"], preferred_element_type=F32)
        y = y + d_ref[...] * u.astype(F32)
        y_ref[...] = y
        gel_ref[...] = _gelu(y).astype(gel_ref.dtype)

    row = lambda b, c: (b * nc + c, 0)
    full = lambda b, c: (0, 0)
    return pl.pallas_call(
        body, grid=(nbatch, nc),
        in_specs=[pl.BlockSpec((tc, SSM_WIDTH), lambda b, c: (b * nc + c, 3)), pl.BlockSpec((SSM_WIDTH, n2), full),
                  pl.BlockSpec((n2, SSM_WIDTH), full), pl.BlockSpec((8, SUBLANES, SSM_LANES), lambda b, c: (0, 0, 0)),
                  pl.BlockSpec((1, SSM_WIDTH), full)],
        out_specs=[pl.BlockSpec((tc, n2), row), pl.BlockSpec((tc, SSM_WIDTH), row), pl.BlockSpec((tc, SSM_WIDTH), row)],
        out_shape=[jax.ShapeDtypeStruct((t, n2), F32), jax.ShapeDtypeStruct((t, SSM_WIDTH), F32),
                   jax.ShapeDtypeStruct((t, SSM_WIDTH), BF16)],
        scratch_shapes=[pltpu.VMEM((1, n2), F32)], name="ssm_fwd",
        compiler_params=_params(("arbitrary", "arbitrary")))(z, wb, wc, tab, dskip)


def _ssm_bwd(dgi, ys, st, z, wbt, wct, tab_rev, dskip, nbatch, seq):
    t = z.shape[0]
    tc = _ssm_chunk(seq)
    nc = seq // tc
    n = SSM_LANES
    n2 = 2 * n
    ng = tc // SUBLANES

    def body(dgi_ref, ys_ref, st_ref, stp_ref, u_ref, wbt_ref, wct_ref, tab_ref, d_ref,
             du_ref, dwb_ref, dwc_ref, dd_ref, da_ref, p_ref, sb_ref, car_ref):
        b, c = pl.program_id(0), pl.program_id(1)
        ct = nc - 1 - c

        @pl.when((b == 0) & (c == 0))
        def _():
            dwb_ref[...] = jnp.zeros_like(dwb_ref)
            dwc_ref[...] = jnp.zeros_like(dwc_ref)
            dd_ref[...] = jnp.zeros_like(dd_ref)
            da_ref[...] = jnp.zeros_like(da_ref)

        @pl.when(c == 0)
        def _():
            car_ref[...] = jnp.zeros_like(car_ref)

        u = u_ref[...]
        dys = dgi_ref[...].astype(F32) * _gelu_grad(ys_ref[...])
        dys_b = dys.astype(BF16)
        st = st_ref[...]
        dd_ref[...] += jnp.sum(dys * u.astype(F32), axis=0, keepdims=True)
        dwc_ref[...] += lax.dot_general(st.astype(BF16), dys_b, _DIMS["tn"], preferred_element_type=F32)
        p_ref[...] = lax.dot_general(dys_b, wct_ref[...], _DIMS["nn"], preferred_element_type=F32)
        cr, ci = _scan_block(p_ref, tab_ref, (car_ref[:, :n], car_ref[:, n:]), ng, True)
        car_ref[:, :n] = cr
        car_ref[:, n:] = ci
        p = p_ref[...]
        pb = p.astype(BF16)
        dwb_ref[...] += lax.dot_general(u, pb, _DIMS["tn"], preferred_element_type=F32)
        du = lax.dot_general(pb, wbt_ref[...], _DIMS["nn"], preferred_element_type=F32) + d_ref[...] * dys
        du_ref[...] = du.astype(du_ref.dtype)
        sb_ref[pl.ds(0, SUBLANES), :] = jnp.where(ct > 0, stp_ref[...], 0.0)
        sb_ref[pl.ds(SUBLANES, tc), :] = st
        row0 = lax.broadcasted_iota(jnp.int32, (SUBLANES, n), 0) == 0

        def acc_step(g, acc):
            ar, ai = acc
            r0 = pl.multiple_of(g * SUBLANES, SUBLANES)
            edge_r = sb_ref[pl.ds(r0 + SUBLANES - 1, 1), :n]
            edge_i = sb_ref[pl.ds(r0 + SUBLANES - 1, 1), n:]
            sr = jnp.where(row0, edge_r, pltpu.roll(sb_ref[pl.ds(r0 + SUBLANES, SUBLANES), :n], 1, 0))
            si = jnp.where(row0, edge_i, pltpu.roll(sb_ref[pl.ds(r0 + SUBLANES, SUBLANES), n:], 1, 0))
            pr = p_ref[pl.ds(r0, SUBLANES), :n]
            pi = p_ref[pl.ds(r0, SUBLANES), n:]
            return ar + pr * sr + pi * si, ai + pi * sr - pr * si

        zero = jnp.zeros((SUBLANES, n), F32)
        ar, ai = lax.fori_loop(0, ng, acc_step, (zero, zero), unroll=2)
        da_ref[:, :n] += ar
        da_ref[:, n:] += ai

    row = lambda b, c: (b * nc + (nc - 1 - c), 0)
    prev8 = lambda b, c: (jnp.maximum((b * nc + (nc - 1 - c)) * (tc // SUBLANES) - 1, 0), 0)
    full = lambda b, c: (0, 0)
    return pl.pallas_call(
        body, grid=(nbatch, nc),
        in_specs=[pl.BlockSpec((tc, SSM_WIDTH), row), pl.BlockSpec((tc, SSM_WIDTH), row), pl.BlockSpec((tc, n2), row),
                  pl.BlockSpec((SUBLANES, n2), prev8),
                  pl.BlockSpec((tc, SSM_WIDTH), lambda b, c: (b * nc + (nc - 1 - c), 3)),
                  pl.BlockSpec((n2, SSM_WIDTH), full), pl.BlockSpec((SSM_WIDTH, n2), full),
                  pl.BlockSpec((8, SUBLANES, n), lambda b, c: (0, 0, 0)), pl.BlockSpec((1, SSM_WIDTH), full)],
        out_specs=[pl.BlockSpec((tc, SSM_WIDTH), row), pl.BlockSpec((SSM_WIDTH, n2), full),
                   pl.BlockSpec((n2, SSM_WIDTH), full), pl.BlockSpec((1, SSM_WIDTH), full),
                   pl.BlockSpec((SUBLANES, n2), full)],
        out_shape=[jax.ShapeDtypeStruct((t, SSM_WIDTH), BF16), jax.ShapeDtypeStruct((SSM_WIDTH, n2), F32),
                   jax.ShapeDtypeStruct((n2, SSM_WIDTH), F32), jax.ShapeDtypeStruct((1, SSM_WIDTH), F32),
                   jax.ShapeDtypeStruct((SUBLANES, n2), F32)],
        scratch_shapes=[pltpu.VMEM((tc, n2), F32), pltpu.VMEM((tc + SUBLANES, n2), F32), pltpu.VMEM((1, n2), F32)],
        name="ssm_bwd", compiler_params=_params(("arbitrary", "arbitrary")))(
        dgi, ys, st, st, z, wbt, wct, tab_rev, dskip)


def _ssm_prep(lam_re, lam_im, log_dt, b_re, b_im, c_re, c_im):
    lr = jnp.minimum(lam_re, -1e-4)
    li = lam_im
    dt = jnp.exp(log_dt)[:, None]
    mag = jnp.exp(lr * dt)
    a_re = mag * jnp.cos(li * dt)
    a_im = mag * jnp.sin(li * dt)
    den = lr * lr + li * li
    x_re, x_im = a_re - 1.0, a_im
    f_re = (x_re * lr + x_im * li) / den
    f_im = (x_im * lr - x_re * li) / den
    bb_re = f_re[..., None] * b_re - f_im[..., None] * b_im
    bb_im = f_re[..., None] * b_im + f_im[..., None] * b_re
    eye = jnp.eye(SSM_GROUPS, dtype=F32)
    emb_b = lambda v: jnp.einsum("gnh,gk->ghkn", v, eye).reshape(SSM_WIDTH, SSM_LANES)
    emb_c = lambda v: jnp.einsum("ghn,gk->gnkh", v, eye).reshape(SSM_LANES, SSM_WIDTH)
    wb = jnp.concatenate([emb_b(bb_re), emb_b(bb_im)], axis=1)
    wc = jnp.concatenate([emb_c(c_re), -emb_c(c_im)], axis=0)
    return a_re.reshape(-1), a_im.reshape(-1), wb, wc


def _ssm_tables(a_re, a_im, reverse):
    if reverse:
        a_im = -a_im
    pw = [(a_re, a_im)]
    for _ in range(SUBLANES - 1):
        pr, pi = pw[-1]
        pw.append((pr * a_re - pi * a_im, pr * a_im + pi * a_re))
    rows = jnp.arange(SUBLANES)[:, None]
    tabs = []
    for k in (1, 2, 4):
        ok = (rows + k <= SUBLANES - 1) if reverse else (rows >= k)
        tabs += [jnp.where(ok, pw[k - 1][0][None], 0.0), jnp.where(ok, pw[k - 1][1][None], 0.0)]
    order = list(range(SUBLANES - 1, -1, -1)) if reverse else list(range(SUBLANES))
    tabs += [jnp.stack([pw[i][0] for i in order]), jnp.stack([pw[i][1] for i in order])]
    return jnp.stack(tabs)


def _conv_chunk(seq):
    return min(512, seq)


def _shifted(buf, sh, tc, offsets):
    for b in range(SUBLANES):
        idx = [i for i, o in enumerate(offsets) if o % SUBLANES == b]
        if not idx:
            continue
        src = buf
        if b:
            span = tc + SUBLANES * max(offsets[i] // SUBLANES for i in idx)
            sh[pl.ds(0, span), :] = buf[pl.ds(b, span), :]
            src = sh
        for i in idx:
            yield i, src[pl.ds(offsets[i] // SUBLANES * SUBLANES, tc), :]


def _conv_fwd(z, w, bias, lg, lb, nbatch, seq):
    t = z.shape[0]
    tc = _conv_chunk(seq)
    nc = seq // tc

    def body(a_ref, g_ref, w_ref, b_ref, lg_ref, lb_ref, cv_ref, sc_ref, ubuf, sh):
        c = pl.program_id(1)

        @pl.when(c == 0)
        def _():
            ubuf[pl.ds(0, CONV_HALO), :] = jnp.zeros((CONV_HALO, CONV_WIDTH), F32)

        @pl.when(c > 0)
        def _():
            ubuf[pl.ds(0, CONV_HALO), :] = ubuf[pl.ds(tc, CONV_HALO), :]

        ubuf[pl.ds(CONV_HALO, tc), :] = a_ref[...].astype(F32) * _sig(g_ref[...].astype(F32))
        acc = jnp.zeros((tc, CONV_WIDTH), F32) + b_ref[...]
        for k, win in _shifted(ubuf, sh, tc, [CONV_HALO - (CONV_K - 1) + k for k in range(CONV_K)]):
            acc = acc + w_ref[pl.ds(k, 1), :] * win
        cv_ref[...] = acc
        mu = jnp.mean(acc, axis=-1, keepdims=True)
        xc = acc - mu
        y = xc * lax.rsqrt(jnp.mean(xc * xc, axis=-1, keepdims=True) + EPS) * lg_ref[...] + lb_ref[...]
        sc_ref[...] = (y * _sig(y)).astype(sc_ref.dtype)

    row = lambda b, c: (b * nc + c, 0)
    full = lambda b, c: (0, 0)
    vec = pl.BlockSpec((1, CONV_WIDTH), full)
    return pl.pallas_call(
        body, grid=(nbatch, nc),
        in_specs=[pl.BlockSpec((tc, CONV_WIDTH), lambda b, c: (b * nc + c, 4)),
                  pl.BlockSpec((tc, CONV_WIDTH), lambda b, c: (b * nc + c, 5)),
                  pl.BlockSpec((CONV_HALO, CONV_WIDTH), full), vec, vec, vec],
        out_specs=[pl.BlockSpec((tc, CONV_WIDTH), row), pl.BlockSpec((tc, CONV_WIDTH), row)],
        out_shape=[jax.ShapeDtypeStruct((t, CONV_WIDTH), F32), jax.ShapeDtypeStruct((t, CONV_WIDTH), BF16)],
        scratch_shapes=[pltpu.VMEM((CONV_HALO + tc, CONV_WIDTH), F32)] * 2, name="conv_fwd",
        compiler_params=_params(("arbitrary", "arbitrary")))(z, z, w, bias, lg, lb)


def _conv_bwd(dsc, cv, z, w, lg, lb, nbatch, seq):
    t = z.shape[0]
    tc = _conv_chunk(seq)
    nc = seq // tc
    hb = tc // CONV_HALO

    def body(dsc_ref, cv_ref, a_ref, g_ref, ap_ref, gp_ref, w_ref, lg_ref, lb_ref,
             da_ref, dg_ref, dw_ref, db_ref, dlg_ref, dlb_ref, ubuf, dbuf, sh):
        b, c = pl.program_id(0), pl.program_id(1)
        ct = nc - 1 - c

        @pl.when((b == 0) & (c == 0))
        def _():
            dw_ref[...] = jnp.zeros_like(dw_ref)
            db_ref[...] = jnp.zeros_like(db_ref)
            dlg_ref[...] = jnp.zeros_like(dlg_ref)
            dlb_ref[...] = jnp.zeros_like(dlb_ref)

        cvv = cv_ref[...]
        mu = jnp.mean(cvv, axis=-1, keepdims=True)
        xc = cvv - mu
        rstd = lax.rsqrt(jnp.mean(xc * xc, axis=-1, keepdims=True) + EPS)
        xh = xc * rstd
        y = xh * lg_ref[...] + lb_ref[...]
        sy = _sig(y)
        dy = dsc_ref[...].astype(F32) * (sy * (1.0 + y * (1.0 - sy)))
        dlg_ref[...] += jnp.sum(dy * xh, axis=0, keepdims=True)
        dlb_ref[...] += jnp.sum(dy, axis=0, keepdims=True)
        dxh = dy * lg_ref[...]
        dcv = rstd * (dxh - jnp.mean(dxh, axis=-1, keepdims=True) - xh * jnp.mean(dxh * xh, axis=-1, keepdims=True))
        db_ref[...] += jnp.sum(dcv, axis=0, keepdims=True)

        @pl.when(c == 0)
        def _():
            dbuf[pl.ds(tc, CONV_HALO), :] = jnp.zeros((CONV_HALO, CONV_WIDTH), F32)

        @pl.when(c > 0)
        def _():
            dbuf[pl.ds(tc, CONV_HALO), :] = dbuf[pl.ds(0, CONV_HALO), :]

        dbuf[pl.ds(0, tc), :] = dcv
        a = a_ref[...].astype(F32)
        sg = _sig(g_ref[...].astype(F32))
        ubuf[pl.ds(0, CONV_HALO), :] = jnp.where(ct > 0, ap_ref[...].astype(F32) * _sig(gp_ref[...].astype(F32)), 0.0)
        ubuf[pl.ds(CONV_HALO, tc), :] = a * sg
        du = jnp.zeros((tc, CONV_WIDTH), F32)
        for k, win in _shifted(dbuf, sh, tc, [CONV_K - 1 - k for k in range(CONV_K)]):
            du = du + w_ref[pl.ds(k, 1), :] * win
        for k, win in _shifted(ubuf, sh, tc, [CONV_HALO - (CONV_K - 1) + k for k in range(CONV_K)]):
            dw_ref[pl.ds(k, 1), :] += jnp.sum(dcv * win, axis=0, keepdims=True)
        da_ref[...] = (du * sg).astype(da_ref.dtype)
        dg_ref[...] = (du * a * sg * (1.0 - sg)).astype(dg_ref.dtype)

    row = lambda b, c: (b * nc + (nc - 1 - c), 0)
    full = lambda b, c: (0, 0)
    vec = pl.BlockSpec((1, CONV_WIDTH), full)
    blk = pl.BlockSpec((tc, CONV_WIDTH), row)

    def zcol(col):
        return pl.BlockSpec((tc, CONV_WIDTH), lambda b, c: (b * nc + (nc - 1 - c), col))

    def zprev(col):
        return pl.BlockSpec((CONV_HALO, CONV_WIDTH),
                            lambda b, c: (jnp.maximum((b * nc + (nc - 1 - c)) * hb - 1, 0), col))

    o = jax.ShapeDtypeStruct((t, CONV_WIDTH), BF16)
    v = jax.ShapeDtypeStruct((1, CONV_WIDTH), F32)
    return pl.pallas_call(
        body, grid=(nbatch, nc),
        in_specs=[blk, blk, zcol(4), zcol(5), zprev(4), zprev(5), pl.BlockSpec((CONV_HALO, CONV_WIDTH), full), vec, vec],
        out_specs=[blk, blk, pl.BlockSpec((CONV_HALO, CONV_WIDTH), full), vec, vec, vec],
        out_shape=[o, o, jax.ShapeDtypeStruct((CONV_HALO, CONV_WIDTH), F32), v, v, v],
        scratch_shapes=[pltpu.VMEM((CONV_HALO + tc, CONV_WIDTH), F32)] * 3, name="conv_bwd", compiler_params=_params(("arbitrary", "arbitrary")))(dsc, cv, z, z, z, z, w, lg, lb)


BIG = ("w_in", "w_attn_out", "w_ssm_glu", "w_conv_out", "w_mix_out", "w_ffn_in", "w_ffn_out", "w_ple_in", "w_ple_gate")
BIG_AXIS = {"w_in": 2, "w_attn_out": 2, "w_ssm_glu": 2, "w_conv_out": 2, "w_mix_out": 1, "w_ffn_in": 2,
            "w_ffn_out": 1, "w_ple_in": 2, "w_ple_gate": 1}
SHARD_MAJOR = ("w_in", "w_ffn_in")
SMALL = ("mix_norm_g", "b_gate", "attn_sinks", "ssm_lambda_re", "ssm_lambda_im", "ssm_log_dt", "ssm_b_re", "ssm_b_im",
         "ssm_c_re", "ssm_c_im", "ssm_d", "b_ssm_glu", "conv_dw_w", "conv_dw_b", "conv_norm_g", "conv_norm_b",
         "ffn_norm_g", "ple_norm_g", "final_norm_g")
WEIGHTS = ("mix_norm_g", "w_in", "b_gate", "attn_sinks", "w_attn_out", "ssm_lambda_re", "ssm_lambda_im", "ssm_log_dt",
           "ssm_b_re", "ssm_b_im", "ssm_c_re", "ssm_c_im", "ssm_d", "w_ssm_glu", "b_ssm_glu", "conv_dw_w", "conv_dw_b",
           "conv_norm_g", "conv_norm_b", "w_conv_out", "w_mix_out", "ffn_norm_g", "w_ffn_in", "w_ffn_out", "w_ple_in",
           "ple_norm_g", "w_ple_gate", "final_norm_g")
SSM_NAMES = ("ssm_lambda_re", "ssm_lambda_im", "ssm_log_dt", "ssm_b_re", "ssm_b_im", "ssm_c_re", "ssm_c_im")


def _heads(v, nh):
    return v.reshape(v.shape[0], nh, HEAD_DIM).transpose(1, 0, 2)


def _tokens(v):
    return v.transpose(1, 0, 2).reshape(v.shape[1], v.shape[0] * HEAD_DIM)


def _row(v):
    return v.reshape(1, -1)


def _layer_fwd(x, p_l, w, s, rope, nbatch, seq, next_shards=None):
    t = x.shape[0]
    tm = 512
    d = D_MODEL
    sv = {}
    sv["x"] = x
    h = _rowwise("rms_mix", _rms_fwd, [R(x), V(_row(s["mix_norm_g"]))], [O(d, BF16)], tm=tm)
    cs = {nm: w[nm].shape[2] for nm in SHARD_MAJOR}
    tb = 1024
    got = {}
    plan = None if next_shards is None else _gather_plan(next_shards, GATHER_A)
    z = _mm("mm_in", h, w["w_in"], "nn", BF16, m=t, n=N_CHIPS * cs["w_in"], k=d, tm=tb, tn=cs["w_in"], tk=d,
            b_sh=cs["w_in"], comm=plan)
    if plan is not None:
        z, outs = z
        got.update(zip(plan["names"], outs))
    sv["h"], sv["z"] = h, z
    c, sa, sb = rope
    qkv_w = Q_WIDTH + 2 * KV_WIDTH
    qkv = _rowwise("rope_fwd", _rope_fwd, [R(z, Q_WIDTH, 0), R(z, KV_WIDTH, 4), R(z, KV_WIDTH, 5), R(c), R(sa), R(sb)],
                   [O(qkv_w, BF16)], tm=tm)
    qkv = _heads(qkv, qkv_w // HEAD_DIM)
    sinks = s["attn_sinks"].reshape(N_Q_HEADS, 1, 1)
    oh, lse = _attn_fwd(qkv, sinks, nbatch, seq)
    o = _tokens(oh)
    ya = _mm("mm_attn_out", o, w["w_attn_out"], "nn", BF16, m=t, n=d, k=Q_WIDTH, tm=tb, tn=d, tk=Q_WIDTH)
    sv.update(qkv=qkv, oh=oh, lse=lse, o=o, ya=ya, sinks=sinks)
    ssm_args = [s[nm] for nm in SSM_NAMES]
    a_re, a_im, wb, wc = _ssm_prep(*ssm_args)
    dskip = _row(s["ssm_d"])
    st, ys, gel = _ssm_fwd(z, wb.astype(BF16), wc.astype(BF16), _ssm_tables(a_re, a_im, False), dskip, nbatch, seq)
    glu = _mm("mm_glu", gel, w["w_ssm_glu"], "nn", BF16, m=t, n=2 * d, k=SSM_WIDTH, tm=tb, tn=2 * d, tk=SSM_WIDTH,
              bias=_row(s["b_ssm_glu"]))
    sv.update(st=st, ys=ys, gel=gel, glu=glu, a=(a_re, a_im), wb=wb, wc=wc, dskip=dskip)
    cw = jnp.pad(s["conv_dw_w"], ((0, CONV_HALO - CONV_K), (0, 0)))
    cv, sc = _conv_fwd(z, cw, _row(s["conv_dw_b"]), _row(s["conv_norm_g"]), _row(s["conv_norm_b"]), nbatch, seq)
    yc = _mm("mm_conv_out", sc, w["w_conv_out"], "nn", BF16, m=t, n=d, k=CONV_WIDTH, tm=tb, tn=d, tk=CONV_WIDTH)
    sv.update(cw=cw, cv=cv, sc=sc, yc=yc)
    bg = _row(s["b_gate"])
    merge_ins = [R(z, 512, 3), R(z, 512, 5), R(z, 512, 7), V(bg, 512, 0), V(bg, 512, 2), V(bg, 512, 4),
                 R(ya, 512, 0), R(glu, 512, 0), R(glu, 512, 2), R(yc, 512, 0)]
    merged = _rowwise("merge_fwd", _merge_fwd, merge_ins, [O(512, BF16, total=d)], tm=tm, ncol=2)
    x1 = _mm("mm_mix", merged, w["w_mix_out"], "nn", F32, m=t, n=d, k=d, tm=tb, tn=d, tk=d, res=x)
    sv.update(merged=merged, x1=x1)
    hf = _rowwise("rms_ffn", _rms_fwd, [R(x1), V(_row(s["ffn_norm_g"]))], [O(d, BF16)], tm=tm)
    plan = None if next_shards is None else _gather_plan(next_shards, GATHER_B)
    f = _mm("mm_ffn_in", hf, w["w_ffn_in"], "nn", BF16, m=t, n=2 * FFN_HIDDEN, k=d, tm=tb, tn=cs["w_ffn_in"], tk=d,
            b_sh=cs["w_ffn_in"], comm=plan)
    if plan is not None:
        f, outs = f
        got.update(zip(plan["names"], outs))
    act = (lambda i, j, kk, fg, fu: _ffn_act(fg, fu), [(f, lambda i, j, kk: (i, 0)), (f, lambda i, j, kk: (i, 1))])
    x2, act = _mm("mm_ffn_out", act, w["w_ffn_out"], "nn", F32, m=t, n=d, k=FFN_HIDDEN, tm=256, tn=d, tk=FFN_HIDDEN,
                  res=x1, a_keep=True)
    sv.update(hf=hf, f=f, act=act, x2=x2)
    e = _mm("mm_ple_in", p_l, w["w_ple_in"], "nn", BF16, m=t, n=d, k=p_l.shape[1], tm=tb, tn=d, tk=p_l.shape[1])
    hp = _rowwise("rms_ple", _rms_fwd, [R(x2), V(_row(s["ple_norm_g"]))], [O(d, BF16)], tm=tm)
    gp = _mm("mm_ple_gate", hp, w["w_ple_gate"], "nn", BF16, m=t, n=d, k=d, tm=tb, tn=d, tk=d)
    x3 = _rowwise("ple_fwd", _ple_fwd, [R(x2), R(gp), R(e)], [O(d, F32)], tm=tm)
    sv.update(e=e, hp=hp, gp=gp, p=p_l)
    return x3, sv, got


def _layer_bwd(dx3, sv, w, s, rope, nbatch, seq):
    t = dx3.shape[0]
    tm = 512
    d = D_MODEL
    gb, gs = {}, {}
    cs = {nm: w[nm].shape[2] for nm in SHARD_MAJOR}
    tb = 1024

    def wg(name, a, b, m, n, tm=1024, tk=1024, shard=None):
        return _mm(name, a, b, "tn", BF16, m=m, n=n, k=t, tm=tm, tn=n if shard is None else cs[shard], tk=tk,
                   o_sh=None if shard is None else cs[shard])

    de, dgp = _rowwise("ple_bwd", _ple_bwd, [R(dx3), R(sv["gp"]), R(sv["e"])], [O(d, BF16), O(d, BF16)], tm=tm)
    gb["w_ple_in"] = wg("wg_ple_in", sv["p"], de, sv["p"].shape[1], d, tk=2048)
    gb["w_ple_gate"] = wg("wg_ple_gate", sv["hp"], dgp, d, d, tk=2048)
    dhp = _mm("mmb_ple_gate", dgp, w["w_ple_gate"], "nt", BF16, m=t, n=d, k=d, tm=tb, tn=d, tk=d)
    dx2, gs["ple_norm_g"] = _rowwise("rms_ple_bwd", _rms_bwd, [R(dhp), R(sv["x2"]), R(dx3), V(_row(s["ple_norm_g"]))],
                                     [O(d, F32)], [A(d)], tm=tm)
    fw = FFN_HIDDEN // 2
    dact = _mm("mmb_ffn_out", dx2, w["w_ffn_out"], "nt", BF16, m=t, n=FFN_HIDDEN, k=d, tm=tb, tn=fw, tk=d)
    gb["w_ffn_out"] = wg("wg_ffn_out", sv["act"], dx2, FFN_HIDDEN, d, tm=fw)
    f = sv["f"]

    def df_tile(is_gate, da, fg, fu):
        dfg, dfu = _ffn_act_bwd(da, fg, fu)
        return jnp.where(is_gate, dfg, dfu)

    assert cs["w_ffn_in"] == fw
    df_rows = (lambda i, j, kk, *v: df_tile(kk < 2, *v),
               [(dact, lambda i, j, kk: (i, kk % 2)), (f, lambda i, j, kk: (i, kk % 2)), (f, lambda i, j, kk: (i, 2 + kk % 2))])
    dhf, df = _mm("mmb_ffn_in", df_rows, w["w_ffn_in"], "nt", BF16, m=t, n=d, k=2 * FFN_HIDDEN, tm=512, tn=d, tk=fw,
                  b_sh=fw, a_keep=True)
    gb["w_ffn_in"] = wg("wg_ffn_in", sv["hf"], df, d, 2 * FFN_HIDDEN, shard="w_ffn_in")
    dx1, gs["ffn_norm_g"] = _rowwise("rms_ffn_bwd", _rms_bwd, [R(dhf), R(sv["x1"]), R(dx2), V(_row(s["ffn_norm_g"]))],
                                     [O(d, F32)], [A(d)], tm=tm)
    dm = _mm("mmb_mix", dx1, w["w_mix_out"], "nt", BF16, m=t, n=d, k=d, tm=tb, tn=d, tk=d)
    gb["w_mix_out"] = wg("wg_mix", sv["merged"], dx1, d, d)
    z, glu, bg = sv["z"], sv["glu"], _row(s["b_gate"])
    ins = [R(dm, 512, 0), R(z, 512, 3), R(z, 512, 5), R(z, 512, 7), V(bg, 512, 0), V(bg, 512, 2), V(bg, 512, 4),
           R(sv["ya"], 512, 0), R(glu, 512, 0), R(glu, 512, 2), R(sv["yc"], 512, 0)]
    ob = lambda: O(512, BF16, total=d)
    ab = lambda: A(512, total=d)
    dya, dga, dgb, dyc, d0, d1, d2, db0, db1, db2, dba, dbb = _rowwise(
        "merge_bwd", _merge_bwd, ins, [ob() for _ in range(7)], [ab() for _ in range(5)], tm=tm, ncol=2)
    gs["b_gate"] = jnp.concatenate([db0, db1, db2], axis=1)
    gs["b_ssm_glu"] = jnp.concatenate([dba, dbb], axis=1)
    dglu = jnp.concatenate([dga, dgb], axis=1)
    gb["w_attn_out"] = wg("wg_attn_out", sv["o"], dya, Q_WIDTH, d, tk=2048)
    do = _mm("mmb_attn_out", dya, w["w_attn_out"], "nt", BF16, m=t, n=Q_WIDTH, k=d, tm=tb, tn=Q_WIDTH, tk=d)
    dqkv, dsink = _attn_bwd(sv["qkv"], sv["oh"], _heads(do, N_Q_HEADS), sv["lse"], sv["sinks"], nbatch, seq)
    dqkv = _tokens(dqkv)
    gs["attn_sinks"] = dsink.reshape(-1)
    c, sa, sb = rope
    dq = _rowwise("rope_bwd_q", _rope_bwd_q, [R(dqkv, Q_WIDTH, 0), R(c), R(sa), R(sb)], [O(Q_WIDTH, BF16)], tm=tm)
    dk, dv = _kv_combine(dqkv, c, sa, sb, seq)
    gb["w_ssm_glu"] = wg("wg_ssm_glu", sv["gel"], dglu, SSM_WIDTH, 2 * d, tk=2048)
    dgi = _mm("mmb_glu", dglu, w["w_ssm_glu"], "nt", BF16, m=t, n=SSM_WIDTH, k=2 * d, tm=tb, tn=SSM_WIDTH, tk=2 * d)
    a_re, a_im = sv["a"]
    du, dwb, dwc, dd, da = _ssm_bwd(dgi, sv["ys"], sv["st"], z, sv["wb"].T.astype(BF16), sv["wc"].T.astype(BF16),
                                    _ssm_tables(a_re, a_im, True), sv["dskip"], nbatch, seq)
    gs["ssm_d"] = dd.reshape(-1)
    da = jnp.sum(da, axis=0)
    _, prep_vjp = jax.vjp(_ssm_prep, *[s[nm] for nm in SSM_NAMES])
    for nm, g in zip(SSM_NAMES, prep_vjp((da[:SSM_LANES], da[SSM_LANES:], dwb, dwc))):
        gs[nm] = g
    gb["w_conv_out"] = wg("wg_conv_out", sv["sc"], dyc, CONV_WIDTH, d, tk=2048)
    dsc = _mm("mmb_conv_out", dyc, w["w_conv_out"], "nt", BF16, m=t, n=CONV_WIDTH, k=d, tm=tb, tn=CONV_WIDTH, tk=d)
    dca, dcg, dcw, dcb, dlg, dlb = _conv_bwd(dsc, sv["cv"], z, sv["cw"], _row(s["conv_norm_g"]),
                                             _row(s["conv_norm_b"]), nbatch, seq)
    gs["conv_dw_w"] = dcw[:CONV_K]
    gs["conv_dw_b"], gs["conv_norm_g"], gs["conv_norm_b"] = dcb.reshape(-1), dlg.reshape(-1), dlb.reshape(-1)
    dz = jnp.concatenate([dq, dk, dv, du, dca, dcg, d0, d1, d2], axis=1)
    gb["w_in"] = wg("wg_in", sv["h"], dz, d, dz.shape[1], tk=2048, shard="w_in")
    dh = _mm("mmb_in", dz, w["w_in"], "nt", BF16, m=t, n=d, k=dz.shape[1], tm=tb, tn=d, tk=cs["w_in"],
             b_sh=cs["w_in"])
    dx, gs["mix_norm_g"] = _rowwise("rms_mix_bwd", _rms_bwd, [R(dh), R(sv["x"]), R(dx1), V(_row(s["mix_norm_g"]))],
                                    [O(d, F32)], [A(d)], tm=tm)
    gs["mix_norm_g"], gs["ffn_norm_g"], gs["ple_norm_g"] = (gs[nm].reshape(-1) for nm in
                                                            ("mix_norm_g", "ffn_norm_g", "ple_norm_g"))
    gs["b_gate"], gs["b_ssm_glu"] = gs["b_gate"].reshape(-1), gs["b_ssm_glu"].reshape(-1)
    return dx, {nm: _shard_major(nm, g) for nm, g in gb.items()}, gs


def _rope_tables(positions):
    inv_freq = ROPE_THETA ** (-jnp.arange(0, ROPE_DIM, 2, dtype=F32) / ROPE_DIM)
    ang = positions.reshape(-1).astype(F32)[:, None] * inv_freq
    cos, sin = jnp.cos(ang), jnp.sin(ang)
    t = ang.shape[0]
    rest = HEAD_DIM - ROPE_DIM
    c = jnp.concatenate([cos, cos, jnp.ones((t, rest), F32)], axis=1)
    sa = jnp.concatenate([-sin, jnp.zeros((t, HEAD_DIM - ROPE_HALF), F32)], axis=1)
    sb = jnp.concatenate([jnp.zeros((t, ROPE_HALF), F32), sin, jnp.zeros((t, rest), F32)], axis=1)
    two = lambda v: jnp.concatenate([v, v], axis=1)
    return two(c), two(sa), two(sb)


def _natural(nm, w4):
    if nm in SHARD_MAJOR:
        return w4
    if BIG_AXIS[nm] == 1:
        return w4.reshape(-1, w4.shape[2])
    return w4.transpose(1, 0, 2).reshape(w4.shape[1], -1)


def _shard_major(nm, g):
    if nm in SHARD_MAJOR:
        return g
    if BIG_AXIS[nm] == 1:
        return g.reshape(N_CHIPS, -1, g.shape[1])
    return g.reshape(g.shape[0], N_CHIPS, -1).transpose(1, 0, 2)


def _untap(taps4, cols):
    flat = taps4.reshape(N_CHIPS, -1)[:, :CONV_K * cols]
    return flat.reshape(N_CHIPS, CONV_K, cols).transpose(1, 0, 2).reshape(CONV_K, N_CHIPS * cols)


def _local_step(x, p, positions, loss_target, small, wfull=None, shards=None):
    nbatch, seq, d = x.shape
    depth = p.shape[0]
    t = nbatch * seq
    rope = _rope_tables(positions)
    xs = x.reshape(t, d)
    saved, ws, ss = [], [], []
    got = None if shards is None else _gather_now((shards, 0))
    for l in range(depth):
        w4 = {nm: wfull[nm][l] for nm in BIG} if shards is None else got
        w_l = {nm: _natural(nm, w4[nm]) for nm in BIG}
        s_l = {nm: small[nm][l] for nm in small if nm != "final_norm_g"}
        if shards is not None:
            s_l["conv_dw_w"] = _untap(got[TAPS], CONV_WIDTH // N_CHIPS)
        nxt = (shards, l + 1) if shards is not None and l + 1 < depth else None
        xs, sv, got = _layer_fwd(xs, p[l].reshape(t, -1), w_l, s_l, rope, nbatch, seq, nxt)
        saved.append(sv)
        ws.append(w_l)
        ss.append(s_l)
    dx, loss_cols, dgf = _rowwise("loss_head", _loss_fn, [R(xs), R(loss_target.reshape(t, d)),
                                                          V(_row(small["final_norm_g"]))],
                                  [O(d, F32)], [A(d), A(d)], tm=512)
    gbs, gss = [None] * depth, [None] * depth
    for l in reversed(range(depth)):
        dx, gbs[l], gss[l] = _layer_bwd(dx, saved[l], ws[l], ss[l], rope, nbatch, seq)
    gbig = {nm: jnp.stack([g[nm] for g in gbs]) for nm in BIG}
    gsmall = {nm: jnp.stack([g[nm] for g in gss]) for nm in SMALL if nm != "final_norm_g"}
    gsmall["final_norm_g"] = dgf.reshape(-1)
    return loss_cols, dx.reshape(nbatch, seq, d), gbig, gsmall


HBM = pl.BlockSpec(memory_space=pltpu.HBM)


def _place():
    x, y, c = lax.axis_index("x"), lax.axis_index("y"), lax.axis_index("c")
    chips = [(1 - x, y), (x, 1 - y), (1 - x, 1 - y)]
    return x, y, c, chips


def _remote(src, dst, send_sem, recv_sem, to):
    return pltpu.make_async_remote_copy(src_ref=src, dst_ref=dst, send_sem=send_sem, recv_sem=recv_sem,
                                        device_id=to, device_id_type=MESH)


TAPS = "taps"
GATHER_ALL = (("w_ffn_in", "w_ffn_out"),
              ("w_in", "w_ple_gate", "w_mix_out", "w_attn_out", "w_ssm_glu", "w_conv_out", "w_ple_in", TAPS))
GATHER_A = (("w_ffn_in",), ("w_in", "w_ple_gate"))
GATHER_B = (("w_ffn_out",), ("w_mix_out", "w_attn_out", "w_ssm_glu", "w_conv_out", "w_ple_in", TAPS))


def _gather_plan(shards, sets):
    stacked, layer = shards
    names = sets[0] + sets[1]
    n = len(names)
    idx = {nm: i for i, nm in enumerate(names)}

    def start(ins, outs, sems):
        send1, recv1, _, _, send0, recv0 = sems
        x, y, c, chips = _place()
        me = 2 * x + y
        for i in range(n):
            _remote(ins[i].at[layer], outs[i].at[me], send0.at[i], recv0.at[i], (x, y, 1 - c)).start()
        for role in (0, 1):
            @pl.when(c == role)
            def _():
                for nm in sets[role]:
                    i = idx[nm]
                    for k, (cx, cy) in enumerate(chips):
                        _remote(ins[i].at[layer], outs[i].at[me], send1.at[i, k], recv1.at[i, k], (cx, cy, c)).start()

    def finish(ins, outs, sems):
        send1, recv1, send2, recv2, send0, recv0 = sems
        x, y, c, chips = _place()
        me = 2 * x + y
        sib = (x, y, 1 - c)
        for role in (0, 1):
            @pl.when(c == role)
            def _():
                passed = []
                for nm in sets[role]:
                    i = idx[nm]
                    for k, (cx, cy) in enumerate(chips):
                        slot = outs[i].at[2 * cx + cy]
                        _remote(slot, slot, send1.at[i, k], recv1.at[i, k], (cx, cy, c)).wait_recv()
                        cp = _remote(slot, slot, send2.at[i, k], recv2.at[i, k], sib)
                        cp.start()
                        passed.append(cp)
                for nm in sets[1 - role]:
                    i = idx[nm]
                    for k, (cx, cy) in enumerate(chips):
                        slot = outs[i].at[2 * cx + cy]
                        _remote(slot, slot, send2.at[i, k], recv2.at[i, k], sib).wait_recv()
                for nm in sets[role]:
                    i = idx[nm]
                    for k, (cx, cy) in enumerate(chips):
                        _remote(ins[i].at[layer], outs[i].at[me], send1.at[i, k], recv1.at[i, k],
                                (cx, cy, c)).wait_send()
                for cp in passed:
                    cp.wait_send()
        for i in range(n):
            _remote(ins[i].at[layer], outs[i].at[me], send0.at[i], recv0.at[i], sib).wait()

    ins = [stacked[nm] for nm in names]
    return dict(names=names, ins=ins, start=start, finish=finish,
                out_shapes=[jax.ShapeDtypeStruct((N_CHIPS,) + v.shape[1:], v.dtype) for v in ins],
                sems=[pltpu.SemaphoreType.DMA((n, 3)) for _ in range(4)] + [pltpu.SemaphoreType.DMA((n,))
                                                                            for _ in range(2)])


def _gather_now(shards):
    return _comm_now("gather_weights", _gather_plan(shards, GATHER_ALL))


def _pair_exchange(grads):
    n = len(grads)
    hl = grads[0].shape[0] // 2

    def body(*refs):
        ins, outs = refs[:n], refs[n:2 * n]
        send, recv = refs[2 * n:]
        x, y, c, _ = _place()
        other = pl.ds((1 - c) * hl, hl)
        cps = [_remote(ins[i].at[other], outs[i], send.at[i], recv.at[i], (x, y, 1 - c)) for i in range(n)]
        for cp in cps:
            cp.start()
        for cp in cps:
            cp.wait()

    out_shape = [jax.ShapeDtypeStruct((hl,) + g.shape[1:], g.dtype) for g in grads]
    sems = [pltpu.SemaphoreType.DMA((n,)) for _ in range(2)]
    return pl.pallas_call(body, out_shape=out_shape, in_specs=[HBM] * n, out_specs=[HBM] * n, scratch_shapes=sems,
                          name="reduce_pair_exchange")(*grads)


def _pair_add(g, r):
    hl, _, rr, cc = r.shape
    rows = hl * N_CHIPS * rr
    nblk = rows // rr

    def body(c_ref, g_ref, r_ref, o_ref):
        o_ref[...] = (g_ref[...].astype(F32) + r_ref[...].astype(F32)).astype(o_ref.dtype)

    grid_spec = pltpu.PrefetchScalarGridSpec(
        num_scalar_prefetch=1, grid=(nblk,),
        in_specs=[pl.BlockSpec((rr, cc), lambda i, c_ref: (c_ref[0] * nblk + i, 0)),
                  pl.BlockSpec((rr, cc), lambda i, c_ref: (i, 0))],
        out_specs=pl.BlockSpec((rr, cc), lambda i, c_ref: (i, 0)))
    c = lax.axis_index("c").astype(jnp.int32).reshape(1)
    out = pl.pallas_call(body, out_shape=jax.ShapeDtypeStruct((rows, cc), r.dtype), grid_spec=grid_spec,
                         name="reduce_pair_add", compiler_params=_params(("parallel",)))(
        c, g.reshape(-1, cc), r.reshape(rows, cc))
    return out.reshape(r.shape)


def _chip_exchange(psums):
    n = len(psums)

    def body(*refs):
        ins, got = refs[:n], refs[n:2 * n]
        send, recv = refs[2 * n:]
        x, y, c, chips = _place()
        cps = [_remote(ins[i].at[:, 2 * cx + cy], got[i].at[k], send.at[i, k], recv.at[i, k], (cx, cy, c))
               for i in range(n) for k, (cx, cy) in enumerate(chips)]
        for cp in cps:
            cp.start()
        for cp in cps:
            cp.wait()

    got_shape = [jax.ShapeDtypeStruct((3, p.shape[0]) + p.shape[2:], p.dtype) for p in psums]
    sems = [pltpu.SemaphoreType.DMA((n, 3)), pltpu.SemaphoreType.DMA((n, 3))]
    return pl.pallas_call(body, out_shape=got_shape, in_specs=[HBM] * n, out_specs=[HBM] * n, scratch_shapes=sems,
                          name="reduce_chip_exchange")(*psums)


def _comm_now(name, plan):
    n = len(plan["ins"])

    def body(*refs):
        ins, outs, sems = refs[:n], refs[n:2 * n], refs[2 * n:]
        plan["start"](ins, outs, sems)
        plan["finish"](ins, outs, sems)

    outs = pl.pallas_call(body, out_shape=plan["out_shapes"], in_specs=[HBM] * n, out_specs=[HBM] * n,
                          scratch_shapes=plan["sems"], name=name)(*plan["ins"])
    return dict(zip(plan["names"], outs))


def _sum4(psum, got):
    hl, _, rr, cc = psum.shape
    tr = rr if rr * cc <= 512 * 1024 else rr // 2

    def body(place_ref, own_ref, g0_ref, g1_ref, g2_ref, o_ref):
        tot = (own_ref[...].astype(F32) + g0_ref[...].astype(F32)) + g1_ref[...].astype(F32)
        o_ref[...] = tot + g2_ref[...].astype(F32)

    def got_spec(k):
        return pl.BlockSpec((None, None, tr, cc), lambda h, i, place: (k, h, i, 0))

    grid_spec = pltpu.PrefetchScalarGridSpec(
        num_scalar_prefetch=1, grid=(hl, rr // tr),
        in_specs=[pl.BlockSpec((None, None, tr, cc), lambda h, i, place: (h, place[0], i, 0)),
                  got_spec(0), got_spec(1), got_spec(2)],
        out_specs=pl.BlockSpec((None, tr, cc), lambda h, i, place: (place[1] * hl + h, i, 0)))
    place = jnp.stack([2 * lax.axis_index("x") + lax.axis_index("y"), lax.axis_index("c")]).astype(jnp.int32)
    return pl.pallas_call(body, out_shape=jax.ShapeDtypeStruct((2 * hl, rr, cc), F32), grid_spec=grid_spec,
                          name="reduce_sum4", compiler_params=_params(("parallel", "parallel")))(
        place, psum, got, got, got)


def _pair_gather(sums):
    n = len(sums)
    hl = sums[0].shape[0] // 2

    def body(*refs):
        bufs = refs[n:2 * n]
        send, recv = refs[2 * n:]
        x, y, c, _ = _place()
        mine = pl.ds(c * hl, hl)
        cps = [_remote(bufs[i].at[mine], bufs[i].at[mine], send.at[i], recv.at[i], (x, y, 1 - c)) for i in range(n)]
        for cp in cps:
            cp.start()
        for cp in cps:
            cp.wait()

    out_shape = [jax.ShapeDtypeStruct(v.shape, v.dtype) for v in sums]
    sems = [pltpu.SemaphoreType.DMA((n,)) for _ in range(2)]
    return pl.pallas_call(body, out_shape=out_shape, in_specs=[HBM] * n, out_specs=[HBM] * n, scratch_shapes=sems,
                          input_output_aliases={i: i for i in range(n)}, name="reduce_pair_gather")(*sums)


def _allreduce_small(vec):
    rows = vec.shape[0]

    def body(v_ref, o_ref, all_ref, send, recv):
        x, y, c, _ = _place()
        me = 4 * x + 2 * y + c
        all_ref[me] = v_ref[...]
        cps = []
        for dlt in range(1, N_DEV):
            fx, fy, fc = (dlt >> 2) & 1, (dlt >> 1) & 1, dlt & 1
            to = (1 - x if fx else x, 1 - y if fy else y, 1 - c if fc else c)
            cps.append(_remote(v_ref, all_ref.at[me], send.at[dlt - 1], recv.at[dlt - 1], to))
        for cp in cps:
            cp.start()
        for cp in cps:
            cp.wait()
        tot = all_ref[0]
        for dev in range(1, N_DEV):
            tot = tot + all_ref[dev]
        o_ref[...] = tot

    vm = pl.BlockSpec(memory_space=pltpu.VMEM)
    return pl.pallas_call(
        body, out_shape=jax.ShapeDtypeStruct(vec.shape, F32), in_specs=[vm], out_specs=vm,
        scratch_shapes=[pltpu.VMEM((N_DEV, rows, 128), F32), pltpu.SemaphoreType.DMA((N_DEV - 1,)),
                        pltpu.SemaphoreType.DMA((N_DEV - 1,))],
        name="allreduce_small", compiler_params=pltpu.CompilerParams(vmem_limit_bytes=VMEM_LIMIT))(vec)


def _adamw(name, w, g, m, v):
    rows, cc = w.shape
    tm = rows if rows * cc <= 512 * 1024 else math.gcd(rows, 256)
    return _rowwise(name, _adamw_fn, [R(w), R(g), R(m), R(v)], [O(cc, F32), O(cc, F32), O(cc, F32)], tm=tm)


def _pack(parts):
    flat = jnp.concatenate([v.reshape(-1).astype(F32) for v in parts])
    pad = (-flat.shape[0]) % (SUBLANES * 128)
    return jnp.pad(flat, (0, pad)).reshape(-1, 128)


def _unpack(packed, shapes):
    flat, out, pos = packed.reshape(-1), [], 0
    for shp in shapes:
        size = math.prod(shp)
        out.append(flat[pos:pos + size].reshape(shp))
        pos += size
    return out


def kernel(x, p, positions, mix_norm_g, w_in, b_gate, attn_sinks, w_attn_out, ssm_lambda_re, ssm_lambda_im, ssm_log_dt, ssm_b_re, ssm_b_im, ssm_c_re, ssm_c_im, ssm_d, w_ssm_glu, b_ssm_glu, conv_dw_w, conv_dw_b, conv_norm_g, conv_norm_b, w_conv_out, w_mix_out, ffn_norm_g, w_ffn_in, w_ffn_out, w_ple_in, ple_norm_g, w_ple_gate, final_norm_g, loss_target, m_mix_norm_g, m_w_in, m_b_gate, m_attn_sinks, m_w_attn_out, m_ssm_lambda_re, m_ssm_lambda_im, m_ssm_log_dt, m_ssm_b_re, m_ssm_b_im, m_ssm_c_re, m_ssm_c_im, m_ssm_d, m_w_ssm_glu, m_b_ssm_glu, m_conv_dw_w, m_conv_dw_b, m_conv_norm_g, m_conv_norm_b, m_w_conv_out, m_w_mix_out, m_ffn_norm_g, m_w_ffn_in, m_w_ffn_out, m_w_ple_in, m_ple_norm_g, m_w_ple_gate, m_final_norm_g, v_mix_norm_g, v_w_in, v_b_gate, v_attn_sinks, v_w_attn_out, v_ssm_lambda_re, v_ssm_lambda_im, v_ssm_log_dt, v_ssm_b_re, v_ssm_b_im, v_ssm_c_re, v_ssm_c_im, v_ssm_d, v_w_ssm_glu, v_b_ssm_glu, v_conv_dw_w, v_conv_dw_b, v_conv_norm_g, v_conv_norm_b, v_w_conv_out, v_w_mix_out, v_ffn_norm_g, v_w_ffn_in, v_w_ffn_out, v_w_ple_in, v_ple_norm_g, v_w_ple_gate, v_final_norm_g):
    given = dict(locals())
    wts = {nm: given[nm] for nm in WEIGHTS}
    mom = {nm: given["m_" + nm] for nm in WEIGHTS}
    var = {nm: given["v_" + nm] for nm in WEIGHTS}
    depth = p.shape[0]
    chip = 2 * lax.axis_index("x") + lax.axis_index("y")

    cw_cols = conv_dw_w.shape[2]
    taps = jnp.pad(conv_dw_w.reshape(depth, -1), ((0, 0), (0, (-CONV_K * cw_cols) % (SUBLANES * 128))))
    shards = {**{nm: wts[nm].astype(BF16) for nm in BIG}, TAPS: taps.reshape(depth, -1, 128)}
    small = {nm: wts[nm] for nm in SMALL if nm != "conv_dw_w"}

    loss_cols, grad_x, gbig, gsmall = _local_step(x, p, positions, loss_target, small, shards=shards)

    parts = [loss_cols] + [gsmall[nm] for nm in SMALL]
    total = _allreduce_small(_pack(parts))
    summed = _unpack(total, [v.shape for v in parts])
    loss = jnp.sum(summed[0])
    gsum = dict(zip(SMALL, summed[1:]))
    gsum["conv_dw_w"] = lax.dynamic_slice_in_dim(gsum["conv_dw_w"], chip * cw_cols, cw_cols, axis=2)
    shapes = [wts[nm].shape for nm in SMALL]
    deltas, new_m, new_v = _adamw("adamw_small", _pack([wts[nm] for nm in SMALL]), _pack([gsum[nm] for nm in SMALL]),
                                  _pack([mom[nm] for nm in SMALL]), _pack([var[nm] for nm in SMALL]))
    grads = dict(gsum)
    delta = dict(zip(SMALL, _unpack(deltas, shapes)))
    newm = dict(zip(SMALL, _unpack(new_m, shapes)))
    newv = dict(zip(SMALL, _unpack(new_v, shapes)))

    gl = [gbig[nm] for nm in BIG]
    sib = _pair_exchange(gl)
    psums = [_pair_add(g, r) for g, r in zip(gl, sib)]
    got = _chip_exchange(psums)
    sums = _pair_gather([_sum4(ps, g) for ps, g in zip(psums, got)])
    for nm, g in zip(BIG, sums):
        shp = wts[nm].shape
        two = lambda v: v.reshape(-1, shp[-1])
        g = g.reshape(shp)
        d_w, n_m, n_v = _adamw("adamw_" + nm, two(wts[nm]), two(g), two(mom[nm]), two(var[nm]))
        grads[nm], delta[nm], newm[nm], newv[nm] = g, d_w.reshape(shp), n_m.reshape(shp), n_v.reshape(shp)

    return (loss, grad_x, *[grads[nm] for nm in WEIGHTS], *[delta[nm] for nm in WEIGHTS],
            *[newm[nm] for nm in WEIGHTS], *[newv[nm] for nm in WEIGHTS])
```

```python
import functools
import math

import jax
import jax.numpy as jnp
from jax import lax
from jax.experimental import pallas as pl
from jax.experimental.pallas import tpu as pltpu

F32 = jnp.float32
BF16 = jnp.bfloat16

D_MODEL = 1024
HEAD_DIM = 64
N_Q_HEADS = 8
N_KV_HEADS = 2
GQA_GROUP = N_Q_HEADS // N_KV_HEADS
ATT_BLOCK = 128
ROPE_THETA = 500000.0
ROPE_DIM = HEAD_DIM // 4
ROPE_HALF = ROPE_DIM // 2
Q_WIDTH = N_Q_HEADS * HEAD_DIM
KV_WIDTH = N_KV_HEADS * HEAD_DIM
SSM_WIDTH = 256
SSM_GROUP = 16
SSM_GROUPS = 16
SSM_STATE = 64
SSM_LANES = SSM_GROUPS * SSM_STATE
CONV_WIDTH = 256
CONV_K = 31
CONV_HALO = 32
FFN_HIDDEN = 2816
EPS = 1e-6
NEG_INF = -1e30
SCALE = HEAD_DIM ** -0.5

ADAM_LR = 0.001
ADAM_B1 = 0.9
ADAM_B2 = 0.999
ADAM_EPS = 1e-08
ADAM_WD = 0.01
ADAM_STEP = 10

N_CHIPS = 4
N_DEV = 8
SUBLANES = 8
VMEM_LIMIT = 56 * 1024 * 1024

MESH = pl.DeviceIdType.MESH


def _params(sem=None):
    return pltpu.CompilerParams(dimension_semantics=sem, vmem_limit_bytes=VMEM_LIMIT)


def R(arr, width=None, cb=0, rb=0):
    return ("r", arr, arr.shape[1] if width is None else width, (cb, rb))


def V(arr, width=None, cb=0):
    return ("v", arr, arr.shape[1] if width is None else width, cb)


def _cbf(cb):
    return cb if callable(cb) else (lambda j, c=cb: c + j)


def _rowwise(name, fn, ins, outs, accs=(), *, tm, ncol=1):
    t = [a for k, a, _, _ in ins if k == "r"][0].shape[0]
    tm = min(tm, t)
    assert t % tm == 0, (name, t, tm)
    n_i, n_o, n_a = len(ins), len(outs), len(accs)

    def body(*refs):
        vals = fn(*[r[...] for r in refs[:n_i]])
        if not isinstance(vals, (tuple, list)):
            vals = (vals,)
        for ref, val in zip(refs[n_i:n_i + n_o], vals[:n_o]):
            ref[...] = val.astype(ref.dtype)
        if n_a:
            acc_refs = refs[n_i + n_o:]

            @pl.when(pl.program_id(1) == 0)
            def _():
                for ref in acc_refs:
                    ref[...] = jnp.zeros_like(ref)

            for ref, val in zip(acc_refs, vals[n_o:]):
                ref[...] += val

    in_specs = []
    for kind, arr, width, cb in ins:
        if kind == "r":
            f = _cbf(cb[0])
            in_specs.append(pl.BlockSpec((tm, width), functools.partial(lambda j, i, f, rb: (i + rb, f(j)), f=f, rb=cb[1])))
        else:
            f = _cbf(cb)
            in_specs.append(pl.BlockSpec((arr.shape[0], width), functools.partial(lambda j, i, f: (0, f(j)), f=f)))
    out_specs, out_shape = [], []
    for total, width, cb, dt in outs:
        f = _cbf(cb)
        out_specs.append(pl.BlockSpec((tm, width), functools.partial(lambda j, i, f: (i, f(j)), f=f)))
        out_shape.append(jax.ShapeDtypeStruct((t, total), dt))
    for total, width, cb in accs:
        f = _cbf(cb)
        out_specs.append(pl.BlockSpec((1, width), functools.partial(lambda j, i, f: (0, f(j)), f=f)))
        out_shape.append(jax.ShapeDtypeStruct((1, total), F32))
    sem = ("arbitrary", "arbitrary") if n_a else ("parallel", "parallel")
    res = pl.pallas_call(body, out_shape=out_shape, grid=(ncol, t // tm), in_specs=in_specs, out_specs=out_specs,
                         name=name, compiler_params=_params(sem))(*[a for _, a, _, _ in ins])
    return res[0] if len(res) == 1 else res


def O(width, dtype, total=None, cb=0):
    return (width if total is None else total, width, cb, dtype)


def A(width, total=None, cb=0):
    return (width if total is None else total, width, cb)


_DIMS = {"nn": (((1,), (0,)), ((), ())), "nt": (((1,), (1,)), ((), ())), "tn": (((0,), (0,)), ((), ()))}


def _mm(name, a, b, mode, out_dtype, *, m, n, k, tm, tn, tk, a_off=0, b_off=0, res=None, bias=None, b_sh=None, o_sh=None,
        comm=None, a_keep=False):
    tm, tn, tk = min(tm, m), min(tn, n), min(tk, k)
    assert m % tm == 0 and n % tn == 0 and k % tk == 0, (name, m, n, k, tm, tn, tk)
    nk = k // tk
    has_res, has_bias = res is not None, bias is not None
    a_fn, a_ops = a if isinstance(a, tuple) else (None, [(a, None)])
    b_fn, b_ops = b if isinstance(b, tuple) else (None, [(b, None)])
    na, nb_ = len(a_ops), len(b_ops)
    a_bytes = sum(m * k * arr.dtype.itemsize for arr, _ in a_ops)
    b_bytes = sum(n * k * arr.dtype.itemsize for arr, _ in b_ops)
    swap = nk == 1 and b_bytes + (n // tn) * a_bytes < a_bytes + (m // tm) * b_bytes
    grid = (n // tn, m // tm, nk) if swap else (m // tm, n // tn, nk)
    ncomm = 0 if comm is None else len(comm["ins"])

    def body(*refs):
        g0, g1, kk = pl.program_id(0), pl.program_id(1), pl.program_id(2)
        gi, gj = (g1, g0) if swap else (g0, g1)
        a_tiles = [r[...] for r in refs[:na]]
        b_tiles = [r[...] for r in refs[na:na + nb_]]
        a_val = a_tiles[0] if a_fn is None else a_fn(gi, gj, kk, *a_tiles)
        b_val = b_tiles[0] if b_fn is None else b_fn(gi, gj, kk, *b_tiles)
        pos = na + nb_
        res_ref = bias_ref = None
        if has_res:
            res_ref = refs[pos]
            pos += 1
        if has_bias:
            bias_ref = refs[pos]
            pos += 1
        comm_ins = refs[pos:pos + ncomm]
        o_ref = refs[pos + ncomm]
        comm_outs = refs[pos + ncomm + 1:pos + 2 * ncomm + 1]
        scratch = refs[pos + 2 * ncomm + 1:]
        if a_keep:
            scratch[0][...] = a_val.astype(BF16)
            scratch = scratch[1:]
        if comm is not None:
            sems = scratch[1:] if nk > 1 else scratch

            @pl.when((g0 == 0) & (g1 == 0) & (kk == 0))
            def _():
                comm["start"](comm_ins, comm_outs, sems)

        def finish(r):
            if has_bias:
                r = r + bias_ref[...]
            if has_res:
                r = r + res_ref[...].astype(F32)
            o_ref[...] = r.astype(o_ref.dtype)

        part = lax.dot_general(a_val.astype(BF16), b_val.astype(BF16), _DIMS[mode], preferred_element_type=F32)
        if nk == 1:
            finish(part)
        else:
            acc_ref = scratch[0]

            @pl.when(kk == 0)
            def _():
                acc_ref[...] = part

            @pl.when(kk > 0)
            def _():
                acc_ref[...] += part

            @pl.when(kk == nk - 1)
            def _():
                finish(acc_ref[...])

        if comm is not None:
            @pl.when((g0 == grid[0] - 1) & (g1 == grid[1] - 1) & (kk == nk - 1))
            def _():
                comm["finish"](comm_ins, comm_outs, sems)

    def at(f):
        return (lambda g0, g1, kk: f(g1, g0, kk)) if swap else f

    if mode == "nn":
        a_spec = pl.BlockSpec((tm, tk), at(lambda i, j, kk: (i, kk + a_off)))
        b_spec = pl.BlockSpec((tk, tn), at(lambda i, j, kk: (kk, j + b_off)))
        if b_sh is not None:
            assert b_sh % tn == 0, (name, b_sh, tn)
            per = b_sh // tn
            b_spec = pl.BlockSpec((None, tk, tn), at(lambda i, j, kk: (j // per, kk, j % per)))
    elif mode == "nt":
        a_spec = pl.BlockSpec((tm, tk), at(lambda i, j, kk: (i, kk + a_off)))
        b_spec = pl.BlockSpec((tn, tk), at(lambda i, j, kk: (j, kk + b_off)))
        if b_sh is not None:
            assert b_sh % tk == 0, (name, b_sh, tk)
            per = b_sh // tk
            b_spec = pl.BlockSpec((None, tn, tk), at(lambda i, j, kk: (kk // per, j, kk % per)))
    else:
        a_spec = pl.BlockSpec((tk, tm), at(lambda i, j, kk: (kk, i + a_off)))
        b_spec = pl.BlockSpec((tk, tn), at(lambda i, j, kk: (kk, j + b_off)))
    a_specs = [a_spec] if a_fn is None else [pl.BlockSpec(a_spec.block_shape, at(f)) for _, f in a_ops]
    b_specs = [b_spec] if b_fn is None else [pl.BlockSpec(b_spec.block_shape, at(f)) for _, f in b_ops]
    in_specs, args = a_specs + b_specs, [arr for arr, _ in a_ops] + [arr for arr, _ in b_ops]
    if has_res:
        in_specs.append(pl.BlockSpec((tm, tn), at(lambda i, j, kk: (i, j))))
        args.append(res)
    if has_bias:
        in_specs.append(pl.BlockSpec((1, tn), at(lambda i, j, kk: (0, j))))
        args.append(bias)
    out_spec, out_shape = pl.BlockSpec((tm, tn), at(lambda i, j, kk: (i, j))), (m, n)
    if o_sh is not None:
        assert o_sh % tn == 0, (name, o_sh, tn)
        per_o = o_sh // tn
        out_spec = pl.BlockSpec((None, tm, tn), at(lambda i, j, kk: (j // per_o, i, j % per_o)))
        out_shape = (n // o_sh, m, o_sh)
    scratch = [pltpu.VMEM((tm, tn), F32)] if nk > 1 else []
    if a_keep:
        assert comm is None and o_sh is None and mode != "tn" and n == tn, name
        outs = pl.pallas_call(
            body, out_shape=[jax.ShapeDtypeStruct(out_shape, out_dtype), jax.ShapeDtypeStruct((m, k), BF16)], grid=grid,
            in_specs=in_specs, out_specs=[out_spec, pl.BlockSpec((tm, tk), at(lambda i, j, kk: (i, kk)))],
            scratch_shapes=scratch, name=name, compiler_params=_params(("parallel", "parallel", "arbitrary")))(*args)
        return outs[0], outs[1]
    if comm is None:
        return pl.pallas_call(
            body, out_shape=jax.ShapeDtypeStruct(out_shape, out_dtype), grid=grid, in_specs=in_specs,
            out_specs=out_spec, scratch_shapes=scratch, name=name,
            compiler_params=_params(("parallel", "parallel", "arbitrary")))(*args)
    outs = pl.pallas_call(
        body, out_shape=[jax.ShapeDtypeStruct(out_shape, out_dtype)] + comm["out_shapes"], grid=grid,
        in_specs=in_specs + [HBM] * ncomm, out_specs=[out_spec] + [HBM] * ncomm,
        scratch_shapes=scratch + comm["sems"], name=name,
        compiler_params=_params(("arbitrary", "arbitrary", "arbitrary")))(*args, *comm["ins"])
    return outs[0], outs[1:]


def _sig(v):
    return jax.nn.sigmoid(v)


def _rms_fwd(x, g):
    r = lax.rsqrt(jnp.mean(x * x, axis=-1, keepdims=True) + EPS)
    return x * r * g


def _rms_bwd(dh, x, dres, g):
    dh = dh.astype(F32)
    r = lax.rsqrt(jnp.mean(x * x, axis=-1, keepdims=True) + EPS)
    xh = x * r
    dxh = dh * g
    dx = r * (dxh - xh * jnp.mean(dxh * xh, axis=-1, keepdims=True))
    return dres + dx, jnp.sum(dh * xh, axis=0, keepdims=True)


def _rope_apply(t, c, sa, sb):
    w = t.shape[1]
    return t * c + pltpu.roll(t, w - ROPE_HALF, 1) * sa + pltpu.roll(t, ROPE_HALF, 1) * sb


def _rope_transpose(g, c, sa, sb):
    w = g.shape[1]
    return g * c + pltpu.roll(g * sa, ROPE_HALF, 1) + pltpu.roll(g * sb, w - ROPE_HALF, 1)


def _tile_lanes(tab, reps):
    return jnp.concatenate([tab] * reps, axis=1) if reps > 1 else tab


def _rope_fwd(q, k, v, c, sa, sb):
    rq = Q_WIDTH // c.shape[1]
    qr = _rope_apply(q.astype(F32), _tile_lanes(c, rq), _tile_lanes(sa, rq), _tile_lanes(sb, rq))
    kr = _rope_apply(k.astype(F32), c, sa, sb)
    return jnp.concatenate([qr, kr, v.astype(F32)], axis=1)


def _rope_bwd_q(g, c, sa, sb):
    rq = Q_WIDTH // c.shape[1]
    return _rope_transpose(g.astype(F32), _tile_lanes(c, rq), _tile_lanes(sa, rq), _tile_lanes(sb, rq))


def _gelu(v):
    return jax.nn.gelu(v, approximate=True)


def _gelu_grad(v):
    c0 = math.sqrt(2.0 / math.pi)
    inner = c0 * (v + 0.044715 * v * v * v)
    th = jnp.tanh(inner)
    return 0.5 * (1.0 + th) + 0.5 * v * (1.0 - th * th) * c0 * (1.0 + 3 * 0.044715 * v * v)


def _merge_fwd(g0, g1, g2, b0, b1, b2, ya, ga, gb, yc):
    s0 = _sig(g0.astype(F32) + b0)
    s1 = _sig(g1.astype(F32) + b1)
    s2 = _sig(g2.astype(F32) + b2)
    ys = ga.astype(F32) * _sig(gb.astype(F32))
    return s0 * ya.astype(F32) + s1 * ys + s2 * yc.astype(F32)


def _merge_bwd(dm, g0, g1, g2, b0, b1, b2, ya, ga, gb, yc):
    dm = dm.astype(F32)
    s0 = _sig(g0.astype(F32) + b0)
    s1 = _sig(g1.astype(F32) + b1)
    s2 = _sig(g2.astype(F32) + b2)
    ga = ga.astype(F32)
    sb = _sig(gb.astype(F32))
    ys = ga * sb
    dya = dm * s0
    dys = dm * s1
    dyc = dm * s2
    dga = dys * sb
    dgb = dys * ga * sb * (1.0 - sb)
    d0 = dm * ya.astype(F32) * s0 * (1.0 - s0)
    d1 = dm * ys * s1 * (1.0 - s1)
    d2 = dm * yc.astype(F32) * s2 * (1.0 - s2)
    cs = lambda v: jnp.sum(v, axis=0, keepdims=True)
    return dya, dga, dgb, dyc, d0, d1, d2, cs(d0), cs(d1), cs(d2), cs(dga), cs(dgb)


def _ffn_act(fg, fu):
    fg = fg.astype(F32)
    return fg * _sig(fg) * fu.astype(F32)


def _ffn_act_bwd(da, fg, fu):
    da, fg, fu = da.astype(F32), fg.astype(F32), fu.astype(F32)
    s = _sig(fg)
    return da * fu * (s * (1.0 + fg * (1.0 - s))), da * fg * s


def _ple_fwd(x, gp, e):
    return x + _sig(gp.astype(F32)) * e.astype(F32)


def _ple_bwd(dx, gp, e):
    s = _sig(gp.astype(F32))
    e = e.astype(F32)
    return dx * s, dx * e * s * (1.0 - s)


def _loss_fn(x, tgt, g):
    d = x.shape[1]
    r = lax.rsqrt(jnp.mean(x * x, axis=-1, keepdims=True) + EPS)
    xh = x * r
    err = xh * g - tgt
    dy = err * (1.0 / d)
    dxh = dy * g
    dx = r * (dxh - xh * jnp.mean(dxh * xh, axis=-1, keepdims=True))
    return dx, jnp.sum(err * err, axis=0, keepdims=True) * (0.5 / d), jnp.sum(dy * xh, axis=0, keepdims=True)


def _adamw_fn(w, g, m, v):
    m = ADAM_B1 * m + (1.0 - ADAM_B1) * g
    v = ADAM_B2 * v + (1.0 - ADAM_B2) * (g * g)
    m_hat = m / (1.0 - ADAM_B1 ** ADAM_STEP)
    v_hat = v / (1.0 - ADAM_B2 ** ADAM_STEP)
    delta = -ADAM_LR * (m_hat / (jnp.sqrt(v_hat) + ADAM_EPS) + ADAM_WD * w)
    return delta, m, v


def _band_mask(n):
    qi = lax.broadcasted_iota(jnp.int32, (ATT_BLOCK, 2 * ATT_BLOCK), 0)
    kj = lax.broadcasted_iota(jnp.int32, (ATT_BLOCK, 2 * ATT_BLOCK), 1)
    dist = qi + ATT_BLOCK - kj
    return (dist >= 0) & (dist < ATT_BLOCK) & ((n > 0) | (kj >= ATT_BLOCK))


K_HEADS_AT = N_Q_HEADS // N_KV_HEADS


def _att_specs(nb):
    qs = pl.BlockSpec((N_Q_HEADS, ATT_BLOCK, HEAD_DIM), lambda b, n: (0, b * nb + n, 0))

    def kv(head_block, back):
        return pl.BlockSpec((N_KV_HEADS, ATT_BLOCK, HEAD_DIM),
                            lambda b, n: (head_block, b * nb + jnp.maximum(n - back, 0), 0))

    stat = pl.BlockSpec((N_Q_HEADS, ATT_BLOCK, 1), lambda b, n: (0, b * nb + n, 0))
    sink = pl.BlockSpec((N_Q_HEADS, 1, 1), lambda b, n: (0, 0, 0))
    return qs, [kv(K_HEADS_AT, 1), kv(K_HEADS_AT, 0), kv(K_HEADS_AT + 1, 1), kv(K_HEADS_AT + 1, 0)], stat, sink


def _attn_fwd(qkv, sinks, nbatch, seq):
    t = qkv.shape[1]
    nb = seq // ATT_BLOCK
    qs, kvs, stat, sink = _att_specs(nb)

    def body(q_ref, kp_ref, kc_ref, vp_ref, vc_ref, sink_ref, o_ref, lse_ref):
        mask = _band_mask(pl.program_id(1))
        rows = GQA_GROUP * ATT_BLOCK
        for kv in range(N_KV_HEADS):
            hs = slice(kv * GQA_GROUP, (kv + 1) * GQA_GROUP)
            kk = jnp.concatenate([kp_ref[kv], kc_ref[kv]], axis=0)
            vv = jnp.concatenate([vp_ref[kv], vc_ref[kv]], axis=0)
            q4 = (q_ref[hs] * SCALE).reshape(rows, HEAD_DIM)
            s = lax.dot_general(q4, kk, _DIMS["nt"], preferred_element_type=F32)
            s = jnp.where(mask, s.reshape(GQA_GROUP, ATT_BLOCK, 2 * ATT_BLOCK), NEG_INF)
            sk = sink_ref[hs]
            mx = jnp.maximum(jnp.max(s, axis=-1, keepdims=True), sk)
            p = jnp.exp(s - mx)
            den = jnp.sum(p, axis=-1, keepdims=True) + jnp.exp(sk - mx)
            o = lax.dot_general(p.reshape(rows, 2 * ATT_BLOCK).astype(BF16), vv, _DIMS["nn"],
                                preferred_element_type=F32).reshape(GQA_GROUP, ATT_BLOCK, HEAD_DIM)
            o_ref[hs] = (o * (1.0 / den)).astype(o_ref.dtype)
            lse_ref[hs] = mx + jnp.log(den)

    return pl.pallas_call(
        body, grid=(nbatch, nb), in_specs=[qs] + kvs + [sink], out_specs=[qs, stat],
        out_shape=[jax.ShapeDtypeStruct((N_Q_HEADS, t, HEAD_DIM), BF16), jax.ShapeDtypeStruct((N_Q_HEADS, t, 1), F32)],
        name="attn_fwd", compiler_params=_params(("parallel", "parallel")))(qkv, qkv, qkv, qkv, qkv, sinks)


def _attn_bwd(qkv, oh, doh, lse, sinks, nbatch, seq):
    t = qkv.shape[1]
    nb = seq // ATT_BLOCK
    qs, kvs, stat, sink = _att_specs(nb)

    def body(q_ref, kp_ref, kc_ref, vp_ref, vc_ref, o_ref, do_ref, lse_ref, sink_ref, dqkv_ref, dsink_ref):
        dq_ref = dqkv_ref.at[pl.ds(0, N_Q_HEADS)]
        dkc_ref, dvc_ref, dkp_ref, dvp_ref = (dqkv_ref.at[pl.ds(N_Q_HEADS + N_KV_HEADS * i, N_KV_HEADS)]
                                              for i in range(4))
        first = (pl.program_id(0) == 0) & (pl.program_id(1) == 0)

        @pl.when(first)
        def _():
            dsink_ref[...] = jnp.zeros_like(dsink_ref)

        mask = _band_mask(pl.program_id(1))
        rows = GQA_GROUP * ATT_BLOCK
        band = (GQA_GROUP, ATT_BLOCK, 2 * ATT_BLOCK)
        for kv in range(N_KV_HEADS):
            hs = slice(kv * GQA_GROUP, (kv + 1) * GQA_GROUP)
            kk = jnp.concatenate([kp_ref[kv], kc_ref[kv]], axis=0)
            vv = jnp.concatenate([vp_ref[kv], vc_ref[kv]], axis=0)
            q4 = q_ref[hs].reshape(rows, HEAD_DIM)
            do4 = do_ref[hs].reshape(rows, HEAD_DIM)
            lse4 = lse_ref[hs]
            s = lax.dot_general(q4 * SCALE, kk, _DIMS["nt"], preferred_element_type=F32).reshape(band)
            p = jnp.where(mask, jnp.exp(s - lse4), 0.0)
            dd = jnp.sum(do_ref[hs].astype(F32) * o_ref[hs].astype(F32), axis=-1, keepdims=True)
            dp = lax.dot_general(do4, vv, _DIMS["nt"], preferred_element_type=F32).reshape(band)
            ds = (p * (dp - dd) * SCALE).astype(BF16).reshape(rows, 2 * ATT_BLOCK)
            dq = lax.dot_general(ds, kk, _DIMS["nn"], preferred_element_type=F32)
            dq_ref[hs] = dq.reshape(GQA_GROUP, ATT_BLOCK, HEAD_DIM).astype(dq_ref.dtype)
            dk = lax.dot_general(ds, q4, _DIMS["tn"], preferred_element_type=F32)
            dv = lax.dot_general(p.astype(BF16).reshape(rows, 2 * ATT_BLOCK), do4, _DIMS["tn"],
                                 preferred_element_type=F32)
            dsink_ref[hs] += -jnp.sum(jnp.exp(sink_ref[hs] - lse4) * dd, axis=1, keepdims=True)
            dkp_ref[kv] = dk[:ATT_BLOCK]
            dkc_ref[kv] = dk[ATT_BLOCK:]
            dvp_ref[kv] = dv[:ATT_BLOCK]
            dvc_ref[kv] = dv[ATT_BLOCK:]

    n_out = 2 * N_Q_HEADS
    return pl.pallas_call(
        body, grid=(nbatch, nb), in_specs=[qs] + kvs + [qs, qs, stat, sink],
        out_specs=[pl.BlockSpec((n_out, ATT_BLOCK, HEAD_DIM), lambda b, n: (0, b * nb + n, 0)), sink],
        out_shape=[jax.ShapeDtypeStruct((n_out, t, HEAD_DIM), F32), jax.ShapeDtypeStruct((N_Q_HEADS, 1, 1), F32)],
        name="attn_bwd", compiler_params=_params(("arbitrary", "arbitrary")))(
        qkv, qkv, qkv, qkv, qkv, oh, doh, lse, sinks)


def _kv_combine(dqkv, c, sa, sb, seq):
    t = dqkv.shape[0]
    nb = seq // ATT_BLOCK
    nblk = t // ATT_BLOCK
    col0 = Q_WIDTH // KV_WIDTH

    def body(kc_ref, kp_ref, vc_ref, vp_ref, c_ref, sa_ref, sb_ref, dk_ref, dv_ref):
        has_next = (pl.program_id(0) % nb) != nb - 1
        dk = kc_ref[...] + jnp.where(has_next, kp_ref[...], 0.0)
        dv = vc_ref[...] + jnp.where(has_next, vp_ref[...], 0.0)
        dk_ref[...] = _rope_transpose(dk, c_ref[...], sa_ref[...], sb_ref[...]).astype(dk_ref.dtype)
        dv_ref[...] = dv.astype(dv_ref.dtype)

    cur = pl.BlockSpec((ATT_BLOCK, KV_WIDTH), lambda i: (i, 0))
    own = lambda col: pl.BlockSpec((ATT_BLOCK, KV_WIDTH), lambda i: (i, col0 + col))
    nxt = lambda col: pl.BlockSpec((ATT_BLOCK, KV_WIDTH), lambda i: (jnp.minimum(i + 1, nblk - 1), col0 + col))
    o = jax.ShapeDtypeStruct((t, KV_WIDTH), BF16)
    return pl.pallas_call(body, grid=(nblk,), in_specs=[own(0), nxt(2), own(1), nxt(3), cur, cur, cur],
                          out_specs=[cur, cur], out_shape=[o, o], name="kv_combine",
                          compiler_params=_params(("parallel",)))(dqkv, dqkv, dqkv, dqkv, c, sa, sb)


def _scan_block(ref, tab_ref, carry, ngroups, reverse):
    shifts = (7, 6, 4) if reverse else (1, 2, 4)
    n = SSM_LANES

    def step(i, car):
        g = (ngroups - 1 - i) if reverse else i
        r0 = pl.multiple_of(g * SUBLANES, SUBLANES)
        xr = ref[pl.ds(r0, SUBLANES), :n]
        xi = ref[pl.ds(r0, SUBLANES), n:]
        for s, sh in enumerate(shifts):
            pr, pi = tab_ref[2 * s], tab_ref[2 * s + 1]
            yr, yi = pltpu.roll(xr, sh, 0), pltpu.roll(xi, sh, 0)
            xr, xi = xr + pr * yr - pi * yi, xi + pr * yi + pi * yr
        cr, ci = car
        qr, qi = tab_ref[6], tab_ref[7]
        xr, xi = xr + qr * cr - qi * ci, xi + qr * ci + qi * cr
        ref[pl.ds(r0, SUBLANES), :n] = xr
        ref[pl.ds(r0, SUBLANES), n:] = xi
        last = r0 if reverse else r0 + SUBLANES - 1
        return ref[pl.ds(last, 1), :n], ref[pl.ds(last, 1), n:]

    return lax.fori_loop(0, ngroups, step, carry, unroll=2)


def _ssm_chunk(seq):
    return min(512, seq)


def _ssm_fwd(z, wb, wc, tab, dskip, nbatch, seq):
    t = z.shape[0]
    tc = _ssm_chunk(seq)
    nc = seq // tc
    n2 = 2 * SSM_LANES

    def body(u_ref, wb_ref, wc_ref, tab_ref, d_ref, st_ref, y_ref, gel_ref, car_ref):
        @pl.when(pl.program_id(1) == 0)
        def _():
            car_ref[...] = jnp.zeros_like(car_ref)

        u = u_ref[...]
        st_ref[...] = lax.dot_general(u, wb_ref[...], _DIMS["nn"], preferred_element_type=F32)
        cr, ci = _scan_block(st_ref, tab_ref, (car_ref[:, :SSM_LANES], car_ref[:, SSM_LANES:]), tc // SUBLANES, False)
        car_ref[:, :SSM_LANES] = cr
        car_ref[:, SSM_LANES:] = ci
        y = lax.dot_general(st_ref[...].astype(BF16), wc_ref[...], _DIMS["nn"], preferred_element_type=F32)
        y = y + d_ref[...] * u.astype(F32)
        y_ref[...] = y
        gel_ref[...] = _gelu(y).astype(gel_ref.dtype)

    row = lambda b, c: (b * nc + c, 0)
    full = lambda b, c: (0, 0)
    return pl.pallas_call(
        body, grid=(nbatch, nc),
        in_specs=[pl.BlockSpec((tc, SSM_WIDTH), lambda b, c: (b * nc + c, 3)), pl.BlockSpec((SSM_WIDTH, n2), full),
                  pl.BlockSpec((n2, SSM_WIDTH), full), pl.BlockSpec((8, SUBLANES, SSM_LANES), lambda b, c: (0, 0, 0)),
                  pl.BlockSpec((1, SSM_WIDTH), full)],
        out_specs=[pl.BlockSpec((tc, n2), row), pl.BlockSpec((tc, SSM_WIDTH), row), pl.BlockSpec((tc, SSM_WIDTH), row)],
        out_shape=[jax.ShapeDtypeStruct((t, n2), F32), jax.ShapeDtypeStruct((t, SSM_WIDTH), F32),
                   jax.ShapeDtypeStruct((t, SSM_WIDTH), BF16)],
        scratch_shapes=[pltpu.VMEM((1, n2), F32)], name="ssm_fwd",
        compiler_params=_params(("arbitrary", "arbitrary")))(z, wb, wc, tab, dskip)


def _ssm_bwd(dgi, ys, st, z, wbt, wct, tab_rev, dskip, nbatch, seq):
    t = z.shape[0]
    tc = _ssm_chunk(seq)
    nc = seq // tc
    n = SSM_LANES
    n2 = 2 * n
    ng = tc // SUBLANES

    def body(dgi_ref, ys_ref, st_ref, stp_ref, u_ref, wbt_ref, wct_ref, tab_ref, d_ref,
             du_ref, dwb_ref, dwc_ref, dd_ref, da_ref, p_ref, sb_ref, car_ref):
        b, c = pl.program_id(0), pl.program_id(1)
        ct = nc - 1 - c

        @pl.when((b == 0) & (c == 0))
        def _():
            dwb_ref[...] = jnp.zeros_like(dwb_ref)
            dwc_ref[...] = jnp.zeros_like(dwc_ref)
            dd_ref[...] = jnp.zeros_like(dd_ref)
            da_ref[...] = jnp.zeros_like(da_ref)

        @pl.when(c == 0)
        def _():
            car_ref[...] = jnp.zeros_like(car_ref)

        u = u_ref[...]
        dys = dgi_ref[...].astype(F32) * _gelu_grad(ys_ref[...])
        dys_b = dys.astype(BF16)
        st = st_ref[...]
        dd_ref[...] += jnp.sum(dys * u.astype(F32), axis=0, keepdims=True)
        dwc_ref[...] += lax.dot_general(st.astype(BF16), dys_b, _DIMS["tn"], preferred_element_type=F32)
        p_ref[...] = lax.dot_general(dys_b, wct_ref[...], _DIMS["nn"], preferred_element_type=F32)
        cr, ci = _scan_block(p_ref, tab_ref, (car_ref[:, :n], car_ref[:, n:]), ng, True)
        car_ref[:, :n] = cr
        car_ref[:, n:] = ci
        p = p_ref[...]
        pb = p.astype(BF16)
        dwb_ref[...] += lax.dot_general(u, pb, _DIMS["tn"], preferred_element_type=F32)
        du = lax.dot_general(pb, wbt_ref[...], _DIMS["nn"], preferred_element_type=F32) + d_ref[...] * dys
        du_ref[...] = du.astype(du_ref.dtype)
        sb_ref[pl.ds(0, SUBLANES), :] = jnp.where(ct > 0, stp_ref[...], 0.0)
        sb_ref[pl.ds(SUBLANES, tc), :] = st
        row0 = lax.broadcasted_iota(jnp.int32, (SUBLANES, n), 0) == 0

        def acc_step(g, acc):
            ar, ai = acc
            r0 = pl.multiple_of(g * SUBLANES, SUBLANES)
            edge_r = sb_ref[pl.ds(r0 + SUBLANES - 1, 1), :n]
            edge_i = sb_ref[pl.ds(r0 + SUBLANES - 1, 1), n:]
            sr = jnp.where(row0, edge_r, pltpu.roll(sb_ref[pl.ds(r0 + SUBLANES, SUBLANES), :n], 1, 0))
            si = jnp.where(row0, edge_i, pltpu.roll(sb_ref[pl.ds(r0 + SUBLANES, SUBLANES), n:], 1, 0))
            pr = p_ref[pl.ds(r0, SUBLANES), :n]
            pi = p_ref[pl.ds(r0, SUBLANES), n:]
            return ar + pr * sr + pi * si, ai + pi * sr - pr * si

        zero = jnp.zeros((SUBLANES, n), F32)
        ar, ai = lax.fori_loop(0, ng, acc_step, (zero, zero), unroll=2)
        da_ref[:, :n] += ar
        da_ref[:, n:] += ai

    row = lambda b, c: (b * nc + (nc - 1 - c), 0)
    prev8 = lambda b, c: (jnp.maximum((b * nc + (nc - 1 - c)) * (tc // SUBLANES) - 1, 0), 0)
    full = lambda b, c: (0, 0)
    return pl.pallas_call(
        body, grid=(nbatch, nc),
        in_specs=[pl.BlockSpec((tc, SSM_WIDTH), row), pl.BlockSpec((tc, SSM_WIDTH), row), pl.BlockSpec((tc, n2), row),
                  pl.BlockSpec((SUBLANES, n2), prev8),
                  pl.BlockSpec((tc, SSM_WIDTH), lambda b, c: (b * nc + (nc - 1 - c), 3)),
                  pl.BlockSpec((n2, SSM_WIDTH), full), pl.BlockSpec((SSM_WIDTH, n2), full),
                  pl.BlockSpec((8, SUBLANES, n), lambda b, c: (0, 0, 0)), pl.BlockSpec((1, SSM_WIDTH), full)],
        out_specs=[pl.BlockSpec((tc, SSM_WIDTH), row), pl.BlockSpec((SSM_WIDTH, n2), full),
                   pl.BlockSpec((n2, SSM_WIDTH), full), pl.BlockSpec((1, SSM_WIDTH), full),
                   pl.BlockSpec((SUBLANES, n2), full)],
        out_shape=[jax.ShapeDtypeStruct((t, SSM_WIDTH), BF16), jax.ShapeDtypeStruct((SSM_WIDTH, n2), F32),
                   jax.ShapeDtypeStruct((n2, SSM_WIDTH), F32), jax.ShapeDtypeStruct((1, SSM_WIDTH), F32),
                   jax.ShapeDtypeStruct((SUBLANES, n2), F32)],
        scratch_shapes=[pltpu.VMEM((tc, n2), F32), pltpu.VMEM((tc + SUBLANES, n2), F32), pltpu.VMEM((1, n2), F32)],
        name="ssm_bwd", compiler_params=_params(("arbitrary", "arbitrary")))(
        dgi, ys, st, st, z, wbt, wct, tab_rev, dskip)


def _ssm_prep(lam_re, lam_im, log_dt, b_re, b_im, c_re, c_im):
    lr = jnp.minimum(lam_re, -1e-4)
    li = lam_im
    dt = jnp.exp(log_dt)[:, None]
    mag = jnp.exp(lr * dt)
    a_re = mag * jnp.cos(li * dt)
    a_im = mag * jnp.sin(li * dt)
    den = lr * lr + li * li
    x_re, x_im = a_re - 1.0, a_im
    f_re = (x_re * lr + x_im * li) / den
    f_im = (x_im * lr - x_re * li) / den
    bb_re = f_re[..., None] * b_re - f_im[..., None] * b_im
    bb_im = f_re[..., None] * b_im + f_im[..., None] * b_re
    eye = jnp.eye(SSM_GROUPS, dtype=F32)
    emb_b = lambda v: jnp.einsum("gnh,gk->ghkn", v, eye).reshape(SSM_WIDTH, SSM_LANES)
    emb_c = lambda v: jnp.einsum("ghn,gk->gnkh", v, eye).reshape(SSM_LANES, SSM_WIDTH)
    wb = jnp.concatenate([emb_b(bb_re), emb_b(bb_im)], axis=1)
    wc = jnp.concatenate([emb_c(c_re), -emb_c(c_im)], axis=0)
    return a_re.reshape(-1), a_im.reshape(-1), wb, wc


def _ssm_tables(a_re, a_im, reverse):
    if reverse:
        a_im = -a_im
    pw = [(a_re, a_im)]
    for _ in range(SUBLANES - 1):
        pr, pi = pw[-1]
        pw.append((pr * a_re - pi * a_im, pr * a_im + pi * a_re))
    rows = jnp.arange(SUBLANES)[:, None]
    tabs = []
    for k in (1, 2, 4):
        ok = (rows + k <= SUBLANES - 1) if reverse else (rows >= k)
        tabs += [jnp.where(ok, pw[k - 1][0][None], 0.0), jnp.where(ok, pw[k - 1][1][None], 0.0)]
    order = list(range(SUBLANES - 1, -1, -1)) if reverse else list(range(SUBLANES))
    tabs += [jnp.stack([pw[i][0] for i in order]), jnp.stack([pw[i][1] for i in order])]
    return jnp.stack(tabs)


def _conv_chunk(seq):
    return min(512, seq)


def _shifted(buf, sh, tc, offsets):
    for b in range(SUBLANES):
        idx = [i for i, o in enumerate(offsets) if o % SUBLANES == b]
        if not idx:
            continue
        src = buf
        if b:
            span = tc + SUBLANES * max(offsets[i] // SUBLANES for i in idx)
            sh[pl.ds(0, span), :] = buf[pl.ds(b, span), :]
            src = sh
        for i in idx:
            yield i, src[pl.ds(offsets[i] // SUBLANES * SUBLANES, tc), :]


def _conv_fwd(z, w, bias, lg, lb, nbatch, seq):
    t = z.shape[0]
    tc = _conv_chunk(seq)
    nc = seq // tc

    def body(a_ref, g_ref, w_ref, b_ref, lg_ref, lb_ref, cv_ref, sc_ref, ubuf, sh):
        c = pl.program_id(1)

        @pl.when(c == 0)
        def _():
            ubuf[pl.ds(0, CONV_HALO), :] = jnp.zeros((CONV_HALO, CONV_WIDTH), F32)

        @pl.when(c > 0)
        def _():
            ubuf[pl.ds(0, CONV_HALO), :] = ubuf[pl.ds(tc, CONV_HALO), :]

        ubuf[pl.ds(CONV_HALO, tc), :] = a_ref[...].astype(F32) * _sig(g_ref[...].astype(F32))
        acc = jnp.zeros((tc, CONV_WIDTH), F32) + b_ref[...]
        for k, win in _shifted(ubuf, sh, tc, [CONV_HALO - (CONV_K - 1) + k for k in range(CONV_K)]):
            acc = acc + w_ref[pl.ds(k, 1), :] * win
        cv_ref[...] = acc
        mu = jnp.mean(acc, axis=-1, keepdims=True)
        xc = acc - mu
        y = xc * lax.rsqrt(jnp.mean(xc * xc, axis=-1, keepdims=True) + EPS) * lg_ref[...] + lb_ref[...]
        sc_ref[...] = (y * _sig(y)).astype(sc_ref.dtype)

    row = lambda b, c: (b * nc + c, 0)
    full = lambda b, c: (0, 0)
    vec = pl.BlockSpec((1, CONV_WIDTH), full)
    return pl.pallas_call(
        body, grid=(nbatch, nc),
        in_specs=[pl.BlockSpec((tc, CONV_WIDTH), lambda b, c: (b * nc + c, 4)),
                  pl.BlockSpec((tc, CONV_WIDTH), lambda b, c: (b * nc + c, 5)),
                  pl.BlockSpec((CONV_HALO, CONV_WIDTH), full), vec, vec, vec],
        out_specs=[pl.BlockSpec((tc, CONV_WIDTH), row), pl.BlockSpec((tc, CONV_WIDTH), row)],
        out_shape=[jax.ShapeDtypeStruct((t, CONV_WIDTH), F32), jax.ShapeDtypeStruct((t, CONV_WIDTH), BF16)],
        scratch_shapes=[pltpu.VMEM((CONV_HALO + tc, CONV_WIDTH), F32)] * 2, name="conv_fwd",
        compiler_params=_params(("arbitrary", "arbitrary")))(z, z, w, bias, lg, lb)


def _conv_bwd(dsc, cv, z, w, lg, lb, nbatch, seq):
    t = z.shape[0]
    tc = _conv_chunk(seq)
    nc = seq // tc
    hb = tc // CONV_HALO

    def body(dsc_ref, cv_ref, a_ref, g_ref, ap_ref, gp_ref, w_ref, lg_ref, lb_ref,
             da_ref, dg_ref, dw_ref, db_ref, dlg_ref, dlb_ref, ubuf, dbuf, sh):
        b, c = pl.program_id(0), pl.program_id(1)
        ct = nc - 1 - c

        @pl.when((b == 0) & (c == 0))
        def _():
            dw_ref[...] = jnp.zeros_like(dw_ref)
            db_ref[...] = jnp.zeros_like(db_ref)
            dlg_ref[...] = jnp.zeros_like(dlg_ref)
            dlb_ref[...] = jnp.zeros_like(dlb_ref)

        cvv = cv_ref[...]
        mu = jnp.mean(cvv, axis=-1, keepdims=True)
        xc = cvv - mu
        rstd = lax.rsqrt(jnp.mean(xc * xc, axis=-1, keepdims=True) + EPS)
        xh = xc * rstd
        y = xh * lg_ref[...] + lb_ref[...]
        sy = _sig(y)
        dy = dsc_ref[...].astype(F32) * (sy * (1.0 + y * (1.0 - sy)))
        dlg_ref[...] += jnp.sum(dy * xh, axis=0, keepdims=True)
        dlb_ref[...] += jnp.sum(dy, axis=0, keepdims=True)
        dxh = dy * lg_ref[...]
        dcv = rstd * (dxh - jnp.mean(dxh, axis=-1, keepdims=True) - xh * jnp.mean(dxh * xh, axis=-1, keepdims=True))
        db_ref[...] += jnp.sum(dcv, axis=0, keepdims=True)

        @pl.when(c == 0)
        def _():
            dbuf[pl.ds(tc, CONV_HALO), :] = jnp.zeros((CONV_HALO, CONV_WIDTH), F32)

        @pl.when(c > 0)
        def _():
            dbuf[pl.ds(tc, CONV_HALO), :] = dbuf[pl.ds(0, CONV_HALO), :]

        dbuf[pl.ds(0, tc), :] = dcv
        a = a_ref[...].astype(F32)
        sg = _sig(g_ref[...].astype(F32))
        ubuf[pl.ds(0, CONV_HALO), :] = jnp.where(ct > 0, ap_ref[...].astype(F32) * _sig(gp_ref[...].astype(F32)), 0.0)
        ubuf[pl.ds(CONV_HALO, tc), :] = a * sg
        du = jnp.zeros((tc, CONV_WIDTH), F32)
        for k, win in _shifted(dbuf, sh, tc, [CONV_K - 1 - k for k in range(CONV_K)]):
            du = du + w_ref[pl.ds(k, 1), :] * win
        for k, win in _shifted(ubuf, sh, tc, [CONV_HALO - (CONV_K - 1) + k for k in range(CONV_K)]):
            dw_ref[pl.ds(k, 1), :] += jnp.sum(dcv * win, axis=0, keepdims=True)
        da_ref[...] = (du * sg).astype(da_ref.dtype)
        dg_ref[...] = (du * a * sg * (1.0 - sg)).astype(dg_ref.dtype)

    row = lambda b, c: (b * nc + (nc - 1 - c), 0)
    full = lambda b, c: (0, 0)
    vec = pl.BlockSpec((1, CONV_WIDTH), full)
    blk = pl.BlockSpec((tc, CONV_WIDTH), row)

    def zcol(col):
        return pl.BlockSpec((tc, CONV_WIDTH), lambda b, c: (b * nc + (nc - 1 - c), col))

    def zprev(col):
        return pl.BlockSpec((CONV_HALO, CONV_WIDTH),
                            lambda b, c: (jnp.maximum((b * nc + (nc - 1 - c)) * hb - 1, 0), col))

    o = jax.ShapeDtypeStruct((t, CONV_WIDTH), BF16)
    v = jax.ShapeDtypeStruct((1, CONV_WIDTH), F32)
    return pl.pallas_call(
        body, grid=(nbatch, nc),
        in_specs=[blk, blk, zcol(4), zcol(5), zprev(4), zprev(5), pl.BlockSpec((CONV_HALO, CONV_WIDTH), full), vec, vec],
        out_specs=[blk, blk, pl.BlockSpec((CONV_HALO, CONV_WIDTH), full), vec, vec, vec],
        out_shape=[o, o, jax.ShapeDtypeStruct((CONV_HALO, CONV_WIDTH), F32), v, v, v],
        scratch_shapes=[pltpu.VMEM((CONV_HALO + tc, CONV_WIDTH), F32)] * 3, name="conv_bwd", compiler_params=_params(("arbitrary", "arbitrary")))(dsc, cv, z, z, z, z, w, lg, lb)


BIG = ("w_in", "w_attn_out", "w_ssm_glu", "w_conv_out", "w_mix_out", "w_ffn_in", "w_ffn_out", "w_ple_in", "w_ple_gate")
BIG_AXIS = {"w_in": 2, "w_attn_out": 2, "w_ssm_glu": 2, "w_conv_out": 2, "w_mix_out": 1, "w_ffn_in": 2,
            "w_ffn_out": 1, "w_ple_in": 2, "w_ple_gate": 1}
SHARD_MAJOR = ("w_in", "w_ffn_in")
SMALL = ("mix_norm_g", "b_gate", "attn_sinks", "ssm_lambda_re", "ssm_lambda_im", "ssm_log_dt", "ssm_b_re", "ssm_b_im",
         "ssm_c_re", "ssm_c_im", "ssm_d", "b_ssm_glu", "conv_dw_w", "conv_dw_b", "conv_norm_g", "conv_norm_b",
         "ffn_norm_g", "ple_norm_g", "final_norm_g")
WEIGHTS = ("mix_norm_g", "w_in", "b_gate", "attn_sinks", "w_attn_out", "ssm_lambda_re", "ssm_lambda_im", "ssm_log_dt",
           "ssm_b_re", "ssm_b_im", "ssm_c_re", "ssm_c_im", "ssm_d", "w_ssm_glu", "b_ssm_glu", "conv_dw_w", "conv_dw_b",
           "conv_norm_g", "conv_norm_b", "w_conv_out", "w_mix_out", "ffn_norm_g", "w_ffn_in", "w_ffn_out", "w_ple_in",
           "ple_norm_g", "w_ple_gate", "final_norm_g")
SSM_NAMES = ("ssm_lambda_re", "ssm_lambda_im", "ssm_log_dt", "ssm_b_re", "ssm_b_im", "ssm_c_re", "ssm_c_im")


def _ple_block(x, p_l, g, w_in, w_gate):
    t, d = x.shape
    kp = p_l.shape[1]
    tm = min(512, t)

    def body(x_ref, p_ref, g_ref, wi_ref, wg_ref, o_ref, gp_ref, h_ref, e_ref):
        xv = x_ref[...]
        e = lax.dot_general(p_ref[...].astype(BF16), wi_ref[...], _DIMS["nn"], preferred_element_type=F32).astype(BF16)
        h = _rms_fwd(xv, g_ref[...]).astype(BF16)
        gp = lax.dot_general(h, wg_ref[...], _DIMS["nn"], preferred_element_type=F32).astype(BF16)
        o_ref[...] = _ple_fwd(xv, gp, e)
        gp_ref[...], h_ref[...], e_ref[...] = gp, h, e

    row = lambda width: pl.BlockSpec((tm, width), lambda i: (i, 0))
    full = lambda v: pl.BlockSpec(v.shape, lambda i: (0, 0))
    out = lambda dt: jax.ShapeDtypeStruct((t, d), dt)
    return pl.pallas_call(
        body, grid=(t // tm,), in_specs=[row(d), row(kp), full(g), full(w_in), full(w_gate)],
        out_specs=[row(d)] * 4, out_shape=[out(F32), out(BF16), out(BF16), out(BF16)], name="ple_block",
        compiler_params=_params(("parallel",)))(x, p_l, g, w_in, w_gate)


def _ple_block_bwd(dx3, gp, e, x, g, w_gate):
    t, d = x.shape
    tm = min(512, t)

    def body(dx3_ref, gp_ref, e_ref, x_ref, g_ref, wg_ref, de_ref, dgp_ref, dx_ref, dg_ref):
        @pl.when(pl.program_id(0) == 0)
        def _():
            dg_ref[...] = jnp.zeros_like(dg_ref)

        dx3 = dx3_ref[...]
        de, dgp = _ple_bwd(dx3, gp_ref[...], e_ref[...])
        dgp = dgp.astype(BF16)
        de_ref[...] = de.astype(de_ref.dtype)
        dgp_ref[...] = dgp
        dh = lax.dot_general(dgp, wg_ref[...], _DIMS["nt"], preferred_element_type=F32).astype(BF16)
        dx, dg = _rms_bwd(dh, x_ref[...], dx3, g_ref[...])
        dx_ref[...] = dx
        dg_ref[...] += dg

    row = pl.BlockSpec((tm, d), lambda i: (i, 0))
    full = lambda v: pl.BlockSpec(v.shape, lambda i: (0, 0))
    out = lambda dt: jax.ShapeDtypeStruct((t, d), dt)
    return pl.pallas_call(
        body, grid=(t // tm,), in_specs=[row, row, row, row, full(g), full(w_gate)],
        out_specs=[row, row, row, pl.BlockSpec((1, d), lambda i: (0, 0))],
        out_shape=[out(BF16), out(BF16), out(F32), jax.ShapeDtypeStruct((1, d), F32)], name="ple_block_bwd",
        compiler_params=_params(("arbitrary",)))(dx3, gp, e, x, g, w_gate)


def _heads(v, nh):
    return v.reshape(v.shape[0], nh, HEAD_DIM).transpose(1, 0, 2)


def _tokens(v):
    return v.transpose(1, 0, 2).reshape(v.shape[1], v.shape[0] * HEAD_DIM)


def _row(v):
    return v.reshape(1, -1)


def _layer_fwd(x, p_l, w, s, rope, nbatch, seq, next_shards=None):
    t = x.shape[0]
    tm = 512
    d = D_MODEL
    sv = {}
    sv["x"] = x
    h = _rowwise("rms_mix", _rms_fwd, [R(x), V(_row(s["mix_norm_g"]))], [O(d, BF16)], tm=tm)
    cs = {nm: w[nm].shape[2] for nm in SHARD_MAJOR}
    tb = 1024
    got = {}
    plan = None if next_shards is None else _gather_plan(next_shards, GATHER_A)
    z = _mm("mm_in", h, w["w_in"], "nn", BF16, m=t, n=N_CHIPS * cs["w_in"], k=d, tm=tb, tn=cs["w_in"], tk=d,
            b_sh=cs["w_in"], comm=plan)
    if plan is not None:
        z, outs = z
        got.update(zip(plan["names"], outs))
    sv["h"], sv["z"] = h, z
    c, sa, sb = rope
    qkv_w = Q_WIDTH + 2 * KV_WIDTH
    qkv = _rowwise("rope_fwd", _rope_fwd, [R(z, Q_WIDTH, 0), R(z, KV_WIDTH, 4), R(z, KV_WIDTH, 5), R(c), R(sa), R(sb)],
                   [O(qkv_w, BF16)], tm=tm)
    qkv = _heads(qkv, qkv_w // HEAD_DIM)
    sinks = s["attn_sinks"].reshape(N_Q_HEADS, 1, 1)
    oh, lse = _attn_fwd(qkv, sinks, nbatch, seq)
    o = _tokens(oh)
    ya = _mm("mm_attn_out", o, w["w_attn_out"], "nn", BF16, m=t, n=d, k=Q_WIDTH, tm=tb, tn=d, tk=Q_WIDTH)
    sv.update(qkv=qkv, oh=oh, lse=lse, o=o, ya=ya, sinks=sinks)
    ssm_args = [s[nm] for nm in SSM_NAMES]
    a_re, a_im, wb, wc = _ssm_prep(*ssm_args)
    dskip = _row(s["ssm_d"])
    st, ys, gel = _ssm_fwd(z, wb.astype(BF16), wc.astype(BF16), _ssm_tables(a_re, a_im, False), dskip, nbatch, seq)
    glu = _mm("mm_glu", gel, w["w_ssm_glu"], "nn", BF16, m=t, n=2 * d, k=SSM_WIDTH, tm=tb, tn=2 * d, tk=SSM_WIDTH,
              bias=_row(s["b_ssm_glu"]))
    sv.update(st=st, ys=ys, gel=gel, glu=glu, a=(a_re, a_im), wb=wb, wc=wc, dskip=dskip)
    cw = jnp.pad(s["conv_dw_w"], ((0, CONV_HALO - CONV_K), (0, 0)))
    cv, sc = _conv_fwd(z, cw, _row(s["conv_dw_b"]), _row(s["conv_norm_g"]), _row(s["conv_norm_b"]), nbatch, seq)
    yc = _mm("mm_conv_out", sc, w["w_conv_out"], "nn", BF16, m=t, n=d, k=CONV_WIDTH, tm=tb, tn=d, tk=CONV_WIDTH)
    sv.update(cw=cw, cv=cv, sc=sc, yc=yc)
    bg = _row(s["b_gate"])
    merge_ins = [R(z, 512, 3), R(z, 512, 5), R(z, 512, 7), V(bg, 512, 0), V(bg, 512, 2), V(bg, 512, 4),
                 R(ya, 512, 0), R(glu, 512, 0), R(glu, 512, 2), R(yc, 512, 0)]
    merged = _rowwise("merge_fwd", _merge_fwd, merge_ins, [O(512, BF16, total=d)], tm=tm, ncol=2)
    x1 = _mm("mm_mix", merged, w["w_mix_out"], "nn", F32, m=t, n=d, k=d, tm=tb, tn=d, tk=d, res=x)
    sv.update(merged=merged, x1=x1)
    hf = _rowwise("rms_ffn", _rms_fwd, [R(x1), V(_row(s["ffn_norm_g"]))], [O(d, BF16)], tm=tm)
    plan = None if next_shards is None else _gather_plan(next_shards, GATHER_B)
    f = _mm("mm_ffn_in", hf, w["w_ffn_in"], "nn", BF16, m=t, n=2 * FFN_HIDDEN, k=d, tm=tb, tn=cs["w_ffn_in"], tk=d,
            b_sh=cs["w_ffn_in"], comm=plan)
    if plan is not None:
        f, outs = f
        got.update(zip(plan["names"], outs))
    act = (lambda i, j, kk, fg, fu: _ffn_act(fg, fu), [(f, lambda i, j, kk: (i, 0)), (f, lambda i, j, kk: (i, 1))])
    x2, act = _mm("mm_ffn_out", act, w["w_ffn_out"], "nn", F32, m=t, n=d, k=FFN_HIDDEN, tm=256, tn=d, tk=FFN_HIDDEN,
                  res=x1, a_keep=True)
    sv.update(hf=hf, f=f, act=act, x2=x2)
    x3, gp, hp, e = _ple_block(x2, p_l, _row(s["ple_norm_g"]), w["w_ple_in"], w["w_ple_gate"])
    sv.update(e=e, hp=hp, gp=gp, p=p_l)
    return x3, sv, got


def _layer_bwd(dx3, sv, w, s, rope, nbatch, seq):
    t = dx3.shape[0]
    tm = 512
    d = D_MODEL
    gb, gs = {}, {}
    cs = {nm: w[nm].shape[2] for nm in SHARD_MAJOR}
    tb = 1024

    def wg(name, a, b, m, n, tm=1024, tk=1024, shard=None):
        return _mm(name, a, b, "tn", BF16, m=m, n=n, k=t, tm=tm, tn=n if shard is None else cs[shard], tk=tk,
                   o_sh=None if shard is None else cs[shard])

    de, dgp, dx2, gs["ple_norm_g"] = _ple_block_bwd(dx3, sv["gp"], sv["e"], sv["x2"], _row(s["ple_norm_g"]),
                                                    w["w_ple_gate"])
    gb["w_ple_in"] = wg("wg_ple_in", sv["p"], de, sv["p"].shape[1], d, tk=2048)
    gb["w_ple_gate"] = wg("wg_ple_gate", sv["hp"], dgp, d, d, tk=2048)
    fw = FFN_HIDDEN // 2
    dact = _mm("mmb_ffn_out", dx2, w["w_ffn_out"], "nt", BF16, m=t, n=FFN_HIDDEN, k=d, tm=tb, tn=fw, tk=d)
    gb["w_ffn_out"] = wg("wg_ffn_out", sv["act"], dx2, FFN_HIDDEN, d, tm=fw)
    f = sv["f"]

    def df_tile(is_gate, da, fg, fu):
        dfg, dfu = _ffn_act_bwd(da, fg, fu)
        return jnp.where(is_gate, dfg, dfu)

    assert cs["w_ffn_in"] == fw
    df_rows = (lambda i, j, kk, *v: df_tile(kk < 2, *v),
               [(dact, lambda i, j, kk: (i, kk % 2)), (f, lambda i, j, kk: (i, kk % 2)), (f, lambda i, j, kk: (i, 2 + kk % 2))])
    dhf, df = _mm("mmb_ffn_in", df_rows, w["w_ffn_in"], "nt", BF16, m=t, n=d, k=2 * FFN_HIDDEN, tm=512, tn=d, tk=fw,
                  b_sh=fw, a_keep=True)
    gb["w_ffn_in"] = wg("wg_ffn_in", sv["hf"], df, d, 2 * FFN_HIDDEN, shard="w_ffn_in")
    dx1, gs["ffn_norm_g"] = _rowwise("rms_ffn_bwd", _rms_bwd, [R(dhf), R(sv["x1"]), R(dx2), V(_row(s["ffn_norm_g"]))],
                                     [O(d, F32)], [A(d)], tm=tm)
    dm = _mm("mmb_mix", dx1, w["w_mix_out"], "nt", BF16, m=t, n=d, k=d, tm=tb, tn=d, tk=d)
    gb["w_mix_out"] = wg("wg_mix", sv["merged"], dx1, d, d)
    z, glu, bg = sv["z"], sv["glu"], _row(s["b_gate"])
    ins = [R(dm, 512, 0), R(z, 512, 3), R(z, 512, 5), R(z, 512, 7), V(bg, 512, 0), V(bg, 512, 2), V(bg, 512, 4),
           R(sv["ya"], 512, 0), R(glu, 512, 0), R(glu, 512, 2), R(sv["yc"], 512, 0)]
    ob = lambda: O(512, BF16, total=d)
    ab = lambda: A(512, total=d)
    dya, dga, dgb, dyc, d0, d1, d2, db0, db1, db2, dba, dbb = _rowwise(
        "merge_bwd", _merge_bwd, ins, [ob() for _ in range(7)], [ab() for _ in range(5)], tm=tm, ncol=2)
    gs["b_gate"] = jnp.concatenate([db0, db1, db2], axis=1)
    gs["b_ssm_glu"] = jnp.concatenate([dba, dbb], axis=1)
    dglu = jnp.concatenate([dga, dgb], axis=1)
    gb["w_attn_out"] = wg("wg_attn_out", sv["o"], dya, Q_WIDTH, d, tk=2048)
    do = _mm("mmb_attn_out", dya, w["w_attn_out"], "nt", BF16, m=t, n=Q_WIDTH, k=d, tm=tb, tn=Q_WIDTH, tk=d)
    dqkv, dsink = _attn_bwd(sv["qkv"], sv["oh"], _heads(do, N_Q_HEADS), sv["lse"], sv["sinks"], nbatch, seq)
    dqkv = _tokens(dqkv)
    gs["attn_sinks"] = dsink.reshape(-1)
    c, sa, sb = rope
    dq = _rowwise("rope_bwd_q", _rope_bwd_q, [R(dqkv, Q_WIDTH, 0), R(c), R(sa), R(sb)], [O(Q_WIDTH, BF16)], tm=tm)
    dk, dv = _kv_combine(dqkv, c, sa, sb, seq)
    gb["w_ssm_glu"] = wg("wg_ssm_glu", sv["gel"], dglu, SSM_WIDTH, 2 * d, tk=2048)
    dgi = _mm("mmb_glu", dglu, w["w_ssm_glu"], "nt", BF16, m=t, n=SSM_WIDTH, k=2 * d, tm=tb, tn=SSM_WIDTH, tk=2 * d)
    a_re, a_im = sv["a"]
    du, dwb, dwc, dd, da = _ssm_bwd(dgi, sv["ys"], sv["st"], z, sv["wb"].T.astype(BF16), sv["wc"].T.astype(BF16),
                                    _ssm_tables(a_re, a_im, True), sv["dskip"], nbatch, seq)
    gs["ssm_d"] = dd.reshape(-1)
    da = jnp.sum(da, axis=0)
    _, prep_vjp = jax.vjp(_ssm_prep, *[s[nm] for nm in SSM_NAMES])
    for nm, g in zip(SSM_NAMES, prep_vjp((da[:SSM_LANES], da[SSM_LANES:], dwb, dwc))):
        gs[nm] = g
    gb["w_conv_out"] = wg("wg_conv_out", sv["sc"], dyc, CONV_WIDTH, d, tk=2048)
    dsc = _mm("mmb_conv_out", dyc, w["w_conv_out"], "nt", BF16, m=t, n=CONV_WIDTH, k=d, tm=tb, tn=CONV_WIDTH, tk=d)
    dca, dcg, dcw, dcb, dlg, dlb = _conv_bwd(dsc, sv["cv"], z, sv["cw"], _row(s["conv_norm_g"]),
                                             _row(s["conv_norm_b"]), nbatch, seq)
    gs["conv_dw_w"] = dcw[:CONV_K]
    gs["conv_dw_b"], gs["conv_norm_g"], gs["conv_norm_b"] = dcb.reshape(-1), dlg.reshape(-1), dlb.reshape(-1)
    dz = jnp.concatenate([dq, dk, dv, du, dca, dcg, d0, d1, d2], axis=1)
    gb["w_in"] = wg("wg_in", sv["h"], dz, d, dz.shape[1], tk=2048, shard="w_in")
    dh = _mm("mmb_in", dz, w["w_in"], "nt", BF16, m=t, n=d, k=dz.shape[1], tm=tb, tn=d, tk=cs["w_in"],
             b_sh=cs["w_in"])
    dx, gs["mix_norm_g"] = _rowwise("rms_mix_bwd", _rms_bwd, [R(dh), R(sv["x"]), R(dx1), V(_row(s["mix_norm_g"]))],
                                    [O(d, F32)], [A(d)], tm=tm)
    gs["mix_norm_g"], gs["ffn_norm_g"], gs["ple_norm_g"] = (gs[nm].reshape(-1) for nm in
                                                            ("mix_norm_g", "ffn_norm_g", "ple_norm_g"))
    gs["b_gate"], gs["b_ssm_glu"] = gs["b_gate"].reshape(-1), gs["b_ssm_glu"].reshape(-1)
    return dx, {nm: _shard_major(nm, g) for nm, g in gb.items()}, gs


def _rope_tables(positions):
    inv_freq = ROPE_THETA ** (-jnp.arange(0, ROPE_DIM, 2, dtype=F32) / ROPE_DIM)
    ang = positions.reshape(-1).astype(F32)[:, None] * inv_freq
    cos, sin = jnp.cos(ang), jnp.sin(ang)
    t = ang.shape[0]
    rest = HEAD_DIM - ROPE_DIM
    c = jnp.concatenate([cos, cos, jnp.ones((t, rest), F32)], axis=1)
    sa = jnp.concatenate([-sin, jnp.zeros((t, HEAD_DIM - ROPE_HALF), F32)], axis=1)
    sb = jnp.concatenate([jnp.zeros((t, ROPE_HALF), F32), sin, jnp.zeros((t, rest), F32)], axis=1)
    two = lambda v: jnp.concatenate([v, v], axis=1)
    return two(c), two(sa), two(sb)


def _natural(nm, w4):
    if nm in SHARD_MAJOR:
        return w4
    if BIG_AXIS[nm] == 1:
        return w4.reshape(-1, w4.shape[2])
    return w4.transpose(1, 0, 2).reshape(w4.shape[1], -1)


def _shard_major(nm, g):
    if nm in SHARD_MAJOR:
        return g
    if BIG_AXIS[nm] == 1:
        return g.reshape(N_CHIPS, -1, g.shape[1])
    return g.reshape(g.shape[0], N_CHIPS, -1).transpose(1, 0, 2)


def _untap(taps4, cols):
    flat = taps4.reshape(N_CHIPS, -1)[:, :CONV_K * cols]
    return flat.reshape(N_CHIPS, CONV_K, cols).transpose(1, 0, 2).reshape(CONV_K, N_CHIPS * cols)


def _local_step(x, p, positions, loss_target, small, wfull=None, shards=None):
    nbatch, seq, d = x.shape
    depth = p.shape[0]
    t = nbatch * seq
    rope = _rope_tables(positions)
    xs = x.reshape(t, d)
    saved, ws, ss = [], [], []
    got = None if shards is None else _gather_now((shards, 0))
    for l in range(depth):
        w4 = {nm: wfull[nm][l] for nm in BIG} if shards is None else got
        w_l = {nm: _natural(nm, w4[nm]) for nm in BIG}
        s_l = {nm: small[nm][l] for nm in small if nm != "final_norm_g"}
        if shards is not None:
            s_l["conv_dw_w"] = _untap(got[TAPS], CONV_WIDTH // N_CHIPS)
        nxt = (shards, l + 1) if shards is not None and l + 1 < depth else None
        xs, sv, got = _layer_fwd(xs, p[l].reshape(t, -1), w_l, s_l, rope, nbatch, seq, nxt)
        saved.append(sv)
        ws.append(w_l)
        ss.append(s_l)
    dx, loss_cols, dgf = _rowwise("loss_head", _loss_fn, [R(xs), R(loss_target.reshape(t, d)),
                                                          V(_row(small["final_norm_g"]))],
                                  [O(d, F32)], [A(d), A(d)], tm=512)
    gbs, gss = [None] * depth, [None] * depth
    for l in reversed(range(depth)):
        dx, gbs[l], gss[l] = _layer_bwd(dx, saved[l], ws[l], ss[l], rope, nbatch, seq)
    gbig = {nm: jnp.stack([g[nm] for g in gbs]) for nm in BIG}
    gsmall = {nm: jnp.stack([g[nm] for g in gss]) for nm in SMALL if nm != "final_norm_g"}
    gsmall["final_norm_g"] = dgf.reshape(-1)
    return loss_cols, dx.reshape(nbatch, seq, d), gbig, gsmall


HBM = pl.BlockSpec(memory_space=pltpu.HBM)


def _place():
    x, y, c = lax.axis_index("x"), lax.axis_index("y"), lax.axis_index("c")
    chips = [(1 - x, y), (x, 1 - y), (1 - x, 1 - y)]
    return x, y, c, chips


def _remote(src, dst, send_sem, recv_sem, to):
    return pltpu.make_async_remote_copy(src_ref=src, dst_ref=dst, send_sem=send_sem, recv_sem=recv_sem,
                                        device_id=to, device_id_type=MESH)


TAPS = "taps"
GATHER_ALL = (("w_ffn_in", "w_ffn_out"),
              ("w_in", "w_ple_gate", "w_mix_out", "w_attn_out", "w_ssm_glu", "w_conv_out", "w_ple_in", TAPS))
GATHER_A = (("w_ffn_in",), ("w_in", "w_ple_gate"))
GATHER_B = (("w_ffn_out",), ("w_mix_out", "w_attn_out", "w_ssm_glu", "w_conv_out", "w_ple_in", TAPS))


def _gather_plan(shards, sets):
    stacked, layer = shards
    names = sets[0] + sets[1]
    n = len(names)
    idx = {nm: i for i, nm in enumerate(names)}

    def start(ins, outs, sems):
        send1, recv1, _, _, send0, recv0 = sems
        x, y, c, chips = _place()
        me = 2 * x + y
        for i in range(n):
            _remote(ins[i].at[layer], outs[i].at[me], send0.at[i], recv0.at[i], (x, y, 1 - c)).start()
        for role in (0, 1):
            @pl.when(c == role)
            def _():
                for nm in sets[role]:
                    i = idx[nm]
                    for k, (cx, cy) in enumerate(chips):
                        _remote(ins[i].at[layer], outs[i].at[me], send1.at[i, k], recv1.at[i, k], (cx, cy, c)).start()

    def finish(ins, outs, sems):
        send1, recv1, send2, recv2, send0, recv0 = sems
        x, y, c, chips = _place()
        me = 2 * x + y
        sib = (x, y, 1 - c)
        for role in (0, 1):
            @pl.when(c == role)
            def _():
                passed = []
                for nm in sets[role]:
                    i = idx[nm]
                    for k, (cx, cy) in enumerate(chips):
                        slot = outs[i].at[2 * cx + cy]
                        _remote(slot, slot, send1.at[i, k], recv1.at[i, k], (cx, cy, c)).wait_recv()
                        cp = _remote(slot, slot, send2.at[i, k], recv2.at[i, k], sib)
                        cp.start()
                        passed.append(cp)
                for nm in sets[1 - role]:
                    i = idx[nm]
                    for k, (cx, cy) in enumerate(chips):
                        slot = outs[i].at[2 * cx + cy]
                        _remote(slot, slot, send2.at[i, k], recv2.at[i, k], sib).wait_recv()
                for nm in sets[role]:
                    i = idx[nm]
                    for k, (cx, cy) in enumerate(chips):
                        _remote(ins[i].at[layer], outs[i].at[me], send1.at[i, k], recv1.at[i, k],
                                (cx, cy, c)).wait_send()
                for cp in passed:
                    cp.wait_send()
        for i in range(n):
            _remote(ins[i].at[layer], outs[i].at[me], send0.at[i], recv0.at[i], sib).wait()

    ins = [stacked[nm] for nm in names]
    return dict(names=names, ins=ins, start=start, finish=finish,
                out_shapes=[jax.ShapeDtypeStruct((N_CHIPS,) + v.shape[1:], v.dtype) for v in ins],
                sems=[pltpu.SemaphoreType.DMA((n, 3)) for _ in range(4)] + [pltpu.SemaphoreType.DMA((n,))
                                                                            for _ in range(2)])


def _gather_now(shards):
    return _comm_now("gather_weights", _gather_plan(shards, GATHER_ALL))


def _pair_exchange(grads):
    n = len(grads)
    hl = grads[0].shape[0] // 2

    def body(*refs):
        ins, outs = refs[:n], refs[n:2 * n]
        send, recv = refs[2 * n:]
        x, y, c, _ = _place()
        other = pl.ds((1 - c) * hl, hl)
        cps = [_remote(ins[i].at[other], outs[i], send.at[i], recv.at[i], (x, y, 1 - c)) for i in range(n)]
        for cp in cps:
            cp.start()
        for cp in cps:
            cp.wait()

    out_shape = [jax.ShapeDtypeStruct((hl,) + g.shape[1:], g.dtype) for g in grads]
    sems = [pltpu.SemaphoreType.DMA((n,)) for _ in range(2)]
    return pl.pallas_call(body, out_shape=out_shape, in_specs=[HBM] * n, out_specs=[HBM] * n, scratch_shapes=sems,
                          name="reduce_pair_exchange")(*grads)


def _pair_add(g, r):
    hl, _, rr, cc = r.shape
    rows = hl * N_CHIPS * rr
    nblk = rows // rr

    def body(c_ref, g_ref, r_ref, o_ref):
        o_ref[...] = (g_ref[...].astype(F32) + r_ref[...].astype(F32)).astype(o_ref.dtype)

    grid_spec = pltpu.PrefetchScalarGridSpec(
        num_scalar_prefetch=1, grid=(nblk,),
        in_specs=[pl.BlockSpec((rr, cc), lambda i, c_ref: (c_ref[0] * nblk + i, 0)),
                  pl.BlockSpec((rr, cc), lambda i, c_ref: (i, 0))],
        out_specs=pl.BlockSpec((rr, cc), lambda i, c_ref: (i, 0)))
    c = lax.axis_index("c").astype(jnp.int32).reshape(1)
    out = pl.pallas_call(body, out_shape=jax.ShapeDtypeStruct((rows, cc), r.dtype), grid_spec=grid_spec,
                         name="reduce_pair_add", compiler_params=_params(("parallel",)))(
        c, g.reshape(-1, cc), r.reshape(rows, cc))
    return out.reshape(r.shape)


def _chip_exchange(psums):
    n = len(psums)

    def body(*refs):
        ins, got = refs[:n], refs[n:2 * n]
        send, recv = refs[2 * n:]
        x, y, c, chips = _place()
        cps = [_remote(ins[i].at[:, 2 * cx + cy], got[i].at[k], send.at[i, k], recv.at[i, k], (cx, cy, c))
               for i in range(n) for k, (cx, cy) in enumerate(chips)]
        for cp in cps:
            cp.start()
        for cp in cps:
            cp.wait()

    got_shape = [jax.ShapeDtypeStruct((3, p.shape[0]) + p.shape[2:], p.dtype) for p in psums]
    sems = [pltpu.SemaphoreType.DMA((n, 3)), pltpu.SemaphoreType.DMA((n, 3))]
    return pl.pallas_call(body, out_shape=got_shape, in_specs=[HBM] * n, out_specs=[HBM] * n, scratch_shapes=sems,
                          name="reduce_chip_exchange")(*psums)


def _comm_now(name, plan):
    n = len(plan["ins"])

    def body(*refs):
        ins, outs, sems = refs[:n], refs[n:2 * n], refs[2 * n:]
        plan["start"](ins, outs, sems)
        plan["finish"](ins, outs, sems)

    outs = pl.pallas_call(body, out_shape=plan["out_shapes"], in_specs=[HBM] * n, out_specs=[HBM] * n,
                          scratch_shapes=plan["sems"], name=name)(*plan["ins"])
    return dict(zip(plan["names"], outs))


def _sum4(psum, got):
    hl, _, rr, cc = psum.shape
    tr = rr if rr * cc <= 512 * 1024 else rr // 2

    def body(place_ref, own_ref, g0_ref, g1_ref, g2_ref, o_ref):
        tot = (own_ref[...].astype(F32) + g0_ref[...].astype(F32)) + g1_ref[...].astype(F32)
        o_ref[...] = tot + g2_ref[...].astype(F32)

    def got_spec(k):
        return pl.BlockSpec((None, None, tr, cc), lambda h, i, place: (k, h, i, 0))

    grid_spec = pltpu.PrefetchScalarGridSpec(
        num_scalar_prefetch=1, grid=(hl, rr // tr),
        in_specs=[pl.BlockSpec((None, None, tr, cc), lambda h, i, place: (h, place[0], i, 0)),
                  got_spec(0), got_spec(1), got_spec(2)],
        out_specs=pl.BlockSpec((None, tr, cc), lambda h, i, place: (place[1] * hl + h, i, 0)))
    place = jnp.stack([2 * lax.axis_index("x") + lax.axis_index("y"), lax.axis_index("c")]).astype(jnp.int32)
    return pl.pallas_call(body, out_shape=jax.ShapeDtypeStruct((2 * hl, rr, cc), F32), grid_spec=grid_spec,
                          name="reduce_sum4", compiler_params=_params(("parallel", "parallel")))(
        place, psum, got, got, got)


def _pair_gather(sums):
    n = len(sums)
    hl = sums[0].shape[0] // 2

    def body(*refs):
        bufs = refs[n:2 * n]
        send, recv = refs[2 * n:]
        x, y, c, _ = _place()
        mine = pl.ds(c * hl, hl)
        cps = [_remote(bufs[i].at[mine], bufs[i].at[mine], send.at[i], recv.at[i], (x, y, 1 - c)) for i in range(n)]
        for cp in cps:
            cp.start()
        for cp in cps:
            cp.wait()

    out_shape = [jax.ShapeDtypeStruct(v.shape, v.dtype) for v in sums]
    sems = [pltpu.SemaphoreType.DMA((n,)) for _ in range(2)]
    return pl.pallas_call(body, out_shape=out_shape, in_specs=[HBM] * n, out_specs=[HBM] * n, scratch_shapes=sems,
                          input_output_aliases={i: i for i in range(n)}, name="reduce_pair_gather")(*sums)


def _allreduce_small(vec):
    rows = vec.shape[0]

    def body(v_ref, o_ref, all_ref, send, recv):
        x, y, c, _ = _place()
        me = 4 * x + 2 * y + c
        all_ref[me] = v_ref[...]
        cps = []
        for dlt in range(1, N_DEV):
            fx, fy, fc = (dlt >> 2) & 1, (dlt >> 1) & 1, dlt & 1
            to = (1 - x if fx else x, 1 - y if fy else y, 1 - c if fc else c)
            cps.append(_remote(v_ref, all_ref.at[me], send.at[dlt - 1], recv.at[dlt - 1], to))
        for cp in cps:
            cp.start()
        for cp in cps:
            cp.wait()
        tot = all_ref[0]
        for dev in range(1, N_DEV):
            tot = tot + all_ref[dev]
        o_ref[...] = tot

    vm = pl.BlockSpec(memory_space=pltpu.VMEM)
    return pl.pallas_call(
        body, out_shape=jax.ShapeDtypeStruct(vec.shape, F32), in_specs=[vm], out_specs=vm,
        scratch_shapes=[pltpu.VMEM((N_DEV, rows, 128), F32), pltpu.SemaphoreType.DMA((N_DEV - 1,)),
                        pltpu.SemaphoreType.DMA((N_DEV - 1,))],
        name="allreduce_small", compiler_params=pltpu.CompilerParams(vmem_limit_bytes=VMEM_LIMIT))(vec)


def _adamw(name, w, g, m, v):
    rows, cc = w.shape
    tm = rows if rows * cc <= 512 * 1024 else math.gcd(rows, 256)
    return _rowwise(name, _adamw_fn, [R(w), R(g), R(m), R(v)], [O(cc, F32), O(cc, F32), O(cc, F32)], tm=tm)


def _pack(parts):
    flat = jnp.concatenate([v.reshape(-1).astype(F32) for v in parts])
    pad = (-flat.shape[0]) % (SUBLANES * 128)
    return jnp.pad(flat, (0, pad)).reshape(-1, 128)


def _unpack(packed, shapes):
    flat, out, pos = packed.reshape(-1), [], 0
    for shp in shapes:
        size = math.prod(shp)
        out.append(flat[pos:pos + size].reshape(shp))
        pos += size
    return out


def kernel(x, p, positions, mix_norm_g, w_in, b_gate, attn_sinks, w_attn_out, ssm_lambda_re, ssm_lambda_im, ssm_log_dt, ssm_b_re, ssm_b_im, ssm_c_re, ssm_c_im, ssm_d, w_ssm_glu, b_ssm_glu, conv_dw_w, conv_dw_b, conv_norm_g, conv_norm_b, w_conv_out, w_mix_out, ffn_norm_g, w_ffn_in, w_ffn_out, w_ple_in, ple_norm_g, w_ple_gate, final_norm_g, loss_target, m_mix_norm_g, m_w_in, m_b_gate, m_attn_sinks, m_w_attn_out, m_ssm_lambda_re, m_ssm_lambda_im, m_ssm_log_dt, m_ssm_b_re, m_ssm_b_im, m_ssm_c_re, m_ssm_c_im, m_ssm_d, m_w_ssm_glu, m_b_ssm_glu, m_conv_dw_w, m_conv_dw_b, m_conv_norm_g, m_conv_norm_b, m_w_conv_out, m_w_mix_out, m_ffn_norm_g, m_w_ffn_in, m_w_ffn_out, m_w_ple_in, m_ple_norm_g, m_w_ple_gate, m_final_norm_g, v_mix_norm_g, v_w_in, v_b_gate, v_attn_sinks, v_w_attn_out, v_ssm_lambda_re, v_ssm_lambda_im, v_ssm_log_dt, v_ssm_b_re, v_ssm_b_im, v_ssm_c_re, v_ssm_c_im, v_ssm_d, v_w_ssm_glu, v_b_ssm_glu, v_conv_dw_w, v_conv_dw_b, v_conv_norm_g, v_conv_norm_b, v_w_conv_out, v_w_mix_out, v_ffn_norm_g, v_w_ffn_in, v_w_ffn_out, v_w_ple_in, v_ple_norm_g, v_w_ple_gate, v_final_norm_g):
    given = dict(locals())
    wts = {nm: given[nm] for nm in WEIGHTS}
    mom = {nm: given["m_" + nm] for nm in WEIGHTS}
    var = {nm: given["v_" + nm] for nm in WEIGHTS}
    depth = p.shape[0]
    chip = 2 * lax.axis_index("x") + lax.axis_index("y")

    cw_cols = conv_dw_w.shape[2]
    taps = jnp.pad(conv_dw_w.reshape(depth, -1), ((0, 0), (0, (-CONV_K * cw_cols) % (SUBLANES * 128))))
    shards = {**{nm: wts[nm].astype(BF16) for nm in BIG}, TAPS: taps.reshape(depth, -1, 128)}
    small = {nm: wts[nm] for nm in SMALL if nm != "conv_dw_w"}

    loss_cols, grad_x, gbig, gsmall = _local_step(x, p, positions, loss_target, small, shards=shards)

    parts = [loss_cols] + [gsmall[nm] for nm in SMALL]
    total = _allreduce_small(_pack(parts))
    summed = _unpack(total, [v.shape for v in parts])
    loss = jnp.sum(summed[0])
    gsum = dict(zip(SMALL, summed[1:]))
    gsum["conv_dw_w"] = lax.dynamic_slice_in_dim(gsum["conv_dw_w"], chip * cw_cols, cw_cols, axis=2)
    shapes = [wts[nm].shape for nm in SMALL]
    deltas, new_m, new_v = _adamw("adamw_small", _pack([wts[nm] for nm in SMALL]), _pack([gsum[nm] for nm in SMALL]),
                                  _pack([mom[nm] for nm in SMALL]), _pack([var[nm] for nm in SMALL]))
    grads = dict(gsum)
    delta = dict(zip(SMALL, _unpack(deltas, shapes)))
    newm = dict(zip(SMALL, _unpack(new_m, shapes)))
    newv = dict(zip(SMALL, _unpack(new_v, shapes)))

    gl = [gbig[nm] for nm in BIG]
    sib = _pair_exchange(gl)
    psums = [_pair_add(g, r) for g, r in zip(gl, sib)]
    got = _chip_exchange(psums)
    sums = _pair_gather([_sum4(ps, g) for ps, g in zip(psums, got)])
    for nm, g in zip(BIG, sums):
        shp = wts[nm].shape
        two = lambda v: v.reshape(-1, shp[-1])
        g = g.reshape(shp)
        d_w, n_m, n_v = _adamw("adamw_" + nm, two(wts[nm]), two(g), two(mom[nm]), two(var[nm]))
        grads[nm], delta[nm], newm[nm], newv[nm] = g, d_w.reshape(shp), n_m.reshape(shp), n_v.reshape(shp)

    return (loss, grad_x, *[grads[nm] for nm in WEIGHTS], *[delta[nm] for nm in WEIGHTS],
            *[newm[nm] for nm in WEIGHTS], *[newv[nm] for nm in WEIGHTS])
```

```python
import functools
import math

import jax
import jax.numpy as jnp
from jax import lax
from jax.experimental import pallas as pl
from jax.experimental.pallas import tpu as pltpu

F32 = jnp.float32
BF16 = jnp.bfloat16

D_MODEL = 1024
HEAD_DIM = 64
N_Q_HEADS = 8
N_KV_HEADS = 2
GQA_GROUP = N_Q_HEADS // N_KV_HEADS
ATT_BLOCK = 128
ROPE_THETA = 500000.0
ROPE_DIM = HEAD_DIM // 4
ROPE_HALF = ROPE_DIM // 2
Q_WIDTH = N_Q_HEADS * HEAD_DIM
KV_WIDTH = N_KV_HEADS * HEAD_DIM
SSM_WIDTH = 256
SSM_GROUP = 16
SSM_GROUPS = 16
SSM_STATE = 64
SSM_LANES = SSM_GROUPS * SSM_STATE
CONV_WIDTH = 256
CONV_K = 31
CONV_HALO = 32
FFN_HIDDEN = 2816
EPS = 1e-6
NEG_INF = -1e30
SCALE = HEAD_DIM ** -0.5

ADAM_LR = 0.001
ADAM_B1 = 0.9
ADAM_B2 = 0.999
ADAM_EPS = 1e-08
ADAM_WD = 0.01
ADAM_STEP = 10

N_CHIPS = 4
N_DEV = 8
SUBLANES = 8
VMEM_LIMIT = 56 * 1024 * 1024

MESH = pl.DeviceIdType.MESH


def _params(sem=None):
    return pltpu.CompilerParams(dimension_semantics=sem, vmem_limit_bytes=VMEM_LIMIT)


def R(arr, width=None, cb=0, rb=0):
    return ("r", arr, arr.shape[1] if width is None else width, (cb, rb))


def V(arr, width=None, cb=0):
    return ("v", arr, arr.shape[1] if width is None else width, cb)


def _cbf(cb):
    return cb if callable(cb) else (lambda j, c=cb: c + j)


def _rowwise(name, fn, ins, outs, accs=(), *, tm, ncol=1):
    t = [a for k, a, _, _ in ins if k == "r"][0].shape[0]
    tm = min(tm, t)
    assert t % tm == 0, (name, t, tm)
    n_i, n_o, n_a = len(ins), len(outs), len(accs)

    def body(*refs):
        vals = fn(*[r[...] for r in refs[:n_i]])
        if not isinstance(vals, (tuple, list)):
            vals = (vals,)
        for ref, val in zip(refs[n_i:n_i + n_o], vals[:n_o]):
            ref[...] = val.astype(ref.dtype)
        if n_a:
            acc_refs = refs[n_i + n_o:]

            @pl.when(pl.program_id(1) == 0)
            def _():
                for ref in acc_refs:
                    ref[...] = jnp.zeros_like(ref)

            for ref, val in zip(acc_refs, vals[n_o:]):
                ref[...] += val

    in_specs = []
    for kind, arr, width, cb in ins:
        if kind == "r":
            f = _cbf(cb[0])
            in_specs.append(pl.BlockSpec((tm, width), functools.partial(lambda j, i, f, rb: (i + rb, f(j)), f=f, rb=cb[1])))
        else:
            f = _cbf(cb)
            in_specs.append(pl.BlockSpec((arr.shape[0], width), functools.partial(lambda j, i, f: (0, f(j)), f=f)))
    out_specs, out_shape = [], []
    for total, width, cb, dt in outs:
        f = _cbf(cb)
        out_specs.append(pl.BlockSpec((tm, width), functools.partial(lambda j, i, f: (i, f(j)), f=f)))
        out_shape.append(jax.ShapeDtypeStruct((t, total), dt))
    for total, width, cb in accs:
        f = _cbf(cb)
        out_specs.append(pl.BlockSpec((1, width), functools.partial(lambda j, i, f: (0, f(j)), f=f)))
        out_shape.append(jax.ShapeDtypeStruct((1, total), F32))
    sem = ("arbitrary", "arbitrary") if n_a else ("parallel", "parallel")
    res = pl.pallas_call(body, out_shape=out_shape, grid=(ncol, t // tm), in_specs=in_specs, out_specs=out_specs,
                         name=name, compiler_params=_params(sem))(*[a for _, a, _, _ in ins])
    return res[0] if len(res) == 1 else res


def O(width, dtype, total=None, cb=0):
    return (width if total is None else total, width, cb, dtype)


def A(width, total=None, cb=0):
    return (width if total is None else total, width, cb)


_DIMS = {"nn": (((1,), (0,)), ((), ())), "nt": (((1,), (1,)), ((), ())), "tn": (((0,), (0,)), ((), ()))}


def _mm(name, a, b, mode, out_dtype, *, m, n, k, tm, tn, tk, a_off=0, b_off=0, res=None, bias=None, b_sh=None, o_sh=None,
        comm=None, a_keep=False):
    tm, tn, tk = min(tm, m), min(tn, n), min(tk, k)
    assert m % tm == 0 and n % tn == 0 and k % tk == 0, (name, m, n, k, tm, tn, tk)
    nk = k // tk
    has_res, has_bias = res is not None, bias is not None
    a_fn, a_ops = a if isinstance(a, tuple) else (None, [(a, None)])
    b_fn, b_ops = b if isinstance(b, tuple) else (None, [(b, None)])
    na, nb_ = len(a_ops), len(b_ops)
    a_bytes = sum(m * k * arr.dtype.itemsize for arr, _ in a_ops)
    b_bytes = sum(n * k * arr.dtype.itemsize for arr, _ in b_ops)
    swap = nk == 1 and b_bytes + (n // tn) * a_bytes < a_bytes + (m // tm) * b_bytes
    grid = (n // tn, m // tm, nk) if swap else (m // tm, n // tn, nk)
    ncomm = 0 if comm is None else len(comm["ins"])

    def body(*refs):
        g0, g1, kk = pl.program_id(0), pl.program_id(1), pl.program_id(2)
        gi, gj = (g1, g0) if swap else (g0, g1)
        a_tiles = [r[...] for r in refs[:na]]
        b_tiles = [r[...] for r in refs[na:na + nb_]]
        a_val = a_tiles[0] if a_fn is None else a_fn(gi, gj, kk, *a_tiles)
        b_val = b_tiles[0] if b_fn is None else b_fn(gi, gj, kk, *b_tiles)
        pos = na + nb_
        res_ref = bias_ref = None
        if has_res:
            res_ref = refs[pos]
            pos += 1
        if has_bias:
            bias_ref = refs[pos]
            pos += 1
        comm_ins = refs[pos:pos + ncomm]
        o_ref = refs[pos + ncomm]
        comm_outs = refs[pos + ncomm + 1:pos + 2 * ncomm + 1]
        scratch = refs[pos + 2 * ncomm + 1:]
        if a_keep:
            scratch[0][...] = a_val.astype(BF16)
            scratch = scratch[1:]
        if comm is not None:
            sems = scratch[1:] if nk > 1 else scratch

            @pl.when((g0 == 0) & (g1 == 0) & (kk == 0))
            def _():
                comm["start"](comm_ins, comm_outs, sems)

        def finish(r):
            if has_bias:
                r = r + bias_ref[...]
            if has_res:
                r = r + res_ref[...].astype(F32)
            o_ref[...] = r.astype(o_ref.dtype)

        part = lax.dot_general(a_val.astype(BF16), b_val.astype(BF16), _DIMS[mode], preferred_element_type=F32)
        if nk == 1:
            finish(part)
        else:
            acc_ref = scratch[0]

            @pl.when(kk == 0)
            def _():
                acc_ref[...] = part

            @pl.when(kk > 0)
            def _():
                acc_ref[...] += part

            @pl.when(kk == nk - 1)
            def _():
                finish(acc_ref[...])

        if comm is not None:
            @pl.when((g0 == grid[0] - 1) & (g1 == grid[1] - 1) & (kk == nk - 1))
            def _():
                comm["finish"](comm_ins, comm_outs, sems)

    def at(f):
        return (lambda g0, g1, kk: f(g1, g0, kk)) if swap else f

    if mode == "nn":
        a_spec = pl.BlockSpec((tm, tk), at(lambda i, j, kk: (i, kk + a_off)))
        b_spec = pl.BlockSpec((tk, tn), at(lambda i, j, kk: (kk, j + b_off)))
        if b_sh is not None:
            assert b_sh % tn == 0, (name, b_sh, tn)
            per = b_sh // tn
            b_spec = pl.BlockSpec((None, tk, tn), at(lambda i, j, kk: (j // per, kk, j % per)))
    elif mode == "nt":
        a_spec = pl.BlockSpec((tm, tk), at(lambda i, j, kk: (i, kk + a_off)))
        b_spec = pl.BlockSpec((tn, tk), at(lambda i, j, kk: (j, kk + b_off)))
        if b_sh is not None:
            assert b_sh % tk == 0, (name, b_sh, tk)
            per = b_sh // tk
            b_spec = pl.BlockSpec((None, tn, tk), at(lambda i, j, kk: (kk // per, j, kk % per)))
    else:
        a_spec = pl.BlockSpec((tk, tm), at(lambda i, j, kk: (kk, i + a_off)))
        b_spec = pl.BlockSpec((tk, tn), at(lambda i, j, kk: (kk, j + b_off)))
    a_specs = [a_spec] if a_fn is None else [pl.BlockSpec(a_spec.block_shape, at(f)) for _, f in a_ops]
    b_specs = [b_spec] if b_fn is None else [pl.BlockSpec(b_spec.block_shape, at(f)) for _, f in b_ops]
    in_specs, args = a_specs + b_specs, [arr for arr, _ in a_ops] + [arr for arr, _ in b_ops]
    if has_res:
        in_specs.append(pl.BlockSpec((tm, tn), at(lambda i, j, kk: (i, j))))
        args.append(res)
    if has_bias:
        in_specs.append(pl.BlockSpec((1, tn), at(lambda i, j, kk: (0, j))))
        args.append(bias)
    out_spec, out_shape = pl.BlockSpec((tm, tn), at(lambda i, j, kk: (i, j))), (m, n)
    if o_sh is not None:
        assert o_sh % tn == 0, (name, o_sh, tn)
        per_o = o_sh // tn
        out_spec = pl.BlockSpec((None, tm, tn), at(lambda i, j, kk: (j // per_o, i, j % per_o)))
        out_shape = (n // o_sh, m, o_sh)
    scratch = [pltpu.VMEM((tm, tn), F32)] if nk > 1 else []
    if a_keep:
        assert comm is None and o_sh is None and mode != "tn" and n == tn, name
        outs = pl.pallas_call(
            body, out_shape=[jax.ShapeDtypeStruct(out_shape, out_dtype), jax.ShapeDtypeStruct((m, k), BF16)], grid=grid,
            in_specs=in_specs, out_specs=[out_spec, pl.BlockSpec((tm, tk), at(lambda i, j, kk: (i, kk)))],
            scratch_shapes=scratch, name=name, compiler_params=_params(("parallel", "parallel", "arbitrary")))(*args)
        return outs[0], outs[1]
    if comm is None:
        return pl.pallas_call(
            body, out_shape=jax.ShapeDtypeStruct(out_shape, out_dtype), grid=grid, in_specs=in_specs,
            out_specs=out_spec, scratch_shapes=scratch, name=name,
            compiler_params=_params(("parallel", "parallel", "arbitrary")))(*args)
    outs = pl.pallas_call(
        body, out_shape=[jax.ShapeDtypeStruct(out_shape, out_dtype)] + comm["out_shapes"], grid=grid,
        in_specs=in_specs + [HBM] * ncomm, out_specs=[out_spec] + [HBM] * ncomm,
        scratch_shapes=scratch + comm["sems"], name=name,
        compiler_params=_params(("arbitrary", "arbitrary", "arbitrary")))(*args, *comm["ins"])
    return outs[0], outs[1:]


def _sig(v):
    return jax.nn.sigmoid(v)


def _rms_fwd(x, g):
    r = lax.rsqrt(jnp.mean(x * x, axis=-1, keepdims=True) + EPS)
    return x * r * g


def _rms_bwd(dh, x, dres, g):
    dh = dh.astype(F32)
    r = lax.rsqrt(jnp.mean(x * x, axis=-1, keepdims=True) + EPS)
    xh = x * r
    dxh = dh * g
    dx = r * (dxh - xh * jnp.mean(dxh * xh, axis=-1, keepdims=True))
    return dres + dx, jnp.sum(dh * xh, axis=0, keepdims=True)


def _rope_apply(t, c, sa, sb):
    w = t.shape[1]
    return t * c + pltpu.roll(t, w - ROPE_HALF, 1) * sa + pltpu.roll(t, ROPE_HALF, 1) * sb


def _rope_transpose(g, c, sa, sb):
    w = g.shape[1]
    return g * c + pltpu.roll(g * sa, ROPE_HALF, 1) + pltpu.roll(g * sb, w - ROPE_HALF, 1)


def _tile_lanes(tab, reps):
    return jnp.concatenate([tab] * reps, axis=1) if reps > 1 else tab


def _rope_fwd(q, k, v, c, sa, sb):
    rq = Q_WIDTH // c.shape[1]
    qr = _rope_apply(q.astype(F32), _tile_lanes(c, rq), _tile_lanes(sa, rq), _tile_lanes(sb, rq))
    kr = _rope_apply(k.astype(F32), c, sa, sb)
    return jnp.concatenate([qr, kr, v.astype(F32)], axis=1)


def _rope_bwd_q(g, c, sa, sb):
    rq = Q_WIDTH // c.shape[1]
    return _rope_transpose(g.astype(F32), _tile_lanes(c, rq), _tile_lanes(sa, rq), _tile_lanes(sb, rq))


def _gelu(v):
    return jax.nn.gelu(v, approximate=True)


def _gelu_grad(v):
    c0 = math.sqrt(2.0 / math.pi)
    inner = c0 * (v + 0.044715 * v * v * v)
    th = jnp.tanh(inner)
    return 0.5 * (1.0 + th) + 0.5 * v * (1.0 - th * th) * c0 * (1.0 + 3 * 0.044715 * v * v)


def _merge_fwd(g0, g1, g2, b0, b1, b2, ya, ga, gb, yc):
    s0 = _sig(g0.astype(F32) + b0)
    s1 = _sig(g1.astype(F32) + b1)
    s2 = _sig(g2.astype(F32) + b2)
    ys = ga.astype(F32) * _sig(gb.astype(F32))
    return s0 * ya.astype(F32) + s1 * ys + s2 * yc.astype(F32)


def _merge_bwd(dm, g0, g1, g2, b0, b1, b2, ya, ga, gb, yc):
    dm = dm.astype(F32)
    s0 = _sig(g0.astype(F32) + b0)
    s1 = _sig(g1.astype(F32) + b1)
    s2 = _sig(g2.astype(F32) + b2)
    ga = ga.astype(F32)
    sb = _sig(gb.astype(F32))
    ys = ga * sb
    dya = dm * s0
    dys = dm * s1
    dyc = dm * s2
    dga = dys * sb
    dgb = dys * ga * sb * (1.0 - sb)
    d0 = dm * ya.astype(F32) * s0 * (1.0 - s0)
    d1 = dm * ys * s1 * (1.0 - s1)
    d2 = dm * yc.astype(F32) * s2 * (1.0 - s2)
    cs = lambda v: jnp.sum(v, axis=0, keepdims=True)
    return dya, dga, dgb, dyc, d0, d1, d2, cs(d0), cs(d1), cs(d2), cs(dga), cs(dgb)


def _ffn_act(fg, fu):
    fg = fg.astype(F32)
    return fg * _sig(fg) * fu.astype(F32)


def _ffn_act_bwd(da, fg, fu):
    da, fg, fu = da.astype(F32), fg.astype(F32), fu.astype(F32)
    s = _sig(fg)
    return da * fu * (s * (1.0 + fg * (1.0 - s))), da * fg * s


def _ple_fwd(x, gp, e):
    return x + _sig(gp.astype(F32)) * e.astype(F32)


def _ple_bwd(dx, gp, e):
    s = _sig(gp.astype(F32))
    e = e.astype(F32)
    return dx * s, dx * e * s * (1.0 - s)


def _loss_fn(x, tgt, g):
    d = x.shape[1]
    r = lax.rsqrt(jnp.mean(x * x, axis=-1, keepdims=True) + EPS)
    xh = x * r
    err = xh * g - tgt
    dy = err * (1.0 / d)
    dxh = dy * g
    dx = r * (dxh - xh * jnp.mean(dxh * xh, axis=-1, keepdims=True))
    return dx, jnp.sum(err * err, axis=0, keepdims=True) * (0.5 / d), jnp.sum(dy * xh, axis=0, keepdims=True)


def _adamw_fn(w, g, m, v):
    m = ADAM_B1 * m + (1.0 - ADAM_B1) * g
    v = ADAM_B2 * v + (1.0 - ADAM_B2) * (g * g)
    m_hat = m / (1.0 - ADAM_B1 ** ADAM_STEP)
    v_hat = v / (1.0 - ADAM_B2 ** ADAM_STEP)
    delta = -ADAM_LR * (m_hat / (jnp.sqrt(v_hat) + ADAM_EPS) + ADAM_WD * w)
    return delta, m, v


def _band_mask(n):
    qi = lax.broadcasted_iota(jnp.int32, (ATT_BLOCK, 2 * ATT_BLOCK), 0)
    kj = lax.broadcasted_iota(jnp.int32, (ATT_BLOCK, 2 * ATT_BLOCK), 1)
    dist = qi + ATT_BLOCK - kj
    return (dist >= 0) & (dist < ATT_BLOCK) & ((n > 0) | (kj >= ATT_BLOCK))


K_HEADS_AT = N_Q_HEADS // N_KV_HEADS


def _att_specs(nb):
    qs = pl.BlockSpec((N_Q_HEADS, ATT_BLOCK, HEAD_DIM), lambda b, n: (0, b * nb + n, 0))

    def kv(head_block, back):
        return pl.BlockSpec((N_KV_HEADS, ATT_BLOCK, HEAD_DIM),
                            lambda b, n: (head_block, b * nb + jnp.maximum(n - back, 0), 0))

    stat = pl.BlockSpec((N_Q_HEADS, ATT_BLOCK, 1), lambda b, n: (0, b * nb + n, 0))
    sink = pl.BlockSpec((N_Q_HEADS, 1, 1), lambda b, n: (0, 0, 0))
    return qs, [kv(K_HEADS_AT, 1), kv(K_HEADS_AT, 0), kv(K_HEADS_AT + 1, 1), kv(K_HEADS_AT + 1, 0)], stat, sink


def _attn_fwd(qkv, sinks, nbatch, seq):
    t = qkv.shape[1]
    nb = seq // ATT_BLOCK
    qs, kvs, stat, sink = _att_specs(nb)

    def body(q_ref, kp_ref, kc_ref, vp_ref, vc_ref, sink_ref, o_ref, lse_ref):
        mask = _band_mask(pl.program_id(1))
        rows = GQA_GROUP * ATT_BLOCK
        for kv in range(N_KV_HEADS):
            hs = slice(kv * GQA_GROUP, (kv + 1) * GQA_GROUP)
            kk = jnp.concatenate([kp_ref[kv], kc_ref[kv]], axis=0)
            vv = jnp.concatenate([vp_ref[kv], vc_ref[kv]], axis=0)
            q4 = (q_ref[hs] * SCALE).reshape(rows, HEAD_DIM)
            s = lax.dot_general(q4, kk, _DIMS["nt"], preferred_element_type=F32)
            s = jnp.where(mask, s.reshape(GQA_GROUP, ATT_BLOCK, 2 * ATT_BLOCK), NEG_INF)
            sk = sink_ref[hs]
            mx = jnp.maximum(jnp.max(s, axis=-1, keepdims=True), sk)
            p = jnp.exp(s - mx)
            den = jnp.sum(p, axis=-1, keepdims=True) + jnp.exp(sk - mx)
            o = lax.dot_general(p.reshape(rows, 2 * ATT_BLOCK).astype(BF16), vv, _DIMS["nn"],
                                preferred_element_type=F32).reshape(GQA_GROUP, ATT_BLOCK, HEAD_DIM)
            o_ref[hs] = (o * (1.0 / den)).astype(o_ref.dtype)
            lse_ref[hs] = mx + jnp.log(den)

    return pl.pallas_call(
        body, grid=(nbatch, nb), in_specs=[qs] + kvs + [sink], out_specs=[qs, stat],
        out_shape=[jax.ShapeDtypeStruct((N_Q_HEADS, t, HEAD_DIM), BF16), jax.ShapeDtypeStruct((N_Q_HEADS, t, 1), F32)],
        name="attn_fwd", compiler_params=_params(("parallel", "parallel")))(qkv, qkv, qkv, qkv, qkv, sinks)


def _attn_bwd(qkv, oh, doh, lse, sinks, nbatch, seq):
    t = qkv.shape[1]
    nb = seq // ATT_BLOCK
    qs, kvs, stat, sink = _att_specs(nb)

    def body(q_ref, kp_ref, kc_ref, vp_ref, vc_ref, o_ref, do_ref, lse_ref, sink_ref, dqkv_ref, dsink_ref):
        dq_ref = dqkv_ref.at[pl.ds(0, N_Q_HEADS)]
        dkc_ref, dvc_ref, dkp_ref, dvp_ref = (dqkv_ref.at[pl.ds(N_Q_HEADS + N_KV_HEADS * i, N_KV_HEADS)]
                                              for i in range(4))
        first = (pl.program_id(0) == 0) & (pl.program_id(1) == 0)

        @pl.when(first)
        def _():
            dsink_ref[...] = jnp.zeros_like(dsink_ref)

        mask = _band_mask(pl.program_id(1))
        rows = GQA_GROUP * ATT_BLOCK
        band = (GQA_GROUP, ATT_BLOCK, 2 * ATT_BLOCK)
        for kv in range(N_KV_HEADS):
            hs = slice(kv * GQA_GROUP, (kv + 1) * GQA_GROUP)
            kk = jnp.concatenate([kp_ref[kv], kc_ref[kv]], axis=0)
            vv = jnp.concatenate([vp_ref[kv], vc_ref[kv]], axis=0)
            q4 = q_ref[hs].reshape(rows, HEAD_DIM)
            do4 = do_ref[hs].reshape(rows, HEAD_DIM)
            lse4 = lse_ref[hs]
            s = lax.dot_general(q4 * SCALE, kk, _DIMS["nt"], preferred_element_type=F32).reshape(band)
            p = jnp.where(mask, jnp.exp(s - lse4), 0.0)
            dd = jnp.sum(do_ref[hs].astype(F32) * o_ref[hs].astype(F32), axis=-1, keepdims=True)
            dp = lax.dot_general(do4, vv, _DIMS["nt"], preferred_element_type=F32).reshape(band)
            ds = (p * (dp - dd) * SCALE).astype(BF16).reshape(rows, 2 * ATT_BLOCK)
            dq = lax.dot_general(ds, kk, _DIMS["nn"], preferred_element_type=F32)
            dq_ref[hs] = dq.reshape(GQA_GROUP, ATT_BLOCK, HEAD_DIM).astype(dq_ref.dtype)
            dk = lax.dot_general(ds, q4, _DIMS["tn"], preferred_element_type=F32)
            dv = lax.dot_general(p.astype(BF16).reshape(rows, 2 * ATT_BLOCK), do4, _DIMS["tn"],
                                 preferred_element_type=F32)
            dsink_ref[hs] += -jnp.sum(jnp.exp(sink_ref[hs] - lse4) * dd, axis=1, keepdims=True)
            dkp_ref[kv] = dk[:ATT_BLOCK]
            dkc_ref[kv] = dk[ATT_BLOCK:]
            dvp_ref[kv] = dv[:ATT_BLOCK]
            dvc_ref[kv] = dv[ATT_BLOCK:]

    n_out = 2 * N_Q_HEADS
    return pl.pallas_call(
        body, grid=(nbatch, nb), in_specs=[qs] + kvs + [qs, qs, stat, sink],
        out_specs=[pl.BlockSpec((n_out, ATT_BLOCK, HEAD_DIM), lambda b, n: (0, b * nb + n, 0)), sink],
        out_shape=[jax.ShapeDtypeStruct((n_out, t, HEAD_DIM), F32), jax.ShapeDtypeStruct((N_Q_HEADS, 1, 1), F32)],
        name="attn_bwd", compiler_params=_params(("arbitrary", "arbitrary")))(
        qkv, qkv, qkv, qkv, qkv, oh, doh, lse, sinks)


def _kv_combine(dqkv, c, sa, sb, seq):
    t = dqkv.shape[0]
    nb = seq // ATT_BLOCK
    nblk = t // ATT_BLOCK
    col0 = Q_WIDTH // KV_WIDTH

    def body(kc_ref, kp_ref, vc_ref, vp_ref, c_ref, sa_ref, sb_ref, dk_ref, dv_ref):
        has_next = (pl.program_id(0) % nb) != nb - 1
        dk = kc_ref[...] + jnp.where(has_next, kp_ref[...], 0.0)
        dv = vc_ref[...] + jnp.where(has_next, vp_ref[...], 0.0)
        dk_ref[...] = _rope_transpose(dk, c_ref[...], sa_ref[...], sb_ref[...]).astype(dk_ref.dtype)
        dv_ref[...] = dv.astype(dv_ref.dtype)

    cur = pl.BlockSpec((ATT_BLOCK, KV_WIDTH), lambda i: (i, 0))
    own = lambda col: pl.BlockSpec((ATT_BLOCK, KV_WIDTH), lambda i: (i, col0 + col))
    nxt = lambda col: pl.BlockSpec((ATT_BLOCK, KV_WIDTH), lambda i: (jnp.minimum(i + 1, nblk - 1), col0 + col))
    o = jax.ShapeDtypeStruct((t, KV_WIDTH), BF16)
    return pl.pallas_call(body, grid=(nblk,), in_specs=[own(0), nxt(2), own(1), nxt(3), cur, cur, cur],
                          out_specs=[cur, cur], out_shape=[o, o], name="kv_combine",
                          compiler_params=_params(("parallel",)))(dqkv, dqkv, dqkv, dqkv, c, sa, sb)


def _scan_block(ref, tab_ref, carry, ngroups, reverse):
    shifts = (7, 6, 4) if reverse else (1, 2, 4)
    n = SSM_LANES

    def step(i, car):
        g = (ngroups - 1 - i) if reverse else i
        r0 = pl.multiple_of(g * SUBLANES, SUBLANES)
        xr = ref[pl.ds(r0, SUBLANES), :n]
        xi = ref[pl.ds(r0, SUBLANES), n:]
        for s, sh in enumerate(shifts):
            pr, pi = tab_ref[2 * s], tab_ref[2 * s + 1]
            yr, yi = pltpu.roll(xr, sh, 0), pltpu.roll(xi, sh, 0)
            xr, xi = xr + pr * yr - pi * yi, xi + pr * yi + pi * yr
        cr, ci = car
        qr, qi = tab_ref[6], tab_ref[7]
        xr, xi = xr + qr * cr - qi * ci, xi + qr * ci + qi * cr
        ref[pl.ds(r0, SUBLANES), :n] = xr
        ref[pl.ds(r0, SUBLANES), n:] = xi
        last = r0 if reverse else r0 + SUBLANES - 1
        return ref[pl.ds(last, 1), :n], ref[pl.ds(last, 1), n:]

    return lax.fori_loop(0, ngroups, step, carry, unroll=2)


def _ssm_chunk(seq):
    return min(512, seq)


def _ssm_fwd(z, wb, wc, tab, dskip, nbatch, seq):
    t = z.shape[0]
    tc = _ssm_chunk(seq)
    nc = seq // tc
    n2 = 2 * SSM_LANES

    def body(u_ref, wb_ref, wc_ref, tab_ref, d_ref, st_ref, y_ref, gel_ref, car_ref):
        @pl.when(pl.program_id(1) == 0)
        def _():
            car_ref[...] = jnp.zeros_like(car_ref)

        u = u_ref[...]
        st_ref[...] = lax.dot_general(u, wb_ref[...], _DIMS["nn"], preferred_element_type=F32)
        cr, ci = _scan_block(st_ref, tab_ref, (car_ref[:, :SSM_LANES], car_ref[:, SSM_LANES:]), tc // SUBLANES, False)
        car_ref[:, :SSM_LANES] = cr
        car_ref[:, SSM_LANES:] = ci
        y = lax.dot_general(st_ref[...].astype(BF16), wc_ref[...], _DIMS["nn"], preferred_element_type=F32)
        y = y + d_ref[...] * u.astype(F32)
        y_ref[...] = y
        gel_ref[...] = _gelu(y).astype(gel_ref.dtype)

    row = lambda b, c: (b * nc + c, 0)
    full = lambda b, c: (0, 0)
    return pl.pallas_call(
        body, grid=(nbatch, nc),
        in_specs=[pl.BlockSpec((tc, SSM_WIDTH), lambda b, c: (b * nc + c, 3)), pl.BlockSpec((SSM_WIDTH, n2), full),
                  pl.BlockSpec((n2, SSM_WIDTH), full), pl.BlockSpec((8, SUBLANES, SSM_LANES), lambda b, c: (0, 0, 0)),
                  pl.BlockSpec((1, SSM_WIDTH), full)],
        out_specs=[pl.BlockSpec((tc, n2), row), pl.BlockSpec((tc, SSM_WIDTH), row), pl.BlockSpec((tc, SSM_WIDTH), row)],
        out_shape=[jax.ShapeDtypeStruct((t, n2), F32), jax.ShapeDtypeStruct((t, SSM_WIDTH), F32),
                   jax.ShapeDtypeStruct((t, SSM_WIDTH), BF16)],
        scratch_shapes=[pltpu.VMEM((1, n2), F32)], name="ssm_fwd",
        compiler_params=_params(("arbitrary", "arbitrary")))(z, wb, wc, tab, dskip)


def _ssm_bwd(dgi, ys, st, z, wbt, wct, tab_rev, dskip, nbatch, seq):
    t = z.shape[0]
    tc = _ssm_chunk(seq)
    nc = seq // tc
    n = SSM_LANES
    n2 = 2 * n
    ng = tc // SUBLANES

    def body(dgi_ref, ys_ref, st_ref, stp_ref, u_ref, wbt_ref, wct_ref, tab_ref, d_ref,
             du_ref, dwb_ref, dwc_ref, dd_ref, da_ref, p_ref, sb_ref, car_ref):
        b, c = pl.program_id(0), pl.program_id(1)
        ct = nc - 1 - c

        @pl.when((b == 0) & (c == 0))
        def _():
            dwb_ref[...] = jnp.zeros_like(dwb_ref)
            dwc_ref[...] = jnp.zeros_like(dwc_ref)
            dd_ref[...] = jnp.zeros_like(dd_ref)
            da_ref[...] = jnp.zeros_like(da_ref)

        @pl.when(c == 0)
        def _():
            car_ref[...] = jnp.zeros_like(car_ref)

        u = u_ref[...]
        dys = dgi_ref[...].astype(F32) * _gelu_grad(ys_ref[...])
        dys_b = dys.astype(BF16)
        st = st_ref[...]
        dd_ref[...] += jnp.sum(dys * u.astype(F32), axis=0, keepdims=True)
        dwc_ref[...] += lax.dot_general(st.astype(BF16), dys_b, _DIMS["tn"], preferred_element_type=F32)
        p_ref[...] = lax.dot_general(dys_b, wct_ref[...], _DIMS["nn"], preferred_element_type=F32)
        cr, ci = _scan_block(p_ref, tab_ref, (car_ref[:, :n], car_ref[:, n:]), ng, True)
        car_ref[:, :n] = cr
        car_ref[:, n:] = ci
        p = p_ref[...]
        pb = p.astype(BF16)
        dwb_ref[...] += lax.dot_general(u, pb, _DIMS["tn"], preferred_element_type=F32)
        du = lax.dot_general(pb, wbt_ref[...], _DIMS["nn"], preferred_element_type=F32) + d_ref[...] * dys
        du_ref[...] = du.astype(du_ref.dtype)
        sb_ref[pl.ds(0, SUBLANES), :] = jnp.where(ct > 0, stp_ref[...], 0.0)
        sb_ref[pl.ds(SUBLANES, tc), :] = st
        row0 = lax.broadcasted_iota(jnp.int32, (SUBLANES, n), 0) == 0

        def acc_step(g, acc):
            ar, ai = acc
            r0 = pl.multiple_of(g * SUBLANES, SUBLANES)
            edge_r = sb_ref[pl.ds(r0 + SUBLANES - 1, 1), :n]
            edge_i = sb_ref[pl.ds(r0 + SUBLANES - 1, 1), n:]
            sr = jnp.where(row0, edge_r, pltpu.roll(sb_ref[pl.ds(r0 + SUBLANES, SUBLANES), :n], 1, 0))
            si = jnp.where(row0, edge_i, pltpu.roll(sb_ref[pl.ds(r0 + SUBLANES, SUBLANES), n:], 1, 0))
            pr = p_ref[pl.ds(r0, SUBLANES), :n]
            pi = p_ref[pl.ds(r0, SUBLANES), n:]
            return ar + pr * sr + pi * si, ai + pi * sr - pr * si

        zero = jnp.zeros((SUBLANES, n), F32)
        ar, ai = lax.fori_loop(0, ng, acc_step, (zero, zero), unroll=2)
        da_ref[:, :n] += ar
        da_ref[:, n:] += ai

    row = lambda b, c: (b * nc + (nc - 1 - c), 0)
    prev8 = lambda b, c: (jnp.maximum((b * nc + (nc - 1 - c)) * (tc // SUBLANES) - 1, 0), 0)
    full = lambda b, c: (0, 0)
    return pl.pallas_call(
        body, grid=(nbatch, nc),
        in_specs=[pl.BlockSpec((tc, SSM_WIDTH), row), pl.BlockSpec((tc, SSM_WIDTH), row), pl.BlockSpec((tc, n2), row),
                  pl.BlockSpec((SUBLANES, n2), prev8),
                  pl.BlockSpec((tc, SSM_WIDTH), lambda b, c: (b * nc + (nc - 1 - c), 3)),
                  pl.BlockSpec((n2, SSM_WIDTH), full), pl.BlockSpec((SSM_WIDTH, n2), full),
                  pl.BlockSpec((8, SUBLANES, n), lambda b, c: (0, 0, 0)), pl.BlockSpec((1, SSM_WIDTH), full)],
        out_specs=[pl.BlockSpec((tc, SSM_WIDTH), row), pl.BlockSpec((SSM_WIDTH, n2), full),
                   pl.BlockSpec((n2, SSM_WIDTH), full), pl.BlockSpec((1, SSM_WIDTH), full),
                   pl.BlockSpec((SUBLANES, n2), full)],
        out_shape=[jax.ShapeDtypeStruct((t, SSM_WIDTH), BF16), jax.ShapeDtypeStruct((SSM_WIDTH, n2), F32),
                   jax.ShapeDtypeStruct((n2, SSM_WIDTH), F32), jax.ShapeDtypeStruct((1, SSM_WIDTH), F32),
                   jax.ShapeDtypeStruct((SUBLANES, n2), F32)],
        scratch_shapes=[pltpu.VMEM((tc, n2), F32), pltpu.VMEM((tc + SUBLANES, n2), F32), pltpu.VMEM((1, n2), F32)],
        name="ssm_bwd", compiler_params=_params(("arbitrary", "arbitrary")))(
        dgi, ys, st, st, z, wbt, wct, tab_rev, dskip)


def _ssm_prep(lam_re, lam_im, log_dt, b_re, b_im, c_re, c_im):
    lr = jnp.minimum(lam_re, -1e-4)
    li = lam_im
    dt = jnp.exp(log_dt)[:, None]
    mag = jnp.exp(lr * dt)
    a_re = mag * jnp.cos(li * dt)
    a_im = mag * jnp.sin(li * dt)
    den = lr * lr + li * li
    x_re, x_im = a_re - 1.0, a_im
    f_re = (x_re * lr + x_im * li) / den
    f_im = (x_im * lr - x_re * li) / den
    bb_re = f_re[..., None] * b_re - f_im[..., None] * b_im
    bb_im = f_re[..., None] * b_im + f_im[..., None] * b_re
    eye = jnp.eye(SSM_GROUPS, dtype=F32)
    emb_b = lambda v: jnp.einsum("gnh,gk->ghkn", v, eye).reshape(SSM_WIDTH, SSM_LANES)
    emb_c = lambda v: jnp.einsum("ghn,gk->gnkh", v, eye).reshape(SSM_LANES, SSM_WIDTH)
    wb = jnp.concatenate([emb_b(bb_re), emb_b(bb_im)], axis=1)
    wc = jnp.concatenate([emb_c(c_re), -emb_c(c_im)], axis=0)
    return a_re.reshape(-1), a_im.reshape(-1), wb, wc


def _ssm_tables(a_re, a_im, reverse):
    if reverse:
        a_im = -a_im
    pw = [(a_re, a_im)]
    for _ in range(SUBLANES - 1):
        pr, pi = pw[-1]
        pw.append((pr * a_re - pi * a_im, pr * a_im + pi * a_re))
    rows = jnp.arange(SUBLANES)[:, None]
    tabs = []
    for k in (1, 2, 4):
        ok = (rows + k <= SUBLANES - 1) if reverse else (rows >= k)
        tabs += [jnp.where(ok, pw[k - 1][0][None], 0.0), jnp.where(ok, pw[k - 1][1][None], 0.0)]
    order = list(range(SUBLANES - 1, -1, -1)) if reverse else list(range(SUBLANES))
    tabs += [jnp.stack([pw[i][0] for i in order]), jnp.stack([pw[i][1] for i in order])]
    return jnp.stack(tabs)


def _conv_chunk(seq):
    return min(512, seq)


def _shifted(buf, sh, tc, offsets):
    for b in range(SUBLANES):
        idx = [i for i, o in enumerate(offsets) if o % SUBLANES == b]
        if not idx:
            continue
        src = buf
        if b:
            span = tc + SUBLANES * max(offsets[i] // SUBLANES for i in idx)
            sh[pl.ds(0, span), :] = buf[pl.ds(b, span), :]
            src = sh
        for i in idx:
            yield i, src[pl.ds(offsets[i] // SUBLANES * SUBLANES, tc), :]


def _conv_fwd(z, w, bias, lg, lb, nbatch, seq):
    t = z.shape[0]
    tc = _conv_chunk(seq)
    nc = seq // tc

    def body(a_ref, g_ref, w_ref, b_ref, lg_ref, lb_ref, cv_ref, sc_ref, ubuf, sh):
        c = pl.program_id(1)

        @pl.when(c == 0)
        def _():
            ubuf[pl.ds(0, CONV_HALO), :] = jnp.zeros((CONV_HALO, CONV_WIDTH), F32)

        @pl.when(c > 0)
        def _():
            ubuf[pl.ds(0, CONV_HALO), :] = ubuf[pl.ds(tc, CONV_HALO), :]

        ubuf[pl.ds(CONV_HALO, tc), :] = a_ref[...].astype(F32) * _sig(g_ref[...].astype(F32))
        acc = jnp.zeros((tc, CONV_WIDTH), F32) + b_ref[...]
        for k, win in _shifted(ubuf, sh, tc, [CONV_HALO - (CONV_K - 1) + k for k in range(CONV_K)]):
            acc = acc + w_ref[pl.ds(k, 1), :] * win
        cv_ref[...] = acc
        mu = jnp.mean(acc, axis=-1, keepdims=True)
        xc = acc - mu
        y = xc * lax.rsqrt(jnp.mean(xc * xc, axis=-1, keepdims=True) + EPS) * lg_ref[...] + lb_ref[...]
        sc_ref[...] = (y * _sig(y)).astype(sc_ref.dtype)

    row = lambda b, c: (b * nc + c, 0)
    full = lambda b, c: (0, 0)
    vec = pl.BlockSpec((1, CONV_WIDTH), full)
    return pl.pallas_call(
        body, grid=(nbatch, nc),
        in_specs=[pl.BlockSpec((tc, CONV_WIDTH), lambda b, c: (b * nc + c, 4)),
                  pl.BlockSpec((tc, CONV_WIDTH), lambda b, c: (b * nc + c, 5)),
                  pl.BlockSpec((CONV_HALO, CONV_WIDTH), full), vec, vec, vec],
        out_specs=[pl.BlockSpec((tc, CONV_WIDTH), row), pl.BlockSpec((tc, CONV_WIDTH), row)],
        out_shape=[jax.ShapeDtypeStruct((t, CONV_WIDTH), F32), jax.ShapeDtypeStruct((t, CONV_WIDTH), BF16)],
        scratch_shapes=[pltpu.VMEM((CONV_HALO + tc, CONV_WIDTH), F32)] * 2, name="conv_fwd",
        compiler_params=_params(("arbitrary", "arbitrary")))(z, z, w, bias, lg, lb)


def _conv_bwd(dsc, cv, z, w, lg, lb, nbatch, seq):
    t = z.shape[0]
    tc = _conv_chunk(seq)
    nc = seq // tc
    hb = tc // CONV_HALO

    def body(dsc_ref, cv_ref, a_ref, g_ref, ap_ref, gp_ref, w_ref, lg_ref, lb_ref,
             da_ref, dg_ref, dw_ref, db_ref, dlg_ref, dlb_ref, ubuf, dbuf, sh):
        b, c = pl.program_id(0), pl.program_id(1)
        ct = nc - 1 - c

        @pl.when((b == 0) & (c == 0))
        def _():
            dw_ref[...] = jnp.zeros_like(dw_ref)
            db_ref[...] = jnp.zeros_like(db_ref)
            dlg_ref[...] = jnp.zeros_like(dlg_ref)
            dlb_ref[...] = jnp.zeros_like(dlb_ref)

        cvv = cv_ref[...]
        mu = jnp.mean(cvv, axis=-1, keepdims=True)
        xc = cvv - mu
        rstd = lax.rsqrt(jnp.mean(xc * xc, axis=-1, keepdims=True) + EPS)
        xh = xc * rstd
        y = xh * lg_ref[...] + lb_ref[...]
        sy = _sig(y)
        dy = dsc_ref[...].astype(F32) * (sy * (1.0 + y * (1.0 - sy)))
        dlg_ref[...] += jnp.sum(dy * xh, axis=0, keepdims=True)
        dlb_ref[...] += jnp.sum(dy, axis=0, keepdims=True)
        dxh = dy * lg_ref[...]
        dcv = rstd * (dxh - jnp.mean(dxh, axis=-1, keepdims=True) - xh * jnp.mean(dxh * xh, axis=-1, keepdims=True))
        db_ref[...] += jnp.sum(dcv, axis=0, keepdims=True)

        @pl.when(c == 0)
        def _():
            dbuf[pl.ds(tc, CONV_HALO), :] = jnp.zeros((CONV_HALO, CONV_WIDTH), F32)

        @pl.when(c > 0)
        def _():
            dbuf[pl.ds(tc, CONV_HALO), :] = dbuf[pl.ds(0, CONV_HALO), :]

        dbuf[pl.ds(0, tc), :] = dcv
        a = a_ref[...].astype(F32)
        sg = _sig(g_ref[...].astype(F32))
        ubuf[pl.ds(0, CONV_HALO), :] = jnp.where(ct > 0, ap_ref[...].astype(F32) * _sig(gp_ref[...].astype(F32)), 0.0)
        ubuf[pl.ds(CONV_HALO, tc), :] = a * sg
        du = jnp.zeros((tc, CONV_WIDTH), F32)
        for k, win in _shifted(dbuf, sh, tc, [CONV_K - 1 - k for k in range(CONV_K)]):
            du = du + w_ref[pl.ds(k, 1), :] * win
        for k, win in _shifted(ubuf, sh, tc, [CONV_HALO - (CONV_K - 1) + k for k in range(CONV_K)]):
            dw_ref[pl.ds(k, 1), :] += jnp.sum(dcv * win, axis=0, keepdims=True)
        da_ref[...] = (du * sg).astype(da_ref.dtype)
        dg_ref[...] = (du * a * sg * (1.0 - sg)).astype(dg_ref.dtype)

    row = lambda b, c: (b * nc + (nc - 1 - c), 0)
    full = lambda b, c: (0, 0)
    vec = pl.BlockSpec((1, CONV_WIDTH), full)
    blk = pl.BlockSpec((tc, CONV_WIDTH), row)

    def zcol(col):
        return pl.BlockSpec((tc, CONV_WIDTH), lambda b, c: (b * nc + (nc - 1 - c), col))

    def zprev(col):
        return pl.BlockSpec((CONV_HALO, CONV_WIDTH),
                            lambda b, c: (jnp.maximum((b * nc + (nc - 1 - c)) * hb - 1, 0), col))

    o = jax.ShapeDtypeStruct((t, CONV_WIDTH), BF16)
    v = jax.ShapeDtypeStruct((1, CONV_WIDTH), F32)
    return pl.pallas_call(
        body, grid=(nbatch, nc),
        in_specs=[blk, blk, zcol(4), zcol(5), zprev(4), zprev(5), pl.BlockSpec((CONV_HALO, CONV_WIDTH), full), vec, vec],
        out_specs=[blk, blk, pl.BlockSpec((CONV_HALO, CONV_WIDTH), full), vec, vec, vec],
        out_shape=[o, o, jax.ShapeDtypeStruct((CONV_HALO, CONV_WIDTH), F32), v, v, v],
        scratch_shapes=[pltpu.VMEM((CONV_HALO + tc, CONV_WIDTH), F32)] * 3, name="conv_bwd", compiler_params=_params(("arbitrary", "arbitrary")))(dsc, cv, z, z, z, z, w, lg, lb)


BIG = ("w_in", "w_attn_out", "w_ssm_glu", "w_conv_out", "w_mix_out", "w_ffn_in", "w_ffn_out", "w_ple_in", "w_ple_gate")
BIG_AXIS = {"w_in": 2, "w_attn_out": 2, "w_ssm_glu": 2, "w_conv_out": 2, "w_mix_out": 1, "w_ffn_in": 2,
            "w_ffn_out": 1, "w_ple_in": 2, "w_ple_gate": 1}
SHARD_MAJOR = ("w_in", "w_ffn_in")
SMALL = ("mix_norm_g", "b_gate", "attn_sinks", "ssm_lambda_re", "ssm_lambda_im", "ssm_log_dt", "ssm_b_re", "ssm_b_im",
         "ssm_c_re", "ssm_c_im", "ssm_d", "b_ssm_glu", "conv_dw_w", "conv_dw_b", "conv_norm_g", "conv_norm_b",
         "ffn_norm_g", "ple_norm_g", "final_norm_g")
WEIGHTS = ("mix_norm_g", "w_in", "b_gate", "attn_sinks", "w_attn_out", "ssm_lambda_re", "ssm_lambda_im", "ssm_log_dt",
           "ssm_b_re", "ssm_b_im", "ssm_c_re", "ssm_c_im", "ssm_d", "w_ssm_glu", "b_ssm_glu", "conv_dw_w", "conv_dw_b",
           "conv_norm_g", "conv_norm_b", "w_conv_out", "w_mix_out", "ffn_norm_g", "w_ffn_in", "w_ffn_out", "w_ple_in",
           "ple_norm_g", "w_ple_gate", "final_norm_g")
SSM_NAMES = ("ssm_lambda_re", "ssm_lambda_im", "ssm_log_dt", "ssm_b_re", "ssm_b_im", "ssm_c_re", "ssm_c_im")


def _ple_block(x, p_l, g, w_in, w_gate):
    t, d = x.shape
    kp = p_l.shape[1]
    tm = min(512, t)

    def body(x_ref, p_ref, g_ref, wi_ref, wg_ref, o_ref, gp_ref, h_ref, e_ref):
        xv = x_ref[...]
        e = lax.dot_general(p_ref[...].astype(BF16), wi_ref[...], _DIMS["nn"], preferred_element_type=F32).astype(BF16)
        h = _rms_fwd(xv, g_ref[...]).astype(BF16)
        gp = lax.dot_general(h, wg_ref[...], _DIMS["nn"], preferred_element_type=F32).astype(BF16)
        o_ref[...] = _ple_fwd(xv, gp, e)
        gp_ref[...], h_ref[...], e_ref[...] = gp, h, e

    row = lambda width: pl.BlockSpec((tm, width), lambda i: (i, 0))
    full = lambda v: pl.BlockSpec(v.shape, lambda i: (0, 0))
    out = lambda dt: jax.ShapeDtypeStruct((t, d), dt)
    return pl.pallas_call(
        body, grid=(t // tm,), in_specs=[row(d), row(kp), full(g), full(w_in), full(w_gate)],
        out_specs=[row(d)] * 4, out_shape=[out(F32), out(BF16), out(BF16), out(BF16)], name="ple_block",
        compiler_params=_params(("parallel",)))(x, p_l, g, w_in, w_gate)


def _ple_block_bwd(dx3, gp, e, x, g, w_gate):
    t, d = x.shape
    tm = min(512, t)

    def body(dx3_ref, gp_ref, e_ref, x_ref, g_ref, wg_ref, de_ref, dgp_ref, dx_ref, dg_ref):
        @pl.when(pl.program_id(0) == 0)
        def _():
            dg_ref[...] = jnp.zeros_like(dg_ref)

        dx3 = dx3_ref[...]
        de, dgp = _ple_bwd(dx3, gp_ref[...], e_ref[...])
        dgp = dgp.astype(BF16)
        de_ref[...] = de.astype(de_ref.dtype)
        dgp_ref[...] = dgp
        dh = lax.dot_general(dgp, wg_ref[...], _DIMS["nt"], preferred_element_type=F32).astype(BF16)
        dx, dg = _rms_bwd(dh, x_ref[...], dx3, g_ref[...])
        dx_ref[...] = dx
        dg_ref[...] += dg

    row = pl.BlockSpec((tm, d), lambda i: (i, 0))
    full = lambda v: pl.BlockSpec(v.shape, lambda i: (0, 0))
    out = lambda dt: jax.ShapeDtypeStruct((t, d), dt)
    return pl.pallas_call(
        body, grid=(t // tm,), in_specs=[row, row, row, row, full(g), full(w_gate)],
        out_specs=[row, row, row, pl.BlockSpec((1, d), lambda i: (0, 0))],
        out_shape=[out(BF16), out(BF16), out(F32), jax.ShapeDtypeStruct((1, d), F32)], name="ple_block_bwd",
        compiler_params=_params(("arbitrary",)))(dx3, gp, e, x, g, w_gate)


def _in_proj_bwd(dz, w_in4, x, dres, g):
    t, d = x.shape
    nsh, _, cc = w_in4.shape
    tm = min(512, t)

    def body(dz_ref, w_ref, x_ref, dres_ref, g_ref, dx_ref, dg_ref):
        @pl.when(pl.program_id(0) == 0)
        def _():
            dg_ref[...] = jnp.zeros_like(dg_ref)

        dh = jnp.zeros((tm, d), F32)
        for sh in range(nsh):
            dh = dh + lax.dot_general(dz_ref[:, sh * cc:(sh + 1) * cc], w_ref[sh], _DIMS["nt"],
                                      preferred_element_type=F32)
        dx, dg = _rms_bwd(dh.astype(BF16), x_ref[...], dres_ref[...], g_ref[...])
        dx_ref[...] = dx
        dg_ref[...] += dg

    row = lambda width: pl.BlockSpec((tm, width), lambda i: (i, 0))
    return pl.pallas_call(
        body, grid=(t // tm,),
        in_specs=[row(nsh * cc), pl.BlockSpec(w_in4.shape, lambda i: (0, 0, 0)), row(d), row(d),
                  pl.BlockSpec(g.shape, lambda i: (0, 0))],
        out_specs=[row(d), pl.BlockSpec((1, d), lambda i: (0, 0))],
        out_shape=[jax.ShapeDtypeStruct((t, d), F32), jax.ShapeDtypeStruct((1, d), F32)], name="in_proj_bwd",
        compiler_params=_params(("arbitrary",)))(dz, w_in4, x, dres, g)


def _heads(v, nh):
    return v.reshape(v.shape[0], nh, HEAD_DIM).transpose(1, 0, 2)


def _tokens(v):
    return v.transpose(1, 0, 2).reshape(v.shape[1], v.shape[0] * HEAD_DIM)


def _row(v):
    return v.reshape(1, -1)


def _layer_fwd(x, p_l, w, s, rope, nbatch, seq, next_shards=None):
    t = x.shape[0]
    tm = 512
    d = D_MODEL
    sv = {}
    sv["x"] = x
    h = _rowwise("rms_mix", _rms_fwd, [R(x), V(_row(s["mix_norm_g"]))], [O(d, BF16)], tm=tm)
    cs = {nm: w[nm].shape[2] for nm in SHARD_MAJOR}
    tb = 1024
    got = {}
    plan = None if next_shards is None else _gather_plan(next_shards, GATHER_A)
    z = _mm("mm_in", h, w["w_in"], "nn", BF16, m=t, n=N_CHIPS * cs["w_in"], k=d, tm=tb, tn=cs["w_in"], tk=d,
            b_sh=cs["w_in"], comm=plan)
    if plan is not None:
        z, outs = z
        got.update(zip(plan["names"], outs))
    sv["h"], sv["z"] = h, z
    c, sa, sb = rope
    qkv_w = Q_WIDTH + 2 * KV_WIDTH
    qkv = _rowwise("rope_fwd", _rope_fwd, [R(z, Q_WIDTH, 0), R(z, KV_WIDTH, 4), R(z, KV_WIDTH, 5), R(c), R(sa), R(sb)],
                   [O(qkv_w, BF16)], tm=tm)
    qkv = _heads(qkv, qkv_w // HEAD_DIM)
    sinks = s["attn_sinks"].reshape(N_Q_HEADS, 1, 1)
    oh, lse = _attn_fwd(qkv, sinks, nbatch, seq)
    o = _tokens(oh)
    ya = _mm("mm_attn_out", o, w["w_attn_out"], "nn", BF16, m=t, n=d, k=Q_WIDTH, tm=tb, tn=d, tk=Q_WIDTH)
    sv.update(qkv=qkv, oh=oh, lse=lse, o=o, ya=ya, sinks=sinks)
    ssm_args = [s[nm] for nm in SSM_NAMES]
    a_re, a_im, wb, wc = _ssm_prep(*ssm_args)
    dskip = _row(s["ssm_d"])
    st, ys, gel = _ssm_fwd(z, wb.astype(BF16), wc.astype(BF16), _ssm_tables(a_re, a_im, False), dskip, nbatch, seq)
    glu = _mm("mm_glu", gel, w["w_ssm_glu"], "nn", BF16, m=t, n=2 * d, k=SSM_WIDTH, tm=tb, tn=2 * d, tk=SSM_WIDTH,
              bias=_row(s["b_ssm_glu"]))
    sv.update(st=st, ys=ys, gel=gel, glu=glu, a=(a_re, a_im), wb=wb, wc=wc, dskip=dskip)
    cw = jnp.pad(s["conv_dw_w"], ((0, CONV_HALO - CONV_K), (0, 0)))
    cv, sc = _conv_fwd(z, cw, _row(s["conv_dw_b"]), _row(s["conv_norm_g"]), _row(s["conv_norm_b"]), nbatch, seq)
    yc = _mm("mm_conv_out", sc, w["w_conv_out"], "nn", BF16, m=t, n=d, k=CONV_WIDTH, tm=tb, tn=d, tk=CONV_WIDTH)
    sv.update(cw=cw, cv=cv, sc=sc, yc=yc)
    bg = _row(s["b_gate"])
    merge_ins = [R(z, 512, 3), R(z, 512, 5), R(z, 512, 7), V(bg, 512, 0), V(bg, 512, 2), V(bg, 512, 4),
                 R(ya, 512, 0), R(glu, 512, 0), R(glu, 512, 2), R(yc, 512, 0)]
    merged = _rowwise("merge_fwd", _merge_fwd, merge_ins, [O(512, BF16, total=d)], tm=tm, ncol=2)
    x1 = _mm("mm_mix", merged, w["w_mix_out"], "nn", F32, m=t, n=d, k=d, tm=tb, tn=d, tk=d, res=x)
    sv.update(merged=merged, x1=x1)
    hf = _rowwise("rms_ffn", _rms_fwd, [R(x1), V(_row(s["ffn_norm_g"]))], [O(d, BF16)], tm=tm)
    plan = None if next_shards is None else _gather_plan(next_shards, GATHER_B)
    f = _mm("mm_ffn_in", hf, w["w_ffn_in"], "nn", BF16, m=t, n=2 * FFN_HIDDEN, k=d, tm=tb, tn=cs["w_ffn_in"], tk=d,
            b_sh=cs["w_ffn_in"], comm=plan)
    if plan is not None:
        f, outs = f
        got.update(zip(plan["names"], outs))
    act = (lambda i, j, kk, fg, fu: _ffn_act(fg, fu), [(f, lambda i, j, kk: (i, 0)), (f, lambda i, j, kk: (i, 1))])
    x2, act = _mm("mm_ffn_out", act, w["w_ffn_out"], "nn", F32, m=t, n=d, k=FFN_HIDDEN, tm=256, tn=d, tk=FFN_HIDDEN,
                  res=x1, a_keep=True)
    sv.update(hf=hf, f=f, act=act, x2=x2)
    x3, gp, hp, e = _ple_block(x2, p_l, _row(s["ple_norm_g"]), w["w_ple_in"], w["w_ple_gate"])
    sv.update(e=e, hp=hp, gp=gp, p=p_l)
    return x3, sv, got


def _layer_bwd(dx3, sv, w, s, rope, nbatch, seq):
    t = dx3.shape[0]
    tm = 512
    d = D_MODEL
    gb, gs = {}, {}
    cs = {nm: w[nm].shape[2] for nm in SHARD_MAJOR}
    tb = 1024

    def wg(name, a, b, m, n, tm=1024, tk=1024, shard=None):
        return _mm(name, a, b, "tn", BF16, m=m, n=n, k=t, tm=tm, tn=n if shard is None else cs[shard], tk=tk,
                   o_sh=None if shard is None else cs[shard])

    de, dgp, dx2, gs["ple_norm_g"] = _ple_block_bwd(dx3, sv["gp"], sv["e"], sv["x2"], _row(s["ple_norm_g"]),
                                                    w["w_ple_gate"])
    gb["w_ple_in"] = wg("wg_ple_in", sv["p"], de, sv["p"].shape[1], d, tk=2048)
    gb["w_ple_gate"] = wg("wg_ple_gate", sv["hp"], dgp, d, d, tk=2048)
    fw = FFN_HIDDEN // 2
    dact = _mm("mmb_ffn_out", dx2, w["w_ffn_out"], "nt", BF16, m=t, n=FFN_HIDDEN, k=d, tm=tb, tn=fw, tk=d)
    gb["w_ffn_out"] = wg("wg_ffn_out", sv["act"], dx2, FFN_HIDDEN, d, tm=fw)
    f = sv["f"]

    def df_tile(is_gate, da, fg, fu):
        dfg, dfu = _ffn_act_bwd(da, fg, fu)
        return jnp.where(is_gate, dfg, dfu)

    assert cs["w_ffn_in"] == fw
    df_rows = (lambda i, j, kk, *v: df_tile(kk < 2, *v),
               [(dact, lambda i, j, kk: (i, kk % 2)), (f, lambda i, j, kk: (i, kk % 2)), (f, lambda i, j, kk: (i, 2 + kk % 2))])
    dhf, df = _mm("mmb_ffn_in", df_rows, w["w_ffn_in"], "nt", BF16, m=t, n=d, k=2 * FFN_HIDDEN, tm=512, tn=d, tk=fw,
                  b_sh=fw, a_keep=True)
    gb["w_ffn_in"] = wg("wg_ffn_in", sv["hf"], df, d, 2 * FFN_HIDDEN, shard="w_ffn_in")
    dx1, gs["ffn_norm_g"] = _rowwise("rms_ffn_bwd", _rms_bwd, [R(dhf), R(sv["x1"]), R(dx2), V(_row(s["ffn_norm_g"]))],
                                     [O(d, F32)], [A(d)], tm=tm)
    dm = _mm("mmb_mix", dx1, w["w_mix_out"], "nt", BF16, m=t, n=d, k=d, tm=tb, tn=d, tk=d)
    gb["w_mix_out"] = wg("wg_mix", sv["merged"], dx1, d, d)
    z, glu, bg = sv["z"], sv["glu"], _row(s["b_gate"])
    ins = [R(dm, 512, 0), R(z, 512, 3), R(z, 512, 5), R(z, 512, 7), V(bg, 512, 0), V(bg, 512, 2), V(bg, 512, 4),
           R(sv["ya"], 512, 0), R(glu, 512, 0), R(glu, 512, 2), R(sv["yc"], 512, 0)]
    ob = lambda: O(512, BF16, total=d)
    ab = lambda: A(512, total=d)
    dya, dga, dgb, dyc, d0, d1, d2, db0, db1, db2, dba, dbb = _rowwise(
        "merge_bwd", _merge_bwd, ins, [ob() for _ in range(7)], [ab() for _ in range(5)], tm=tm, ncol=2)
    gs["b_gate"] = jnp.concatenate([db0, db1, db2], axis=1)
    gs["b_ssm_glu"] = jnp.concatenate([dba, dbb], axis=1)
    dglu = jnp.concatenate([dga, dgb], axis=1)
    gb["w_attn_out"] = wg("wg_attn_out", sv["o"], dya, Q_WIDTH, d, tk=2048)
    do = _mm("mmb_attn_out", dya, w["w_attn_out"], "nt", BF16, m=t, n=Q_WIDTH, k=d, tm=tb, tn=Q_WIDTH, tk=d)
    dqkv, dsink = _attn_bwd(sv["qkv"], sv["oh"], _heads(do, N_Q_HEADS), sv["lse"], sv["sinks"], nbatch, seq)
    dqkv = _tokens(dqkv)
    gs["attn_sinks"] = dsink.reshape(-1)
    c, sa, sb = rope
    dq = _rowwise("rope_bwd_q", _rope_bwd_q, [R(dqkv, Q_WIDTH, 0), R(c), R(sa), R(sb)], [O(Q_WIDTH, BF16)], tm=tm)
    dk, dv = _kv_combine(dqkv, c, sa, sb, seq)
    gb["w_ssm_glu"] = wg("wg_ssm_glu", sv["gel"], dglu, SSM_WIDTH, 2 * d, tk=2048)
    dgi = _mm("mmb_glu", dglu, w["w_ssm_glu"], "nt", BF16, m=t, n=SSM_WIDTH, k=2 * d, tm=tb, tn=SSM_WIDTH, tk=2 * d)
    a_re, a_im = sv["a"]
    du, dwb, dwc, dd, da = _ssm_bwd(dgi, sv["ys"], sv["st"], z, sv["wb"].T.astype(BF16), sv["wc"].T.astype(BF16),
                                    _ssm_tables(a_re, a_im, True), sv["dskip"], nbatch, seq)
    gs["ssm_d"] = dd.reshape(-1)
    da = jnp.sum(da, axis=0)
    _, prep_vjp = jax.vjp(_ssm_prep, *[s[nm] for nm in SSM_NAMES])
    for nm, g in zip(SSM_NAMES, prep_vjp((da[:SSM_LANES], da[SSM_LANES:], dwb, dwc))):
        gs[nm] = g
    gb["w_conv_out"] = wg("wg_conv_out", sv["sc"], dyc, CONV_WIDTH, d, tk=2048)
    dsc = _mm("mmb_conv_out", dyc, w["w_conv_out"], "nt", BF16, m=t, n=CONV_WIDTH, k=d, tm=tb, tn=CONV_WIDTH, tk=d)
    dca, dcg, dcw, dcb, dlg, dlb = _conv_bwd(dsc, sv["cv"], z, sv["cw"], _row(s["conv_norm_g"]),
                                             _row(s["conv_norm_b"]), nbatch, seq)
    gs["conv_dw_w"] = dcw[:CONV_K]
    gs["conv_dw_b"], gs["conv_norm_g"], gs["conv_norm_b"] = dcb.reshape(-1), dlg.reshape(-1), dlb.reshape(-1)
    dz = jnp.concatenate([dq, dk, dv, du, dca, dcg, d0, d1, d2], axis=1)
    gb["w_in"] = wg("wg_in", sv["h"], dz, d, dz.shape[1], tk=2048, shard="w_in")
    dx, gs["mix_norm_g"] = _in_proj_bwd(dz, w["w_in"], sv["x"], dx1, _row(s["mix_norm_g"]))
    gs["mix_norm_g"], gs["ffn_norm_g"], gs["ple_norm_g"] = (gs[nm].reshape(-1) for nm in
                                                            ("mix_norm_g", "ffn_norm_g", "ple_norm_g"))
    gs["b_gate"], gs["b_ssm_glu"] = gs["b_gate"].reshape(-1), gs["b_ssm_glu"].reshape(-1)
    return dx, {nm: _shard_major(nm, g) for nm, g in gb.items()}, gs


def _rope_tables(positions):
    inv_freq = ROPE_THETA ** (-jnp.arange(0, ROPE_DIM, 2, dtype=F32) / ROPE_DIM)
    ang = positions.reshape(-1).astype(F32)[:, None] * inv_freq
    cos, sin = jnp.cos(ang), jnp.sin(ang)
    t = ang.shape[0]
    rest = HEAD_DIM - ROPE_DIM
    c = jnp.concatenate([cos, cos, jnp.ones((t, rest), F32)], axis=1)
    sa = jnp.concatenate([-sin, jnp.zeros((t, HEAD_DIM - ROPE_HALF), F32)], axis=1)
    sb = jnp.concatenate([jnp.zeros((t, ROPE_HALF), F32), sin, jnp.zeros((t, rest), F32)], axis=1)
    two = lambda v: jnp.concatenate([v, v], axis=1)
    return two(c), two(sa), two(sb)


def _natural(nm, w4):
    if nm in SHARD_MAJOR:
        return w4
    if BIG_AXIS[nm] == 1:
        return w4.reshape(-1, w4.shape[2])
    return w4.transpose(1, 0, 2).reshape(w4.shape[1], -1)


def _shard_major(nm, g):
    if nm in SHARD_MAJOR:
        return g
    if BIG_AXIS[nm] == 1:
        return g.reshape(N_CHIPS, -1, g.shape[1])
    return g.reshape(g.shape[0], N_CHIPS, -1).transpose(1, 0, 2)


def _untap(taps4, cols):
    flat = taps4.reshape(N_CHIPS, -1)[:, :CONV_K * cols]
    return flat.reshape(N_CHIPS, CONV_K, cols).transpose(1, 0, 2).reshape(CONV_K, N_CHIPS * cols)


def _local_step(x, p, positions, loss_target, small, wfull=None, shards=None):
    nbatch, seq, d = x.shape
    depth = p.shape[0]
    t = nbatch * seq
    rope = _rope_tables(positions)
    xs = x.reshape(t, d)
    saved, ws, ss = [], [], []
    got = None if shards is None else _gather_now((shards, 0))
    for l in range(depth):
        w4 = {nm: wfull[nm][l] for nm in BIG} if shards is None else got
        w_l = {nm: _natural(nm, w4[nm]) for nm in BIG}
        s_l = {nm: small[nm][l] for nm in small if nm != "final_norm_g"}
        if shards is not None:
            s_l["conv_dw_w"] = _untap(got[TAPS], CONV_WIDTH // N_CHIPS)
        nxt = (shards, l + 1) if shards is not None and l + 1 < depth else None
        xs, sv, got = _layer_fwd(xs, p[l].reshape(t, -1), w_l, s_l, rope, nbatch, seq, nxt)
        saved.append(sv)
        ws.append(w_l)
        ss.append(s_l)
    dx, loss_cols, dgf = _rowwise("loss_head", _loss_fn, [R(xs), R(loss_target.reshape(t, d)),
                                                          V(_row(small["final_norm_g"]))],
                                  [O(d, F32)], [A(d), A(d)], tm=512)
    gbs, gss = [None] * depth, [None] * depth
    for l in reversed(range(depth)):
        dx, gbs[l], gss[l] = _layer_bwd(dx, saved[l], ws[l], ss[l], rope, nbatch, seq)
    gbig = {nm: jnp.stack([g[nm] for g in gbs]) for nm in BIG}
    gsmall = {nm: jnp.stack([g[nm] for g in gss]) for nm in SMALL if nm != "final_norm_g"}
    gsmall["final_norm_g"] = dgf.reshape(-1)
    return loss_cols, dx.reshape(nbatch, seq, d), gbig, gsmall


HBM = pl.BlockSpec(memory_space=pltpu.HBM)


def _place():
    x, y, c = lax.axis_index("x"), lax.axis_index("y"), lax.axis_index("c")
    chips = [(1 - x, y), (x, 1 - y), (1 - x, 1 - y)]
    return x, y, c, chips


def _remote(src, dst, send_sem, recv_sem, to):
    return pltpu.make_async_remote_copy(src_ref=src, dst_ref=dst, send_sem=send_sem, recv_sem=recv_sem,
                                        device_id=to, device_id_type=MESH)


TAPS = "taps"
GATHER_ALL = (("w_ffn_in", "w_ffn_out"),
              ("w_in", "w_ple_gate", "w_mix_out", "w_attn_out", "w_ssm_glu", "w_conv_out", "w_ple_in", TAPS))
GATHER_A = (("w_ffn_in",), ("w_in", "w_ple_gate"))
GATHER_B = (("w_ffn_out",), ("w_mix_out", "w_attn_out", "w_ssm_glu", "w_conv_out", "w_ple_in", TAPS))


def _gather_plan(shards, sets):
    stacked, layer = shards
    names = sets[0] + sets[1]
    n = len(names)
    idx = {nm: i for i, nm in enumerate(names)}

    def start(ins, outs, sems):
        send1, recv1, _, _, send0, recv0 = sems
        x, y, c, chips = _place()
        me = 2 * x + y
        for i in range(n):
            _remote(ins[i].at[layer], outs[i].at[me], send0.at[i], recv0.at[i], (x, y, 1 - c)).start()
        for role in (0, 1):
            @pl.when(c == role)
            def _():
                for nm in sets[role]:
                    i = idx[nm]
                    for k, (cx, cy) in enumerate(chips):
                        _remote(ins[i].at[layer], outs[i].at[me], send1.at[i, k], recv1.at[i, k], (cx, cy, c)).start()

    def finish(ins, outs, sems):
        send1, recv1, send2, recv2, send0, recv0 = sems
        x, y, c, chips = _place()
        me = 2 * x + y
        sib = (x, y, 1 - c)
        for role in (0, 1):
            @pl.when(c == role)
            def _():
                passed = []
                for nm in sets[role]:
                    i = idx[nm]
                    for k, (cx, cy) in enumerate(chips):
                        slot = outs[i].at[2 * cx + cy]
                        _remote(slot, slot, send1.at[i, k], recv1.at[i, k], (cx, cy, c)).wait_recv()
                        cp = _remote(slot, slot, send2.at[i, k], recv2.at[i, k], sib)
                        cp.start()
                        passed.append(cp)
                for nm in sets[1 - role]:
                    i = idx[nm]
                    for k, (cx, cy) in enumerate(chips):
                        slot = outs[i].at[2 * cx + cy]
                        _remote(slot, slot, send2.at[i, k], recv2.at[i, k], sib).wait_recv()
                for nm in sets[role]:
                    i = idx[nm]
                    for k, (cx, cy) in enumerate(chips):
                        _remote(ins[i].at[layer], outs[i].at[me], send1.at[i, k], recv1.at[i, k],
                                (cx, cy, c)).wait_send()
                for cp in passed:
                    cp.wait_send()
        for i in range(n):
            _remote(ins[i].at[layer], outs[i].at[me], send0.at[i], recv0.at[i], sib).wait()

    ins = [stacked[nm] for nm in names]
    return dict(names=names, ins=ins, start=start, finish=finish,
                out_shapes=[jax.ShapeDtypeStruct((N_CHIPS,) + v.shape[1:], v.dtype) for v in ins],
                sems=[pltpu.SemaphoreType.DMA((n, 3)) for _ in range(4)] + [pltpu.SemaphoreType.DMA((n,))
                                                                            for _ in range(2)])


def _gather_now(shards):
    return _comm_now("gather_weights", _gather_plan(shards, GATHER_ALL))


def _pair_exchange(grads):
    n = len(grads)
    hl = grads[0].shape[0] // 2

    def body(*refs):
        ins, outs = refs[:n], refs[n:2 * n]
        send, recv = refs[2 * n:]
        x, y, c, _ = _place()
        other = pl.ds((1 - c) * hl, hl)
        cps = [_remote(ins[i].at[other], outs[i], send.at[i], recv.at[i], (x, y, 1 - c)) for i in range(n)]
        for cp in cps:
            cp.start()
        for cp in cps:
            cp.wait()

    out_shape = [jax.ShapeDtypeStruct((hl,) + g.shape[1:], g.dtype) for g in grads]
    sems = [pltpu.SemaphoreType.DMA((n,)) for _ in range(2)]
    return pl.pallas_call(body, out_shape=out_shape, in_specs=[HBM] * n, out_specs=[HBM] * n, scratch_shapes=sems,
                          name="reduce_pair_exchange")(*grads)


def _pair_add(g, r):
    hl, _, rr, cc = r.shape
    rows = hl * N_CHIPS * rr
    nblk = rows // rr

    def body(c_ref, g_ref, r_ref, o_ref):
        o_ref[...] = (g_ref[...].astype(F32) + r_ref[...].astype(F32)).astype(o_ref.dtype)

    grid_spec = pltpu.PrefetchScalarGridSpec(
        num_scalar_prefetch=1, grid=(nblk,),
        in_specs=[pl.BlockSpec((rr, cc), lambda i, c_ref: (c_ref[0] * nblk + i, 0)),
                  pl.BlockSpec((rr, cc), lambda i, c_ref: (i, 0))],
        out_specs=pl.BlockSpec((rr, cc), lambda i, c_ref: (i, 0)))
    c = lax.axis_index("c").astype(jnp.int32).reshape(1)
    out = pl.pallas_call(body, out_shape=jax.ShapeDtypeStruct((rows, cc), r.dtype), grid_spec=grid_spec,
                         name="reduce_pair_add", compiler_params=_params(("parallel",)))(
        c, g.reshape(-1, cc), r.reshape(rows, cc))
    return out.reshape(r.shape)


def _chip_exchange(psums):
    n = len(psums)

    def body(*refs):
        ins, got = refs[:n], refs[n:2 * n]
        send, recv = refs[2 * n:]
        x, y, c, chips = _place()
        cps = [_remote(ins[i].at[:, 2 * cx + cy], got[i].at[k], send.at[i, k], recv.at[i, k], (cx, cy, c))
               for i in range(n) for k, (cx, cy) in enumerate(chips)]
        for cp in cps:
            cp.start()
        for cp in cps:
            cp.wait()

    got_shape = [jax.ShapeDtypeStruct((3, p.shape[0]) + p.shape[2:], p.dtype) for p in psums]
    sems = [pltpu.SemaphoreType.DMA((n, 3)), pltpu.SemaphoreType.DMA((n, 3))]
    return pl.pallas_call(body, out_shape=got_shape, in_specs=[HBM] * n, out_specs=[HBM] * n, scratch_shapes=sems,
                          name="reduce_chip_exchange")(*psums)


def _comm_now(name, plan):
    n = len(plan["ins"])

    def body(*refs):
        ins, outs, sems = refs[:n], refs[n:2 * n], refs[2 * n:]
        plan["start"](ins, outs, sems)
        plan["finish"](ins, outs, sems)

    outs = pl.pallas_call(body, out_shape=plan["out_shapes"], in_specs=[HBM] * n, out_specs=[HBM] * n,
                          scratch_shapes=plan["sems"], name=name)(*plan["ins"])
    return dict(zip(plan["names"], outs))


def _sum4(psum, got):
    hl, _, rr, cc = psum.shape
    tr = rr if rr * cc <= 512 * 1024 else rr // 2

    def body(place_ref, own_ref, g0_ref, g1_ref, g2_ref, o_ref):
        tot = (own_ref[...].astype(F32) + g0_ref[...].astype(F32)) + g1_ref[...].astype(F32)
        o_ref[...] = tot + g2_ref[...].astype(F32)

    def got_spec(k):
        return pl.BlockSpec((None, None, tr, cc), lambda h, i, place: (k, h, i, 0))

    grid_spec = pltpu.PrefetchScalarGridSpec(
        num_scalar_prefetch=1, grid=(hl, rr // tr),
        in_specs=[pl.BlockSpec((None, None, tr, cc), lambda h, i, place: (h, place[0], i, 0)),
                  got_spec(0), got_spec(1), got_spec(2)],
        out_specs=pl.BlockSpec((None, tr, cc), lambda h, i, place: (place[1] * hl + h, i, 0)))
    place = jnp.stack([2 * lax.axis_index("x") + lax.axis_index("y"), lax.axis_index("c")]).astype(jnp.int32)
    return pl.pallas_call(body, out_shape=jax.ShapeDtypeStruct((2 * hl, rr, cc), F32), grid_spec=grid_spec,
                          name="reduce_sum4", compiler_params=_params(("parallel", "parallel")))(
        place, psum, got, got, got)


def _pair_gather(sums):
    n = len(sums)
    hl = sums[0].shape[0] // 2

    def body(*refs):
        bufs = refs[n:2 * n]
        send, recv = refs[2 * n:]
        x, y, c, _ = _place()
        mine = pl.ds(c * hl, hl)
        cps = [_remote(bufs[i].at[mine], bufs[i].at[mine], send.at[i], recv.at[i], (x, y, 1 - c)) for i in range(n)]
        for cp in cps:
            cp.start()
        for cp in cps:
            cp.wait()

    out_shape = [jax.ShapeDtypeStruct(v.shape, v.dtype) for v in sums]
    sems = [pltpu.SemaphoreType.DMA((n,)) for _ in range(2)]
    return pl.pallas_call(body, out_shape=out_shape, in_specs=[HBM] * n, out_specs=[HBM] * n, scratch_shapes=sems,
                          input_output_aliases={i: i for i in range(n)}, name="reduce_pair_gather")(*sums)


def _allreduce_small(vec):
    rows = vec.shape[0]

    def body(v_ref, o_ref, all_ref, send, recv):
        x, y, c, _ = _place()
        me = 4 * x + 2 * y + c
        all_ref[me] = v_ref[...]
        cps = []
        for dlt in range(1, N_DEV):
            fx, fy, fc = (dlt >> 2) & 1, (dlt >> 1) & 1, dlt & 1
            to = (1 - x if fx else x, 1 - y if fy else y, 1 - c if fc else c)
            cps.append(_remote(v_ref, all_ref.at[me], send.at[dlt - 1], recv.at[dlt - 1], to))
        for cp in cps:
            cp.start()
        for cp in cps:
            cp.wait()
        tot = all_ref[0]
        for dev in range(1, N_DEV):
            tot = tot + all_ref[dev]
        o_ref[...] = tot

    vm = pl.BlockSpec(memory_space=pltpu.VMEM)
    return pl.pallas_call(
        body, out_shape=jax.ShapeDtypeStruct(vec.shape, F32), in_specs=[vm], out_specs=vm,
        scratch_shapes=[pltpu.VMEM((N_DEV, rows, 128), F32), pltpu.SemaphoreType.DMA((N_DEV - 1,)),
                        pltpu.SemaphoreType.DMA((N_DEV - 1,))],
        name="allreduce_small", compiler_params=pltpu.CompilerParams(vmem_limit_bytes=VMEM_LIMIT))(vec)


def _adamw(name, w, g, m, v):
    rows, cc = w.shape
    tm = rows if rows * cc <= 512 * 1024 else math.gcd(rows, 256)
    return _rowwise(name, _adamw_fn, [R(w), R(g), R(m), R(v)], [O(cc, F32), O(cc, F32), O(cc, F32)], tm=tm)


def _pack(parts):
    flat = jnp.concatenate([v.reshape(-1).astype(F32) for v in parts])
    pad = (-flat.shape[0]) % (SUBLANES * 128)
    return jnp.pad(flat, (0, pad)).reshape(-1, 128)


def _unpack(packed, shapes):
    flat, out, pos = packed.reshape(-1), [], 0
    for shp in shapes:
        size = math.prod(shp)
        out.append(flat[pos:pos + size].reshape(shp))
        pos += size
    return out


def kernel(x, p, positions, mix_norm_g, w_in, b_gate, attn_sinks, w_attn_out, ssm_lambda_re, ssm_lambda_im, ssm_log_dt, ssm_b_re, ssm_b_im, ssm_c_re, ssm_c_im, ssm_d, w_ssm_glu, b_ssm_glu, conv_dw_w, conv_dw_b, conv_norm_g, conv_norm_b, w_conv_out, w_mix_out, ffn_norm_g, w_ffn_in, w_ffn_out, w_ple_in, ple_norm_g, w_ple_gate, final_norm_g, loss_target, m_mix_norm_g, m_w_in, m_b_gate, m_attn_sinks, m_w_attn_out, m_ssm_lambda_re, m_ssm_lambda_im, m_ssm_log_dt, m_ssm_b_re, m_ssm_b_im, m_ssm_c_re, m_ssm_c_im, m_ssm_d, m_w_ssm_glu, m_b_ssm_glu, m_conv_dw_w, m_conv_dw_b, m_conv_norm_g, m_conv_norm_b, m_w_conv_out, m_w_mix_out, m_ffn_norm_g, m_w_ffn_in, m_w_ffn_out, m_w_ple_in, m_ple_norm_g, m_w_ple_gate, m_final_norm_g, v_mix_norm_g, v_w_in, v_b_gate, v_attn_sinks, v_w_attn_out, v_ssm_lambda_re, v_ssm_lambda_im, v_ssm_log_dt, v_ssm_b_re, v_ssm_b_im, v_ssm_c_re, v_ssm_c_im, v_ssm_d, v_w_ssm_glu, v_b_ssm_glu, v_conv_dw_w, v_conv_dw_b, v_conv_norm_g, v_conv_norm_b, v_w_conv_out, v_w_mix_out, v_ffn_norm_g, v_w_ffn_in, v_w_ffn_out, v_w_ple_in, v_ple_norm_g, v_w_ple_gate, v_final_norm_g):
    given = dict(locals())
    wts = {nm: given[nm] for nm in WEIGHTS}
    mom = {nm: given["m_" + nm] for nm in WEIGHTS}
    var = {nm: given["v_" + nm] for nm in WEIGHTS}
    depth = p.shape[0]
    chip = 2 * lax.axis_index("x") + lax.axis_index("y")

    cw_cols = conv_dw_w.shape[2]
    taps = jnp.pad(conv_dw_w.reshape(depth, -1), ((0, 0), (0, (-CONV_K * cw_cols) % (SUBLANES * 128))))
    shards = {**{nm: wts[nm].astype(BF16) for nm in BIG}, TAPS: taps.reshape(depth, -1, 128)}
    small = {nm: wts[nm] for nm in SMALL if nm != "conv_dw_w"}

    loss_cols, grad_x, gbig, gsmall = _local_step(x, p, positions, loss_target, small, shards=shards)

    parts = [loss_cols] + [gsmall[nm] for nm in SMALL]
    total = _allreduce_small(_pack(parts))
    summed = _unpack(total, [v.shape for v in parts])
    loss = jnp.sum(summed[0])
    gsum = dict(zip(SMALL, summed[1:]))
    gsum["conv_dw_w"] = lax.dynamic_slice_in_dim(gsum["conv_dw_w"], chip * cw_cols, cw_cols, axis=2)
    shapes = [wts[nm].shape for nm in SMALL]
    deltas, new_m, new_v = _adamw("adamw_small", _pack([wts[nm] for nm in SMALL]), _pack([gsum[nm] for nm in SMALL]),
                                  _pack([mom[nm] for nm in SMALL]), _pack([var[nm] for nm in SMALL]))
    grads = dict(gsum)
    delta = dict(zip(SMALL, _unpack(deltas, shapes)))
    newm = dict(zip(SMALL, _unpack(new_m, shapes)))
    newv = dict(zip(SMALL, _unpack(new_v, shapes)))

    gl = [gbig[nm] for nm in BIG]
    sib = _pair_exchange(gl)
    psums = [_pair_add(g, r) for g, r in zip(gl, sib)]
    got = _chip_exchange(psums)
    sums = _pair_gather([_sum4(ps, g) for ps, g in zip(psums, got)])
    for nm, g in zip(BIG, sums):
        shp = wts[nm].shape
        two = lambda v: v.reshape(-1, shp[-1])
        g = g.reshape(shp)
        d_w, n_m, n_v = _adamw("adamw_" + nm, two(wts[nm]), two(g), two(mom[nm]), two(var[nm]))
        grads[nm], delta[nm], newm[nm], newv[nm] = g, d_w.reshape(shp), n_m.reshape(shp), n_v.reshape(shp)

    return (loss, grad_x, *[grads[nm] for nm in WEIGHTS], *[delta[nm] for nm in WEIGHTS],
            *[newm[nm] for nm in WEIGHTS], *[newv[nm] for nm in WEIGHTS])
```

```python
import functools
import math

import jax
import jax.numpy as jnp
from jax import lax
from jax.experimental import pallas as pl
from jax.experimental.pallas import tpu as pltpu

F32 = jnp.float32
BF16 = jnp.bfloat16

D_MODEL = 1024
HEAD_DIM = 64
N_Q_HEADS = 8
N_KV_HEADS = 2
GQA_GROUP = N_Q_HEADS // N_KV_HEADS
ATT_BLOCK = 128
ROPE_THETA = 500000.0
ROPE_DIM = HEAD_DIM // 4
ROPE_HALF = ROPE_DIM // 2
Q_WIDTH = N_Q_HEADS * HEAD_DIM
KV_WIDTH = N_KV_HEADS * HEAD_DIM
SSM_WIDTH = 256
SSM_GROUP = 16
SSM_GROUPS = 16
SSM_STATE = 64
SSM_LANES = SSM_GROUPS * SSM_STATE
CONV_WIDTH = 256
CONV_K = 31
CONV_HALO = 32
FFN_HIDDEN = 2816
EPS = 1e-6
NEG_INF = -1e30
SCALE = HEAD_DIM ** -0.5

ADAM_LR = 0.001
ADAM_B1 = 0.9
ADAM_B2 = 0.999
ADAM_EPS = 1e-08
ADAM_WD = 0.01
ADAM_STEP = 10

N_CHIPS = 4
N_DEV = 8
SUBLANES = 8
VMEM_LIMIT = 56 * 1024 * 1024

MESH = pl.DeviceIdType.MESH


def _params(sem=None):
    return pltpu.CompilerParams(dimension_semantics=sem, vmem_limit_bytes=VMEM_LIMIT)


def R(arr, width=None, cb=0, rb=0):
    return ("r", arr, arr.shape[1] if width is None else width, (cb, rb))


def V(arr, width=None, cb=0):
    return ("v", arr, arr.shape[1] if width is None else width, cb)


def _cbf(cb):
    return cb if callable(cb) else (lambda j, c=cb: c + j)


def _rowwise(name, fn, ins, outs, accs=(), *, tm, ncol=1):
    t = [a for k, a, _, _ in ins if k == "r"][0].shape[0]
    tm = min(tm, t)
    assert t % tm == 0, (name, t, tm)
    n_i, n_o, n_a = len(ins), len(outs), len(accs)

    def body(*refs):
        vals = fn(*[r[...] for r in refs[:n_i]])
        if not isinstance(vals, (tuple, list)):
            vals = (vals,)
        for ref, val in zip(refs[n_i:n_i + n_o], vals[:n_o]):
            ref[...] = val.astype(ref.dtype)
        if n_a:
            acc_refs = refs[n_i + n_o:]

            @pl.when(pl.program_id(1) == 0)
            def _():
                for ref in acc_refs:
                    ref[...] = jnp.zeros_like(ref)

            for ref, val in zip(acc_refs, vals[n_o:]):
                ref[...] += val

    in_specs = []
    for kind, arr, width, cb in ins:
        if kind == "r":
            f = _cbf(cb[0])
            in_specs.append(pl.BlockSpec((tm, width), functools.partial(lambda j, i, f, rb: (i + rb, f(j)), f=f, rb=cb[1])))
        else:
            f = _cbf(cb)
            in_specs.append(pl.BlockSpec((arr.shape[0], width), functools.partial(lambda j, i, f: (0, f(j)), f=f)))
    out_specs, out_shape = [], []
    for total, width, cb, dt in outs:
        f = _cbf(cb)
        out_specs.append(pl.BlockSpec((tm, width), functools.partial(lambda j, i, f: (i, f(j)), f=f)))
        out_shape.append(jax.ShapeDtypeStruct((t, total), dt))
    for total, width, cb in accs:
        f = _cbf(cb)
        out_specs.append(pl.BlockSpec((1, width), functools.partial(lambda j, i, f: (0, f(j)), f=f)))
        out_shape.append(jax.ShapeDtypeStruct((1, total), F32))
    sem = ("arbitrary", "arbitrary") if n_a else ("parallel", "parallel")
    res = pl.pallas_call(body, out_shape=out_shape, grid=(ncol, t // tm), in_specs=in_specs, out_specs=out_specs,
                         name=name, compiler_params=_params(sem))(*[a for _, a, _, _ in ins])
    return res[0] if len(res) == 1 else res


def O(width, dtype, total=None, cb=0):
    return (width if total is None else total, width, cb, dtype)


def A(width, total=None, cb=0):
    return (width if total is None else total, width, cb)


_DIMS = {"nn": (((1,), (0,)), ((), ())), "nt": (((1,), (1,)), ((), ())), "tn": (((0,), (0,)), ((), ()))}


def _mm(name, a, b, mode, out_dtype, *, m, n, k, tm, tn, tk, a_off=0, b_off=0, res=None, bias=None, b_sh=None, o_sh=None,
        comm=None, a_keep=False):
    tm, tn, tk = min(tm, m), min(tn, n), min(tk, k)
    assert m % tm == 0 and n % tn == 0 and k % tk == 0, (name, m, n, k, tm, tn, tk)
    nk = k // tk
    has_res, has_bias = res is not None, bias is not None
    a_fn, a_ops = a if isinstance(a, tuple) else (None, [(a, None)])
    b_fn, b_ops = b if isinstance(b, tuple) else (None, [(b, None)])
    na, nb_ = len(a_ops), len(b_ops)
    a_bytes = sum(m * k * arr.dtype.itemsize for arr, _ in a_ops)
    b_bytes = sum(n * k * arr.dtype.itemsize for arr, _ in b_ops)
    swap = nk == 1 and b_bytes + (n // tn) * a_bytes < a_bytes + (m // tm) * b_bytes
    grid = (n // tn, m // tm, nk) if swap else (m // tm, n // tn, nk)
    ncomm = 0 if comm is None else len(comm["ins"])

    def body(*refs):
        g0, g1, kk = pl.program_id(0), pl.program_id(1), pl.program_id(2)
        gi, gj = (g1, g0) if swap else (g0, g1)
        a_tiles = [r[...] for r in refs[:na]]
        b_tiles = [r[...] for r in refs[na:na + nb_]]
        a_val = a_tiles[0] if a_fn is None else a_fn(gi, gj, kk, *a_tiles)
        b_val = b_tiles[0] if b_fn is None else b_fn(gi, gj, kk, *b_tiles)
        pos = na + nb_
        res_ref = bias_ref = None
        if has_res:
            res_ref = refs[pos]
            pos += 1
        if has_bias:
            bias_ref = refs[pos]
            pos += 1
        comm_ins = refs[pos:pos + ncomm]
        o_ref = refs[pos + ncomm]
        comm_outs = refs[pos + ncomm + 1:pos + 2 * ncomm + 1]
        scratch = refs[pos + 2 * ncomm + 1:]
        if a_keep:
            scratch[0][...] = a_val.astype(BF16)
            scratch = scratch[1:]
        if comm is not None:
            sems = scratch[1:] if nk > 1 else scratch

            @pl.when((g0 == 0) & (g1 == 0) & (kk == 0))
            def _():
                comm["start"](comm_ins, comm_outs, sems)

        def finish(r):
            if has_bias:
                r = r + bias_ref[...]
            if has_res:
                r = r + res_ref[...].astype(F32)
            o_ref[...] = r.astype(o_ref.dtype)

        part = lax.dot_general(a_val.astype(BF16), b_val.astype(BF16), _DIMS[mode], preferred_element_type=F32)
        if nk == 1:
            finish(part)
        else:
            acc_ref = scratch[0]

            @pl.when(kk == 0)
            def _():
                acc_ref[...] = part

            @pl.when(kk > 0)
            def _():
                acc_ref[...] += part

            @pl.when(kk == nk - 1)
            def _():
                finish(acc_ref[...])

        if comm is not None:
            @pl.when((g0 == grid[0] - 1) & (g1 == grid[1] - 1) & (kk == nk - 1))
            def _():
                comm["finish"](comm_ins, comm_outs, sems)

    def at(f):
        return (lambda g0, g1, kk: f(g1, g0, kk)) if swap else f

    if mode == "nn":
        a_spec = pl.BlockSpec((tm, tk), at(lambda i, j, kk: (i, kk + a_off)))
        b_spec = pl.BlockSpec((tk, tn), at(lambda i, j, kk: (kk, j + b_off)))
        if b_sh is not None:
            assert b_sh % tn == 0, (name, b_sh, tn)
            per = b_sh // tn
            b_spec = pl.BlockSpec((None, tk, tn), at(lambda i, j, kk: (j // per, kk, j % per)))
    elif mode == "nt":
        a_spec = pl.BlockSpec((tm, tk), at(lambda i, j, kk: (i, kk + a_off)))
        b_spec = pl.BlockSpec((tn, tk), at(lambda i, j, kk: (j, kk + b_off)))
        if b_sh is not None:
            assert b_sh % tk == 0, (name, b_sh, tk)
            per = b_sh // tk
            b_spec = pl.BlockSpec((None, tn, tk), at(lambda i, j, kk: (kk // per, j, kk % per)))
    else:
        a_spec = pl.BlockSpec((tk, tm), at(lambda i, j, kk: (kk, i + a_off)))
        b_spec = pl.BlockSpec((tk, tn), at(lambda i, j, kk: (kk, j + b_off)))
    a_specs = [a_spec] if a_fn is None else [pl.BlockSpec(a_spec.block_shape, at(f)) for _, f in a_ops]
    b_specs = [b_spec] if b_fn is None else [pl.BlockSpec(b_spec.block_shape, at(f)) for _, f in b_ops]
    in_specs, args = a_specs + b_specs, [arr for arr, _ in a_ops] + [arr for arr, _ in b_ops]
    if has_res:
        in_specs.append(pl.BlockSpec((tm, tn), at(lambda i, j, kk: (i, j))))
        args.append(res)
    if has_bias:
        in_specs.append(pl.BlockSpec((1, tn), at(lambda i, j, kk: (0, j))))
        args.append(bias)
    out_spec, out_shape = pl.BlockSpec((tm, tn), at(lambda i, j, kk: (i, j))), (m, n)
    if o_sh is not None:
        assert o_sh % tn == 0, (name, o_sh, tn)
        per_o = o_sh // tn
        out_spec = pl.BlockSpec((None, tm, tn), at(lambda i, j, kk: (j // per_o, i, j % per_o)))
        out_shape = (n // o_sh, m, o_sh)
    scratch = [pltpu.VMEM((tm, tn), F32)] if nk > 1 else []
    if a_keep:
        assert comm is None and o_sh is None and mode != "tn" and n == tn, name
        outs = pl.pallas_call(
            body, out_shape=[jax.ShapeDtypeStruct(out_shape, out_dtype), jax.ShapeDtypeStruct((m, k), BF16)], grid=grid,
            in_specs=in_specs, out_specs=[out_spec, pl.BlockSpec((tm, tk), at(lambda i, j, kk: (i, kk)))],
            scratch_shapes=scratch, name=name, compiler_params=_params(("parallel", "parallel", "arbitrary")))(*args)
        return outs[0], outs[1]
    if comm is None:
        return pl.pallas_call(
            body, out_shape=jax.ShapeDtypeStruct(out_shape, out_dtype), grid=grid, in_specs=in_specs,
            out_specs=out_spec, scratch_shapes=scratch, name=name,
            compiler_params=_params(("parallel", "parallel", "arbitrary")))(*args)
    outs = pl.pallas_call(
        body, out_shape=[jax.ShapeDtypeStruct(out_shape, out_dtype)] + comm["out_shapes"], grid=grid,
        in_specs=in_specs + [HBM] * ncomm, out_specs=[out_spec] + [HBM] * ncomm,
        scratch_shapes=scratch + comm["sems"], name=name,
        compiler_params=_params(("arbitrary", "arbitrary", "arbitrary")))(*args, *comm["ins"])
    return outs[0], outs[1:]


def _sig(v):
    return jax.nn.sigmoid(v)


def _rms_fwd(x, g):
    r = lax.rsqrt(jnp.mean(x * x, axis=-1, keepdims=True) + EPS)
    return x * r * g


def _rms_bwd(dh, x, dres, g):
    dh = dh.astype(F32)
    r = lax.rsqrt(jnp.mean(x * x, axis=-1, keepdims=True) + EPS)
    xh = x * r
    dxh = dh * g
    dx = r * (dxh - xh * jnp.mean(dxh * xh, axis=-1, keepdims=True))
    return dres + dx, jnp.sum(dh * xh, axis=0, keepdims=True)


def _rope_apply(t, c, sa, sb):
    w = t.shape[1]
    return t * c + pltpu.roll(t, w - ROPE_HALF, 1) * sa + pltpu.roll(t, ROPE_HALF, 1) * sb


def _rope_transpose(g, c, sa, sb):
    w = g.shape[1]
    return g * c + pltpu.roll(g * sa, ROPE_HALF, 1) + pltpu.roll(g * sb, w - ROPE_HALF, 1)


def _tile_lanes(tab, reps):
    return jnp.concatenate([tab] * reps, axis=1) if reps > 1 else tab


def _rope_fwd(q, k, v, c, sa, sb):
    rq = Q_WIDTH // c.shape[1]
    qr = _rope_apply(q.astype(F32), _tile_lanes(c, rq), _tile_lanes(sa, rq), _tile_lanes(sb, rq))
    kr = _rope_apply(k.astype(F32), c, sa, sb)
    return jnp.concatenate([qr, kr, v.astype(F32)], axis=1)


def _rope_bwd_q(g, c, sa, sb):
    rq = Q_WIDTH // c.shape[1]
    return _rope_transpose(g.astype(F32), _tile_lanes(c, rq), _tile_lanes(sa, rq), _tile_lanes(sb, rq))


def _gelu(v):
    return jax.nn.gelu(v, approximate=True)


def _gelu_grad(v):
    c0 = math.sqrt(2.0 / math.pi)
    inner = c0 * (v + 0.044715 * v * v * v)
    th = jnp.tanh(inner)
    return 0.5 * (1.0 + th) + 0.5 * v * (1.0 - th * th) * c0 * (1.0 + 3 * 0.044715 * v * v)


def _merge_fwd(g0, g1, g2, b0, b1, b2, ya, ga, gb, yc):
    s0 = _sig(g0.astype(F32) + b0)
    s1 = _sig(g1.astype(F32) + b1)
    s2 = _sig(g2.astype(F32) + b2)
    ys = ga.astype(F32) * _sig(gb.astype(F32))
    return s0 * ya.astype(F32) + s1 * ys + s2 * yc.astype(F32)


def _merge_bwd(dm, g0, g1, g2, b0, b1, b2, ya, ga, gb, yc):
    dm = dm.astype(F32)
    s0 = _sig(g0.astype(F32) + b0)
    s1 = _sig(g1.astype(F32) + b1)
    s2 = _sig(g2.astype(F32) + b2)
    ga = ga.astype(F32)
    sb = _sig(gb.astype(F32))
    ys = ga * sb
    dya = dm * s0
    dys = dm * s1
    dyc = dm * s2
    dga = dys * sb
    dgb = dys * ga * sb * (1.0 - sb)
    d0 = dm * ya.astype(F32) * s0 * (1.0 - s0)
    d1 = dm * ys * s1 * (1.0 - s1)
    d2 = dm * yc.astype(F32) * s2 * (1.0 - s2)
    cs = lambda v: jnp.sum(v, axis=0, keepdims=True)
    return dya, dga, dgb, dyc, d0, d1, d2, cs(d0), cs(d1), cs(d2), cs(dga), cs(dgb)


def _ffn_act(fg, fu):
    fg = fg.astype(F32)
    return fg * _sig(fg) * fu.astype(F32)


def _ffn_act_bwd(da, fg, fu):
    da, fg, fu = da.astype(F32), fg.astype(F32), fu.astype(F32)
    s = _sig(fg)
    return da * fu * (s * (1.0 + fg * (1.0 - s))), da * fg * s


def _ple_fwd(x, gp, e):
    return x + _sig(gp.astype(F32)) * e.astype(F32)


def _ple_bwd(dx, gp, e):
    s = _sig(gp.astype(F32))
    e = e.astype(F32)
    return dx * s, dx * e * s * (1.0 - s)


def _loss_fn(x, tgt, g):
    d = x.shape[1]
    r = lax.rsqrt(jnp.mean(x * x, axis=-1, keepdims=True) + EPS)
    xh = x * r
    err = xh * g - tgt
    dy = err * (1.0 / d)
    dxh = dy * g
    dx = r * (dxh - xh * jnp.mean(dxh * xh, axis=-1, keepdims=True))
    return dx, jnp.sum(err * err, axis=0, keepdims=True) * (0.5 / d), jnp.sum(dy * xh, axis=0, keepdims=True)


def _adamw_fn(w, g, m, v):
    m = ADAM_B1 * m + (1.0 - ADAM_B1) * g
    v = ADAM_B2 * v + (1.0 - ADAM_B2) * (g * g)
    m_hat = m / (1.0 - ADAM_B1 ** ADAM_STEP)
    v_hat = v / (1.0 - ADAM_B2 ** ADAM_STEP)
    delta = -ADAM_LR * (m_hat / (jnp.sqrt(v_hat) + ADAM_EPS) + ADAM_WD * w)
    return delta, m, v


def _band_mask(n):
    qi = lax.broadcasted_iota(jnp.int32, (ATT_BLOCK, 2 * ATT_BLOCK), 0)
    kj = lax.broadcasted_iota(jnp.int32, (ATT_BLOCK, 2 * ATT_BLOCK), 1)
    dist = qi + ATT_BLOCK - kj
    return (dist >= 0) & (dist < ATT_BLOCK) & ((n > 0) | (kj >= ATT_BLOCK))


K_HEADS_AT = N_Q_HEADS // N_KV_HEADS


def _att_specs(nb):
    qs = pl.BlockSpec((N_Q_HEADS, ATT_BLOCK, HEAD_DIM), lambda b, n: (0, b * nb + n, 0))

    def kv(head_block, back):
        return pl.BlockSpec((N_KV_HEADS, ATT_BLOCK, HEAD_DIM),
                            lambda b, n: (head_block, b * nb + jnp.maximum(n - back, 0), 0))

    stat = pl.BlockSpec((N_Q_HEADS, ATT_BLOCK, 1), lambda b, n: (0, b * nb + n, 0))
    sink = pl.BlockSpec((N_Q_HEADS, 1, 1), lambda b, n: (0, 0, 0))
    return qs, [kv(K_HEADS_AT, 1), kv(K_HEADS_AT, 0), kv(K_HEADS_AT + 1, 1), kv(K_HEADS_AT + 1, 0)], stat, sink


def _attn_fwd(qkv, sinks, nbatch, seq):
    t = qkv.shape[1]
    nb = seq // ATT_BLOCK
    qs, kvs, stat, sink = _att_specs(nb)

    def body(q_ref, kp_ref, kc_ref, vp_ref, vc_ref, sink_ref, o_ref, lse_ref):
        mask = _band_mask(pl.program_id(1))
        rows = GQA_GROUP * ATT_BLOCK
        for kv in range(N_KV_HEADS):
            hs = slice(kv * GQA_GROUP, (kv + 1) * GQA_GROUP)
            kk = jnp.concatenate([kp_ref[kv], kc_ref[kv]], axis=0)
            vv = jnp.concatenate([vp_ref[kv], vc_ref[kv]], axis=0)
            q4 = (q_ref[hs] * SCALE).reshape(rows, HEAD_DIM)
            s = lax.dot_general(q4, kk, _DIMS["nt"], preferred_element_type=F32)
            s = jnp.where(mask, s.reshape(GQA_GROUP, ATT_BLOCK, 2 * ATT_BLOCK), NEG_INF)
            sk = sink_ref[hs]
            mx = jnp.maximum(jnp.max(s, axis=-1, keepdims=True), sk)
            p = jnp.exp(s - mx)
            den = jnp.sum(p, axis=-1, keepdims=True) + jnp.exp(sk - mx)
            o = lax.dot_general(p.reshape(rows, 2 * ATT_BLOCK).astype(BF16), vv, _DIMS["nn"],
                                preferred_element_type=F32).reshape(GQA_GROUP, ATT_BLOCK, HEAD_DIM)
            o_ref[hs] = (o * (1.0 / den)).astype(o_ref.dtype)
            lse_ref[hs] = mx + jnp.log(den)

    return pl.pallas_call(
        body, grid=(nbatch, nb), in_specs=[qs] + kvs + [sink], out_specs=[qs, stat],
        out_shape=[jax.ShapeDtypeStruct((N_Q_HEADS, t, HEAD_DIM), BF16), jax.ShapeDtypeStruct((N_Q_HEADS, t, 1), F32)],
        name="attn_fwd", compiler_params=_params(("parallel", "parallel")))(qkv, qkv, qkv, qkv, qkv, sinks)


def _attn_bwd(qkv, oh, doh, lse, sinks, nbatch, seq):
    t = qkv.shape[1]
    nb = seq // ATT_BLOCK
    qs, kvs, stat, sink = _att_specs(nb)

    def body(q_ref, kp_ref, kc_ref, vp_ref, vc_ref, o_ref, do_ref, lse_ref, sink_ref, dqkv_ref, dsink_ref):
        dq_ref = dqkv_ref.at[pl.ds(0, N_Q_HEADS)]
        dkc_ref, dvc_ref, dkp_ref, dvp_ref = (dqkv_ref.at[pl.ds(N_Q_HEADS + N_KV_HEADS * i, N_KV_HEADS)]
                                              for i in range(4))
        first = (pl.program_id(0) == 0) & (pl.program_id(1) == 0)

        @pl.when(first)
        def _():
            dsink_ref[...] = jnp.zeros_like(dsink_ref)

        mask = _band_mask(pl.program_id(1))
        rows = GQA_GROUP * ATT_BLOCK
        band = (GQA_GROUP, ATT_BLOCK, 2 * ATT_BLOCK)
        for kv in range(N_KV_HEADS):
            hs = slice(kv * GQA_GROUP, (kv + 1) * GQA_GROUP)
            kk = jnp.concatenate([kp_ref[kv], kc_ref[kv]], axis=0)
            vv = jnp.concatenate([vp_ref[kv], vc_ref[kv]], axis=0)
            q4 = q_ref[hs].reshape(rows, HEAD_DIM)
            do4 = do_ref[hs].reshape(rows, HEAD_DIM)
            lse4 = lse_ref[hs]
            s = lax.dot_general(q4 * SCALE, kk, _DIMS["nt"], preferred_element_type=F32).reshape(band)
            p = jnp.where(mask, jnp.exp(s - lse4), 0.0)
            dd = jnp.sum(do_ref[hs].astype(F32) * o_ref[hs].astype(F32), axis=-1, keepdims=True)
            dp = lax.dot_general(do4, vv, _DIMS["nt"], preferred_element_type=F32).reshape(band)
            ds = (p * (dp - dd) * SCALE).astype(BF16).reshape(rows, 2 * ATT_BLOCK)
            dq = lax.dot_general(ds, kk, _DIMS["nn"], preferred_element_type=F32)
            dq_ref[hs] = dq.reshape(GQA_GROUP, ATT_BLOCK, HEAD_DIM).astype(dq_ref.dtype)
            dk = lax.dot_general(ds, q4, _DIMS["tn"], preferred_element_type=F32)
            dv = lax.dot_general(p.astype(BF16).reshape(rows, 2 * ATT_BLOCK), do4, _DIMS["tn"],
                                 preferred_element_type=F32)
            dsink_ref[hs] += -jnp.sum(jnp.exp(sink_ref[hs] - lse4) * dd, axis=1, keepdims=True)
            dkp_ref[kv] = dk[:ATT_BLOCK]
            dkc_ref[kv] = dk[ATT_BLOCK:]
            dvp_ref[kv] = dv[:ATT_BLOCK]
            dvc_ref[kv] = dv[ATT_BLOCK:]

    n_out = 2 * N_Q_HEADS
    return pl.pallas_call(
        body, grid=(nbatch, nb), in_specs=[qs] + kvs + [qs, qs, stat, sink],
        out_specs=[pl.BlockSpec((n_out, ATT_BLOCK, HEAD_DIM), lambda b, n: (0, b * nb + n, 0)), sink],
        out_shape=[jax.ShapeDtypeStruct((n_out, t, HEAD_DIM), F32), jax.ShapeDtypeStruct((N_Q_HEADS, 1, 1), F32)],
        name="attn_bwd", compiler_params=_params(("arbitrary", "arbitrary")))(
        qkv, qkv, qkv, qkv, qkv, oh, doh, lse, sinks)


def _kv_combine(dqkv, c, sa, sb, seq):
    t = dqkv.shape[0]
    nb = seq // ATT_BLOCK
    nblk = t // ATT_BLOCK
    col0 = Q_WIDTH // KV_WIDTH

    def body(kc_ref, kp_ref, vc_ref, vp_ref, c_ref, sa_ref, sb_ref, dk_ref, dv_ref):
        has_next = (pl.program_id(0) % nb) != nb - 1
        dk = kc_ref[...] + jnp.where(has_next, kp_ref[...], 0.0)
        dv = vc_ref[...] + jnp.where(has_next, vp_ref[...], 0.0)
        dk_ref[...] = _rope_transpose(dk, c_ref[...], sa_ref[...], sb_ref[...]).astype(dk_ref.dtype)
        dv_ref[...] = dv.astype(dv_ref.dtype)

    cur = pl.BlockSpec((ATT_BLOCK, KV_WIDTH), lambda i: (i, 0))
    own = lambda col: pl.BlockSpec((ATT_BLOCK, KV_WIDTH), lambda i: (i, col0 + col))
    nxt = lambda col: pl.BlockSpec((ATT_BLOCK, KV_WIDTH), lambda i: (jnp.minimum(i + 1, nblk - 1), col0 + col))
    o = jax.ShapeDtypeStruct((t, KV_WIDTH), BF16)
    return pl.pallas_call(body, grid=(nblk,), in_specs=[own(0), nxt(2), own(1), nxt(3), cur, cur, cur],
                          out_specs=[cur, cur], out_shape=[o, o], name="kv_combine",
                          compiler_params=_params(("parallel",)))(dqkv, dqkv, dqkv, dqkv, c, sa, sb)


def _scan_block(ref, tab_ref, carry, ngroups, reverse):
    shifts = (7, 6, 4) if reverse else (1, 2, 4)
    n = SSM_LANES

    def step(i, car):
        g = (ngroups - 1 - i) if reverse else i
        r0 = pl.multiple_of(g * SUBLANES, SUBLANES)
        xr = ref[pl.ds(r0, SUBLANES), :n]
        xi = ref[pl.ds(r0, SUBLANES), n:]
        for s, sh in enumerate(shifts):
            pr, pi = tab_ref[2 * s], tab_ref[2 * s + 1]
            yr, yi = pltpu.roll(xr, sh, 0), pltpu.roll(xi, sh, 0)
            xr, xi = xr + pr * yr - pi * yi, xi + pr * yi + pi * yr
        cr, ci = car
        qr, qi = tab_ref[6], tab_ref[7]
        xr, xi = xr + qr * cr - qi * ci, xi + qr * ci + qi * cr
        ref[pl.ds(r0, SUBLANES), :n] = xr
        ref[pl.ds(r0, SUBLANES), n:] = xi
        last = r0 if reverse else r0 + SUBLANES - 1
        return ref[pl.ds(last, 1), :n], ref[pl.ds(last, 1), n:]

    return lax.fori_loop(0, ngroups, step, carry, unroll=2)


def _ssm_chunk(seq):
    return min(512, seq)


def _ssm_fwd(z, wb, wc, tab, dskip, nbatch, seq):
    t = z.shape[0]
    tc = _ssm_chunk(seq)
    nc = seq // tc
    n2 = 2 * SSM_LANES

    def body(u_ref, wb_ref, wc_ref, tab_ref, d_ref, st_ref, y_ref, gel_ref, car_ref):
        @pl.when(pl.program_id(1) == 0)
        def _():
            car_ref[...] = jnp.zeros_like(car_ref)

        u = u_ref[...]
        st_ref[...] = lax.dot_general(u, wb_ref[...], _DIMS["nn"], preferred_element_type=F32)
        cr, ci = _scan_block(st_ref, tab_ref, (car_ref[:, :SSM_LANES], car_ref[:, SSM_LANES:]), tc // SUBLANES, False)
        car_ref[:, :SSM_LANES] = cr
        car_ref[:, SSM_LANES:] = ci
        y = lax.dot_general(st_ref[...].astype(BF16), wc_ref[...], _DIMS["nn"], preferred_element_type=F32)
        y = y + d_ref[...] * u.astype(F32)
        y_ref[...] = y
        gel_ref[...] = _gelu(y).astype(gel_ref.dtype)

    row = lambda b, c: (b * nc + c, 0)
    full = lambda b, c: (0, 0)
    return pl.pallas_call(
        body, grid=(nbatch, nc),
        in_specs=[pl.BlockSpec((tc, SSM_WIDTH), lambda b, c: (b * nc + c, 3)), pl.BlockSpec((SSM_WIDTH, n2), full),
                  pl.BlockSpec((n2, SSM_WIDTH), full), pl.BlockSpec((8, SUBLANES, SSM_LANES), lambda b, c: (0, 0, 0)),
                  pl.BlockSpec((1, SSM_WIDTH), full)],
        out_specs=[pl.BlockSpec((tc, n2), row), pl.BlockSpec((tc, SSM_WIDTH), row), pl.BlockSpec((tc, SSM_WIDTH), row)],
        out_shape=[jax.ShapeDtypeStruct((t, n2), F32), jax.ShapeDtypeStruct((t, SSM_WIDTH), F32),
                   jax.ShapeDtypeStruct((t, SSM_WIDTH), BF16)],
        scratch_shapes=[pltpu.VMEM((1, n2), F32)], name="ssm_fwd",
        compiler_params=_params(("arbitrary", "arbitrary")))(z, wb, wc, tab, dskip)


def _ssm_bwd(dgi, ys, st, z, wbt, wct, tab_rev, dskip, nbatch, seq):
    t = z.shape[0]
    tc = _ssm_chunk(seq)
    nc = seq // tc
    n = SSM_LANES
    n2 = 2 * n
    ng = tc // SUBLANES

    def body(dgi_ref, ys_ref, st_ref, stp_ref, u_ref, wbt_ref, wct_ref, tab_ref, d_ref,
             du_ref, dwb_ref, dwc_ref, dd_ref, da_ref, p_ref, sb_ref, car_ref):
        b, c = pl.program_id(0), pl.program_id(1)
        ct = nc - 1 - c

        @pl.when((b == 0) & (c == 0))
        def _():
            dwb_ref[...] = jnp.zeros_like(dwb_ref)
            dwc_ref[...] = jnp.zeros_like(dwc_ref)
            dd_ref[...] = jnp.zeros_like(dd_ref)
            da_ref[...] = jnp.zeros_like(da_ref)

        @pl.when(c == 0)
        def _():
            car_ref[...] = jnp.zeros_like(car_ref)

        u = u_ref[...]
        dys = dgi_ref[...].astype(F32) * _gelu_grad(ys_ref[...])
        dys_b = dys.astype(BF16)
        st = st_ref[...]
        dd_ref[...] += jnp.sum(dys * u.astype(F32), axis=0, keepdims=True)
        dwc_ref[...] += lax.dot_general(st.astype(BF16), dys_b, _DIMS["tn"], preferred_element_type=F32)
        p_ref[...] = lax.dot_general(dys_b, wct_ref[...], _DIMS["nn"], preferred_element_type=F32)
        cr, ci = _scan_block(p_ref, tab_ref, (car_ref[:, :n], car_ref[:, n:]), ng, True)
        car_ref[:, :n] = cr
        car_ref[:, n:] = ci
        p = p_ref[...]
        pb = p.astype(BF16)
        dwb_ref[...] += lax.dot_general(u, pb, _DIMS["tn"], preferred_element_type=F32)
        du = lax.dot_general(pb, wbt_ref[...], _DIMS["nn"], preferred_element_type=F32) + d_ref[...] * dys
        du_ref[...] = du.astype(du_ref.dtype)
        sb_ref[pl.ds(0, SUBLANES), :] = jnp.where(ct > 0, stp_ref[...], 0.0)
        sb_ref[pl.ds(SUBLANES, tc), :] = st
        row0 = lax.broadcasted_iota(jnp.int32, (SUBLANES, n), 0) == 0

        def acc_step(g, acc):
            ar, ai = acc
            r0 = pl.multiple_of(g * SUBLANES, SUBLANES)
            edge_r = sb_ref[pl.ds(r0 + SUBLANES - 1, 1), :n]
            edge_i = sb_ref[pl.ds(r0 + SUBLANES - 1, 1), n:]
            sr = jnp.where(row0, edge_r, pltpu.roll(sb_ref[pl.ds(r0 + SUBLANES, SUBLANES), :n], 1, 0))
            si = jnp.where(row0, edge_i, pltpu.roll(sb_ref[pl.ds(r0 + SUBLANES, SUBLANES), n:], 1, 0))
            pr = p_ref[pl.ds(r0, SUBLANES), :n]
            pi = p_ref[pl.ds(r0, SUBLANES), n:]
            return ar + pr * sr + pi * si, ai + pi * sr - pr * si

        zero = jnp.zeros((SUBLANES, n), F32)
        ar, ai = lax.fori_loop(0, ng, acc_step, (zero, zero), unroll=2)
        da_ref[:, :n] += ar
        da_ref[:, n:] += ai

    row = lambda b, c: (b * nc + (nc - 1 - c), 0)
    prev8 = lambda b, c: (jnp.maximum((b * nc + (nc - 1 - c)) * (tc // SUBLANES) - 1, 0), 0)
    full = lambda b, c: (0, 0)
    return pl.pallas_call(
        body, grid=(nbatch, nc),
        in_specs=[pl.BlockSpec((tc, SSM_WIDTH), row), pl.BlockSpec((tc, SSM_WIDTH), row), pl.BlockSpec((tc, n2), row),
                  pl.BlockSpec((SUBLANES, n2), prev8),
                  pl.BlockSpec((tc, SSM_WIDTH), lambda b, c: (b * nc + (nc - 1 - c), 3)),
                  pl.BlockSpec((n2, SSM_WIDTH), full), pl.BlockSpec((SSM_WIDTH, n2), full),
                  pl.BlockSpec((8, SUBLANES, n), lambda b, c: (0, 0, 0)), pl.BlockSpec((1, SSM_WIDTH), full)],
        out_specs=[pl.BlockSpec((tc, SSM_WIDTH), row), pl.BlockSpec((SSM_WIDTH, n2), full),
                   pl.BlockSpec((n2, SSM_WIDTH), full), pl.BlockSpec((1, SSM_WIDTH), full),
                   pl.BlockSpec((SUBLANES, n2), full)],
        out_shape=[jax.ShapeDtypeStruct((t, SSM_WIDTH), BF16), jax.ShapeDtypeStruct((SSM_WIDTH, n2), F32),
                   jax.ShapeDtypeStruct((n2, SSM_WIDTH), F32), jax.ShapeDtypeStruct((1, SSM_WIDTH), F32),
                   jax.ShapeDtypeStruct((SUBLANES, n2), F32)],
        scratch_shapes=[pltpu.VMEM((tc, n2), F32), pltpu.VMEM((tc + SUBLANES, n2), F32), pltpu.VMEM((1, n2), F32)],
        name="ssm_bwd", compiler_params=_params(("arbitrary", "arbitrary")))(
        dgi, ys, st, st, z, wbt, wct, tab_rev, dskip)


def _ssm_prep(lam_re, lam_im, log_dt, b_re, b_im, c_re, c_im):
    lr = jnp.minimum(lam_re, -1e-4)
    li = lam_im
    dt = jnp.exp(log_dt)[:, None]
    mag = jnp.exp(lr * dt)
    a_re = mag * jnp.cos(li * dt)
    a_im = mag * jnp.sin(li * dt)
    den = lr * lr + li * li
    x_re, x_im = a_re - 1.0, a_im
    f_re = (x_re * lr + x_im * li) / den
    f_im = (x_im * lr - x_re * li) / den
    bb_re = f_re[..., None] * b_re - f_im[..., None] * b_im
    bb_im = f_re[..., None] * b_im + f_im[..., None] * b_re
    eye = jnp.eye(SSM_GROUPS, dtype=F32)
    emb_b = lambda v: jnp.einsum("gnh,gk->ghkn", v, eye).reshape(SSM_WIDTH, SSM_LANES)
    emb_c = lambda v: jnp.einsum("ghn,gk->gnkh", v, eye).reshape(SSM_LANES, SSM_WIDTH)
    wb = jnp.concatenate([emb_b(bb_re), emb_b(bb_im)], axis=1)
    wc = jnp.concatenate([emb_c(c_re), -emb_c(c_im)], axis=0)
    return a_re.reshape(-1), a_im.reshape(-1), wb, wc


def _ssm_tables(a_re, a_im, reverse):
    if reverse:
        a_im = -a_im
    pw = [(a_re, a_im)]
    for _ in range(SUBLANES - 1):
        pr, pi = pw[-1]
        pw.append((pr * a_re - pi * a_im, pr * a_im + pi * a_re))
    rows = jnp.arange(SUBLANES)[:, None]
    tabs = []
    for k in (1, 2, 4):
        ok = (rows + k <= SUBLANES - 1) if reverse else (rows >= k)
        tabs += [jnp.where(ok, pw[k - 1][0][None], 0.0), jnp.where(ok, pw[k - 1][1][None], 0.0)]
    order = list(range(SUBLANES - 1, -1, -1)) if reverse else list(range(SUBLANES))
    tabs += [jnp.stack([pw[i][0] for i in order]), jnp.stack([pw[i][1] for i in order])]
    return jnp.stack(tabs)


def _conv_chunk(seq):
    return min(512, seq)


def _shifted(buf, sh, tc, offsets):
    for b in range(SUBLANES):
        idx = [i for i, o in enumerate(offsets) if o % SUBLANES == b]
        if not idx:
            continue
        src = buf
        if b:
            span = tc + SUBLANES * max(offsets[i] // SUBLANES for i in idx)
            sh[pl.ds(0, span), :] = buf[pl.ds(b, span), :]
            src = sh
        for i in idx:
            yield i, src[pl.ds(offsets[i] // SUBLANES * SUBLANES, tc), :]


def _conv_fwd(z, w, bias, lg, lb, nbatch, seq):
    t = z.shape[0]
    tc = _conv_chunk(seq)
    nc = seq // tc

    def body(a_ref, g_ref, w_ref, b_ref, lg_ref, lb_ref, cv_ref, sc_ref, ubuf, sh):
        c = pl.program_id(1)

        @pl.when(c == 0)
        def _():
            ubuf[pl.ds(0, CONV_HALO), :] = jnp.zeros((CONV_HALO, CONV_WIDTH), F32)

        @pl.when(c > 0)
        def _():
            ubuf[pl.ds(0, CONV_HALO), :] = ubuf[pl.ds(tc, CONV_HALO), :]

        ubuf[pl.ds(CONV_HALO, tc), :] = a_ref[...].astype(F32) * _sig(g_ref[...].astype(F32))
        acc = jnp.zeros((tc, CONV_WIDTH), F32) + b_ref[...]
        for k, win in _shifted(ubuf, sh, tc, [CONV_HALO - (CONV_K - 1) + k for k in range(CONV_K)]):
            acc = acc + w_ref[pl.ds(k, 1), :] * win
        cv_ref[...] = acc
        mu = jnp.mean(acc, axis=-1, keepdims=True)
        xc = acc - mu
        y = xc * lax.rsqrt(jnp.mean(xc * xc, axis=-1, keepdims=True) + EPS) * lg_ref[...] + lb_ref[...]
        sc_ref[...] = (y * _sig(y)).astype(sc_ref.dtype)

    row = lambda b, c: (b * nc + c, 0)
    full = lambda b, c: (0, 0)
    vec = pl.BlockSpec((1, CONV_WIDTH), full)
    return pl.pallas_call(
        body, grid=(nbatch, nc),
        in_specs=[pl.BlockSpec((tc, CONV_WIDTH), lambda b, c: (b * nc + c, 4)),
                  pl.BlockSpec((tc, CONV_WIDTH), lambda b, c: (b * nc + c, 5)),
                  pl.BlockSpec((CONV_HALO, CONV_WIDTH), full), vec, vec, vec],
        out_specs=[pl.BlockSpec((tc, CONV_WIDTH), row), pl.BlockSpec((tc, CONV_WIDTH), row)],
        out_shape=[jax.ShapeDtypeStruct((t, CONV_WIDTH), F32), jax.ShapeDtypeStruct((t, CONV_WIDTH), BF16)],
        scratch_shapes=[pltpu.VMEM((CONV_HALO + tc, CONV_WIDTH), F32)] * 2, name="conv_fwd",
        compiler_params=_params(("arbitrary", "arbitrary")))(z, z, w, bias, lg, lb)


def _conv_bwd(dsc, cv, z, w, lg, lb, nbatch, seq):
    t = z.shape[0]
    tc = _conv_chunk(seq)
    nc = seq // tc
    hb = tc // CONV_HALO

    def body(dsc_ref, cv_ref, a_ref, g_ref, ap_ref, gp_ref, w_ref, lg_ref, lb_ref,
             da_ref, dg_ref, dw_ref, db_ref, dlg_ref, dlb_ref, ubuf, dbuf, sh):
        b, c = pl.program_id(0), pl.program_id(1)
        ct = nc - 1 - c

        @pl.when((b == 0) & (c == 0))
        def _():
            dw_ref[...] = jnp.zeros_like(dw_ref)
            db_ref[...] = jnp.zeros_like(db_ref)
            dlg_ref[...] = jnp.zeros_like(dlg_ref)
            dlb_ref[...] = jnp.zeros_like(dlb_ref)

        cvv = cv_ref[...]
        mu = jnp.mean(cvv, axis=-1, keepdims=True)
        xc = cvv - mu
        rstd = lax.rsqrt(jnp.mean(xc * xc, axis=-1, keepdims=True) + EPS)
        xh = xc * rstd
        y = xh * lg_ref[...] + lb_ref[...]
        sy = _sig(y)
        dy = dsc_ref[...].astype(F32) * (sy * (1.0 + y * (1.0 - sy)))
        dlg_ref[...] += jnp.sum(dy * xh, axis=0, keepdims=True)
        dlb_ref[...] += jnp.sum(dy, axis=0, keepdims=True)
        dxh = dy * lg_ref[...]
        dcv = rstd * (dxh - jnp.mean(dxh, axis=-1, keepdims=True) - xh * jnp.mean(dxh * xh, axis=-1, keepdims=True))
        db_ref[...] += jnp.sum(dcv, axis=0, keepdims=True)

        @pl.when(c == 0)
        def _():
            dbuf[pl.ds(tc, CONV_HALO), :] = jnp.zeros((CONV_HALO, CONV_WIDTH), F32)

        @pl.when(c > 0)
        def _():
            dbuf[pl.ds(tc, CONV_HALO), :] = dbuf[pl.ds(0, CONV_HALO), :]

        dbuf[pl.ds(0, tc), :] = dcv
        a = a_ref[...].astype(F32)
        sg = _sig(g_ref[...].astype(F32))
        ubuf[pl.ds(0, CONV_HALO), :] = jnp.where(ct > 0, ap_ref[...].astype(F32) * _sig(gp_ref[...].astype(F32)), 0.0)
        ubuf[pl.ds(CONV_HALO, tc), :] = a * sg
        du = jnp.zeros((tc, CONV_WIDTH), F32)
        for k, win in _shifted(dbuf, sh, tc, [CONV_K - 1 - k for k in range(CONV_K)]):
            du = du + w_ref[pl.ds(k, 1), :] * win
        for k, win in _shifted(ubuf, sh, tc, [CONV_HALO - (CONV_K - 1) + k for k in range(CONV_K)]):
            dw_ref[pl.ds(k, 1), :] += jnp.sum(dcv * win, axis=0, keepdims=True)
        da_ref[...] = (du * sg).astype(da_ref.dtype)
        dg_ref[...] = (du * a * sg * (1.0 - sg)).astype(dg_ref.dtype)

    row = lambda b, c: (b * nc + (nc - 1 - c), 0)
    full = lambda b, c: (0, 0)
    vec = pl.BlockSpec((1, CONV_WIDTH), full)
    blk = pl.BlockSpec((tc, CONV_WIDTH), row)

    def zcol(col):
        return pl.BlockSpec((tc, CONV_WIDTH), lambda b, c: (b * nc + (nc - 1 - c), col))

    def zprev(col):
        return pl.BlockSpec((CONV_HALO, CONV_WIDTH),
                            lambda b, c: (jnp.maximum((b * nc + (nc - 1 - c)) * hb - 1, 0), col))

    o = jax.ShapeDtypeStruct((t, CONV_WIDTH), BF16)
    v = jax.ShapeDtypeStruct((1, CONV_WIDTH), F32)
    return pl.pallas_call(
        body, grid=(nbatch, nc),
        in_specs=[blk, blk, zcol(4), zcol(5), zprev(4), zprev(5), pl.BlockSpec((CONV_HALO, CONV_WIDTH), full), vec, vec],
        out_specs=[blk, blk, pl.BlockSpec((CONV_HALO, CONV_WIDTH), full), vec, vec, vec],
        out_shape=[o, o, jax.ShapeDtypeStruct((CONV_HALO, CONV_WIDTH), F32), v, v, v],
        scratch_shapes=[pltpu.VMEM((CONV_HALO + tc, CONV_WIDTH), F32)] * 3, name="conv_bwd", compiler_params=_params(("arbitrary", "arbitrary")))(dsc, cv, z, z, z, z, w, lg, lb)


BIG = ("w_in", "w_attn_out", "w_ssm_glu", "w_conv_out", "w_mix_out", "w_ffn_in", "w_ffn_out", "w_ple_in", "w_ple_gate")
BIG_AXIS = {"w_in": 2, "w_attn_out": 2, "w_ssm_glu": 2, "w_conv_out": 2, "w_mix_out": 1, "w_ffn_in": 2,
            "w_ffn_out": 1, "w_ple_in": 2, "w_ple_gate": 1}
SHARD_MAJOR = ("w_in", "w_ffn_in")
SMALL = ("mix_norm_g", "b_gate", "attn_sinks", "ssm_lambda_re", "ssm_lambda_im", "ssm_log_dt", "ssm_b_re", "ssm_b_im",
         "ssm_c_re", "ssm_c_im", "ssm_d", "b_ssm_glu", "conv_dw_w", "conv_dw_b", "conv_norm_g", "conv_norm_b",
         "ffn_norm_g", "ple_norm_g", "final_norm_g")
WEIGHTS = ("mix_norm_g", "w_in", "b_gate", "attn_sinks", "w_attn_out", "ssm_lambda_re", "ssm_lambda_im", "ssm_log_dt",
           "ssm_b_re", "ssm_b_im", "ssm_c_re", "ssm_c_im", "ssm_d", "w_ssm_glu", "b_ssm_glu", "conv_dw_w", "conv_dw_b",
           "conv_norm_g", "conv_norm_b", "w_conv_out", "w_mix_out", "ffn_norm_g", "w_ffn_in", "w_ffn_out", "w_ple_in",
           "ple_norm_g", "w_ple_gate", "final_norm_g")
SSM_NAMES = ("ssm_lambda_re", "ssm_lambda_im", "ssm_log_dt", "ssm_b_re", "ssm_b_im", "ssm_c_re", "ssm_c_im")


def _ple_block(x, p_l, g, w_in, w_gate):
    t, d = x.shape
    kp = p_l.shape[1]
    tm = min(512, t)

    def body(x_ref, p_ref, g_ref, wi_ref, wg_ref, o_ref, gp_ref, h_ref, e_ref):
        xv = x_ref[...]
        e = lax.dot_general(p_ref[...].astype(BF16), wi_ref[...], _DIMS["nn"], preferred_element_type=F32).astype(BF16)
        h = _rms_fwd(xv, g_ref[...]).astype(BF16)
        gp = lax.dot_general(h, wg_ref[...], _DIMS["nn"], preferred_element_type=F32).astype(BF16)
        o_ref[...] = _ple_fwd(xv, gp, e)
        gp_ref[...], h_ref[...], e_ref[...] = gp, h, e

    row = lambda width: pl.BlockSpec((tm, width), lambda i: (i, 0))
    full = lambda v: pl.BlockSpec(v.shape, lambda i: (0, 0))
    out = lambda dt: jax.ShapeDtypeStruct((t, d), dt)
    return pl.pallas_call(
        body, grid=(t // tm,), in_specs=[row(d), row(kp), full(g), full(w_in), full(w_gate)],
        out_specs=[row(d)] * 4, out_shape=[out(F32), out(BF16), out(BF16), out(BF16)], name="ple_block",
        compiler_params=_params(("parallel",)))(x, p_l, g, w_in, w_gate)


def _mix_out_block(merged, w_mix, x, g):
    t, d = x.shape
    tm = min(512, t)

    def body(m_ref, w_ref, x_ref, g_ref, o_ref, h_ref):
        x1 = x_ref[...] + lax.dot_general(m_ref[...], w_ref[...], _DIMS["nn"], preferred_element_type=F32)
        o_ref[...] = x1
        h_ref[...] = _rms_fwd(x1, g_ref[...]).astype(h_ref.dtype)

    row = pl.BlockSpec((tm, d), lambda i: (i, 0))
    full = lambda v: pl.BlockSpec(v.shape, lambda i: (0, 0))
    return pl.pallas_call(
        body, grid=(t // tm,), in_specs=[row, full(w_mix), row, full(g)], out_specs=[row, row],
        out_shape=[jax.ShapeDtypeStruct((t, d), F32), jax.ShapeDtypeStruct((t, d), BF16)], name="mix_out_block",
        compiler_params=_params(("parallel",)))(merged, w_mix, x, g)


def _ple_block_bwd(dx3, gp, e, x, g, w_gate):
    t, d = x.shape
    tm = min(512, t)

    def body(dx3_ref, gp_ref, e_ref, x_ref, g_ref, wg_ref, de_ref, dgp_ref, dx_ref, dg_ref):
        @pl.when(pl.program_id(0) == 0)
        def _():
            dg_ref[...] = jnp.zeros_like(dg_ref)

        dx3 = dx3_ref[...]
        de, dgp = _ple_bwd(dx3, gp_ref[...], e_ref[...])
        dgp = dgp.astype(BF16)
        de_ref[...] = de.astype(de_ref.dtype)
        dgp_ref[...] = dgp
        dh = lax.dot_general(dgp, wg_ref[...], _DIMS["nt"], preferred_element_type=F32).astype(BF16)
        dx, dg = _rms_bwd(dh, x_ref[...], dx3, g_ref[...])
        dx_ref[...] = dx
        dg_ref[...] += dg

    row = pl.BlockSpec((tm, d), lambda i: (i, 0))
    full = lambda v: pl.BlockSpec(v.shape, lambda i: (0, 0))
    out = lambda dt: jax.ShapeDtypeStruct((t, d), dt)
    return pl.pallas_call(
        body, grid=(t // tm,), in_specs=[row, row, row, row, full(g), full(w_gate)],
        out_specs=[row, row, row, pl.BlockSpec((1, d), lambda i: (0, 0))],
        out_shape=[out(BF16), out(BF16), out(F32), jax.ShapeDtypeStruct((1, d), F32)], name="ple_block_bwd",
        compiler_params=_params(("arbitrary",)))(dx3, gp, e, x, g, w_gate)


def _in_proj_bwd(dz, w_in4, x, dres, g):
    t, d = x.shape
    nsh, _, cc = w_in4.shape
    tm = min(512, t)

    def body(dz_ref, w_ref, x_ref, dres_ref, g_ref, dx_ref, dg_ref):
        @pl.when(pl.program_id(0) == 0)
        def _():
            dg_ref[...] = jnp.zeros_like(dg_ref)

        dh = jnp.zeros((tm, d), F32)
        for sh in range(nsh):
            dh = dh + lax.dot_general(dz_ref[:, sh * cc:(sh + 1) * cc], w_ref[sh], _DIMS["nt"],
                                      preferred_element_type=F32)
        dx, dg = _rms_bwd(dh.astype(BF16), x_ref[...], dres_ref[...], g_ref[...])
        dx_ref[...] = dx
        dg_ref[...] += dg

    row = lambda width: pl.BlockSpec((tm, width), lambda i: (i, 0))
    return pl.pallas_call(
        body, grid=(t // tm,),
        in_specs=[row(nsh * cc), pl.BlockSpec(w_in4.shape, lambda i: (0, 0, 0)), row(d), row(d),
                  pl.BlockSpec(g.shape, lambda i: (0, 0))],
        out_specs=[row(d), pl.BlockSpec((1, d), lambda i: (0, 0))],
        out_shape=[jax.ShapeDtypeStruct((t, d), F32), jax.ShapeDtypeStruct((1, d), F32)], name="in_proj_bwd",
        compiler_params=_params(("arbitrary",)))(dz, w_in4, x, dres, g)


def _heads(v, nh):
    return v.reshape(v.shape[0], nh, HEAD_DIM).transpose(1, 0, 2)


def _tokens(v):
    return v.transpose(1, 0, 2).reshape(v.shape[1], v.shape[0] * HEAD_DIM)


def _row(v):
    return v.reshape(1, -1)


def _layer_fwd(x, p_l, w, s, rope, nbatch, seq, next_shards=None):
    t = x.shape[0]
    tm = 512
    d = D_MODEL
    sv = {}
    sv["x"] = x
    h = _rowwise("rms_mix", _rms_fwd, [R(x), V(_row(s["mix_norm_g"]))], [O(d, BF16)], tm=tm)
    cs = {nm: w[nm].shape[2] for nm in SHARD_MAJOR}
    tb = 1024
    got = {}
    plan = None if next_shards is None else _gather_plan(next_shards, GATHER_A)
    z = _mm("mm_in", h, w["w_in"], "nn", BF16, m=t, n=N_CHIPS * cs["w_in"], k=d, tm=tb, tn=cs["w_in"], tk=d,
            b_sh=cs["w_in"], comm=plan)
    if plan is not None:
        z, outs = z
        got.update(zip(plan["names"], outs))
    sv["h"], sv["z"] = h, z
    c, sa, sb = rope
    qkv_w = Q_WIDTH + 2 * KV_WIDTH
    qkv = _rowwise("rope_fwd", _rope_fwd, [R(z, Q_WIDTH, 0), R(z, KV_WIDTH, 4), R(z, KV_WIDTH, 5), R(c), R(sa), R(sb)],
                   [O(qkv_w, BF16)], tm=tm)
    qkv = _heads(qkv, qkv_w // HEAD_DIM)
    sinks = s["attn_sinks"].reshape(N_Q_HEADS, 1, 1)
    oh, lse = _attn_fwd(qkv, sinks, nbatch, seq)
    o = _tokens(oh)
    ya = _mm("mm_attn_out", o, w["w_attn_out"], "nn", BF16, m=t, n=d, k=Q_WIDTH, tm=tb, tn=d, tk=Q_WIDTH)
    sv.update(qkv=qkv, oh=oh, lse=lse, o=o, ya=ya, sinks=sinks)
    ssm_args = [s[nm] for nm in SSM_NAMES]
    a_re, a_im, wb, wc = _ssm_prep(*ssm_args)
    dskip = _row(s["ssm_d"])
    st, ys, gel = _ssm_fwd(z, wb.astype(BF16), wc.astype(BF16), _ssm_tables(a_re, a_im, False), dskip, nbatch, seq)
    glu = _mm("mm_glu", gel, w["w_ssm_glu"], "nn", BF16, m=t, n=2 * d, k=SSM_WIDTH, tm=tb, tn=2 * d, tk=SSM_WIDTH,
              bias=_row(s["b_ssm_glu"]))
    sv.update(st=st, ys=ys, gel=gel, glu=glu, a=(a_re, a_im), wb=wb, wc=wc, dskip=dskip)
    cw = jnp.pad(s["conv_dw_w"], ((0, CONV_HALO - CONV_K), (0, 0)))
    cv, sc = _conv_fwd(z, cw, _row(s["conv_dw_b"]), _row(s["conv_norm_g"]), _row(s["conv_norm_b"]), nbatch, seq)
    yc = _mm("mm_conv_out", sc, w["w_conv_out"], "nn", BF16, m=t, n=d, k=CONV_WIDTH, tm=tb, tn=d, tk=CONV_WIDTH)
    sv.update(cw=cw, cv=cv, sc=sc, yc=yc)
    bg = _row(s["b_gate"])
    merge_ins = [R(z, 512, 3), R(z, 512, 5), R(z, 512, 7), V(bg, 512, 0), V(bg, 512, 2), V(bg, 512, 4),
                 R(ya, 512, 0), R(glu, 512, 0), R(glu, 512, 2), R(yc, 512, 0)]
    merged = _rowwise("merge_fwd", _merge_fwd, merge_ins, [O(512, BF16, total=d)], tm=tm, ncol=2)
    x1, hf = _mix_out_block(merged, w["w_mix_out"], x, _row(s["ffn_norm_g"]))
    sv.update(merged=merged, x1=x1)
    plan = None if next_shards is None else _gather_plan(next_shards, GATHER_B)
    f = _mm("mm_ffn_in", hf, w["w_ffn_in"], "nn", BF16, m=t, n=2 * FFN_HIDDEN, k=d, tm=tb, tn=cs["w_ffn_in"], tk=d,
            b_sh=cs["w_ffn_in"], comm=plan)
    if plan is not None:
        f, outs = f
        got.update(zip(plan["names"], outs))
    act = (lambda i, j, kk, fg, fu: _ffn_act(fg, fu), [(f, lambda i, j, kk: (i, 0)), (f, lambda i, j, kk: (i, 1))])
    x2, act = _mm("mm_ffn_out", act, w["w_ffn_out"], "nn", F32, m=t, n=d, k=FFN_HIDDEN, tm=256, tn=d, tk=FFN_HIDDEN,
                  res=x1, a_keep=True)
    sv.update(hf=hf, f=f, act=act, x2=x2)
    x3, gp, hp, e = _ple_block(x2, p_l, _row(s["ple_norm_g"]), w["w_ple_in"], w["w_ple_gate"])
    sv.update(e=e, hp=hp, gp=gp, p=p_l)
    return x3, sv, got


def _layer_bwd(dx3, sv, w, s, rope, nbatch, seq):
    t = dx3.shape[0]
    tm = 512
    d = D_MODEL
    gb, gs = {}, {}
    cs = {nm: w[nm].shape[2] for nm in SHARD_MAJOR}
    tb = 1024

    def wg(name, a, b, m, n, tm=1024, tk=1024, shard=None):
        return _mm(name, a, b, "tn", BF16, m=m, n=n, k=t, tm=tm, tn=n if shard is None else cs[shard], tk=tk,
                   o_sh=None if shard is None else cs[shard])

    de, dgp, dx2, gs["ple_norm_g"] = _ple_block_bwd(dx3, sv["gp"], sv["e"], sv["x2"], _row(s["ple_norm_g"]),
                                                    w["w_ple_gate"])
    gb["w_ple_in"] = wg("wg_ple_in", sv["p"], de, sv["p"].shape[1], d, tk=2048)
    gb["w_ple_gate"] = wg("wg_ple_gate", sv["hp"], dgp, d, d, tk=2048)
    fw = FFN_HIDDEN // 2
    dact = _mm("mmb_ffn_out", dx2, w["w_ffn_out"], "nt", BF16, m=t, n=FFN_HIDDEN, k=d, tm=tb, tn=fw, tk=d)
    gb["w_ffn_out"] = wg("wg_ffn_out", sv["act"], dx2, FFN_HIDDEN, d, tm=fw)
    f = sv["f"]

    def df_tile(is_gate, da, fg, fu):
        dfg, dfu = _ffn_act_bwd(da, fg, fu)
        return jnp.where(is_gate, dfg, dfu)

    assert cs["w_ffn_in"] == fw
    df_rows = (lambda i, j, kk, *v: df_tile(kk < 2, *v),
               [(dact, lambda i, j, kk: (i, kk % 2)), (f, lambda i, j, kk: (i, kk % 2)), (f, lambda i, j, kk: (i, 2 + kk % 2))])
    dhf, df = _mm("mmb_ffn_in", df_rows, w["w_ffn_in"], "nt", BF16, m=t, n=d, k=2 * FFN_HIDDEN, tm=512, tn=d, tk=fw,
                  b_sh=fw, a_keep=True)
    gb["w_ffn_in"] = wg("wg_ffn_in", sv["hf"], df, d, 2 * FFN_HIDDEN, shard="w_ffn_in")
    dx1, gs["ffn_norm_g"] = _rowwise("rms_ffn_bwd", _rms_bwd, [R(dhf), R(sv["x1"]), R(dx2), V(_row(s["ffn_norm_g"]))],
                                     [O(d, F32)], [A(d)], tm=tm)
    dm = _mm("mmb_mix", dx1, w["w_mix_out"], "nt", BF16, m=t, n=d, k=d, tm=tb, tn=d, tk=d)
    gb["w_mix_out"] = wg("wg_mix", sv["merged"], dx1, d, d)
    z, glu, bg = sv["z"], sv["glu"], _row(s["b_gate"])
    ins = [R(dm, 512, 0), R(z, 512, 3), R(z, 512, 5), R(z, 512, 7), V(bg, 512, 0), V(bg, 512, 2), V(bg, 512, 4),
           R(sv["ya"], 512, 0), R(glu, 512, 0), R(glu, 512, 2), R(sv["yc"], 512, 0)]
    ob = lambda: O(512, BF16, total=d)
    ab = lambda: A(512, total=d)
    dya, dga, dgb, dyc, d0, d1, d2, db0, db1, db2, dba, dbb = _rowwise(
        "merge_bwd", _merge_bwd, ins, [ob() for _ in range(7)], [ab() for _ in range(5)], tm=tm, ncol=2)
    gs["b_gate"] = jnp.concatenate([db0, db1, db2], axis=1)
    gs["b_ssm_glu"] = jnp.concatenate([dba, dbb], axis=1)
    dglu = jnp.concatenate([dga, dgb], axis=1)
    gb["w_attn_out"] = wg("wg_attn_out", sv["o"], dya, Q_WIDTH, d, tk=2048)
    do = _mm("mmb_attn_out", dya, w["w_attn_out"], "nt", BF16, m=t, n=Q_WIDTH, k=d, tm=tb, tn=Q_WIDTH, tk=d)
    dqkv, dsink = _attn_bwd(sv["qkv"], sv["oh"], _heads(do, N_Q_HEADS), sv["lse"], sv["sinks"], nbatch, seq)
    dqkv = _tokens(dqkv)
    gs["attn_sinks"] = dsink.reshape(-1)
    c, sa, sb = rope
    dq = _rowwise("rope_bwd_q", _rope_bwd_q, [R(dqkv, Q_WIDTH, 0), R(c), R(sa), R(sb)], [O(Q_WIDTH, BF16)], tm=tm)
    dk, dv = _kv_combine(dqkv, c, sa, sb, seq)
    gb["w_ssm_glu"] = wg("wg_ssm_glu", sv["gel"], dglu, SSM_WIDTH, 2 * d, tk=2048)
    dgi = _mm("mmb_glu", dglu, w["w_ssm_glu"], "nt", BF16, m=t, n=SSM_WIDTH, k=2 * d, tm=tb, tn=SSM_WIDTH, tk=2 * d)
    a_re, a_im = sv["a"]
    du, dwb, dwc, dd, da = _ssm_bwd(dgi, sv["ys"], sv["st"], z, sv["wb"].T.astype(BF16), sv["wc"].T.astype(BF16),
                                    _ssm_tables(a_re, a_im, True), sv["dskip"], nbatch, seq)
    gs["ssm_d"] = dd.reshape(-1)
    da = jnp.sum(da, axis=0)
    _, prep_vjp = jax.vjp(_ssm_prep, *[s[nm] for nm in SSM_NAMES])
    for nm, g in zip(SSM_NAMES, prep_vjp((da[:SSM_LANES], da[SSM_LANES:], dwb, dwc))):
        gs[nm] = g
    gb["w_conv_out"] = wg("wg_conv_out", sv["sc"], dyc, CONV_WIDTH, d, tk=2048)
    dsc = _mm("mmb_conv_out", dyc, w["w_conv_out"], "nt", BF16, m=t, n=CONV_WIDTH, k=d, tm=tb, tn=CONV_WIDTH, tk=d)
    dca, dcg, dcw, dcb, dlg, dlb = _conv_bwd(dsc, sv["cv"], z, sv["cw"], _row(s["conv_norm_g"]),
                                             _row(s["conv_norm_b"]), nbatch, seq)
    gs["conv_dw_w"] = dcw[:CONV_K]
    gs["conv_dw_b"], gs["conv_norm_g"], gs["conv_norm_b"] = dcb.reshape(-1), dlg.reshape(-1), dlb.reshape(-1)
    dz = jnp.concatenate([dq, dk, dv, du, dca, dcg, d0, d1, d2], axis=1)
    gb["w_in"] = wg("wg_in", sv["h"], dz, d, dz.shape[1], tk=2048, shard="w_in")
    dx, gs["mix_norm_g"] = _in_proj_bwd(dz, w["w_in"], sv["x"], dx1, _row(s["mix_norm_g"]))
    gs["mix_norm_g"], gs["ffn_norm_g"], gs["ple_norm_g"] = (gs[nm].reshape(-1) for nm in
                                                            ("mix_norm_g", "ffn_norm_g", "ple_norm_g"))
    gs["b_gate"], gs["b_ssm_glu"] = gs["b_gate"].reshape(-1), gs["b_ssm_glu"].reshape(-1)
    return dx, {nm: _shard_major(nm, g) for nm, g in gb.items()}, gs


def _rope_tables(positions):
    inv_freq = ROPE_THETA ** (-jnp.arange(0, ROPE_DIM, 2, dtype=F32) / ROPE_DIM)
    ang = positions.reshape(-1).astype(F32)[:, None] * inv_freq
    cos, sin = jnp.cos(ang), jnp.sin(ang)
    t = ang.shape[0]
    rest = HEAD_DIM - ROPE_DIM
    c = jnp.concatenate([cos, cos, jnp.ones((t, rest), F32)], axis=1)
    sa = jnp.concatenate([-sin, jnp.zeros((t, HEAD_DIM - ROPE_HALF), F32)], axis=1)
    sb = jnp.concatenate([jnp.zeros((t, ROPE_HALF), F32), sin, jnp.zeros((t, rest), F32)], axis=1)
    two = lambda v: jnp.concatenate([v, v], axis=1)
    return two(c), two(sa), two(sb)


def _natural(nm, w4):
    if nm in SHARD_MAJOR:
        return w4
    if BIG_AXIS[nm] == 1:
        return w4.reshape(-1, w4.shape[2])
    return w4.transpose(1, 0, 2).reshape(w4.shape[1], -1)


def _shard_major(nm, g):
    if nm in SHARD_MAJOR:
        return g
    if BIG_AXIS[nm] == 1:
        return g.reshape(N_CHIPS, -1, g.shape[1])
    return g.reshape(g.shape[0], N_CHIPS, -1).transpose(1, 0, 2)


def _untap(taps4, cols):
    flat = taps4.reshape(N_CHIPS, -1)[:, :CONV_K * cols]
    return flat.reshape(N_CHIPS, CONV_K, cols).transpose(1, 0, 2).reshape(CONV_K, N_CHIPS * cols)


def _local_step(x, p, positions, loss_target, small, wfull=None, shards=None):
    nbatch, seq, d = x.shape
    depth = p.shape[0]
    t = nbatch * seq
    rope = _rope_tables(positions)
    xs = x.reshape(t, d)
    saved, ws, ss = [], [], []
    got = None if shards is None else _gather_now((shards, 0))
    for l in range(depth):
        w4 = {nm: wfull[nm][l] for nm in BIG} if shards is None else got
        w_l = {nm: _natural(nm, w4[nm]) for nm in BIG}
        s_l = {nm: small[nm][l] for nm in small if nm != "final_norm_g"}
        if shards is not None:
            s_l["conv_dw_w"] = _untap(got[TAPS], CONV_WIDTH // N_CHIPS)
        nxt = (shards, l + 1) if shards is not None and l + 1 < depth else None
        xs, sv, got = _layer_fwd(xs, p[l].reshape(t, -1), w_l, s_l, rope, nbatch, seq, nxt)
        saved.append(sv)
        ws.append(w_l)
        ss.append(s_l)
    dx, loss_cols, dgf = _rowwise("loss_head", _loss_fn, [R(xs), R(loss_target.reshape(t, d)),
                                                          V(_row(small["final_norm_g"]))],
                                  [O(d, F32)], [A(d), A(d)], tm=512)
    gbs, gss = [None] * depth, [None] * depth
    for l in reversed(range(depth)):
        dx, gbs[l], gss[l] = _layer_bwd(dx, saved[l], ws[l], ss[l], rope, nbatch, seq)
    gbig = {nm: jnp.stack([g[nm] for g in gbs]) for nm in BIG}
    gsmall = {nm: jnp.stack([g[nm] for g in gss]) for nm in SMALL if nm != "final_norm_g"}
    gsmall["final_norm_g"] = dgf.reshape(-1)
    return loss_cols, dx.reshape(nbatch, seq, d), gbig, gsmall


HBM = pl.BlockSpec(memory_space=pltpu.HBM)


def _place():
    x, y, c = lax.axis_index("x"), lax.axis_index("y"), lax.axis_index("c")
    chips = [(1 - x, y), (x, 1 - y), (1 - x, 1 - y)]
    return x, y, c, chips


def _remote(src, dst, send_sem, recv_sem, to):
    return pltpu.make_async_remote_copy(src_ref=src, dst_ref=dst, send_sem=send_sem, recv_sem=recv_sem,
                                        device_id=to, device_id_type=MESH)


TAPS = "taps"
GATHER_ALL = (("w_ffn_in", "w_ffn_out"),
              ("w_in", "w_ple_gate", "w_mix_out", "w_attn_out", "w_ssm_glu", "w_conv_out", "w_ple_in", TAPS))
GATHER_A = (("w_ffn_in",), ("w_in", "w_ple_gate"))
GATHER_B = (("w_ffn_out",), ("w_mix_out", "w_attn_out", "w_ssm_glu", "w_conv_out", "w_ple_in", TAPS))


def _gather_plan(shards, sets):
    stacked, layer = shards
    names = sets[0] + sets[1]
    n = len(names)
    idx = {nm: i for i, nm in enumerate(names)}

    def start(ins, outs, sems):
        send1, recv1, _, _, send0, recv0 = sems
        x, y, c, chips = _place()
        me = 2 * x + y
        for i in range(n):
            _remote(ins[i].at[layer], outs[i].at[me], send0.at[i], recv0.at[i], (x, y, 1 - c)).start()
        for role in (0, 1):
            @pl.when(c == role)
            def _():
                for nm in sets[role]:
                    i = idx[nm]
                    for k, (cx, cy) in enumerate(chips):
                        _remote(ins[i].at[layer], outs[i].at[me], send1.at[i, k], recv1.at[i, k], (cx, cy, c)).start()

    def finish(ins, outs, sems):
        send1, recv1, send2, recv2, send0, recv0 = sems
        x, y, c, chips = _place()
        me = 2 * x + y
        sib = (x, y, 1 - c)
        for role in (0, 1):
            @pl.when(c == role)
            def _():
                passed = []
                for nm in sets[role]:
                    i = idx[nm]
                    for k, (cx, cy) in enumerate(chips):
                        slot = outs[i].at[2 * cx + cy]
                        _remote(slot, slot, send1.at[i, k], recv1.at[i, k], (cx, cy, c)).wait_recv()
                        cp = _remote(slot, slot, send2.at[i, k], recv2.at[i, k], sib)
                        cp.start()
                        passed.append(cp)
                for nm in sets[1 - role]:
                    i = idx[nm]
                    for k, (cx, cy) in enumerate(chips):
                        slot = outs[i].at[2 * cx + cy]
                        _remote(slot, slot, send2.at[i, k], recv2.at[i, k], sib).wait_recv()
                for nm in sets[role]:
                    i = idx[nm]
                    for k, (cx, cy) in enumerate(chips):
                        _remote(ins[i].at[layer], outs[i].at[me], send1.at[i, k], recv1.at[i, k],
                                (cx, cy, c)).wait_send()
                for cp in passed:
                    cp.wait_send()
        for i in range(n):
            _remote(ins[i].at[layer], outs[i].at[me], send0.at[i], recv0.at[i], sib).wait()

    ins = [stacked[nm] for nm in names]
    return dict(names=names, ins=ins, start=start, finish=finish,
                out_shapes=[jax.ShapeDtypeStruct((N_CHIPS,) + v.shape[1:], v.dtype) for v in ins],
                sems=[pltpu.SemaphoreType.DMA((n, 3)) for _ in range(4)] + [pltpu.SemaphoreType.DMA((n,))
                                                                            for _ in range(2)])


def _gather_now(shards):
    return _comm_now("gather_weights", _gather_plan(shards, GATHER_ALL))


def _pair_exchange(grads):
    n = len(grads)
    hl = grads[0].shape[0] // 2

    def body(*refs):
        ins, outs = refs[:n], refs[n:2 * n]
        send, recv = refs[2 * n:]
        x, y, c, _ = _place()
        other = pl.ds((1 - c) * hl, hl)
        cps = [_remote(ins[i].at[other], outs[i], send.at[i], recv.at[i], (x, y, 1 - c)) for i in range(n)]
        for cp in cps:
            cp.start()
        for cp in cps:
            cp.wait()

    out_shape = [jax.ShapeDtypeStruct((hl,) + g.shape[1:], g.dtype) for g in grads]
    sems = [pltpu.SemaphoreType.DMA((n,)) for _ in range(2)]
    return pl.pallas_call(body, out_shape=out_shape, in_specs=[HBM] * n, out_specs=[HBM] * n, scratch_shapes=sems,
                          name="reduce_pair_exchange")(*grads)


def _pair_add(g, r):
    hl, _, rr, cc = r.shape
    rows = hl * N_CHIPS * rr
    nblk = rows // rr

    def body(c_ref, g_ref, r_ref, o_ref):
        o_ref[...] = (g_ref[...].astype(F32) + r_ref[...].astype(F32)).astype(o_ref.dtype)

    grid_spec = pltpu.PrefetchScalarGridSpec(
        num_scalar_prefetch=1, grid=(nblk,),
        in_specs=[pl.BlockSpec((rr, cc), lambda i, c_ref: (c_ref[0] * nblk + i, 0)),
                  pl.BlockSpec((rr, cc), lambda i, c_ref: (i, 0))],
        out_specs=pl.BlockSpec((rr, cc), lambda i, c_ref: (i, 0)))
    c = lax.axis_index("c").astype(jnp.int32).reshape(1)
    out = pl.pallas_call(body, out_shape=jax.ShapeDtypeStruct((rows, cc), r.dtype), grid_spec=grid_spec,
                         name="reduce_pair_add", compiler_params=_params(("parallel",)))(
        c, g.reshape(-1, cc), r.reshape(rows, cc))
    return out.reshape(r.shape)


def _chip_exchange(psums):
    n = len(psums)

    def body(*refs):
        ins, got = refs[:n], refs[n:2 * n]
        send, recv = refs[2 * n:]
        x, y, c, chips = _place()
        cps = [_remote(ins[i].at[:, 2 * cx + cy], got[i].at[k], send.at[i, k], recv.at[i, k], (cx, cy, c))
               for i in range(n) for k, (cx, cy) in enumerate(chips)]
        for cp in cps:
            cp.start()
        for cp in cps:
            cp.wait()

    got_shape = [jax.ShapeDtypeStruct((3, p.shape[0]) + p.shape[2:], p.dtype) for p in psums]
    sems = [pltpu.SemaphoreType.DMA((n, 3)), pltpu.SemaphoreType.DMA((n, 3))]
    return pl.pallas_call(body, out_shape=got_shape, in_specs=[HBM] * n, out_specs=[HBM] * n, scratch_shapes=sems,
                          name="reduce_chip_exchange")(*psums)


def _comm_now(name, plan):
    n = len(plan["ins"])

    def body(*refs):
        ins, outs, sems = refs[:n], refs[n:2 * n], refs[2 * n:]
        plan["start"](ins, outs, sems)
        plan["finish"](ins, outs, sems)

    outs = pl.pallas_call(body, out_shape=plan["out_shapes"], in_specs=[HBM] * n, out_specs=[HBM] * n,
                          scratch_shapes=plan["sems"], name=name)(*plan["ins"])
    return dict(zip(plan["names"], outs))


def _sum4(psum, got):
    hl, _, rr, cc = psum.shape
    tr = rr if rr * cc <= 512 * 1024 else rr // 2

    def body(place_ref, own_ref, g0_ref, g1_ref, g2_ref, o_ref):
        tot = (own_ref[...].astype(F32) + g0_ref[...].astype(F32)) + g1_ref[...].astype(F32)
        o_ref[...] = tot + g2_ref[...].astype(F32)

    def got_spec(k):
        return pl.BlockSpec((None, None, tr, cc), lambda h, i, place: (k, h, i, 0))

    grid_spec = pltpu.PrefetchScalarGridSpec(
        num_scalar_prefetch=1, grid=(hl, rr // tr),
        in_specs=[pl.BlockSpec((None, None, tr, cc), lambda h, i, place: (h, place[0], i, 0)),
                  got_spec(0), got_spec(1), got_spec(2)],
        out_specs=pl.BlockSpec((None, tr, cc), lambda h, i, place: (place[1] * hl + h, i, 0)))
    place = jnp.stack([2 * lax.axis_index("x") + lax.axis_index("y"), lax.axis_index("c")]).astype(jnp.int32)
    return pl.pallas_call(body, out_shape=jax.ShapeDtypeStruct((2 * hl, rr, cc), F32), grid_spec=grid_spec,
                          name="reduce_sum4", compiler_params=_params(("parallel", "parallel")))(
        place, psum, got, got, got)


def _pair_gather(sums):
    n = len(sums)
    hl = sums[0].shape[0] // 2

    def body(*refs):
        bufs = refs[n:2 * n]
        send, recv = refs[2 * n:]
        x, y, c, _ = _place()
        mine = pl.ds(c * hl, hl)
        cps = [_remote(bufs[i].at[mine], bufs[i].at[mine], send.at[i], recv.at[i], (x, y, 1 - c)) for i in range(n)]
        for cp in cps:
            cp.start()
        for cp in cps:
            cp.wait()

    out_shape = [jax.ShapeDtypeStruct(v.shape, v.dtype) for v in sums]
    sems = [pltpu.SemaphoreType.DMA((n,)) for _ in range(2)]
    return pl.pallas_call(body, out_shape=out_shape, in_specs=[HBM] * n, out_specs=[HBM] * n, scratch_shapes=sems,
                          input_output_aliases={i: i for i in range(n)}, name="reduce_pair_gather")(*sums)


def _allreduce_small(vec):
    rows = vec.shape[0]

    def body(v_ref, o_ref, all_ref, send, recv):
        x, y, c, _ = _place()
        me = 4 * x + 2 * y + c
        all_ref[me] = v_ref[...]
        cps = []
        for dlt in range(1, N_DEV):
            fx, fy, fc = (dlt >> 2) & 1, (dlt >> 1) & 1, dlt & 1
            to = (1 - x if fx else x, 1 - y if fy else y, 1 - c if fc else c)
            cps.append(_remote(v_ref, all_ref.at[me], send.at[dlt - 1], recv.at[dlt - 1], to))
        for cp in cps:
            cp.start()
        for cp in cps:
            cp.wait()
        tot = all_ref[0]
        for dev in range(1, N_DEV):
            tot = tot + all_ref[dev]
        o_ref[...] = tot

    vm = pl.BlockSpec(memory_space=pltpu.VMEM)
    return pl.pallas_call(
        body, out_shape=jax.ShapeDtypeStruct(vec.shape, F32), in_specs=[vm], out_specs=vm,
        scratch_shapes=[pltpu.VMEM((N_DEV, rows, 128), F32), pltpu.SemaphoreType.DMA((N_DEV - 1,)),
                        pltpu.SemaphoreType.DMA((N_DEV - 1,))],
        name="allreduce_small", compiler_params=pltpu.CompilerParams(vmem_limit_bytes=VMEM_LIMIT))(vec)


def _adamw(name, w, g, m, v):
    rows, cc = w.shape
    tm = rows if rows * cc <= 512 * 1024 else math.gcd(rows, 256)
    return _rowwise(name, _adamw_fn, [R(w), R(g), R(m), R(v)], [O(cc, F32), O(cc, F32), O(cc, F32)], tm=tm)


def _pack(parts):
    flat = jnp.concatenate([v.reshape(-1).astype(F32) for v in parts])
    pad = (-flat.shape[0]) % (SUBLANES * 128)
    return jnp.pad(flat, (0, pad)).reshape(-1, 128)


def _unpack(packed, shapes):
    flat, out, pos = packed.reshape(-1), [], 0
    for shp in shapes:
        size = math.prod(shp)
        out.append(flat[pos:pos + size].reshape(shp))
        pos += size
    return out


def kernel(x, p, positions, mix_norm_g, w_in, b_gate, attn_sinks, w_attn_out, ssm_lambda_re, ssm_lambda_im, ssm_log_dt, ssm_b_re, ssm_b_im, ssm_c_re, ssm_c_im, ssm_d, w_ssm_glu, b_ssm_glu, conv_dw_w, conv_dw_b, conv_norm_g, conv_norm_b, w_conv_out, w_mix_out, ffn_norm_g, w_ffn_in, w_ffn_out, w_ple_in, ple_norm_g, w_ple_gate, final_norm_g, loss_target, m_mix_norm_g, m_w_in, m_b_gate, m_attn_sinks, m_w_attn_out, m_ssm_lambda_re, m_ssm_lambda_im, m_ssm_log_dt, m_ssm_b_re, m_ssm_b_im, m_ssm_c_re, m_ssm_c_im, m_ssm_d, m_w_ssm_glu, m_b_ssm_glu, m_conv_dw_w, m_conv_dw_b, m_conv_norm_g, m_conv_norm_b, m_w_conv_out, m_w_mix_out, m_ffn_norm_g, m_w_ffn_in, m_w_ffn_out, m_w_ple_in, m_ple_norm_g, m_w_ple_gate, m_final_norm_g, v_mix_norm_g, v_w_in, v_b_gate, v_attn_sinks, v_w_attn_out, v_ssm_lambda_re, v_ssm_lambda_im, v_ssm_log_dt, v_ssm_b_re, v_ssm_b_im, v_ssm_c_re, v_ssm_c_im, v_ssm_d, v_w_ssm_glu, v_b_ssm_glu, v_conv_dw_w, v_conv_dw_b, v_conv_norm_g, v_conv_norm_b, v_w_conv_out, v_w_mix_out, v_ffn_norm_g, v_w_ffn_in, v_w_ffn_out, v_w_ple_in, v_ple_norm_g, v_w_ple_gate, v_final_norm_g):
    given = dict(locals())
    wts = {nm: given[nm] for nm in WEIGHTS}
    mom = {nm: given["m_" + nm] for nm in WEIGHTS}
    var = {nm: given["v_" + nm] for nm in WEIGHTS}
    depth = p.shape[0]
    chip = 2 * lax.axis_index("x") + lax.axis_index("y")

    cw_cols = conv_dw_w.shape[2]
    taps = jnp.pad(conv_dw_w.reshape(depth, -1), ((0, 0), (0, (-CONV_K * cw_cols) % (SUBLANES * 128))))
    shards = {**{nm: wts[nm].astype(BF16) for nm in BIG}, TAPS: taps.reshape(depth, -1, 128)}
    small = {nm: wts[nm] for nm in SMALL if nm != "conv_dw_w"}

    loss_cols, grad_x, gbig, gsmall = _local_step(x, p, positions, loss_target, small, shards=shards)

    parts = [loss_cols] + [gsmall[nm] for nm in SMALL]
    total = _allreduce_small(_pack(parts))
    summed = _unpack(total, [v.shape for v in parts])
    loss = jnp.sum(summed[0])
    gsum = dict(zip(SMALL, summed[1:]))
    gsum["conv_dw_w"] = lax.dynamic_slice_in_dim(gsum["conv_dw_w"], chip * cw_cols, cw_cols, axis=2)
    shapes = [wts[nm].shape for nm in SMALL]
    deltas, new_m, new_v = _adamw("adamw_small", _pack([wts[nm] for nm in SMALL]), _pack([gsum[nm] for nm in SMALL]),
                                  _pack([mom[nm] for nm in SMALL]), _pack([var[nm] for nm in SMALL]))
    grads = dict(gsum)
    delta = dict(zip(SMALL, _unpack(deltas, shapes)))
    newm = dict(zip(SMALL, _unpack(new_m, shapes)))
    newv = dict(zip(SMALL, _unpack(new_v, shapes)))

    gl = [gbig[nm] for nm in BIG]
    sib = _pair_exchange(gl)
    psums = [_pair_add(g, r) for g, r in zip(gl, sib)]
    got = _chip_exchange(psums)
    sums = _pair_gather([_sum4(ps, g) for ps, g in zip(psums, got)])
    for nm, g in zip(BIG, sums):
        shp = wts[nm].shape
        two = lambda v: v.reshape(-1, shp[-1])
        g = g.reshape(shp)
        d_w, n_m, n_v = _adamw("adamw_" + nm, two(wts[nm]), two(g), two(mom[nm]), two(var[nm]))
        grads[nm], delta[nm], newm[nm], newv[nm] = g, d_w.reshape(shp), n_m.reshape(shp), n_v.reshape(shp)

    return (loss, grad_x, *[grads[nm] for nm in WEIGHTS], *[delta[nm] for nm in WEIGHTS],
            *[newm[nm] for nm in WEIGHTS], *[newv[nm] for nm in WEIGHTS])
```

```python
import functools
import math

import jax
import jax.numpy as jnp
from jax import lax
from jax.experimental import pallas as pl
from jax.experimental.pallas import tpu as pltpu

F32 = jnp.float32
BF16 = jnp.bfloat16

D_MODEL = 1024
HEAD_DIM = 64
N_Q_HEADS = 8
N_KV_HEADS = 2
GQA_GROUP = N_Q_HEADS // N_KV_HEADS
ATT_BLOCK = 128
ROPE_THETA = 500000.0
ROPE_DIM = HEAD_DIM // 4
ROPE_HALF = ROPE_DIM // 2
Q_WIDTH = N_Q_HEADS * HEAD_DIM
KV_WIDTH = N_KV_HEADS * HEAD_DIM
SSM_WIDTH = 256
SSM_GROUP = 16
SSM_GROUPS = 16
SSM_STATE = 64
SSM_LANES = SSM_GROUPS * SSM_STATE
CONV_WIDTH = 256
CONV_K = 31
CONV_HALO = 32
FFN_HIDDEN = 2816
EPS = 1e-6
NEG_INF = -1e30
SCALE = HEAD_DIM ** -0.5

ADAM_LR = 0.001
ADAM_B1 = 0.9
ADAM_B2 = 0.999
ADAM_EPS = 1e-08
ADAM_WD = 0.01
ADAM_STEP = 10

N_CHIPS = 4
N_DEV = 8
SUBLANES = 8
VMEM_LIMIT = 56 * 1024 * 1024

MESH = pl.DeviceIdType.MESH


def _params(sem=None):
    return pltpu.CompilerParams(dimension_semantics=sem, vmem_limit_bytes=VMEM_LIMIT)


def R(arr, width=None, cb=0, rb=0):
    return ("r", arr, arr.shape[1] if width is None else width, (cb, rb))


def V(arr, width=None, cb=0):
    return ("v", arr, arr.shape[1] if width is None else width, cb)


def _cbf(cb):
    return cb if callable(cb) else (lambda j, c=cb: c + j)


def _rowwise(name, fn, ins, outs, accs=(), *, tm, ncol=1):
    t = [a for k, a, _, _ in ins if k == "r"][0].shape[0]
    tm = min(tm, t)
    assert t % tm == 0, (name, t, tm)
    n_i, n_o, n_a = len(ins), len(outs), len(accs)

    def body(*refs):
        vals = fn(*[r[...] for r in refs[:n_i]])
        if not isinstance(vals, (tuple, list)):
            vals = (vals,)
        for ref, val in zip(refs[n_i:n_i + n_o], vals[:n_o]):
            ref[...] = val.astype(ref.dtype)
        if n_a:
            acc_refs = refs[n_i + n_o:]

            @pl.when(pl.program_id(1) == 0)
            def _():
                for ref in acc_refs:
                    ref[...] = jnp.zeros_like(ref)

            for ref, val in zip(acc_refs, vals[n_o:]):
                ref[...] += val

    in_specs = []
    for kind, arr, width, cb in ins:
        if kind == "r":
            f = _cbf(cb[0])
            in_specs.append(pl.BlockSpec((tm, width), functools.partial(lambda j, i, f, rb: (i + rb, f(j)), f=f, rb=cb[1])))
        else:
            f = _cbf(cb)
            in_specs.append(pl.BlockSpec((arr.shape[0], width), functools.partial(lambda j, i, f: (0, f(j)), f=f)))
    out_specs, out_shape = [], []
    for total, width, cb, dt in outs:
        f = _cbf(cb)
        out_specs.append(pl.BlockSpec((tm, width), functools.partial(lambda j, i, f: (i, f(j)), f=f)))
        out_shape.append(jax.ShapeDtypeStruct((t, total), dt))
    for total, width, cb in accs:
        f = _cbf(cb)
        out_specs.append(pl.BlockSpec((1, width), functools.partial(lambda j, i, f: (0, f(j)), f=f)))
        out_shape.append(jax.ShapeDtypeStruct((1, total), F32))
    sem = ("arbitrary", "arbitrary") if n_a else ("parallel", "parallel")
    res = pl.pallas_call(body, out_shape=out_shape, grid=(ncol, t // tm), in_specs=in_specs, out_specs=out_specs,
                         name=name, compiler_params=_params(sem))(*[a for _, a, _, _ in ins])
    return res[0] if len(res) == 1 else res


def O(width, dtype, total=None, cb=0):
    return (width if total is None else total, width, cb, dtype)


def A(width, total=None, cb=0):
    return (width if total is None else total, width, cb)


_DIMS = {"nn": (((1,), (0,)), ((), ())), "nt": (((1,), (1,)), ((), ())), "tn": (((0,), (0,)), ((), ()))}


def _mm(name, a, b, mode, out_dtype, *, m, n, k, tm, tn, tk, a_off=0, b_off=0, res=None, bias=None, b_sh=None, o_sh=None,
        comm=None, a_keep=False):
    tm, tn, tk = min(tm, m), min(tn, n), min(tk, k)
    assert m % tm == 0 and n % tn == 0 and k % tk == 0, (name, m, n, k, tm, tn, tk)
    nk = k // tk
    has_res, has_bias = res is not None, bias is not None
    a_fn, a_ops = a if isinstance(a, tuple) else (None, [(a, None)])
    b_fn, b_ops = b if isinstance(b, tuple) else (None, [(b, None)])
    na, nb_ = len(a_ops), len(b_ops)
    a_bytes = sum(m * k * arr.dtype.itemsize for arr, _ in a_ops)
    b_bytes = sum(n * k * arr.dtype.itemsize for arr, _ in b_ops)
    swap = nk == 1 and b_bytes + (n // tn) * a_bytes < a_bytes + (m // tm) * b_bytes
    grid = (n // tn, m // tm, nk) if swap else (m // tm, n // tn, nk)
    ncomm = 0 if comm is None else len(comm["ins"])

    def body(*refs):
        g0, g1, kk = pl.program_id(0), pl.program_id(1), pl.program_id(2)
        gi, gj = (g1, g0) if swap else (g0, g1)
        a_tiles = [r[...] for r in refs[:na]]
        b_tiles = [r[...] for r in refs[na:na + nb_]]
        a_val = a_tiles[0] if a_fn is None else a_fn(gi, gj, kk, *a_tiles)
        b_val = b_tiles[0] if b_fn is None else b_fn(gi, gj, kk, *b_tiles)
        pos = na + nb_
        res_ref = bias_ref = None
        if has_res:
            res_ref = refs[pos]
            pos += 1
        if has_bias:
            bias_ref = refs[pos]
            pos += 1
        comm_ins = refs[pos:pos + ncomm]
        o_ref = refs[pos + ncomm]
        comm_outs = refs[pos + ncomm + 1:pos + 2 * ncomm + 1]
        scratch = refs[pos + 2 * ncomm + 1:]
        if a_keep:
            scratch[0][...] = a_val.astype(BF16)
            scratch = scratch[1:]
        if comm is not None:
            sems = scratch[1:] if nk > 1 else scratch

            @pl.when((g0 == 0) & (g1 == 0) & (kk == 0))
            def _():
                comm["start"](comm_ins, comm_outs, sems)

        def finish(r):
            if has_bias:
                r = r + bias_ref[...]
            if has_res:
                r = r + res_ref[...].astype(F32)
            o_ref[...] = r.astype(o_ref.dtype)

        part = lax.dot_general(a_val.astype(BF16), b_val.astype(BF16), _DIMS[mode], preferred_element_type=F32)
        if nk == 1:
            finish(part)
        else:
            acc_ref = scratch[0]

            @pl.when(kk == 0)
            def _():
                acc_ref[...] = part

            @pl.when(kk > 0)
            def _():
                acc_ref[...] += part

            @pl.when(kk == nk - 1)
            def _():
                finish(acc_ref[...])

        if comm is not None:
            @pl.when((g0 == grid[0] - 1) & (g1 == grid[1] - 1) & (kk == nk - 1))
            def _():
                comm["finish"](comm_ins, comm_outs, sems)

    def at(f):
        return (lambda g0, g1, kk: f(g1, g0, kk)) if swap else f

    if mode == "nn":
        a_spec = pl.BlockSpec((tm, tk), at(lambda i, j, kk: (i, kk + a_off)))
        b_spec = pl.BlockSpec((tk, tn), at(lambda i, j, kk: (kk, j + b_off)))
        if b_sh is not None:
            assert b_sh % tn == 0, (name, b_sh, tn)
            per = b_sh // tn
            b_spec = pl.BlockSpec((None, tk, tn), at(lambda i, j, kk: (j // per, kk, j % per)))
    elif mode == "nt":
        a_spec = pl.BlockSpec((tm, tk), at(lambda i, j, kk: (i, kk + a_off)))
        b_spec = pl.BlockSpec((tn, tk), at(lambda i, j, kk: (j, kk + b_off)))
        if b_sh is not None:
            assert b_sh % tk == 0, (name, b_sh, tk)
            per = b_sh // tk
            b_spec = pl.BlockSpec((None, tn, tk), at(lambda i, j, kk: (kk // per, j, kk % per)))
    else:
        a_spec = pl.BlockSpec((tk, tm), at(lambda i, j, kk: (kk, i + a_off)))
        b_spec = pl.BlockSpec((tk, tn), at(lambda i, j, kk: (kk, j + b_off)))
    a_specs = [a_spec] if a_fn is None else [pl.BlockSpec(a_spec.block_shape, at(f)) for _, f in a_ops]
    b_specs = [b_spec] if b_fn is None else [pl.BlockSpec(b_spec.block_shape, at(f)) for _, f in b_ops]
    in_specs, args = a_specs + b_specs, [arr for arr, _ in a_ops] + [arr for arr, _ in b_ops]
    if has_res:
        in_specs.append(pl.BlockSpec((tm, tn), at(lambda i, j, kk: (i, j))))
        args.append(res)
    if has_bias:
        in_specs.append(pl.BlockSpec((1, tn), at(lambda i, j, kk: (0, j))))
        args.append(bias)
    out_spec, out_shape = pl.BlockSpec((tm, tn), at(lambda i, j, kk: (i, j))), (m, n)
    if o_sh is not None:
        assert o_sh % tn == 0, (name, o_sh, tn)
        per_o = o_sh // tn
        out_spec = pl.BlockSpec((None, tm, tn), at(lambda i, j, kk: (j // per_o, i, j % per_o)))
        out_shape = (n // o_sh, m, o_sh)
    scratch = [pltpu.VMEM((tm, tn), F32)] if nk > 1 else []
    if a_keep:
        assert comm is None and o_sh is None and mode != "tn" and n == tn, name
        outs = pl.pallas_call(
            body, out_shape=[jax.ShapeDtypeStruct(out_shape, out_dtype), jax.ShapeDtypeStruct((m, k), BF16)], grid=grid,
            in_specs=in_specs, out_specs=[out_spec, pl.BlockSpec((tm, tk), at(lambda i, j, kk: (i, kk)))],
            scratch_shapes=scratch, name=name, compiler_params=_params(("parallel", "parallel", "arbitrary")))(*args)
        return outs[0], outs[1]
    if comm is None:
        return pl.pallas_call(
            body, out_shape=jax.ShapeDtypeStruct(out_shape, out_dtype), grid=grid, in_specs=in_specs,
            out_specs=out_spec, scratch_shapes=scratch, name=name,
            compiler_params=_params(("parallel", "parallel", "arbitrary")))(*args)
    outs = pl.pallas_call(
        body, out_shape=[jax.ShapeDtypeStruct(out_shape, out_dtype)] + comm["out_shapes"], grid=grid,
        in_specs=in_specs + [HBM] * ncomm, out_specs=[out_spec] + [HBM] * ncomm,
        scratch_shapes=scratch + comm["sems"], name=name,
        compiler_params=_params(("arbitrary", "arbitrary", "arbitrary")))(*args, *comm["ins"])
    return outs[0], outs[1:]


def _sig(v):
    return jax.nn.sigmoid(v)


def _rms_fwd(x, g):
    r = lax.rsqrt(jnp.mean(x * x, axis=-1, keepdims=True) + EPS)
    return x * r * g


def _rms_bwd(dh, x, dres, g):
    dh = dh.astype(F32)
    r = lax.rsqrt(jnp.mean(x * x, axis=-1, keepdims=True) + EPS)
    xh = x * r
    dxh = dh * g
    dx = r * (dxh - xh * jnp.mean(dxh * xh, axis=-1, keepdims=True))
    return dres + dx, jnp.sum(dh * xh, axis=0, keepdims=True)


def _rope_apply(t, c, sa, sb):
    w = t.shape[1]
    return t * c + pltpu.roll(t, w - ROPE_HALF, 1) * sa + pltpu.roll(t, ROPE_HALF, 1) * sb


def _rope_transpose(g, c, sa, sb):
    w = g.shape[1]
    return g * c + pltpu.roll(g * sa, ROPE_HALF, 1) + pltpu.roll(g * sb, w - ROPE_HALF, 1)


def _tile_lanes(tab, reps):
    return jnp.concatenate([tab] * reps, axis=1) if reps > 1 else tab


def _rope_fwd(q, k, v, c, sa, sb):
    rq = Q_WIDTH // c.shape[1]
    qr = _rope_apply(q.astype(F32), _tile_lanes(c, rq), _tile_lanes(sa, rq), _tile_lanes(sb, rq))
    kr = _rope_apply(k.astype(F32), c, sa, sb)
    return jnp.concatenate([qr, kr, v.astype(F32)], axis=1)


def _rope_bwd_q(g, c, sa, sb):
    rq = Q_WIDTH // c.shape[1]
    return _rope_transpose(g.astype(F32), _tile_lanes(c, rq), _tile_lanes(sa, rq), _tile_lanes(sb, rq))


def _gelu(v):
    return jax.nn.gelu(v, approximate=True)


def _gelu_grad(v):
    c0 = math.sqrt(2.0 / math.pi)
    inner = c0 * (v + 0.044715 * v * v * v)
    th = jnp.tanh(inner)
    return 0.5 * (1.0 + th) + 0.5 * v * (1.0 - th * th) * c0 * (1.0 + 3 * 0.044715 * v * v)


def _merge_fwd(g0, g1, g2, b0, b1, b2, ya, ga, gb, yc):
    s0 = _sig(g0.astype(F32) + b0)
    s1 = _sig(g1.astype(F32) + b1)
    s2 = _sig(g2.astype(F32) + b2)
    ys = ga.astype(F32) * _sig(gb.astype(F32))
    return s0 * ya.astype(F32) + s1 * ys + s2 * yc.astype(F32)


def _merge_bwd(dm, g0, g1, g2, b0, b1, b2, ya, ga, gb, yc):
    dm = dm.astype(F32)
    s0 = _sig(g0.astype(F32) + b0)
    s1 = _sig(g1.astype(F32) + b1)
    s2 = _sig(g2.astype(F32) + b2)
    ga = ga.astype(F32)
    sb = _sig(gb.astype(F32))
    ys = ga * sb
    dya = dm * s0
    dys = dm * s1
    dyc = dm * s2
    dga = dys * sb
    dgb = dys * ga * sb * (1.0 - sb)
    d0 = dm * ya.astype(F32) * s0 * (1.0 - s0)
    d1 = dm * ys * s1 * (1.0 - s1)
    d2 = dm * yc.astype(F32) * s2 * (1.0 - s2)
    cs = lambda v: jnp.sum(v, axis=0, keepdims=True)
    return dya, dga, dgb, dyc, d0, d1, d2, cs(d0), cs(d1), cs(d2), cs(dga), cs(dgb)


def _ffn_act(fg, fu):
    fg = fg.astype(F32)
    return fg * _sig(fg) * fu.astype(F32)


def _ffn_act_bwd(da, fg, fu):
    da, fg, fu = da.astype(F32), fg.astype(F32), fu.astype(F32)
    s = _sig(fg)
    return da * fu * (s * (1.0 + fg * (1.0 - s))), da * fg * s


def _ple_fwd(x, gp, e):
    return x + _sig(gp.astype(F32)) * e.astype(F32)


def _ple_bwd(dx, gp, e):
    s = _sig(gp.astype(F32))
    e = e.astype(F32)
    return dx * s, dx * e * s * (1.0 - s)


def _loss_fn(x, tgt, g):
    d = x.shape[1]
    r = lax.rsqrt(jnp.mean(x * x, axis=-1, keepdims=True) + EPS)
    xh = x * r
    err = xh * g - tgt
    dy = err * (1.0 / d)
    dxh = dy * g
    dx = r * (dxh - xh * jnp.mean(dxh * xh, axis=-1, keepdims=True))
    return dx, jnp.sum(err * err, axis=0, keepdims=True) * (0.5 / d), jnp.sum(dy * xh, axis=0, keepdims=True)


def _adamw_fn(w, g, m, v):
    m = ADAM_B1 * m + (1.0 - ADAM_B1) * g
    v = ADAM_B2 * v + (1.0 - ADAM_B2) * (g * g)
    m_hat = m / (1.0 - ADAM_B1 ** ADAM_STEP)
    v_hat = v / (1.0 - ADAM_B2 ** ADAM_STEP)
    delta = -ADAM_LR * (m_hat / (jnp.sqrt(v_hat) + ADAM_EPS) + ADAM_WD * w)
    return delta, m, v


def _band_mask(n):
    qi = lax.broadcasted_iota(jnp.int32, (ATT_BLOCK, 2 * ATT_BLOCK), 0)
    kj = lax.broadcasted_iota(jnp.int32, (ATT_BLOCK, 2 * ATT_BLOCK), 1)
    dist = qi + ATT_BLOCK - kj
    return (dist >= 0) & (dist < ATT_BLOCK) & ((n > 0) | (kj >= ATT_BLOCK))


K_HEADS_AT = N_Q_HEADS // N_KV_HEADS


def _att_specs(nb):
    qs = pl.BlockSpec((N_Q_HEADS, ATT_BLOCK, HEAD_DIM), lambda b, n: (0, b * nb + n, 0))

    def kv(head_block, back):
        return pl.BlockSpec((N_KV_HEADS, ATT_BLOCK, HEAD_DIM),
                            lambda b, n: (head_block, b * nb + jnp.maximum(n - back, 0), 0))

    stat = pl.BlockSpec((N_Q_HEADS, ATT_BLOCK, 1), lambda b, n: (0, b * nb + n, 0))
    sink = pl.BlockSpec((N_Q_HEADS, 1, 1), lambda b, n: (0, 0, 0))
    return qs, [kv(K_HEADS_AT, 1), kv(K_HEADS_AT, 0), kv(K_HEADS_AT + 1, 1), kv(K_HEADS_AT + 1, 0)], stat, sink


def _attn_fwd(qkv, sinks, nbatch, seq):
    t = qkv.shape[1]
    nb = seq // ATT_BLOCK
    qs, kvs, stat, sink = _att_specs(nb)

    def body(q_ref, kp_ref, kc_ref, vp_ref, vc_ref, sink_ref, o_ref, lse_ref):
        mask = _band_mask(pl.program_id(1))
        rows = GQA_GROUP * ATT_BLOCK
        for kv in range(N_KV_HEADS):
            hs = slice(kv * GQA_GROUP, (kv + 1) * GQA_GROUP)
            kk = jnp.concatenate([kp_ref[kv], kc_ref[kv]], axis=0)
            vv = jnp.concatenate([vp_ref[kv], vc_ref[kv]], axis=0)
            q4 = (q_ref[hs] * SCALE).reshape(rows, HEAD_DIM)
            s = lax.dot_general(q4, kk, _DIMS["nt"], preferred_element_type=F32)
            s = jnp.where(mask, s.reshape(GQA_GROUP, ATT_BLOCK, 2 * ATT_BLOCK), NEG_INF)
            sk = sink_ref[hs]
            mx = jnp.maximum(jnp.max(s, axis=-1, keepdims=True), sk)
            p = jnp.exp(s - mx)
            den = jnp.sum(p, axis=-1, keepdims=True) + jnp.exp(sk - mx)
            o = lax.dot_general(p.reshape(rows, 2 * ATT_BLOCK).astype(BF16), vv, _DIMS["nn"],
                                preferred_element_type=F32).reshape(GQA_GROUP, ATT_BLOCK, HEAD_DIM)
            o_ref[hs] = (o * (1.0 / den)).astype(o_ref.dtype)
            lse_ref[hs] = mx + jnp.log(den)

    return pl.pallas_call(
        body, grid=(nbatch, nb), in_specs=[qs] + kvs + [sink], out_specs=[qs, stat],
        out_shape=[jax.ShapeDtypeStruct((N_Q_HEADS, t, HEAD_DIM), BF16), jax.ShapeDtypeStruct((N_Q_HEADS, t, 1), F32)],
        name="attn_fwd", compiler_params=_params(("parallel", "parallel")))(qkv, qkv, qkv, qkv, qkv, sinks)


def _attn_bwd(qkv, oh, doh, lse, sinks, nbatch, seq):
    t = qkv.shape[1]
    nb = seq // ATT_BLOCK
    qs, kvs, stat, sink = _att_specs(nb)

    def body(q_ref, kp_ref, kc_ref, vp_ref, vc_ref, o_ref, do_ref, lse_ref, sink_ref, dqkv_ref, dsink_ref):
        dq_ref = dqkv_ref.at[pl.ds(0, N_Q_HEADS)]
        dkc_ref, dvc_ref, dkp_ref, dvp_ref = (dqkv_ref.at[pl.ds(N_Q_HEADS + N_KV_HEADS * i, N_KV_HEADS)]
                                              for i in range(4))
        first = (pl.program_id(0) == 0) & (pl.program_id(1) == 0)

        @pl.when(first)
        def _():
            dsink_ref[...] = jnp.zeros_like(dsink_ref)

        mask = _band_mask(pl.program_id(1))
        rows = GQA_GROUP * ATT_BLOCK
        band = (GQA_GROUP, ATT_BLOCK, 2 * ATT_BLOCK)
        for kv in range(N_KV_HEADS):
            hs = slice(kv * GQA_GROUP, (kv + 1) * GQA_GROUP)
            kk = jnp.concatenate([kp_ref[kv], kc_ref[kv]], axis=0)
            vv = jnp.concatenate([vp_ref[kv], vc_ref[kv]], axis=0)
            q4 = q_ref[hs].reshape(rows, HEAD_DIM)
            do4 = do_ref[hs].reshape(rows, HEAD_DIM)
            lse4 = lse_ref[hs]
            s = lax.dot_general(q4 * SCALE, kk, _DIMS["nt"], preferred_element_type=F32).reshape(band)
            p = jnp.where(mask, jnp.exp(s - lse4), 0.0)
            dd = jnp.sum(do_ref[hs].astype(F32) * o_ref[hs].astype(F32), axis=-1, keepdims=True)
            dp = lax.dot_general(do4, vv, _DIMS["nt"], preferred_element_type=F32).reshape(band)
            ds = (p * (dp - dd) * SCALE).astype(BF16).reshape(rows, 2 * ATT_BLOCK)
            dq = lax.dot_general(ds, kk, _DIMS["nn"], preferred_element_type=F32)
            dq_ref[hs] = dq.reshape(GQA_GROUP, ATT_BLOCK, HEAD_DIM).astype(dq_ref.dtype)
            dk = lax.dot_general(ds, q4, _DIMS["tn"], preferred_element_type=F32)
            dv = lax.dot_general(p.astype(BF16).reshape(rows, 2 * ATT_BLOCK), do4, _DIMS["tn"],
                                 preferred_element_type=F32)
            dsink_ref[hs] += -jnp.sum(jnp.exp(sink_ref[hs] - lse4) * dd, axis=1, keepdims=True)
            dkp_ref[kv] = dk[:ATT_BLOCK]
            dkc_ref[kv] = dk[ATT_BLOCK:]
            dvp_ref[kv] = dv[:ATT_BLOCK]
            dvc_ref[kv] = dv[ATT_BLOCK:]

    n_out = 2 * N_Q_HEADS
    return pl.pallas_call(
        body, grid=(nbatch, nb), in_specs=[qs] + kvs + [qs, qs, stat, sink],
        out_specs=[pl.BlockSpec((n_out, ATT_BLOCK, HEAD_DIM), lambda b, n: (0, b * nb + n, 0)), sink],
        out_shape=[jax.ShapeDtypeStruct((n_out, t, HEAD_DIM), F32), jax.ShapeDtypeStruct((N_Q_HEADS, 1, 1), F32)],
        name="attn_bwd", compiler_params=_params(("arbitrary", "arbitrary")))(
        qkv, qkv, qkv, qkv, qkv, oh, doh, lse, sinks)


def _kv_combine(dqkv, c, sa, sb, seq):
    t = dqkv.shape[0]
    nb = seq // ATT_BLOCK
    nblk = t // ATT_BLOCK
    col0 = Q_WIDTH // KV_WIDTH

    def body(kc_ref, kp_ref, vc_ref, vp_ref, c_ref, sa_ref, sb_ref, dk_ref, dv_ref):
        has_next = (pl.program_id(0) % nb) != nb - 1
        dk = kc_ref[...] + jnp.where(has_next, kp_ref[...], 0.0)
        dv = vc_ref[...] + jnp.where(has_next, vp_ref[...], 0.0)
        dk_ref[...] = _rope_transpose(dk, c_ref[...], sa_ref[...], sb_ref[...]).astype(dk_ref.dtype)
        dv_ref[...] = dv.astype(dv_ref.dtype)

    cur = pl.BlockSpec((ATT_BLOCK, KV_WIDTH), lambda i: (i, 0))
    own = lambda col: pl.BlockSpec((ATT_BLOCK, KV_WIDTH), lambda i: (i, col0 + col))
    nxt = lambda col: pl.BlockSpec((ATT_BLOCK, KV_WIDTH), lambda i: (jnp.minimum(i + 1, nblk - 1), col0 + col))
    o = jax.ShapeDtypeStruct((t, KV_WIDTH), BF16)
    return pl.pallas_call(body, grid=(nblk,), in_specs=[own(0), nxt(2), own(1), nxt(3), cur, cur, cur],
                          out_specs=[cur, cur], out_shape=[o, o], name="kv_combine",
                          compiler_params=_params(("parallel",)))(dqkv, dqkv, dqkv, dqkv, c, sa, sb)


def _scan_block(ref, tab_ref, carry, ngroups, reverse):
    shifts = (7, 6, 4) if reverse else (1, 2, 4)
    n = SSM_LANES

    def step(i, car):
        g = (ngroups - 1 - i) if reverse else i
        r0 = pl.multiple_of(g * SUBLANES, SUBLANES)
        xr = ref[pl.ds(r0, SUBLANES), :n]
        xi = ref[pl.ds(r0, SUBLANES), n:]
        for s, sh in enumerate(shifts):
            pr, pi = tab_ref[2 * s], tab_ref[2 * s + 1]
            yr, yi = pltpu.roll(xr, sh, 0), pltpu.roll(xi, sh, 0)
            xr, xi = xr + pr * yr - pi * yi, xi + pr * yi + pi * yr
        cr, ci = car
        qr, qi = tab_ref[6], tab_ref[7]
        xr, xi = xr + qr * cr - qi * ci, xi + qr * ci + qi * cr
        ref[pl.ds(r0, SUBLANES), :n] = xr
        ref[pl.ds(r0, SUBLANES), n:] = xi
        last = r0 if reverse else r0 + SUBLANES - 1
        return ref[pl.ds(last, 1), :n], ref[pl.ds(last, 1), n:]

    return lax.fori_loop(0, ngroups, step, carry, unroll=2)


def _ssm_chunk(seq):
    return min(512, seq)


def _ssm_fwd(z, wb, wc, tab, dskip, wglu, bglu, nbatch, seq):
    t = z.shape[0]
    tc = _ssm_chunk(seq)
    nc = seq // tc
    n2 = 2 * SSM_LANES
    nglu = wglu.shape[1]

    def body(u_ref, wb_ref, wc_ref, tab_ref, d_ref, wg_ref, bg_ref, st_ref, y_ref, gel_ref, glu_ref, car_ref):
        @pl.when(pl.program_id(1) == 0)
        def _():
            car_ref[...] = jnp.zeros_like(car_ref)

        u = u_ref[...]
        st_ref[...] = lax.dot_general(u, wb_ref[...], _DIMS["nn"], preferred_element_type=F32)
        cr, ci = _scan_block(st_ref, tab_ref, (car_ref[:, :SSM_LANES], car_ref[:, SSM_LANES:]), tc // SUBLANES, False)
        car_ref[:, :SSM_LANES] = cr
        car_ref[:, SSM_LANES:] = ci
        y = lax.dot_general(st_ref[...].astype(BF16), wc_ref[...], _DIMS["nn"], preferred_element_type=F32)
        y = y + d_ref[...] * u.astype(F32)
        y_ref[...] = y
        gel = _gelu(y).astype(BF16)
        gel_ref[...] = gel
        glu = lax.dot_general(gel, wg_ref[...], _DIMS["nn"], preferred_element_type=F32) + bg_ref[...]
        glu_ref[...] = glu.astype(glu_ref.dtype)

    row = lambda b, c: (b * nc + c, 0)
    full = lambda b, c: (0, 0)
    return pl.pallas_call(
        body, grid=(nbatch, nc),
        in_specs=[pl.BlockSpec((tc, SSM_WIDTH), lambda b, c: (b * nc + c, 3)), pl.BlockSpec((SSM_WIDTH, n2), full),
                  pl.BlockSpec((n2, SSM_WIDTH), full), pl.BlockSpec((8, SUBLANES, SSM_LANES), lambda b, c: (0, 0, 0)),
                  pl.BlockSpec((1, SSM_WIDTH), full), pl.BlockSpec((SSM_WIDTH, nglu), full),
                  pl.BlockSpec((1, nglu), full)],
        out_specs=[pl.BlockSpec((tc, n2), row), pl.BlockSpec((tc, SSM_WIDTH), row), pl.BlockSpec((tc, SSM_WIDTH), row),
                   pl.BlockSpec((tc, nglu), row)],
        out_shape=[jax.ShapeDtypeStruct((t, n2), F32), jax.ShapeDtypeStruct((t, SSM_WIDTH), F32),
                   jax.ShapeDtypeStruct((t, SSM_WIDTH), BF16), jax.ShapeDtypeStruct((t, nglu), BF16)],
        scratch_shapes=[pltpu.VMEM((1, n2), F32)], name="ssm_fwd",
        compiler_params=_params(("arbitrary", "arbitrary")))(z, wb, wc, tab, dskip, wglu, bglu)


def _ssm_bwd(dgi, ys, st, z, wbt, wct, tab_rev, dskip, nbatch, seq):
    t = z.shape[0]
    tc = _ssm_chunk(seq)
    nc = seq // tc
    n = SSM_LANES
    n2 = 2 * n
    ng = tc // SUBLANES

    def body(dgi_ref, ys_ref, st_ref, stp_ref, u_ref, wbt_ref, wct_ref, tab_ref, d_ref,
             du_ref, dwb_ref, dwc_ref, dd_ref, da_ref, p_ref, sb_ref, car_ref):
        b, c = pl.program_id(0), pl.program_id(1)
        ct = nc - 1 - c

        @pl.when((b == 0) & (c == 0))
        def _():
            dwb_ref[...] = jnp.zeros_like(dwb_ref)
            dwc_ref[...] = jnp.zeros_like(dwc_ref)
            dd_ref[...] = jnp.zeros_like(dd_ref)
            da_ref[...] = jnp.zeros_like(da_ref)

        @pl.when(c == 0)
        def _():
            car_ref[...] = jnp.zeros_like(car_ref)

        u = u_ref[...]
        dys = dgi_ref[...].astype(F32) * _gelu_grad(ys_ref[...])
        dys_b = dys.astype(BF16)
        st = st_ref[...]
        dd_ref[...] += jnp.sum(dys * u.astype(F32), axis=0, keepdims=True)
        dwc_ref[...] += lax.dot_general(st.astype(BF16), dys_b, _DIMS["tn"], preferred_element_type=F32)
        p_ref[...] = lax.dot_general(dys_b, wct_ref[...], _DIMS["nn"], preferred_element_type=F32)
        cr, ci = _scan_block(p_ref, tab_ref, (car_ref[:, :n], car_ref[:, n:]), ng, True)
        car_ref[:, :n] = cr
        car_ref[:, n:] = ci
        p = p_ref[...]
        pb = p.astype(BF16)
        dwb_ref[...] += lax.dot_general(u, pb, _DIMS["tn"], preferred_element_type=F32)
        du = lax.dot_general(pb, wbt_ref[...], _DIMS["nn"], preferred_element_type=F32) + d_ref[...] * dys
        du_ref[...] = du.astype(du_ref.dtype)
        sb_ref[pl.ds(0, SUBLANES), :] = jnp.where(ct > 0, stp_ref[...], 0.0)
        sb_ref[pl.ds(SUBLANES, tc), :] = st
        row0 = lax.broadcasted_iota(jnp.int32, (SUBLANES, n), 0) == 0

        def acc_step(g, acc):
            ar, ai = acc
            r0 = pl.multiple_of(g * SUBLANES, SUBLANES)
            edge_r = sb_ref[pl.ds(r0 + SUBLANES - 1, 1), :n]
            edge_i = sb_ref[pl.ds(r0 + SUBLANES - 1, 1), n:]
            sr = jnp.where(row0, edge_r, pltpu.roll(sb_ref[pl.ds(r0 + SUBLANES, SUBLANES), :n], 1, 0))
            si = jnp.where(row0, edge_i, pltpu.roll(sb_ref[pl.ds(r0 + SUBLANES, SUBLANES), n:], 1, 0))
            pr = p_ref[pl.ds(r0, SUBLANES), :n]
            pi = p_ref[pl.ds(r0, SUBLANES), n:]
            return ar + pr * sr + pi * si, ai + pi * sr - pr * si

        zero = jnp.zeros((SUBLANES, n), F32)
        ar, ai = lax.fori_loop(0, ng, acc_step, (zero, zero), unroll=2)
        da_ref[:, :n] += ar
        da_ref[:, n:] += ai

    row = lambda b, c: (b * nc + (nc - 1 - c), 0)
    prev8 = lambda b, c: (jnp.maximum((b * nc + (nc - 1 - c)) * (tc // SUBLANES) - 1, 0), 0)
    full = lambda b, c: (0, 0)
    return pl.pallas_call(
        body, grid=(nbatch, nc),
        in_specs=[pl.BlockSpec((tc, SSM_WIDTH), row), pl.BlockSpec((tc, SSM_WIDTH), row), pl.BlockSpec((tc, n2), row),
                  pl.BlockSpec((SUBLANES, n2), prev8),
                  pl.BlockSpec((tc, SSM_WIDTH), lambda b, c: (b * nc + (nc - 1 - c), 3)),
                  pl.BlockSpec((n2, SSM_WIDTH), full), pl.BlockSpec((SSM_WIDTH, n2), full),
                  pl.BlockSpec((8, SUBLANES, n), lambda b, c: (0, 0, 0)), pl.BlockSpec((1, SSM_WIDTH), full)],
        out_specs=[pl.BlockSpec((tc, SSM_WIDTH), row), pl.BlockSpec((SSM_WIDTH, n2), full),
                   pl.BlockSpec((n2, SSM_WIDTH), full), pl.BlockSpec((1, SSM_WIDTH), full),
                   pl.BlockSpec((SUBLANES, n2), full)],
        out_shape=[jax.ShapeDtypeStruct((t, SSM_WIDTH), BF16), jax.ShapeDtypeStruct((SSM_WIDTH, n2), F32),
                   jax.ShapeDtypeStruct((n2, SSM_WIDTH), F32), jax.ShapeDtypeStruct((1, SSM_WIDTH), F32),
                   jax.ShapeDtypeStruct((SUBLANES, n2), F32)],
        scratch_shapes=[pltpu.VMEM((tc, n2), F32), pltpu.VMEM((tc + SUBLANES, n2), F32), pltpu.VMEM((1, n2), F32)],
        name="ssm_bwd", compiler_params=_params(("arbitrary", "arbitrary")))(
        dgi, ys, st, st, z, wbt, wct, tab_rev, dskip)


def _ssm_prep(lam_re, lam_im, log_dt, b_re, b_im, c_re, c_im):
    lr = jnp.minimum(lam_re, -1e-4)
    li = lam_im
    dt = jnp.exp(log_dt)[:, None]
    mag = jnp.exp(lr * dt)
    a_re = mag * jnp.cos(li * dt)
    a_im = mag * jnp.sin(li * dt)
    den = lr * lr + li * li
    x_re, x_im = a_re - 1.0, a_im
    f_re = (x_re * lr + x_im * li) / den
    f_im = (x_im * lr - x_re * li) / den
    bb_re = f_re[..., None] * b_re - f_im[..., None] * b_im
    bb_im = f_re[..., None] * b_im + f_im[..., None] * b_re
    eye = jnp.eye(SSM_GROUPS, dtype=F32)
    emb_b = lambda v: jnp.einsum("gnh,gk->ghkn", v, eye).reshape(SSM_WIDTH, SSM_LANES)
    emb_c = lambda v: jnp.einsum("ghn,gk->gnkh", v, eye).reshape(SSM_LANES, SSM_WIDTH)
    wb = jnp.concatenate([emb_b(bb_re), emb_b(bb_im)], axis=1)
    wc = jnp.concatenate([emb_c(c_re), -emb_c(c_im)], axis=0)
    return a_re.reshape(-1), a_im.reshape(-1), wb, wc


def _ssm_tables(a_re, a_im, reverse):
    if reverse:
        a_im = -a_im
    pw = [(a_re, a_im)]
    for _ in range(SUBLANES - 1):
        pr, pi = pw[-1]
        pw.append((pr * a_re - pi * a_im, pr * a_im + pi * a_re))
    rows = jnp.arange(SUBLANES)[:, None]
    tabs = []
    for k in (1, 2, 4):
        ok = (rows + k <= SUBLANES - 1) if reverse else (rows >= k)
        tabs += [jnp.where(ok, pw[k - 1][0][None], 0.0), jnp.where(ok, pw[k - 1][1][None], 0.0)]
    order = list(range(SUBLANES - 1, -1, -1)) if reverse else list(range(SUBLANES))
    tabs += [jnp.stack([pw[i][0] for i in order]), jnp.stack([pw[i][1] for i in order])]
    return jnp.stack(tabs)


def _conv_chunk(seq):
    return min(512, seq)


def _shifted(buf, sh, tc, offsets):
    for b in range(SUBLANES):
        idx = [i for i, o in enumerate(offsets) if o % SUBLANES == b]
        if not idx:
            continue
        src = buf
        if b:
            span = tc + SUBLANES * max(offsets[i] // SUBLANES for i in idx)
            sh[pl.ds(0, span), :] = buf[pl.ds(b, span), :]
            src = sh
        for i in idx:
            yield i, src[pl.ds(offsets[i] // SUBLANES * SUBLANES, tc), :]


def _conv_fwd(z, w, bias, lg, lb, w_out, nbatch, seq):
    t = z.shape[0]
    tc = _conv_chunk(seq)
    nc = seq // tc
    nout = w_out.shape[1]

    def body(a_ref, g_ref, w_ref, b_ref, lg_ref, lb_ref, wo_ref, cv_ref, sc_ref, yc_ref, ubuf, sh):
        c = pl.program_id(1)

        @pl.when(c == 0)
        def _():
            ubuf[pl.ds(0, CONV_HALO), :] = jnp.zeros((CONV_HALO, CONV_WIDTH), F32)

        @pl.when(c > 0)
        def _():
            ubuf[pl.ds(0, CONV_HALO), :] = ubuf[pl.ds(tc, CONV_HALO), :]

        ubuf[pl.ds(CONV_HALO, tc), :] = a_ref[...].astype(F32) * _sig(g_ref[...].astype(F32))
        acc = jnp.zeros((tc, CONV_WIDTH), F32) + b_ref[...]
        for k, win in _shifted(ubuf, sh, tc, [CONV_HALO - (CONV_K - 1) + k for k in range(CONV_K)]):
            acc = acc + w_ref[pl.ds(k, 1), :] * win
        cv_ref[...] = acc
        mu = jnp.mean(acc, axis=-1, keepdims=True)
        xc = acc - mu
        y = xc * lax.rsqrt(jnp.mean(xc * xc, axis=-1, keepdims=True) + EPS) * lg_ref[...] + lb_ref[...]
        sc = (y * _sig(y)).astype(BF16)
        sc_ref[...] = sc
        yc_ref[...] = lax.dot_general(sc, wo_ref[...], _DIMS["nn"], preferred_element_type=F32).astype(yc_ref.dtype)

    row = lambda b, c: (b * nc + c, 0)
    full = lambda b, c: (0, 0)
    vec = pl.BlockSpec((1, CONV_WIDTH), full)
    return pl.pallas_call(
        body, grid=(nbatch, nc),
        in_specs=[pl.BlockSpec((tc, CONV_WIDTH), lambda b, c: (b * nc + c, 4)),
                  pl.BlockSpec((tc, CONV_WIDTH), lambda b, c: (b * nc + c, 5)),
                  pl.BlockSpec((CONV_HALO, CONV_WIDTH), full), vec, vec, vec, pl.BlockSpec((CONV_WIDTH, nout), full)],
        out_specs=[pl.BlockSpec((tc, CONV_WIDTH), row), pl.BlockSpec((tc, CONV_WIDTH), row),
                   pl.BlockSpec((tc, nout), row)],
        out_shape=[jax.ShapeDtypeStruct((t, CONV_WIDTH), F32), jax.ShapeDtypeStruct((t, CONV_WIDTH), BF16),
                   jax.ShapeDtypeStruct((t, nout), BF16)],
        scratch_shapes=[pltpu.VMEM((CONV_HALO + tc, CONV_WIDTH), F32)] * 2, name="conv_fwd",
        compiler_params=_params(("arbitrary", "arbitrary")))(z, z, w, bias, lg, lb, w_out)


def _conv_bwd(dsc, cv, z, w, lg, lb, nbatch, seq):
    t = z.shape[0]
    tc = _conv_chunk(seq)
    nc = seq // tc
    hb = tc // CONV_HALO

    def body(dsc_ref, cv_ref, a_ref, g_ref, ap_ref, gp_ref, w_ref, lg_ref, lb_ref,
             da_ref, dg_ref, dw_ref, db_ref, dlg_ref, dlb_ref, ubuf, dbuf, sh):
        b, c = pl.program_id(0), pl.program_id(1)
        ct = nc - 1 - c

        @pl.when((b == 0) & (c == 0))
        def _():
            dw_ref[...] = jnp.zeros_like(dw_ref)
            db_ref[...] = jnp.zeros_like(db_ref)
            dlg_ref[...] = jnp.zeros_like(dlg_ref)
            dlb_ref[...] = jnp.zeros_like(dlb_ref)

        cvv = cv_ref[...]
        mu = jnp.mean(cvv, axis=-1, keepdims=True)
        xc = cvv - mu
        rstd = lax.rsqrt(jnp.mean(xc * xc, axis=-1, keepdims=True) + EPS)
        xh = xc * rstd
        y = xh * lg_ref[...] + lb_ref[...]
        sy = _sig(y)
        dy = dsc_ref[...].astype(F32) * (sy * (1.0 + y * (1.0 - sy)))
        dlg_ref[...] += jnp.sum(dy * xh, axis=0, keepdims=True)
        dlb_ref[...] += jnp.sum(dy, axis=0, keepdims=True)
        dxh = dy * lg_ref[...]
        dcv = rstd * (dxh - jnp.mean(dxh, axis=-1, keepdims=True) - xh * jnp.mean(dxh * xh, axis=-1, keepdims=True))
        db_ref[...] += jnp.sum(dcv, axis=0, keepdims=True)

        @pl.when(c == 0)
        def _():
            dbuf[pl.ds(tc, CONV_HALO), :] = jnp.zeros((CONV_HALO, CONV_WIDTH), F32)

        @pl.when(c > 0)
        def _():
            dbuf[pl.ds(tc, CONV_HALO), :] = dbuf[pl.ds(0, CONV_HALO), :]

        dbuf[pl.ds(0, tc), :] = dcv
        a = a_ref[...].astype(F32)
        sg = _sig(g_ref[...].astype(F32))
        ubuf[pl.ds(0, CONV_HALO), :] = jnp.where(ct > 0, ap_ref[...].astype(F32) * _sig(gp_ref[...].astype(F32)), 0.0)
        ubuf[pl.ds(CONV_HALO, tc), :] = a * sg
        du = jnp.zeros((tc, CONV_WIDTH), F32)
        for k, win in _shifted(dbuf, sh, tc, [CONV_K - 1 - k for k in range(CONV_K)]):
            du = du + w_ref[pl.ds(k, 1), :] * win
        for k, win in _shifted(ubuf, sh, tc, [CONV_HALO - (CONV_K - 1) + k for k in range(CONV_K)]):
            dw_ref[pl.ds(k, 1), :] += jnp.sum(dcv * win, axis=0, keepdims=True)
        da_ref[...] = (du * sg).astype(da_ref.dtype)
        dg_ref[...] = (du * a * sg * (1.0 - sg)).astype(dg_ref.dtype)

    row = lambda b, c: (b * nc + (nc - 1 - c), 0)
    full = lambda b, c: (0, 0)
    vec = pl.BlockSpec((1, CONV_WIDTH), full)
    blk = pl.BlockSpec((tc, CONV_WIDTH), row)

    def zcol(col):
        return pl.BlockSpec((tc, CONV_WIDTH), lambda b, c: (b * nc + (nc - 1 - c), col))

    def zprev(col):
        return pl.BlockSpec((CONV_HALO, CONV_WIDTH),
                            lambda b, c: (jnp.maximum((b * nc + (nc - 1 - c)) * hb - 1, 0), col))

    o = jax.ShapeDtypeStruct((t, CONV_WIDTH), BF16)
    v = jax.ShapeDtypeStruct((1, CONV_WIDTH), F32)
    return pl.pallas_call(
        body, grid=(nbatch, nc),
        in_specs=[blk, blk, zcol(4), zcol(5), zprev(4), zprev(5), pl.BlockSpec((CONV_HALO, CONV_WIDTH), full), vec, vec],
        out_specs=[blk, blk, pl.BlockSpec((CONV_HALO, CONV_WIDTH), full), vec, vec, vec],
        out_shape=[o, o, jax.ShapeDtypeStruct((CONV_HALO, CONV_WIDTH), F32), v, v, v],
        scratch_shapes=[pltpu.VMEM((CONV_HALO + tc, CONV_WIDTH), F32)] * 3, name="conv_bwd", compiler_params=_params(("arbitrary", "arbitrary")))(dsc, cv, z, z, z, z, w, lg, lb)


BIG = ("w_in", "w_attn_out", "w_ssm_glu", "w_conv_out", "w_mix_out", "w_ffn_in", "w_ffn_out", "w_ple_in", "w_ple_gate")
BIG_AXIS = {"w_in": 2, "w_attn_out": 2, "w_ssm_glu": 2, "w_conv_out": 2, "w_mix_out": 1, "w_ffn_in": 2,
            "w_ffn_out": 1, "w_ple_in": 2, "w_ple_gate": 1}
SHARD_MAJOR = ("w_in", "w_ffn_in")
SMALL = ("mix_norm_g", "b_gate", "attn_sinks", "ssm_lambda_re", "ssm_lambda_im", "ssm_log_dt", "ssm_b_re", "ssm_b_im",
         "ssm_c_re", "ssm_c_im", "ssm_d", "b_ssm_glu", "conv_dw_w", "conv_dw_b", "conv_norm_g", "conv_norm_b",
         "ffn_norm_g", "ple_norm_g", "final_norm_g")
WEIGHTS = ("mix_norm_g", "w_in", "b_gate", "attn_sinks", "w_attn_out", "ssm_lambda_re", "ssm_lambda_im", "ssm_log_dt",
           "ssm_b_re", "ssm_b_im", "ssm_c_re", "ssm_c_im", "ssm_d", "w_ssm_glu", "b_ssm_glu", "conv_dw_w", "conv_dw_b",
           "conv_norm_g", "conv_norm_b", "w_conv_out", "w_mix_out", "ffn_norm_g", "w_ffn_in", "w_ffn_out", "w_ple_in",
           "ple_norm_g", "w_ple_gate", "final_norm_g")
SSM_NAMES = ("ssm_lambda_re", "ssm_lambda_im", "ssm_log_dt", "ssm_b_re", "ssm_b_im", "ssm_c_re", "ssm_c_im")


def _ple_block(x, p_l, g, w_in, w_gate):
    t, d = x.shape
    kp = p_l.shape[1]
    tm = min(512, t)

    def body(x_ref, p_ref, g_ref, wi_ref, wg_ref, o_ref, gp_ref, h_ref, e_ref):
        xv = x_ref[...]
        e = lax.dot_general(p_ref[...].astype(BF16), wi_ref[...], _DIMS["nn"], preferred_element_type=F32).astype(BF16)
        h = _rms_fwd(xv, g_ref[...]).astype(BF16)
        gp = lax.dot_general(h, wg_ref[...], _DIMS["nn"], preferred_element_type=F32).astype(BF16)
        o_ref[...] = _ple_fwd(xv, gp, e)
        gp_ref[...], h_ref[...], e_ref[...] = gp, h, e

    row = lambda width: pl.BlockSpec((tm, width), lambda i: (i, 0))
    full = lambda v: pl.BlockSpec(v.shape, lambda i: (0, 0))
    out = lambda dt: jax.ShapeDtypeStruct((t, d), dt)
    return pl.pallas_call(
        body, grid=(t // tm,), in_specs=[row(d), row(kp), full(g), full(w_in), full(w_gate)],
        out_specs=[row(d)] * 4, out_shape=[out(F32), out(BF16), out(BF16), out(BF16)], name="ple_block",
        compiler_params=_params(("parallel",)))(x, p_l, g, w_in, w_gate)


def _mix_out_block(merged, w_mix, x, g):
    t, d = x.shape
    tm = min(512, t)

    def body(m_ref, w_ref, x_ref, g_ref, o_ref, h_ref):
        x1 = x_ref[...] + lax.dot_general(m_ref[...], w_ref[...], _DIMS["nn"], preferred_element_type=F32)
        o_ref[...] = x1
        h_ref[...] = _rms_fwd(x1, g_ref[...]).astype(h_ref.dtype)

    row = pl.BlockSpec((tm, d), lambda i: (i, 0))
    full = lambda v: pl.BlockSpec(v.shape, lambda i: (0, 0))
    return pl.pallas_call(
        body, grid=(t // tm,), in_specs=[row, full(w_mix), row, full(g)], out_specs=[row, row],
        out_shape=[jax.ShapeDtypeStruct((t, d), F32), jax.ShapeDtypeStruct((t, d), BF16)], name="mix_out_block",
        compiler_params=_params(("parallel",)))(merged, w_mix, x, g)


def _ple_block_bwd(dx3, gp, e, x, g, w_gate):
    t, d = x.shape
    tm = min(512, t)

    def body(dx3_ref, gp_ref, e_ref, x_ref, g_ref, wg_ref, de_ref, dgp_ref, dx_ref, dg_ref):
        @pl.when(pl.program_id(0) == 0)
        def _():
            dg_ref[...] = jnp.zeros_like(dg_ref)

        dx3 = dx3_ref[...]
        de, dgp = _ple_bwd(dx3, gp_ref[...], e_ref[...])
        dgp = dgp.astype(BF16)
        de_ref[...] = de.astype(de_ref.dtype)
        dgp_ref[...] = dgp
        dh = lax.dot_general(dgp, wg_ref[...], _DIMS["nt"], preferred_element_type=F32).astype(BF16)
        dx, dg = _rms_bwd(dh, x_ref[...], dx3, g_ref[...])
        dx_ref[...] = dx
        dg_ref[...] += dg

    row = pl.BlockSpec((tm, d), lambda i: (i, 0))
    full = lambda v: pl.BlockSpec(v.shape, lambda i: (0, 0))
    out = lambda dt: jax.ShapeDtypeStruct((t, d), dt)
    return pl.pallas_call(
        body, grid=(t // tm,), in_specs=[row, row, row, row, full(g), full(w_gate)],
        out_specs=[row, row, row, pl.BlockSpec((1, d), lambda i: (0, 0))],
        out_shape=[out(BF16), out(BF16), out(F32), jax.ShapeDtypeStruct((1, d), F32)], name="ple_block_bwd",
        compiler_params=_params(("arbitrary",)))(dx3, gp, e, x, g, w_gate)


def _in_proj_bwd(dz, w_in4, x, dres, g):
    t, d = x.shape
    nsh, _, cc = w_in4.shape
    tm = min(512, t)

    def body(dz_ref, w_ref, x_ref, dres_ref, g_ref, dx_ref, dg_ref):
        @pl.when(pl.program_id(0) == 0)
        def _():
            dg_ref[...] = jnp.zeros_like(dg_ref)

        dh = jnp.zeros((tm, d), F32)
        for sh in range(nsh):
            dh = dh + lax.dot_general(dz_ref[:, sh * cc:(sh + 1) * cc], w_ref[sh], _DIMS["nt"],
                                      preferred_element_type=F32)
        dx, dg = _rms_bwd(dh.astype(BF16), x_ref[...], dres_ref[...], g_ref[...])
        dx_ref[...] = dx
        dg_ref[...] += dg

    row = lambda width: pl.BlockSpec((tm, width), lambda i: (i, 0))
    return pl.pallas_call(
        body, grid=(t // tm,),
        in_specs=[row(nsh * cc), pl.BlockSpec(w_in4.shape, lambda i: (0, 0, 0)), row(d), row(d),
                  pl.BlockSpec(g.shape, lambda i: (0, 0))],
        out_specs=[row(d), pl.BlockSpec((1, d), lambda i: (0, 0))],
        out_shape=[jax.ShapeDtypeStruct((t, d), F32), jax.ShapeDtypeStruct((1, d), F32)], name="in_proj_bwd",
        compiler_params=_params(("arbitrary",)))(dz, w_in4, x, dres, g)


def _heads(v, nh):
    return v.reshape(v.shape[0], nh, HEAD_DIM).transpose(1, 0, 2)


def _tokens(v):
    return v.transpose(1, 0, 2).reshape(v.shape[1], v.shape[0] * HEAD_DIM)


def _row(v):
    return v.reshape(1, -1)


def _layer_fwd(x, p_l, w, s, rope, nbatch, seq, next_shards=None):
    t = x.shape[0]
    tm = 512
    d = D_MODEL
    sv = {}
    sv["x"] = x
    h = _rowwise("rms_mix", _rms_fwd, [R(x), V(_row(s["mix_norm_g"]))], [O(d, BF16)], tm=tm)
    cs = {nm: w[nm].shape[2] for nm in SHARD_MAJOR}
    tb = 1024
    got = {}
    plan = None if next_shards is None else _gather_plan(next_shards, GATHER_A)
    z = _mm("mm_in", h, w["w_in"], "nn", BF16, m=t, n=N_CHIPS * cs["w_in"], k=d, tm=tb, tn=cs["w_in"], tk=d,
            b_sh=cs["w_in"], comm=plan)
    if plan is not None:
        z, outs = z
        got.update(zip(plan["names"], outs))
    sv["h"], sv["z"] = h, z
    c, sa, sb = rope
    qkv_w = Q_WIDTH + 2 * KV_WIDTH
    qkv = _rowwise("rope_fwd", _rope_fwd, [R(z, Q_WIDTH, 0), R(z, KV_WIDTH, 4), R(z, KV_WIDTH, 5), R(c), R(sa), R(sb)],
                   [O(qkv_w, BF16)], tm=tm)
    qkv = _heads(qkv, qkv_w // HEAD_DIM)
    sinks = s["attn_sinks"].reshape(N_Q_HEADS, 1, 1)
    oh, lse = _attn_fwd(qkv, sinks, nbatch, seq)
    o = _tokens(oh)
    ya = _mm("mm_attn_out", o, w["w_attn_out"], "nn", BF16, m=t, n=d, k=Q_WIDTH, tm=tb, tn=d, tk=Q_WIDTH)
    sv.update(qkv=qkv, oh=oh, lse=lse, o=o, ya=ya, sinks=sinks)
    ssm_args = [s[nm] for nm in SSM_NAMES]
    a_re, a_im, wb, wc = _ssm_prep(*ssm_args)
    dskip = _row(s["ssm_d"])
    st, ys, gel, glu = _ssm_fwd(z, wb.astype(BF16), wc.astype(BF16), _ssm_tables(a_re, a_im, False), dskip,
                                w["w_ssm_glu"], _row(s["b_ssm_glu"]), nbatch, seq)
    sv.update(st=st, ys=ys, gel=gel, glu=glu, a=(a_re, a_im), wb=wb, wc=wc, dskip=dskip)
    cw = jnp.pad(s["conv_dw_w"], ((0, CONV_HALO - CONV_K), (0, 0)))
    cv, sc, yc = _conv_fwd(z, cw, _row(s["conv_dw_b"]), _row(s["conv_norm_g"]), _row(s["conv_norm_b"]),
                           w["w_conv_out"], nbatch, seq)
    sv.update(cw=cw, cv=cv, sc=sc, yc=yc)
    bg = _row(s["b_gate"])
    merge_ins = [R(z, 512, 3), R(z, 512, 5), R(z, 512, 7), V(bg, 512, 0), V(bg, 512, 2), V(bg, 512, 4),
                 R(ya, 512, 0), R(glu, 512, 0), R(glu, 512, 2), R(yc, 512, 0)]
    merged = _rowwise("merge_fwd", _merge_fwd, merge_ins, [O(512, BF16, total=d)], tm=tm, ncol=2)
    x1, hf = _mix_out_block(merged, w["w_mix_out"], x, _row(s["ffn_norm_g"]))
    sv.update(merged=merged, x1=x1)
    plan = None if next_shards is None else _gather_plan(next_shards, GATHER_B)
    f = _mm("mm_ffn_in", hf, w["w_ffn_in"], "nn", BF16, m=t, n=2 * FFN_HIDDEN, k=d, tm=tb, tn=cs["w_ffn_in"], tk=d,
            b_sh=cs["w_ffn_in"], comm=plan)
    if plan is not None:
        f, outs = f
        got.update(zip(plan["names"], outs))
    act = (lambda i, j, kk, fg, fu: _ffn_act(fg, fu), [(f, lambda i, j, kk: (i, 0)), (f, lambda i, j, kk: (i, 1))])
    x2, act = _mm("mm_ffn_out", act, w["w_ffn_out"], "nn", F32, m=t, n=d, k=FFN_HIDDEN, tm=256, tn=d, tk=FFN_HIDDEN,
                  res=x1, a_keep=True)
    sv.update(hf=hf, f=f, act=act, x2=x2)
    x3, gp, hp, e = _ple_block(x2, p_l, _row(s["ple_norm_g"]), w["w_ple_in"], w["w_ple_gate"])
    sv.update(e=e, hp=hp, gp=gp, p=p_l)
    return x3, sv, got


def _layer_bwd(dx3, sv, w, s, rope, nbatch, seq):
    t = dx3.shape[0]
    tm = 512
    d = D_MODEL
    gb, gs = {}, {}
    cs = {nm: w[nm].shape[2] for nm in SHARD_MAJOR}
    tb = 1024

    def wg(name, a, b, m, n, tm=1024, tk=1024, shard=None):
        return _mm(name, a, b, "tn", BF16, m=m, n=n, k=t, tm=tm, tn=n if shard is None else cs[shard], tk=tk,
                   o_sh=None if shard is None else cs[shard])

    de, dgp, dx2, gs["ple_norm_g"] = _ple_block_bwd(dx3, sv["gp"], sv["e"], sv["x2"], _row(s["ple_norm_g"]),
                                                    w["w_ple_gate"])
    gb["w_ple_in"] = wg("wg_ple_in", sv["p"], de, sv["p"].shape[1], d, tk=2048)
    gb["w_ple_gate"] = wg("wg_ple_gate", sv["hp"], dgp, d, d, tk=2048)
    fw = FFN_HIDDEN // 2
    dact = _mm("mmb_ffn_out", dx2, w["w_ffn_out"], "nt", BF16, m=t, n=FFN_HIDDEN, k=d, tm=tb, tn=fw, tk=d)
    gb["w_ffn_out"] = wg("wg_ffn_out", sv["act"], dx2, FFN_HIDDEN, d, tm=fw)
    f = sv["f"]

    def df_tile(is_gate, da, fg, fu):
        dfg, dfu = _ffn_act_bwd(da, fg, fu)
        return jnp.where(is_gate, dfg, dfu)

    assert cs["w_ffn_in"] == fw
    df_rows = (lambda i, j, kk, *v: df_tile(kk < 2, *v),
               [(dact, lambda i, j, kk: (i, kk % 2)), (f, lambda i, j, kk: (i, kk % 2)), (f, lambda i, j, kk: (i, 2 + kk % 2))])
    dhf, df = _mm("mmb_ffn_in", df_rows, w["w_ffn_in"], "nt", BF16, m=t, n=d, k=2 * FFN_HIDDEN, tm=512, tn=d, tk=fw,
                  b_sh=fw, a_keep=True)
    gb["w_ffn_in"] = wg("wg_ffn_in", sv["hf"], df, d, 2 * FFN_HIDDEN, shard="w_ffn_in")
    dx1, gs["ffn_norm_g"] = _rowwise("rms_ffn_bwd", _rms_bwd, [R(dhf), R(sv["x1"]), R(dx2), V(_row(s["ffn_norm_g"]))],
                                     [O(d, F32)], [A(d)], tm=tm)
    dm = _mm("mmb_mix", dx1, w["w_mix_out"], "nt", BF16, m=t, n=d, k=d, tm=tb, tn=d, tk=d)
    gb["w_mix_out"] = wg("wg_mix", sv["merged"], dx1, d, d)
    z, glu, bg = sv["z"], sv["glu"], _row(s["b_gate"])
    ins = [R(dm, 512, 0), R(z, 512, 3), R(z, 512, 5), R(z, 512, 7), V(bg, 512, 0), V(bg, 512, 2), V(bg, 512, 4),
           R(sv["ya"], 512, 0), R(glu, 512, 0), R(glu, 512, 2), R(sv["yc"], 512, 0)]
    ob = lambda: O(512, BF16, total=d)
    ab = lambda: A(512, total=d)
    dya, dga, dgb, dyc, d0, d1, d2, db0, db1, db2, dba, dbb = _rowwise(
        "merge_bwd", _merge_bwd, ins, [ob() for _ in range(7)], [ab() for _ in range(5)], tm=tm, ncol=2)
    gs["b_gate"] = jnp.concatenate([db0, db1, db2], axis=1)
    gs["b_ssm_glu"] = jnp.concatenate([dba, dbb], axis=1)
    dglu = jnp.concatenate([dga, dgb], axis=1)
    gb["w_attn_out"] = wg("wg_attn_out", sv["o"], dya, Q_WIDTH, d, tk=2048)
    do = _mm("mmb_attn_out", dya, w["w_attn_out"], "nt", BF16, m=t, n=Q_WIDTH, k=d, tm=tb, tn=Q_WIDTH, tk=d)
    dqkv, dsink = _attn_bwd(sv["qkv"], sv["oh"], _heads(do, N_Q_HEADS), sv["lse"], sv["sinks"], nbatch, seq)
    dqkv = _tokens(dqkv)
    gs["attn_sinks"] = dsink.reshape(-1)
    c, sa, sb = rope
    dq = _rowwise("rope_bwd_q", _rope_bwd_q, [R(dqkv, Q_WIDTH, 0), R(c), R(sa), R(sb)], [O(Q_WIDTH, BF16)], tm=tm)
    dk, dv = _kv_combine(dqkv, c, sa, sb, seq)
    gb["w_ssm_glu"] = wg("wg_ssm_glu", sv["gel"], dglu, SSM_WIDTH, 2 * d, tk=2048)
    dgi = _mm("mmb_glu", dglu, w["w_ssm_glu"], "nt", BF16, m=t, n=SSM_WIDTH, k=2 * d, tm=tb, tn=SSM_WIDTH, tk=2 * d)
    a_re, a_im = sv["a"]
    du, dwb, dwc, dd, da = _ssm_bwd(dgi, sv["ys"], sv["st"], z, sv["wb"].T.astype(BF16), sv["wc"].T.astype(BF16),
                                    _ssm_tables(a_re, a_im, True), sv["dskip"], nbatch, seq)
    gs["ssm_d"] = dd.reshape(-1)
    da = jnp.sum(da, axis=0)
    _, prep_vjp = jax.vjp(_ssm_prep, *[s[nm] for nm in SSM_NAMES])
    for nm, g in zip(SSM_NAMES, prep_vjp((da[:SSM_LANES], da[SSM_LANES:], dwb, dwc))):
        gs[nm] = g
    gb["w_conv_out"] = wg("wg_conv_out", sv["sc"], dyc, CONV_WIDTH, d, tk=2048)
    dsc = _mm("mmb_conv_out", dyc, w["w_conv_out"], "nt", BF16, m=t, n=CONV_WIDTH, k=d, tm=tb, tn=CONV_WIDTH, tk=d)
    dca, dcg, dcw, dcb, dlg, dlb = _conv_bwd(dsc, sv["cv"], z, sv["cw"], _row(s["conv_norm_g"]),
                                             _row(s["conv_norm_b"]), nbatch, seq)
    gs["conv_dw_w"] = dcw[:CONV_K]
    gs["conv_dw_b"], gs["conv_norm_g"], gs["conv_norm_b"] = dcb.reshape(-1), dlg.reshape(-1), dlb.reshape(-1)
    dz = jnp.concatenate([dq, dk, dv, du, dca, dcg, d0, d1, d2], axis=1)
    gb["w_in"] = wg("wg_in", sv["h"], dz, d, dz.shape[1], tk=2048, shard="w_in")
    dx, gs["mix_norm_g"] = _in_proj_bwd(dz, w["w_in"], sv["x"], dx1, _row(s["mix_norm_g"]))
    gs["mix_norm_g"], gs["ffn_norm_g"], gs["ple_norm_g"] = (gs[nm].reshape(-1) for nm in
                                                            ("mix_norm_g", "ffn_norm_g", "ple_norm_g"))
    gs["b_gate"], gs["b_ssm_glu"] = gs["b_gate"].reshape(-1), gs["b_ssm_glu"].reshape(-1)
    return dx, {nm: _shard_major(nm, g) for nm, g in gb.items()}, gs


def _rope_tables(positions):
    inv_freq = ROPE_THETA ** (-jnp.arange(0, ROPE_DIM, 2, dtype=F32) / ROPE_DIM)
    ang = positions.reshape(-1).astype(F32)[:, None] * inv_freq
    cos, sin = jnp.cos(ang), jnp.sin(ang)
    t = ang.shape[0]
    rest = HEAD_DIM - ROPE_DIM
    c = jnp.concatenate([cos, cos, jnp.ones((t, rest), F32)], axis=1)
    sa = jnp.concatenate([-sin, jnp.zeros((t, HEAD_DIM - ROPE_HALF), F32)], axis=1)
    sb = jnp.concatenate([jnp.zeros((t, ROPE_HALF), F32), sin, jnp.zeros((t, rest), F32)], axis=1)
    two = lambda v: jnp.concatenate([v, v], axis=1)
    return two(c), two(sa), two(sb)


def _natural(nm, w4):
    if nm in SHARD_MAJOR:
        return w4
    if BIG_AXIS[nm] == 1:
        return w4.reshape(-1, w4.shape[2])
    return w4.transpose(1, 0, 2).reshape(w4.shape[1], -1)


def _shard_major(nm, g):
    if nm in SHARD_MAJOR:
        return g
    if BIG_AXIS[nm] == 1:
        return g.reshape(N_CHIPS, -1, g.shape[1])
    return g.reshape(g.shape[0], N_CHIPS, -1).transpose(1, 0, 2)


def _untap(taps4, cols):
    flat = taps4.reshape(N_CHIPS, -1)[:, :CONV_K * cols]
    return flat.reshape(N_CHIPS, CONV_K, cols).transpose(1, 0, 2).reshape(CONV_K, N_CHIPS * cols)


def _local_step(x, p, positions, loss_target, small, wfull=None, shards=None):
    nbatch, seq, d = x.shape
    depth = p.shape[0]
    t = nbatch * seq
    rope = _rope_tables(positions)
    xs = x.reshape(t, d)
    saved, ws, ss = [], [], []
    got = None if shards is None else _gather_now((shards, 0))
    for l in range(depth):
        w4 = {nm: wfull[nm][l] for nm in BIG} if shards is None else got
        w_l = {nm: _natural(nm, w4[nm]) for nm in BIG}
        s_l = {nm: small[nm][l] for nm in small if nm != "final_norm_g"}
        if shards is not None:
            s_l["conv_dw_w"] = _untap(got[TAPS], CONV_WIDTH // N_CHIPS)
        nxt = (shards, l + 1) if shards is not None and l + 1 < depth else None
        xs, sv, got = _layer_fwd(xs, p[l].reshape(t, -1), w_l, s_l, rope, nbatch, seq, nxt)
        saved.append(sv)
        ws.append(w_l)
        ss.append(s_l)
    dx, loss_cols, dgf = _rowwise("loss_head", _loss_fn, [R(xs), R(loss_target.reshape(t, d)),
                                                          V(_row(small["final_norm_g"]))],
                                  [O(d, F32)], [A(d), A(d)], tm=512)
    gbs, gss = [None] * depth, [None] * depth
    for l in reversed(range(depth)):
        dx, gbs[l], gss[l] = _layer_bwd(dx, saved[l], ws[l], ss[l], rope, nbatch, seq)
    gbig = {nm: jnp.stack([g[nm] for g in gbs]) for nm in BIG}
    gsmall = {nm: jnp.stack([g[nm] for g in gss]) for nm in SMALL if nm != "final_norm_g"}
    gsmall["final_norm_g"] = dgf.reshape(-1)
    return loss_cols, dx.reshape(nbatch, seq, d), gbig, gsmall


HBM = pl.BlockSpec(memory_space=pltpu.HBM)


def _place():
    x, y, c = lax.axis_index("x"), lax.axis_index("y"), lax.axis_index("c")
    chips = [(1 - x, y), (x, 1 - y), (1 - x, 1 - y)]
    return x, y, c, chips


def _remote(src, dst, send_sem, recv_sem, to):
    return pltpu.make_async_remote_copy(src_ref=src, dst_ref=dst, send_sem=send_sem, recv_sem=recv_sem,
                                        device_id=to, device_id_type=MESH)


TAPS = "taps"
GATHER_ALL = (("w_ffn_in", "w_ffn_out"),
              ("w_in", "w_ple_gate", "w_mix_out", "w_attn_out", "w_ssm_glu", "w_conv_out", "w_ple_in", TAPS))
GATHER_A = (("w_ffn_in",), ("w_in", "w_ple_gate"))
GATHER_B = (("w_ffn_out",), ("w_mix_out", "w_attn_out", "w_ssm_glu", "w_conv_out", "w_ple_in", TAPS))


def _gather_plan(shards, sets):
    stacked, layer = shards
    names = sets[0] + sets[1]
    n = len(names)
    idx = {nm: i for i, nm in enumerate(names)}

    def start(ins, outs, sems):
        send1, recv1, _, _, send0, recv0 = sems
        x, y, c, chips = _place()
        me = 2 * x + y
        for i in range(n):
            _remote(ins[i].at[layer], outs[i].at[me], send0.at[i], recv0.at[i], (x, y, 1 - c)).start()
        for role in (0, 1):
            @pl.when(c == role)
            def _():
                for nm in sets[role]:
                    i = idx[nm]
                    for k, (cx, cy) in enumerate(chips):
                        _remote(ins[i].at[layer], outs[i].at[me], send1.at[i, k], recv1.at[i, k], (cx, cy, c)).start()

    def finish(ins, outs, sems):
        send1, recv1, send2, recv2, send0, recv0 = sems
        x, y, c, chips = _place()
        me = 2 * x + y
        sib = (x, y, 1 - c)
        for role in (0, 1):
            @pl.when(c == role)
            def _():
                passed = []
                for nm in sets[role]:
                    i = idx[nm]
                    for k, (cx, cy) in enumerate(chips):
                        slot = outs[i].at[2 * cx + cy]
                        _remote(slot, slot, send1.at[i, k], recv1.at[i, k], (cx, cy, c)).wait_recv()
                        cp = _remote(slot, slot, send2.at[i, k], recv2.at[i, k], sib)
                        cp.start()
                        passed.append(cp)
                for nm in sets[1 - role]:
                    i = idx[nm]
                    for k, (cx, cy) in enumerate(chips):
                        slot = outs[i].at[2 * cx + cy]
                        _remote(slot, slot, send2.at[i, k], recv2.at[i, k], sib).wait_recv()
                for nm in sets[role]:
                    i = idx[nm]
                    for k, (cx, cy) in enumerate(chips):
                        _remote(ins[i].at[layer], outs[i].at[me], send1.at[i, k], recv1.at[i, k],
                                (cx, cy, c)).wait_send()
                for cp in passed:
                    cp.wait_send()
        for i in range(n):
            _remote(ins[i].at[layer], outs[i].at[me], send0.at[i], recv0.at[i], sib).wait()

    ins = [stacked[nm] for nm in names]
    return dict(names=names, ins=ins, start=start, finish=finish,
                out_shapes=[jax.ShapeDtypeStruct((N_CHIPS,) + v.shape[1:], v.dtype) for v in ins],
                sems=[pltpu.SemaphoreType.DMA((n, 3)) for _ in range(4)] + [pltpu.SemaphoreType.DMA((n,))
                                                                            for _ in range(2)])


def _gather_now(shards):
    return _comm_now("gather_weights", _gather_plan(shards, GATHER_ALL))


def _pair_exchange(grads):
    n = len(grads)
    hl = grads[0].shape[0] // 2

    def body(*refs):
        ins, outs = refs[:n], refs[n:2 * n]
        send, recv = refs[2 * n:]
        x, y, c, _ = _place()
        other = pl.ds((1 - c) * hl, hl)
        cps = [_remote(ins[i].at[other], outs[i], send.at[i], recv.at[i], (x, y, 1 - c)) for i in range(n)]
        for cp in cps:
            cp.start()
        for cp in cps:
            cp.wait()

    out_shape = [jax.ShapeDtypeStruct((hl,) + g.shape[1:], g.dtype) for g in grads]
    sems = [pltpu.SemaphoreType.DMA((n,)) for _ in range(2)]
    return pl.pallas_call(body, out_shape=out_shape, in_specs=[HBM] * n, out_specs=[HBM] * n, scratch_shapes=sems,
                          name="reduce_pair_exchange")(*grads)


def _pair_add(g, r):
    hl, _, rr, cc = r.shape
    rows = hl * N_CHIPS * rr
    nblk = rows // rr

    def body(c_ref, g_ref, r_ref, o_ref):
        o_ref[...] = (g_ref[...].astype(F32) + r_ref[...].astype(F32)).astype(o_ref.dtype)

    grid_spec = pltpu.PrefetchScalarGridSpec(
        num_scalar_prefetch=1, grid=(nblk,),
        in_specs=[pl.BlockSpec((rr, cc), lambda i, c_ref: (c_ref[0] * nblk + i, 0)),
                  pl.BlockSpec((rr, cc), lambda i, c_ref: (i, 0))],
        out_specs=pl.BlockSpec((rr, cc), lambda i, c_ref: (i, 0)))
    c = lax.axis_index("c").astype(jnp.int32).reshape(1)
    out = pl.pallas_call(body, out_shape=jax.ShapeDtypeStruct((rows, cc), r.dtype), grid_spec=grid_spec,
                         name="reduce_pair_add", compiler_params=_params(("parallel",)))(
        c, g.reshape(-1, cc), r.reshape(rows, cc))
    return out.reshape(r.shape)


def _chip_exchange(psums):
    n = len(psums)

    def body(*refs):
        ins, got = refs[:n], refs[n:2 * n]
        send, recv = refs[2 * n:]
        x, y, c, chips = _place()
        cps = [_remote(ins[i].at[:, 2 * cx + cy], got[i].at[k], send.at[i, k], recv.at[i, k], (cx, cy, c))
               for i in range(n) for k, (cx, cy) in enumerate(chips)]
        for cp in cps:
            cp.start()
        for cp in cps:
            cp.wait()

    got_shape = [jax.ShapeDtypeStruct((3, p.shape[0]) + p.shape[2:], p.dtype) for p in psums]
    sems = [pltpu.SemaphoreType.DMA((n, 3)), pltpu.SemaphoreType.DMA((n, 3))]
    return pl.pallas_call(body, out_shape=got_shape, in_specs=[HBM] * n, out_specs=[HBM] * n, scratch_shapes=sems,
                          name="reduce_chip_exchange")(*psums)


def _comm_now(name, plan):
    n = len(plan["ins"])

    def body(*refs):
        ins, outs, sems = refs[:n], refs[n:2 * n], refs[2 * n:]
        plan["start"](ins, outs, sems)
        plan["finish"](ins, outs, sems)

    outs = pl.pallas_call(body, out_shape=plan["out_shapes"], in_specs=[HBM] * n, out_specs=[HBM] * n,
                          scratch_shapes=plan["sems"], name=name)(*plan["ins"])
    return dict(zip(plan["names"], outs))


def _sum4(psum, got):
    hl, _, rr, cc = psum.shape
    tr = rr if rr * cc <= 512 * 1024 else rr // 2

    def body(place_ref, own_ref, g0_ref, g1_ref, g2_ref, o_ref):
        tot = (own_ref[...].astype(F32) + g0_ref[...].astype(F32)) + g1_ref[...].astype(F32)
        o_ref[...] = tot + g2_ref[...].astype(F32)

    def got_spec(k):
        return pl.BlockSpec((None, None, tr, cc), lambda h, i, place: (k, h, i, 0))

    grid_spec = pltpu.PrefetchScalarGridSpec(
        num_scalar_prefetch=1, grid=(hl, rr // tr),
        in_specs=[pl.BlockSpec((None, None, tr, cc), lambda h, i, place: (h, place[0], i, 0)),
                  got_spec(0), got_spec(1), got_spec(2)],
        out_specs=pl.BlockSpec((None, tr, cc), lambda h, i, place: (place[1] * hl + h, i, 0)))
    place = jnp.stack([2 * lax.axis_index("x") + lax.axis_index("y"), lax.axis_index("c")]).astype(jnp.int32)
    return pl.pallas_call(body, out_shape=jax.ShapeDtypeStruct((2 * hl, rr, cc), F32), grid_spec=grid_spec,
                          name="reduce_sum4", compiler_params=_params(("parallel", "parallel")))(
        place, psum, got, got, got)


def _pair_gather(sums):
    n = len(sums)
    hl = sums[0].shape[0] // 2

    def body(*refs):
        bufs = refs[n:2 * n]
        send, recv = refs[2 * n:]
        x, y, c, _ = _place()
        mine = pl.ds(c * hl, hl)
        cps = [_remote(bufs[i].at[mine], bufs[i].at[mine], send.at[i], recv.at[i], (x, y, 1 - c)) for i in range(n)]
        for cp in cps:
            cp.start()
        for cp in cps:
            cp.wait()

    out_shape = [jax.ShapeDtypeStruct(v.shape, v.dtype) for v in sums]
    sems = [pltpu.SemaphoreType.DMA((n,)) for _ in range(2)]
    return pl.pallas_call(body, out_shape=out_shape, in_specs=[HBM] * n, out_specs=[HBM] * n, scratch_shapes=sems,
                          input_output_aliases={i: i for i in range(n)}, name="reduce_pair_gather")(*sums)


def _allreduce_small(vec):
    rows = vec.shape[0]

    def body(v_ref, o_ref, all_ref, send, recv):
        x, y, c, _ = _place()
        me = 4 * x + 2 * y + c
        all_ref[me] = v_ref[...]
        cps = []
        for dlt in range(1, N_DEV):
            fx, fy, fc = (dlt >> 2) & 1, (dlt >> 1) & 1, dlt & 1
            to = (1 - x if fx else x, 1 - y if fy else y, 1 - c if fc else c)
            cps.append(_remote(v_ref, all_ref.at[me], send.at[dlt - 1], recv.at[dlt - 1], to))
        for cp in cps:
            cp.start()
        for cp in cps:
            cp.wait()
        tot = all_ref[0]
        for dev in range(1, N_DEV):
            tot = tot + all_ref[dev]
        o_ref[...] = tot

    vm = pl.BlockSpec(memory_space=pltpu.VMEM)
    return pl.pallas_call(
        body, out_shape=jax.ShapeDtypeStruct(vec.shape, F32), in_specs=[vm], out_specs=vm,
        scratch_shapes=[pltpu.VMEM((N_DEV, rows, 128), F32), pltpu.SemaphoreType.DMA((N_DEV - 1,)),
                        pltpu.SemaphoreType.DMA((N_DEV - 1,))],
        name="allreduce_small", compiler_params=pltpu.CompilerParams(vmem_limit_bytes=VMEM_LIMIT))(vec)


def _adamw(name, w, g, m, v):
    rows, cc = w.shape
    tm = rows if rows * cc <= 512 * 1024 else math.gcd(rows, 256)
    return _rowwise(name, _adamw_fn, [R(w), R(g), R(m), R(v)], [O(cc, F32), O(cc, F32), O(cc, F32)], tm=tm)


def _pack(parts):
    flat = jnp.concatenate([v.reshape(-1).astype(F32) for v in parts])
    pad = (-flat.shape[0]) % (SUBLANES * 128)
    return jnp.pad(flat, (0, pad)).reshape(-1, 128)


def _unpack(packed, shapes):
    flat, out, pos = packed.reshape(-1), [], 0
    for shp in shapes:
        size = math.prod(shp)
        out.append(flat[pos:pos + size].reshape(shp))
        pos += size
    return out


def kernel(x, p, positions, mix_norm_g, w_in, b_gate, attn_sinks, w_attn_out, ssm_lambda_re, ssm_lambda_im, ssm_log_dt, ssm_b_re, ssm_b_im, ssm_c_re, ssm_c_im, ssm_d, w_ssm_glu, b_ssm_glu, conv_dw_w, conv_dw_b, conv_norm_g, conv_norm_b, w_conv_out, w_mix_out, ffn_norm_g, w_ffn_in, w_ffn_out, w_ple_in, ple_norm_g, w_ple_gate, final_norm_g, loss_target, m_mix_norm_g, m_w_in, m_b_gate, m_attn_sinks, m_w_attn_out, m_ssm_lambda_re, m_ssm_lambda_im, m_ssm_log_dt, m_ssm_b_re, m_ssm_b_im, m_ssm_c_re, m_ssm_c_im, m_ssm_d, m_w_ssm_glu, m_b_ssm_glu, m_conv_dw_w, m_conv_dw_b, m_conv_norm_g, m_conv_norm_b, m_w_conv_out, m_w_mix_out, m_ffn_norm_g, m_w_ffn_in, m_w_ffn_out, m_w_ple_in, m_ple_norm_g, m_w_ple_gate, m_final_norm_g, v_mix_norm_g, v_w_in, v_b_gate, v_attn_sinks, v_w_attn_out, v_ssm_lambda_re, v_ssm_lambda_im, v_ssm_log_dt, v_ssm_b_re, v_ssm_b_im, v_ssm_c_re, v_ssm_c_im, v_ssm_d, v_w_ssm_glu, v_b_ssm_glu, v_conv_dw_w, v_conv_dw_b, v_conv_norm_g, v_conv_norm_b, v_w_conv_out, v_w_mix_out, v_ffn_norm_g, v_w_ffn_in, v_w_ffn_out, v_w_ple_in, v_ple_norm_g, v_w_ple_gate, v_final_norm_g):
    given = dict(locals())
    wts = {nm: given[nm] for nm in WEIGHTS}
    mom = {nm: given["m_" + nm] for nm in WEIGHTS}
    var = {nm: given["v_" + nm] for nm in WEIGHTS}
    depth = p.shape[0]
    chip = 2 * lax.axis_index("x") + lax.axis_index("y")

    cw_cols = conv_dw_w.shape[2]
    taps = jnp.pad(conv_dw_w.reshape(depth, -1), ((0, 0), (0, (-CONV_K * cw_cols) % (SUBLANES * 128))))
    shards = {**{nm: wts[nm].astype(BF16) for nm in BIG}, TAPS: taps.reshape(depth, -1, 128)}
    small = {nm: wts[nm] for nm in SMALL if nm != "conv_dw_w"}

    loss_cols, grad_x, gbig, gsmall = _local_step(x, p, positions, loss_target, small, shards=shards)

    parts = [loss_cols] + [gsmall[nm] for nm in SMALL]
    total = _allreduce_small(_pack(parts))
    summed = _unpack(total, [v.shape for v in parts])
    loss = jnp.sum(summed[0])
    gsum = dict(zip(SMALL, summed[1:]))
    gsum["conv_dw_w"] = lax.dynamic_slice_in_dim(gsum["conv_dw_w"], chip * cw_cols, cw_cols, axis=2)
    shapes = [wts[nm].shape for nm in SMALL]
    deltas, new_m, new_v = _adamw("adamw_small", _pack([wts[nm] for nm in SMALL]), _pack([gsum[nm] for nm in SMALL]),
                                  _pack([mom[nm] for nm in SMALL]), _pack([var[nm] for nm in SMALL]))
    grads = dict(gsum)
    delta = dict(zip(SMALL, _unpack(deltas, shapes)))
    newm = dict(zip(SMALL, _unpack(new_m, shapes)))
    newv = dict(zip(SMALL, _unpack(new_v, shapes)))

    gl = [gbig[nm] for nm in BIG]
    sib = _pair_exchange(gl)
    psums = [_pair_add(g, r) for g, r in zip(gl, sib)]
    got = _chip_exchange(psums)
    sums = _pair_gather([_sum4(ps, g) for ps, g in zip(psums, got)])
    for nm, g in zip(BIG, sums):
        shp = wts[nm].shape
        two = lambda v: v.reshape(-1, shp[-1])
        g = g.reshape(shp)
        d_w, n_m, n_v = _adamw("adamw_" + nm, two(wts[nm]), two(g), two(mom[nm]), two(var[nm]))
        grads[nm], delta[nm], newm[nm], newv[nm] = g, d_w.reshape(shp), n_m.reshape(shp), n_v.reshape(shp)

    return (loss, grad_x, *[grads[nm] for nm in WEIGHTS], *[delta[nm] for nm in WEIGHTS],
            *[newm[nm] for nm in WEIGHTS], *[newv[nm] for nm in WEIGHTS])
```

```python
import functools
import math

import jax
import jax.numpy as jnp
from jax import lax
from jax.experimental import pallas as pl
from jax.experimental.pallas import tpu as pltpu

F32 = jnp.float32
BF16 = jnp.bfloat16

D_MODEL = 1024
HEAD_DIM = 64
N_Q_HEADS = 8
N_KV_HEADS = 2
GQA_GROUP = N_Q_HEADS // N_KV_HEADS
ATT_BLOCK = 128
ROPE_THETA = 500000.0
ROPE_DIM = HEAD_DIM // 4
ROPE_HALF = ROPE_DIM // 2
Q_WIDTH = N_Q_HEADS * HEAD_DIM
KV_WIDTH = N_KV_HEADS * HEAD_DIM
SSM_WIDTH = 256
SSM_GROUP = 16
SSM_GROUPS = 16
SSM_STATE = 64
SSM_LANES = SSM_GROUPS * SSM_STATE
CONV_WIDTH = 256
CONV_K = 31
CONV_HALO = 32
FFN_HIDDEN = 2816
EPS = 1e-6
NEG_INF = -1e30
SCALE = HEAD_DIM ** -0.5

ADAM_LR = 0.001
ADAM_B1 = 0.9
ADAM_B2 = 0.999
ADAM_EPS = 1e-08
ADAM_WD = 0.01
ADAM_STEP = 10

N_CHIPS = 4
N_DEV = 8
SUBLANES = 8
VMEM_LIMIT = 56 * 1024 * 1024

MESH = pl.DeviceIdType.MESH


def _params(sem=None):
    return pltpu.CompilerParams(dimension_semantics=sem, vmem_limit_bytes=VMEM_LIMIT)


def R(arr, width=None, cb=0, rb=0):
    return ("r", arr, arr.shape[1] if width is None else width, (cb, rb))


def V(arr, width=None, cb=0):
    return ("v", arr, arr.shape[1] if width is None else width, cb)


def _cbf(cb):
    return cb if callable(cb) else (lambda j, c=cb: c + j)


def _rowwise(name, fn, ins, outs, accs=(), *, tm, ncol=1):
    t = [a for k, a, _, _ in ins if k == "r"][0].shape[0]
    tm = min(tm, t)
    assert t % tm == 0, (name, t, tm)
    n_i, n_o, n_a = len(ins), len(outs), len(accs)

    def body(*refs):
        vals = fn(*[r[...] for r in refs[:n_i]])
        if not isinstance(vals, (tuple, list)):
            vals = (vals,)
        for ref, val in zip(refs[n_i:n_i + n_o], vals[:n_o]):
            ref[...] = val.astype(ref.dtype)
        if n_a:
            acc_refs = refs[n_i + n_o:]

            @pl.when(pl.program_id(1) == 0)
            def _():
                for ref in acc_refs:
                    ref[...] = jnp.zeros_like(ref)

            for ref, val in zip(acc_refs, vals[n_o:]):
                ref[...] += val

    in_specs = []
    for kind, arr, width, cb in ins:
        if kind == "r":
            f = _cbf(cb[0])
            in_specs.append(pl.BlockSpec((tm, width), functools.partial(lambda j, i, f, rb: (i + rb, f(j)), f=f, rb=cb[1])))
        else:
            f = _cbf(cb)
            in_specs.append(pl.BlockSpec((arr.shape[0], width), functools.partial(lambda j, i, f: (0, f(j)), f=f)))
    out_specs, out_shape = [], []
    for total, width, cb, dt in outs:
        f = _cbf(cb)
        out_specs.append(pl.BlockSpec((tm, width), functools.partial(lambda j, i, f: (i, f(j)), f=f)))
        out_shape.append(jax.ShapeDtypeStruct((t, total), dt))
    for total, width, cb in accs:
        f = _cbf(cb)
        out_specs.append(pl.BlockSpec((1, width), functools.partial(lambda j, i, f: (0, f(j)), f=f)))
        out_shape.append(jax.ShapeDtypeStruct((1, total), F32))
    sem = ("arbitrary", "arbitrary") if n_a else ("parallel", "parallel")
    res = pl.pallas_call(body, out_shape=out_shape, grid=(ncol, t // tm), in_specs=in_specs, out_specs=out_specs,
                         name=name, compiler_params=_params(sem))(*[a for _, a, _, _ in ins])
    return res[0] if len(res) == 1 else res


def O(width, dtype, total=None, cb=0):
    return (width if total is None else total, width, cb, dtype)


def A(width, total=None, cb=0):
    return (width if total is None else total, width, cb)


_DIMS = {"nn": (((1,), (0,)), ((), ())), "nt": (((1,), (1,)), ((), ())), "tn": (((0,), (0,)), ((), ()))}


def _mm(name, a, b, mode, out_dtype, *, m, n, k, tm, tn, tk, a_off=0, b_off=0, res=None, bias=None, b_sh=None, o_sh=None,
        comm=None, a_keep=False):
    tm, tn, tk = min(tm, m), min(tn, n), min(tk, k)
    assert m % tm == 0 and n % tn == 0 and k % tk == 0, (name, m, n, k, tm, tn, tk)
    nk = k // tk
    has_res, has_bias = res is not None, bias is not None
    a_fn, a_ops = a if isinstance(a, tuple) else (None, [(a, None)])
    b_fn, b_ops = b if isinstance(b, tuple) else (None, [(b, None)])
    na, nb_ = len(a_ops), len(b_ops)
    a_bytes = sum(m * k * arr.dtype.itemsize for arr, _ in a_ops)
    b_bytes = sum(n * k * arr.dtype.itemsize for arr, _ in b_ops)
    swap = nk == 1 and b_bytes + (n // tn) * a_bytes < a_bytes + (m // tm) * b_bytes
    grid = (n // tn, m // tm, nk) if swap else (m // tm, n // tn, nk)
    ncomm = 0 if comm is None else len(comm["ins"])

    def body(*refs):
        g0, g1, kk = pl.program_id(0), pl.program_id(1), pl.program_id(2)
        gi, gj = (g1, g0) if swap else (g0, g1)
        a_tiles = [r[...] for r in refs[:na]]
        b_tiles = [r[...] for r in refs[na:na + nb_]]
        a_val = a_tiles[0] if a_fn is None else a_fn(gi, gj, kk, *a_tiles)
        b_val = b_tiles[0] if b_fn is None else b_fn(gi, gj, kk, *b_tiles)
        pos = na + nb_
        res_ref = bias_ref = None
        if has_res:
            res_ref = refs[pos]
            pos += 1
        if has_bias:
            bias_ref = refs[pos]
            pos += 1
        comm_ins = refs[pos:pos + ncomm]
        o_ref = refs[pos + ncomm]
        comm_outs = refs[pos + ncomm + 1:pos + 2 * ncomm + 1]
        scratch = refs[pos + 2 * ncomm + 1:]
        if a_keep:
            scratch[0][...] = a_val.astype(BF16)
            scratch = scratch[1:]
        if comm is not None:
            sems = scratch[1:] if nk > 1 else scratch

            @pl.when((g0 == 0) & (g1 == 0) & (kk == 0))
            def _():
                comm["start"](comm_ins, comm_outs, sems)

        def finish(r):
            if has_bias:
                r = r + bias_ref[...]
            if has_res:
                r = r + res_ref[...].astype(F32)
            o_ref[...] = r.astype(o_ref.dtype)

        part = lax.dot_general(a_val.astype(BF16), b_val.astype(BF16), _DIMS[mode], preferred_element_type=F32)
        if nk == 1:
            finish(part)
        else:
            acc_ref = scratch[0]

            @pl.when(kk == 0)
            def _():
                acc_ref[...] = part

            @pl.when(kk > 0)
            def _():
                acc_ref[...] += part

            @pl.when(kk == nk - 1)
            def _():
                finish(acc_ref[...])

        if comm is not None:
            @pl.when((g0 == grid[0] - 1) & (g1 == grid[1] - 1) & (kk == nk - 1))
            def _():
                comm["finish"](comm_ins, comm_outs, sems)

    def at(f):
        return (lambda g0, g1, kk: f(g1, g0, kk)) if swap else f

    if mode == "nn":
        a_spec = pl.BlockSpec((tm, tk), at(lambda i, j, kk: (i, kk + a_off)))
        b_spec = pl.BlockSpec((tk, tn), at(lambda i, j, kk: (kk, j + b_off)))
        if b_sh is not None:
            assert b_sh % tn == 0, (name, b_sh, tn)
            per = b_sh // tn
            b_spec = pl.BlockSpec((None, tk, tn), at(lambda i, j, kk: (j // per, kk, j % per)))
    elif mode == "nt":
        a_spec = pl.BlockSpec((tm, tk), at(lambda i, j, kk: (i, kk + a_off)))
        b_spec = pl.BlockSpec((tn, tk), at(lambda i, j, kk: (j, kk + b_off)))
        if b_sh is not None:
            assert b_sh % tk == 0, (name, b_sh, tk)
            per = b_sh // tk
            b_spec = pl.BlockSpec((None, tn, tk), at(lambda i, j, kk: (kk // per, j, kk % per)))
    else:
        a_spec = pl.BlockSpec((tk, tm), at(lambda i, j, kk: (kk, i + a_off)))
        b_spec = pl.BlockSpec((tk, tn), at(lambda i, j, kk: (kk, j + b_off)))
    a_specs = [a_spec] if a_fn is None else [pl.BlockSpec(a_spec.block_shape, at(f)) for _, f in a_ops]
    b_specs = [b_spec] if b_fn is None else [pl.BlockSpec(b_spec.block_shape, at(f)) for _, f in b_ops]
    in_specs, args = a_specs + b_specs, [arr for arr, _ in a_ops] + [arr for arr, _ in b_ops]
    if has_res:
        in_specs.append(pl.BlockSpec((tm, tn), at(lambda i, j, kk: (i, j))))
        args.append(res)
    if has_bias:
        in_specs.append(pl.BlockSpec((1, tn), at(lambda i, j, kk: (0, j))))
        args.append(bias)
    out_spec, out_shape = pl.BlockSpec((tm, tn), at(lambda i, j, kk: (i, j))), (m, n)
    if o_sh is not None:
        assert o_sh % tn == 0, (name, o_sh, tn)
        per_o = o_sh // tn
        out_spec = pl.BlockSpec((None, tm, tn), at(lambda i, j, kk: (j // per_o, i, j % per_o)))
        out_shape = (n // o_sh, m, o_sh)
    scratch = [pltpu.VMEM((tm, tn), F32)] if nk > 1 else []
    if a_keep:
        assert comm is None and o_sh is None and mode != "tn" and n == tn, name
        outs = pl.pallas_call(
            body, out_shape=[jax.ShapeDtypeStruct(out_shape, out_dtype), jax.ShapeDtypeStruct((m, k), BF16)], grid=grid,
            in_specs=in_specs, out_specs=[out_spec, pl.BlockSpec((tm, tk), at(lambda i, j, kk: (i, kk)))],
            scratch_shapes=scratch, name=name, compiler_params=_params(("parallel", "parallel", "arbitrary")))(*args)
        return outs[0], outs[1]
    if comm is None:
        return pl.pallas_call(
            body, out_shape=jax.ShapeDtypeStruct(out_shape, out_dtype), grid=grid, in_specs=in_specs,
            out_specs=out_spec, scratch_shapes=scratch, name=name,
            compiler_params=_params(("parallel", "parallel", "arbitrary")))(*args)
    outs = pl.pallas_call(
        body, out_shape=[jax.ShapeDtypeStruct(out_shape, out_dtype)] + comm["out_shapes"], grid=grid,
        in_specs=in_specs + [HBM] * ncomm, out_specs=[out_spec] + [HBM] * ncomm,
        scratch_shapes=scratch + comm["sems"], name=name,
        compiler_params=_params(("arbitrary", "arbitrary", "arbitrary")))(*args, *comm["ins"])
    return outs[0], outs[1:]


def _sig(v):
    return jax.nn.sigmoid(v)


def _rms_fwd(x, g):
    r = lax.rsqrt(jnp.mean(x * x, axis=-1, keepdims=True) + EPS)
    return x * r * g


def _rms_bwd(dh, x, dres, g):
    dh = dh.astype(F32)
    r = lax.rsqrt(jnp.mean(x * x, axis=-1, keepdims=True) + EPS)
    xh = x * r
    dxh = dh * g
    dx = r * (dxh - xh * jnp.mean(dxh * xh, axis=-1, keepdims=True))
    return dres + dx, jnp.sum(dh * xh, axis=0, keepdims=True)


def _rope_apply(t, c, sa, sb):
    w = t.shape[1]
    return t * c + pltpu.roll(t, w - ROPE_HALF, 1) * sa + pltpu.roll(t, ROPE_HALF, 1) * sb


def _rope_transpose(g, c, sa, sb):
    w = g.shape[1]
    return g * c + pltpu.roll(g * sa, ROPE_HALF, 1) + pltpu.roll(g * sb, w - ROPE_HALF, 1)


def _tile_lanes(tab, reps):
    return jnp.concatenate([tab] * reps, axis=1) if reps > 1 else tab


def _rope_fwd(q, k, v, c, sa, sb):
    rq = Q_WIDTH // c.shape[1]
    qr = _rope_apply(q.astype(F32), _tile_lanes(c, rq), _tile_lanes(sa, rq), _tile_lanes(sb, rq))
    kr = _rope_apply(k.astype(F32), c, sa, sb)
    return jnp.concatenate([qr, kr, v.astype(F32)], axis=1)


def _rope_bwd_q(g, c, sa, sb):
    rq = Q_WIDTH // c.shape[1]
    return _rope_transpose(g.astype(F32), _tile_lanes(c, rq), _tile_lanes(sa, rq), _tile_lanes(sb, rq))


def _gelu(v):
    return jax.nn.gelu(v, approximate=True)


def _gelu_grad(v):
    c0 = math.sqrt(2.0 / math.pi)
    inner = c0 * (v + 0.044715 * v * v * v)
    th = jnp.tanh(inner)
    return 0.5 * (1.0 + th) + 0.5 * v * (1.0 - th * th) * c0 * (1.0 + 3 * 0.044715 * v * v)


def _merge_fwd(g0, g1, g2, b0, b1, b2, ya, ga, gb, yc):
    s0 = _sig(g0.astype(F32) + b0)
    s1 = _sig(g1.astype(F32) + b1)
    s2 = _sig(g2.astype(F32) + b2)
    ys = ga.astype(F32) * _sig(gb.astype(F32))
    return s0 * ya.astype(F32) + s1 * ys + s2 * yc.astype(F32)


def _merge_bwd(dm, g0, g1, g2, b0, b1, b2, ya, ga, gb, yc):
    dm = dm.astype(F32)
    s0 = _sig(g0.astype(F32) + b0)
    s1 = _sig(g1.astype(F32) + b1)
    s2 = _sig(g2.astype(F32) + b2)
    ga = ga.astype(F32)
    sb = _sig(gb.astype(F32))
    ys = ga * sb
    dya = dm * s0
    dys = dm * s1
    dyc = dm * s2
    dga = dys * sb
    dgb = dys * ga * sb * (1.0 - sb)
    d0 = dm * ya.astype(F32) * s0 * (1.0 - s0)
    d1 = dm * ys * s1 * (1.0 - s1)
    d2 = dm * yc.astype(F32) * s2 * (1.0 - s2)
    cs = lambda v: jnp.sum(v, axis=0, keepdims=True)
    return dya, dga, dgb, dyc, d0, d1, d2, cs(d0), cs(d1), cs(d2), cs(dga), cs(dgb)


def _ffn_act(fg, fu):
    fg = fg.astype(F32)
    return fg * _sig(fg) * fu.astype(F32)


def _ffn_act_bwd(da, fg, fu):
    da, fg, fu = da.astype(F32), fg.astype(F32), fu.astype(F32)
    s = _sig(fg)
    return da * fu * (s * (1.0 + fg * (1.0 - s))), da * fg * s


def _ple_fwd(x, gp, e):
    return x + _sig(gp.astype(F32)) * e.astype(F32)


def _ple_bwd(dx, gp, e):
    s = _sig(gp.astype(F32))
    e = e.astype(F32)
    return dx * s, dx * e * s * (1.0 - s)


def _loss_fn(x, tgt, g):
    d = x.shape[1]
    r = lax.rsqrt(jnp.mean(x * x, axis=-1, keepdims=True) + EPS)
    xh = x * r
    err = xh * g - tgt
    dy = err * (1.0 / d)
    dxh = dy * g
    dx = r * (dxh - xh * jnp.mean(dxh * xh, axis=-1, keepdims=True))
    return dx, jnp.sum(err * err, axis=0, keepdims=True) * (0.5 / d), jnp.sum(dy * xh, axis=0, keepdims=True)


def _adamw_fn(w, g, m, v):
    m = ADAM_B1 * m + (1.0 - ADAM_B1) * g
    v = ADAM_B2 * v + (1.0 - ADAM_B2) * (g * g)
    m_hat = m / (1.0 - ADAM_B1 ** ADAM_STEP)
    v_hat = v / (1.0 - ADAM_B2 ** ADAM_STEP)
    delta = -ADAM_LR * (m_hat / (jnp.sqrt(v_hat) + ADAM_EPS) + ADAM_WD * w)
    return delta, m, v


def _band_mask(n):
    qi = lax.broadcasted_iota(jnp.int32, (ATT_BLOCK, 2 * ATT_BLOCK), 0)
    kj = lax.broadcasted_iota(jnp.int32, (ATT_BLOCK, 2 * ATT_BLOCK), 1)
    dist = qi + ATT_BLOCK - kj
    return (dist >= 0) & (dist < ATT_BLOCK) & ((n > 0) | (kj >= ATT_BLOCK))


K_HEADS_AT = N_Q_HEADS // N_KV_HEADS


def _att_specs(nb):
    qs = pl.BlockSpec((N_Q_HEADS, ATT_BLOCK, HEAD_DIM), lambda b, n: (0, b * nb + n, 0))

    def kv(head_block, back):
        return pl.BlockSpec((N_KV_HEADS, ATT_BLOCK, HEAD_DIM),
                            lambda b, n: (head_block, b * nb + jnp.maximum(n - back, 0), 0))

    stat = pl.BlockSpec((N_Q_HEADS, ATT_BLOCK, 1), lambda b, n: (0, b * nb + n, 0))
    sink = pl.BlockSpec((N_Q_HEADS, 1, 1), lambda b, n: (0, 0, 0))
    return qs, [kv(K_HEADS_AT, 1), kv(K_HEADS_AT, 0), kv(K_HEADS_AT + 1, 1), kv(K_HEADS_AT + 1, 0)], stat, sink


def _attn_fwd(qkv, sinks, nbatch, seq):
    t = qkv.shape[1]
    nb = seq // ATT_BLOCK
    qs, kvs, stat, sink = _att_specs(nb)

    def body(q_ref, kp_ref, kc_ref, vp_ref, vc_ref, sink_ref, o_ref, lse_ref):
        mask = _band_mask(pl.program_id(1))
        rows = GQA_GROUP * ATT_BLOCK
        for kv in range(N_KV_HEADS):
            hs = slice(kv * GQA_GROUP, (kv + 1) * GQA_GROUP)
            kk = jnp.concatenate([kp_ref[kv], kc_ref[kv]], axis=0)
            vv = jnp.concatenate([vp_ref[kv], vc_ref[kv]], axis=0)
            q4 = (q_ref[hs] * SCALE).reshape(rows, HEAD_DIM)
            s = lax.dot_general(q4, kk, _DIMS["nt"], preferred_element_type=F32)
            s = jnp.where(mask, s.reshape(GQA_GROUP, ATT_BLOCK, 2 * ATT_BLOCK), NEG_INF)
            sk = sink_ref[hs]
            mx = jnp.maximum(jnp.max(s, axis=-1, keepdims=True), sk)
            p = jnp.exp(s - mx)
            den = jnp.sum(p, axis=-1, keepdims=True) + jnp.exp(sk - mx)
            o = lax.dot_general(p.reshape(rows, 2 * ATT_BLOCK).astype(BF16), vv, _DIMS["nn"],
                                preferred_element_type=F32).reshape(GQA_GROUP, ATT_BLOCK, HEAD_DIM)
            o_ref[hs] = (o * (1.0 / den)).astype(o_ref.dtype)
            lse_ref[hs] = mx + jnp.log(den)

    return pl.pallas_call(
        body, grid=(nbatch, nb), in_specs=[qs] + kvs + [sink], out_specs=[qs, stat],
        out_shape=[jax.ShapeDtypeStruct((N_Q_HEADS, t, HEAD_DIM), BF16), jax.ShapeDtypeStruct((N_Q_HEADS, t, 1), F32)],
        name="attn_fwd", compiler_params=_params(("parallel", "parallel")))(qkv, qkv, qkv, qkv, qkv, sinks)


def _attn_bwd(qkv, oh, doh, lse, sinks, nbatch, seq):
    t = qkv.shape[1]
    nb = seq // ATT_BLOCK
    qs, kvs, stat, sink = _att_specs(nb)

    def body(q_ref, kp_ref, kc_ref, vp_ref, vc_ref, o_ref, do_ref, lse_ref, sink_ref, dqkv_ref, dsink_ref):
        dq_ref = dqkv_ref.at[pl.ds(0, N_Q_HEADS)]
        dkc_ref, dvc_ref, dkp_ref, dvp_ref = (dqkv_ref.at[pl.ds(N_Q_HEADS + N_KV_HEADS * i, N_KV_HEADS)]
                                              for i in range(4))
        first = (pl.program_id(0) == 0) & (pl.program_id(1) == 0)

        @pl.when(first)
        def _():
            dsink_ref[...] = jnp.zeros_like(dsink_ref)

        mask = _band_mask(pl.program_id(1))
        rows = GQA_GROUP * ATT_BLOCK
        band = (GQA_GROUP, ATT_BLOCK, 2 * ATT_BLOCK)
        for kv in range(N_KV_HEADS):
            hs = slice(kv * GQA_GROUP, (kv + 1) * GQA_GROUP)
            kk = jnp.concatenate([kp_ref[kv], kc_ref[kv]], axis=0)
            vv = jnp.concatenate([vp_ref[kv], vc_ref[kv]], axis=0)
            q4 = q_ref[hs].reshape(rows, HEAD_DIM)
            do4 = do_ref[hs].reshape(rows, HEAD_DIM)
            lse4 = lse_ref[hs]
            s = lax.dot_general(q4 * SCALE, kk, _DIMS["nt"], preferred_element_type=F32).reshape(band)
            p = jnp.where(mask, jnp.exp(s - lse4), 0.0)
            dd = jnp.sum(do_ref[hs].astype(F32) * o_ref[hs].astype(F32), axis=-1, keepdims=True)
            dp = lax.dot_general(do4, vv, _DIMS["nt"], preferred_element_type=F32).reshape(band)
            ds = (p * (dp - dd) * SCALE).astype(BF16).reshape(rows, 2 * ATT_BLOCK)
            dq = lax.dot_general(ds, kk, _DIMS["nn"], preferred_element_type=F32)
            dq_ref[hs] = dq.reshape(GQA_GROUP, ATT_BLOCK, HEAD_DIM).astype(dq_ref.dtype)
            dk = lax.dot_general(ds, q4, _DIMS["tn"], preferred_element_type=F32)
            dv = lax.dot_general(p.astype(BF16).reshape(rows, 2 * ATT_BLOCK), do4, _DIMS["tn"],
                                 preferred_element_type=F32)
            dsink_ref[hs] += -jnp.sum(jnp.exp(sink_ref[hs] - lse4) * dd, axis=1, keepdims=True)
            dkp_ref[kv] = dk[:ATT_BLOCK]
            dkc_ref[kv] = dk[ATT_BLOCK:]
            dvp_ref[kv] = dv[:ATT_BLOCK]
            dvc_ref[kv] = dv[ATT_BLOCK:]

    n_out = 2 * N_Q_HEADS
    return pl.pallas_call(
        body, grid=(nbatch, nb), in_specs=[qs] + kvs + [qs, qs, stat, sink],
        out_specs=[pl.BlockSpec((n_out, ATT_BLOCK, HEAD_DIM), lambda b, n: (0, b * nb + n, 0)), sink],
        out_shape=[jax.ShapeDtypeStruct((n_out, t, HEAD_DIM), F32), jax.ShapeDtypeStruct((N_Q_HEADS, 1, 1), F32)],
        name="attn_bwd", compiler_params=_params(("arbitrary", "arbitrary")))(
        qkv, qkv, qkv, qkv, qkv, oh, doh, lse, sinks)


def _kv_combine(dqkv, c, sa, sb, seq):
    t = dqkv.shape[0]
    nb = seq // ATT_BLOCK
    nblk = t // ATT_BLOCK
    col0 = Q_WIDTH // KV_WIDTH

    def body(kc_ref, kp_ref, vc_ref, vp_ref, c_ref, sa_ref, sb_ref, dk_ref, dv_ref):
        has_next = (pl.program_id(0) % nb) != nb - 1
        dk = kc_ref[...] + jnp.where(has_next, kp_ref[...], 0.0)
        dv = vc_ref[...] + jnp.where(has_next, vp_ref[...], 0.0)
        dk_ref[...] = _rope_transpose(dk, c_ref[...], sa_ref[...], sb_ref[...]).astype(dk_ref.dtype)
        dv_ref[...] = dv.astype(dv_ref.dtype)

    cur = pl.BlockSpec((ATT_BLOCK, KV_WIDTH), lambda i: (i, 0))
    own = lambda col: pl.BlockSpec((ATT_BLOCK, KV_WIDTH), lambda i: (i, col0 + col))
    nxt = lambda col: pl.BlockSpec((ATT_BLOCK, KV_WIDTH), lambda i: (jnp.minimum(i + 1, nblk - 1), col0 + col))
    o = jax.ShapeDtypeStruct((t, KV_WIDTH), BF16)
    return pl.pallas_call(body, grid=(nblk,), in_specs=[own(0), nxt(2), own(1), nxt(3), cur, cur, cur],
                          out_specs=[cur, cur], out_shape=[o, o], name="kv_combine",
                          compiler_params=_params(("parallel",)))(dqkv, dqkv, dqkv, dqkv, c, sa, sb)


def _scan_block(ref, tab_ref, carry, ngroups, reverse):
    shifts = (7, 6, 4) if reverse else (1, 2, 4)
    n = SSM_LANES

    def step(i, car):
        g = (ngroups - 1 - i) if reverse else i
        r0 = pl.multiple_of(g * SUBLANES, SUBLANES)
        xr = ref[pl.ds(r0, SUBLANES), :n]
        xi = ref[pl.ds(r0, SUBLANES), n:]
        for s, sh in enumerate(shifts):
            pr, pi = tab_ref[2 * s], tab_ref[2 * s + 1]
            yr, yi = pltpu.roll(xr, sh, 0), pltpu.roll(xi, sh, 0)
            xr, xi = xr + pr * yr - pi * yi, xi + pr * yi + pi * yr
        cr, ci = car
        qr, qi = tab_ref[6], tab_ref[7]
        xr, xi = xr + qr * cr - qi * ci, xi + qr * ci + qi * cr
        ref[pl.ds(r0, SUBLANES), :n] = xr
        ref[pl.ds(r0, SUBLANES), n:] = xi
        last = r0 if reverse else r0 + SUBLANES - 1
        return ref[pl.ds(last, 1), :n], ref[pl.ds(last, 1), n:]

    return lax.fori_loop(0, ngroups, step, carry, unroll=2)


def _ssm_chunk(seq):
    return min(512, seq)


def _ssm_fwd(z, wb, wc, tab, dskip, wglu, bglu, nbatch, seq):
    t = z.shape[0]
    tc = _ssm_chunk(seq)
    nc = seq // tc
    n2 = 2 * SSM_LANES
    nglu = wglu.shape[1]

    def body(u_ref, wb_ref, wc_ref, tab_ref, d_ref, wg_ref, bg_ref, st_ref, y_ref, gel_ref, glu_ref, car_ref):
        @pl.when(pl.program_id(1) == 0)
        def _():
            car_ref[...] = jnp.zeros_like(car_ref)

        u = u_ref[...]
        st_ref[...] = lax.dot_general(u, wb_ref[...], _DIMS["nn"], preferred_element_type=F32)
        cr, ci = _scan_block(st_ref, tab_ref, (car_ref[:, :SSM_LANES], car_ref[:, SSM_LANES:]), tc // SUBLANES, False)
        car_ref[:, :SSM_LANES] = cr
        car_ref[:, SSM_LANES:] = ci
        y = lax.dot_general(st_ref[...].astype(BF16), wc_ref[...], _DIMS["nn"], preferred_element_type=F32)
        y = y + d_ref[...] * u.astype(F32)
        y_ref[...] = y
        gel = _gelu(y).astype(BF16)
        gel_ref[...] = gel
        glu = lax.dot_general(gel, wg_ref[...], _DIMS["nn"], preferred_element_type=F32) + bg_ref[...]
        glu_ref[...] = glu.astype(glu_ref.dtype)

    row = lambda b, c: (b * nc + c, 0)
    full = lambda b, c: (0, 0)
    return pl.pallas_call(
        body, grid=(nbatch, nc),
        in_specs=[pl.BlockSpec((tc, SSM_WIDTH), lambda b, c: (b * nc + c, 3)), pl.BlockSpec((SSM_WIDTH, n2), full),
                  pl.BlockSpec((n2, SSM_WIDTH), full), pl.BlockSpec((8, SUBLANES, SSM_LANES), lambda b, c: (0, 0, 0)),
                  pl.BlockSpec((1, SSM_WIDTH), full), pl.BlockSpec((SSM_WIDTH, nglu), full),
                  pl.BlockSpec((1, nglu), full)],
        out_specs=[pl.BlockSpec((tc, n2), row), pl.BlockSpec((tc, SSM_WIDTH), row), pl.BlockSpec((tc, SSM_WIDTH), row),
                   pl.BlockSpec((tc, nglu), row)],
        out_shape=[jax.ShapeDtypeStruct((t, n2), F32), jax.ShapeDtypeStruct((t, SSM_WIDTH), F32),
                   jax.ShapeDtypeStruct((t, SSM_WIDTH), BF16), jax.ShapeDtypeStruct((t, nglu), BF16)],
        scratch_shapes=[pltpu.VMEM((1, n2), F32)], name="ssm_fwd",
        compiler_params=_params(("arbitrary", "arbitrary")))(z, wb, wc, tab, dskip, wglu, bglu)


def _ssm_bwd(dgi, ys, st, z, wbt, wct, tab_rev, dskip, nbatch, seq):
    t = z.shape[0]
    tc = _ssm_chunk(seq)
    nc = seq // tc
    n = SSM_LANES
    n2 = 2 * n
    ng = tc // SUBLANES

    def body(dgi_ref, ys_ref, st_ref, stp_ref, u_ref, wbt_ref, wct_ref, tab_ref, d_ref,
             du_ref, dwb_ref, dwc_ref, dd_ref, da_ref, p_ref, sb_ref, car_ref):
        b, c = pl.program_id(0), pl.program_id(1)
        ct = nc - 1 - c

        @pl.when((b == 0) & (c == 0))
        def _():
            dwb_ref[...] = jnp.zeros_like(dwb_ref)
            dwc_ref[...] = jnp.zeros_like(dwc_ref)
            dd_ref[...] = jnp.zeros_like(dd_ref)
            da_ref[...] = jnp.zeros_like(da_ref)

        @pl.when(c == 0)
        def _():
            car_ref[...] = jnp.zeros_like(car_ref)

        u = u_ref[...]
        dys = dgi_ref[...].astype(F32) * _gelu_grad(ys_ref[...])
        dys_b = dys.astype(BF16)
        st = st_ref[...]
        dd_ref[...] += jnp.sum(dys * u.astype(F32), axis=0, keepdims=True)
        dwc_ref[...] += lax.dot_general(st.astype(BF16), dys_b, _DIMS["tn"], preferred_element_type=F32)
        p_ref[...] = lax.dot_general(dys_b, wct_ref[...], _DIMS["nn"], preferred_element_type=F32)
        cr, ci = _scan_block(p_ref, tab_ref, (car_ref[:, :n], car_ref[:, n:]), ng, True)
        car_ref[:, :n] = cr
        car_ref[:, n:] = ci
        p = p_ref[...]
        pb = p.astype(BF16)
        dwb_ref[...] += lax.dot_general(u, pb, _DIMS["tn"], preferred_element_type=F32)
        du = lax.dot_general(pb, wbt_ref[...], _DIMS["nn"], preferred_element_type=F32) + d_ref[...] * dys
        du_ref[...] = du.astype(du_ref.dtype)
        sb_ref[pl.ds(0, SUBLANES), :] = jnp.where(ct > 0, stp_ref[...], 0.0)
        sb_ref[pl.ds(SUBLANES, tc), :] = st
        row0 = lax.broadcasted_iota(jnp.int32, (SUBLANES, n), 0) == 0

        def acc_step(g, acc):
            ar, ai = acc
            r0 = pl.multiple_of(g * SUBLANES, SUBLANES)
            edge_r = sb_ref[pl.ds(r0 + SUBLANES - 1, 1), :n]
            edge_i = sb_ref[pl.ds(r0 + SUBLANES - 1, 1), n:]
            sr = jnp.where(row0, edge_r, pltpu.roll(sb_ref[pl.ds(r0 + SUBLANES, SUBLANES), :n], 1, 0))
            si = jnp.where(row0, edge_i, pltpu.roll(sb_ref[pl.ds(r0 + SUBLANES, SUBLANES), n:], 1, 0))
            pr = p_ref[pl.ds(r0, SUBLANES), :n]
            pi = p_ref[pl.ds(r0, SUBLANES), n:]
            return ar + pr * sr + pi * si, ai + pi * sr - pr * si

        zero = jnp.zeros((SUBLANES, n), F32)
        ar, ai = lax.fori_loop(0, ng, acc_step, (zero, zero), unroll=2)
        da_ref[:, :n] += ar
        da_ref[:, n:] += ai

    row = lambda b, c: (b * nc + (nc - 1 - c), 0)
    prev8 = lambda b, c: (jnp.maximum((b * nc + (nc - 1 - c)) * (tc // SUBLANES) - 1, 0), 0)
    full = lambda b, c: (0, 0)
    return pl.pallas_call(
        body, grid=(nbatch, nc),
        in_specs=[pl.BlockSpec((tc, SSM_WIDTH), row), pl.BlockSpec((tc, SSM_WIDTH), row), pl.BlockSpec((tc, n2), row),
                  pl.BlockSpec((SUBLANES, n2), prev8),
                  pl.BlockSpec((tc, SSM_WIDTH), lambda b, c: (b * nc + (nc - 1 - c), 3)),
                  pl.BlockSpec((n2, SSM_WIDTH), full), pl.BlockSpec((SSM_WIDTH, n2), full),
                  pl.BlockSpec((8, SUBLANES, n), lambda b, c: (0, 0, 0)), pl.BlockSpec((1, SSM_WIDTH), full)],
        out_specs=[pl.BlockSpec((tc, SSM_WIDTH), row), pl.BlockSpec((SSM_WIDTH, n2), full),
                   pl.BlockSpec((n2, SSM_WIDTH), full), pl.BlockSpec((1, SSM_WIDTH), full),
                   pl.BlockSpec((SUBLANES, n2), full)],
        out_shape=[jax.ShapeDtypeStruct((t, SSM_WIDTH), BF16), jax.ShapeDtypeStruct((SSM_WIDTH, n2), F32),
                   jax.ShapeDtypeStruct((n2, SSM_WIDTH), F32), jax.ShapeDtypeStruct((1, SSM_WIDTH), F32),
                   jax.ShapeDtypeStruct((SUBLANES, n2), F32)],
        scratch_shapes=[pltpu.VMEM((tc, n2), F32), pltpu.VMEM((tc + SUBLANES, n2), F32), pltpu.VMEM((1, n2), F32)],
        name="ssm_bwd", compiler_params=_params(("arbitrary", "arbitrary")))(
        dgi, ys, st, st, z, wbt, wct, tab_rev, dskip)


def _ssm_prep(lam_re, lam_im, log_dt, b_re, b_im, c_re, c_im):
    lr = jnp.minimum(lam_re, -1e-4)
    li = lam_im
    dt = jnp.exp(log_dt)[:, None]
    mag = jnp.exp(lr * dt)
    a_re = mag * jnp.cos(li * dt)
    a_im = mag * jnp.sin(li * dt)
    den = lr * lr + li * li
    x_re, x_im = a_re - 1.0, a_im
    f_re = (x_re * lr + x_im * li) / den
    f_im = (x_im * lr - x_re * li) / den
    bb_re = f_re[..., None] * b_re - f_im[..., None] * b_im
    bb_im = f_re[..., None] * b_im + f_im[..., None] * b_re
    eye = jnp.eye(SSM_GROUPS, dtype=F32)
    emb_b = lambda v: jnp.einsum("gnh,gk->ghkn", v, eye).reshape(SSM_WIDTH, SSM_LANES)
    emb_c = lambda v: jnp.einsum("ghn,gk->gnkh", v, eye).reshape(SSM_LANES, SSM_WIDTH)
    wb = jnp.concatenate([emb_b(bb_re), emb_b(bb_im)], axis=1)
    wc = jnp.concatenate([emb_c(c_re), -emb_c(c_im)], axis=0)
    return a_re.reshape(-1), a_im.reshape(-1), wb, wc


def _ssm_tables(a_re, a_im, reverse):
    if reverse:
        a_im = -a_im
    pw = [(a_re, a_im)]
    for _ in range(SUBLANES - 1):
        pr, pi = pw[-1]
        pw.append((pr * a_re - pi * a_im, pr * a_im + pi * a_re))
    rows = jnp.arange(SUBLANES)[:, None]
    tabs = []
    for k in (1, 2, 4):
        ok = (rows + k <= SUBLANES - 1) if reverse else (rows >= k)
        tabs += [jnp.where(ok, pw[k - 1][0][None], 0.0), jnp.where(ok, pw[k - 1][1][None], 0.0)]
    order = list(range(SUBLANES - 1, -1, -1)) if reverse else list(range(SUBLANES))
    tabs += [jnp.stack([pw[i][0] for i in order]), jnp.stack([pw[i][1] for i in order])]
    return jnp.stack(tabs)


def _conv_chunk(seq):
    return min(512, seq)


def _shifted(buf, sh, tc, offsets):
    for b in range(SUBLANES):
        idx = [i for i, o in enumerate(offsets) if o % SUBLANES == b]
        if not idx:
            continue
        src = buf
        if b:
            span = tc + SUBLANES * max(offsets[i] // SUBLANES for i in idx)
            sh[pl.ds(0, span), :] = buf[pl.ds(b, span), :]
            src = sh
        for i in idx:
            yield i, src[pl.ds(offsets[i] // SUBLANES * SUBLANES, tc), :]


def _conv_fwd(z, w, bias, lg, lb, w_out, nbatch, seq):
    t = z.shape[0]
    tc = _conv_chunk(seq)
    nc = seq // tc
    nout = w_out.shape[1]

    def body(a_ref, g_ref, w_ref, b_ref, lg_ref, lb_ref, wo_ref, cv_ref, sc_ref, yc_ref, ubuf, sh):
        c = pl.program_id(1)

        @pl.when(c == 0)
        def _():
            ubuf[pl.ds(0, CONV_HALO), :] = jnp.zeros((CONV_HALO, CONV_WIDTH), F32)

        @pl.when(c > 0)
        def _():
            ubuf[pl.ds(0, CONV_HALO), :] = ubuf[pl.ds(tc, CONV_HALO), :]

        ubuf[pl.ds(CONV_HALO, tc), :] = a_ref[...].astype(F32) * _sig(g_ref[...].astype(F32))
        acc = jnp.zeros((tc, CONV_WIDTH), F32) + b_ref[...]
        for k, win in _shifted(ubuf, sh, tc, [CONV_HALO - (CONV_K - 1) + k for k in range(CONV_K)]):
            acc = acc + w_ref[pl.ds(k, 1), :] * win
        cv_ref[...] = acc
        mu = jnp.mean(acc, axis=-1, keepdims=True)
        xc = acc - mu
        y = xc * lax.rsqrt(jnp.mean(xc * xc, axis=-1, keepdims=True) + EPS) * lg_ref[...] + lb_ref[...]
        sc = (y * _sig(y)).astype(BF16)
        sc_ref[...] = sc
        yc_ref[...] = lax.dot_general(sc, wo_ref[...], _DIMS["nn"], preferred_element_type=F32).astype(yc_ref.dtype)

    row = lambda b, c: (b * nc + c, 0)
    full = lambda b, c: (0, 0)
    vec = pl.BlockSpec((1, CONV_WIDTH), full)
    return pl.pallas_call(
        body, grid=(nbatch, nc),
        in_specs=[pl.BlockSpec((tc, CONV_WIDTH), lambda b, c: (b * nc + c, 4)),
                  pl.BlockSpec((tc, CONV_WIDTH), lambda b, c: (b * nc + c, 5)),
                  pl.BlockSpec((CONV_HALO, CONV_WIDTH), full), vec, vec, vec, pl.BlockSpec((CONV_WIDTH, nout), full)],
        out_specs=[pl.BlockSpec((tc, CONV_WIDTH), row), pl.BlockSpec((tc, CONV_WIDTH), row),
                   pl.BlockSpec((tc, nout), row)],
        out_shape=[jax.ShapeDtypeStruct((t, CONV_WIDTH), F32), jax.ShapeDtypeStruct((t, CONV_WIDTH), BF16),
                   jax.ShapeDtypeStruct((t, nout), BF16)],
        scratch_shapes=[pltpu.VMEM((CONV_HALO + tc, CONV_WIDTH), F32)] * 2, name="conv_fwd",
        compiler_params=_params(("arbitrary", "arbitrary")))(z, z, w, bias, lg, lb, w_out)


def _conv_bwd(dsc, cv, z, w, lg, lb, nbatch, seq):
    t = z.shape[0]
    tc = _conv_chunk(seq)
    nc = seq // tc
    hb = tc // CONV_HALO

    def body(dsc_ref, cv_ref, a_ref, g_ref, ap_ref, gp_ref, w_ref, lg_ref, lb_ref,
             da_ref, dg_ref, dw_ref, db_ref, dlg_ref, dlb_ref, ubuf, dbuf, sh):
        b, c = pl.program_id(0), pl.program_id(1)
        ct = nc - 1 - c

        @pl.when((b == 0) & (c == 0))
        def _():
            dw_ref[...] = jnp.zeros_like(dw_ref)
            db_ref[...] = jnp.zeros_like(db_ref)
            dlg_ref[...] = jnp.zeros_like(dlg_ref)
            dlb_ref[...] = jnp.zeros_like(dlb_ref)

        cvv = cv_ref[...]
        mu = jnp.mean(cvv, axis=-1, keepdims=True)
        xc = cvv - mu
        rstd = lax.rsqrt(jnp.mean(xc * xc, axis=-1, keepdims=True) + EPS)
        xh = xc * rstd
        y = xh * lg_ref[...] + lb_ref[...]
        sy = _sig(y)
        dy = dsc_ref[...].astype(F32) * (sy * (1.0 + y * (1.0 - sy)))
        dlg_ref[...] += jnp.sum(dy * xh, axis=0, keepdims=True)
        dlb_ref[...] += jnp.sum(dy, axis=0, keepdims=True)
        dxh = dy * lg_ref[...]
        dcv = rstd * (dxh - jnp.mean(dxh, axis=-1, keepdims=True) - xh * jnp.mean(dxh * xh, axis=-1, keepdims=True))
        db_ref[...] += jnp.sum(dcv, axis=0, keepdims=True)

        @pl.when(c == 0)
        def _():
            dbuf[pl.ds(tc, CONV_HALO), :] = jnp.zeros((CONV_HALO, CONV_WIDTH), F32)

        @pl.when(c > 0)
        def _():
            dbuf[pl.ds(tc, CONV_HALO), :] = dbuf[pl.ds(0, CONV_HALO), :]

        dbuf[pl.ds(0, tc), :] = dcv
        a = a_ref[...].astype(F32)
        sg = _sig(g_ref[...].astype(F32))
        ubuf[pl.ds(0, CONV_HALO), :] = jnp.where(ct > 0, ap_ref[...].astype(F32) * _sig(gp_ref[...].astype(F32)), 0.0)
        ubuf[pl.ds(CONV_HALO, tc), :] = a * sg
        du = jnp.zeros((tc, CONV_WIDTH), F32)
        for k, win in _shifted(dbuf, sh, tc, [CONV_K - 1 - k for k in range(CONV_K)]):
            du = du + w_ref[pl.ds(k, 1), :] * win
        for k, win in _shifted(ubuf, sh, tc, [CONV_HALO - (CONV_K - 1) + k for k in range(CONV_K)]):
            dw_ref[pl.ds(k, 1), :] += jnp.sum(dcv * win, axis=0, keepdims=True)
        da_ref[...] = (du * sg).astype(da_ref.dtype)
        dg_ref[...] = (du * a * sg * (1.0 - sg)).astype(dg_ref.dtype)

    row = lambda b, c: (b * nc + (nc - 1 - c), 0)
    full = lambda b, c: (0, 0)
    vec = pl.BlockSpec((1, CONV_WIDTH), full)
    blk = pl.BlockSpec((tc, CONV_WIDTH), row)

    def zcol(col):
        return pl.BlockSpec((tc, CONV_WIDTH), lambda b, c: (b * nc + (nc - 1 - c), col))

    def zprev(col):
        return pl.BlockSpec((CONV_HALO, CONV_WIDTH),
                            lambda b, c: (jnp.maximum((b * nc + (nc - 1 - c)) * hb - 1, 0), col))

    o = jax.ShapeDtypeStruct((t, CONV_WIDTH), BF16)
    v = jax.ShapeDtypeStruct((1, CONV_WIDTH), F32)
    return pl.pallas_call(
        body, grid=(nbatch, nc),
        in_specs=[blk, blk, zcol(4), zcol(5), zprev(4), zprev(5), pl.BlockSpec((CONV_HALO, CONV_WIDTH), full), vec, vec],
        out_specs=[blk, blk, pl.BlockSpec((CONV_HALO, CONV_WIDTH), full), vec, vec, vec],
        out_shape=[o, o, jax.ShapeDtypeStruct((CONV_HALO, CONV_WIDTH), F32), v, v, v],
        scratch_shapes=[pltpu.VMEM((CONV_HALO + tc, CONV_WIDTH), F32)] * 3, name="conv_bwd", compiler_params=_params(("arbitrary", "arbitrary")))(dsc, cv, z, z, z, z, w, lg, lb)


BIG = ("w_in", "w_attn_out", "w_ssm_glu", "w_conv_out", "w_mix_out", "w_ffn_in", "w_ffn_out", "w_ple_in", "w_ple_gate")
BIG_AXIS = {"w_in": 2, "w_attn_out": 2, "w_ssm_glu": 2, "w_conv_out": 2, "w_mix_out": 1, "w_ffn_in": 2,
            "w_ffn_out": 1, "w_ple_in": 2, "w_ple_gate": 1}
SHARD_MAJOR = ("w_in", "w_ffn_in")
SMALL = ("mix_norm_g", "b_gate", "attn_sinks", "ssm_lambda_re", "ssm_lambda_im", "ssm_log_dt", "ssm_b_re", "ssm_b_im",
         "ssm_c_re", "ssm_c_im", "ssm_d", "b_ssm_glu", "conv_dw_w", "conv_dw_b", "conv_norm_g", "conv_norm_b",
         "ffn_norm_g", "ple_norm_g", "final_norm_g")
WEIGHTS = ("mix_norm_g", "w_in", "b_gate", "attn_sinks", "w_attn_out", "ssm_lambda_re", "ssm_lambda_im", "ssm_log_dt",
           "ssm_b_re", "ssm_b_im", "ssm_c_re", "ssm_c_im", "ssm_d", "w_ssm_glu", "b_ssm_glu", "conv_dw_w", "conv_dw_b",
           "conv_norm_g", "conv_norm_b", "w_conv_out", "w_mix_out", "ffn_norm_g", "w_ffn_in", "w_ffn_out", "w_ple_in",
           "ple_norm_g", "w_ple_gate", "final_norm_g")
SSM_NAMES = ("ssm_lambda_re", "ssm_lambda_im", "ssm_log_dt", "ssm_b_re", "ssm_b_im", "ssm_c_re", "ssm_c_im")


def _ple_block(x, p_l, g, w_in, w_gate):
    t, d = x.shape
    kp = p_l.shape[1]
    tm = min(512, t)

    def body(x_ref, p_ref, g_ref, wi_ref, wg_ref, o_ref, gp_ref, h_ref, e_ref):
        xv = x_ref[...]
        e = lax.dot_general(p_ref[...].astype(BF16), wi_ref[...], _DIMS["nn"], preferred_element_type=F32).astype(BF16)
        h = _rms_fwd(xv, g_ref[...]).astype(BF16)
        gp = lax.dot_general(h, wg_ref[...], _DIMS["nn"], preferred_element_type=F32).astype(BF16)
        o_ref[...] = _ple_fwd(xv, gp, e)
        gp_ref[...], h_ref[...], e_ref[...] = gp, h, e

    row = lambda width: pl.BlockSpec((tm, width), lambda i: (i, 0))
    full = lambda v: pl.BlockSpec(v.shape, lambda i: (0, 0))
    out = lambda dt: jax.ShapeDtypeStruct((t, d), dt)
    return pl.pallas_call(
        body, grid=(t // tm,), in_specs=[row(d), row(kp), full(g), full(w_in), full(w_gate)],
        out_specs=[row(d)] * 4, out_shape=[out(F32), out(BF16), out(BF16), out(BF16)], name="ple_block",
        compiler_params=_params(("parallel",)))(x, p_l, g, w_in, w_gate)


def _mix_out_block(merged, w_mix, x, g):
    t, d = x.shape
    tm = min(512, t)

    def body(m_ref, w_ref, x_ref, g_ref, o_ref, h_ref):
        x1 = x_ref[...] + lax.dot_general(m_ref[...], w_ref[...], _DIMS["nn"], preferred_element_type=F32)
        o_ref[...] = x1
        h_ref[...] = _rms_fwd(x1, g_ref[...]).astype(h_ref.dtype)

    row = pl.BlockSpec((tm, d), lambda i: (i, 0))
    full = lambda v: pl.BlockSpec(v.shape, lambda i: (0, 0))
    return pl.pallas_call(
        body, grid=(t // tm,), in_specs=[row, full(w_mix), row, full(g)], out_specs=[row, row],
        out_shape=[jax.ShapeDtypeStruct((t, d), F32), jax.ShapeDtypeStruct((t, d), BF16)], name="mix_out_block",
        compiler_params=_params(("parallel",)))(merged, w_mix, x, g)


def _ple_block_bwd(dx3, gp, e, x, g, w_gate):
    t, d = x.shape
    tm = min(512, t)

    def body(dx3_ref, gp_ref, e_ref, x_ref, g_ref, wg_ref, de_ref, dgp_ref, dx_ref, dg_ref):
        @pl.when(pl.program_id(0) == 0)
        def _():
            dg_ref[...] = jnp.zeros_like(dg_ref)

        dx3 = dx3_ref[...]
        de, dgp = _ple_bwd(dx3, gp_ref[...], e_ref[...])
        dgp = dgp.astype(BF16)
        de_ref[...] = de.astype(de_ref.dtype)
        dgp_ref[...] = dgp
        dh = lax.dot_general(dgp, wg_ref[...], _DIMS["nt"], preferred_element_type=F32).astype(BF16)
        dx, dg = _rms_bwd(dh, x_ref[...], dx3, g_ref[...])
        dx_ref[...] = dx
        dg_ref[...] += dg

    row = pl.BlockSpec((tm, d), lambda i: (i, 0))
    full = lambda v: pl.BlockSpec(v.shape, lambda i: (0, 0))
    out = lambda dt: jax.ShapeDtypeStruct((t, d), dt)
    return pl.pallas_call(
        body, grid=(t // tm,), in_specs=[row, row, row, row, full(g), full(w_gate)],
        out_specs=[row, row, row, pl.BlockSpec((1, d), lambda i: (0, 0))],
        out_shape=[out(BF16), out(BF16), out(F32), jax.ShapeDtypeStruct((1, d), F32)], name="ple_block_bwd",
        compiler_params=_params(("arbitrary",)))(dx3, gp, e, x, g, w_gate)


def _in_proj_bwd(dz, w_in4, x, dres, g):
    t, d = x.shape
    nsh, _, cc = w_in4.shape
    tm = min(512, t)

    def body(dz_ref, w_ref, x_ref, dres_ref, g_ref, dx_ref, dg_ref):
        @pl.when(pl.program_id(0) == 0)
        def _():
            dg_ref[...] = jnp.zeros_like(dg_ref)

        dh = jnp.zeros((tm, d), F32)
        for sh in range(nsh):
            dh = dh + lax.dot_general(dz_ref[:, sh * cc:(sh + 1) * cc], w_ref[sh], _DIMS["nt"],
                                      preferred_element_type=F32)
        dx, dg = _rms_bwd(dh.astype(BF16), x_ref[...], dres_ref[...], g_ref[...])
        dx_ref[...] = dx
        dg_ref[...] += dg

    row = lambda width: pl.BlockSpec((tm, width), lambda i: (i, 0))
    return pl.pallas_call(
        body, grid=(t // tm,),
        in_specs=[row(nsh * cc), pl.BlockSpec(w_in4.shape, lambda i: (0, 0, 0)), row(d), row(d),
                  pl.BlockSpec(g.shape, lambda i: (0, 0))],
        out_specs=[row(d), pl.BlockSpec((1, d), lambda i: (0, 0))],
        out_shape=[jax.ShapeDtypeStruct((t, d), F32), jax.ShapeDtypeStruct((1, d), F32)], name="in_proj_bwd",
        compiler_params=_params(("arbitrary",)))(dz, w_in4, x, dres, g)


def _heads(v, nh):
    return v.reshape(v.shape[0], nh, HEAD_DIM).transpose(1, 0, 2)


def _tokens(v):
    return v.transpose(1, 0, 2).reshape(v.shape[1], v.shape[0] * HEAD_DIM)


def _row(v):
    return v.reshape(1, -1)


def _layer_fwd(x, p_l, w, s, rope, nbatch, seq, next_shards=None):
    t = x.shape[0]
    tm = 512
    d = D_MODEL
    sv = {}
    sv["x"] = x
    h = _rowwise("rms_mix", _rms_fwd, [R(x), V(_row(s["mix_norm_g"]))], [O(d, BF16)], tm=tm)
    cs = {nm: w[nm].shape[2] for nm in SHARD_MAJOR}
    tb = 1024
    got = {}
    plan = None if next_shards is None else _gather_plan(next_shards, GATHER_A)
    z = _mm("mm_in", h, w["w_in"], "nn", BF16, m=t, n=N_CHIPS * cs["w_in"], k=d, tm=tb, tn=cs["w_in"], tk=d,
            b_sh=cs["w_in"], comm=plan)
    if plan is not None:
        z, outs = z
        got.update(zip(plan["names"], outs))
    sv["h"], sv["z"] = h, z
    c, sa, sb = rope
    qkv_w = Q_WIDTH + 2 * KV_WIDTH
    qkv = _rowwise("rope_fwd", _rope_fwd, [R(z, Q_WIDTH, 0), R(z, KV_WIDTH, 4), R(z, KV_WIDTH, 5), R(c), R(sa), R(sb)],
                   [O(qkv_w, BF16)], tm=tm)
    qkv = _heads(qkv, qkv_w // HEAD_DIM)
    sinks = s["attn_sinks"].reshape(N_Q_HEADS, 1, 1)
    oh, lse = _attn_fwd(qkv, sinks, nbatch, seq)
    o = _tokens(oh)
    ya = _mm("mm_attn_out", o, w["w_attn_out"], "nn", BF16, m=t, n=d, k=Q_WIDTH, tm=tb, tn=d, tk=Q_WIDTH)
    sv.update(qkv=qkv, oh=oh, lse=lse, o=o, ya=ya, sinks=sinks)
    ssm_args = [s[nm] for nm in SSM_NAMES]
    a_re, a_im, wb, wc = _ssm_prep(*ssm_args)
    dskip = _row(s["ssm_d"])
    st, ys, gel, glu = _ssm_fwd(z, wb.astype(BF16), wc.astype(BF16), _ssm_tables(a_re, a_im, False), dskip,
                                w["w_ssm_glu"], _row(s["b_ssm_glu"]), nbatch, seq)
    sv.update(st=st, ys=ys, gel=gel, glu=glu, a=(a_re, a_im), wb=wb, wc=wc, dskip=dskip)
    cw = jnp.pad(s["conv_dw_w"], ((0, CONV_HALO - CONV_K), (0, 0)))
    cv, sc, yc = _conv_fwd(z, cw, _row(s["conv_dw_b"]), _row(s["conv_norm_g"]), _row(s["conv_norm_b"]),
                           w["w_conv_out"], nbatch, seq)
    sv.update(cw=cw, cv=cv, sc=sc, yc=yc)
    bg = _row(s["b_gate"])
    merge_ins = [R(z, 512, 3), R(z, 512, 5), R(z, 512, 7), V(bg, 512, 0), V(bg, 512, 2), V(bg, 512, 4),
                 R(ya, 512, 0), R(glu, 512, 0), R(glu, 512, 2), R(yc, 512, 0)]
    merged = _rowwise("merge_fwd", _merge_fwd, merge_ins, [O(512, BF16, total=d)], tm=tm, ncol=2)
    x1, hf = _mix_out_block(merged, w["w_mix_out"], x, _row(s["ffn_norm_g"]))
    sv.update(merged=merged, x1=x1)
    plan = None if next_shards is None else _gather_plan(next_shards, GATHER_B)
    f = _mm("mm_ffn_in", hf, w["w_ffn_in"], "nn", BF16, m=t, n=2 * FFN_HIDDEN, k=d, tm=tb, tn=cs["w_ffn_in"], tk=d,
            b_sh=cs["w_ffn_in"], comm=plan)
    if plan is not None:
        f, outs = f
        got.update(zip(plan["names"], outs))
    act = (lambda i, j, kk, fg, fu: _ffn_act(fg, fu), [(f, lambda i, j, kk: (i, 0)), (f, lambda i, j, kk: (i, 1))])
    x2, act = _mm("mm_ffn_out", act, w["w_ffn_out"], "nn", F32, m=t, n=d, k=FFN_HIDDEN, tm=256, tn=d, tk=FFN_HIDDEN,
                  res=x1, a_keep=True)
    sv.update(hf=hf, f=f, act=act, x2=x2)
    x3, gp, hp, e = _ple_block(x2, p_l, _row(s["ple_norm_g"]), w["w_ple_in"], w["w_ple_gate"])
    sv.update(e=e, hp=hp, gp=gp, p=p_l)
    return x3, sv, got


def _layer_bwd(dx3, sv, w, s, rope, nbatch, seq):
    t = dx3.shape[0]
    tm = 512
    d = D_MODEL
    gb, gs = {}, {}
    cs = {nm: w[nm].shape[2] for nm in SHARD_MAJOR}
    tb = 1024

    def wg(name, a, b, m, n, tm=1024, tk=1024, shard=None):
        return _mm(name, a, b, "tn", BF16, m=m, n=n, k=t, tm=tm, tn=n if shard is None else cs[shard], tk=tk,
                   o_sh=None if shard is None else cs[shard])

    de, dgp, dx2, gs["ple_norm_g"] = _ple_block_bwd(dx3, sv["gp"], sv["e"], sv["x2"], _row(s["ple_norm_g"]),
                                                    w["w_ple_gate"])
    gb["w_ple_in"] = wg("wg_ple_in", sv["p"], de, sv["p"].shape[1], d, tk=2048)
    gb["w_ple_gate"] = wg("wg_ple_gate", sv["hp"], dgp, d, d, tk=2048)
    fw = FFN_HIDDEN // 2
    dact = _mm("mmb_ffn_out", dx2, w["w_ffn_out"], "nt", BF16, m=t, n=FFN_HIDDEN, k=d, tm=tb, tn=fw, tk=d)
    gb["w_ffn_out"] = wg("wg_ffn_out", sv["act"], dx2, FFN_HIDDEN, d, tm=fw)
    f = sv["f"]

    def df_tile(is_gate, da, fg, fu):
        dfg, dfu = _ffn_act_bwd(da, fg, fu)
        return jnp.where(is_gate, dfg, dfu)

    assert cs["w_ffn_in"] == fw
    df_rows = (lambda i, j, kk, *v: df_tile(kk < 2, *v),
               [(dact, lambda i, j, kk: (i, kk % 2)), (f, lambda i, j, kk: (i, kk % 2)), (f, lambda i, j, kk: (i, 2 + kk % 2))])
    dhf, df = _mm("mmb_ffn_in", df_rows, w["w_ffn_in"], "nt", BF16, m=t, n=d, k=2 * FFN_HIDDEN, tm=512, tn=d, tk=fw,
                  b_sh=fw, a_keep=True)
    gb["w_ffn_in"] = wg("wg_ffn_in", sv["hf"], df, d, 2 * FFN_HIDDEN, tk=2048, shard="w_ffn_in")
    dx1, gs["ffn_norm_g"] = _rowwise("rms_ffn_bwd", _rms_bwd, [R(dhf), R(sv["x1"]), R(dx2), V(_row(s["ffn_norm_g"]))],
                                     [O(d, F32)], [A(d)], tm=tm)
    dm = _mm("mmb_mix", dx1, w["w_mix_out"], "nt", BF16, m=t, n=d, k=d, tm=tb, tn=d, tk=d)
    gb["w_mix_out"] = wg("wg_mix", sv["merged"], dx1, d, d)
    z, glu, bg = sv["z"], sv["glu"], _row(s["b_gate"])
    ins = [R(dm, 512, 0), R(z, 512, 3), R(z, 512, 5), R(z, 512, 7), V(bg, 512, 0), V(bg, 512, 2), V(bg, 512, 4),
           R(sv["ya"], 512, 0), R(glu, 512, 0), R(glu, 512, 2), R(sv["yc"], 512, 0)]
    ob = lambda: O(512, BF16, total=d)
    ab = lambda: A(512, total=d)
    dya, dga, dgb, dyc, d0, d1, d2, db0, db1, db2, dba, dbb = _rowwise(
        "merge_bwd", _merge_bwd, ins, [ob() for _ in range(7)], [ab() for _ in range(5)], tm=tm, ncol=2)
    gs["b_gate"] = jnp.concatenate([db0, db1, db2], axis=1)
    gs["b_ssm_glu"] = jnp.concatenate([dba, dbb], axis=1)
    dglu = jnp.concatenate([dga, dgb], axis=1)
    gb["w_attn_out"] = wg("wg_attn_out", sv["o"], dya, Q_WIDTH, d, tk=2048)
    do = _mm("mmb_attn_out", dya, w["w_attn_out"], "nt", BF16, m=t, n=Q_WIDTH, k=d, tm=tb, tn=Q_WIDTH, tk=d)
    dqkv, dsink = _attn_bwd(sv["qkv"], sv["oh"], _heads(do, N_Q_HEADS), sv["lse"], sv["sinks"], nbatch, seq)
    dqkv = _tokens(dqkv)
    gs["attn_sinks"] = dsink.reshape(-1)
    c, sa, sb = rope
    dq = _rowwise("rope_bwd_q", _rope_bwd_q, [R(dqkv, Q_WIDTH, 0), R(c), R(sa), R(sb)], [O(Q_WIDTH, BF16)], tm=tm)
    dk, dv = _kv_combine(dqkv, c, sa, sb, seq)
    gb["w_ssm_glu"] = wg("wg_ssm_glu", sv["gel"], dglu, SSM_WIDTH, 2 * d, tk=2048)
    dgi = _mm("mmb_glu", dglu, w["w_ssm_glu"], "nt", BF16, m=t, n=SSM_WIDTH, k=2 * d, tm=tb, tn=SSM_WIDTH, tk=2 * d)
    a_re, a_im = sv["a"]
    du, dwb, dwc, dd, da = _ssm_bwd(dgi, sv["ys"], sv["st"], z, sv["wb"].T.astype(BF16), sv["wc"].T.astype(BF16),
                                    _ssm_tables(a_re, a_im, True), sv["dskip"], nbatch, seq)
    gs["ssm_d"] = dd.reshape(-1)
    da = jnp.sum(da, axis=0)
    _, prep_vjp = jax.vjp(_ssm_prep, *[s[nm] for nm in SSM_NAMES])
    for nm, g in zip(SSM_NAMES, prep_vjp((da[:SSM_LANES], da[SSM_LANES:], dwb, dwc))):
        gs[nm] = g
    gb["w_conv_out"] = wg("wg_conv_out", sv["sc"], dyc, CONV_WIDTH, d, tk=2048)
    dsc = _mm("mmb_conv_out", dyc, w["w_conv_out"], "nt", BF16, m=t, n=CONV_WIDTH, k=d, tm=tb, tn=CONV_WIDTH, tk=d)
    dca, dcg, dcw, dcb, dlg, dlb = _conv_bwd(dsc, sv["cv"], z, sv["cw"], _row(s["conv_norm_g"]),
                                             _row(s["conv_norm_b"]), nbatch, seq)
    gs["conv_dw_w"] = dcw[:CONV_K]
    gs["conv_dw_b"], gs["conv_norm_g"], gs["conv_norm_b"] = dcb.reshape(-1), dlg.reshape(-1), dlb.reshape(-1)
    dz = jnp.concatenate([dq, dk, dv, du, dca, dcg, d0, d1, d2], axis=1)
    gb["w_in"] = wg("wg_in", sv["h"], dz, d, dz.shape[1], tk=2048, shard="w_in")
    dx, gs["mix_norm_g"] = _in_proj_bwd(dz, w["w_in"], sv["x"], dx1, _row(s["mix_norm_g"]))
    gs["mix_norm_g"], gs["ffn_norm_g"], gs["ple_norm_g"] = (gs[nm].reshape(-1) for nm in
                                                            ("mix_norm_g", "ffn_norm_g", "ple_norm_g"))
    gs["b_gate"], gs["b_ssm_glu"] = gs["b_gate"].reshape(-1), gs["b_ssm_glu"].reshape(-1)
    return dx, {nm: _shard_major(nm, g) for nm, g in gb.items()}, gs


def _rope_tables(positions):
    inv_freq = ROPE_THETA ** (-jnp.arange(0, ROPE_DIM, 2, dtype=F32) / ROPE_DIM)
    ang = positions.reshape(-1).astype(F32)[:, None] * inv_freq
    cos, sin = jnp.cos(ang), jnp.sin(ang)
    t = ang.shape[0]
    rest = HEAD_DIM - ROPE_DIM
    c = jnp.concatenate([cos, cos, jnp.ones((t, rest), F32)], axis=1)
    sa = jnp.concatenate([-sin, jnp.zeros((t, HEAD_DIM - ROPE_HALF), F32)], axis=1)
    sb = jnp.concatenate([jnp.zeros((t, ROPE_HALF), F32), sin, jnp.zeros((t, rest), F32)], axis=1)
    two = lambda v: jnp.concatenate([v, v], axis=1)
    return two(c), two(sa), two(sb)


def _natural(nm, w4):
    if nm in SHARD_MAJOR:
        return w4
    if BIG_AXIS[nm] == 1:
        return w4.reshape(-1, w4.shape[2])
    return w4.transpose(1, 0, 2).reshape(w4.shape[1], -1)


def _shard_major(nm, g):
    if nm in SHARD_MAJOR:
        return g
    if BIG_AXIS[nm] == 1:
        return g.reshape(N_CHIPS, -1, g.shape[1])
    return g.reshape(g.shape[0], N_CHIPS, -1).transpose(1, 0, 2)


def _untap(taps4, cols):
    flat = taps4.reshape(N_CHIPS, -1)[:, :CONV_K * cols]
    return flat.reshape(N_CHIPS, CONV_K, cols).transpose(1, 0, 2).reshape(CONV_K, N_CHIPS * cols)


def _local_step(x, p, positions, loss_target, small, wfull=None, shards=None):
    nbatch, seq, d = x.shape
    depth = p.shape[0]
    t = nbatch * seq
    rope = _rope_tables(positions)
    xs = x.reshape(t, d)
    saved, ws, ss = [], [], []
    got = None if shards is None else _gather_now((shards, 0))
    for l in range(depth):
        w4 = {nm: wfull[nm][l] for nm in BIG} if shards is None else got
        w_l = {nm: _natural(nm, w4[nm]) for nm in BIG}
        s_l = {nm: small[nm][l] for nm in small if nm != "final_norm_g"}
        if shards is not None:
            s_l["conv_dw_w"] = _untap(got[TAPS], CONV_WIDTH // N_CHIPS)
        nxt = (shards, l + 1) if shards is not None and l + 1 < depth else None
        xs, sv, got = _layer_fwd(xs, p[l].reshape(t, -1), w_l, s_l, rope, nbatch, seq, nxt)
        saved.append(sv)
        ws.append(w_l)
        ss.append(s_l)
    dx, loss_cols, dgf = _rowwise("loss_head", _loss_fn, [R(xs), R(loss_target.reshape(t, d)),
                                                          V(_row(small["final_norm_g"]))],
                                  [O(d, F32)], [A(d), A(d)], tm=512)
    gbs, gss = [None] * depth, [None] * depth
    for l in reversed(range(depth)):
        dx, gbs[l], gss[l] = _layer_bwd(dx, saved[l], ws[l], ss[l], rope, nbatch, seq)
    gbig = {nm: jnp.stack([g[nm] for g in gbs]) for nm in BIG}
    gsmall = {nm: jnp.stack([g[nm] for g in gss]) for nm in SMALL if nm != "final_norm_g"}
    gsmall["final_norm_g"] = dgf.reshape(-1)
    return loss_cols, dx.reshape(nbatch, seq, d), gbig, gsmall


HBM = pl.BlockSpec(memory_space=pltpu.HBM)


def _place():
    x, y, c = lax.axis_index("x"), lax.axis_index("y"), lax.axis_index("c")
    chips = [(1 - x, y), (x, 1 - y), (1 - x, 1 - y)]
    return x, y, c, chips


def _remote(src, dst, send_sem, recv_sem, to):
    return pltpu.make_async_remote_copy(src_ref=src, dst_ref=dst, send_sem=send_sem, recv_sem=recv_sem,
                                        device_id=to, device_id_type=MESH)


TAPS = "taps"
GATHER_ALL = (("w_ffn_in", "w_ffn_out"),
              ("w_in", "w_ple_gate", "w_mix_out", "w_attn_out", "w_ssm_glu", "w_conv_out", "w_ple_in", TAPS))
GATHER_A = (("w_ffn_in",), ("w_in", "w_ple_gate"))
GATHER_B = (("w_ffn_out",), ("w_mix_out", "w_attn_out", "w_ssm_glu", "w_conv_out", "w_ple_in", TAPS))


def _gather_plan(shards, sets):
    stacked, layer = shards
    names = sets[0] + sets[1]
    n = len(names)
    idx = {nm: i for i, nm in enumerate(names)}

    def start(ins, outs, sems):
        send1, recv1, _, _, send0, recv0 = sems
        x, y, c, chips = _place()
        me = 2 * x + y
        for i in range(n):
            _remote(ins[i].at[layer], outs[i].at[me], send0.at[i], recv0.at[i], (x, y, 1 - c)).start()
        for role in (0, 1):
            @pl.when(c == role)
            def _():
                for nm in sets[role]:
                    i = idx[nm]
                    for k, (cx, cy) in enumerate(chips):
                        _remote(ins[i].at[layer], outs[i].at[me], send1.at[i, k], recv1.at[i, k], (cx, cy, c)).start()

    def finish(ins, outs, sems):
        send1, recv1, send2, recv2, send0, recv0 = sems
        x, y, c, chips = _place()
        me = 2 * x + y
        sib = (x, y, 1 - c)
        for role in (0, 1):
            @pl.when(c == role)
            def _():
                passed = []
                for nm in sets[role]:
                    i = idx[nm]
                    for k, (cx, cy) in enumerate(chips):
                        slot = outs[i].at[2 * cx + cy]
                        _remote(slot, slot, send1.at[i, k], recv1.at[i, k], (cx, cy, c)).wait_recv()
                        cp = _remote(slot, slot, send2.at[i, k], recv2.at[i, k], sib)
                        cp.start()
                        passed.append(cp)
                for nm in sets[1 - role]:
                    i = idx[nm]
                    for k, (cx, cy) in enumerate(chips):
                        slot = outs[i].at[2 * cx + cy]
                        _remote(slot, slot, send2.at[i, k], recv2.at[i, k], sib).wait_recv()
                for nm in sets[role]:
                    i = idx[nm]
                    for k, (cx, cy) in enumerate(chips):
                        _remote(ins[i].at[layer], outs[i].at[me], send1.at[i, k], recv1.at[i, k],
                                (cx, cy, c)).wait_send()
                for cp in passed:
                    cp.wait_send()
        for i in range(n):
            _remote(ins[i].at[layer], outs[i].at[me], send0.at[i], recv0.at[i], sib).wait()

    ins = [stacked[nm] for nm in names]
    return dict(names=names, ins=ins, start=start, finish=finish,
                out_shapes=[jax.ShapeDtypeStruct((N_CHIPS,) + v.shape[1:], v.dtype) for v in ins],
                sems=[pltpu.SemaphoreType.DMA((n, 3)) for _ in range(4)] + [pltpu.SemaphoreType.DMA((n,))
                                                                            for _ in range(2)])


def _gather_now(shards):
    return _comm_now("gather_weights", _gather_plan(shards, GATHER_ALL))


def _pair_exchange(grads):
    n = len(grads)
    hl = grads[0].shape[0] // 2

    def body(*refs):
        ins, outs = refs[:n], refs[n:2 * n]
        send, recv = refs[2 * n:]
        x, y, c, _ = _place()
        other = pl.ds((1 - c) * hl, hl)
        cps = [_remote(ins[i].at[other], outs[i], send.at[i], recv.at[i], (x, y, 1 - c)) for i in range(n)]
        for cp in cps:
            cp.start()
        for cp in cps:
            cp.wait()

    out_shape = [jax.ShapeDtypeStruct((hl,) + g.shape[1:], g.dtype) for g in grads]
    sems = [pltpu.SemaphoreType.DMA((n,)) for _ in range(2)]
    return pl.pallas_call(body, out_shape=out_shape, in_specs=[HBM] * n, out_specs=[HBM] * n, scratch_shapes=sems,
                          name="reduce_pair_exchange")(*grads)


def _pair_add(g, r):
    hl, _, rr, cc = r.shape
    rows = hl * N_CHIPS * rr
    nblk = rows // rr

    def body(c_ref, g_ref, r_ref, o_ref):
        o_ref[...] = (g_ref[...].astype(F32) + r_ref[...].astype(F32)).astype(o_ref.dtype)

    grid_spec = pltpu.PrefetchScalarGridSpec(
        num_scalar_prefetch=1, grid=(nblk,),
        in_specs=[pl.BlockSpec((rr, cc), lambda i, c_ref: (c_ref[0] * nblk + i, 0)),
                  pl.BlockSpec((rr, cc), lambda i, c_ref: (i, 0))],
        out_specs=pl.BlockSpec((rr, cc), lambda i, c_ref: (i, 0)))
    c = lax.axis_index("c").astype(jnp.int32).reshape(1)
    out = pl.pallas_call(body, out_shape=jax.ShapeDtypeStruct((rows, cc), r.dtype), grid_spec=grid_spec,
                         name="reduce_pair_add", compiler_params=_params(("parallel",)))(
        c, g.reshape(-1, cc), r.reshape(rows, cc))
    return out.reshape(r.shape)


def _chip_exchange(psums):
    n = len(psums)

    def body(*refs):
        ins, got = refs[:n], refs[n:2 * n]
        send, recv = refs[2 * n:]
        x, y, c, chips = _place()
        cps = [_remote(ins[i].at[:, 2 * cx + cy], got[i].at[k], send.at[i, k], recv.at[i, k], (cx, cy, c))
               for i in range(n) for k, (cx, cy) in enumerate(chips)]
        for cp in cps:
            cp.start()
        for cp in cps:
            cp.wait()

    got_shape = [jax.ShapeDtypeStruct((3, p.shape[0]) + p.shape[2:], p.dtype) for p in psums]
    sems = [pltpu.SemaphoreType.DMA((n, 3)), pltpu.SemaphoreType.DMA((n, 3))]
    return pl.pallas_call(body, out_shape=got_shape, in_specs=[HBM] * n, out_specs=[HBM] * n, scratch_shapes=sems,
                          name="reduce_chip_exchange")(*psums)


def _comm_now(name, plan):
    n = len(plan["ins"])

    def body(*refs):
        ins, outs, sems = refs[:n], refs[n:2 * n], refs[2 * n:]
        plan["start"](ins, outs, sems)
        plan["finish"](ins, outs, sems)

    outs = pl.pallas_call(body, out_shape=plan["out_shapes"], in_specs=[HBM] * n, out_specs=[HBM] * n,
                          scratch_shapes=plan["sems"], name=name)(*plan["ins"])
    return dict(zip(plan["names"], outs))


def _sum4(psum, got):
    hl, _, rr, cc = psum.shape
    tr = rr if rr * cc <= 512 * 1024 else rr // 2

    def body(place_ref, own_ref, g0_ref, g1_ref, g2_ref, o_ref):
        tot = (own_ref[...].astype(F32) + g0_ref[...].astype(F32)) + g1_ref[...].astype(F32)
        o_ref[...] = tot + g2_ref[...].astype(F32)

    def got_spec(k):
        return pl.BlockSpec((None, None, tr, cc), lambda h, i, place: (k, h, i, 0))

    grid_spec = pltpu.PrefetchScalarGridSpec(
        num_scalar_prefetch=1, grid=(hl, rr // tr),
        in_specs=[pl.BlockSpec((None, None, tr, cc), lambda h, i, place: (h, place[0], i, 0)),
                  got_spec(0), got_spec(1), got_spec(2)],
        out_specs=pl.BlockSpec((None, tr, cc), lambda h, i, place: (place[1] * hl + h, i, 0)))
    place = jnp.stack([2 * lax.axis_index("x") + lax.axis_index("y"), lax.axis_index("c")]).astype(jnp.int32)
    return pl.pallas_call(body, out_shape=jax.ShapeDtypeStruct((2 * hl, rr, cc), F32), grid_spec=grid_spec,
                          name="reduce_sum4", compiler_params=_params(("parallel", "parallel")))(
        place, psum, got, got, got)


def _pair_gather(sums):
    n = len(sums)
    hl = sums[0].shape[0] // 2

    def body(*refs):
        bufs = refs[n:2 * n]
        send, recv = refs[2 * n:]
        x, y, c, _ = _place()
        mine = pl.ds(c * hl, hl)
        cps = [_remote(bufs[i].at[mine], bufs[i].at[mine], send.at[i], recv.at[i], (x, y, 1 - c)) for i in range(n)]
        for cp in cps:
            cp.start()
        for cp in cps:
            cp.wait()

    out_shape = [jax.ShapeDtypeStruct(v.shape, v.dtype) for v in sums]
    sems = [pltpu.SemaphoreType.DMA((n,)) for _ in range(2)]
    return pl.pallas_call(body, out_shape=out_shape, in_specs=[HBM] * n, out_specs=[HBM] * n, scratch_shapes=sems,
                          input_output_aliases={i: i for i in range(n)}, name="reduce_pair_gather")(*sums)


def _allreduce_small(vec):
    rows = vec.shape[0]

    def body(v_ref, o_ref, all_ref, send, recv):
        x, y, c, _ = _place()
        me = 4 * x + 2 * y + c
        all_ref[me] = v_ref[...]
        cps = []
        for dlt in range(1, N_DEV):
            fx, fy, fc = (dlt >> 2) & 1, (dlt >> 1) & 1, dlt & 1
            to = (1 - x if fx else x, 1 - y if fy else y, 1 - c if fc else c)
            cps.append(_remote(v_ref, all_ref.at[me], send.at[dlt - 1], recv.at[dlt - 1], to))
        for cp in cps:
            cp.start()
        for cp in cps:
            cp.wait()
        tot = all_ref[0]
        for dev in range(1, N_DEV):
            tot = tot + all_ref[dev]
        o_ref[...] = tot

    vm = pl.BlockSpec(memory_space=pltpu.VMEM)
    return pl.pallas_call(
        body, out_shape=jax.ShapeDtypeStruct(vec.shape, F32), in_specs=[vm], out_specs=vm,
        scratch_shapes=[pltpu.VMEM((N_DEV, rows, 128), F32), pltpu.SemaphoreType.DMA((N_DEV - 1,)),
                        pltpu.SemaphoreType.DMA((N_DEV - 1,))],
        name="allreduce_small", compiler_params=pltpu.CompilerParams(vmem_limit_bytes=VMEM_LIMIT))(vec)


def _adamw(name, w, g, m, v):
    rows, cc = w.shape
    tm = rows if rows * cc <= 512 * 1024 else math.gcd(rows, 256)
    return _rowwise(name, _adamw_fn, [R(w), R(g), R(m), R(v)], [O(cc, F32), O(cc, F32), O(cc, F32)], tm=tm)


def _pack(parts):
    flat = jnp.concatenate([v.reshape(-1).astype(F32) for v in parts])
    pad = (-flat.shape[0]) % (SUBLANES * 128)
    return jnp.pad(flat, (0, pad)).reshape(-1, 128)


def _unpack(packed, shapes):
    flat, out, pos = packed.reshape(-1), [], 0
    for shp in shapes:
        size = math.prod(shp)
        out.append(flat[pos:pos + size].reshape(shp))
        pos += size
    return out


def kernel(x, p, positions, mix_norm_g, w_in, b_gate, attn_sinks, w_attn_out, ssm_lambda_re, ssm_lambda_im, ssm_log_dt, ssm_b_re, ssm_b_im, ssm_c_re, ssm_c_im, ssm_d, w_ssm_glu, b_ssm_glu, conv_dw_w, conv_dw_b, conv_norm_g, conv_norm_b, w_conv_out, w_mix_out, ffn_norm_g, w_ffn_in, w_ffn_out, w_ple_in, ple_norm_g, w_ple_gate, final_norm_g, loss_target, m_mix_norm_g, m_w_in, m_b_gate, m_attn_sinks, m_w_attn_out, m_ssm_lambda_re, m_ssm_lambda_im, m_ssm_log_dt, m_ssm_b_re, m_ssm_b_im, m_ssm_c_re, m_ssm_c_im, m_ssm_d, m_w_ssm_glu, m_b_ssm_glu, m_conv_dw_w, m_conv_dw_b, m_conv_norm_g, m_conv_norm_b, m_w_conv_out, m_w_mix_out, m_ffn_norm_g, m_w_ffn_in, m_w_ffn_out, m_w_ple_in, m_ple_norm_g, m_w_ple_gate, m_final_norm_g, v_mix_norm_g, v_w_in, v_b_gate, v_attn_sinks, v_w_attn_out, v_ssm_lambda_re, v_ssm_lambda_im, v_ssm_log_dt, v_ssm_b_re, v_ssm_b_im, v_ssm_c_re, v_ssm_c_im, v_ssm_d, v_w_ssm_glu, v_b_ssm_glu, v_conv_dw_w, v_conv_dw_b, v_conv_norm_g, v_conv_norm_b, v_w_conv_out, v_w_mix_out, v_ffn_norm_g, v_w_ffn_in, v_w_ffn_out, v_w_ple_in, v_ple_norm_g, v_w_ple_gate, v_final_norm_g):
    given = dict(locals())
    wts = {nm: given[nm] for nm in WEIGHTS}
    mom = {nm: given["m_" + nm] for nm in WEIGHTS}
    var = {nm: given["v_" + nm] for nm in WEIGHTS}
    depth = p.shape[0]
    chip = 2 * lax.axis_index("x") + lax.axis_index("y")

    cw_cols = conv_dw_w.shape[2]
    taps = jnp.pad(conv_dw_w.reshape(depth, -1), ((0, 0), (0, (-CONV_K * cw_cols) % (SUBLANES * 128))))
    shards = {**{nm: wts[nm].astype(BF16) for nm in BIG}, TAPS: taps.reshape(depth, -1, 128)}
    small = {nm: wts[nm] for nm in SMALL if nm != "conv_dw_w"}

    loss_cols, grad_x, gbig, gsmall = _local_step(x, p, positions, loss_target, small, shards=shards)

    parts = [loss_cols] + [gsmall[nm] for nm in SMALL]
    total = _allreduce_small(_pack(parts))
    summed = _unpack(total, [v.shape for v in parts])
    loss = jnp.sum(summed[0])
    gsum = dict(zip(SMALL, summed[1:]))
    gsum["conv_dw_w"] = lax.dynamic_slice_in_dim(gsum["conv_dw_w"], chip * cw_cols, cw_cols, axis=2)
    shapes = [wts[nm].shape for nm in SMALL]
    deltas, new_m, new_v = _adamw("adamw_small", _pack([wts[nm] for nm in SMALL]), _pack([gsum[nm] for nm in SMALL]),
                                  _pack([mom[nm] for nm in SMALL]), _pack([var[nm] for nm in SMALL]))
    grads = dict(gsum)
    delta = dict(zip(SMALL, _unpack(deltas, shapes)))
    newm = dict(zip(SMALL, _unpack(new_m, shapes)))
    newv = dict(zip(SMALL, _unpack(new_v, shapes)))

    gl = [gbig[nm] for nm in BIG]
    sib = _pair_exchange(gl)
    psums = [_pair_add(g, r) for g, r in zip(gl, sib)]
    got = _chip_exchange(psums)
    sums = _pair_gather([_sum4(ps, g) for ps, g in zip(psums, got)])
    for nm, g in zip(BIG, sums):
        shp = wts[nm].shape
        two = lambda v: v.reshape(-1, shp[-1])
        g = g.reshape(shp)
        d_w, n_m, n_v = _adamw("adamw_" + nm, two(wts[nm]), two(g), two(mom[nm]), two(var[nm]))
        grads[nm], delta[nm], newm[nm], newv[nm] = g, d_w.reshape(shp), n_m.reshape(shp), n_v.reshape(shp)

    return (loss, grad_x, *[grads[nm] for nm in WEIGHTS], *[delta[nm] for nm in WEIGHTS],
            *[newm[nm] for nm in WEIGHTS], *[newv[nm] for nm in WEIGHTS])
```
